```python
import math
import jax, jax.numpy as jnp
from jax import lax
import numpy as np

D_MODEL = 2048
BATCH = 1
SEQ = 16384
DEPTH = 1

HEAD_DIM = 64
NSA_HEADS = 8
NSA_KV_HEADS = 2
NSA_HPG = NSA_HEADS // NSA_KV_HEADS
FOX_HEADS = 8
MIX_W = NSA_HEADS * HEAD_DIM
CMP_LEN = 32
CMP_STRIDE = 16
CMP_HID = 256
SLC_LEN = 64
SLC_TOPK = 16
WINDOW = 512
Q_BLOCK = 128
T5_BUCKETS = 32
T5_MAX_EXACT = 16
T5_MAX_DIST = 128
N_GROUPS = 8
EXPERTS_PER_GROUP = 8
N_EXPERTS = N_GROUPS * EXPERTS_PER_GROUP
TOP_K_EXPERTS = 2
D_EXPERT = 512
ROW_BLOCK = 128
LN_EPS = 1e-5
NEG_INF = -1e30
FORCE_SCORE = 1e4
DEEPNORM_ALPHA = (2 * DEPTH) ** 0.25
DEEPNORM_BETA = (8 * DEPTH) ** -0.25

C_NSA_Q = NSA_HEADS * HEAD_DIM
C_NSA_KV = 6 * NSA_KV_HEADS * HEAD_DIM
C_NSA_GATE = 3 * NSA_HEADS
C_FOX_QKV = 3 * FOX_HEADS * HEAD_DIM
C_FOX_F = FOX_HEADS
C_MERGE = 2 * D_MODEL
OFF_KV = C_NSA_Q
OFF_GATE = OFF_KV + C_NSA_KV
OFF_FOX = OFF_GATE + C_NSA_GATE
OFF_FGT = OFF_FOX + C_FOX_QKV
OFF_MERGE = OFF_FGT + C_FOX_F
IN_COLS = OFF_MERGE + C_MERGE

kernel_name = "hybrid_nsa_fox_hmoe_block"


def layer_norm(x, g=None, b=None):
    xf = x.astype(jnp.float32)
    mu = xf.mean(-1, keepdims=True)
    var = jnp.square(xf - mu).mean(-1, keepdims=True)
    y = (xf - mu) * lax.rsqrt(var + LN_EPS)
    if g is not None:
        y = y * g.astype(jnp.float32) + b.astype(jnp.float32)
    return y.astype(x.dtype)


def masked_softmax(s, mask):
    p = jax.nn.softmax(jnp.where(mask, s, NEG_INF), axis=-1)
    return p * mask.any(-1, keepdims=True)


def t5_bucket(dist):
    n = jnp.maximum(dist, 0)
    ratio = jnp.log(jnp.maximum(n, T5_MAX_EXACT).astype(jnp.float32) / T5_MAX_EXACT)
    big = T5_MAX_EXACT + (ratio / math.log(T5_MAX_DIST / T5_MAX_EXACT)
                          * (T5_BUCKETS - T5_MAX_EXACT)).astype(jnp.int32)
    return jnp.where(n < T5_MAX_EXACT, n, jnp.minimum(big, T5_BUCKETS - 1))


def split_heads(a, n_heads):
    B, S, _ = a.shape
    return a.reshape(B, S, n_heads, HEAD_DIM).transpose(0, 2, 1, 3)


def compress_kv(kv, pe, w1, b1, w2):
    S = kv.shape[3]
    n_cmp = (S - CMP_LEN) // CMP_STRIDE + 1
    idx = jnp.arange(n_cmp)[:, None] * CMP_STRIDE + jnp.arange(CMP_LEN)[None, :]
    blocks = kv[:, :, :, idx, :] + pe[:, None, None, None]
    flat = blocks.reshape(*blocks.shape[:4], CMP_LEN * HEAD_DIM)
    hid = jax.nn.gelu(jnp.einsum('zbgnf,zfh->zbgnh', flat, w1) + b1[:, None, None, None])
    return jnp.einsum('zbgnh,zhd->zbgnd', hid, w2)


def gather_blocks(blocks, sel):
    return jax.vmap(jax.vmap(lambda bl, ix: bl[ix]))(blocks, sel)


def nsa_attention(q, k_cmp, v_cmp, k_slc, v_slc, k_win, v_win, gates, t5_table):
    B, G, HPG, S, HD = q.shape
    n_cmp = k_cmp.shape[2]
    n_slc = S // SLC_LEN
    n_sel = min(SLC_TOPK, n_slc)
    ratio = SLC_LEN // CMP_STRIDE
    scale = HD ** -0.5
    tb = t5_table.T.reshape(G, HPG, T5_BUCKETS).astype(jnp.float32)
    g_ix = jnp.arange(G)[None, :, None, None, None]
    h_ix = jnp.arange(HPG)[None, None, :, None, None]
    cmp_end = jnp.arange(n_cmp) * CMP_STRIDE + CMP_LEN - 1
    w_main = jnp.array([0.5] + [1.0] * (ratio - 1), jnp.float32)
    blk = jnp.arange(n_slc)
    win_off = jnp.arange(WINDOW + Q_BLOCK)

    def block(qb):
        q0 = qb * Q_BLOCK
        t = q0 + jnp.arange(Q_BLOCK)
        qblk = lax.dynamic_slice_in_dim(q, q0, Q_BLOCK, axis=3)
        gblk = lax.dynamic_slice_in_dim(gates, q0, Q_BLOCK, axis=3)
        s_c = (jnp.einsum('bghtd,bgnd->bghtn', qblk, k_cmp).astype(jnp.float32) * scale
               + tb[:, :, t5_bucket(t[:, None] - cmp_end[None, :])])
        p_c = masked_softmax(s_c, cmp_end[None, :] <= t[:, None])
        o_c = jnp.einsum('bghtn,bgnd->bghtd', p_c.astype(v_cmp.dtype), v_cmp)
        imp = jnp.pad(p_c.sum(2), ((0, 0), (0, 0), (0, 0), (1, ratio * n_slc - n_cmp)))
        p_slc = (imp[..., :ratio * n_slc].reshape(B, G, Q_BLOCK, n_slc, ratio) @ w_main
                 + 0.5 * imp[..., ratio::ratio])
        cur = (t // SLC_LEN)[:, None]
        forced = (blk == 0) | (blk == cur) | (blk == cur - 1)
        score = jnp.where(forced, FORCE_SCORE, jnp.where(blk <= cur, p_slc, -1.0))
        _, sel = lax.top_k(score, n_sel)
        k_sel = gather_blocks(k_slc, sel).reshape(B, G, Q_BLOCK, n_sel * SLC_LEN, HD)
        v_sel = gather_blocks(v_slc, sel).reshape(B, G, Q_BLOCK, n_sel * SLC_LEN, HD)
        kpos = (sel[..., None] * SLC_LEN + jnp.arange(SLC_LEN)).reshape(B, G, Q_BLOCK, n_sel * SLC_LEN)
        bucket_s = t5_bucket(t[:, None] - kpos)[:, :, None]
        s_s = (jnp.einsum('bghtd,bgtkd->bghtk', qblk, k_sel).astype(jnp.float32) * scale
               + tb[g_ix, h_ix, bucket_s])
        p_s = masked_softmax(s_s, (kpos <= t[:, None])[:, :, None])
        o_s = jnp.einsum('bghtk,bgtkd->bghtd', p_s.astype(v_sel.dtype), v_sel)
        kw = lax.dynamic_slice_in_dim(k_win, q0, WINDOW + Q_BLOCK, axis=2)
        vw = lax.dynamic_slice_in_dim(v_win, q0, WINDOW + Q_BLOCK, axis=2)
        wpos = q0 - WINDOW + win_off
        dist = t[:, None] - wpos[None, :]
        s_w = (jnp.einsum('bghtd,bgkd->bghtk', qblk, kw).astype(jnp.float32) * scale
               + tb[:, :, t5_bucket(dist)])
        p_w = masked_softmax(s_w, (dist >= 0) & (dist < WINDOW) & (wpos[None, :] >= 0))
        o_w = jnp.einsum('bghtk,bgkd->bghtd', p_w.astype(vw.dtype), vw)
        return gblk[..., 0:1] * o_c + gblk[..., 1:2] * o_s + gblk[..., 2:3] * o_w

    out = lax.map(block, jnp.arange(S // Q_BLOCK))
    return out.transpose(1, 0, 4, 2, 3, 5).reshape(B, S, G * HPG * HD)


def forgetting_attention(q, k, v, log_f):
    B, H, S, HD = q.shape
    scale = HD ** -0.5
    F = lax.cumsum(log_f, axis=2)
    kpos = jnp.arange(S)

    def block(qb):
        q0 = qb * Q_BLOCK
        t = q0 + jnp.arange(Q_BLOCK)
        qblk = lax.dynamic_slice_in_dim(q, q0, Q_BLOCK, axis=2)
        Fq = lax.dynamic_slice_in_dim(F, q0, Q_BLOCK, axis=2)
        s = (jnp.einsum('bhtd,bhsd->bhts', qblk, k).astype(jnp.float32) * scale
             + (Fq[..., None] - F[:, :, None, :]))
        p = masked_softmax(s, kpos[None, :] <= t[:, None])
        return jnp.einsum('bhts,bhsd->bhtd', p.astype(v.dtype), v)

    out = lax.map(block, jnp.arange(S // Q_BLOCK))
    return out.transpose(1, 0, 3, 2, 4).reshape(B, S, H * HD)


def token_mixing(u, w_in, b_fgt, t5_table, cmp_pe, cmp_w1, cmp_b1, cmp_w2, w_br_nsa, w_br_fox, w_o):
    B, S, _ = u.shape
    proj = u @ w_in
    nsa_q, nsa_kv, nsa_g, fox_qkv, fox_f, merge = jnp.split(
        proj, [OFF_KV, OFF_GATE, OFF_FOX, OFF_FGT, OFF_MERGE], axis=-1)
    q = split_heads(nsa_q, NSA_HEADS).reshape(B, NSA_KV_HEADS, NSA_HPG, S, HEAD_DIM)
    kv = nsa_kv.reshape(B, S, 6, NSA_KV_HEADS, HEAD_DIM).transpose(2, 0, 3, 1, 4)
    kv_cmp = compress_kv(kv[0:2], cmp_pe, cmp_w1, cmp_b1, cmp_w2)
    kv_slc = kv[2:4].reshape(2, B, NSA_KV_HEADS, S // SLC_LEN, SLC_LEN, HEAD_DIM)
    kv_win = jnp.pad(kv[4:6], ((0, 0), (0, 0), (0, 0), (WINDOW, 0), (0, 0)))
    gates = jax.nn.sigmoid(nsa_g.reshape(B, S, NSA_KV_HEADS, NSA_HPG, 3).transpose(0, 2, 3, 1, 4))
    o_nsa = nsa_attention(q, kv_cmp[0], kv_cmp[1], kv_slc[0], kv_slc[1],
                          kv_win[0], kv_win[1], gates, t5_table)
    fq, fk, fv = jnp.split(fox_qkv, 3, axis=-1)
    log_f = jax.nn.log_sigmoid((fox_f + b_fgt).astype(jnp.float32)).transpose(0, 2, 1)
    o_fox = forgetting_attention(split_heads(fq, FOX_HEADS), split_heads(fk, FOX_HEADS),
                                 split_heads(fv, FOX_HEADS), log_f)
    g_nsa, g_fox = jnp.split(jax.nn.sigmoid(merge), 2, axis=-1)
    merged = g_nsa * (o_nsa @ w_br_nsa) + g_fox * (o_fox @ w_br_fox)
    return merged @ w_o


def hierarchical_moe(h, w_rg, b_rg, w_re, b_re, w_gate, w_up, w_down):
    B, S, D = h.shape
    T = B * S
    x = h.reshape(T, D)
    p_grp = jax.nn.softmax((x @ w_rg + b_rg).astype(jnp.float32), axis=-1)
    grp = jnp.argmax(p_grp, axis=-1)
    p_grp_sel = jnp.max(p_grp, axis=-1)
    e_logits = (jnp.einsum('td,dge->tge', x, w_re) + b_re)[jnp.arange(T), grp]
    p_top, e_top = lax.top_k(jax.nn.softmax(e_logits.astype(jnp.float32), axis=-1), TOP_K_EXPERTS)
    w_tok = p_grp_sel[:, None] * p_top / p_top.sum(-1, keepdims=True)
    eid = (grp[:, None] * EXPERTS_PER_GROUP + e_top).reshape(-1)
    tok = jnp.repeat(jnp.arange(T), TOP_K_EXPERTS)
    wts = w_tok.reshape(-1)
    n_rows = T * TOP_K_EXPERTS + N_EXPERTS * ROW_BLOCK
    n_blocks = n_rows // ROW_BLOCK
    order = jnp.argsort(eid)
    e_sorted = eid[order]
    counts = jnp.bincount(eid, length=N_EXPERTS)
    padded = (counts + ROW_BLOCK - 1) // ROW_BLOCK * ROW_BLOCK
    pad_end = jnp.cumsum(padded)
    pad_start = pad_end - padded
    start = jnp.cumsum(counts) - counts
    dest = pad_start[e_sorted] + jnp.arange(eid.shape[0]) - start[e_sorted]
    row_tok = jnp.zeros((n_rows,), jnp.int32).at[dest].set(tok[order])
    row_w = jnp.zeros((n_rows,), x.dtype).at[dest].set(wts[order].astype(x.dtype))
    blk_exp = jnp.minimum(jnp.searchsorted(pad_end, jnp.arange(n_blocks) * ROW_BLOCK, side='right'),
                          N_EXPERTS - 1)
    xs = x[row_tok].reshape(n_blocks, ROW_BLOCK, D)

    def expert_block(args):
        xb, e = args
        return (jax.nn.silu(xb @ w_gate[e]) * (xb @ w_up[e])) @ w_down[e]

    ys = lax.map(expert_block, (xs, blk_exp)).reshape(n_rows, D)
    y = jax.ops.segment_sum(ys * row_w[:, None], row_tok, num_segments=T)
    return y.reshape(B, S, D)


def setup_inputs(seed: int = 0) -> dict:
    key = jax.random.key(seed)
    ks = jax.random.split(key, 26)
    L = DEPTH

    def nrm(k, shape, s):
        return jax.random.normal(k, shape, jnp.float32) * s

    return {
        "x": nrm(ks[0], (BATCH, SEQ, D_MODEL), 1.0),
        "c": nrm(ks[1], (BATCH, D_MODEL), 1.0),
        "w_ada": nrm(ks[2], (L, D_MODEL, 6 * D_MODEL), 0.1 * D_MODEL ** -0.5),
        "b_ada": nrm(ks[3], (L, 6 * D_MODEL), 0.01),
        "w_in": nrm(ks[4], (L, D_MODEL, IN_COLS), D_MODEL ** -0.5),
        "b_fgt": jnp.linspace(1.0, 5.0, FOX_HEADS, dtype=jnp.float32)[None, :] + nrm(ks[5], (L, FOX_HEADS), 0.1),
        "t5_table": nrm(ks[6], (T5_BUCKETS, NSA_HEADS), 0.5),
        "cmp_pe": nrm(ks[7], (L, 2, CMP_LEN, HEAD_DIM), 0.02),
        "cmp_w1": nrm(ks[8], (L, 2, CMP_LEN * HEAD_DIM, CMP_HID), (CMP_LEN * HEAD_DIM) ** -0.5),
        "cmp_b1": nrm(ks[9], (L, 2, CMP_HID), 0.01),
        "cmp_w2": nrm(ks[10], (L, 2, CMP_HID, HEAD_DIM), CMP_HID ** -0.5),
        "w_br_nsa": nrm(ks[11], (L, MIX_W, D_MODEL), MIX_W ** -0.5),
        "w_br_fox": nrm(ks[12], (L, MIX_W, D_MODEL), MIX_W ** -0.5),
        "w_o": nrm(ks[13], (L, D_MODEL, D_MODEL), DEEPNORM_BETA * D_MODEL ** -0.5),
        "ln1_g": 1.0 + nrm(ks[14], (L, D_MODEL), 0.02),
        "ln1_b": nrm(ks[15], (L, D_MODEL), 0.01),
        "w_rg": nrm(ks[16], (L, D_MODEL, N_GROUPS), D_MODEL ** -0.5),
        "b_rg": nrm(ks[17], (L, N_GROUPS), 0.01),
        "w_re": nrm(ks[18], (L, D_MODEL, N_GROUPS, EXPERTS_PER_GROUP), D_MODEL ** -0.5),
        "b_re": nrm(ks[19], (L, N_GROUPS, EXPERTS_PER_GROUP), 0.01),
        "w_gate": nrm(ks[20], (L, N_EXPERTS, D_MODEL, D_EXPERT), D_MODEL ** -0.5),
        "w_up": nrm(ks[21], (L, N_EXPERTS, D_MODEL, D_EXPERT), D_MODEL ** -0.5),
        "w_down": nrm(ks[22], (L, N_EXPERTS, D_EXPERT, D_MODEL), DEEPNORM_BETA * D_EXPERT ** -0.5),
        "ln2_g": 1.0 + nrm(ks[23], (L, D_MODEL), 0.02),
        "ln2_b": nrm(ks[24], (L, D_MODEL), 0.01),
    }


def reference(x, c, w_ada, b_ada, w_in, b_fgt, t5_table, cmp_pe, cmp_w1, cmp_b1, cmp_w2,
              w_br_nsa, w_br_fox, w_o, ln1_g, ln1_b, w_rg, b_rg, w_re, b_re,
              w_gate, w_up, w_down, ln2_g, ln2_b):
    c_act = jax.nn.silu(c)
    for l in range(DEPTH):
        mod = (c_act @ w_ada[l] + b_ada[l])[:, None, :]
        sh1, sc1, g1, sh2, sc2, g2 = jnp.split(mod, 6, axis=-1)
        u = layer_norm(x) * (1.0 + sc1) + sh1
        y = token_mixing(u, w_in[l], b_fgt[l], t5_table, cmp_pe[l], cmp_w1[l], cmp_b1[l], cmp_w2[l],
                         w_br_nsa[l], w_br_fox[l], w_o[l])
        x = layer_norm(DEEPNORM_ALPHA * x + (1.0 + g1) * y, ln1_g[l], ln1_b[l])
        u = layer_norm(x) * (1.0 + sc2) + sh2
        y = hierarchical_moe(u, w_rg[l], b_rg[l], w_re[l], b_re[l], w_gate[l], w_up[l], w_down[l])
        x = layer_norm(DEEPNORM_ALPHA * x + (1.0 + g2) * y, ln2_g[l], ln2_b[l])
    return x
```

```python
import functools
import math

import numpy as np
import jax
import jax.numpy as jnp
from jax import lax
from jax.experimental import pallas as pl
from jax.experimental.pallas import tpu as pltpu

F32 = jnp.float32
BF16 = jnp.bfloat16
HIGHEST = lax.Precision.HIGHEST

HEAD_DIM = 64
NSA_HEADS = 8
NSA_GROUPS = 2
NSA_HPG = NSA_HEADS // NSA_GROUPS
FOX_HEADS = 8
MIX_W = NSA_HEADS * HEAD_DIM
CMP_LEN = 32
CMP_STRIDE = 16
SLC_LEN = 64
SLC_TOPK = 16
WINDOW = 512
T5_BUCKETS = 32
T5_MAX_EXACT = 16
T5_MAX_DIST = 128
T5_FAR_DIST = 113
N_GROUPS = 8
EXPERTS_PER_GROUP = 8
N_EXPERTS = N_GROUPS * EXPERTS_PER_GROUP
ROW_BLOCK = 128
LN_EPS = 1e-5
NEG_INF = -1e30
M_INIT = -1e29
FORCE_SCORE = 1e4

LANES = 128
CMP_PAD = 8
CMP_TQ = 128
ATT_TQ = 256
FOX_TQ = 512
VMEM_LIMIT = 56 * 1024 * 1024


def _cparams(sem, vmem=VMEM_LIMIT):
    return pltpu.CompilerParams(dimension_semantics=sem, vmem_limit_bytes=vmem)


def _sigmoid(x):
    return 1.0 / (1.0 + jnp.exp(-x))


def _layer_norm(x):
    mu = jnp.mean(x, axis=-1, keepdims=True)
    xc = x - mu
    var = jnp.mean(xc * xc, axis=-1, keepdims=True)
    return xc * lax.rsqrt(var + LN_EPS)


def _split3(x):
    hi = x.astype(BF16)
    r1 = x - hi.astype(F32)
    mid = r1.astype(BF16)
    lo = (r1 - mid.astype(F32)).astype(BF16)
    return hi, mid, lo


def _dot(a, b):
    return jnp.dot(a, b, preferred_element_type=F32)


def _dot_nt(a, b):
    return lax.dot_general(a, b, (((1,), (1,)), ((), ())), preferred_element_type=F32)


def _dot3(x, w_bf16):
    hi, mid, lo = _split3(x)
    return _dot(hi, w_bf16) + _dot(mid, w_bf16) + _dot(lo, w_bf16)


def _ada_kernel(c_ref, w_ref, b_ref, o_ref):
    c = c_ref[...]
    a = c * _sigmoid(c)
    o_ref[...] = jnp.dot(a, w_ref[...], precision=HIGHEST, preferred_element_type=F32) + b_ref[...]


def _ada_mod(c, w, b):
    d, n = w.shape
    tn = 1024
    c8 = jnp.broadcast_to(c, (8, d))
    out = pl.pallas_call(
        _ada_kernel,
        grid=(n // tn,),
        in_specs=[pl.BlockSpec((8, d), lambda j: (0, 0)),
                  pl.BlockSpec((d, tn), lambda j: (0, j)),
                  pl.BlockSpec((1, tn), lambda j: (0, j))],
        out_specs=pl.BlockSpec((8, tn), lambda j: (0, j)),
        out_shape=jax.ShapeDtypeStruct((8, n), F32),
        compiler_params=_cparams(("parallel",)),
        name="ada_mod",
    )(c8, w, b.reshape(1, n))
    return out[0:1]


def _lnmod_kernel(x_ref, sc_ref, sh_ref, o_ref):
    y = _layer_norm(x_ref[...])
    o_ref[...] = (y * (1.0 + sc_ref[...]) + sh_ref[...]).astype(o_ref.dtype)


def _ln_mod(x, sc, sh, out_dtype, tm=512):
    m, d = x.shape
    return pl.pallas_call(
        _lnmod_kernel,
        grid=(m // tm,),
        in_specs=[pl.BlockSpec((tm, d), lambda i: (i, 0)),
                  pl.BlockSpec((1, d), lambda i: (0, 0)),
                  pl.BlockSpec((1, d), lambda i: (0, 0))],
        out_specs=pl.BlockSpec((tm, d), lambda i: (i, 0)),
        out_shape=jax.ShapeDtypeStruct((m, d), out_dtype),
        compiler_params=_cparams(("parallel",)),
        name="ln_mod",
    )(x, sc, sh)


def _mm_kernel(a_ref, w_ref, o_ref):
    o_ref[...] = _dot(a_ref[...], w_ref[...]).astype(o_ref.dtype)


def _matmul(a, w, out_dtype, tm, tn, name):
    m, k = a.shape
    n = w.shape[1]
    return pl.pallas_call(
        _mm_kernel,
        grid=(n // tn, m // tm),
        in_specs=[pl.BlockSpec((tm, k), lambda j, i: (i, 0)),
                  pl.BlockSpec((k, tn), lambda j, i: (0, j))],
        out_specs=pl.BlockSpec((tm, tn), lambda j, i: (i, j)),
        out_shape=jax.ShapeDtypeStruct((m, n), out_dtype),
        compiler_params=_cparams(("parallel", "parallel")),
        name=name,
    )(a, w)


def _gelu_tanh(x):
    return 0.5 * x * (1.0 + jnp.tanh(math.sqrt(2.0 / math.pi) * (x + 0.044715 * (x * x * x))))


def _compress_kernel(c_ref, pe_ref, w1a_ref, w1b_ref, b1_ref, w2_ref, o_ref, *, nch):
    c = c_ref[0, 0]
    w1a = w1a_ref[0]
    w1b = w1b_ref[0]
    half = CMP_STRIDE * HEAD_DIM
    a = _dot(c, w1a)
    b = _dot(c, w1b)
    b_next = pltpu.roll(b, shift=nch - 1, axis=0)
    pe = pe_ref[0]
    pb = _dot(pe[:, :half], w1a) + _dot(pe[:, half:], w1b)
    hid = _gelu_tanh(a + b_next + pb[0:1, :] + b1_ref[0])
    o_ref[0, 0] = _dot(hid.astype(BF16), w2_ref[0])


def _compress(kv_cmp, pe, w1, b1, w2):
    z, g, s, hd = kv_cmp.shape
    nch = s // CMP_STRIDE
    half = CMP_STRIDE * hd
    chunks = kv_cmp.reshape(z, g, nch, half)
    pe8 = jnp.broadcast_to(pe.reshape(z, 1, CMP_LEN * hd), (z, 8, CMP_LEN * hd)).astype(BF16)
    w1b16 = w1.astype(BF16)
    hidn = w1.shape[-1]
    return pl.pallas_call(
        functools.partial(_compress_kernel, nch=nch),
        grid=(z, g),
        in_specs=[pl.BlockSpec((1, 1, nch, half), lambda zi, gi: (zi, gi, 0, 0)),
                  pl.BlockSpec((1, 8, 2 * half), lambda zi, gi: (zi, 0, 0)),
                  pl.BlockSpec((1, half, hidn), lambda zi, gi: (zi, 0, 0)),
                  pl.BlockSpec((1, half, hidn), lambda zi, gi: (zi, 1, 0)),
                  pl.BlockSpec((1, 1, hidn), lambda zi, gi: (zi, 0, 0)),
                  pl.BlockSpec((1, hidn, hd), lambda zi, gi: (zi, 0, 0))],
        out_specs=pl.BlockSpec((1, 1, nch, hd), lambda zi, gi: (zi, gi, 0, 0)),
        out_shape=jax.ShapeDtypeStruct((z, g, nch, hd), F32),
        compiler_params=_cparams(("parallel", "parallel")),
        name="compress_kv",
    )(chunks, pe8, w1b16, w1b16, b1.reshape(z, 1, hidn), w2.astype(BF16))


def _t5_bucket_np(dist):
    n = np.maximum(dist, 0)
    ratio = np.log(np.maximum(n, T5_MAX_EXACT).astype(np.float64) / T5_MAX_EXACT)
    big = T5_MAX_EXACT + (ratio / math.log(T5_MAX_DIST / T5_MAX_EXACT)
                          * (T5_BUCKETS - T5_MAX_EXACT)).astype(np.int64)
    return np.where(n < T5_MAX_EXACT, n, np.minimum(big, T5_BUCKETS - 1)).astype(np.int32)


def _bias_table(tbh, dist, valid):
    g, hpg, _ = tbh.shape
    i, j = dist.shape
    vals = tbh[:, :, _t5_bucket_np(dist)]
    vals = jnp.where(jnp.asarray(valid)[None, None], vals, NEG_INF)
    return vals.reshape(g, hpg * i, j)


def _nsa_bias_tables(t5_table):
    tbh = t5_table.T.reshape(NSA_GROUPS, NSA_HPG, T5_BUCKETS).astype(F32)
    i = np.arange(CMP_TQ)[:, None]
    c = np.arange(LANES)[None, :]
    dist = i - (CMP_LEN - 1) - CMP_STRIDE * (c - CMP_PAD)
    near_cols = CMP_TQ // CMP_STRIDE + CMP_PAD
    b_cmp = _bias_table(tbh, dist, (dist >= 0) & (c < near_cols))
    i = np.arange(ATT_TQ)[:, None]
    j = np.arange(2 * ATT_TQ)[None, :]
    dist = i - j + ATT_TQ
    b_sel = _bias_table(tbh, dist, dist >= 0)
    j = np.arange(3 * ATT_TQ)[None, :]
    dist = i - j + 2 * ATT_TQ
    b_win = _bias_table(tbh, dist, (dist >= 0) & (dist < WINDOW))
    far = tbh[:, :, T5_BUCKETS - 1]
    far_cmp = jnp.repeat(far, CMP_TQ, axis=1)[..., None]
    far_att = jnp.repeat(far, ATT_TQ, axis=1)[..., None]
    return b_cmp, far_cmp, b_sel, b_win, far_att


def _slc_weight_np(ncp, nslc):
    w = np.zeros((ncp, nslc), np.float32)
    ratio = SLC_LEN // CMP_STRIDE
    for off, val in ((-1, 0.5), (0, 1.0), (1, 1.0), (2, 1.0), (3, 0.5)):
        j = np.arange(nslc)
        n = ratio * j + off
        ok = (n >= 0) & (n + CMP_PAD < ncp)
        w[n[ok] + CMP_PAD, j[ok]] += val
    return w


def _nsa_cmp_kernel(q_ref, kc_ref, vc_ref, bnear_ref, bfar_ref, w_ref, oc_ref, sel_ref, *, ncp, nslc, n_valid_cmp):
    tq = CMP_TQ
    rows = NSA_HPG * tq
    qb = pl.program_id(1)
    cpb = tq // CMP_STRIDE
    q2 = q_ref[0].reshape(rows, HEAD_DIM)
    kc = kc_ref[0].astype(BF16)
    vc = vc_ref[0].astype(BF16)
    r0 = pl.multiple_of(cpb * qb, 8)
    r = lax.broadcasted_iota(jnp.int32, (1, ncp), 1)
    s_far = _dot_nt(q2, kc) + bfar_ref[0]
    s_far = jnp.where((r >= CMP_PAD) & (r < r0), s_far, NEG_INF)
    kn = kc_ref[0, pl.ds(r0, LANES), :].astype(BF16)
    vn = vc_ref[0, pl.ds(r0, LANES), :].astype(BF16)
    rn = r0 + lax.broadcasted_iota(jnp.int32, (1, LANES), 1)
    s_near = _dot_nt(q2, kn) + bnear_ref[0]
    s_near = jnp.where((rn >= CMP_PAD) & (rn < n_valid_cmp + CMP_PAD), s_near, NEG_INF)
    m = jnp.maximum(jnp.max(s_far, axis=-1, keepdims=True), jnp.max(s_near, axis=-1, keepdims=True))
    e_far = jnp.exp(s_far - m)
    e_near = jnp.exp(s_near - m)
    l = jnp.sum(e_far, axis=-1, keepdims=True) + jnp.sum(e_near, axis=-1, keepdims=True)
    inv = jnp.where(m > M_INIT, 1.0 / l, 0.0)
    p_far = e_far * inv
    p_near = e_near * inv
    o = _dot(p_far.astype(BF16), vc) + _dot(p_near.astype(BF16), vn)
    oc_ref[0] = o.reshape(NSA_HPG, tq, HEAD_DIM)
    imp_far = jnp.sum(p_far.reshape(NSA_HPG, tq, ncp), axis=0)
    imp_near = jnp.sum(p_near.reshape(NSA_HPG, tq, LANES), axis=0)
    w_all = w_ref[...].astype(BF16)
    w_near = w_ref[pl.ds(r0, LANES), :].astype(BF16)
    p_slc = _dot3(imp_far, w_all) + _dot3(imp_near, w_near)
    blk = lax.broadcasted_iota(jnp.int32, (1, nslc), 1)
    t = qb * tq + lax.broadcasted_iota(jnp.int32, (tq, 1), 0)
    cur = t // SLC_LEN
    forced = (blk == 0) | (blk == cur) | (blk == cur - 1)
    score = jnp.where(forced, FORCE_SCORE, jnp.where(blk <= cur, p_slc, -1.0))
    blk_f = blk.astype(F32)
    sel = jnp.zeros((tq, nslc), F32)
    for _ in range(min(SLC_TOPK, nslc)):
        mx = jnp.max(score, axis=-1, keepdims=True)
        first = jnp.min(jnp.where(score == mx, blk_f, float(nslc)), axis=-1, keepdims=True)
        hit = blk_f == first
        sel = jnp.where(hit, 1.0, sel)
        score = jnp.where(hit, -2.0, score)
    sel_ref[0] = sel.astype(BF16)


def _nsa_compress_select(q, kc_pad, vc_pad, b_cmp, far_cmp, w_slc, n_valid_cmp):
    g, hpg, s, hd = q.shape
    ncp = kc_pad.shape[1]
    nslc = s // SLC_LEN
    tq = CMP_TQ
    rows = hpg * tq
    return pl.pallas_call(
        functools.partial(_nsa_cmp_kernel, ncp=ncp, nslc=nslc, n_valid_cmp=n_valid_cmp),
        grid=(g, s // tq),
        in_specs=[pl.BlockSpec((1, hpg, tq, hd), lambda gi, qi: (gi, 0, qi, 0)),
                  pl.BlockSpec((1, ncp, hd), lambda gi, qi: (gi, 0, 0)),
                  pl.BlockSpec((1, ncp, hd), lambda gi, qi: (gi, 0, 0)),
                  pl.BlockSpec((1, rows, LANES), lambda gi, qi: (gi, 0, 0)),
                  pl.BlockSpec((1, rows, 1), lambda gi, qi: (gi, 0, 0)),
                  pl.BlockSpec((ncp, nslc), lambda gi, qi: (0, 0))],
        out_specs=[pl.BlockSpec((1, hpg, tq, hd), lambda gi, qi: (gi, 0, qi, 0)),
                   pl.BlockSpec((1, tq, nslc), lambda gi, qi: (gi, qi, 0))],
        out_shape=[jax.ShapeDtypeStruct((g, hpg, s, hd), F32),
                   jax.ShapeDtypeStruct((g, s, nslc), BF16)],
        compiler_params=_cparams(("parallel", "parallel")),
        name="nsa_compress_select",
    )(q, kc_pad, vc_pad, b_cmp, far_cmp, w_slc)


def _flash_step(s, v_tile, m_ref, l_ref, acc_ref):
    m_old = m_ref[...]
    m_new = jnp.maximum(m_old, jnp.max(s, axis=-1, keepdims=True))
    alpha = jnp.exp(m_old - m_new)
    p = jnp.exp(s - m_new)
    l_ref[...] = alpha * l_ref[...] + jnp.sum(p, axis=-1, keepdims=True)
    acc_ref[...] = alpha * acc_ref[...] + _dot(p.astype(BF16), v_tile)
    m_ref[...] = m_new


def _flash_init(m_ref, l_ref, acc_ref):
    m_ref[...] = jnp.full(m_ref.shape, M_INIT, F32)
    l_ref[...] = jnp.zeros(l_ref.shape, F32)
    acc_ref[...] = jnp.zeros(acc_ref.shape, F32)


def _nsa_att_kernel(q_ref, kst_ref, vs_ref, kwt_ref, vw_ref, sel_ref, bsel_ref, bwin_ref, bfar_ref,
                    oc_ref, gate_ref, o_ref, ms, ls, accs, mw, lw, accw, *, nslc):
    tq = ATT_TQ
    rows = NSA_HPG * tq
    gi = pl.program_id(0)
    qb = pl.program_id(1)
    q2 = q_ref[0].reshape(rows, HEAD_DIM)
    sel_q = sel_ref[0]
    _flash_init(ms, ls, accs)
    _flash_init(mw, lw, accw)
    blk_row = lax.broadcasted_iota(jnp.int32, (nslc, tq), 0)
    key_col = lax.broadcasted_iota(jnp.int32, (nslc, tq), 1)

    def sel_mask_add(k0):
        expand = jnp.where(blk_row == (k0 + key_col) // SLC_LEN, 1.0, 0.0).astype(BF16)
        picked = _dot(sel_q, expand)
        madd = (picked - 1.0) * (-NEG_INF)
        return jnp.concatenate([madd] * NSA_HPG, axis=0)

    def sel_tile(kt, bias):
        k0 = pl.multiple_of(kt * tq, tq)
        s = _dot(q2, kst_ref[0, :, pl.ds(k0, tq)]) + bias + sel_mask_add(k0)
        _flash_step(s, vs_ref[pl.ds(k0, tq), :], ms, ls, accs)

    def far_body(kt, carry):
        sel_tile(kt, bfar_ref[0])
        return carry

    lax.fori_loop(0, jnp.maximum(qb - 1, 0), far_body, 0)

    @pl.when(qb >= 1)
    def _():
        sel_tile(qb - 1, bsel_ref[0, :, 0:tq])

    sel_tile(qb, bsel_ref[0, :, tq:2 * tq])

    def win_tile(kt, bias):
        k0 = pl.multiple_of(kt * tq, tq)
        s = _dot(q2, kwt_ref[0, :, pl.ds(k0, tq)]) + bias
        _flash_step(s, vw_ref[pl.ds(k0, tq), :], mw, lw, accw)

    @pl.when(qb >= 2)
    def _():
        win_tile(qb - 2, bwin_ref[0, :, 0:tq])

    @pl.when(qb >= 1)
    def _():
        win_tile(qb - 1, bwin_ref[0, :, tq:2 * tq])

    win_tile(qb, bwin_ref[0, :, 2 * tq:3 * tq])

    acc_s = jnp.where(gi == 0, accs[:, 0:HEAD_DIM], accs[:, HEAD_DIM:2 * HEAD_DIM])
    acc_w = jnp.where(gi == 0, accw[:, 0:HEAD_DIM], accw[:, HEAD_DIM:2 * HEAD_DIM])
    o_s = acc_s / ls[...]
    o_w = acc_w / lw[...]
    gt = _sigmoid(gate_ref[0].reshape(rows, 3))
    o_c = oc_ref[0].reshape(rows, HEAD_DIM)
    out = gt[:, 0:1] * o_c + gt[:, 1:2] * o_s + gt[:, 2:3] * o_w
    o_ref[0] = out.reshape(NSA_HPG, tq, HEAD_DIM).astype(o_ref.dtype)


def _nsa_attention(q, kst, vs2, kwt, vw2, sel, b_sel, b_win, far_att, o_c, gates):
    g, hpg, s, hd = q.shape
    nslc = s // SLC_LEN
    tq = ATT_TQ
    rows = hpg * tq
    resident = pl.Buffered(1)
    return pl.pallas_call(
        functools.partial(_nsa_att_kernel, nslc=nslc),
        grid=(g, s // tq),
        in_specs=[pl.BlockSpec((1, hpg, tq, hd), lambda gi, qi: (gi, 0, qi, 0)),
                  pl.BlockSpec((1, hd, s), lambda gi, qi: (gi, 0, 0), pipeline_mode=resident),
                  pl.BlockSpec((s, 2 * hd), lambda gi, qi: (0, 0), pipeline_mode=resident),
                  pl.BlockSpec((1, hd, s), lambda gi, qi: (gi, 0, 0), pipeline_mode=resident),
                  pl.BlockSpec((s, 2 * hd), lambda gi, qi: (0, 0), pipeline_mode=resident),
                  pl.BlockSpec((1, tq, nslc), lambda gi, qi: (gi, qi, 0)),
                  pl.BlockSpec((1, rows, 2 * tq), lambda gi, qi: (gi, 0, 0), pipeline_mode=resident),
                  pl.BlockSpec((1, rows, 3 * tq), lambda gi, qi: (gi, 0, 0), pipeline_mode=resident),
                  pl.BlockSpec((1, rows, 1), lambda gi, qi: (gi, 0, 0)),
                  pl.BlockSpec((1, hpg, tq, hd), lambda gi, qi: (gi, 0, qi, 0)),
                  pl.BlockSpec((1, hpg, tq, 3), lambda gi, qi: (gi, 0, qi, 0))],
        out_specs=pl.BlockSpec((1, hpg, tq, hd), lambda gi, qi: (gi, 0, qi, 0)),
        out_shape=jax.ShapeDtypeStruct((g, hpg, s, hd), BF16),
        scratch_shapes=[pltpu.VMEM((rows, 1), F32), pltpu.VMEM((rows, 1), F32), pltpu.VMEM((rows, 2 * hd), F32),
                        pltpu.VMEM((rows, 1), F32), pltpu.VMEM((rows, 1), F32), pltpu.VMEM((rows, 2 * hd), F32)],
        compiler_params=_cparams(("arbitrary", "arbitrary")),
        name="nsa_select_window",
    )(q, kst, vs2, kwt, vw2, sel, b_sel, b_win, far_att, o_c, gates)


def _decay_kernel(z_ref, b_ref, o_ref, carry_ref, *, tb):
    @pl.when(pl.program_id(0) == 0)
    def _():
        carry_ref[...] = jnp.zeros(carry_ref.shape, F32)

    z = z_ref[...] + b_ref[...]
    log_f = jnp.minimum(z, 0.0) - jnp.log1p(jnp.exp(-jnp.abs(z)))
    r = lax.broadcasted_iota(jnp.int32, (tb, tb), 0)
    c = lax.broadcasted_iota(jnp.int32, (tb, tb), 1)
    tri = jnp.where(r >= c, 1.0, 0.0).astype(BF16)
    run = _dot3_rhs(tri, log_f) + carry_ref[...]
    o_ref[...] = run
    carry_ref[...] = run[tb - 1:tb, :]


def _dot3_rhs(w_bf16, x):
    hi, mid, lo = _split3(x)
    return _dot(w_bf16, hi) + _dot(w_bf16, mid) + _dot(w_bf16, lo)


def _decay_cumsum(z, bias, tb=512):
    s, n = z.shape
    return pl.pallas_call(
        functools.partial(_decay_kernel, tb=tb),
        grid=(s // tb,),
        in_specs=[pl.BlockSpec((tb, n), lambda i: (i, 0)),
                  pl.BlockSpec((1, n), lambda i: (0, 0))],
        out_specs=pl.BlockSpec((tb, n), lambda i: (i, 0)),
        out_shape=jax.ShapeDtypeStruct((s, n), F32),
        scratch_shapes=[pltpu.VMEM((1, n), F32)],
        compiler_params=_cparams(("arbitrary",)),
        name="decay_cumsum",
    )(z, bias)


def _fox_kernel(q_ref, kt_ref, v_ref, fq_ref, fk_ref, o_ref, m_ref, l_ref, acc_ref):
    tq = FOX_TQ
    h = pl.program_id(0)
    qb = pl.program_id(1)
    q = q_ref[0]
    fq = fq_ref[0]
    _flash_init(m_ref, l_ref, acc_ref)

    def scores(kt):
        k0 = pl.multiple_of(kt * tq, tq)
        s = _dot(q, kt_ref[0, :, pl.ds(k0, tq)])
        return (s + fq) - fk_ref[0, :, pl.ds(k0, tq)], k0

    def far_body(kt, carry):
        s, k0 = scores(kt)
        _flash_step(s, v_ref[0, pl.ds(k0, tq), :], m_ref, l_ref, acc_ref)
        return carry

    lax.fori_loop(0, qb, far_body, 0)
    s, k0 = scores(qb)
    r = lax.broadcasted_iota(jnp.int32, (tq, tq), 0)
    c = lax.broadcasted_iota(jnp.int32, (tq, tq), 1)
    s = jnp.where(c <= r, s, NEG_INF)
    _flash_step(s, v_ref[0, pl.ds(k0, tq), :], m_ref, l_ref, acc_ref)
    acc = jnp.where(h % 2 == 0, acc_ref[:, 0:HEAD_DIM], acc_ref[:, HEAD_DIM:2 * HEAD_DIM])
    o_ref[0] = (acc / l_ref[...]).astype(o_ref.dtype)


def _fox_attention(q, kt, v2, fq, fk):
    h, s, hd = q.shape
    tq = FOX_TQ
    resident = pl.Buffered(1)
    return pl.pallas_call(
        _fox_kernel,
        grid=(h, s // tq),
        in_specs=[pl.BlockSpec((1, tq, hd), lambda hi, qi: (hi, qi, 0)),
                  pl.BlockSpec((1, hd, s), lambda hi, qi: (hi, 0, 0), pipeline_mode=resident),
                  pl.BlockSpec((1, s, 2 * hd), lambda hi, qi: (hi // 2, 0, 0), pipeline_mode=resident),
                  pl.BlockSpec((1, tq, 1), lambda hi, qi: (hi, qi, 0)),
                  pl.BlockSpec((1, 1, s), lambda hi, qi: (hi, 0, 0))],
        out_specs=pl.BlockSpec((1, tq, hd), lambda hi, qi: (hi, qi, 0)),
        out_shape=jax.ShapeDtypeStruct((h, s, hd), BF16),
        scratch_shapes=[pltpu.VMEM((tq, 1), F32), pltpu.VMEM((tq, 1), F32), pltpu.VMEM((tq, 2 * hd), F32)],
        compiler_params=_cparams(("arbitrary", "arbitrary")),
        name="fox_attention",
    )(q, kt, v2, fq, fk)


def _merge_kernel(on_ref, of_ref, mg_ref, x_ref, wn_ref, wf_ref, wo_ref, g1_ref, lg_ref, lb_ref, o_ref, *, alpha):
    d = x_ref.shape[-1]
    a = _dot(on_ref[...], wn_ref[...])
    b = _dot(of_ref[...], wf_ref[...])
    gm = _sigmoid(mg_ref[...].astype(F32))
    merged = gm[:, 0:d] * a + gm[:, d:2 * d] * b
    y = _dot(merged.astype(BF16), wo_ref[...])
    z = alpha * x_ref[...] + (1.0 + g1_ref[...]) * y
    o_ref[...] = _layer_norm(z) * lg_ref[...] + lb_ref[...]


def _merge_project(o_nsa, o_fox, merge, x, wn, wf, wo, g1, ln_g, ln_b, alpha, tm=256):
    m, d = x.shape
    w = o_nsa.shape[1]
    resident = pl.Buffered(1)
    row = lambda i: (i, 0)
    fixed = lambda i: (0, 0)
    return pl.pallas_call(
        functools.partial(_merge_kernel, alpha=alpha),
        grid=(m // tm,),
        in_specs=[pl.BlockSpec((tm, w), row), pl.BlockSpec((tm, w), row),
                  pl.BlockSpec((tm, 2 * d), row), pl.BlockSpec((tm, d), row),
                  pl.BlockSpec((w, d), fixed, pipeline_mode=resident),
                  pl.BlockSpec((w, d), fixed, pipeline_mode=resident),
                  pl.BlockSpec((d, d), fixed, pipeline_mode=resident),
                  pl.BlockSpec((1, d), fixed), pl.BlockSpec((1, d), fixed), pl.BlockSpec((1, d), fixed)],
        out_specs=pl.BlockSpec((tm, d), row),
        out_shape=jax.ShapeDtypeStruct((m, d), F32),
        compiler_params=_cparams(("parallel",)),
        name="merge_project_ln",
    )(o_nsa, o_fox, merge, x, wn, wf, wo, g1, ln_g, ln_b)


def _router_kernel(x_ref, sc_ref, sh_ref, w_ref, b_ref, u_ref, r_ref):
    u = _layer_norm(x_ref[...]) * (1.0 + sc_ref[...]) + sh_ref[...]
    u_ref[...] = u
    logits = jnp.dot(u, w_ref[...], precision=HIGHEST, preferred_element_type=F32) + b_ref[...]
    lane = lax.broadcasted_iota(jnp.int32, (1, LANES), 1).astype(F32)
    none = float(LANES)
    is_g = lane < N_GROUPS
    lg = jnp.where(is_g, logits, NEG_INF)
    eg = jnp.exp(lg - jnp.max(lg, axis=-1, keepdims=True))
    pg = eg / jnp.sum(eg, axis=-1, keepdims=True)
    p_grp = jnp.max(pg, axis=-1, keepdims=True)
    grp = jnp.min(jnp.where(pg == p_grp, lane, none), axis=-1, keepdims=True)
    lo = N_GROUPS + grp * EXPERTS_PER_GROUP
    is_e = (lane >= lo) & (lane < lo + EXPERTS_PER_GROUP)
    le = jnp.where(is_e, logits, NEG_INF)
    ee = jnp.exp(le - jnp.max(le, axis=-1, keepdims=True))
    pe = jnp.where(is_e, ee / jnp.sum(ee, axis=-1, keepdims=True), -1.0)
    p1 = jnp.max(pe, axis=-1, keepdims=True)
    i1 = jnp.min(jnp.where(pe == p1, lane, none), axis=-1, keepdims=True)
    pe2 = jnp.where(lane == i1, -1.0, pe)
    p2 = jnp.max(pe2, axis=-1, keepdims=True)
    i2 = jnp.min(jnp.where(pe2 == p2, lane, none), axis=-1, keepdims=True)
    den = p1 + p2
    r_ref[...] = jnp.where(lane == 0, i1 - N_GROUPS,
                           jnp.where(lane == 1, i2 - N_GROUPS,
                                     jnp.where(lane == 2, p_grp * p1 / den,
                                               jnp.where(lane == 3, p_grp * p2 / den, 0.0))))


def _router(x1, sc, sh, w_r, b_r, tm=256):
    m, d = x1.shape
    row = lambda i: (i, 0)
    fixed = lambda i: (0, 0)
    return pl.pallas_call(
        _router_kernel,
        grid=(m // tm,),
        in_specs=[pl.BlockSpec((tm, d), row), pl.BlockSpec((1, d), fixed), pl.BlockSpec((1, d), fixed),
                  pl.BlockSpec((d, LANES), fixed), pl.BlockSpec((1, LANES), fixed)],
        out_specs=[pl.BlockSpec((tm, d), row), pl.BlockSpec((tm, LANES), row)],
        out_shape=[jax.ShapeDtypeStruct((m, d), F32), jax.ShapeDtypeStruct((m, LANES), F32)],
        compiler_params=_cparams(("parallel",)),
        name="moe_router",
    )(x1, sc, sh, w_r, b_r)


def _moe_kernel(be_ref, nv_ref, tok_ref, dst_ref, rw_ref, u_hbm, wg_ref, wu_ref, wd_ref, out_hbm,
                xbuf, ybuf, wgb, wub, wdb, sem_in, sem_out):
    i = pl.program_id(0)
    nv = nv_ref[i]

    @pl.when(i == 0)
    def _():
        xbuf[...] = jnp.zeros(xbuf.shape, F32)

    def row_in(r, tok):
        return pltpu.make_async_copy(u_hbm.at[pl.ds(tok, 1), :], xbuf.at[pl.ds(r, 1), :], sem_in)

    def row_out(r, dst):
        return pltpu.make_async_copy(ybuf.at[pl.ds(r, 1), :], out_hbm.at[pl.ds(dst, 1), :], sem_out)

    def start_in(r, carry):
        row_in(r, tok_ref[0, 0, r]).start()
        return carry

    lax.fori_loop(0, nv, start_in, 0)

    prev = be_ref[jnp.maximum(i - 1, 0)]

    @pl.when((i == 0) | (be_ref[i] != prev))
    def _():
        wgb[...] = wg_ref[0].astype(BF16)
        wub[...] = wu_ref[0].astype(BF16)
        wdb[...] = wd_ref[0].astype(BF16)

    def wait_in(r, carry):
        row_in(r, 0).wait()
        return carry

    lax.fori_loop(0, nv, wait_in, 0)

    xb = xbuf[...].astype(BF16)
    gate = _dot(xb, wgb[...])
    up = _dot(xb, wub[...])
    hid = (gate * _sigmoid(gate)) * up
    ybuf[...] = _dot(hid.astype(BF16), wdb[...]) * rw_ref[0]

    def start_out(r, carry):
        row_out(r, dst_ref[0, 0, r]).start()
        return carry

    lax.fori_loop(0, nv, start_out, 0)

    def wait_out(r, carry):
        row_out(r, 0).wait()
        return carry

    lax.fori_loop(0, nv, wait_out, 0)


def _moe_experts(u, blk_exp, n_valid, row_tok, row_dst, row_w, w_gate, w_up, w_down):
    t, d = u.shape
    n_blocks = blk_exp.shape[0]
    de = w_gate.shape[-1]
    rb = ROW_BLOCK
    grid_spec = pltpu.PrefetchScalarGridSpec(
        num_scalar_prefetch=2,
        grid=(n_blocks,),
        in_specs=[pl.BlockSpec((1, 1, rb), lambda i, be, nv: (i, 0, 0), memory_space=pltpu.SMEM),
                  pl.BlockSpec((1, 1, rb), lambda i, be, nv: (i, 0, 0), memory_space=pltpu.SMEM),
                  pl.BlockSpec((1, rb, 1), lambda i, be, nv: (i, 0, 0)),
                  pl.BlockSpec(memory_space=pl.ANY),
                  pl.BlockSpec((1, d, de), lambda i, be, nv: (be[i], 0, 0)),
                  pl.BlockSpec((1, d, de), lambda i, be, nv: (be[i], 0, 0)),
                  pl.BlockSpec((1, de, d), lambda i, be, nv: (be[i], 0, 0))],
        out_specs=pl.BlockSpec(memory_space=pl.ANY),
        scratch_shapes=[pltpu.VMEM((rb, d), F32), pltpu.VMEM((rb, d), F32),
                        pltpu.VMEM((d, de), BF16), pltpu.VMEM((d, de), BF16), pltpu.VMEM((de, d), BF16),
                        pltpu.SemaphoreType.DMA(()), pltpu.SemaphoreType.DMA(())],
    )
    return pl.pallas_call(
        _moe_kernel,
        grid_spec=grid_spec,
        out_shape=jax.ShapeDtypeStruct((2 * t, d), F32),
        compiler_params=_cparams(("arbitrary",)),
        name="moe_experts",
    )(blk_exp, n_valid, row_tok.reshape(n_blocks, 1, rb), row_dst.reshape(n_blocks, 1, rb),
      row_w.reshape(n_blocks, rb, 1), u, w_gate, w_up, w_down)


def _moe_dispatch(route, t):
    k = 2
    eid = route[:, 0:k].astype(jnp.int32).reshape(-1)
    wts = route[:, k:2 * k].reshape(-1)
    n_asg = t * k
    n_rows = n_asg + N_EXPERTS * ROW_BLOCK
    n_blocks = n_rows // ROW_BLOCK
    order = jnp.argsort(eid)
    e_sorted = eid[order]
    counts = jnp.bincount(eid, length=N_EXPERTS)
    padded = (counts + ROW_BLOCK - 1) // ROW_BLOCK * ROW_BLOCK
    pad_end = jnp.cumsum(padded)
    pad_start = pad_end - padded
    start = jnp.cumsum(counts) - counts
    dest = pad_start[e_sorted] + jnp.arange(n_asg) - start[e_sorted]
    tok = (order // k).astype(jnp.int32)
    slot = (order % k).astype(jnp.int32)
    row_tok = jnp.zeros((n_rows,), jnp.int32).at[dest].set(tok)
    row_dst = jnp.zeros((n_rows,), jnp.int32).at[dest].set(slot * t + tok)
    row_w = jnp.zeros((n_rows,), F32).at[dest].set(wts[order])
    blk_start = jnp.arange(n_blocks) * ROW_BLOCK
    blk_exp = jnp.minimum(jnp.searchsorted(pad_end, blk_start, side='right'), N_EXPERTS - 1).astype(jnp.int32)
    n_valid = jnp.clip(pad_start[blk_exp] + counts[blk_exp] - blk_start, 0, ROW_BLOCK).astype(jnp.int32)
    return blk_exp, n_valid, row_tok, row_dst, row_w


def _final_kernel(x_ref, y0_ref, y1_ref, g2_ref, lg_ref, lb_ref, o_ref, *, alpha):
    z = alpha * x_ref[...] + (1.0 + g2_ref[...]) * (y0_ref[...] + y1_ref[...])
    o_ref[...] = _layer_norm(z) * lg_ref[...] + lb_ref[...]


def _final_ln(x1, y2, g2, ln_g, ln_b, alpha, tm=512):
    m, d = x1.shape
    nb = m // tm
    fixed = lambda i: (0, 0)
    return pl.pallas_call(
        functools.partial(_final_kernel, alpha=alpha),
        grid=(nb,),
        in_specs=[pl.BlockSpec((tm, d), lambda i: (i, 0)),
                  pl.BlockSpec((tm, d), lambda i: (i, 0)),
                  pl.BlockSpec((tm, d), lambda i: (i + nb, 0)),
                  pl.BlockSpec((1, d), fixed), pl.BlockSpec((1, d), fixed), pl.BlockSpec((1, d), fixed)],
        out_specs=pl.BlockSpec((tm, d), lambda i: (i, 0)),
        out_shape=jax.ShapeDtypeStruct((m, d), F32),
        compiler_params=_cparams(("parallel",)),
        name="final_ln",
    )(x1, y2, y2, g2, ln_g, ln_b)


def _layer(x2d, c, w_ada, b_ada, w_in, b_fgt, t5_table, cmp_pe, cmp_w1, cmp_b1, cmp_w2,
           w_br_nsa, w_br_fox, w_o, ln1_g, ln1_b, w_rg, b_rg, w_re, b_re,
           w_gate, w_up, w_down, ln2_g, ln2_b, alpha):
    s, d = x2d.shape
    hd = HEAD_DIM
    g = NSA_GROUPS
    mod = _ada_mod(c, w_ada, b_ada)
    sh1, sc1, g1, sh2, sc2, g2 = [mod[:, i * d:(i + 1) * d] for i in range(6)]

    c_q = NSA_HEADS * hd
    c_kv = 6 * g * hd
    c_gate = 3 * NSA_HEADS
    c_fox = 3 * FOX_HEADS * hd
    off_kv = c_q
    off_gate = off_kv + c_kv
    off_fox = off_gate + c_gate
    off_fgt = off_fox + c_fox
    off_merge = off_fgt + FOX_HEADS
    scale = hd ** -0.5
    w_attn = jnp.concatenate([w_in[:, 0:off_kv] * scale, w_in[:, off_kv:off_gate],
                              w_in[:, off_fox:off_fox + FOX_HEADS * hd] * scale,
                              w_in[:, off_fox + FOX_HEADS * hd:off_fgt]], axis=1).astype(BF16)
    n_small = c_gate + FOX_HEADS
    w_small = jnp.concatenate([w_in[:, off_gate:off_fox], w_in[:, off_fgt:off_merge],
                               jnp.zeros((d, LANES - n_small), F32)], axis=1).astype(BF16)
    w_merge = w_in[:, off_merge:].astype(BF16)

    u = _ln_mod(x2d, sc1, sh1, BF16)
    n_attn = w_attn.shape[1]
    proj = _matmul(u, w_attn, BF16, 512, n_attn // 2, "in_proj_attn")
    small = _matmul(u, w_small, F32, 512, LANES, "in_proj_small")
    merge = _matmul(u, w_merge, BF16, 512, 1024, "in_proj_merge")

    q_nsa = proj[:, 0:c_q].reshape(s, g, NSA_HPG, hd).transpose(1, 2, 0, 3)
    kv = proj[:, c_q:c_q + c_kv].reshape(s, 6, g, hd).transpose(1, 2, 0, 3)
    fox = proj[:, c_q + c_kv:].reshape(s, 3, FOX_HEADS, hd).transpose(1, 2, 0, 3)

    kv_cmp = _compress(kv[0:2], cmp_pe, cmp_w1, cmp_b1, cmp_w2)
    nch = s // CMP_STRIDE
    ncp = nch + LANES
    kv_cmp_pad = jnp.pad(kv_cmp, ((0, 0), (0, 0), (CMP_PAD, ncp - nch - CMP_PAD), (0, 0)))
    b_cmp, far_cmp, b_sel, b_win, far_att = _nsa_bias_tables(t5_table)
    w_slc = jnp.asarray(_slc_weight_np(ncp, s // SLC_LEN))
    o_c, sel = _nsa_compress_select(q_nsa, kv_cmp_pad[0], kv_cmp_pad[1], b_cmp, far_cmp, w_slc, nch - 1)
    kst = kv[2].transpose(0, 2, 1)
    kwt = kv[4].transpose(0, 2, 1)
    vs2 = kv[3].transpose(1, 0, 2).reshape(s, g * hd)
    vw2 = kv[5].transpose(1, 0, 2).reshape(s, g * hd)
    gates = small[:, 0:c_gate].reshape(s, g, NSA_HPG, 3).transpose(1, 2, 0, 3)
    o_nsa = _nsa_attention(q_nsa, kst, vs2, kwt, vw2, sel, b_sel, b_win, far_att, o_c, gates)
    o_nsa = o_nsa.transpose(2, 0, 1, 3).reshape(s, MIX_W)

    fgt_bias = jnp.concatenate([jnp.zeros((c_gate,), F32), b_fgt, jnp.zeros((LANES - n_small,), F32)])[None, :]
    decay = _decay_cumsum(small, fgt_bias)[:, c_gate:c_gate + FOX_HEADS]
    fq = decay.T[:, :, None]
    fk = decay.T[:, None, :]
    fox_kt = fox[1].transpose(0, 2, 1)
    fox_v2 = fox[2].reshape(FOX_HEADS // 2, 2, s, hd).transpose(0, 2, 1, 3).reshape(FOX_HEADS // 2, s, 2 * hd)
    o_fox = _fox_attention(fox[0], fox_kt, fox_v2, fq, fk)
    o_fox = o_fox.transpose(1, 0, 2).reshape(s, MIX_W)

    x1 = _merge_project(o_nsa, o_fox, merge, x2d, w_br_nsa.astype(BF16), w_br_fox.astype(BF16),
                        w_o.astype(BF16), g1, ln1_g[None, :], ln1_b[None, :], alpha)

    n_r = N_GROUPS + N_EXPERTS
    w_r = jnp.concatenate([w_rg, w_re.reshape(d, N_EXPERTS), jnp.zeros((d, LANES - n_r), F32)], axis=1)
    b_r = jnp.concatenate([b_rg, b_re.reshape(N_EXPERTS), jnp.zeros((LANES - n_r,), F32)])[None, :]
    u2, route = _router(x1, sc2, sh2, w_r, b_r)
    blk_exp, n_valid, row_tok, row_dst, row_w = _moe_dispatch(route, s)
    y2 = _moe_experts(u2, blk_exp, n_valid, row_tok, row_dst, row_w, w_gate, w_up, w_down)
    return _final_ln(x1, y2, g2, ln2_g[None, :], ln2_b[None, :], alpha)


def kernel(x, c, w_ada, b_ada, w_in, b_fgt, t5_table, cmp_pe, cmp_w1, cmp_b1, cmp_w2, w_br_nsa, w_br_fox, w_o,
           ln1_g, ln1_b, w_rg, b_rg, w_re, b_re, w_gate, w_up, w_down, ln2_g, ln2_b):
    b, s, d = x.shape
    depth = w_ada.shape[0]
    assert b == 1
    alpha = (2 * depth) ** 0.25
    h = x[0]
    for l in range(depth):
        h = _layer(h, c, w_ada[l], b_ada[l], w_in[l], b_fgt[l], t5_table, cmp_pe[l], cmp_w1[l], cmp_b1[l],
                   cmp_w2[l], w_br_nsa[l], w_br_fox[l], w_o[l], ln1_g[l], ln1_b[l], w_rg[l], b_rg[l],
                   w_re[l], b_re[l], w_gate[l], w_up[l], w_down[l], ln2_g[l], ln2_b[l], alpha)
    return h[None]
```

```python
import functools
import math

import numpy as np
import jax
import jax.numpy as jnp
from jax import lax
from jax.experimental import pallas as pl
from jax.experimental.pallas import tpu as pltpu

F32 = jnp.float32
BF16 = jnp.bfloat16
HIGHEST = lax.Precision.HIGHEST
LOG2E = math.log2(math.e)

HEAD_DIM = 64
NSA_HEADS = 8
NSA_GROUPS = 2
NSA_HPG = NSA_HEADS // NSA_GROUPS
FOX_HEADS = 8
MIX_W = NSA_HEADS * HEAD_DIM
CMP_LEN = 32
CMP_STRIDE = 16
SLC_LEN = 64
SLC_TOPK = 16
WINDOW = 512
T5_BUCKETS = 32
T5_MAX_EXACT = 16
T5_MAX_DIST = 128
N_GROUPS = 8
EXPERTS_PER_GROUP = 8
N_EXPERTS = N_GROUPS * EXPERTS_PER_GROUP
ROW_BLOCK = 128
LN_EPS = 1e-5
NEG_INF = -1e30
M_INIT = -1e29
FORCE_SCORE = 1e4

LANES = 128
BF16_ROWS = 16
CMP_PAD = 8
CMP_TQ = 128
ATT_TQ = 256
FOX_TQ = 1024
FOX_TK = 512
AUG_K = 128
AUG_V = HEAD_DIM + 16
ROW_MASK = HEAD_DIM
ROW_BIAS = HEAD_DIM + 16
VMEM_LIMIT = 56 * 1024 * 1024


def _cparams(sem, vmem=VMEM_LIMIT):
    return pltpu.CompilerParams(dimension_semantics=sem, vmem_limit_bytes=vmem)


def _sigmoid(x):
    return 1.0 / (1.0 + jnp.exp(-x))


def _layer_norm(x):
    mu = jnp.mean(x, axis=-1, keepdims=True)
    xc = x - mu
    var = jnp.mean(xc * xc, axis=-1, keepdims=True)
    return xc * lax.rsqrt(var + LN_EPS)


def _split3(x):
    hi = x.astype(BF16)
    r1 = x - hi.astype(F32)
    mid = r1.astype(BF16)
    lo = (r1 - mid.astype(F32)).astype(BF16)
    return hi, mid, lo


def _split3_exact(x):
    def trunc(v):
        bits = lax.bitcast_convert_type(v, jnp.uint32) & jnp.uint32(0xFFFF0000)
        return lax.bitcast_convert_type(bits, F32)
    hi = trunc(x)
    r1 = x - hi
    mid = trunc(r1)
    lo = r1 - mid
    return hi.astype(BF16), mid.astype(BF16), lo.astype(BF16)


def _dot(a, b):
    return jnp.dot(a, b, preferred_element_type=F32)


def _dot_nt(a, b):
    return lax.dot_general(a, b, (((1,), (1,)), ((), ())), preferred_element_type=F32)


def _dot3(x, w_bf16):
    hi, mid, lo = _split3(x)
    return _dot(hi, w_bf16) + _dot(mid, w_bf16) + _dot(lo, w_bf16)


def _dot3_rhs(w_bf16, x):
    hi, mid, lo = _split3(x)
    return _dot(w_bf16, hi) + _dot(w_bf16, mid) + _dot(w_bf16, lo)


def _ada_kernel(c_ref, w_ref, b_ref, o_ref):
    c = c_ref[...]
    a = c * _sigmoid(c)
    o_ref[...] = jnp.dot(a, w_ref[...], precision=HIGHEST, preferred_element_type=F32) + b_ref[...]


def _ada_mod(c, w, b):
    d, n = w.shape
    tn = 1024
    c8 = jnp.broadcast_to(c, (8, d))
    out = pl.pallas_call(
        _ada_kernel,
        grid=(n // tn,),
        in_specs=[pl.BlockSpec((8, d), lambda j: (0, 0)),
                  pl.BlockSpec((d, tn), lambda j: (0, j)),
                  pl.BlockSpec((1, tn), lambda j: (0, j))],
        out_specs=pl.BlockSpec((8, tn), lambda j: (0, j)),
        out_shape=jax.ShapeDtypeStruct((8, n), F32),
        compiler_params=_cparams(("parallel",)),
        name="ada_mod",
    )(c8, w, b.reshape(1, n))
    return out[0:1]


def _lnmod_kernel(x_ref, sc_ref, sh_ref, o_ref):
    y = _layer_norm(x_ref[...])
    o_ref[...] = (y * (1.0 + sc_ref[...]) + sh_ref[...]).astype(o_ref.dtype)


def _ln_mod(x, sc, sh, out_dtype, tm=512):
    m, d = x.shape
    return pl.pallas_call(
        _lnmod_kernel,
        grid=(m // tm,),
        in_specs=[pl.BlockSpec((tm, d), lambda i: (i, 0)),
                  pl.BlockSpec((1, d), lambda i: (0, 0)),
                  pl.BlockSpec((1, d), lambda i: (0, 0))],
        out_specs=pl.BlockSpec((tm, d), lambda i: (i, 0)),
        out_shape=jax.ShapeDtypeStruct((m, d), out_dtype),
        compiler_params=_cparams(("parallel",)),
        name="ln_mod",
    )(x, sc, sh)


def _mm_kernel(a_ref, w_ref, o_ref):
    o_ref[...] = _dot(a_ref[...], w_ref[...]).astype(o_ref.dtype)


def _matmul(a, w, out_dtype, tm, tn, name):
    m, k = a.shape
    n = w.shape[1]
    return pl.pallas_call(
        _mm_kernel,
        grid=(n // tn, m // tm),
        in_specs=[pl.BlockSpec((tm, k), lambda j, i: (i, 0)),
                  pl.BlockSpec((k, tn), lambda j, i: (0, j))],
        out_specs=pl.BlockSpec((tm, tn), lambda j, i: (i, j)),
        out_shape=jax.ShapeDtypeStruct((m, n), out_dtype),
        compiler_params=_cparams(("parallel", "parallel")),
        name=name,
    )(a, w)


def _gelu_tanh(x):
    return 0.5 * x * (1.0 + jnp.tanh(math.sqrt(2.0 / math.pi) * (x + 0.044715 * (x * x * x))))


def _compress_kernel(c_ref, pe_ref, w1a_ref, w1b_ref, b1_ref, w2_ref, o_ref, *, nch):
    c = c_ref[0, 0]
    w1a = w1a_ref[0]
    w1b = w1b_ref[0]
    half = CMP_STRIDE * HEAD_DIM
    a = _dot(c, w1a)
    b = _dot(c, w1b)
    b_next = pltpu.roll(b, shift=nch - 1, axis=0)
    pe = pe_ref[0]
    pb = _dot(pe[:, :half], w1a) + _dot(pe[:, half:], w1b)
    hid = _gelu_tanh(a + b_next + pb[0:1, :] + b1_ref[0])
    o_ref[0, 0] = _dot(hid.astype(BF16), w2_ref[0])


def _compress(kv_cmp, pe, w1, b1, w2):
    z, g, s, hd = kv_cmp.shape
    nch = s // CMP_STRIDE
    half = CMP_STRIDE * hd
    chunks = kv_cmp.reshape(z, g, nch, half)
    pe8 = jnp.broadcast_to(pe.reshape(z, 1, CMP_LEN * hd), (z, 8, CMP_LEN * hd)).astype(BF16)
    w1b16 = w1.astype(BF16)
    hidn = w1.shape[-1]
    return pl.pallas_call(
        functools.partial(_compress_kernel, nch=nch),
        grid=(z, g),
        in_specs=[pl.BlockSpec((1, 1, nch, half), lambda zi, gi: (zi, gi, 0, 0)),
                  pl.BlockSpec((1, 8, 2 * half), lambda zi, gi: (zi, 0, 0)),
                  pl.BlockSpec((1, half, hidn), lambda zi, gi: (zi, 0, 0)),
                  pl.BlockSpec((1, half, hidn), lambda zi, gi: (zi, 1, 0)),
                  pl.BlockSpec((1, 1, hidn), lambda zi, gi: (zi, 0, 0)),
                  pl.BlockSpec((1, hidn, hd), lambda zi, gi: (zi, 0, 0))],
        out_specs=pl.BlockSpec((1, 1, nch, hd), lambda zi, gi: (zi, gi, 0, 0)),
        out_shape=jax.ShapeDtypeStruct((z, g, nch, hd), F32),
        compiler_params=_cparams(("parallel", "parallel")),
        name="compress_kv",
    )(chunks, pe8, w1b16, w1b16, b1.reshape(z, 1, hidn), w2.astype(BF16))


def _t5_bucket_np(dist):
    n = np.maximum(dist, 0)
    ratio = np.log(np.maximum(n, T5_MAX_EXACT).astype(np.float64) / T5_MAX_EXACT)
    big = T5_MAX_EXACT + (ratio / math.log(T5_MAX_DIST / T5_MAX_EXACT)
                          * (T5_BUCKETS - T5_MAX_EXACT)).astype(np.int64)
    return np.where(n < T5_MAX_EXACT, n, np.minimum(big, T5_BUCKETS - 1)).astype(np.int32)


def _toeplitz_t(w, n_keys, n_q):
    length = n_keys + n_q - 1
    w_pad = jnp.concatenate([w, jnp.zeros(w.shape[:-1] + (1,), w.dtype)], axis=-1)
    reps = (1,) * (w.ndim - 1) + (n_keys,)
    flat = jnp.tile(w_pad, reps)[..., :n_keys * length]
    return flat.reshape(w.shape[:-1] + (n_keys, length))[..., n_keys - 1:n_keys - 1 + n_q]


def _att_table_t(tbh, n_keys, tq, lo, hi, minus_far):
    d = np.arange(n_keys + tq - 1) - (tq - 1)
    valid = (d >= lo) & (d < hi)
    vals = tbh[:, :, _t5_bucket_np(d)]
    if minus_far:
        vals = vals - tbh[:, :, T5_BUCKETS - 1:]
    w = jnp.where(jnp.asarray(valid), vals * LOG2E, NEG_INF)
    t = _toeplitz_t(w, n_keys, tq)
    g, hpg = tbh.shape[:2]
    return t.transpose(0, 2, 1, 3).reshape(g, n_keys, hpg * tq)


def _nsa_bias_tables(t5_table):
    tbh = t5_table.T.reshape(NSA_GROUPS, NSA_HPG, T5_BUCKETS).astype(F32)
    i = np.arange(CMP_TQ)[:, None]
    c = np.arange(LANES)[None, :]
    dist = i - (CMP_LEN - 1) - CMP_STRIDE * (c - CMP_PAD)
    near_cols = CMP_TQ // CMP_STRIDE + CMP_PAD
    vals = tbh[:, :, _t5_bucket_np(dist)] * LOG2E
    vals = jnp.where(jnp.asarray((dist >= 0) & (c < near_cols))[None, None], vals, NEG_INF)
    b_cmp = vals.reshape(NSA_GROUPS, NSA_HPG * CMP_TQ, LANES)
    far = tbh[:, :, T5_BUCKETS - 1] * LOG2E
    far_cmp = jnp.repeat(far, CMP_TQ, axis=1)[..., None]
    b_sel = _att_table_t(tbh, 2 * ATT_TQ, ATT_TQ, 0, 1 << 30, True)
    b_win = _att_table_t(tbh, 3 * ATT_TQ, ATT_TQ, 0, WINDOW, False)
    hi, mid, lo = _split3_exact(jnp.repeat(far, ATT_TQ, axis=1))
    far3 = jnp.stack([hi, mid, lo], axis=1)
    far3 = jnp.pad(far3, ((0, 0), (0, BF16_ROWS - 3), (0, 0)))
    return b_cmp, far_cmp, b_sel, b_win, far3


def _slc_weight_np(ncp, nslc):
    w = np.zeros((ncp, nslc), np.float32)
    ratio = SLC_LEN // CMP_STRIDE
    for off, val in ((-1, 0.5), (0, 1.0), (1, 1.0), (2, 1.0), (3, 0.5)):
        j = np.arange(nslc)
        n = ratio * j + off
        ok = (n >= 0) & (n + CMP_PAD < ncp)
        w[n[ok] + CMP_PAD, j[ok]] += val
    return w


def _nsa_cmp_kernel(q_ref, kc_ref, vc_ref, bnear_ref, bfar_ref, w_ref, oc_ref, sel_ref, *, ncp, nslc, n_valid_cmp):
    tq = CMP_TQ
    rows = NSA_HPG * tq
    qb = pl.program_id(1)
    cpb = tq // CMP_STRIDE
    q2 = q_ref[0].reshape(rows, HEAD_DIM)
    kc = kc_ref[0].astype(BF16)
    vc = vc_ref[0].astype(BF16)
    r0 = pl.multiple_of(cpb * qb, 8)
    r = lax.broadcasted_iota(jnp.int32, (1, ncp), 1)
    s_far = _dot_nt(q2, kc) + bfar_ref[0]
    s_far = jnp.where((r >= CMP_PAD) & (r < r0), s_far, NEG_INF)
    kn = kc_ref[0, pl.ds(r0, LANES), :].astype(BF16)
    vn = vc_ref[0, pl.ds(r0, LANES), :].astype(BF16)
    rn = r0 + lax.broadcasted_iota(jnp.int32, (1, LANES), 1)
    s_near = _dot_nt(q2, kn) + bnear_ref[0]
    s_near = jnp.where((rn >= CMP_PAD) & (rn < n_valid_cmp + CMP_PAD), s_near, NEG_INF)
    m = jnp.maximum(jnp.max(s_far, axis=-1, keepdims=True), jnp.max(s_near, axis=-1, keepdims=True))
    e_far = jnp.exp2(s_far - m)
    e_near = jnp.exp2(s_near - m)
    l = jnp.sum(e_far, axis=-1, keepdims=True) + jnp.sum(e_near, axis=-1, keepdims=True)
    inv = jnp.where(m > M_INIT, 1.0 / l, 0.0)
    p_far = e_far * inv
    p_near = e_near * inv
    o = _dot(p_far.astype(BF16), vc) + _dot(p_near.astype(BF16), vn)
    oc_ref[0] = o.reshape(NSA_HPG, tq, HEAD_DIM)
    imp_far = jnp.sum(p_far.reshape(NSA_HPG, tq, ncp), axis=0)
    imp_near = jnp.sum(p_near.reshape(NSA_HPG, tq, LANES), axis=0)
    w_all = w_ref[...].astype(BF16)
    w_near = w_ref[pl.ds(r0, LANES), :].astype(BF16)
    p_slc = _dot3(imp_far, w_all) + _dot3(imp_near, w_near)
    blk = lax.broadcasted_iota(jnp.int32, (1, nslc), 1)
    t = qb * tq + lax.broadcasted_iota(jnp.int32, (tq, 1), 0)
    cur = t // SLC_LEN
    forced = (blk == 0) | (blk == cur) | (blk == cur - 1)
    score = jnp.where(forced, FORCE_SCORE, jnp.where(blk <= cur, p_slc, -1.0))
    blk_f = blk.astype(F32)
    sel = jnp.zeros((tq, nslc), F32)
    for _ in range(min(SLC_TOPK, nslc)):
        mx = jnp.max(score, axis=-1, keepdims=True)
        first = jnp.min(jnp.where(score == mx, blk_f, float(nslc)), axis=-1, keepdims=True)
        hit = blk_f == first
        sel = jnp.where(hit, 1.0, sel)
        score = jnp.where(hit, -2.0, score)
    sel_ref[0] = sel.astype(BF16)


def _nsa_compress_select(q, kc_pad, vc_pad, b_cmp, far_cmp, w_slc, n_valid_cmp):
    g, hpg, s, hd = q.shape
    ncp = kc_pad.shape[1]
    nslc = s // SLC_LEN
    tq = CMP_TQ
    rows = hpg * tq
    return pl.pallas_call(
        functools.partial(_nsa_cmp_kernel, ncp=ncp, nslc=nslc, n_valid_cmp=n_valid_cmp),
        grid=(g, s // tq),
        in_specs=[pl.BlockSpec((1, hpg, tq, hd), lambda gi, qi: (gi, 0, qi, 0)),
                  pl.BlockSpec((1, ncp, hd), lambda gi, qi: (gi, 0, 0)),
                  pl.BlockSpec((1, ncp, hd), lambda gi, qi: (gi, 0, 0)),
                  pl.BlockSpec((1, rows, LANES), lambda gi, qi: (gi, 0, 0)),
                  pl.BlockSpec((1, rows, 1), lambda gi, qi: (gi, 0, 0)),
                  pl.BlockSpec((ncp, nslc), lambda gi, qi: (0, 0))],
        out_specs=[pl.BlockSpec((1, hpg, tq, hd), lambda gi, qi: (gi, 0, qi, 0)),
                   pl.BlockSpec((1, tq, nslc), lambda gi, qi: (gi, qi, 0))],
        out_shape=[jax.ShapeDtypeStruct((g, hpg, s, hd), F32),
                   jax.ShapeDtypeStruct((g, s, nslc), BF16)],
        compiler_params=_cparams(("parallel", "parallel")),
        name="nsa_compress_select",
    )(q, kc_pad, vc_pad, b_cmp, far_cmp, w_slc)


def _flash_init_t(m_ref, acc_ref):
    m_ref[...] = jnp.full(m_ref.shape, M_INIT, F32)
    acc_ref[...] = jnp.zeros(acc_ref.shape, F32)


def _flash_step_t(s, vt_tile, m_ref, acc_ref):
    m_old = m_ref[...]
    m_new = jnp.maximum(m_old, jnp.max(s, axis=0, keepdims=True))
    p = jnp.exp2(s - m_new).astype(BF16)
    acc_ref[...] = jnp.exp2(m_old - m_new) * acc_ref[...] + _dot(vt_tile, p)
    m_ref[...] = m_new


def _flash_result_t(acc_ref):
    acc = acc_ref[...]
    return acc[0:HEAD_DIM, :] / acc[HEAD_DIM:HEAD_DIM + 1, :]


def _nsa_att_kernel(qt_ref, ks_ref, vst_ref, kw_ref, vwt_ref, selt_ref, far3_ref, bsel_ref, bwin_ref,
                    oct_ref, gate_ref, o_ref, rhs_s, rhs_w, mask_t, ms, accs, mw, accw):
    tq = ATT_TQ
    qb = pl.program_id(1)
    qt = qt_ref[0, 0]
    rhs_s[...] = jnp.zeros(rhs_s.shape, BF16)
    rhs_s[0:HEAD_DIM, :] = qt
    rhs_s[ROW_BIAS:ROW_BIAS + BF16_ROWS, :] = far3_ref[0]
    rhs_w[...] = jnp.zeros(rhs_w.shape, BF16)
    rhs_w[0:HEAD_DIM, :] = qt
    madd = ((selt_ref[0].astype(F32) - 1.0) * (-NEG_INF)).astype(BF16)
    mask_t[...] = jnp.concatenate([madd] * NSA_HPG, axis=1)
    _flash_init_t(ms, accs)
    _flash_init_t(mw, accw)
    blocks_per_tile = tq // SLC_LEN

    def sel_tile(kt, table):
        k0 = pl.multiple_of(kt * tq, tq)
        chunk = pl.multiple_of((kt * blocks_per_tile) // BF16_ROWS * BF16_ROWS, BF16_ROWS)
        rhs_s[ROW_MASK:ROW_MASK + BF16_ROWS, :] = mask_t[pl.ds(chunk, BF16_ROWS), :]
        s = _dot(ks_ref[0, pl.ds(k0, tq), :], rhs_s[...])
        if table is not None:
            s = s + table
        _flash_step_t(s, vst_ref[0, :, pl.ds(k0, tq)], ms, accs)

    def far_body(kt, carry):
        sel_tile(kt, None)
        return carry

    lax.fori_loop(0, jnp.maximum(qb - 1, 0), far_body, 0)

    @pl.when(qb >= 1)
    def _():
        sel_tile(qb - 1, bsel_ref[0, 0:tq, :])

    sel_tile(qb, bsel_ref[0, tq:2 * tq, :])

    def win_tile(kt, table):
        k0 = pl.multiple_of(kt * tq, tq)
        s = _dot(kw_ref[0, pl.ds(k0, tq), :], rhs_w[...]) + table
        _flash_step_t(s, vwt_ref[0, :, pl.ds(k0, tq)], mw, accw)

    @pl.when(qb >= 2)
    def _():
        win_tile(qb - 2, bwin_ref[0, 0:tq, :])

    @pl.when(qb >= 1)
    def _():
        win_tile(qb - 1, bwin_ref[0, tq:2 * tq, :])

    win_tile(qb, bwin_ref[0, 2 * tq:3 * tq, :])

    gt = _sigmoid(gate_ref[0, 0])
    out = gt[0:1, :] * oct_ref[0, 0] + gt[1:2, :] * _flash_result_t(accs) + gt[2:3, :] * _flash_result_t(accw)
    o_ref[0, 0] = out.astype(o_ref.dtype)


def _nsa_attention(qt, ks_aug, vst_aug, kw_aug, vwt_aug, selt, far3, b_sel, b_win, oct_, gates_t):
    g, nq, hd, lanes = qt.shape
    s = ks_aug.shape[1]
    nslc = selt.shape[1]
    tq = ATT_TQ
    resident = pl.Buffered(1)
    per_q = lambda gi, qi: (gi, qi, 0, 0)
    per_g = lambda gi, qi: (gi, 0, 0)
    return pl.pallas_call(
        _nsa_att_kernel,
        grid=(g, nq),
        in_specs=[pl.BlockSpec((1, 1, hd, lanes), per_q),
                  pl.BlockSpec((1, s, AUG_K), per_g, pipeline_mode=resident),
                  pl.BlockSpec((1, AUG_V, s), per_g, pipeline_mode=resident),
                  pl.BlockSpec((1, s, AUG_K), per_g, pipeline_mode=resident),
                  pl.BlockSpec((1, AUG_V, s), per_g, pipeline_mode=resident),
                  pl.BlockSpec((1, nslc, tq), lambda gi, qi: (gi, 0, qi)),
                  pl.BlockSpec((1, BF16_ROWS, lanes), per_g),
                  pl.BlockSpec((1, 2 * tq, lanes), per_g, pipeline_mode=resident),
                  pl.BlockSpec((1, 3 * tq, lanes), per_g, pipeline_mode=resident),
                  pl.BlockSpec((1, 1, hd, lanes), per_q),
                  pl.BlockSpec((1, 1, 8, lanes), per_q)],
        out_specs=pl.BlockSpec((1, 1, hd, lanes), per_q),
        out_shape=jax.ShapeDtypeStruct((g, nq, hd, lanes), BF16),
        scratch_shapes=[pltpu.VMEM((AUG_K, lanes), BF16), pltpu.VMEM((AUG_K, lanes), BF16),
                        pltpu.VMEM((nslc, lanes), BF16),
                        pltpu.VMEM((1, lanes), F32), pltpu.VMEM((AUG_V, lanes), F32),
                        pltpu.VMEM((1, lanes), F32), pltpu.VMEM((AUG_V, lanes), F32)],
        compiler_params=_cparams(("arbitrary", "arbitrary")),
        name="nsa_select_window",
    )(qt, ks_aug, vst_aug, kw_aug, vwt_aug, selt, far3, b_sel, b_win, oct_, gates_t)


def _decay_kernel(z_ref, b_ref, o_ref, carry_ref, *, tb):
    @pl.when(pl.program_id(0) == 0)
    def _():
        carry_ref[...] = jnp.zeros(carry_ref.shape, F32)

    z = z_ref[...] + b_ref[...]
    log_f = jnp.minimum(z, 0.0) - jnp.log1p(jnp.exp(-jnp.abs(z)))
    r = lax.broadcasted_iota(jnp.int32, (tb, tb), 0)
    c = lax.broadcasted_iota(jnp.int32, (tb, tb), 1)
    tri = jnp.where(r >= c, 1.0, 0.0).astype(BF16)
    run = _dot3_rhs(tri, log_f) + carry_ref[...]
    o_ref[...] = run
    carry_ref[...] = run[tb - 1:tb, :]


def _decay_cumsum(z, bias, tb=512):
    s, n = z.shape
    return pl.pallas_call(
        functools.partial(_decay_kernel, tb=tb),
        grid=(s // tb,),
        in_specs=[pl.BlockSpec((tb, n), lambda i: (i, 0)),
                  pl.BlockSpec((1, n), lambda i: (0, 0))],
        out_specs=pl.BlockSpec((tb, n), lambda i: (i, 0)),
        out_shape=jax.ShapeDtypeStruct((s, n), F32),
        scratch_shapes=[pltpu.VMEM((1, n), F32)],
        compiler_params=_cparams(("arbitrary",)),
        name="decay_cumsum",
    )(z, bias)


def _fox_kernel(qt_ref, k_ref, vt_ref, o_ref, m_ref, acc_ref):
    tq = FOX_TQ
    tk = FOX_TK
    qb = pl.program_id(1)
    rhs = qt_ref[0]
    _flash_init_t(m_ref, acc_ref)

    def tile(kt, diagonal):
        k0 = pl.multiple_of(kt * tk, tk)
        s = _dot(k_ref[0, pl.ds(k0, tk), :], rhs)
        if diagonal:
            key = k0 + lax.broadcasted_iota(jnp.int32, (tk, tq), 0)
            qry = qb * tq + lax.broadcasted_iota(jnp.int32, (tk, tq), 1)
            s = jnp.where(key <= qry, s, NEG_INF)
        _flash_step_t(s, vt_ref[0, :, pl.ds(k0, tk)], m_ref, acc_ref)

    def far_body(kt, carry):
        tile(kt, False)
        return carry

    n_far = qb * (tq // tk)
    lax.fori_loop(0, n_far, far_body, 0)
    for j in range(tq // tk):
        tile(n_far + j, True)
    o_ref[0] = _flash_result_t(acc_ref).astype(o_ref.dtype)


def _fox_attention(qt_aug, k_aug, vt_aug):
    h, _, s = qt_aug.shape
    tq = FOX_TQ
    resident = pl.Buffered(1)
    return pl.pallas_call(
        _fox_kernel,
        grid=(h, s // tq),
        in_specs=[pl.BlockSpec((1, AUG_K, tq), lambda hi, qi: (hi, 0, qi)),
                  pl.BlockSpec((1, s, AUG_K), lambda hi, qi: (hi, 0, 0), pipeline_mode=resident),
                  pl.BlockSpec((1, AUG_V, s), lambda hi, qi: (hi, 0, 0), pipeline_mode=resident)],
        out_specs=pl.BlockSpec((1, HEAD_DIM, tq), lambda hi, qi: (hi, 0, qi)),
        out_shape=jax.ShapeDtypeStruct((h, HEAD_DIM, s), BF16),
        scratch_shapes=[pltpu.VMEM((1, tq), F32), pltpu.VMEM((AUG_V, tq), F32)],
        compiler_params=_cparams(("arbitrary", "arbitrary")),
        name="fox_attention",
    )(qt_aug, k_aug, vt_aug)


def _merge_kernel(on_ref, of_ref, mg_ref, x_ref, wn_ref, wf_ref, wo_ref, g1_ref, lg_ref, lb_ref, o_ref, *, alpha):
    d = x_ref.shape[-1]
    a = _dot(on_ref[...], wn_ref[...])
    b = _dot(of_ref[...], wf_ref[...])
    gm = _sigmoid(mg_ref[...].astype(F32))
    merged = gm[:, 0:d] * a + gm[:, d:2 * d] * b
    y = _dot(merged.astype(BF16), wo_ref[...])
    z = alpha * x_ref[...] + (1.0 + g1_ref[...]) * y
    o_ref[...] = _layer_norm(z) * lg_ref[...] + lb_ref[...]


def _merge_project(o_nsa, o_fox, merge, x, wn, wf, wo, g1, ln_g, ln_b, alpha, tm=256):
    m, d = x.shape
    w = o_nsa.shape[1]
    resident = pl.Buffered(1)
    row = lambda i: (i, 0)
    fixed = lambda i: (0, 0)
    return pl.pallas_call(
        functools.partial(_merge_kernel, alpha=alpha),
        grid=(m // tm,),
        in_specs=[pl.BlockSpec((tm, w), row), pl.BlockSpec((tm, w), row),
                  pl.BlockSpec((tm, 2 * d), row), pl.BlockSpec((tm, d), row),
                  pl.BlockSpec((w, d), fixed, pipeline_mode=resident),
                  pl.BlockSpec((w, d), fixed, pipeline_mode=resident),
                  pl.BlockSpec((d, d), fixed, pipeline_mode=resident),
                  pl.BlockSpec((1, d), fixed), pl.BlockSpec((1, d), fixed), pl.BlockSpec((1, d), fixed)],
        out_specs=pl.BlockSpec((tm, d), row),
        out_shape=jax.ShapeDtypeStruct((m, d), F32),
        compiler_params=_cparams(("parallel",)),
        name="merge_project_ln",
    )(o_nsa, o_fox, merge, x, wn, wf, wo, g1, ln_g, ln_b)


def _router_kernel(x_ref, sc_ref, sh_ref, w_ref, b_ref, u_ref, r_ref):
    u = _layer_norm(x_ref[...]) * (1.0 + sc_ref[...]) + sh_ref[...]
    u_ref[...] = u
    logits = jnp.dot(u, w_ref[...], precision=HIGHEST, preferred_element_type=F32) + b_ref[...]
    lane = lax.broadcasted_iota(jnp.int32, (1, LANES), 1).astype(F32)
    none = float(LANES)
    is_g = lane < N_GROUPS
    lg = jnp.where(is_g, logits, NEG_INF)
    eg = jnp.exp(lg - jnp.max(lg, axis=-1, keepdims=True))
    pg = eg / jnp.sum(eg, axis=-1, keepdims=True)
    p_grp = jnp.max(pg, axis=-1, keepdims=True)
    grp = jnp.min(jnp.where(pg == p_grp, lane, none), axis=-1, keepdims=True)
    lo = N_GROUPS + grp * EXPERTS_PER_GROUP
    is_e = (lane >= lo) & (lane < lo + EXPERTS_PER_GROUP)
    le = jnp.where(is_e, logits, NEG_INF)
    ee = jnp.exp(le - jnp.max(le, axis=-1, keepdims=True))
    pe = jnp.where(is_e, ee / jnp.sum(ee, axis=-1, keepdims=True), -1.0)
    p1 = jnp.max(pe, axis=-1, keepdims=True)
    i1 = jnp.min(jnp.where(pe == p1, lane, none), axis=-1, keepdims=True)
    pe2 = jnp.where(lane == i1, -1.0, pe)
    p2 = jnp.max(pe2, axis=-1, keepdims=True)
    i2 = jnp.min(jnp.where(pe2 == p2, lane, none), axis=-1, keepdims=True)
    den = p1 + p2
    r_ref[...] = jnp.where(lane == 0, i1 - N_GROUPS,
                           jnp.where(lane == 1, i2 - N_GROUPS,
                                     jnp.where(lane == 2, p_grp * p1 / den,
                                               jnp.where(lane == 3, p_grp * p2 / den, 0.0))))


def _router(x1, sc, sh, w_r, b_r, tm=256):
    m, d = x1.shape
    row = lambda i: (i, 0)
    fixed = lambda i: (0, 0)
    return pl.pallas_call(
        _router_kernel,
        grid=(m // tm,),
        in_specs=[pl.BlockSpec((tm, d), row), pl.BlockSpec((1, d), fixed), pl.BlockSpec((1, d), fixed),
                  pl.BlockSpec((d, LANES), fixed), pl.BlockSpec((1, LANES), fixed)],
        out_specs=[pl.BlockSpec((tm, d), row), pl.BlockSpec((tm, LANES), row)],
        out_shape=[jax.ShapeDtypeStruct((m, d), F32), jax.ShapeDtypeStruct((m, LANES), F32)],
        compiler_params=_cparams(("parallel",)),
        name="moe_router",
    )(x1, sc, sh, w_r, b_r)


def _moe_kernel(be_ref, nv_ref, tok_ref, dst_ref, rw_ref, u_hbm, wg_ref, wu_ref, wd_ref, out_hbm,
                xbuf, ybuf, wgb, wub, wdb, sem_in, sem_out):
    i = pl.program_id(0)
    nv = nv_ref[i]

    @pl.when(i == 0)
    def _():
        xbuf[...] = jnp.zeros(xbuf.shape, F32)

    def row_in(r, tok):
        return pltpu.make_async_copy(u_hbm.at[pl.ds(tok, 1), :], xbuf.at[pl.ds(r, 1), :], sem_in)

    def row_out(r, dst):
        return pltpu.make_async_copy(ybuf.at[pl.ds(r, 1), :], out_hbm.at[pl.ds(dst, 1), :], sem_out)

    def start_in(r, carry):
        row_in(r, tok_ref[0, 0, r]).start()
        return carry

    lax.fori_loop(0, nv, start_in, 0)

    prev = be_ref[jnp.maximum(i - 1, 0)]

    @pl.when((i == 0) | (be_ref[i] != prev))
    def _():
        wgb[...] = wg_ref[0].astype(BF16)
        wub[...] = wu_ref[0].astype(BF16)
        wdb[...] = wd_ref[0].astype(BF16)

    def wait_in(r, carry):
        row_in(r, 0).wait()
        return carry

    lax.fori_loop(0, nv, wait_in, 0)

    xb = xbuf[...].astype(BF16)
    gate = _dot(xb, wgb[...])
    up = _dot(xb, wub[...])
    hid = (gate * _sigmoid(gate)) * up
    ybuf[...] = _dot(hid.astype(BF16), wdb[...]) * rw_ref[0]

    def start_out(r, carry):
        row_out(r, dst_ref[0, 0, r]).start()
        return carry

    lax.fori_loop(0, nv, start_out, 0)

    def wait_out(r, carry):
        row_out(r, 0).wait()
        return carry

    lax.fori_loop(0, nv, wait_out, 0)


def _moe_experts(u, blk_exp, n_valid, row_tok, row_dst, row_w, w_gate, w_up, w_down):
    t, d = u.shape
    n_blocks = blk_exp.shape[0]
    de = w_gate.shape[-1]
    rb = ROW_BLOCK
    grid_spec = pltpu.PrefetchScalarGridSpec(
        num_scalar_prefetch=2,
        grid=(n_blocks,),
        in_specs=[pl.BlockSpec((1, 1, rb), lambda i, be, nv: (i, 0, 0), memory_space=pltpu.SMEM),
                  pl.BlockSpec((1, 1, rb), lambda i, be, nv: (i, 0, 0), memory_space=pltpu.SMEM),
                  pl.BlockSpec((1, rb, 1), lambda i, be, nv: (i, 0, 0)),
                  pl.BlockSpec(memory_space=pl.ANY),
                  pl.BlockSpec((1, d, de), lambda i, be, nv: (be[i], 0, 0)),
                  pl.BlockSpec((1, d, de), lambda i, be, nv: (be[i], 0, 0)),
                  pl.BlockSpec((1, de, d), lambda i, be, nv: (be[i], 0, 0))],
        out_specs=pl.BlockSpec(memory_space=pl.ANY),
        scratch_shapes=[pltpu.VMEM((rb, d), F32), pltpu.VMEM((rb, d), F32),
                        pltpu.VMEM((d, de), BF16), pltpu.VMEM((d, de), BF16), pltpu.VMEM((de, d), BF16),
                        pltpu.SemaphoreType.DMA(()), pltpu.SemaphoreType.DMA(())],
    )
    return pl.pallas_call(
        _moe_kernel,
        grid_spec=grid_spec,
        out_shape=jax.ShapeDtypeStruct((2 * t, d), F32),
        compiler_params=_cparams(("arbitrary",)),
        name="moe_experts",
    )(blk_exp, n_valid, row_tok.reshape(n_blocks, 1, rb), row_dst.reshape(n_blocks, 1, rb),
      row_w.reshape(n_blocks, rb, 1), u, w_gate, w_up, w_down)


def _moe_dispatch(route, t):
    k = 2
    eid = route[:, 0:k].astype(jnp.int32).reshape(-1)
    wts = route[:, k:2 * k].reshape(-1)
    n_asg = t * k
    n_rows = n_asg + N_EXPERTS * ROW_BLOCK
    n_blocks = n_rows // ROW_BLOCK
    order = jnp.argsort(eid)
    e_sorted = eid[order]
    counts = jnp.bincount(eid, length=N_EXPERTS)
    padded = (counts + ROW_BLOCK - 1) // ROW_BLOCK * ROW_BLOCK
    pad_end = jnp.cumsum(padded)
    pad_start = pad_end - padded
    start = jnp.cumsum(counts) - counts
    dest = pad_start[e_sorted] + jnp.arange(n_asg) - start[e_sorted]
    tok = (order // k).astype(jnp.int32)
    slot = (order % k).astype(jnp.int32)
    row_tok = jnp.zeros((n_rows,), jnp.int32).at[dest].set(tok)
    row_dst = jnp.zeros((n_rows,), jnp.int32).at[dest].set(slot * t + tok)
    row_w = jnp.zeros((n_rows,), F32).at[dest].set(wts[order])
    blk_start = jnp.arange(n_blocks) * ROW_BLOCK
    blk_exp = jnp.minimum(jnp.searchsorted(pad_end, blk_start, side='right'), N_EXPERTS - 1).astype(jnp.int32)
    n_valid = jnp.clip(pad_start[blk_exp] + counts[blk_exp] - blk_start, 0, ROW_BLOCK).astype(jnp.int32)
    return blk_exp, n_valid, row_tok, row_dst, row_w


def _final_kernel(x_ref, y0_ref, y1_ref, g2_ref, lg_ref, lb_ref, o_ref, *, alpha):
    z = alpha * x_ref[...] + (1.0 + g2_ref[...]) * (y0_ref[...] + y1_ref[...])
    o_ref[...] = _layer_norm(z) * lg_ref[...] + lb_ref[...]


def _final_ln(x1, y2, g2, ln_g, ln_b, alpha, tm=512):
    m, d = x1.shape
    nb = m // tm
    fixed = lambda i: (0, 0)
    return pl.pallas_call(
        functools.partial(_final_kernel, alpha=alpha),
        grid=(nb,),
        in_specs=[pl.BlockSpec((tm, d), lambda i: (i, 0)),
                  pl.BlockSpec((tm, d), lambda i: (i, 0)),
                  pl.BlockSpec((tm, d), lambda i: (i + nb, 0)),
                  pl.BlockSpec((1, d), fixed), pl.BlockSpec((1, d), fixed), pl.BlockSpec((1, d), fixed)],
        out_specs=pl.BlockSpec((tm, d), lambda i: (i, 0)),
        out_shape=jax.ShapeDtypeStruct((m, d), F32),
        compiler_params=_cparams(("parallel",)),
        name="final_ln",
    )(x1, y2, y2, g2, ln_g, ln_b)


def _to_lane_blocks(a, tq):
    g, hpg, s, c = a.shape
    return a.reshape(g, hpg, s // tq, tq, c).transpose(0, 2, 4, 1, 3).reshape(g, s // tq, c, hpg * tq)


def _layer(x2d, c, w_ada, b_ada, w_in, b_fgt, t5_table, cmp_pe, cmp_w1, cmp_b1, cmp_w2,
           w_br_nsa, w_br_fox, w_o, ln1_g, ln1_b, w_rg, b_rg, w_re, b_re,
           w_gate, w_up, w_down, ln2_g, ln2_b, alpha):
    s, d = x2d.shape
    hd = HEAD_DIM
    g = NSA_GROUPS
    mod = _ada_mod(c, w_ada, b_ada)
    sh1, sc1, g1, sh2, sc2, g2 = [mod[:, i * d:(i + 1) * d] for i in range(6)]

    c_q = NSA_HEADS * hd
    c_kv = 6 * g * hd
    c_gate = 3 * NSA_HEADS
    c_fox = 3 * FOX_HEADS * hd
    off_kv = c_q
    off_gate = off_kv + c_kv
    off_fox = off_gate + c_gate
    off_fgt = off_fox + c_fox
    off_merge = off_fgt + FOX_HEADS
    qscale = hd ** -0.5 * LOG2E
    w_attn = jnp.concatenate([w_in[:, 0:off_kv] * qscale, w_in[:, off_kv:off_gate],
                              w_in[:, off_fox:off_fox + FOX_HEADS * hd] * qscale,
                              w_in[:, off_fox + FOX_HEADS * hd:off_fgt]], axis=1).astype(BF16)
    n_small = c_gate + FOX_HEADS
    w_small = jnp.concatenate([w_in[:, off_gate:off_fox], w_in[:, off_fgt:off_merge],
                               jnp.zeros((d, LANES - n_small), F32)], axis=1).astype(BF16)
    w_merge = w_in[:, off_merge:].astype(BF16)

    u = _ln_mod(x2d, sc1, sh1, BF16)
    n_attn = w_attn.shape[1]
    proj = _matmul(u, w_attn, BF16, 512, n_attn // 2, "in_proj_attn")
    small = _matmul(u, w_small, F32, 512, LANES, "in_proj_small")
    merge = _matmul(u, w_merge, BF16, 512, 1024, "in_proj_merge")

    q_nsa = proj[:, 0:c_q].reshape(s, g, NSA_HPG, hd).transpose(1, 2, 0, 3)
    kv = proj[:, c_q:c_q + c_kv].reshape(s, 6, g, hd).transpose(1, 2, 0, 3)
    fox = proj[:, c_q + c_kv:].reshape(s, 3, FOX_HEADS, hd).transpose(1, 2, 0, 3)

    kv_cmp = _compress(kv[0:2], cmp_pe, cmp_w1, cmp_b1, cmp_w2)
    nch = s // CMP_STRIDE
    ncp = nch + LANES
    nslc = s // SLC_LEN
    kv_cmp_pad = jnp.pad(kv_cmp, ((0, 0), (0, 0), (CMP_PAD, ncp - nch - CMP_PAD), (0, 0)))
    b_cmp, far_cmp, b_sel, b_win, far3 = _nsa_bias_tables(t5_table)
    w_slc = jnp.asarray(_slc_weight_np(ncp, nslc))
    o_c, sel = _nsa_compress_select(q_nsa, kv_cmp_pad[0], kv_cmp_pad[1], b_cmp, far_cmp, w_slc, nch - 1)

    def ones(*shape):
        return jnp.ones(shape, BF16)

    def zeros(*shape):
        return jnp.zeros(shape, BF16)

    blk_onehot = jnp.asarray((np.arange(s)[:, None] // SLC_LEN % BF16_ROWS == np.arange(BF16_ROWS)[None, :])
                             .astype(np.float32), BF16)
    ks_aug = jnp.concatenate([kv[2], jnp.broadcast_to(blk_onehot, (g, s, BF16_ROWS)), ones(g, s, 3),
                              zeros(g, s, AUG_K - ROW_BIAS - 3)], axis=-1)
    kw_aug = jnp.concatenate([kv[4], zeros(g, s, AUG_K - hd)], axis=-1)
    vst_aug = jnp.concatenate([kv[3].transpose(0, 2, 1), ones(g, 8, s), zeros(g, 8, s)], axis=1)
    vwt_aug = jnp.concatenate([kv[5].transpose(0, 2, 1), ones(g, 8, s), zeros(g, 8, s)], axis=1)
    qt = _to_lane_blocks(q_nsa, ATT_TQ)
    oct_ = _to_lane_blocks(o_c, ATT_TQ)
    gates = small[:, 0:c_gate].reshape(s, g, NSA_HPG, 3).transpose(1, 2, 0, 3)
    gates_t = jnp.pad(_to_lane_blocks(gates, ATT_TQ), ((0, 0), (0, 0), (0, 5), (0, 0)))
    o_nsa_t = _nsa_attention(qt, ks_aug, vst_aug, kw_aug, vwt_aug, sel.transpose(0, 2, 1), far3,
                             b_sel, b_win, oct_, gates_t)
    nq = s // ATT_TQ
    o_nsa = o_nsa_t.reshape(g, nq, hd, NSA_HPG, ATT_TQ).transpose(1, 4, 0, 3, 2).reshape(s, MIX_W)

    fgt_bias = jnp.concatenate([jnp.zeros((c_gate,), F32), b_fgt, jnp.zeros((LANES - n_small,), F32)])[None, :]
    decay = _decay_cumsum(small, fgt_bias)[:, c_gate:c_gate + FOX_HEADS]
    nh = FOX_HEADS
    d_hi, d_mid, d_lo = _split3_exact(-decay.T * LOG2E)
    fox_k_aug = jnp.concatenate([fox[1], jnp.stack([d_hi, d_mid, d_lo], axis=-1), zeros(nh, s, AUG_K - hd - 3)],
                                axis=-1)
    fox_qt_aug = jnp.concatenate([fox[0].transpose(0, 2, 1), ones(nh, 3, s), zeros(nh, AUG_K - hd - 3, s)], axis=1)
    fox_vt_aug = jnp.concatenate([fox[2].transpose(0, 2, 1), ones(nh, 8, s), zeros(nh, 8, s)], axis=1)
    o_fox = _fox_attention(fox_qt_aug, fox_k_aug, fox_vt_aug)
    o_fox = o_fox.transpose(2, 0, 1).reshape(s, MIX_W)

    x1 = _merge_project(o_nsa, o_fox, merge, x2d, w_br_nsa.astype(BF16), w_br_fox.astype(BF16),
                        w_o.astype(BF16), g1, ln1_g[None, :], ln1_b[None, :], alpha)

    n_r = N_GROUPS + N_EXPERTS
    w_r = jnp.concatenate([w_rg, w_re.reshape(d, N_EXPERTS), jnp.zeros((d, LANES - n_r), F32)], axis=1)
    b_r = jnp.concatenate([b_rg, b_re.reshape(N_EXPERTS), jnp.zeros((LANES - n_r,), F32)])[None, :]
    u2, route = _router(x1, sc2, sh2, w_r, b_r)
    blk_exp, n_valid, row_tok, row_dst, row_w = _moe_dispatch(route, s)
    y2 = _moe_experts(u2, blk_exp, n_valid, row_tok, row_dst, row_w, w_gate, w_up, w_down)
    return _final_ln(x1, y2, g2, ln2_g[None, :], ln2_b[None, :], alpha)


def kernel(x, c, w_ada, b_ada, w_in, b_fgt, t5_table, cmp_pe, cmp_w1, cmp_b1, cmp_w2, w_br_nsa, w_br_fox, w_o,
           ln1_g, ln1_b, w_rg, b_rg, w_re, b_re, w_gate, w_up, w_down, ln2_g, ln2_b):
    b, s, d = x.shape
    depth = w_ada.shape[0]
    assert b == 1
    alpha = (2 * depth) ** 0.25
    h = x[0]
    for l in range(depth):
        h = _layer(h, c, w_ada[l], b_ada[l], w_in[l], b_fgt[l], t5_table, cmp_pe[l], cmp_w1[l], cmp_b1[l],
                   cmp_w2[l], w_br_nsa[l], w_br_fox[l], w_o[l], ln1_g[l], ln1_b[l], w_rg[l], b_rg[l],
                   w_re[l], b_re[l], w_gate[l], w_up[l], w_down[l], ln2_g[l], ln2_b[l], alpha)
    return h[None]
```

```python
import functools
import math

import numpy as np
import jax
import jax.numpy as jnp
from jax import lax
from jax.experimental import pallas as pl
from jax.experimental.pallas import tpu as pltpu

F32 = jnp.float32
BF16 = jnp.bfloat16
HIGHEST = lax.Precision.HIGHEST
LOG2E = math.log2(math.e)

HEAD_DIM = 64
NSA_HEADS = 8
NSA_GROUPS = 2
NSA_HPG = NSA_HEADS // NSA_GROUPS
FOX_HEADS = 8
MIX_W = NSA_HEADS * HEAD_DIM
CMP_LEN = 32
CMP_STRIDE = 16
SLC_LEN = 64
SLC_TOPK = 16
WINDOW = 512
T5_BUCKETS = 32
T5_MAX_EXACT = 16
T5_MAX_DIST = 128
N_GROUPS = 8
EXPERTS_PER_GROUP = 8
N_EXPERTS = N_GROUPS * EXPERTS_PER_GROUP
ROW_BLOCK = 128
LN_EPS = 1e-5
NEG_INF = -1e30
M_INIT = -1e29
FORCE_SCORE = 1e4

LANES = 128
BF16_ROWS = 16
CMP_PAD = 8
CMP_TQ = 128
ATT_TQ = 256
FOX_TQ = 1024
FOX_TK = 512
AUG_K = 128
AUG_V = HEAD_DIM + 16
ROW_MASK = HEAD_DIM
ROW_BIAS = HEAD_DIM + 16
VMEM_LIMIT = 56 * 1024 * 1024


def _cparams(sem, vmem=VMEM_LIMIT):
    return pltpu.CompilerParams(dimension_semantics=sem, vmem_limit_bytes=vmem)


def _sigmoid(x):
    return 1.0 / (1.0 + jnp.exp(-x))


def _layer_norm(x):
    mu = jnp.mean(x, axis=-1, keepdims=True)
    xc = x - mu
    var = jnp.mean(xc * xc, axis=-1, keepdims=True)
    return xc * lax.rsqrt(var + LN_EPS)


def _split3(x):
    hi = x.astype(BF16)
    r1 = x - hi.astype(F32)
    mid = r1.astype(BF16)
    lo = (r1 - mid.astype(F32)).astype(BF16)
    return hi, mid, lo


def _split3_exact(x):
    def trunc(v):
        bits = lax.bitcast_convert_type(v, jnp.uint32) & jnp.uint32(0xFFFF0000)
        return lax.bitcast_convert_type(bits, F32)
    hi = trunc(x)
    r1 = x - hi
    mid = trunc(r1)
    lo = r1 - mid
    return hi.astype(BF16), mid.astype(BF16), lo.astype(BF16)


def _dot(a, b):
    return jnp.dot(a, b, preferred_element_type=F32)


def _dot_nt(a, b):
    return lax.dot_general(a, b, (((1,), (1,)), ((), ())), preferred_element_type=F32)


def _dot3(x, w_bf16):
    hi, mid, lo = _split3(x)
    return _dot(hi, w_bf16) + _dot(mid, w_bf16) + _dot(lo, w_bf16)


def _dot3_rhs(w_bf16, x):
    hi, mid, lo = _split3(x)
    return _dot(w_bf16, hi) + _dot(w_bf16, mid) + _dot(w_bf16, lo)


def _ada_kernel(c_ref, w_ref, b_ref, o_ref):
    c = c_ref[...]
    a = c * _sigmoid(c)
    o_ref[...] = jnp.dot(a, w_ref[...], precision=HIGHEST, preferred_element_type=F32) + b_ref[...]


def _ada_mod(c, w, b):
    d, n = w.shape
    tn = 1024
    c8 = jnp.broadcast_to(c, (8, d))
    out = pl.pallas_call(
        _ada_kernel,
        grid=(n // tn,),
        in_specs=[pl.BlockSpec((8, d), lambda j: (0, 0)),
                  pl.BlockSpec((d, tn), lambda j: (0, j)),
                  pl.BlockSpec((1, tn), lambda j: (0, j))],
        out_specs=pl.BlockSpec((8, tn), lambda j: (0, j)),
        out_shape=jax.ShapeDtypeStruct((8, n), F32),
        compiler_params=_cparams(("parallel",)),
        name="ada_mod",
    )(c8, w, b.reshape(1, n))
    return out[0:1]


def _lnmod_kernel(x_ref, sc_ref, sh_ref, o_ref):
    y = _layer_norm(x_ref[...])
    o_ref[...] = (y * (1.0 + sc_ref[...]) + sh_ref[...]).astype(o_ref.dtype)


def _ln_mod(x, sc, sh, out_dtype, tm=512):
    m, d = x.shape
    return pl.pallas_call(
        _lnmod_kernel,
        grid=(m // tm,),
        in_specs=[pl.BlockSpec((tm, d), lambda i: (i, 0)),
                  pl.BlockSpec((1, d), lambda i: (0, 0)),
                  pl.BlockSpec((1, d), lambda i: (0, 0))],
        out_specs=pl.BlockSpec((tm, d), lambda i: (i, 0)),
        out_shape=jax.ShapeDtypeStruct((m, d), out_dtype),
        compiler_params=_cparams(("parallel",)),
        name="ln_mod",
    )(x, sc, sh)


def _mm_kernel(a_ref, w_ref, o_ref):
    o_ref[...] = _dot(a_ref[...], w_ref[...]).astype(o_ref.dtype)


def _matmul(a, w, out_dtype, tm, tn, name):
    m, k = a.shape
    n = w.shape[1]
    return pl.pallas_call(
        _mm_kernel,
        grid=(n // tn, m // tm),
        in_specs=[pl.BlockSpec((tm, k), lambda j, i: (i, 0)),
                  pl.BlockSpec((k, tn), lambda j, i: (0, j))],
        out_specs=pl.BlockSpec((tm, tn), lambda j, i: (i, j)),
        out_shape=jax.ShapeDtypeStruct((m, n), out_dtype),
        compiler_params=_cparams(("parallel", "parallel")),
        name=name,
    )(a, w)


def _gelu_tanh(x):
    return 0.5 * x * (1.0 + jnp.tanh(math.sqrt(2.0 / math.pi) * (x + 0.044715 * (x * x * x))))


def _compress_kernel(c_ref, pe_ref, w1a_ref, w1b_ref, b1_ref, w2_ref, o_ref, *, nch):
    c = c_ref[0, 0]
    w1a = w1a_ref[0]
    w1b = w1b_ref[0]
    half = CMP_STRIDE * HEAD_DIM
    a = _dot(c, w1a)
    b = _dot(c, w1b)
    b_next = pltpu.roll(b, shift=nch - 1, axis=0)
    pe = pe_ref[0]
    pb = _dot(pe[:, :half], w1a) + _dot(pe[:, half:], w1b)
    hid = _gelu_tanh(a + b_next + pb[0:1, :] + b1_ref[0])
    o_ref[0, 0] = _dot(hid.astype(BF16), w2_ref[0])


def _compress(kv_cmp, pe, w1, b1, w2):
    z, g, s, hd = kv_cmp.shape
    nch = s // CMP_STRIDE
    half = CMP_STRIDE * hd
    chunks = kv_cmp.reshape(z, g, nch, half)
    pe8 = jnp.broadcast_to(pe.reshape(z, 1, CMP_LEN * hd), (z, 8, CMP_LEN * hd)).astype(BF16)
    w1b16 = w1.astype(BF16)
    hidn = w1.shape[-1]
    return pl.pallas_call(
        functools.partial(_compress_kernel, nch=nch),
        grid=(z, g),
        in_specs=[pl.BlockSpec((1, 1, nch, half), lambda zi, gi: (zi, gi, 0, 0)),
                  pl.BlockSpec((1, 8, 2 * half), lambda zi, gi: (zi, 0, 0)),
                  pl.BlockSpec((1, half, hidn), lambda zi, gi: (zi, 0, 0)),
                  pl.BlockSpec((1, half, hidn), lambda zi, gi: (zi, 1, 0)),
                  pl.BlockSpec((1, 1, hidn), lambda zi, gi: (zi, 0, 0)),
                  pl.BlockSpec((1, hidn, hd), lambda zi, gi: (zi, 0, 0))],
        out_specs=pl.BlockSpec((1, 1, nch, hd), lambda zi, gi: (zi, gi, 0, 0)),
        out_shape=jax.ShapeDtypeStruct((z, g, nch, hd), F32),
        compiler_params=_cparams(("parallel", "parallel")),
        name="compress_kv",
    )(chunks, pe8, w1b16, w1b16, b1.reshape(z, 1, hidn), w2.astype(BF16))


def _t5_bucket_np(dist):
    n = np.maximum(dist, 0)
    ratio = np.log(np.maximum(n, T5_MAX_EXACT).astype(np.float64) / T5_MAX_EXACT)
    big = T5_MAX_EXACT + (ratio / math.log(T5_MAX_DIST / T5_MAX_EXACT)
                          * (T5_BUCKETS - T5_MAX_EXACT)).astype(np.int64)
    return np.where(n < T5_MAX_EXACT, n, np.minimum(big, T5_BUCKETS - 1)).astype(np.int32)


def _toeplitz_t(w, n_keys, n_q):
    length = n_keys + n_q - 1
    w_pad = jnp.concatenate([w, jnp.zeros(w.shape[:-1] + (1,), w.dtype)], axis=-1)
    reps = (1,) * (w.ndim - 1) + (n_keys,)
    flat = jnp.tile(w_pad, reps)[..., :n_keys * length]
    return flat.reshape(w.shape[:-1] + (n_keys, length))[..., n_keys - 1:n_keys - 1 + n_q]


def _att_table_t(tbh, n_keys, tq, lo, hi, minus_far):
    d = np.arange(n_keys + tq - 1) - (tq - 1)
    valid = (d >= lo) & (d < hi)
    vals = tbh[:, :, _t5_bucket_np(d)]
    if minus_far:
        vals = vals - tbh[:, :, T5_BUCKETS - 1:]
    w = jnp.where(jnp.asarray(valid), vals * LOG2E, NEG_INF)
    t = _toeplitz_t(w, n_keys, tq)
    g, hpg = tbh.shape[:2]
    return t.transpose(0, 2, 1, 3).reshape(g, n_keys, hpg * tq)


def _nsa_bias_tables(t5_table):
    tbh = t5_table.T.reshape(NSA_GROUPS, NSA_HPG, T5_BUCKETS).astype(F32)
    i = np.arange(CMP_TQ)[:, None]
    c = np.arange(LANES)[None, :]
    dist = i - (CMP_LEN - 1) - CMP_STRIDE * (c - CMP_PAD)
    near_cols = CMP_TQ // CMP_STRIDE + CMP_PAD
    vals = tbh[:, :, _t5_bucket_np(dist)] * LOG2E
    vals = jnp.where(jnp.asarray((dist >= 0) & (c < near_cols))[None, None], vals, NEG_INF)
    b_cmp = vals.reshape(NSA_GROUPS, NSA_HPG * CMP_TQ, LANES)
    far = tbh[:, :, T5_BUCKETS - 1] * LOG2E
    far_cmp = jnp.repeat(far, CMP_TQ, axis=1)[..., None]
    b_sel = _att_table_t(tbh, 2 * ATT_TQ, ATT_TQ, 0, 1 << 30, True)
    b_sel = jnp.pad(b_sel, ((0, 0), (ATT_TQ, 0), (0, 0)))
    b_win = _att_table_t(tbh, 3 * ATT_TQ, ATT_TQ, 0, WINDOW, False)
    b_win = jnp.pad(b_win, ((0, 0), (0, ATT_TQ), (0, 0)), constant_values=NEG_INF)
    hi, mid, lo = _split3_exact(jnp.repeat(far, ATT_TQ, axis=1))
    far3 = jnp.stack([hi, mid, lo], axis=1)
    far3 = jnp.pad(far3, ((0, 0), (0, BF16_ROWS - 3), (0, 0)))
    return b_cmp, far_cmp, b_sel, b_win, far3


def _slc_weight_np(ncp, nslc):
    w = np.zeros((ncp, nslc), np.float32)
    ratio = SLC_LEN // CMP_STRIDE
    for off, val in ((-1, 0.5), (0, 1.0), (1, 1.0), (2, 1.0), (3, 0.5)):
        j = np.arange(nslc)
        n = ratio * j + off
        ok = (n >= 0) & (n + CMP_PAD < ncp)
        w[n[ok] + CMP_PAD, j[ok]] += val
    return w


def _nsa_cmp_kernel(q_ref, kc_ref, vc_ref, bnear_ref, bfar_ref, w_ref, oc_ref, sel_ref, *, ncp, nslc, n_valid_cmp):
    tq = CMP_TQ
    rows = NSA_HPG * tq
    qb = pl.program_id(1)
    cpb = tq // CMP_STRIDE
    q2 = q_ref[0].reshape(rows, HEAD_DIM)
    kc = kc_ref[0].astype(BF16)
    vc = vc_ref[0].astype(BF16)
    r0 = pl.multiple_of(cpb * qb, 8)
    r = lax.broadcasted_iota(jnp.int32, (1, ncp), 1)
    s_far = _dot_nt(q2, kc) + bfar_ref[0]
    s_far = jnp.where((r >= CMP_PAD) & (r < r0), s_far, NEG_INF)
    kn = kc_ref[0, pl.ds(r0, LANES), :].astype(BF16)
    vn = vc_ref[0, pl.ds(r0, LANES), :].astype(BF16)
    rn = r0 + lax.broadcasted_iota(jnp.int32, (1, LANES), 1)
    s_near = _dot_nt(q2, kn) + bnear_ref[0]
    s_near = jnp.where((rn >= CMP_PAD) & (rn < n_valid_cmp + CMP_PAD), s_near, NEG_INF)
    m = jnp.maximum(jnp.max(s_far, axis=-1, keepdims=True), jnp.max(s_near, axis=-1, keepdims=True))
    e_far = jnp.exp2(s_far - m)
    e_near = jnp.exp2(s_near - m)
    l = jnp.sum(e_far, axis=-1, keepdims=True) + jnp.sum(e_near, axis=-1, keepdims=True)
    inv = jnp.where(m > M_INIT, 1.0 / l, 0.0)
    p_far = e_far * inv
    p_near = e_near * inv
    o = _dot(p_far.astype(BF16), vc) + _dot(p_near.astype(BF16), vn)
    oc_ref[0] = o.reshape(NSA_HPG, tq, HEAD_DIM)
    imp_far = jnp.sum(p_far.reshape(NSA_HPG, tq, ncp), axis=0)
    imp_near = jnp.sum(p_near.reshape(NSA_HPG, tq, LANES), axis=0)
    w_all = w_ref[...].astype(BF16)
    w_near = w_ref[pl.ds(r0, LANES), :].astype(BF16)
    p_slc = _dot3(imp_far, w_all) + _dot3(imp_near, w_near)
    blk = lax.broadcasted_iota(jnp.int32, (1, nslc), 1)
    t = qb * tq + lax.broadcasted_iota(jnp.int32, (tq, 1), 0)
    cur = t // SLC_LEN
    forced = (blk == 0) | (blk == cur) | (blk == cur - 1)
    score = jnp.where(forced, FORCE_SCORE, jnp.where(blk <= cur, p_slc, -1.0))
    blk_f = blk.astype(F32)
    sel = jnp.zeros((tq, nslc), F32)
    for _ in range(min(SLC_TOPK, nslc)):
        mx = jnp.max(score, axis=-1, keepdims=True)
        first = jnp.min(jnp.where(score == mx, blk_f, float(nslc)), axis=-1, keepdims=True)
        hit = blk_f == first
        sel = jnp.where(hit, 1.0, sel)
        score = jnp.where(hit, -2.0, score)
    sel_ref[0] = sel.astype(BF16)


def _nsa_compress_select(q, kc_pad, vc_pad, b_cmp, far_cmp, w_slc, n_valid_cmp):
    g, hpg, s, hd = q.shape
    ncp = kc_pad.shape[1]
    nslc = s // SLC_LEN
    tq = CMP_TQ
    rows = hpg * tq
    return pl.pallas_call(
        functools.partial(_nsa_cmp_kernel, ncp=ncp, nslc=nslc, n_valid_cmp=n_valid_cmp),
        grid=(g, s // tq),
        in_specs=[pl.BlockSpec((1, hpg, tq, hd), lambda gi, qi: (gi, 0, qi, 0)),
                  pl.BlockSpec((1, ncp, hd), lambda gi, qi: (gi, 0, 0)),
                  pl.BlockSpec((1, ncp, hd), lambda gi, qi: (gi, 0, 0)),
                  pl.BlockSpec((1, rows, LANES), lambda gi, qi: (gi, 0, 0)),
                  pl.BlockSpec((1, rows, 1), lambda gi, qi: (gi, 0, 0)),
                  pl.BlockSpec((ncp, nslc), lambda gi, qi: (0, 0))],
        out_specs=[pl.BlockSpec((1, hpg, tq, hd), lambda gi, qi: (gi, 0, qi, 0)),
                   pl.BlockSpec((1, tq, nslc), lambda gi, qi: (gi, qi, 0))],
        out_shape=[jax.ShapeDtypeStruct((g, hpg, s, hd), F32),
                   jax.ShapeDtypeStruct((g, s, nslc), BF16)],
        compiler_params=_cparams(("parallel", "parallel")),
        name="nsa_compress_select",
    )(q, kc_pad, vc_pad, b_cmp, far_cmp, w_slc)


def _flash_init_t(m_ref, acc_ref):
    m_ref[...] = jnp.full(m_ref.shape, M_INIT, F32)
    acc_ref[...] = jnp.zeros(acc_ref.shape, F32)


def _flash_step_t(s, vt_tile, m_ref, acc_ref):
    m_old = m_ref[...]
    m_new = jnp.maximum(m_old, jnp.max(s, axis=0, keepdims=True))
    p = jnp.exp2(s - m_new).astype(BF16)
    acc_ref[...] = jnp.exp2(m_old - m_new) * acc_ref[...] + _dot(vt_tile, p)
    m_ref[...] = m_new


def _flash_result_t(acc_ref):
    acc = acc_ref[...]
    return acc[0:HEAD_DIM, :] / acc[HEAD_DIM:HEAD_DIM + 1, :]


def _nsa_att_kernel(qt_ref, ks_ref, vst_ref, kw_ref, vwt_ref, selt_ref, far3_ref, bsel_ref, bwin_ref,
                    oct_ref, gate_ref, o_ref, rhs_s, rhs_w, mask_t, ms, accs, mw, accw, s_even, s_odd):
    tq = ATT_TQ
    qb = pl.program_id(1)
    qt = qt_ref[0, 0]
    rhs_s[...] = jnp.zeros(rhs_s.shape, BF16)
    rhs_s[0:HEAD_DIM, :] = qt
    rhs_s[ROW_BIAS:ROW_BIAS + BF16_ROWS, :] = far3_ref[0]
    rhs_w[...] = jnp.zeros(rhs_w.shape, BF16)
    rhs_w[0:HEAD_DIM, :] = qt
    madd = ((selt_ref[0].astype(F32) - 1.0) * (-NEG_INF)).astype(BF16)
    mask_t[...] = jnp.concatenate([madd] * NSA_HPG, axis=1)
    _flash_init_t(ms, accs)
    _flash_init_t(mw, accw)
    blocks_per_tile = tq // SLC_LEN

    def sel_scores(kt, s_ref):
        kt = jnp.minimum(kt, qb)
        k0 = pl.multiple_of(kt * tq, tq)
        chunk = pl.multiple_of((kt * blocks_per_tile) // BF16_ROWS * BF16_ROWS, BF16_ROWS)
        rhs_s[ROW_MASK:ROW_MASK + BF16_ROWS, :] = mask_t[pl.ds(chunk, BF16_ROWS), :]
        rel = jnp.clip(kt - qb + 2, 0, 2)
        table = bsel_ref[0, pl.ds(pl.multiple_of(rel * tq, tq), tq), :]
        s_ref[...] = _dot(ks_ref[0, pl.ds(k0, tq), :], rhs_s[...]) + table

    def sel_consume(kt, s_ref):
        k0 = pl.multiple_of(kt * tq, tq)
        _flash_step_t(s_ref[...], vst_ref[0, :, pl.ds(k0, tq)], ms, accs)

    def win_scores(j, s_ref):
        kt = qb - 2 + j
        k0 = pl.multiple_of(jnp.maximum(kt, 0) * tq, tq)
        row = pl.multiple_of(jnp.where(kt >= 0, j, 3) * tq, tq)
        s_ref[...] = _dot(kw_ref[0, pl.ds(k0, tq), :], rhs_w[...]) + bwin_ref[0, pl.ds(row, tq), :]

    def win_consume(j, s_ref):
        k0 = pl.multiple_of(jnp.maximum(qb - 2 + j, 0) * tq, tq)
        _flash_step_t(s_ref[...], vwt_ref[0, :, pl.ds(k0, tq)], mw, accw)

    n_sel = qb + 1
    sel_scores(0, s_even)

    def pair_body(j, carry):
        sel_scores(2 * j + 1, s_odd)
        sel_consume(2 * j, s_even)
        sel_scores(2 * j + 2, s_even)
        sel_consume(2 * j + 1, s_odd)
        return carry

    lax.fori_loop(0, n_sel // 2, pair_body, 0)
    win_scores(0, s_odd)

    @pl.when(n_sel % 2 == 1)
    def _():
        sel_consume(qb, s_even)

    win_scores(1, s_even)
    win_consume(0, s_odd)
    win_scores(2, s_odd)
    win_consume(1, s_even)
    win_consume(2, s_odd)

    gt = _sigmoid(gate_ref[0, 0])
    out = gt[0:1, :] * oct_ref[0, 0] + gt[1:2, :] * _flash_result_t(accs) + gt[2:3, :] * _flash_result_t(accw)
    o_ref[0, 0] = out.astype(o_ref.dtype)


def _nsa_attention(qt, ks_aug, vst_aug, kw_aug, vwt_aug, selt, far3, b_sel, b_win, oct_, gates_t):
    g, nq, hd, lanes = qt.shape
    s = ks_aug.shape[1]
    nslc = selt.shape[1]
    tq = ATT_TQ
    resident = pl.Buffered(1)
    per_q = lambda gi, qi: (gi, qi, 0, 0)
    per_g = lambda gi, qi: (gi, 0, 0)
    return pl.pallas_call(
        _nsa_att_kernel,
        grid=(g, nq),
        in_specs=[pl.BlockSpec((1, 1, hd, lanes), per_q),
                  pl.BlockSpec((1, s, AUG_K), per_g, pipeline_mode=resident),
                  pl.BlockSpec((1, AUG_V, s), per_g, pipeline_mode=resident),
                  pl.BlockSpec((1, s, AUG_K), per_g, pipeline_mode=resident),
                  pl.BlockSpec((1, AUG_V, s), per_g, pipeline_mode=resident),
                  pl.BlockSpec((1, nslc, tq), lambda gi, qi: (gi, 0, qi)),
                  pl.BlockSpec((1, BF16_ROWS, lanes), per_g),
                  pl.BlockSpec((1, 3 * tq, lanes), per_g, pipeline_mode=resident),
                  pl.BlockSpec((1, 4 * tq, lanes), per_g, pipeline_mode=resident),
                  pl.BlockSpec((1, 1, hd, lanes), per_q),
                  pl.BlockSpec((1, 1, 8, lanes), per_q)],
        out_specs=pl.BlockSpec((1, 1, hd, lanes), per_q),
        out_shape=jax.ShapeDtypeStruct((g, nq, hd, lanes), BF16),
        scratch_shapes=[pltpu.VMEM((AUG_K, lanes), BF16), pltpu.VMEM((AUG_K, lanes), BF16),
                        pltpu.VMEM((nslc, lanes), BF16),
                        pltpu.VMEM((1, lanes), F32), pltpu.VMEM((AUG_V, lanes), F32),
                        pltpu.VMEM((1, lanes), F32), pltpu.VMEM((AUG_V, lanes), F32),
                        pltpu.VMEM((tq, lanes), F32), pltpu.VMEM((tq, lanes), F32)],
        compiler_params=_cparams(("arbitrary", "arbitrary")),
        name="nsa_select_window",
    )(qt, ks_aug, vst_aug, kw_aug, vwt_aug, selt, far3, b_sel, b_win, oct_, gates_t)


def _decay_kernel(z_ref, b_ref, o_ref, carry_ref, *, tb):
    @pl.when(pl.program_id(0) == 0)
    def _():
        carry_ref[...] = jnp.zeros(carry_ref.shape, F32)

    z = z_ref[...] + b_ref[...]
    log_f = jnp.minimum(z, 0.0) - jnp.log1p(jnp.exp(-jnp.abs(z)))
    r = lax.broadcasted_iota(jnp.int32, (tb, tb), 0)
    c = lax.broadcasted_iota(jnp.int32, (tb, tb), 1)
    tri = jnp.where(r >= c, 1.0, 0.0).astype(BF16)
    run = _dot3_rhs(tri, log_f) + carry_ref[...]
    o_ref[...] = run
    carry_ref[...] = run[tb - 1:tb, :]


def _decay_cumsum(z, bias, tb=512):
    s, n = z.shape
    return pl.pallas_call(
        functools.partial(_decay_kernel, tb=tb),
        grid=(s // tb,),
        in_specs=[pl.BlockSpec((tb, n), lambda i: (i, 0)),
                  pl.BlockSpec((1, n), lambda i: (0, 0))],
        out_specs=pl.BlockSpec((tb, n), lambda i: (i, 0)),
        out_shape=jax.ShapeDtypeStruct((s, n), F32),
        scratch_shapes=[pltpu.VMEM((1, n), F32)],
        compiler_params=_cparams(("arbitrary",)),
        name="decay_cumsum",
    )(z, bias)


def _fox_kernel(qt_ref, k_ref, vt_ref, o_ref, m_ref, acc_ref, s_even, s_odd):
    tq = FOX_TQ
    tk = FOX_TK
    assert tq == 2 * tk
    qb = pl.program_id(1)
    rhs = qt_ref[0]
    _flash_init_t(m_ref, acc_ref)

    def scores(kt, s_ref):
        k0 = pl.multiple_of(kt * tk, tk)
        s_ref[...] = _dot(k_ref[0, pl.ds(k0, tk), :], rhs)

    def consume(kt, s_ref, diagonal):
        k0 = pl.multiple_of(kt * tk, tk)
        s = s_ref[...]
        if diagonal:
            key = k0 + lax.broadcasted_iota(jnp.int32, (tk, tq), 0)
            qry = qb * tq + lax.broadcasted_iota(jnp.int32, (tk, tq), 1)
            s = jnp.where(key <= qry, s, NEG_INF)
        _flash_step_t(s, vt_ref[0, :, pl.ds(k0, tk)], m_ref, acc_ref)

    scores(0, s_even)

    def pair_body(j, carry):
        scores(2 * j + 1, s_odd)
        consume(2 * j, s_even, False)
        scores(2 * j + 2, s_even)
        consume(2 * j + 1, s_odd, False)
        return carry

    lax.fori_loop(0, qb, pair_body, 0)
    scores(2 * qb + 1, s_odd)
    consume(2 * qb, s_even, True)
    consume(2 * qb + 1, s_odd, True)
    o_ref[0] = _flash_result_t(acc_ref).astype(o_ref.dtype)


def _fox_attention(qt_aug, k_aug, vt_aug):
    h, _, s = qt_aug.shape
    tq = FOX_TQ
    resident = pl.Buffered(1)
    return pl.pallas_call(
        _fox_kernel,
        grid=(h, s // tq),
        in_specs=[pl.BlockSpec((1, AUG_K, tq), lambda hi, qi: (hi, 0, qi)),
                  pl.BlockSpec((1, s, AUG_K), lambda hi, qi: (hi, 0, 0), pipeline_mode=resident),
                  pl.BlockSpec((1, AUG_V, s), lambda hi, qi: (hi, 0, 0), pipeline_mode=resident)],
        out_specs=pl.BlockSpec((1, HEAD_DIM, tq), lambda hi, qi: (hi, 0, qi)),
        out_shape=jax.ShapeDtypeStruct((h, HEAD_DIM, s), BF16),
        scratch_shapes=[pltpu.VMEM((1, tq), F32), pltpu.VMEM((AUG_V, tq), F32),
                        pltpu.VMEM((FOX_TK, tq), F32), pltpu.VMEM((FOX_TK, tq), F32)],
        compiler_params=_cparams(("arbitrary", "arbitrary")),
        name="fox_attention",
    )(qt_aug, k_aug, vt_aug)


def _merge_kernel(on_ref, of_ref, mg_ref, x_ref, wn_ref, wf_ref, wo_ref, g1_ref, lg_ref, lb_ref, o_ref, *, alpha):
    d = x_ref.shape[-1]
    a = _dot(on_ref[...], wn_ref[...])
    b = _dot(of_ref[...], wf_ref[...])
    gm = _sigmoid(mg_ref[...].astype(F32))
    merged = gm[:, 0:d] * a + gm[:, d:2 * d] * b
    y = _dot(merged.astype(BF16), wo_ref[...])
    z = alpha * x_ref[...] + (1.0 + g1_ref[...]) * y
    o_ref[...] = _layer_norm(z) * lg_ref[...] + lb_ref[...]


def _merge_project(o_nsa, o_fox, merge, x, wn, wf, wo, g1, ln_g, ln_b, alpha, tm=256):
    m, d = x.shape
    w = o_nsa.shape[1]
    resident = pl.Buffered(1)
    row = lambda i: (i, 0)
    fixed = lambda i: (0, 0)
    return pl.pallas_call(
        functools.partial(_merge_kernel, alpha=alpha),
        grid=(m // tm,),
        in_specs=[pl.BlockSpec((tm, w), row), pl.BlockSpec((tm, w), row),
                  pl.BlockSpec((tm, 2 * d), row), pl.BlockSpec((tm, d), row),
                  pl.BlockSpec((w, d), fixed, pipeline_mode=resident),
                  pl.BlockSpec((w, d), fixed, pipeline_mode=resident),
                  pl.BlockSpec((d, d), fixed, pipeline_mode=resident),
                  pl.BlockSpec((1, d), fixed), pl.BlockSpec((1, d), fixed), pl.BlockSpec((1, d), fixed)],
        out_specs=pl.BlockSpec((tm, d), row),
        out_shape=jax.ShapeDtypeStruct((m, d), F32),
        compiler_params=_cparams(("parallel",)),
        name="merge_project_ln",
    )(o_nsa, o_fox, merge, x, wn, wf, wo, g1, ln_g, ln_b)


def _router_kernel(x_ref, sc_ref, sh_ref, w_ref, b_ref, u_ref, r_ref):
    u = _layer_norm(x_ref[...]) * (1.0 + sc_ref[...]) + sh_ref[...]
    u_ref[...] = u
    logits = jnp.dot(u, w_ref[...], precision=HIGHEST, preferred_element_type=F32) + b_ref[...]
    lane = lax.broadcasted_iota(jnp.int32, (1, LANES), 1).astype(F32)
    none = float(LANES)
    is_g = lane < N_GROUPS
    lg = jnp.where(is_g, logits, NEG_INF)
    eg = jnp.exp(lg - jnp.max(lg, axis=-1, keepdims=True))
    pg = eg / jnp.sum(eg, axis=-1, keepdims=True)
    p_grp = jnp.max(pg, axis=-1, keepdims=True)
    grp = jnp.min(jnp.where(pg == p_grp, lane, none), axis=-1, keepdims=True)
    lo = N_GROUPS + grp * EXPERTS_PER_GROUP
    is_e = (lane >= lo) & (lane < lo + EXPERTS_PER_GROUP)
    le = jnp.where(is_e, logits, NEG_INF)
    ee = jnp.exp(le - jnp.max(le, axis=-1, keepdims=True))
    pe = jnp.where(is_e, ee / jnp.sum(ee, axis=-1, keepdims=True), -1.0)
    p1 = jnp.max(pe, axis=-1, keepdims=True)
    i1 = jnp.min(jnp.where(pe == p1, lane, none), axis=-1, keepdims=True)
    pe2 = jnp.where(lane == i1, -1.0, pe)
    p2 = jnp.max(pe2, axis=-1, keepdims=True)
    i2 = jnp.min(jnp.where(pe2 == p2, lane, none), axis=-1, keepdims=True)
    den = p1 + p2
    r_ref[...] = jnp.where(lane == 0, i1 - N_GROUPS,
                           jnp.where(lane == 1, i2 - N_GROUPS,
                                     jnp.where(lane == 2, p_grp * p1 / den,
                                               jnp.where(lane == 3, p_grp * p2 / den, 0.0))))


def _router(x1, sc, sh, w_r, b_r, tm=256):
    m, d = x1.shape
    row = lambda i: (i, 0)
    fixed = lambda i: (0, 0)
    return pl.pallas_call(
        _router_kernel,
        grid=(m // tm,),
        in_specs=[pl.BlockSpec((tm, d), row), pl.BlockSpec((1, d), fixed), pl.BlockSpec((1, d), fixed),
                  pl.BlockSpec((d, LANES), fixed), pl.BlockSpec((1, LANES), fixed)],
        out_specs=[pl.BlockSpec((tm, d), row), pl.BlockSpec((tm, LANES), row)],
        out_shape=[jax.ShapeDtypeStruct((m, d), F32), jax.ShapeDtypeStruct((m, LANES), F32)],
        compiler_params=_cparams(("parallel",)),
        name="moe_router",
    )(x1, sc, sh, w_r, b_r)


def _moe_kernel(be_ref, nv_ref, tok_ref, dst_ref, rw_ref, u_hbm, wg_ref, wu_ref, wd_ref, out_hbm,
                xbuf, ybuf, wgb, wub, wdb, sem_in, sem_out):
    i = pl.program_id(0)
    nv = nv_ref[i]

    @pl.when(i == 0)
    def _():
        xbuf[...] = jnp.zeros(xbuf.shape, F32)

    def row_in(r, tok):
        return pltpu.make_async_copy(u_hbm.at[pl.ds(tok, 1), :], xbuf.at[pl.ds(r, 1), :], sem_in)

    def row_out(r, dst):
        return pltpu.make_async_copy(ybuf.at[pl.ds(r, 1), :], out_hbm.at[pl.ds(dst, 1), :], sem_out)

    def start_in(r, carry):
        row_in(r, tok_ref[0, 0, r]).start()
        return carry

    lax.fori_loop(0, nv, start_in, 0)

    prev = be_ref[jnp.maximum(i - 1, 0)]

    @pl.when((i == 0) | (be_ref[i] != prev))
    def _():
        wgb[...] = wg_ref[0].astype(BF16)
        wub[...] = wu_ref[0].astype(BF16)
        wdb[...] = wd_ref[0].astype(BF16)

    def wait_in(r, carry):
        row_in(r, 0).wait()
        return carry

    lax.fori_loop(0, nv, wait_in, 0)

    xb = xbuf[...].astype(BF16)
    gate = _dot(xb, wgb[...])
    up = _dot(xb, wub[...])
    hid = (gate * _sigmoid(gate)) * up
    ybuf[...] = _dot(hid.astype(BF16), wdb[...]) * rw_ref[0]

    def start_out(r, carry):
        row_out(r, dst_ref[0, 0, r]).start()
        return carry

    lax.fori_loop(0, nv, start_out, 0)

    def wait_out(r, carry):
        row_out(r, 0).wait()
        return carry

    lax.fori_loop(0, nv, wait_out, 0)


def _moe_experts(u, blk_exp, n_valid, row_tok, row_dst, row_w, w_gate, w_up, w_down):
    t, d = u.shape
    n_blocks = blk_exp.shape[0]
    de = w_gate.shape[-1]
    rb = ROW_BLOCK
    grid_spec = pltpu.PrefetchScalarGridSpec(
        num_scalar_prefetch=2,
        grid=(n_blocks,),
        in_specs=[pl.BlockSpec((1, 1, rb), lambda i, be, nv: (i, 0, 0), memory_space=pltpu.SMEM),
                  pl.BlockSpec((1, 1, rb), lambda i, be, nv: (i, 0, 0), memory_space=pltpu.SMEM),
                  pl.BlockSpec((1, rb, 1), lambda i, be, nv: (i, 0, 0)),
                  pl.BlockSpec(memory_space=pl.ANY),
                  pl.BlockSpec((1, d, de), lambda i, be, nv: (be[i], 0, 0)),
                  pl.BlockSpec((1, d, de), lambda i, be, nv: (be[i], 0, 0)),
                  pl.BlockSpec((1, de, d), lambda i, be, nv: (be[i], 0, 0))],
        out_specs=pl.BlockSpec(memory_space=pl.ANY),
        scratch_shapes=[pltpu.VMEM((rb, d), F32), pltpu.VMEM((rb, d), F32),
                        pltpu.VMEM((d, de), BF16), pltpu.VMEM((d, de), BF16), pltpu.VMEM((de, d), BF16),
                        pltpu.SemaphoreType.DMA(()), pltpu.SemaphoreType.DMA(())],
    )
    return pl.pallas_call(
        _moe_kernel,
        grid_spec=grid_spec,
        out_shape=jax.ShapeDtypeStruct((2 * t, d), F32),
        compiler_params=_cparams(("arbitrary",)),
        name="moe_experts",
    )(blk_exp, n_valid, row_tok.reshape(n_blocks, 1, rb), row_dst.reshape(n_blocks, 1, rb),
      row_w.reshape(n_blocks, rb, 1), u, w_gate, w_up, w_down)


def _moe_dispatch(route, t):
    k = 2
    eid = route[:, 0:k].astype(jnp.int32).reshape(-1)
    wts = route[:, k:2 * k].reshape(-1)
    n_asg = t * k
    n_rows = n_asg + N_EXPERTS * ROW_BLOCK
    n_blocks = n_rows // ROW_BLOCK
    order = jnp.argsort(eid)
    e_sorted = eid[order]
    counts = jnp.bincount(eid, length=N_EXPERTS)
    padded = (counts + ROW_BLOCK - 1) // ROW_BLOCK * ROW_BLOCK
    pad_end = jnp.cumsum(padded)
    pad_start = pad_end - padded
    start = jnp.cumsum(counts) - counts
    dest = pad_start[e_sorted] + jnp.arange(n_asg) - start[e_sorted]
    tok = (order // k).astype(jnp.int32)
    slot = (order % k).astype(jnp.int32)
    row_tok = jnp.zeros((n_rows,), jnp.int32).at[dest].set(tok)
    row_dst = jnp.zeros((n_rows,), jnp.int32).at[dest].set(slot * t + tok)
    row_w = jnp.zeros((n_rows,), F32).at[dest].set(wts[order])
    blk_start = jnp.arange(n_blocks) * ROW_BLOCK
    blk_exp = jnp.minimum(jnp.searchsorted(pad_end, blk_start, side='right'), N_EXPERTS - 1).astype(jnp.int32)
    n_valid = jnp.clip(pad_start[blk_exp] + counts[blk_exp] - blk_start, 0, ROW_BLOCK).astype(jnp.int32)
    return blk_exp, n_valid, row_tok, row_dst, row_w


def _final_kernel(x_ref, y0_ref, y1_ref, g2_ref, lg_ref, lb_ref, o_ref, *, alpha):
    z = alpha * x_ref[...] + (1.0 + g2_ref[...]) * (y0_ref[...] + y1_ref[...])
    o_ref[...] = _layer_norm(z) * lg_ref[...] + lb_ref[...]


def _final_ln(x1, y2, g2, ln_g, ln_b, alpha, tm=512):
    m, d = x1.shape
    nb = m // tm
    fixed = lambda i: (0, 0)
    return pl.pallas_call(
        functools.partial(_final_kernel, alpha=alpha),
        grid=(nb,),
        in_specs=[pl.BlockSpec((tm, d), lambda i: (i, 0)),
                  pl.BlockSpec((tm, d), lambda i: (i, 0)),
                  pl.BlockSpec((tm, d), lambda i: (i + nb, 0)),
                  pl.BlockSpec((1, d), fixed), pl.BlockSpec((1, d), fixed), pl.BlockSpec((1, d), fixed)],
        out_specs=pl.BlockSpec((tm, d), lambda i: (i, 0)),
        out_shape=jax.ShapeDtypeStruct((m, d), F32),
        compiler_params=_cparams(("parallel",)),
        name="final_ln",
    )(x1, y2, y2, g2, ln_g, ln_b)


def _to_lane_blocks(a, tq):
    g, hpg, s, c = a.shape
    return a.reshape(g, hpg, s // tq, tq, c).transpose(0, 2, 4, 1, 3).reshape(g, s // tq, c, hpg * tq)


def _layer(x2d, c, w_ada, b_ada, w_in, b_fgt, t5_table, cmp_pe, cmp_w1, cmp_b1, cmp_w2,
           w_br_nsa, w_br_fox, w_o, ln1_g, ln1_b, w_rg, b_rg, w_re, b_re,
           w_gate, w_up, w_down, ln2_g, ln2_b, alpha):
    s, d = x2d.shape
    hd = HEAD_DIM
    g = NSA_GROUPS
    mod = _ada_mod(c, w_ada, b_ada)
    sh1, sc1, g1, sh2, sc2, g2 = [mod[:, i * d:(i + 1) * d] for i in range(6)]

    c_q = NSA_HEADS * hd
    c_kv = 6 * g * hd
    c_gate = 3 * NSA_HEADS
    c_fox = 3 * FOX_HEADS * hd
    off_kv = c_q
    off_gate = off_kv + c_kv
    off_fox = off_gate + c_gate
    off_fgt = off_fox + c_fox
    off_merge = off_fgt + FOX_HEADS
    qscale = hd ** -0.5 * LOG2E
    w_attn = jnp.concatenate([w_in[:, 0:off_kv] * qscale, w_in[:, off_kv:off_gate],
                              w_in[:, off_fox:off_fox + FOX_HEADS * hd] * qscale,
                              w_in[:, off_fox + FOX_HEADS * hd:off_fgt]], axis=1).astype(BF16)
    n_small = c_gate + FOX_HEADS
    w_small = jnp.concatenate([w_in[:, off_gate:off_fox], w_in[:, off_fgt:off_merge],
                               jnp.zeros((d, LANES - n_small), F32)], axis=1).astype(BF16)
    w_merge = w_in[:, off_merge:].astype(BF16)

    u = _ln_mod(x2d, sc1, sh1, BF16)
    n_attn = w_attn.shape[1]
    proj = _matmul(u, w_attn, BF16, 512, n_attn // 2, "in_proj_attn")
    small = _matmul(u, w_small, F32, 512, LANES, "in_proj_small")
    merge = _matmul(u, w_merge, BF16, 512, 1024, "in_proj_merge")

    q_nsa = proj[:, 0:c_q].reshape(s, g, NSA_HPG, hd).transpose(1, 2, 0, 3)
    kv = proj[:, c_q:c_q + c_kv].reshape(s, 6, g, hd).transpose(1, 2, 0, 3)
    fox = proj[:, c_q + c_kv:].reshape(s, 3, FOX_HEADS, hd).transpose(1, 2, 0, 3)

    kv_cmp = _compress(kv[0:2], cmp_pe, cmp_w1, cmp_b1, cmp_w2)
    nch = s // CMP_STRIDE
    ncp = nch + LANES
    nslc = s // SLC_LEN
    kv_cmp_pad = jnp.pad(kv_cmp, ((0, 0), (0, 0), (CMP_PAD, ncp - nch - CMP_PAD), (0, 0)))
    b_cmp, far_cmp, b_sel, b_win, far3 = _nsa_bias_tables(t5_table)
    w_slc = jnp.asarray(_slc_weight_np(ncp, nslc))
    o_c, sel = _nsa_compress_select(q_nsa, kv_cmp_pad[0], kv_cmp_pad[1], b_cmp, far_cmp, w_slc, nch - 1)

    def ones(*shape):
        return jnp.ones(shape, BF16)

    def zeros(*shape):
        return jnp.zeros(shape, BF16)

    blk_onehot = jnp.asarray((np.arange(s)[:, None] // SLC_LEN % BF16_ROWS == np.arange(BF16_ROWS)[None, :])
                             .astype(np.float32), BF16)
    ks_aug = jnp.concatenate([kv[2], jnp.broadcast_to(blk_onehot, (g, s, BF16_ROWS)), ones(g, s, 3),
                              zeros(g, s, AUG_K - ROW_BIAS - 3)], axis=-1)
    kw_aug = jnp.concatenate([kv[4], zeros(g, s, AUG_K - hd)], axis=-1)
    vst_aug = jnp.concatenate([kv[3].transpose(0, 2, 1), ones(g, 8, s), zeros(g, 8, s)], axis=1)
    vwt_aug = jnp.concatenate([kv[5].transpose(0, 2, 1), ones(g, 8, s), zeros(g, 8, s)], axis=1)
    qt = _to_lane_blocks(q_nsa, ATT_TQ)
    oct_ = _to_lane_blocks(o_c, ATT_TQ)
    gates = small[:, 0:c_gate].reshape(s, g, NSA_HPG, 3).transpose(1, 2, 0, 3)
    gates_t = jnp.pad(_to_lane_blocks(gates, ATT_TQ), ((0, 0), (0, 0), (0, 5), (0, 0)))
    o_nsa_t = _nsa_attention(qt, ks_aug, vst_aug, kw_aug, vwt_aug, sel.transpose(0, 2, 1), far3,
                             b_sel, b_win, oct_, gates_t)
    nq = s // ATT_TQ
    o_nsa = o_nsa_t.reshape(g, nq, hd, NSA_HPG, ATT_TQ).transpose(1, 4, 0, 3, 2).reshape(s, MIX_W)

    fgt_bias = jnp.concatenate([jnp.zeros((c_gate,), F32), b_fgt, jnp.zeros((LANES - n_small,), F32)])[None, :]
    decay = _decay_cumsum(small, fgt_bias)[:, c_gate:c_gate + FOX_HEADS]
    nh = FOX_HEADS
    d_hi, d_mid, d_lo = _split3_exact(-decay.T * LOG2E)
    fox_k_aug = jnp.concatenate([fox[1], jnp.stack([d_hi, d_mid, d_lo], axis=-1), zeros(nh, s, AUG_K - hd - 3)],
                                axis=-1)
    fox_qt_aug = jnp.concatenate([fox[0].transpose(0, 2, 1), ones(nh, 3, s), zeros(nh, AUG_K - hd - 3, s)], axis=1)
    fox_vt_aug = jnp.concatenate([fox[2].transpose(0, 2, 1), ones(nh, 8, s), zeros(nh, 8, s)], axis=1)
    o_fox = _fox_attention(fox_qt_aug, fox_k_aug, fox_vt_aug)
    o_fox = o_fox.transpose(2, 0, 1).reshape(s, MIX_W)

    x1 = _merge_project(o_nsa, o_fox, merge, x2d, w_br_nsa.astype(BF16), w_br_fox.astype(BF16),
                        w_o.astype(BF16), g1, ln1_g[None, :], ln1_b[None, :], alpha)

    n_r = N_GROUPS + N_EXPERTS
    w_r = jnp.concatenate([w_rg, w_re.reshape(d, N_EXPERTS), jnp.zeros((d, LANES - n_r), F32)], axis=1)
    b_r = jnp.concatenate([b_rg, b_re.reshape(N_EXPERTS), jnp.zeros((LANES - n_r,), F32)])[None, :]
    u2, route = _router(x1, sc2, sh2, w_r, b_r)
    blk_exp, n_valid, row_tok, row_dst, row_w = _moe_dispatch(route, s)
    y2 = _moe_experts(u2, blk_exp, n_valid, row_tok, row_dst, row_w, w_gate, w_up, w_down)
    return _final_ln(x1, y2, g2, ln2_g[None, :], ln2_b[None, :], alpha)


def kernel(x, c, w_ada, b_ada, w_in, b_fgt, t5_table, cmp_pe, cmp_w1, cmp_b1, cmp_w2, w_br_nsa, w_br_fox, w_o,
           ln1_g, ln1_b, w_rg, b_rg, w_re, b_re, w_gate, w_up, w_down, ln2_g, ln2_b):
    b, s, d = x.shape
    depth = w_ada.shape[0]
    assert b == 1
    alpha = (2 * depth) ** 0.25
    h = x[0]
    for l in range(depth):
        h = _layer(h, c, w_ada[l], b_ada[l], w_in[l], b_fgt[l], t5_table, cmp_pe[l], cmp_w1[l], cmp_b1[l],
                   cmp_w2[l], w_br_nsa[l], w_br_fox[l], w_o[l], ln1_g[l], ln1_b[l], w_rg[l], b_rg[l],
                   w_re[l], b_re[l], w_gate[l], w_up[l], w_down[l], ln2_g[l], ln2_b[l], alpha)
    return h[None]
```

```python
import functools
import math

import numpy as np
import jax
import jax.numpy as jnp
from jax import lax
from jax.experimental import pallas as pl
from jax.experimental.pallas import tpu as pltpu

F32 = jnp.float32
BF16 = jnp.bfloat16
HIGHEST = lax.Precision.HIGHEST
LOG2E = math.log2(math.e)

HEAD_DIM = 64
NSA_HEADS = 8
NSA_GROUPS = 2
NSA_HPG = NSA_HEADS // NSA_GROUPS
FOX_HEADS = 8
MIX_W = NSA_HEADS * HEAD_DIM
CMP_LEN = 32
CMP_STRIDE = 16
SLC_LEN = 64
SLC_TOPK = 16
WINDOW = 512
T5_BUCKETS = 32
T5_MAX_EXACT = 16
T5_MAX_DIST = 128
N_GROUPS = 8
EXPERTS_PER_GROUP = 8
N_EXPERTS = N_GROUPS * EXPERTS_PER_GROUP
ROW_BLOCK = 128
LN_EPS = 1e-5
NEG_INF = -1e30
M_INIT = -1e29
FORCE_SCORE = 1e4

LANES = 128
BF16_ROWS = 16
CMP_PAD = 8
CMP_TQ = 128
ATT_TQ = 256
FOX_TQ = 1024
FOX_TK = 512
AUG_K = 128
AUG_V = HEAD_DIM + 16
ROW_MASK = HEAD_DIM
ROW_BIAS = HEAD_DIM + 16
VMEM_LIMIT = 56 * 1024 * 1024


def _cparams(sem, vmem=VMEM_LIMIT):
    return pltpu.CompilerParams(dimension_semantics=sem, vmem_limit_bytes=vmem)


def _sigmoid(x):
    return 1.0 / (1.0 + jnp.exp(-x))


def _layer_norm(x):
    mu = jnp.mean(x, axis=-1, keepdims=True)
    xc = x - mu
    var = jnp.mean(xc * xc, axis=-1, keepdims=True)
    return xc * lax.rsqrt(var + LN_EPS)


def _split3(x):
    hi = x.astype(BF16)
    r1 = x - hi.astype(F32)
    mid = r1.astype(BF16)
    lo = (r1 - mid.astype(F32)).astype(BF16)
    return hi, mid, lo


def _split3_exact(x):
    def trunc(v):
        bits = lax.bitcast_convert_type(v, jnp.uint32) & jnp.uint32(0xFFFF0000)
        return lax.bitcast_convert_type(bits, F32)
    hi = trunc(x)
    r1 = x - hi
    mid = trunc(r1)
    lo = r1 - mid
    return hi.astype(BF16), mid.astype(BF16), lo.astype(BF16)


def _dot(a, b):
    return jnp.dot(a, b, preferred_element_type=F32)


def _dot_nt(a, b):
    return lax.dot_general(a, b, (((1,), (1,)), ((), ())), preferred_element_type=F32)


def _dot3(x, w_bf16):
    hi, mid, lo = _split3(x)
    return _dot(hi, w_bf16) + _dot(mid, w_bf16) + _dot(lo, w_bf16)


def _dot3_rhs(w_bf16, x):
    hi, mid, lo = _split3(x)
    return _dot(w_bf16, hi) + _dot(w_bf16, mid) + _dot(w_bf16, lo)


def _ada_kernel(c_ref, w_ref, b_ref, o_ref):
    c = c_ref[...]
    a = c * _sigmoid(c)
    o_ref[...] = jnp.dot(a, w_ref[...], precision=HIGHEST, preferred_element_type=F32) + b_ref[...]


def _ada_mod(c, w, b):
    d, n = w.shape
    tn = 1024
    c8 = jnp.broadcast_to(c, (8, d))
    out = pl.pallas_call(
        _ada_kernel,
        grid=(n // tn,),
        in_specs=[pl.BlockSpec((8, d), lambda j: (0, 0)),
                  pl.BlockSpec((d, tn), lambda j: (0, j)),
                  pl.BlockSpec((1, tn), lambda j: (0, j))],
        out_specs=pl.BlockSpec((8, tn), lambda j: (0, j)),
        out_shape=jax.ShapeDtypeStruct((8, n), F32),
        compiler_params=_cparams(("parallel",)),
        name="ada_mod",
    )(c8, w, b.reshape(1, n))
    return out[0:1]


def _lnmod_kernel(x_ref, sc_ref, sh_ref, o_ref):
    y = _layer_norm(x_ref[...])
    o_ref[...] = (y * (1.0 + sc_ref[...]) + sh_ref[...]).astype(o_ref.dtype)


def _ln_mod(x, sc, sh, out_dtype, tm=512):
    m, d = x.shape
    return pl.pallas_call(
        _lnmod_kernel,
        grid=(m // tm,),
        in_specs=[pl.BlockSpec((tm, d), lambda i: (i, 0)),
                  pl.BlockSpec((1, d), lambda i: (0, 0)),
                  pl.BlockSpec((1, d), lambda i: (0, 0))],
        out_specs=pl.BlockSpec((tm, d), lambda i: (i, 0)),
        out_shape=jax.ShapeDtypeStruct((m, d), out_dtype),
        compiler_params=_cparams(("parallel",)),
        name="ln_mod",
    )(x, sc, sh)


def _mm_kernel(a_ref, w_ref, o_ref):
    o_ref[...] = _dot(a_ref[...], w_ref[...]).astype(o_ref.dtype)


def _matmul(a, w, out_dtype, tm, tn, name):
    m, k = a.shape
    n = w.shape[1]
    return pl.pallas_call(
        _mm_kernel,
        grid=(n // tn, m // tm),
        in_specs=[pl.BlockSpec((tm, k), lambda j, i: (i, 0)),
                  pl.BlockSpec((k, tn), lambda j, i: (0, j))],
        out_specs=pl.BlockSpec((tm, tn), lambda j, i: (i, j)),
        out_shape=jax.ShapeDtypeStruct((m, n), out_dtype),
        compiler_params=_cparams(("parallel", "parallel")),
        name=name,
    )(a, w)


def _gelu_tanh(x):
    return 0.5 * x * (1.0 + jnp.tanh(math.sqrt(2.0 / math.pi) * (x + 0.044715 * (x * x * x))))


def _compress_kernel(c_ref, pe_ref, w1a_ref, w1b_ref, b1_ref, w2_ref, o_ref, *, nch):
    c = c_ref[0, 0]
    w1a = w1a_ref[0]
    w1b = w1b_ref[0]
    half = CMP_STRIDE * HEAD_DIM
    a = _dot(c, w1a)
    b = _dot(c, w1b)
    b_next = pltpu.roll(b, shift=nch - 1, axis=0)
    pe = pe_ref[0]
    pb = _dot(pe[:, :half], w1a) + _dot(pe[:, half:], w1b)
    hid = _gelu_tanh(a + b_next + pb[0:1, :] + b1_ref[0])
    o_ref[0, 0] = _dot(hid.astype(BF16), w2_ref[0])


def _compress(kv_cmp, pe, w1, b1, w2):
    z, g, s, hd = kv_cmp.shape
    nch = s // CMP_STRIDE
    half = CMP_STRIDE * hd
    chunks = kv_cmp.reshape(z, g, nch, half)
    pe8 = jnp.broadcast_to(pe.reshape(z, 1, CMP_LEN * hd), (z, 8, CMP_LEN * hd)).astype(BF16)
    w1b16 = w1.astype(BF16)
    hidn = w1.shape[-1]
    return pl.pallas_call(
        functools.partial(_compress_kernel, nch=nch),
        grid=(z, g),
        in_specs=[pl.BlockSpec((1, 1, nch, half), lambda zi, gi: (zi, gi, 0, 0)),
                  pl.BlockSpec((1, 8, 2 * half), lambda zi, gi: (zi, 0, 0)),
                  pl.BlockSpec((1, half, hidn), lambda zi, gi: (zi, 0, 0)),
                  pl.BlockSpec((1, half, hidn), lambda zi, gi: (zi, 1, 0)),
                  pl.BlockSpec((1, 1, hidn), lambda zi, gi: (zi, 0, 0)),
                  pl.BlockSpec((1, hidn, hd), lambda zi, gi: (zi, 0, 0))],
        out_specs=pl.BlockSpec((1, 1, nch, hd), lambda zi, gi: (zi, gi, 0, 0)),
        out_shape=jax.ShapeDtypeStruct((z, g, nch, hd), F32),
        compiler_params=_cparams(("parallel", "parallel")),
        name="compress_kv",
    )(chunks, pe8, w1b16, w1b16, b1.reshape(z, 1, hidn), w2.astype(BF16))


def _t5_bucket_np(dist):
    n = np.maximum(dist, 0)
    ratio = np.log(np.maximum(n, T5_MAX_EXACT).astype(np.float64) / T5_MAX_EXACT)
    big = T5_MAX_EXACT + (ratio / math.log(T5_MAX_DIST / T5_MAX_EXACT)
                          * (T5_BUCKETS - T5_MAX_EXACT)).astype(np.int64)
    return np.where(n < T5_MAX_EXACT, n, np.minimum(big, T5_BUCKETS - 1)).astype(np.int32)


def _toeplitz_t(w, n_keys, n_q):
    length = n_keys + n_q - 1
    w_pad = jnp.concatenate([w, jnp.zeros(w.shape[:-1] + (1,), w.dtype)], axis=-1)
    reps = (1,) * (w.ndim - 1) + (n_keys,)
    flat = jnp.tile(w_pad, reps)[..., :n_keys * length]
    return flat.reshape(w.shape[:-1] + (n_keys, length))[..., n_keys - 1:n_keys - 1 + n_q]


def _att_table_t(tbh, n_keys, tq, lo, hi, minus_far):
    d = np.arange(n_keys + tq - 1) - (tq - 1)
    valid = (d >= lo) & (d < hi)
    vals = tbh[:, :, _t5_bucket_np(d)]
    if minus_far:
        vals = vals - tbh[:, :, T5_BUCKETS - 1:]
    w = jnp.where(jnp.asarray(valid), vals * LOG2E, NEG_INF)
    t = _toeplitz_t(w, n_keys, tq)
    g, hpg = tbh.shape[:2]
    return t.transpose(0, 2, 1, 3).reshape(g, n_keys, hpg * tq)


def _nsa_bias_tables(t5_table):
    tbh = t5_table.T.reshape(NSA_GROUPS, NSA_HPG, T5_BUCKETS).astype(F32)
    i = np.arange(CMP_TQ)[:, None]
    c = np.arange(LANES)[None, :]
    dist = i - (CMP_LEN - 1) - CMP_STRIDE * (c - CMP_PAD)
    near_cols = CMP_TQ // CMP_STRIDE + CMP_PAD
    vals = tbh[:, :, _t5_bucket_np(dist)] * LOG2E
    vals = jnp.where(jnp.asarray((dist >= 0) & (c < near_cols))[None, None], vals, NEG_INF)
    b_cmp = vals.reshape(NSA_GROUPS, NSA_HPG * CMP_TQ, LANES)
    far = tbh[:, :, T5_BUCKETS - 1] * LOG2E
    far_cmp = jnp.repeat(far, CMP_TQ, axis=1)[..., None]
    b_sel = _att_table_t(tbh, 2 * ATT_TQ, ATT_TQ, 0, 1 << 30, True)
    b_sel = jnp.pad(b_sel, ((0, 0), (ATT_TQ, 0), (0, 0)))
    b_win = _att_table_t(tbh, 3 * ATT_TQ, ATT_TQ, 0, WINDOW, False)
    b_win = jnp.pad(b_win, ((0, 0), (0, ATT_TQ), (0, 0)), constant_values=NEG_INF)
    hi, mid, lo = _split3_exact(jnp.repeat(far, ATT_TQ, axis=1))
    far3 = jnp.stack([hi, mid, lo], axis=1)
    far3 = jnp.pad(far3, ((0, 0), (0, BF16_ROWS - 3), (0, 0)))
    return b_cmp, far_cmp, b_sel, b_win, far3


def _slc_weight_np(ncp, nslc):
    w = np.zeros((ncp, nslc), np.float32)
    ratio = SLC_LEN // CMP_STRIDE
    for off, val in ((-1, 0.5), (0, 1.0), (1, 1.0), (2, 1.0), (3, 0.5)):
        j = np.arange(nslc)
        n = ratio * j + off
        ok = (n >= 0) & (n + CMP_PAD < ncp)
        w[n[ok] + CMP_PAD, j[ok]] += val
    return w


def _nsa_cmp_kernel(q_ref, kc_ref, vc_ref, bnear_ref, bfar_ref, w_ref, oc_ref, sel_ref, *, ncp, nslc, n_valid_cmp):
    tq = CMP_TQ
    rows = NSA_HPG * tq
    qb = pl.program_id(1)
    cpb = tq // CMP_STRIDE
    q2 = q_ref[0].reshape(rows, HEAD_DIM)
    kc = kc_ref[0].astype(BF16)
    vc = vc_ref[0].astype(BF16)
    r0 = pl.multiple_of(cpb * qb, 8)
    r = lax.broadcasted_iota(jnp.int32, (1, ncp), 1)
    s_far = _dot_nt(q2, kc) + bfar_ref[0]
    s_far = jnp.where((r >= CMP_PAD) & (r < r0), s_far, NEG_INF)
    kn = kc_ref[0, pl.ds(r0, LANES), :].astype(BF16)
    vn = vc_ref[0, pl.ds(r0, LANES), :].astype(BF16)
    rn = r0 + lax.broadcasted_iota(jnp.int32, (1, LANES), 1)
    s_near = _dot_nt(q2, kn) + bnear_ref[0]
    s_near = jnp.where((rn >= CMP_PAD) & (rn < n_valid_cmp + CMP_PAD), s_near, NEG_INF)
    m = jnp.maximum(jnp.max(s_far, axis=-1, keepdims=True), jnp.max(s_near, axis=-1, keepdims=True))
    e_far = jnp.exp2(s_far - m)
    e_near = jnp.exp2(s_near - m)
    l = jnp.sum(e_far, axis=-1, keepdims=True) + jnp.sum(e_near, axis=-1, keepdims=True)
    inv = jnp.where(m > M_INIT, 1.0 / l, 0.0)
    p_far = e_far * inv
    p_near = e_near * inv
    o = _dot(p_far.astype(BF16), vc) + _dot(p_near.astype(BF16), vn)
    oc_ref[0] = o.reshape(NSA_HPG, tq, HEAD_DIM)
    imp_far = jnp.sum(p_far.reshape(NSA_HPG, tq, ncp), axis=0)
    imp_near = jnp.sum(p_near.reshape(NSA_HPG, tq, LANES), axis=0)
    w_all = w_ref[...].astype(BF16)
    w_near = w_ref[pl.ds(r0, LANES), :].astype(BF16)
    p_slc = _dot3(imp_far, w_all) + _dot3(imp_near, w_near)
    blk = lax.broadcasted_iota(jnp.int32, (1, nslc), 1)
    t = qb * tq + lax.broadcasted_iota(jnp.int32, (tq, 1), 0)
    cur = t // SLC_LEN
    forced = (blk == 0) | (blk == cur) | (blk == cur - 1)
    score = jnp.where(forced, FORCE_SCORE, jnp.where(blk <= cur, p_slc, -1.0))
    blk_f = blk.astype(F32)
    sel = jnp.zeros((tq, nslc), F32)
    for _ in range(min(SLC_TOPK, nslc)):
        mx = jnp.max(score, axis=-1, keepdims=True)
        first = jnp.min(jnp.where(score == mx, blk_f, float(nslc)), axis=-1, keepdims=True)
        hit = blk_f == first
        sel = jnp.where(hit, 1.0, sel)
        score = jnp.where(hit, -2.0, score)
    sel_ref[0] = sel.astype(BF16)


def _nsa_compress_select(q, kc_pad, vc_pad, b_cmp, far_cmp, w_slc, n_valid_cmp):
    g, hpg, s, hd = q.shape
    ncp = kc_pad.shape[1]
    nslc = s // SLC_LEN
    tq = CMP_TQ
    rows = hpg * tq
    return pl.pallas_call(
        functools.partial(_nsa_cmp_kernel, ncp=ncp, nslc=nslc, n_valid_cmp=n_valid_cmp),
        grid=(g, s // tq),
        in_specs=[pl.BlockSpec((1, hpg, tq, hd), lambda gi, qi: (gi, 0, qi, 0)),
                  pl.BlockSpec((1, ncp, hd), lambda gi, qi: (gi, 0, 0)),
                  pl.BlockSpec((1, ncp, hd), lambda gi, qi: (gi, 0, 0)),
                  pl.BlockSpec((1, rows, LANES), lambda gi, qi: (gi, 0, 0)),
                  pl.BlockSpec((1, rows, 1), lambda gi, qi: (gi, 0, 0)),
                  pl.BlockSpec((ncp, nslc), lambda gi, qi: (0, 0))],
        out_specs=[pl.BlockSpec((1, hpg, tq, hd), lambda gi, qi: (gi, 0, qi, 0)),
                   pl.BlockSpec((1, tq, nslc), lambda gi, qi: (gi, qi, 0))],
        out_shape=[jax.ShapeDtypeStruct((g, hpg, s, hd), F32),
                   jax.ShapeDtypeStruct((g, s, nslc), BF16)],
        compiler_params=_cparams(("parallel", "parallel")),
        name="nsa_compress_select",
    )(q, kc_pad, vc_pad, b_cmp, far_cmp, w_slc)


def _flash_init_t(m_ref, acc_ref):
    m_ref[...] = jnp.full(m_ref.shape, M_INIT, F32)
    acc_ref[...] = jnp.zeros(acc_ref.shape, F32)


def _flash_step_t(s, vt_tile, m_ref, acc_ref):
    m_old = m_ref[...]
    m_new = jnp.maximum(m_old, jnp.max(s, axis=0, keepdims=True))
    p = jnp.exp2(s - m_new).astype(BF16)
    acc_ref[...] = jnp.exp2(m_old - m_new) * acc_ref[...] + _dot(vt_tile, p)
    m_ref[...] = m_new


def _flash_result_t(acc_ref):
    acc = acc_ref[...]
    return acc[0:HEAD_DIM, :] / acc[HEAD_DIM:HEAD_DIM + 1, :]


def _nsa_att_kernel(qt_ref, ks_ref, vst_ref, kw_ref, vwt_ref, selt_ref, far3_ref, bsel_ref, bwin_ref,
                    oct_ref, gate_ref, o_ref, rhs_s, rhs_w, mask_t, ms, accs, mw, accw, s_even, s_odd):
    tq = ATT_TQ
    qb = pl.program_id(1)
    qt = qt_ref[0, 0]
    rhs_s[...] = jnp.zeros(rhs_s.shape, BF16)
    rhs_s[0:HEAD_DIM, :] = qt
    rhs_s[ROW_BIAS:ROW_BIAS + BF16_ROWS, :] = far3_ref[0]
    rhs_w[...] = jnp.zeros(rhs_w.shape, BF16)
    rhs_w[0:HEAD_DIM, :] = qt
    madd = ((selt_ref[0].astype(F32) - 1.0) * (-NEG_INF)).astype(BF16)
    mask_t[...] = jnp.concatenate([madd] * NSA_HPG, axis=1)
    _flash_init_t(ms, accs)
    _flash_init_t(mw, accw)
    blocks_per_tile = tq // SLC_LEN

    def sel_scores(kt, s_ref):
        kt = jnp.minimum(kt, qb)
        k0 = pl.multiple_of(kt * tq, tq)
        chunk = pl.multiple_of((kt * blocks_per_tile) // BF16_ROWS * BF16_ROWS, BF16_ROWS)
        rhs_s[ROW_MASK:ROW_MASK + BF16_ROWS, :] = mask_t[pl.ds(chunk, BF16_ROWS), :]
        rel = jnp.clip(kt - qb + 2, 0, 2)
        table = bsel_ref[0, pl.ds(pl.multiple_of(rel * tq, tq), tq), :]
        s_ref[...] = _dot(ks_ref[0, pl.ds(k0, tq), :], rhs_s[...]) + table

    def sel_consume(kt, s_ref):
        k0 = pl.multiple_of(kt * tq, tq)
        _flash_step_t(s_ref[...], vst_ref[0, :, pl.ds(k0, tq)], ms, accs)

    def win_scores(j, s_ref):
        kt = qb - 2 + j
        k0 = pl.multiple_of(jnp.maximum(kt, 0) * tq, tq)
        row = pl.multiple_of(jnp.where(kt >= 0, j, 3) * tq, tq)
        s_ref[...] = _dot(kw_ref[0, pl.ds(k0, tq), :], rhs_w[...]) + bwin_ref[0, pl.ds(row, tq), :]

    def win_consume(j, s_ref):
        k0 = pl.multiple_of(jnp.maximum(qb - 2 + j, 0) * tq, tq)
        _flash_step_t(s_ref[...], vwt_ref[0, :, pl.ds(k0, tq)], mw, accw)

    n_sel = qb + 1
    sel_scores(0, s_even)

    def pair_body(j, carry):
        sel_scores(2 * j + 1, s_odd)
        sel_consume(2 * j, s_even)
        sel_scores(2 * j + 2, s_even)
        sel_consume(2 * j + 1, s_odd)
        return carry

    lax.fori_loop(0, n_sel // 2, pair_body, 0)
    win_scores(0, s_odd)

    @pl.when(n_sel % 2 == 1)
    def _():
        sel_consume(qb, s_even)

    win_scores(1, s_even)
    win_consume(0, s_odd)
    win_scores(2, s_odd)
    win_consume(1, s_even)
    win_consume(2, s_odd)

    gt = _sigmoid(gate_ref[0, 0])
    out = gt[0:1, :] * oct_ref[0, 0] + gt[1:2, :] * _flash_result_t(accs) + gt[2:3, :] * _flash_result_t(accw)
    o_ref[0, 0] = out.astype(o_ref.dtype)


def _nsa_attention(qt, ks_aug, vst_aug, kw_aug, vwt_aug, selt, far3, b_sel, b_win, oct_, gates_t):
    g, nq, hd, lanes = qt.shape
    s = ks_aug.shape[1]
    nslc = selt.shape[1]
    tq = ATT_TQ
    resident = pl.Buffered(1)
    per_q = lambda gi, qi: (gi, qi, 0, 0)
    per_g = lambda gi, qi: (gi, 0, 0)
    return pl.pallas_call(
        _nsa_att_kernel,
        grid=(g, nq),
        in_specs=[pl.BlockSpec((1, 1, hd, lanes), per_q),
                  pl.BlockSpec((1, s, AUG_K), per_g, pipeline_mode=resident),
                  pl.BlockSpec((1, AUG_V, s), per_g, pipeline_mode=resident),
                  pl.BlockSpec((1, s, AUG_K), per_g, pipeline_mode=resident),
                  pl.BlockSpec((1, AUG_V, s), per_g, pipeline_mode=resident),
                  pl.BlockSpec((1, nslc, tq), lambda gi, qi: (gi, 0, qi)),
                  pl.BlockSpec((1, BF16_ROWS, lanes), per_g),
                  pl.BlockSpec((1, 3 * tq, lanes), per_g, pipeline_mode=resident),
                  pl.BlockSpec((1, 4 * tq, lanes), per_g, pipeline_mode=resident),
                  pl.BlockSpec((1, 1, hd, lanes), per_q),
                  pl.BlockSpec((1, 1, 8, lanes), per_q)],
        out_specs=pl.BlockSpec((1, 1, hd, lanes), per_q),
        out_shape=jax.ShapeDtypeStruct((g, nq, hd, lanes), BF16),
        scratch_shapes=[pltpu.VMEM((AUG_K, lanes), BF16), pltpu.VMEM((AUG_K, lanes), BF16),
                        pltpu.VMEM((nslc, lanes), BF16),
                        pltpu.VMEM((1, lanes), F32), pltpu.VMEM((AUG_V, lanes), F32),
                        pltpu.VMEM((1, lanes), F32), pltpu.VMEM((AUG_V, lanes), F32),
                        pltpu.VMEM((tq, lanes), F32), pltpu.VMEM((tq, lanes), F32)],
        compiler_params=_cparams(("arbitrary", "arbitrary")),
        name="nsa_select_window",
    )(qt, ks_aug, vst_aug, kw_aug, vwt_aug, selt, far3, b_sel, b_win, oct_, gates_t)


def _decay_kernel(z_ref, b_ref, o_ref, carry_ref, *, tb):
    @pl.when(pl.program_id(0) == 0)
    def _():
        carry_ref[...] = jnp.zeros(carry_ref.shape, F32)

    z = z_ref[...] + b_ref[...]
    log_f = jnp.minimum(z, 0.0) - jnp.log1p(jnp.exp(-jnp.abs(z)))
    r = lax.broadcasted_iota(jnp.int32, (tb, tb), 0)
    c = lax.broadcasted_iota(jnp.int32, (tb, tb), 1)
    tri = jnp.where(r >= c, 1.0, 0.0).astype(BF16)
    run = _dot3_rhs(tri, log_f) + carry_ref[...]
    o_ref[...] = run
    carry_ref[...] = run[tb - 1:tb, :]


def _decay_cumsum(z, bias, tb=512):
    s, n = z.shape
    return pl.pallas_call(
        functools.partial(_decay_kernel, tb=tb),
        grid=(s // tb,),
        in_specs=[pl.BlockSpec((tb, n), lambda i: (i, 0)),
                  pl.BlockSpec((1, n), lambda i: (0, 0))],
        out_specs=pl.BlockSpec((tb, n), lambda i: (i, 0)),
        out_shape=jax.ShapeDtypeStruct((s, n), F32),
        scratch_shapes=[pltpu.VMEM((1, n), F32)],
        compiler_params=_cparams(("arbitrary",)),
        name="decay_cumsum",
    )(z, bias)


def _fox_kernel(qt_ref, k_ref, vt_ref, o_ref, m_ref, acc_ref, s_even, s_odd):
    tq = FOX_TQ
    tk = FOX_TK
    assert tq == 2 * tk
    qb = pl.program_id(1)
    rhs = qt_ref[0]
    _flash_init_t(m_ref, acc_ref)

    def scores(kt, s_ref):
        k0 = pl.multiple_of(kt * tk, tk)
        s_ref[...] = _dot(k_ref[0, pl.ds(k0, tk), :], rhs)

    def consume(kt, s_ref, diagonal):
        k0 = pl.multiple_of(kt * tk, tk)
        s = s_ref[...]
        if diagonal:
            key = k0 + lax.broadcasted_iota(jnp.int32, (tk, tq), 0)
            qry = qb * tq + lax.broadcasted_iota(jnp.int32, (tk, tq), 1)
            s = jnp.where(key <= qry, s, NEG_INF)
        _flash_step_t(s, vt_ref[0, :, pl.ds(k0, tk)], m_ref, acc_ref)

    scores(0, s_even)

    def pair_body(j, carry):
        scores(2 * j + 1, s_odd)
        consume(2 * j, s_even, False)
        scores(2 * j + 2, s_even)
        consume(2 * j + 1, s_odd, False)
        return carry

    lax.fori_loop(0, qb, pair_body, 0)
    scores(2 * qb + 1, s_odd)
    consume(2 * qb, s_even, True)
    consume(2 * qb + 1, s_odd, True)
    o_ref[0] = _flash_result_t(acc_ref).astype(o_ref.dtype)


def _fox_attention(qt_aug, k_aug, vt_aug):
    h, _, s = qt_aug.shape
    tq = FOX_TQ
    resident = pl.Buffered(1)
    return pl.pallas_call(
        _fox_kernel,
        grid=(h, s // tq),
        in_specs=[pl.BlockSpec((1, AUG_K, tq), lambda hi, qi: (hi, 0, qi)),
                  pl.BlockSpec((1, s, AUG_K), lambda hi, qi: (hi, 0, 0), pipeline_mode=resident),
                  pl.BlockSpec((1, AUG_V, s), lambda hi, qi: (hi, 0, 0), pipeline_mode=resident)],
        out_specs=pl.BlockSpec((1, HEAD_DIM, tq), lambda hi, qi: (hi, 0, qi)),
        out_shape=jax.ShapeDtypeStruct((h, HEAD_DIM, s), BF16),
        scratch_shapes=[pltpu.VMEM((1, tq), F32), pltpu.VMEM((AUG_V, tq), F32),
                        pltpu.VMEM((FOX_TK, tq), F32), pltpu.VMEM((FOX_TK, tq), F32)],
        compiler_params=_cparams(("arbitrary", "arbitrary")),
        name="fox_attention",
    )(qt_aug, k_aug, vt_aug)


def _merge_kernel(on_ref, of_ref, mg_ref, x_ref, wn_ref, wf_ref, wo_ref, g1_ref, lg_ref, lb_ref, o_ref, *, alpha):
    d = x_ref.shape[-1]
    a = _dot(on_ref[...], wn_ref[...])
    b = _dot(of_ref[...], wf_ref[...])
    gm = _sigmoid(mg_ref[...].astype(F32))
    merged = gm[:, 0:d] * a + gm[:, d:2 * d] * b
    y = _dot(merged.astype(BF16), wo_ref[...])
    z = alpha * x_ref[...] + (1.0 + g1_ref[...]) * y
    o_ref[...] = _layer_norm(z) * lg_ref[...] + lb_ref[...]


def _merge_project(o_nsa, o_fox, merge, x, wn, wf, wo, g1, ln_g, ln_b, alpha, tm=256):
    m, d = x.shape
    w = o_nsa.shape[1]
    resident = pl.Buffered(1)
    row = lambda i: (i, 0)
    fixed = lambda i: (0, 0)
    return pl.pallas_call(
        functools.partial(_merge_kernel, alpha=alpha),
        grid=(m // tm,),
        in_specs=[pl.BlockSpec((tm, w), row), pl.BlockSpec((tm, w), row),
                  pl.BlockSpec((tm, 2 * d), row), pl.BlockSpec((tm, d), row),
                  pl.BlockSpec((w, d), fixed, pipeline_mode=resident),
                  pl.BlockSpec((w, d), fixed, pipeline_mode=resident),
                  pl.BlockSpec((d, d), fixed, pipeline_mode=resident),
                  pl.BlockSpec((1, d), fixed), pl.BlockSpec((1, d), fixed), pl.BlockSpec((1, d), fixed)],
        out_specs=pl.BlockSpec((tm, d), row),
        out_shape=jax.ShapeDtypeStruct((m, d), F32),
        compiler_params=_cparams(("parallel",)),
        name="merge_project_ln",
    )(o_nsa, o_fox, merge, x, wn, wf, wo, g1, ln_g, ln_b)


def _router_kernel(x_ref, sc_ref, sh_ref, w_ref, b_ref, u_ref, r_ref):
    u = _layer_norm(x_ref[...]) * (1.0 + sc_ref[...]) + sh_ref[...]
    u_ref[...] = u
    logits = jnp.dot(u, w_ref[...], precision=HIGHEST, preferred_element_type=F32) + b_ref[...]
    lane = lax.broadcasted_iota(jnp.int32, (1, LANES), 1).astype(F32)
    none = float(LANES)
    is_g = lane < N_GROUPS
    lg = jnp.where(is_g, logits, NEG_INF)
    eg = jnp.exp(lg - jnp.max(lg, axis=-1, keepdims=True))
    pg = eg / jnp.sum(eg, axis=-1, keepdims=True)
    p_grp = jnp.max(pg, axis=-1, keepdims=True)
    grp = jnp.min(jnp.where(pg == p_grp, lane, none), axis=-1, keepdims=True)
    lo = N_GROUPS + grp * EXPERTS_PER_GROUP
    is_e = (lane >= lo) & (lane < lo + EXPERTS_PER_GROUP)
    le = jnp.where(is_e, logits, NEG_INF)
    ee = jnp.exp(le - jnp.max(le, axis=-1, keepdims=True))
    pe = jnp.where(is_e, ee / jnp.sum(ee, axis=-1, keepdims=True), -1.0)
    p1 = jnp.max(pe, axis=-1, keepdims=True)
    i1 = jnp.min(jnp.where(pe == p1, lane, none), axis=-1, keepdims=True)
    pe2 = jnp.where(lane == i1, -1.0, pe)
    p2 = jnp.max(pe2, axis=-1, keepdims=True)
    i2 = jnp.min(jnp.where(pe2 == p2, lane, none), axis=-1, keepdims=True)
    den = p1 + p2
    r_ref[...] = jnp.where(lane == 0, i1 - N_GROUPS,
                           jnp.where(lane == 1, i2 - N_GROUPS,
                                     jnp.where(lane == 2, p_grp * p1 / den,
                                               jnp.where(lane == 3, p_grp * p2 / den, 0.0))))


def _router(x1, sc, sh, w_r, b_r, tm=256):
    m, d = x1.shape
    row = lambda i: (i, 0)
    fixed = lambda i: (0, 0)
    return pl.pallas_call(
        _router_kernel,
        grid=(m // tm,),
        in_specs=[pl.BlockSpec((tm, d), row), pl.BlockSpec((1, d), fixed), pl.BlockSpec((1, d), fixed),
                  pl.BlockSpec((d, LANES), fixed), pl.BlockSpec((1, LANES), fixed)],
        out_specs=[pl.BlockSpec((tm, d), row), pl.BlockSpec((tm, LANES), row)],
        out_shape=[jax.ShapeDtypeStruct((m, d), F32), jax.ShapeDtypeStruct((m, LANES), F32)],
        compiler_params=_cparams(("parallel",)),
        name="moe_router",
    )(x1, sc, sh, w_r, b_r)


def _moe_kernel(be_ref, tok_ref, tok_next_ref, dst_ref, rw_ref, u_hbm, wg_ref, wu_ref, wd_ref, out_hbm,
                xbuf, ybuf, wgb, wub, wdb, sem_in, sem_out, *, n_dump0):
    rb = ROW_BLOCK
    i = pl.program_id(0)
    last = pl.num_programs(0) - 1
    slot = i % 2

    def row_in(r, tok, sl):
        return pltpu.make_async_copy(u_hbm.at[pl.ds(tok, 1), :], xbuf.at[sl, pl.ds(r, 1), :], sem_in.at[sl])

    def row_out(r, dst):
        return pltpu.make_async_copy(ybuf.at[pl.ds(r, 1), :], out_hbm.at[pl.ds(dst, 1), :], sem_out)

    @pl.when(i == 0)
    def _():
        for r in range(rb):
            row_in(r, tok_ref[0, 0, r], 0).start()
        ybuf[...] = jnp.zeros(ybuf.shape, F32)
        pltpu.make_async_copy(ybuf, out_hbm.at[pl.ds(n_dump0, rb), :], sem_out).start()

    prev = be_ref[jnp.maximum(i - 1, 0)]

    @pl.when((i == 0) | (be_ref[i] != prev))
    def _():
        wgb[...] = wg_ref[0].astype(BF16)
        wub[...] = wu_ref[0].astype(BF16)
        wdb[...] = wd_ref[0].astype(BF16)

    for r in range(rb):
        row_in(r, 0, slot).wait()
    xb = xbuf[slot].astype(BF16)
    for r in range(rb):
        row_in(r, tok_next_ref[0, 0, r], 1 - slot).start()
    gate = _dot(xb, wgb[...])
    up = _dot(xb, wub[...])
    hid = (gate * _sigmoid(gate)) * up
    y = _dot(hid.astype(BF16), wdb[...]) * rw_ref[0]
    for r in range(rb):
        row_out(r, 0).wait()
    ybuf[...] = y
    for r in range(rb):
        row_out(r, dst_ref[0, 0, r]).start()

    @pl.when(i == last)
    def _():
        for r in range(rb):
            row_out(r, 0).wait()
        for r in range(rb):
            row_in(r, 0, 1 - slot).wait()


def _moe_experts(u, blk_exp, row_tok, row_dst, row_w, w_gate, w_up, w_down):
    t, d = u.shape
    n_blocks = blk_exp.shape[0]
    de = w_gate.shape[-1]
    rb = ROW_BLOCK
    tok3 = row_tok.reshape(n_blocks, 1, rb)
    grid_spec = pltpu.PrefetchScalarGridSpec(
        num_scalar_prefetch=1,
        grid=(n_blocks,),
        in_specs=[pl.BlockSpec((1, 1, rb), lambda i, be: (i, 0, 0), memory_space=pltpu.SMEM),
                  pl.BlockSpec((1, 1, rb), lambda i, be: (jnp.minimum(i + 1, n_blocks - 1), 0, 0),
                               memory_space=pltpu.SMEM),
                  pl.BlockSpec((1, 1, rb), lambda i, be: (i, 0, 0), memory_space=pltpu.SMEM),
                  pl.BlockSpec((1, rb, 1), lambda i, be: (i, 0, 0)),
                  pl.BlockSpec(memory_space=pl.ANY),
                  pl.BlockSpec((1, d, de), lambda i, be: (be[i], 0, 0)),
                  pl.BlockSpec((1, d, de), lambda i, be: (be[i], 0, 0)),
                  pl.BlockSpec((1, de, d), lambda i, be: (be[i], 0, 0))],
        out_specs=pl.BlockSpec(memory_space=pl.ANY),
        scratch_shapes=[pltpu.VMEM((2, rb, d), F32), pltpu.VMEM((rb, d), F32),
                        pltpu.VMEM((d, de), BF16), pltpu.VMEM((d, de), BF16), pltpu.VMEM((de, d), BF16),
                        pltpu.SemaphoreType.DMA((2,)), pltpu.SemaphoreType.DMA(())],
    )
    return pl.pallas_call(
        functools.partial(_moe_kernel, n_dump0=2 * t),
        grid_spec=grid_spec,
        out_shape=jax.ShapeDtypeStruct((2 * t + rb, d), F32),
        compiler_params=_cparams(("arbitrary",)),
        name="moe_experts",
    )(blk_exp, tok3, tok3, row_dst.reshape(n_blocks, 1, rb), row_w.reshape(n_blocks, rb, 1), u, w_gate, w_up, w_down)


def _moe_dispatch(route, t):
    k = 2
    eid = route[:, 0:k].astype(jnp.int32).reshape(-1)
    wts = route[:, k:2 * k].reshape(-1)
    n_asg = t * k
    n_rows = n_asg + N_EXPERTS * ROW_BLOCK
    n_blocks = n_rows // ROW_BLOCK
    onehot = (eid[:, None] == jnp.arange(N_EXPERTS, dtype=jnp.int32)[None, :]).astype(jnp.int32)
    rank = jnp.sum((jnp.cumsum(onehot, axis=0) - onehot) * onehot, axis=1)
    counts = jnp.sum(onehot, axis=0)
    padded = (counts + ROW_BLOCK - 1) // ROW_BLOCK * ROW_BLOCK
    pad_end = jnp.cumsum(padded)
    pad_start = pad_end - padded
    dest = jnp.sum(onehot * pad_start[None, :], axis=1) + rank
    asg = jnp.arange(n_asg, dtype=jnp.int32)
    upd = jnp.stack([(asg % k) * t + asg // k, lax.bitcast_convert_type(wts, jnp.int32)], axis=1)
    init = jnp.stack([n_asg + jnp.arange(n_rows, dtype=jnp.int32) % ROW_BLOCK,
                      jnp.zeros((n_rows,), jnp.int32)], axis=1)
    rows = init.at[dest].set(upd)
    row_dst = rows[:, 0]
    row_w = lax.bitcast_convert_type(rows[:, 1], F32)
    row_tok = jnp.where(row_dst < n_asg, row_dst % t, 0)
    blk_start = jnp.arange(n_blocks, dtype=jnp.int32) * ROW_BLOCK
    blk_exp = jnp.minimum(jnp.sum((pad_end[None, :] <= blk_start[:, None]).astype(jnp.int32), axis=1),
                          N_EXPERTS - 1)
    return blk_exp, row_tok, row_dst, row_w


def _final_kernel(x_ref, y0_ref, y1_ref, g2_ref, lg_ref, lb_ref, o_ref, *, alpha):
    z = alpha * x_ref[...] + (1.0 + g2_ref[...]) * (y0_ref[...] + y1_ref[...])
    o_ref[...] = _layer_norm(z) * lg_ref[...] + lb_ref[...]


def _final_ln(x1, y2, g2, ln_g, ln_b, alpha, tm=512):
    m, d = x1.shape
    nb = m // tm
    fixed = lambda i: (0, 0)
    return pl.pallas_call(
        functools.partial(_final_kernel, alpha=alpha),
        grid=(nb,),
        in_specs=[pl.BlockSpec((tm, d), lambda i: (i, 0)),
                  pl.BlockSpec((tm, d), lambda i: (i, 0)),
                  pl.BlockSpec((tm, d), lambda i: (i + nb, 0)),
                  pl.BlockSpec((1, d), fixed), pl.BlockSpec((1, d), fixed), pl.BlockSpec((1, d), fixed)],
        out_specs=pl.BlockSpec((tm, d), lambda i: (i, 0)),
        out_shape=jax.ShapeDtypeStruct((m, d), F32),
        compiler_params=_cparams(("parallel",)),
        name="final_ln",
    )(x1, y2, y2, g2, ln_g, ln_b)


def _to_lane_blocks(a, tq):
    g, hpg, s, c = a.shape
    return a.reshape(g, hpg, s // tq, tq, c).transpose(0, 2, 4, 1, 3).reshape(g, s // tq, c, hpg * tq)


def _layer(x2d, c, w_ada, b_ada, w_in, b_fgt, t5_table, cmp_pe, cmp_w1, cmp_b1, cmp_w2,
           w_br_nsa, w_br_fox, w_o, ln1_g, ln1_b, w_rg, b_rg, w_re, b_re,
           w_gate, w_up, w_down, ln2_g, ln2_b, alpha):
    s, d = x2d.shape
    hd = HEAD_DIM
    g = NSA_GROUPS
    mod = _ada_mod(c, w_ada, b_ada)
    sh1, sc1, g1, sh2, sc2, g2 = [mod[:, i * d:(i + 1) * d] for i in range(6)]

    c_q = NSA_HEADS * hd
    c_kv = 6 * g * hd
    c_gate = 3 * NSA_HEADS
    c_fox = 3 * FOX_HEADS * hd
    off_kv = c_q
    off_gate = off_kv + c_kv
    off_fox = off_gate + c_gate
    off_fgt = off_fox + c_fox
    off_merge = off_fgt + FOX_HEADS
    qscale = hd ** -0.5 * LOG2E
    w_attn = jnp.concatenate([w_in[:, 0:off_kv] * qscale, w_in[:, off_kv:off_gate],
                              w_in[:, off_fox:off_fox + FOX_HEADS * hd] * qscale,
                              w_in[:, off_fox + FOX_HEADS * hd:off_fgt]], axis=1).astype(BF16)
    n_small = c_gate + FOX_HEADS
    w_small = jnp.concatenate([w_in[:, off_gate:off_fox], w_in[:, off_fgt:off_merge],
                               jnp.zeros((d, LANES - n_small), F32)], axis=1).astype(BF16)
    w_merge = w_in[:, off_merge:].astype(BF16)

    u = _ln_mod(x2d, sc1, sh1, BF16)
    n_attn = w_attn.shape[1]
    proj = _matmul(u, w_attn, BF16, 512, n_attn // 2, "in_proj_attn")
    small = _matmul(u, w_small, F32, 512, LANES, "in_proj_small")
    merge = _matmul(u, w_merge, BF16, 512, 1024, "in_proj_merge")

    q_nsa = proj[:, 0:c_q].reshape(s, g, NSA_HPG, hd).transpose(1, 2, 0, 3)
    kv = proj[:, c_q:c_q + c_kv].reshape(s, 6, g, hd).transpose(1, 2, 0, 3)
    fox = proj[:, c_q + c_kv:].reshape(s, 3, FOX_HEADS, hd).transpose(1, 2, 0, 3)

    kv_cmp = _compress(kv[0:2], cmp_pe, cmp_w1, cmp_b1, cmp_w2)
    nch = s // CMP_STRIDE
    ncp = nch + LANES
    nslc = s // SLC_LEN
    kv_cmp_pad = jnp.pad(kv_cmp, ((0, 0), (0, 0), (CMP_PAD, ncp - nch - CMP_PAD), (0, 0)))
    b_cmp, far_cmp, b_sel, b_win, far3 = _nsa_bias_tables(t5_table)
    w_slc = jnp.asarray(_slc_weight_np(ncp, nslc))
    o_c, sel = _nsa_compress_select(q_nsa, kv_cmp_pad[0], kv_cmp_pad[1], b_cmp, far_cmp, w_slc, nch - 1)

    def ones(*shape):
        return jnp.ones(shape, BF16)

    def zeros(*shape):
        return jnp.zeros(shape, BF16)

    blk_onehot = jnp.asarray((np.arange(s)[:, None] // SLC_LEN % BF16_ROWS == np.arange(BF16_ROWS)[None, :])
                             .astype(np.float32), BF16)
    ks_aug = jnp.concatenate([kv[2], jnp.broadcast_to(blk_onehot, (g, s, BF16_ROWS)), ones(g, s, 3),
                              zeros(g, s, AUG_K - ROW_BIAS - 3)], axis=-1)
    kw_aug = jnp.concatenate([kv[4], zeros(g, s, AUG_K - hd)], axis=-1)
    vst_aug = jnp.concatenate([kv[3].transpose(0, 2, 1), ones(g, 8, s), zeros(g, 8, s)], axis=1)
    vwt_aug = jnp.concatenate([kv[5].transpose(0, 2, 1), ones(g, 8, s), zeros(g, 8, s)], axis=1)
    qt = _to_lane_blocks(q_nsa, ATT_TQ)
    oct_ = _to_lane_blocks(o_c, ATT_TQ)
    gates = small[:, 0:c_gate].reshape(s, g, NSA_HPG, 3).transpose(1, 2, 0, 3)
    gates_t = jnp.pad(_to_lane_blocks(gates, ATT_TQ), ((0, 0), (0, 0), (0, 5), (0, 0)))
    o_nsa_t = _nsa_attention(qt, ks_aug, vst_aug, kw_aug, vwt_aug, sel.transpose(0, 2, 1), far3,
                             b_sel, b_win, oct_, gates_t)
    nq = s // ATT_TQ
    o_nsa = o_nsa_t.reshape(g, nq, hd, NSA_HPG, ATT_TQ).transpose(1, 4, 0, 3, 2).reshape(s, MIX_W)

    fgt_bias = jnp.concatenate([jnp.zeros((c_gate,), F32), b_fgt, jnp.zeros((LANES - n_small,), F32)])[None, :]
    decay = _decay_cumsum(small, fgt_bias)[:, c_gate:c_gate + FOX_HEADS]
    nh = FOX_HEADS
    d_hi, d_mid, d_lo = _split3_exact(-decay.T * LOG2E)
    fox_k_aug = jnp.concatenate([fox[1], jnp.stack([d_hi, d_mid, d_lo], axis=-1), zeros(nh, s, AUG_K - hd - 3)],
                                axis=-1)
    fox_qt_aug = jnp.concatenate([fox[0].transpose(0, 2, 1), ones(nh, 3, s), zeros(nh, AUG_K - hd - 3, s)], axis=1)
    fox_vt_aug = jnp.concatenate([fox[2].transpose(0, 2, 1), ones(nh, 8, s), zeros(nh, 8, s)], axis=1)
    o_fox = _fox_attention(fox_qt_aug, fox_k_aug, fox_vt_aug)
    o_fox = o_fox.transpose(2, 0, 1).reshape(s, MIX_W)

    x1 = _merge_project(o_nsa, o_fox, merge, x2d, w_br_nsa.astype(BF16), w_br_fox.astype(BF16),
                        w_o.astype(BF16), g1, ln1_g[None, :], ln1_b[None, :], alpha)

    n_r = N_GROUPS + N_EXPERTS
    w_r = jnp.concatenate([w_rg, w_re.reshape(d, N_EXPERTS), jnp.zeros((d, LANES - n_r), F32)], axis=1)
    b_r = jnp.concatenate([b_rg, b_re.reshape(N_EXPERTS), jnp.zeros((LANES - n_r,), F32)])[None, :]
    u2, route = _router(x1, sc2, sh2, w_r, b_r)
    blk_exp, row_tok, row_dst, row_w = _moe_dispatch(route, s)
    y2 = _moe_experts(u2, blk_exp, row_tok, row_dst, row_w, w_gate, w_up, w_down)
    return _final_ln(x1, y2, g2, ln2_g[None, :], ln2_b[None, :], alpha)


def kernel(x, c, w_ada, b_ada, w_in, b_fgt, t5_table, cmp_pe, cmp_w1, cmp_b1, cmp_w2, w_br_nsa, w_br_fox, w_o,
           ln1_g, ln1_b, w_rg, b_rg, w_re, b_re, w_gate, w_up, w_down, ln2_g, ln2_b):
    b, s, d = x.shape
    depth = w_ada.shape[0]
    assert b == 1
    alpha = (2 * depth) ** 0.25
    h = x[0]
    for l in range(depth):
        h = _layer(h, c, w_ada[l], b_ada[l], w_in[l], b_fgt[l], t5_table, cmp_pe[l], cmp_w1[l], cmp_b1[l],
                   cmp_w2[l], w_br_nsa[l], w_br_fox[l], w_o[l], ln1_g[l], ln1_b[l], w_rg[l], b_rg[l],
                   w_re[l], b_re[l], w_gate[l], w_up[l], w_down[l], ln2_g[l], ln2_b[l], alpha)
    return h[None]
```

```python
import functools
import math

import numpy as np
import jax
import jax.numpy as jnp
from jax import lax
from jax.experimental import pallas as pl
from jax.experimental.pallas import tpu as pltpu

F32 = jnp.float32
BF16 = jnp.bfloat16
HIGHEST = lax.Precision.HIGHEST
LOG2E = math.log2(math.e)

HEAD_DIM = 64
NSA_HEADS = 8
NSA_GROUPS = 2
NSA_HPG = NSA_HEADS // NSA_GROUPS
FOX_HEADS = 8
MIX_W = NSA_HEADS * HEAD_DIM
CMP_LEN = 32
CMP_STRIDE = 16
SLC_LEN = 64
SLC_TOPK = 16
WINDOW = 512
T5_BUCKETS = 32
T5_MAX_EXACT = 16
T5_MAX_DIST = 128
N_GROUPS = 8
EXPERTS_PER_GROUP = 8
N_EXPERTS = N_GROUPS * EXPERTS_PER_GROUP
ROW_BLOCK = 128
LN_EPS = 1e-5
NEG_INF = -1e30
M_INIT = -1e29
FORCE_SCORE = 1e4

LANES = 128
BF16_ROWS = 16
CMP_PAD = 8
ATT_TQ = 256
CMP_NEAR_ROWS = 32
FOX_TQ = 1024
FOX_TK = 512
AUG_K = 128
AUG_V = HEAD_DIM + 16
ROW_MASK = HEAD_DIM
ROW_BIAS = HEAD_DIM + 16
VMEM_LIMIT = 56 * 1024 * 1024


def _cparams(sem, vmem=VMEM_LIMIT):
    return pltpu.CompilerParams(dimension_semantics=sem, vmem_limit_bytes=vmem)


def _sigmoid(x):
    return 1.0 / (1.0 + jnp.exp(-x))


def _layer_norm(x):
    mu = jnp.mean(x, axis=-1, keepdims=True)
    xc = x - mu
    var = jnp.mean(xc * xc, axis=-1, keepdims=True)
    return xc * lax.rsqrt(var + LN_EPS)


def _split3(x):
    hi = x.astype(BF16)
    r1 = x - hi.astype(F32)
    mid = r1.astype(BF16)
    lo = (r1 - mid.astype(F32)).astype(BF16)
    return hi, mid, lo


def _split3_exact(x):
    def trunc(v):
        bits = lax.bitcast_convert_type(v, jnp.uint32) & jnp.uint32(0xFFFF0000)
        return lax.bitcast_convert_type(bits, F32)
    hi = trunc(x)
    r1 = x - hi
    mid = trunc(r1)
    lo = r1 - mid
    return hi.astype(BF16), mid.astype(BF16), lo.astype(BF16)


def _dot(a, b):
    return jnp.dot(a, b, preferred_element_type=F32)


def _dot_nt(a, b):
    return lax.dot_general(a, b, (((1,), (1,)), ((), ())), preferred_element_type=F32)


def _dot3(x, w_bf16):
    hi, mid, lo = _split3(x)
    return _dot(hi, w_bf16) + _dot(mid, w_bf16) + _dot(lo, w_bf16)


def _dot3_rhs(w_bf16, x):
    hi, mid, lo = _split3(x)
    return _dot(w_bf16, hi) + _dot(w_bf16, mid) + _dot(w_bf16, lo)


def _ada_kernel(c_ref, w_ref, b_ref, o_ref):
    c = c_ref[...]
    a = c * _sigmoid(c)
    o_ref[...] = jnp.dot(a, w_ref[...], precision=HIGHEST, preferred_element_type=F32) + b_ref[...]


def _ada_mod(c, w, b):
    d, n = w.shape
    tn = 1024
    c8 = jnp.broadcast_to(c, (8, d))
    out = pl.pallas_call(
        _ada_kernel,
        grid=(n // tn,),
        in_specs=[pl.BlockSpec((8, d), lambda j: (0, 0)),
                  pl.BlockSpec((d, tn), lambda j: (0, j)),
                  pl.BlockSpec((1, tn), lambda j: (0, j))],
        out_specs=pl.BlockSpec((8, tn), lambda j: (0, j)),
        out_shape=jax.ShapeDtypeStruct((8, n), F32),
        compiler_params=_cparams(("parallel",)),
        name="ada_mod",
    )(c8, w, b.reshape(1, n))
    return out[0:1]


def _lnmod_kernel(x_ref, sc_ref, sh_ref, o_ref):
    y = _layer_norm(x_ref[...])
    o_ref[...] = (y * (1.0 + sc_ref[...]) + sh_ref[...]).astype(o_ref.dtype)


def _ln_mod(x, sc, sh, out_dtype, tm=512):
    m, d = x.shape
    return pl.pallas_call(
        _lnmod_kernel,
        grid=(m // tm,),
        in_specs=[pl.BlockSpec((tm, d), lambda i: (i, 0)),
                  pl.BlockSpec((1, d), lambda i: (0, 0)),
                  pl.BlockSpec((1, d), lambda i: (0, 0))],
        out_specs=pl.BlockSpec((tm, d), lambda i: (i, 0)),
        out_shape=jax.ShapeDtypeStruct((m, d), out_dtype),
        compiler_params=_cparams(("parallel",)),
        name="ln_mod",
    )(x, sc, sh)


def _mm_kernel(a_ref, w_ref, o_ref):
    o_ref[...] = _dot(a_ref[...], w_ref[...]).astype(o_ref.dtype)


def _matmul(a, w, out_dtype, tm, tn, name):
    m, k = a.shape
    n = w.shape[1]
    return pl.pallas_call(
        _mm_kernel,
        grid=(n // tn, m // tm),
        in_specs=[pl.BlockSpec((tm, k), lambda j, i: (i, 0)),
                  pl.BlockSpec((k, tn), lambda j, i: (0, j))],
        out_specs=pl.BlockSpec((tm, tn), lambda j, i: (i, j)),
        out_shape=jax.ShapeDtypeStruct((m, n), out_dtype),
        compiler_params=_cparams(("parallel", "parallel")),
        name=name,
    )(a, w)


def _gelu_tanh(x):
    return 0.5 * x * (1.0 + jnp.tanh(math.sqrt(2.0 / math.pi) * (x + 0.044715 * (x * x * x))))


def _compress_kernel(c_ref, pe_ref, w1a_ref, w1b_ref, b1_ref, w2_ref, o_ref, *, nch):
    c = c_ref[0, 0]
    w1a = w1a_ref[0]
    w1b = w1b_ref[0]
    half = CMP_STRIDE * HEAD_DIM
    a = _dot(c, w1a)
    b = _dot(c, w1b)
    b_next = pltpu.roll(b, shift=nch - 1, axis=0)
    pe = pe_ref[0]
    pb = _dot(pe[:, :half], w1a) + _dot(pe[:, half:], w1b)
    hid = _gelu_tanh(a + b_next + pb[0:1, :] + b1_ref[0])
    o_ref[0, 0] = _dot(hid.astype(BF16), w2_ref[0])


def _compress(kv_cmp, pe, w1, b1, w2):
    z, g, s, hd = kv_cmp.shape
    nch = s // CMP_STRIDE
    half = CMP_STRIDE * hd
    chunks = kv_cmp.reshape(z, g, nch, half)
    pe8 = jnp.broadcast_to(pe.reshape(z, 1, CMP_LEN * hd), (z, 8, CMP_LEN * hd)).astype(BF16)
    w1b16 = w1.astype(BF16)
    hidn = w1.shape[-1]
    return pl.pallas_call(
        functools.partial(_compress_kernel, nch=nch),
        grid=(z, g),
        in_specs=[pl.BlockSpec((1, 1, nch, half), lambda zi, gi: (zi, gi, 0, 0)),
                  pl.BlockSpec((1, 8, 2 * half), lambda zi, gi: (zi, 0, 0)),
                  pl.BlockSpec((1, half, hidn), lambda zi, gi: (zi, 0, 0)),
                  pl.BlockSpec((1, half, hidn), lambda zi, gi: (zi, 1, 0)),
                  pl.BlockSpec((1, 1, hidn), lambda zi, gi: (zi, 0, 0)),
                  pl.BlockSpec((1, hidn, hd), lambda zi, gi: (zi, 0, 0))],
        out_specs=pl.BlockSpec((1, 1, nch, hd), lambda zi, gi: (zi, gi, 0, 0)),
        out_shape=jax.ShapeDtypeStruct((z, g, nch, hd), F32),
        compiler_params=_cparams(("parallel", "parallel")),
        name="compress_kv",
    )(chunks, pe8, w1b16, w1b16, b1.reshape(z, 1, hidn), w2.astype(BF16))


def _t5_bucket_np(dist):
    n = np.maximum(dist, 0)
    ratio = np.log(np.maximum(n, T5_MAX_EXACT).astype(np.float64) / T5_MAX_EXACT)
    big = T5_MAX_EXACT + (ratio / math.log(T5_MAX_DIST / T5_MAX_EXACT)
                          * (T5_BUCKETS - T5_MAX_EXACT)).astype(np.int64)
    return np.where(n < T5_MAX_EXACT, n, np.minimum(big, T5_BUCKETS - 1)).astype(np.int32)


def _toeplitz_t(w, n_keys, n_q):
    length = n_keys + n_q - 1
    w_pad = jnp.concatenate([w, jnp.zeros(w.shape[:-1] + (1,), w.dtype)], axis=-1)
    reps = (1,) * (w.ndim - 1) + (n_keys,)
    flat = jnp.tile(w_pad, reps)[..., :n_keys * length]
    return flat.reshape(w.shape[:-1] + (n_keys, length))[..., n_keys - 1:n_keys - 1 + n_q]


def _att_table_t(tbh, n_keys, tq, lo, hi, minus_far):
    d = np.arange(n_keys + tq - 1) - (tq - 1)
    valid = (d >= lo) & (d < hi)
    vals = tbh[:, :, _t5_bucket_np(d)]
    if minus_far:
        vals = vals - tbh[:, :, T5_BUCKETS - 1:]
    w = jnp.where(jnp.asarray(valid), vals * LOG2E, NEG_INF)
    t = _toeplitz_t(w, n_keys, tq)
    g, hpg = tbh.shape[:2]
    return t.transpose(0, 2, 1, 3).reshape(g, n_keys, hpg * tq)


def _nsa_bias_tables(t5_table):
    tbh = t5_table.T.reshape(NSA_GROUPS, NSA_HPG, T5_BUCKETS).astype(F32)
    j = np.arange(CMP_NEAR_ROWS)[:, None]
    i = np.arange(ATT_TQ)[None, :]
    dist = i - (CMP_LEN - 1) - CMP_STRIDE * (j - CMP_PAD)
    vals = (tbh[:, :, _t5_bucket_np(dist)] - tbh[:, :, T5_BUCKETS - 1][:, :, None, None]) * LOG2E
    vals = jnp.where(jnp.asarray(dist >= 0)[None, None], vals, NEG_INF)
    b_cmp = vals.transpose(0, 2, 1, 3).reshape(NSA_GROUPS, CMP_NEAR_ROWS, NSA_HPG * ATT_TQ)
    far = tbh[:, :, T5_BUCKETS - 1] * LOG2E
    b_sel = _att_table_t(tbh, 2 * ATT_TQ, ATT_TQ, 0, 1 << 30, True)
    b_sel = jnp.pad(b_sel, ((0, 0), (ATT_TQ, 0), (0, 0)))
    b_win = _att_table_t(tbh, 3 * ATT_TQ, ATT_TQ, 0, WINDOW, False)
    b_win = jnp.pad(b_win, ((0, 0), (0, ATT_TQ), (0, 0)), constant_values=NEG_INF)
    hi, mid, lo = _split3_exact(jnp.repeat(far, ATT_TQ, axis=1))
    far3 = jnp.stack([hi, mid, lo, jnp.full(hi.shape, NEG_INF, BF16)], axis=1)
    far3 = jnp.pad(far3, ((0, 0), (0, BF16_ROWS - 4), (0, 0)))
    return b_cmp, b_sel, b_win, far3


def _slc_weight_np(ncp, nslc):
    w = np.zeros((ncp, nslc), np.float32)
    ratio = SLC_LEN // CMP_STRIDE
    for off, val in ((-1, 0.5), (0, 1.0), (1, 1.0), (2, 1.0), (3, 0.5)):
        j = np.arange(nslc)
        n = ratio * j + off
        ok = (n >= 0) & (n + CMP_PAD < ncp)
        w[n[ok] + CMP_PAD, j[ok]] += val
    return w


def _nsa_cmp_kernel(qt_ref, kc_ref, vct_ref, near_ref, far3_ref, oct_ref, selt_ref, rhs, s_scr, imp_scr, *, nslc):
    tq = ATT_TQ
    lanes = NSA_HPG * tq
    qb = pl.program_id(1)
    cpb = tq // CMP_STRIDE
    rhs[...] = jnp.zeros(rhs.shape, BF16)
    rhs[0:HEAD_DIM, :] = qt_ref[0, 0]
    rhs[ROW_BIAS:ROW_BIAS + BF16_ROWS, :] = far3_ref[0]
    chunk = lax.broadcasted_iota(jnp.int32, (AUG_K, lanes), 0)
    rhs[AUG_K:2 * AUG_K, :] = jnp.where(chunk >= qb + CMP_NEAR_ROWS // BF16_ROWS, NEG_INF, 0.0).astype(BF16)
    s_scr[...] = _dot(kc_ref[0], rhs[...])
    r0 = pl.multiple_of(cpb * qb, BF16_ROWS)
    s_scr[pl.ds(r0, CMP_NEAR_ROWS), :] = s_scr[pl.ds(r0, CMP_NEAR_ROWS), :] + near_ref[0]
    s = s_scr[...]
    m = jnp.max(s, axis=0, keepdims=True)
    e = jnp.exp2(s - m)
    l = jnp.sum(e, axis=0, keepdims=True)
    p = e * jnp.where(m > M_INIT, 1.0 / l, 0.0)
    oct_ref[0, 0] = _dot(vct_ref[0], p.astype(BF16))
    imp = p[:, 0:tq]
    for h in range(1, NSA_HPG):
        imp = imp + p[:, h * tq:(h + 1) * tq]
    n_lane_blocks = tq // LANES
    for c in range(n_lane_blocks):
        imp_scr[c] = imp[:, c * LANES:(c + 1) * LANES]
    ratio = SLC_LEN // CMP_STRIDE

    def taps(off):
        return jnp.concatenate([imp_scr[c, pl.ds(CMP_PAD + off, nslc, stride=ratio), :]
                                for c in range(n_lane_blocks)], axis=1)

    p_slc = 0.5 * (taps(-1) + taps(ratio - 1))
    for off in range(ratio - 1):
        p_slc = p_slc + taps(off)
    blk = lax.broadcasted_iota(jnp.int32, (nslc, tq), 0)
    cur = (qb * tq + lax.broadcasted_iota(jnp.int32, (nslc, tq), 1)) // SLC_LEN
    forced = (blk == 0) | (blk == cur) | (blk == cur - 1)
    score = jnp.where(forced, FORCE_SCORE, jnp.where(blk <= cur, p_slc, -1.0))
    blk_f = blk.astype(F32)
    sel = jnp.zeros((nslc, tq), F32)
    for _ in range(min(SLC_TOPK, nslc)):
        mx = jnp.max(score, axis=0, keepdims=True)
        first = jnp.min(jnp.where(score == mx, blk_f, float(nslc)), axis=0, keepdims=True)
        hit = blk_f == first
        sel = jnp.where(hit, 1.0, sel)
        score = jnp.where(hit, -2.0, score)
    selt_ref[0] = sel.astype(BF16)


def _nsa_compress_select(qt, kc_aug, vct, b_cmp, far3):
    g, nq, hd, lanes = qt.shape
    ncp = kc_aug.shape[1]
    tq = ATT_TQ
    nslc = nq * tq // SLC_LEN
    assert ncp // BF16_ROWS <= AUG_K
    assert tq // CMP_STRIDE == BF16_ROWS
    per_q = lambda gi, qi: (gi, qi, 0, 0)
    per_g = lambda gi, qi: (gi, 0, 0)
    return pl.pallas_call(
        functools.partial(_nsa_cmp_kernel, nslc=nslc),
        grid=(g, nq),
        in_specs=[pl.BlockSpec((1, 1, hd, lanes), per_q),
                  pl.BlockSpec((1, ncp, 2 * AUG_K), per_g),
                  pl.BlockSpec((1, hd, ncp), per_g),
                  pl.BlockSpec((1, CMP_NEAR_ROWS, lanes), per_g),
                  pl.BlockSpec((1, BF16_ROWS, lanes), per_g)],
        out_specs=[pl.BlockSpec((1, 1, hd, lanes), per_q),
                   pl.BlockSpec((1, nslc, tq), lambda gi, qi: (gi, 0, qi))],
        out_shape=[jax.ShapeDtypeStruct((g, nq, hd, lanes), F32),
                   jax.ShapeDtypeStruct((g, nslc, nq * tq), BF16)],
        scratch_shapes=[pltpu.VMEM((2 * AUG_K, lanes), BF16), pltpu.VMEM((ncp, lanes), F32),
                        pltpu.VMEM((tq // LANES, ncp, LANES), F32)],
        compiler_params=_cparams(("parallel", "parallel")),
        name="nsa_compress_select",
    )(qt, kc_aug, vct, b_cmp, far3)


def _flash_init_t(m_ref, acc_ref):
    m_ref[...] = jnp.full(m_ref.shape, M_INIT, F32)
    acc_ref[...] = jnp.zeros(acc_ref.shape, F32)


def _flash_step_t(s, vt_tile, m_ref, acc_ref):
    m_old = m_ref[...]
    m_new = jnp.maximum(m_old, jnp.max(s, axis=0, keepdims=True))
    p = jnp.exp2(s - m_new).astype(BF16)
    acc_ref[...] = jnp.exp2(m_old - m_new) * acc_ref[...] + _dot(vt_tile, p)
    m_ref[...] = m_new


def _flash_result_t(acc_ref):
    acc = acc_ref[...]
    return acc[0:HEAD_DIM, :] / acc[HEAD_DIM:HEAD_DIM + 1, :]


def _nsa_att_kernel(qt_ref, ks_ref, vst_ref, kw_ref, vwt_ref, selt_ref, far3_ref, bsel_ref, bwin_ref,
                    oct_ref, gate_ref, o_ref, rhs_s, rhs_w, mask_t, ms, accs, mw, accw, s_even, s_odd):
    tq = ATT_TQ
    qb = pl.program_id(1)
    qt = qt_ref[0, 0]
    rhs_s[...] = jnp.zeros(rhs_s.shape, BF16)
    rhs_s[0:HEAD_DIM, :] = qt
    rhs_s[ROW_BIAS:ROW_BIAS + BF16_ROWS, :] = far3_ref[0]
    rhs_w[...] = jnp.zeros(rhs_w.shape, BF16)
    rhs_w[0:HEAD_DIM, :] = qt
    madd = ((selt_ref[0].astype(F32) - 1.0) * (-NEG_INF)).astype(BF16)
    mask_t[...] = jnp.concatenate([madd] * NSA_HPG, axis=1)
    _flash_init_t(ms, accs)
    _flash_init_t(mw, accw)
    blocks_per_tile = tq // SLC_LEN

    def sel_scores(kt, s_ref):
        kt = jnp.minimum(kt, qb)
        k0 = pl.multiple_of(kt * tq, tq)
        chunk = pl.multiple_of((kt * blocks_per_tile) // BF16_ROWS * BF16_ROWS, BF16_ROWS)
        rhs_s[ROW_MASK:ROW_MASK + BF16_ROWS, :] = mask_t[pl.ds(chunk, BF16_ROWS), :]
        rel = jnp.clip(kt - qb + 2, 0, 2)
        table = bsel_ref[0, pl.ds(pl.multiple_of(rel * tq, tq), tq), :]
        s_ref[...] = _dot(ks_ref[0, pl.ds(k0, tq), :], rhs_s[...]) + table

    def sel_consume(kt, s_ref):
        k0 = pl.multiple_of(kt * tq, tq)
        _flash_step_t(s_ref[...], vst_ref[0, :, pl.ds(k0, tq)], ms, accs)

    def win_scores(j, s_ref):
        kt = qb - 2 + j
        k0 = pl.multiple_of(jnp.maximum(kt, 0) * tq, tq)
        row = pl.multiple_of(jnp.where(kt >= 0, j, 3) * tq, tq)
        s_ref[...] = _dot(kw_ref[0, pl.ds(k0, tq), :], rhs_w[...]) + bwin_ref[0, pl.ds(row, tq), :]

    def win_consume(j, s_ref):
        k0 = pl.multiple_of(jnp.maximum(qb - 2 + j, 0) * tq, tq)
        _flash_step_t(s_ref[...], vwt_ref[0, :, pl.ds(k0, tq)], mw, accw)

    n_sel = qb + 1
    sel_scores(0, s_even)

    def pair_body(j, carry):
        sel_scores(2 * j + 1, s_odd)
        sel_consume(2 * j, s_even)
        sel_scores(2 * j + 2, s_even)
        sel_consume(2 * j + 1, s_odd)
        return carry

    lax.fori_loop(0, n_sel // 2, pair_body, 0)
    win_scores(0, s_odd)

    @pl.when(n_sel % 2 == 1)
    def _():
        sel_consume(qb, s_even)

    win_scores(1, s_even)
    win_consume(0, s_odd)
    win_scores(2, s_odd)
    win_consume(1, s_even)
    win_consume(2, s_odd)

    gt = _sigmoid(gate_ref[0, 0])
    out = gt[0:1, :] * oct_ref[0, 0] + gt[1:2, :] * _flash_result_t(accs) + gt[2:3, :] * _flash_result_t(accw)
    o_ref[0, 0] = out.astype(o_ref.dtype)


def _nsa_attention(qt, ks_aug, vst_aug, kw_aug, vwt_aug, selt, far3, b_sel, b_win, oct_, gates_t):
    g, nq, hd, lanes = qt.shape
    s = ks_aug.shape[1]
    nslc = selt.shape[1]
    tq = ATT_TQ
    resident = pl.Buffered(1)
    per_q = lambda gi, qi: (gi, qi, 0, 0)
    per_g = lambda gi, qi: (gi, 0, 0)
    return pl.pallas_call(
        _nsa_att_kernel,
        grid=(g, nq),
        in_specs=[pl.BlockSpec((1, 1, hd, lanes), per_q),
                  pl.BlockSpec((1, s, AUG_K), per_g, pipeline_mode=resident),
                  pl.BlockSpec((1, AUG_V, s), per_g, pipeline_mode=resident),
                  pl.BlockSpec((1, s, AUG_K), per_g, pipeline_mode=resident),
                  pl.BlockSpec((1, AUG_V, s), per_g, pipeline_mode=resident),
                  pl.BlockSpec((1, nslc, tq), lambda gi, qi: (gi, 0, qi)),
                  pl.BlockSpec((1, BF16_ROWS, lanes), per_g),
                  pl.BlockSpec((1, 3 * tq, lanes), per_g, pipeline_mode=resident),
                  pl.BlockSpec((1, 4 * tq, lanes), per_g, pipeline_mode=resident),
                  pl.BlockSpec((1, 1, hd, lanes), per_q),
                  pl.BlockSpec((1, 1, 8, lanes), per_q)],
        out_specs=pl.BlockSpec((1, 1, hd, lanes), per_q),
        out_shape=jax.ShapeDtypeStruct((g, nq, hd, lanes), BF16),
        scratch_shapes=[pltpu.VMEM((AUG_K, lanes), BF16), pltpu.VMEM((AUG_K, lanes), BF16),
                        pltpu.VMEM((nslc, lanes), BF16),
                        pltpu.VMEM((1, lanes), F32), pltpu.VMEM((AUG_V, lanes), F32),
                        pltpu.VMEM((1, lanes), F32), pltpu.VMEM((AUG_V, lanes), F32),
                        pltpu.VMEM((tq, lanes), F32), pltpu.VMEM((tq, lanes), F32)],
        compiler_params=_cparams(("arbitrary", "arbitrary")),
        name="nsa_select_window",
    )(qt, ks_aug, vst_aug, kw_aug, vwt_aug, selt, far3, b_sel, b_win, oct_, gates_t)


def _decay_kernel(z_ref, b_ref, o_ref, carry_ref, *, tb):
    @pl.when(pl.program_id(0) == 0)
    def _():
        carry_ref[...] = jnp.zeros(carry_ref.shape, F32)

    z = z_ref[...] + b_ref[...]
    log_f = jnp.minimum(z, 0.0) - jnp.log1p(jnp.exp(-jnp.abs(z)))
    r = lax.broadcasted_iota(jnp.int32, (tb, tb), 0)
    c = lax.broadcasted_iota(jnp.int32, (tb, tb), 1)
    tri = jnp.where(r >= c, 1.0, 0.0).astype(BF16)
    run = _dot3_rhs(tri, log_f) + carry_ref[...]
    o_ref[...] = run
    carry_ref[...] = run[tb - 1:tb, :]


def _decay_cumsum(z, bias, tb=512):
    s, n = z.shape
    return pl.pallas_call(
        functools.partial(_decay_kernel, tb=tb),
        grid=(s // tb,),
        in_specs=[pl.BlockSpec((tb, n), lambda i: (i, 0)),
                  pl.BlockSpec((1, n), lambda i: (0, 0))],
        out_specs=pl.BlockSpec((tb, n), lambda i: (i, 0)),
        out_shape=jax.ShapeDtypeStruct((s, n), F32),
        scratch_shapes=[pltpu.VMEM((1, n), F32)],
        compiler_params=_cparams(("arbitrary",)),
        name="decay_cumsum",
    )(z, bias)


def _fox_kernel(qt_ref, k_ref, vt_ref, o_ref, m_ref, acc_ref, s_even, s_odd):
    tq = FOX_TQ
    tk = FOX_TK
    assert tq == 2 * tk
    qb = pl.program_id(1)
    rhs = qt_ref[0]
    _flash_init_t(m_ref, acc_ref)

    def scores(kt, s_ref):
        k0 = pl.multiple_of(kt * tk, tk)
        s_ref[...] = _dot(k_ref[0, pl.ds(k0, tk), :], rhs)

    def consume(kt, s_ref, diagonal):
        k0 = pl.multiple_of(kt * tk, tk)
        s = s_ref[...]
        if diagonal:
            key = k0 + lax.broadcasted_iota(jnp.int32, (tk, tq), 0)
            qry = qb * tq + lax.broadcasted_iota(jnp.int32, (tk, tq), 1)
            s = jnp.where(key <= qry, s, NEG_INF)
        _flash_step_t(s, vt_ref[0, :, pl.ds(k0, tk)], m_ref, acc_ref)

    scores(0, s_even)

    def pair_body(j, carry):
        scores(2 * j + 1, s_odd)
        consume(2 * j, s_even, False)
        scores(2 * j + 2, s_even)
        consume(2 * j + 1, s_odd, False)
        return carry

    lax.fori_loop(0, qb, pair_body, 0)
    scores(2 * qb + 1, s_odd)
    consume(2 * qb, s_even, True)
    consume(2 * qb + 1, s_odd, True)
    o_ref[0] = _flash_result_t(acc_ref).astype(o_ref.dtype)


def _fox_attention(qt_aug, k_aug, vt_aug):
    h, _, s = qt_aug.shape
    tq = FOX_TQ
    resident = pl.Buffered(1)
    return pl.pallas_call(
        _fox_kernel,
        grid=(h, s // tq),
        in_specs=[pl.BlockSpec((1, AUG_K, tq), lambda hi, qi: (hi, 0, qi)),
                  pl.BlockSpec((1, s, AUG_K), lambda hi, qi: (hi, 0, 0), pipeline_mode=resident),
                  pl.BlockSpec((1, AUG_V, s), lambda hi, qi: (hi, 0, 0), pipeline_mode=resident)],
        out_specs=pl.BlockSpec((1, HEAD_DIM, tq), lambda hi, qi: (hi, 0, qi)),
        out_shape=jax.ShapeDtypeStruct((h, HEAD_DIM, s), BF16),
        scratch_shapes=[pltpu.VMEM((1, tq), F32), pltpu.VMEM((AUG_V, tq), F32),
                        pltpu.VMEM((FOX_TK, tq), F32), pltpu.VMEM((FOX_TK, tq), F32)],
        compiler_params=_cparams(("arbitrary", "arbitrary")),
        name="fox_attention",
    )(qt_aug, k_aug, vt_aug)


def _merge_kernel(on_ref, of_ref, mg_ref, x_ref, wn_ref, wf_ref, wo_ref, g1_ref, lg_ref, lb_ref, o_ref, *, alpha):
    d = x_ref.shape[-1]
    a = _dot(on_ref[...], wn_ref[...])
    b = _dot(of_ref[...], wf_ref[...])
    gm = _sigmoid(mg_ref[...].astype(F32))
    merged = gm[:, 0:d] * a + gm[:, d:2 * d] * b
    y = _dot(merged.astype(BF16), wo_ref[...])
    z = alpha * x_ref[...] + (1.0 + g1_ref[...]) * y
    o_ref[...] = _layer_norm(z) * lg_ref[...] + lb_ref[...]


def _merge_project(o_nsa, o_fox, merge, x, wn, wf, wo, g1, ln_g, ln_b, alpha, tm=256):
    m, d = x.shape
    w = o_nsa.shape[1]
    resident = pl.Buffered(1)
    row = lambda i: (i, 0)
    fixed = lambda i: (0, 0)
    return pl.pallas_call(
        functools.partial(_merge_kernel, alpha=alpha),
        grid=(m // tm,),
        in_specs=[pl.BlockSpec((tm, w), row), pl.BlockSpec((tm, w), row),
                  pl.BlockSpec((tm, 2 * d), row), pl.BlockSpec((tm, d), row),
                  pl.BlockSpec((w, d), fixed, pipeline_mode=resident),
                  pl.BlockSpec((w, d), fixed, pipeline_mode=resident),
                  pl.BlockSpec((d, d), fixed, pipeline_mode=resident),
                  pl.BlockSpec((1, d), fixed), pl.BlockSpec((1, d), fixed), pl.BlockSpec((1, d), fixed)],
        out_specs=pl.BlockSpec((tm, d), row),
        out_shape=jax.ShapeDtypeStruct((m, d), F32),
        compiler_params=_cparams(("parallel",)),
        name="merge_project_ln",
    )(o_nsa, o_fox, merge, x, wn, wf, wo, g1, ln_g, ln_b)


def _router_kernel(x_ref, sc_ref, sh_ref, w_ref, b_ref, u_ref, r_ref):
    u = _layer_norm(x_ref[...]) * (1.0 + sc_ref[...]) + sh_ref[...]
    u_ref[...] = u
    logits = jnp.dot(u, w_ref[...], precision=HIGHEST, preferred_element_type=F32) + b_ref[...]
    lane = lax.broadcasted_iota(jnp.int32, (1, LANES), 1).astype(F32)
    none = float(LANES)
    is_g = lane < N_GROUPS
    lg = jnp.where(is_g, logits, NEG_INF)
    eg = jnp.exp(lg - jnp.max(lg, axis=-1, keepdims=True))
    pg = eg / jnp.sum(eg, axis=-1, keepdims=True)
    p_grp = jnp.max(pg, axis=-1, keepdims=True)
    grp = jnp.min(jnp.where(pg == p_grp, lane, none), axis=-1, keepdims=True)
    lo = N_GROUPS + grp * EXPERTS_PER_GROUP
    is_e = (lane >= lo) & (lane < lo + EXPERTS_PER_GROUP)
    le = jnp.where(is_e, logits, NEG_INF)
    ee = jnp.exp(le - jnp.max(le, axis=-1, keepdims=True))
    pe = jnp.where(is_e, ee / jnp.sum(ee, axis=-1, keepdims=True), -1.0)
    p1 = jnp.max(pe, axis=-1, keepdims=True)
    i1 = jnp.min(jnp.where(pe == p1, lane, none), axis=-1, keepdims=True)
    pe2 = jnp.where(lane == i1, -1.0, pe)
    p2 = jnp.max(pe2, axis=-1, keepdims=True)
    i2 = jnp.min(jnp.where(pe2 == p2, lane, none), axis=-1, keepdims=True)
    den = p1 + p2
    r_ref[...] = jnp.where(lane == 0, i1 - N_GROUPS,
                           jnp.where(lane == 1, i2 - N_GROUPS,
                                     jnp.where(lane == 2, p_grp * p1 / den,
                                               jnp.where(lane == 3, p_grp * p2 / den, 0.0))))


def _router(x1, sc, sh, w_r, b_r, tm=256):
    m, d = x1.shape
    row = lambda i: (i, 0)
    fixed = lambda i: (0, 0)
    return pl.pallas_call(
        _router_kernel,
        grid=(m // tm,),
        in_specs=[pl.BlockSpec((tm, d), row), pl.BlockSpec((1, d), fixed), pl.BlockSpec((1, d), fixed),
                  pl.BlockSpec((d, LANES), fixed), pl.BlockSpec((1, LANES), fixed)],
        out_specs=[pl.BlockSpec((tm, d), row), pl.BlockSpec((tm, LANES), row)],
        out_shape=[jax.ShapeDtypeStruct((m, d), F32), jax.ShapeDtypeStruct((m, LANES), F32)],
        compiler_params=_cparams(("parallel",)),
        name="moe_router",
    )(x1, sc, sh, w_r, b_r)


def _moe_kernel(be_ref, tok_ref, tok_next_ref, dst_ref, rw_ref, u_hbm, wg_ref, wu_ref, wd_ref, out_hbm,
                xbuf, ybuf, wgb, wub, wdb, sem_in, sem_out, *, n_dump0):
    rb = ROW_BLOCK
    i = pl.program_id(0)
    last = pl.num_programs(0) - 1
    slot = i % 2

    def row_in(r, tok, sl):
        return pltpu.make_async_copy(u_hbm.at[pl.ds(tok, 1), :], xbuf.at[sl, pl.ds(r, 1), :], sem_in.at[sl])

    def row_out(r, dst):
        return pltpu.make_async_copy(ybuf.at[pl.ds(r, 1), :], out_hbm.at[pl.ds(dst, 1), :], sem_out)

    @pl.when(i == 0)
    def _():
        for r in range(rb):
            row_in(r, tok_ref[0, 0, r], 0).start()
        ybuf[...] = jnp.zeros(ybuf.shape, F32)
        pltpu.make_async_copy(ybuf, out_hbm.at[pl.ds(n_dump0, rb), :], sem_out).start()

    prev = be_ref[jnp.maximum(i - 1, 0)]

    @pl.when((i == 0) | (be_ref[i] != prev))
    def _():
        wgb[...] = wg_ref[0].astype(BF16)
        wub[...] = wu_ref[0].astype(BF16)
        wdb[...] = wd_ref[0].astype(BF16)

    for r in range(rb):
        row_in(r, 0, slot).wait()
    xb = xbuf[slot].astype(BF16)
    for r in range(rb):
        row_in(r, tok_next_ref[0, 0, r], 1 - slot).start()
    gate = _dot(xb, wgb[...])
    up = _dot(xb, wub[...])
    hid = (gate * _sigmoid(gate)) * up
    y = _dot(hid.astype(BF16), wdb[...]) * rw_ref[0]
    for r in range(rb):
        row_out(r, 0).wait()
    ybuf[...] = y
    for r in range(rb):
        row_out(r, dst_ref[0, 0, r]).start()

    @pl.when(i == last)
    def _():
        for r in range(rb):
            row_out(r, 0).wait()
        for r in range(rb):
            row_in(r, 0, 1 - slot).wait()


def _moe_experts(u, blk_exp, row_tok, row_dst, row_w, w_gate, w_up, w_down):
    t, d = u.shape
    n_blocks = blk_exp.shape[0]
    de = w_gate.shape[-1]
    rb = ROW_BLOCK
    tok3 = row_tok.reshape(n_blocks, 1, rb)
    grid_spec = pltpu.PrefetchScalarGridSpec(
        num_scalar_prefetch=1,
        grid=(n_blocks,),
        in_specs=[pl.BlockSpec((1, 1, rb), lambda i, be: (i, 0, 0), memory_space=pltpu.SMEM),
                  pl.BlockSpec((1, 1, rb), lambda i, be: (jnp.minimum(i + 1, n_blocks - 1), 0, 0),
                               memory_space=pltpu.SMEM),
                  pl.BlockSpec((1, 1, rb), lambda i, be: (i, 0, 0), memory_space=pltpu.SMEM),
                  pl.BlockSpec((1, rb, 1), lambda i, be: (i, 0, 0)),
                  pl.BlockSpec(memory_space=pl.ANY),
                  pl.BlockSpec((1, d, de), lambda i, be: (be[i], 0, 0)),
                  pl.BlockSpec((1, d, de), lambda i, be: (be[i], 0, 0)),
                  pl.BlockSpec((1, de, d), lambda i, be: (be[i], 0, 0))],
        out_specs=pl.BlockSpec(memory_space=pl.ANY),
        scratch_shapes=[pltpu.VMEM((2, rb, d), F32), pltpu.VMEM((rb, d), F32),
                        pltpu.VMEM((d, de), BF16), pltpu.VMEM((d, de), BF16), pltpu.VMEM((de, d), BF16),
                        pltpu.SemaphoreType.DMA((2,)), pltpu.SemaphoreType.DMA(())],
    )
    return pl.pallas_call(
        functools.partial(_moe_kernel, n_dump0=2 * t),
        grid_spec=grid_spec,
        out_shape=jax.ShapeDtypeStruct((2 * t + rb, d), F32),
        compiler_params=_cparams(("arbitrary",)),
        name="moe_experts",
    )(blk_exp, tok3, tok3, row_dst.reshape(n_blocks, 1, rb), row_w.reshape(n_blocks, rb, 1), u, w_gate, w_up, w_down)


def _moe_dispatch(route, t):
    k = 2
    eid = route[:, 0:k].astype(jnp.int32).reshape(-1)
    wts = route[:, k:2 * k].reshape(-1)
    n_asg = t * k
    n_rows = n_asg + N_EXPERTS * ROW_BLOCK
    n_blocks = n_rows // ROW_BLOCK
    onehot = (eid[:, None] == jnp.arange(N_EXPERTS, dtype=jnp.int32)[None, :]).astype(jnp.int32)
    rank = jnp.sum((jnp.cumsum(onehot, axis=0) - onehot) * onehot, axis=1)
    counts = jnp.sum(onehot, axis=0)
    padded = (counts + ROW_BLOCK - 1) // ROW_BLOCK * ROW_BLOCK
    pad_end = jnp.cumsum(padded)
    pad_start = pad_end - padded
    dest = jnp.sum(onehot * pad_start[None, :], axis=1) + rank
    asg = jnp.arange(n_asg, dtype=jnp.int32)
    upd = jnp.stack([(asg % k) * t + asg // k, lax.bitcast_convert_type(wts, jnp.int32)], axis=1)
    init = jnp.stack([n_asg + jnp.arange(n_rows, dtype=jnp.int32) % ROW_BLOCK,
                      jnp.zeros((n_rows,), jnp.int32)], axis=1)
    rows = init.at[dest].set(upd)
    row_dst = rows[:, 0]
    row_w = lax.bitcast_convert_type(rows[:, 1], F32)
    row_tok = jnp.where(row_dst < n_asg, row_dst % t, 0)
    blk_start = jnp.arange(n_blocks, dtype=jnp.int32) * ROW_BLOCK
    blk_exp = jnp.minimum(jnp.sum((pad_end[None, :] <= blk_start[:, None]).astype(jnp.int32), axis=1),
                          N_EXPERTS - 1)
    return blk_exp, row_tok, row_dst, row_w


def _final_kernel(x_ref, y0_ref, y1_ref, g2_ref, lg_ref, lb_ref, o_ref, *, alpha):
    z = alpha * x_ref[...] + (1.0 + g2_ref[...]) * (y0_ref[...] + y1_ref[...])
    o_ref[...] = _layer_norm(z) * lg_ref[...] + lb_ref[...]


def _final_ln(x1, y2, g2, ln_g, ln_b, alpha, tm=512):
    m, d = x1.shape
    nb = m // tm
    fixed = lambda i: (0, 0)
    return pl.pallas_call(
        functools.partial(_final_kernel, alpha=alpha),
        grid=(nb,),
        in_specs=[pl.BlockSpec((tm, d), lambda i: (i, 0)),
                  pl.BlockSpec((tm, d), lambda i: (i, 0)),
                  pl.BlockSpec((tm, d), lambda i: (i + nb, 0)),
                  pl.BlockSpec((1, d), fixed), pl.BlockSpec((1, d), fixed), pl.BlockSpec((1, d), fixed)],
        out_specs=pl.BlockSpec((tm, d), lambda i: (i, 0)),
        out_shape=jax.ShapeDtypeStruct((m, d), F32),
        compiler_params=_cparams(("parallel",)),
        name="final_ln",
    )(x1, y2, y2, g2, ln_g, ln_b)


def _to_lane_blocks(a, tq):
    g, hpg, s, c = a.shape
    return a.reshape(g, hpg, s // tq, tq, c).transpose(0, 2, 4, 1, 3).reshape(g, s // tq, c, hpg * tq)


def _layer(x2d, c, w_ada, b_ada, w_in, b_fgt, t5_table, cmp_pe, cmp_w1, cmp_b1, cmp_w2,
           w_br_nsa, w_br_fox, w_o, ln1_g, ln1_b, w_rg, b_rg, w_re, b_re,
           w_gate, w_up, w_down, ln2_g, ln2_b, alpha):
    s, d = x2d.shape
    hd = HEAD_DIM
    g = NSA_GROUPS
    mod = _ada_mod(c, w_ada, b_ada)
    sh1, sc1, g1, sh2, sc2, g2 = [mod[:, i * d:(i + 1) * d] for i in range(6)]

    c_q = NSA_HEADS * hd
    c_kv = 6 * g * hd
    c_gate = 3 * NSA_HEADS
    c_fox = 3 * FOX_HEADS * hd
    off_kv = c_q
    off_gate = off_kv + c_kv
    off_fox = off_gate + c_gate
    off_fgt = off_fox + c_fox
    off_merge = off_fgt + FOX_HEADS
    qscale = hd ** -0.5 * LOG2E
    w_attn = jnp.concatenate([w_in[:, 0:off_kv] * qscale, w_in[:, off_kv:off_gate],
                              w_in[:, off_fox:off_fox + FOX_HEADS * hd] * qscale,
                              w_in[:, off_fox + FOX_HEADS * hd:off_fgt]], axis=1).astype(BF16)
    n_small = c_gate + FOX_HEADS
    w_small = jnp.concatenate([w_in[:, off_gate:off_fox], w_in[:, off_fgt:off_merge],
                               jnp.zeros((d, LANES - n_small), F32)], axis=1).astype(BF16)
    w_merge = w_in[:, off_merge:].astype(BF16)

    u = _ln_mod(x2d, sc1, sh1, BF16)
    n_attn = w_attn.shape[1]
    proj = _matmul(u, w_attn, BF16, 512, n_attn // 2, "in_proj_attn")
    small = _matmul(u, w_small, F32, 512, LANES, "in_proj_small")
    merge = _matmul(u, w_merge, BF16, 512, 1024, "in_proj_merge")

    q_nsa = proj[:, 0:c_q].reshape(s, g, NSA_HPG, hd).transpose(1, 2, 0, 3)
    kv = proj[:, c_q:c_q + c_kv].reshape(s, 6, g, hd).transpose(1, 2, 0, 3)
    fox = proj[:, c_q + c_kv:].reshape(s, 3, FOX_HEADS, hd).transpose(1, 2, 0, 3)

    kv_cmp = _compress(kv[0:2], cmp_pe, cmp_w1, cmp_b1, cmp_w2)
    nch = s // CMP_STRIDE
    ncp = nch + LANES
    nslc = s // SLC_LEN
    kv_cmp_pad = jnp.pad(kv_cmp.astype(BF16), ((0, 0), (0, 0), (CMP_PAD, ncp - nch - CMP_PAD), (0, 0)))
    b_cmp, b_sel, b_win, far3 = _nsa_bias_tables(t5_table)

    def ones(*shape):
        return jnp.ones(shape, BF16)

    def zeros(*shape):
        return jnp.zeros(shape, BF16)

    row_ix = np.arange(ncp)
    row_ok = (row_ix >= CMP_PAD) & (row_ix < nch - 1 + CMP_PAD)
    row_cols = np.zeros((ncp, 2 * AUG_K - hd), np.float32)
    row_cols[:, ROW_BIAS - hd:ROW_BIAS - hd + 3] = row_ok[:, None]
    row_cols[:, ROW_BIAS - hd + 3] = ~row_ok
    row_cols[row_ix, AUG_K - hd + row_ix // BF16_ROWS] = 1.0
    kc_aug = jnp.concatenate([kv_cmp_pad[0], jnp.broadcast_to(jnp.asarray(row_cols, BF16), (g,) + row_cols.shape)],
                             axis=-1)
    vct = kv_cmp_pad[1].transpose(0, 2, 1)
    qt = _to_lane_blocks(q_nsa, ATT_TQ)
    oct_, selt = _nsa_compress_select(qt, kc_aug, vct, b_cmp, far3)

    blk_onehot = jnp.asarray((np.arange(s)[:, None] // SLC_LEN % BF16_ROWS == np.arange(BF16_ROWS)[None, :])
                             .astype(np.float32), BF16)
    ks_aug = jnp.concatenate([kv[2], jnp.broadcast_to(blk_onehot, (g, s, BF16_ROWS)), ones(g, s, 3),
                              zeros(g, s, AUG_K - ROW_BIAS - 3)], axis=-1)
    kw_aug = jnp.concatenate([kv[4], zeros(g, s, AUG_K - hd)], axis=-1)
    vst_aug = jnp.concatenate([kv[3].transpose(0, 2, 1), ones(g, 8, s), zeros(g, 8, s)], axis=1)
    vwt_aug = jnp.concatenate([kv[5].transpose(0, 2, 1), ones(g, 8, s), zeros(g, 8, s)], axis=1)
    gates = small[:, 0:c_gate].reshape(s, g, NSA_HPG, 3).transpose(1, 2, 0, 3)
    gates_t = jnp.pad(_to_lane_blocks(gates, ATT_TQ), ((0, 0), (0, 0), (0, 5), (0, 0)))
    o_nsa_t = _nsa_attention(qt, ks_aug, vst_aug, kw_aug, vwt_aug, selt, far3,
                             b_sel, b_win, oct_, gates_t)
    nq = s // ATT_TQ
    o_nsa = o_nsa_t.reshape(g, nq, hd, NSA_HPG, ATT_TQ).transpose(1, 4, 0, 3, 2).reshape(s, MIX_W)

    fgt_bias = jnp.concatenate([jnp.zeros((c_gate,), F32), b_fgt, jnp.zeros((LANES - n_small,), F32)])[None, :]
    decay = _decay_cumsum(small, fgt_bias)[:, c_gate:c_gate + FOX_HEADS]
    nh = FOX_HEADS
    d_hi, d_mid, d_lo = _split3_exact(-decay.T * LOG2E)
    fox_k_aug = jnp.concatenate([fox[1], jnp.stack([d_hi, d_mid, d_lo], axis=-1), zeros(nh, s, AUG_K - hd - 3)],
                                axis=-1)
    fox_qt_aug = jnp.concatenate([fox[0].transpose(0, 2, 1), ones(nh, 3, s), zeros(nh, AUG_K - hd - 3, s)], axis=1)
    fox_vt_aug = jnp.concatenate([fox[2].transpose(0, 2, 1), ones(nh, 8, s), zeros(nh, 8, s)], axis=1)
    o_fox = _fox_attention(fox_qt_aug, fox_k_aug, fox_vt_aug)
    o_fox = o_fox.transpose(2, 0, 1).reshape(s, MIX_W)

    x1 = _merge_project(o_nsa, o_fox, merge, x2d, w_br_nsa.astype(BF16), w_br_fox.astype(BF16),
                        w_o.astype(BF16), g1, ln1_g[None, :], ln1_b[None, :], alpha)

    n_r = N_GROUPS + N_EXPERTS
    w_r = jnp.concatenate([w_rg, w_re.reshape(d, N_EXPERTS), jnp.zeros((d, LANES - n_r), F32)], axis=1)
    b_r = jnp.concatenate([b_rg, b_re.reshape(N_EXPERTS), jnp.zeros((LANES - n_r,), F32)])[None, :]
    u2, route = _router(x1, sc2, sh2, w_r, b_r)
    blk_exp, row_tok, row_dst, row_w = _moe_dispatch(route, s)
    y2 = _moe_experts(u2, blk_exp, row_tok, row_dst, row_w, w_gate, w_up, w_down)
    return _final_ln(x1, y2, g2, ln2_g[None, :], ln2_b[None, :], alpha)


def kernel(x, c, w_ada, b_ada, w_in, b_fgt, t5_table, cmp_pe, cmp_w1, cmp_b1, cmp_w2, w_br_nsa, w_br_fox, w_o,
           ln1_g, ln1_b, w_rg, b_rg, w_re, b_re, w_gate, w_up, w_down, ln2_g, ln2_b):
    b, s, d = x.shape
    depth = w_ada.shape[0]
    assert b == 1
    alpha = (2 * depth) ** 0.25
    h = x[0]
    for l in range(depth):
        h = _layer(h, c, w_ada[l], b_ada[l], w_in[l], b_fgt[l], t5_table, cmp_pe[l], cmp_w1[l], cmp_b1[l],
                   cmp_w2[l], w_br_nsa[l], w_br_fox[l], w_o[l], ln1_g[l], ln1_b[l], w_rg[l], b_rg[l],
                   w_re[l], b_re[l], w_gate[l], w_up[l], w_down[l], ln2_g[l], ln2_b[l], alpha)
    return h[None]
```

```python
import functools
import math

import numpy as np
import jax
import jax.numpy as jnp
from jax import lax
from jax.experimental import pallas as pl
from jax.experimental.pallas import tpu as pltpu

F32 = jnp.float32
BF16 = jnp.bfloat16
HIGHEST = lax.Precision.HIGHEST
LOG2E = math.log2(math.e)

HEAD_DIM = 64
NSA_HEADS = 8
NSA_GROUPS = 2
NSA_HPG = NSA_HEADS // NSA_GROUPS
FOX_HEADS = 8
MIX_W = NSA_HEADS * HEAD_DIM
CMP_LEN = 32
CMP_STRIDE = 16
SLC_LEN = 64
SLC_TOPK = 16
WINDOW = 512
T5_BUCKETS = 32
T5_MAX_EXACT = 16
T5_MAX_DIST = 128
N_GROUPS = 8
EXPERTS_PER_GROUP = 8
N_EXPERTS = N_GROUPS * EXPERTS_PER_GROUP
ROW_BLOCK = 128
LN_EPS = 1e-5
NEG_INF = -1e30
M_INIT = -1e29
FORCE_SCORE = 1e4

LANES = 128
BF16_ROWS = 16
CMP_PAD = 8
ATT_TQ = 256
CMP_NEAR_ROWS = 32
FOX_TQ = 1024
FOX_TK = 512
AUG_K = 128
AUG_V = HEAD_DIM + 16
ROW_MASK = HEAD_DIM
ROW_BIAS = HEAD_DIM + 16
VMEM_LIMIT = 56 * 1024 * 1024


def _cparams(sem, vmem=VMEM_LIMIT):
    return pltpu.CompilerParams(dimension_semantics=sem, vmem_limit_bytes=vmem)


def _sigmoid(x):
    return 1.0 / (1.0 + jnp.exp(-x))


def _layer_norm(x):
    mu = jnp.mean(x, axis=-1, keepdims=True)
    xc = x - mu
    var = jnp.mean(xc * xc, axis=-1, keepdims=True)
    return xc * lax.rsqrt(var + LN_EPS)


def _split3(x):
    hi = x.astype(BF16)
    r1 = x - hi.astype(F32)
    mid = r1.astype(BF16)
    lo = (r1 - mid.astype(F32)).astype(BF16)
    return hi, mid, lo


def _split3_exact(x):
    def trunc(v):
        bits = lax.bitcast_convert_type(v, jnp.uint32) & jnp.uint32(0xFFFF0000)
        return lax.bitcast_convert_type(bits, F32)
    hi = trunc(x)
    r1 = x - hi
    mid = trunc(r1)
    lo = r1 - mid
    return hi.astype(BF16), mid.astype(BF16), lo.astype(BF16)


def _dot(a, b):
    return jnp.dot(a, b, preferred_element_type=F32)


def _dot_nt(a, b):
    return lax.dot_general(a, b, (((1,), (1,)), ((), ())), preferred_element_type=F32)


def _dot3(x, w_bf16):
    hi, mid, lo = _split3(x)
    return _dot(hi, w_bf16) + _dot(mid, w_bf16) + _dot(lo, w_bf16)


def _dot3_rhs(w_bf16, x):
    hi, mid, lo = _split3(x)
    return _dot(w_bf16, hi) + _dot(w_bf16, mid) + _dot(w_bf16, lo)


def _ada_kernel(c_ref, w_ref, b_ref, o_ref):
    c = c_ref[...]
    a = c * _sigmoid(c)
    o_ref[...] = jnp.dot(a, w_ref[...], precision=HIGHEST, preferred_element_type=F32) + b_ref[...]


def _ada_mod(c, w, b):
    d, n = w.shape
    tn = 1024
    c8 = jnp.broadcast_to(c, (8, d))
    out = pl.pallas_call(
        _ada_kernel,
        grid=(n // tn,),
        in_specs=[pl.BlockSpec((8, d), lambda j: (0, 0)),
                  pl.BlockSpec((d, tn), lambda j: (0, j)),
                  pl.BlockSpec((1, tn), lambda j: (0, j))],
        out_specs=pl.BlockSpec((8, tn), lambda j: (0, j)),
        out_shape=jax.ShapeDtypeStruct((8, n), F32),
        compiler_params=_cparams(("parallel",)),
        name="ada_mod",
    )(c8, w, b.reshape(1, n))
    return out[0:1]


def _lnmod_kernel(x_ref, sc_ref, sh_ref, o_ref):
    y = _layer_norm(x_ref[...])
    o_ref[...] = (y * (1.0 + sc_ref[...]) + sh_ref[...]).astype(o_ref.dtype)


def _ln_mod(x, sc, sh, out_dtype, tm=512):
    m, d = x.shape
    return pl.pallas_call(
        _lnmod_kernel,
        grid=(m // tm,),
        in_specs=[pl.BlockSpec((tm, d), lambda i: (i, 0)),
                  pl.BlockSpec((1, d), lambda i: (0, 0)),
                  pl.BlockSpec((1, d), lambda i: (0, 0))],
        out_specs=pl.BlockSpec((tm, d), lambda i: (i, 0)),
        out_shape=jax.ShapeDtypeStruct((m, d), out_dtype),
        compiler_params=_cparams(("parallel",)),
        name="ln_mod",
    )(x, sc, sh)


def _mm_kernel(a_ref, w_ref, o_ref):
    o_ref[...] = _dot(a_ref[...], w_ref[...]).astype(o_ref.dtype)


def _matmul(a, w, out_dtype, tm, tn, name):
    m, k = a.shape
    n = w.shape[1]
    return pl.pallas_call(
        _mm_kernel,
        grid=(n // tn, m // tm),
        in_specs=[pl.BlockSpec((tm, k), lambda j, i: (i, 0)),
                  pl.BlockSpec((k, tn), lambda j, i: (0, j))],
        out_specs=pl.BlockSpec((tm, tn), lambda j, i: (i, j)),
        out_shape=jax.ShapeDtypeStruct((m, n), out_dtype),
        compiler_params=_cparams(("parallel", "parallel")),
        name=name,
    )(a, w)


def _mm_nt_kernel(w_ref, a_ref, o_ref):
    o_ref[...] = _dot_nt(w_ref[...], a_ref[...]).astype(o_ref.dtype)


def _matmul_nt(w_t, a, out_dtype, tm, name):
    n, k = w_t.shape
    m = a.shape[0]
    return pl.pallas_call(
        _mm_nt_kernel,
        grid=(m // tm,),
        in_specs=[pl.BlockSpec((n, k), lambda i: (0, 0), pipeline_mode=pl.Buffered(1)),
                  pl.BlockSpec((tm, k), lambda i: (i, 0))],
        out_specs=pl.BlockSpec((n, tm), lambda i: (0, i)),
        out_shape=jax.ShapeDtypeStruct((n, m), out_dtype),
        compiler_params=_cparams(("parallel",)),
        name=name,
    )(w_t, a)


def _gelu_tanh(x):
    return 0.5 * x * (1.0 + jnp.tanh(math.sqrt(2.0 / math.pi) * (x + 0.044715 * (x * x * x))))


def _compress_kernel(c_ref, pe_ref, w1a_ref, w1b_ref, b1_ref, w2_ref, o_ref, *, nch):
    c = c_ref[0, 0]
    w1a = w1a_ref[0]
    w1b = w1b_ref[0]
    half = CMP_STRIDE * HEAD_DIM
    a = _dot(c, w1a)
    b = _dot(c, w1b)
    b_next = pltpu.roll(b, shift=nch - 1, axis=0)
    pe = pe_ref[0]
    pb = _dot(pe[:, :half], w1a) + _dot(pe[:, half:], w1b)
    hid = _gelu_tanh(a + b_next + pb[0:1, :] + b1_ref[0])
    o_ref[0, 0] = _dot(hid.astype(BF16), w2_ref[0])


def _compress(kv_cmp, pe, w1, b1, w2):
    z, g, s, hd = kv_cmp.shape
    nch = s // CMP_STRIDE
    half = CMP_STRIDE * hd
    chunks = kv_cmp.reshape(z, g, nch, half)
    pe8 = jnp.broadcast_to(pe.reshape(z, 1, CMP_LEN * hd), (z, 8, CMP_LEN * hd)).astype(BF16)
    w1b16 = w1.astype(BF16)
    hidn = w1.shape[-1]
    return pl.pallas_call(
        functools.partial(_compress_kernel, nch=nch),
        grid=(z, g),
        in_specs=[pl.BlockSpec((1, 1, nch, half), lambda zi, gi: (zi, gi, 0, 0)),
                  pl.BlockSpec((1, 8, 2 * half), lambda zi, gi: (zi, 0, 0)),
                  pl.BlockSpec((1, half, hidn), lambda zi, gi: (zi, 0, 0)),
                  pl.BlockSpec((1, half, hidn), lambda zi, gi: (zi, 1, 0)),
                  pl.BlockSpec((1, 1, hidn), lambda zi, gi: (zi, 0, 0)),
                  pl.BlockSpec((1, hidn, hd), lambda zi, gi: (zi, 0, 0))],
        out_specs=pl.BlockSpec((1, 1, nch, hd), lambda zi, gi: (zi, gi, 0, 0)),
        out_shape=jax.ShapeDtypeStruct((z, g, nch, hd), F32),
        compiler_params=_cparams(("parallel", "parallel")),
        name="compress_kv",
    )(chunks, pe8, w1b16, w1b16, b1.reshape(z, 1, hidn), w2.astype(BF16))


def _t5_bucket_np(dist):
    n = np.maximum(dist, 0)
    ratio = np.log(np.maximum(n, T5_MAX_EXACT).astype(np.float64) / T5_MAX_EXACT)
    big = T5_MAX_EXACT + (ratio / math.log(T5_MAX_DIST / T5_MAX_EXACT)
                          * (T5_BUCKETS - T5_MAX_EXACT)).astype(np.int64)
    return np.where(n < T5_MAX_EXACT, n, np.minimum(big, T5_BUCKETS - 1)).astype(np.int32)


def _toeplitz_t(w, n_keys, n_q):
    length = n_keys + n_q - 1
    w_pad = jnp.concatenate([w, jnp.zeros(w.shape[:-1] + (1,), w.dtype)], axis=-1)
    reps = (1,) * (w.ndim - 1) + (n_keys,)
    flat = jnp.tile(w_pad, reps)[..., :n_keys * length]
    return flat.reshape(w.shape[:-1] + (n_keys, length))[..., n_keys - 1:n_keys - 1 + n_q]


def _att_table_t(tbh, n_keys, tq, lo, hi, minus_far):
    d = np.arange(n_keys + tq - 1) - (tq - 1)
    valid = (d >= lo) & (d < hi)
    vals = tbh[:, :, _t5_bucket_np(d)]
    if minus_far:
        vals = vals - tbh[:, :, T5_BUCKETS - 1:]
    w = jnp.where(jnp.asarray(valid), vals * LOG2E, NEG_INF)
    t = _toeplitz_t(w, n_keys, tq)
    g, hpg = tbh.shape[:2]
    return t.transpose(0, 2, 1, 3).reshape(g, n_keys, hpg * tq)


def _nsa_bias_tables(t5_table):
    tbh = t5_table.T.reshape(NSA_GROUPS, NSA_HPG, T5_BUCKETS).astype(F32)
    j = np.arange(CMP_NEAR_ROWS)[:, None]
    i = np.arange(ATT_TQ)[None, :]
    dist = i - (CMP_LEN - 1) - CMP_STRIDE * (j - CMP_PAD)
    vals = (tbh[:, :, _t5_bucket_np(dist)] - tbh[:, :, T5_BUCKETS - 1][:, :, None, None]) * LOG2E
    vals = jnp.where(jnp.asarray(dist >= 0)[None, None], vals, NEG_INF)
    b_cmp = vals.transpose(0, 2, 1, 3).reshape(NSA_GROUPS, CMP_NEAR_ROWS, NSA_HPG * ATT_TQ)
    far = tbh[:, :, T5_BUCKETS - 1] * LOG2E
    b_sel = _att_table_t(tbh, 2 * ATT_TQ, ATT_TQ, 0, 1 << 30, True)
    b_sel = jnp.pad(b_sel, ((0, 0), (ATT_TQ, 0), (0, 0)))
    b_win = _att_table_t(tbh, 3 * ATT_TQ, ATT_TQ, 0, WINDOW, False)
    b_win = jnp.pad(b_win, ((0, 0), (0, ATT_TQ), (0, 0)), constant_values=NEG_INF)
    hi, mid, lo = _split3_exact(jnp.repeat(far, ATT_TQ, axis=1))
    far3 = jnp.stack([hi, mid, lo, jnp.full(hi.shape, NEG_INF, BF16)], axis=1)
    far3 = jnp.pad(far3, ((0, 0), (0, BF16_ROWS - 4), (0, 0)))
    return b_cmp, b_sel, b_win, far3


def _slc_weight_np(ncp, nslc):
    w = np.zeros((ncp, nslc), np.float32)
    ratio = SLC_LEN // CMP_STRIDE
    for off, val in ((-1, 0.5), (0, 1.0), (1, 1.0), (2, 1.0), (3, 0.5)):
        j = np.arange(nslc)
        n = ratio * j + off
        ok = (n >= 0) & (n + CMP_PAD < ncp)
        w[n[ok] + CMP_PAD, j[ok]] += val
    return w


def _nsa_cmp_kernel(qt_ref, kc_ref, vct_ref, near_ref, far3_ref, oct_ref, selt_ref, rhs, s_scr, imp_scr, *, nslc):
    tq = ATT_TQ
    lanes = NSA_HPG * tq
    qb = pl.program_id(1)
    cpb = tq // CMP_STRIDE
    rhs[...] = jnp.zeros(rhs.shape, BF16)
    rhs[0:HEAD_DIM, :] = qt_ref[0, 0]
    rhs[ROW_BIAS:ROW_BIAS + BF16_ROWS, :] = far3_ref[0]
    chunk = lax.broadcasted_iota(jnp.int32, (AUG_K, lanes), 0)
    rhs[AUG_K:2 * AUG_K, :] = jnp.where(chunk >= qb + CMP_NEAR_ROWS // BF16_ROWS, NEG_INF, 0.0).astype(BF16)
    s_scr[...] = _dot(kc_ref[0], rhs[...])
    r0 = pl.multiple_of(cpb * qb, BF16_ROWS)
    s_scr[pl.ds(r0, CMP_NEAR_ROWS), :] = s_scr[pl.ds(r0, CMP_NEAR_ROWS), :] + near_ref[0]
    s = s_scr[...]
    m = jnp.max(s, axis=0, keepdims=True)
    e = jnp.exp2(s - m)
    l = jnp.sum(e, axis=0, keepdims=True)
    p = e * jnp.where(m > M_INIT, 1.0 / l, 0.0)
    oct_ref[0, 0] = _dot(vct_ref[0], p.astype(BF16))
    imp = p[:, 0:tq]
    for h in range(1, NSA_HPG):
        imp = imp + p[:, h * tq:(h + 1) * tq]
    n_lane_blocks = tq // LANES
    for c in range(n_lane_blocks):
        imp_scr[c] = imp[:, c * LANES:(c + 1) * LANES]
    ratio = SLC_LEN // CMP_STRIDE

    def taps(off):
        return jnp.concatenate([imp_scr[c, pl.ds(CMP_PAD + off, nslc, stride=ratio), :]
                                for c in range(n_lane_blocks)], axis=1)

    p_slc = 0.5 * (taps(-1) + taps(ratio - 1))
    for off in range(ratio - 1):
        p_slc = p_slc + taps(off)
    blk = lax.broadcasted_iota(jnp.int32, (nslc, tq), 0)
    cur = (qb * tq + lax.broadcasted_iota(jnp.int32, (nslc, tq), 1)) // SLC_LEN
    forced = (blk == 0) | (blk == cur) | (blk == cur - 1)
    score = jnp.where(forced, FORCE_SCORE, jnp.where(blk <= cur, p_slc, -1.0))
    blk_f = blk.astype(F32)
    sel = jnp.zeros((nslc, tq), F32)
    for _ in range(min(SLC_TOPK, nslc)):
        mx = jnp.max(score, axis=0, keepdims=True)
        first = jnp.min(jnp.where(score == mx, blk_f, float(nslc)), axis=0, keepdims=True)
        hit = blk_f == first
        sel = jnp.where(hit, 1.0, sel)
        score = jnp.where(hit, -2.0, score)
    selt_ref[0] = sel.astype(BF16)


def _nsa_compress_select(qt, kc_aug, vct, b_cmp, far3):
    g, nq, hd, lanes = qt.shape
    ncp = kc_aug.shape[1]
    tq = ATT_TQ
    nslc = nq * tq // SLC_LEN
    assert ncp // BF16_ROWS <= AUG_K
    assert tq // CMP_STRIDE == BF16_ROWS
    per_q = lambda gi, qi: (gi, qi, 0, 0)
    per_g = lambda gi, qi: (gi, 0, 0)
    return pl.pallas_call(
        functools.partial(_nsa_cmp_kernel, nslc=nslc),
        grid=(g, nq),
        in_specs=[pl.BlockSpec((1, 1, hd, lanes), per_q),
                  pl.BlockSpec((1, ncp, 2 * AUG_K), per_g),
                  pl.BlockSpec((1, hd, ncp), per_g),
                  pl.BlockSpec((1, CMP_NEAR_ROWS, lanes), per_g),
                  pl.BlockSpec((1, BF16_ROWS, lanes), per_g)],
        out_specs=[pl.BlockSpec((1, 1, hd, lanes), per_q),
                   pl.BlockSpec((1, nslc, tq), lambda gi, qi: (gi, 0, qi))],
        out_shape=[jax.ShapeDtypeStruct((g, nq, hd, lanes), F32),
                   jax.ShapeDtypeStruct((g, nslc, nq * tq), BF16)],
        scratch_shapes=[pltpu.VMEM((2 * AUG_K, lanes), BF16), pltpu.VMEM((ncp, lanes), F32),
                        pltpu.VMEM((tq // LANES, ncp, LANES), F32)],
        compiler_params=_cparams(("parallel", "parallel")),
        name="nsa_compress_select",
    )(qt, kc_aug, vct, b_cmp, far3)


def _flash_init_t(m_ref, acc_ref):
    m_ref[...] = jnp.full(m_ref.shape, M_INIT, F32)
    acc_ref[...] = jnp.zeros(acc_ref.shape, F32)


def _flash_step_t(s, vt_tile, m_ref, acc_ref):
    m_old = m_ref[...]
    m_new = jnp.maximum(m_old, jnp.max(s, axis=0, keepdims=True))
    p = jnp.exp2(s - m_new).astype(BF16)
    acc_ref[...] = jnp.exp2(m_old - m_new) * acc_ref[...] + _dot(vt_tile, p)
    m_ref[...] = m_new


def _flash_result_t(acc_ref):
    acc = acc_ref[...]
    return acc[0:HEAD_DIM, :] / acc[HEAD_DIM:HEAD_DIM + 1, :]


def _nsa_att_kernel(qt_ref, ks_ref, kx_ref, vst_ref, kw_ref, vwt_ref, selt_ref, far3_ref, bsel_ref, bwin_ref,
                    oct_ref, gate_ref, o_ref, rhs_s, rhs_w, mask_t, ms, accs, mw, accw, s_even, s_odd):
    tq = ATT_TQ
    qb = pl.program_id(1)
    ones_rows = _ones_rows(tq)
    kx_tiles = kx_ref.shape[0] // tq
    qt = qt_ref[0, 0]
    rhs_s[...] = jnp.zeros(rhs_s.shape, BF16)
    rhs_s[0:HEAD_DIM, :] = qt
    rhs_s[ROW_BIAS:ROW_BIAS + BF16_ROWS, :] = far3_ref[0]
    rhs_w[...] = jnp.zeros(rhs_w.shape, BF16)
    rhs_w[0:HEAD_DIM, :] = qt
    madd = ((selt_ref[0].astype(F32) - 1.0) * (-NEG_INF)).astype(BF16)
    mask_t[...] = jnp.concatenate([madd] * NSA_HPG, axis=1)
    _flash_init_t(ms, accs)
    _flash_init_t(mw, accw)
    blocks_per_tile = tq // SLC_LEN

    def sel_scores(kt, s_ref):
        kt = jnp.minimum(kt, qb)
        k0 = pl.multiple_of(kt * tq, tq)
        chunk = pl.multiple_of((kt * blocks_per_tile) // BF16_ROWS * BF16_ROWS, BF16_ROWS)
        rhs_s[ROW_MASK:ROW_MASK + BF16_ROWS, :] = mask_t[pl.ds(chunk, BF16_ROWS), :]
        rel = jnp.clip(kt - qb + 2, 0, 2)
        table = bsel_ref[0, pl.ds(pl.multiple_of(rel * tq, tq), tq), :]
        k_aug = ks_ref[pl.ds(k0, tq), :] + kx_ref[pl.ds(pl.multiple_of((kt % kx_tiles) * tq, tq), tq), :]
        s_ref[...] = _dot(k_aug, rhs_s[...]) + table

    def sel_consume(kt, s_ref):
        k0 = pl.multiple_of(kt * tq, tq)
        vt_aug = jnp.concatenate([vst_ref[0, :, pl.ds(k0, tq)], ones_rows], axis=0)
        _flash_step_t(s_ref[...], vt_aug, ms, accs)

    def win_scores(j, s_ref):
        kt = qb - 2 + j
        k0 = pl.multiple_of(jnp.maximum(kt, 0) * tq, tq)
        row = pl.multiple_of(jnp.where(kt >= 0, j, 3) * tq, tq)
        s_ref[...] = _dot(kw_ref[pl.ds(k0, tq), :], rhs_w[...]) + bwin_ref[0, pl.ds(row, tq), :]

    def win_consume(j, s_ref):
        k0 = pl.multiple_of(jnp.maximum(qb - 2 + j, 0) * tq, tq)
        vt_aug = jnp.concatenate([vwt_ref[0, :, pl.ds(k0, tq)], ones_rows], axis=0)
        _flash_step_t(s_ref[...], vt_aug, mw, accw)

    n_sel = qb + 1
    sel_scores(0, s_even)

    def pair_body(j, carry):
        sel_scores(2 * j + 1, s_odd)
        sel_consume(2 * j, s_even)
        sel_scores(2 * j + 2, s_even)
        sel_consume(2 * j + 1, s_odd)
        return carry

    lax.fori_loop(0, n_sel // 2, pair_body, 0)
    win_scores(0, s_odd)

    @pl.when(n_sel % 2 == 1)
    def _():
        sel_consume(qb, s_even)

    win_scores(1, s_even)
    win_consume(0, s_odd)
    win_scores(2, s_odd)
    win_consume(1, s_even)
    win_consume(2, s_odd)

    gt = _sigmoid(gate_ref[0, 0])
    out = gt[0:1, :] * oct_ref[0, 0] + gt[1:2, :] * _flash_result_t(accs) + gt[2:3, :] * _flash_result_t(accw)
    o_ref[0, 0] = out.astype(o_ref.dtype)


def _nsa_attention(qt, k_tok, ks_block0, kw_block0, k_extra, vst, vwt, selt, far3, b_sel, b_win, oct_, gates_t):
    g, nq, hd, lanes = qt.shape
    s = k_tok.shape[0]
    nslc = selt.shape[1]
    tq = ATT_TQ
    resident = pl.Buffered(1)
    per_q = lambda gi, qi: (gi, qi, 0, 0)
    per_g = lambda gi, qi: (gi, 0, 0)
    return pl.pallas_call(
        _nsa_att_kernel,
        grid=(g, nq),
        in_specs=[pl.BlockSpec((1, 1, hd, lanes), per_q),
                  pl.BlockSpec((s, AUG_K), lambda gi, qi: (0, ks_block0 + gi), pipeline_mode=resident),
                  pl.BlockSpec(k_extra.shape, lambda gi, qi: (0, 0), pipeline_mode=resident),
                  pl.BlockSpec((1, hd, s), per_g, pipeline_mode=resident),
                  pl.BlockSpec((s, AUG_K), lambda gi, qi: (0, kw_block0 + gi), pipeline_mode=resident),
                  pl.BlockSpec((1, hd, s), per_g, pipeline_mode=resident),
                  pl.BlockSpec((1, nslc, tq), lambda gi, qi: (gi, 0, qi)),
                  pl.BlockSpec((1, BF16_ROWS, lanes), per_g),
                  pl.BlockSpec((1, 3 * tq, lanes), per_g, pipeline_mode=resident),
                  pl.BlockSpec((1, 4 * tq, lanes), per_g, pipeline_mode=resident),
                  pl.BlockSpec((1, 1, hd, lanes), per_q),
                  pl.BlockSpec((1, 1, 8, lanes), per_q)],
        out_specs=pl.BlockSpec((1, 1, hd, lanes), per_q),
        out_shape=jax.ShapeDtypeStruct((g, nq, hd, lanes), BF16),
        scratch_shapes=[pltpu.VMEM((AUG_K, lanes), BF16), pltpu.VMEM((AUG_K, lanes), BF16),
                        pltpu.VMEM((nslc, lanes), BF16),
                        pltpu.VMEM((1, lanes), F32), pltpu.VMEM((AUG_V, lanes), F32),
                        pltpu.VMEM((1, lanes), F32), pltpu.VMEM((AUG_V, lanes), F32),
                        pltpu.VMEM((tq, lanes), F32), pltpu.VMEM((tq, lanes), F32)],
        compiler_params=_cparams(("arbitrary", "arbitrary")),
        name="nsa_select_window",
    )(qt, k_tok, k_extra, vst, k_tok, vwt, selt, far3, b_sel, b_win, oct_, gates_t)


def _decay_kernel(z_ref, b_ref, place_ref, o_ref, carry_ref, *, tb):
    @pl.when(pl.program_id(0) == 0)
    def _():
        carry_ref[...] = jnp.zeros(carry_ref.shape, F32)

    z = z_ref[...] + b_ref[...]
    log_f = jnp.minimum(z, 0.0) - jnp.log1p(jnp.exp(-jnp.abs(z)))
    r = lax.broadcasted_iota(jnp.int32, (tb, tb), 0)
    c = lax.broadcasted_iota(jnp.int32, (tb, tb), 1)
    tri = jnp.where(r >= c, 1.0, 0.0).astype(BF16)
    run = _dot3_rhs(tri, log_f) + carry_ref[...]
    carry_ref[...] = run[tb - 1:tb, :]
    hi, mid, lo = _split3_exact(-run * LOG2E)
    o_ref[...] = (_dot(hi, place_ref[0]) + _dot(mid, place_ref[1]) + _dot(lo, place_ref[2])).astype(BF16)


def _decay_pieces(z, bias, first_lane, n_heads, tb=512):
    s, n = z.shape
    place = np.zeros((3, n, n_heads * AUG_K), np.float32)
    for h in range(n_heads):
        for piece in range(3):
            place[piece, first_lane + h, h * AUG_K + HEAD_DIM + piece] = 1.0
    return pl.pallas_call(
        functools.partial(_decay_kernel, tb=tb),
        grid=(s // tb,),
        in_specs=[pl.BlockSpec((tb, n), lambda i: (i, 0)),
                  pl.BlockSpec((1, n), lambda i: (0, 0)),
                  pl.BlockSpec((3, n, n_heads * AUG_K), lambda i: (0, 0, 0))],
        out_specs=pl.BlockSpec((tb, n_heads * AUG_K), lambda i: (i, 0)),
        out_shape=jax.ShapeDtypeStruct((s, n_heads * AUG_K), BF16),
        scratch_shapes=[pltpu.VMEM((1, n), F32)],
        compiler_params=_cparams(("arbitrary",)),
        name="decay_cumsum",
    )(z, bias, jnp.asarray(place, BF16))


def _ones_rows(width):
    return jnp.where(lax.broadcasted_iota(jnp.int32, (BF16_ROWS, width), 0) < 8, 1.0, 0.0).astype(BF16)


def _fox_kernel(qt_ref, k_ref, dk_ref, vt_ref, o_ref, rhs, m_ref, acc_ref, s_even, s_odd):
    tq = FOX_TQ
    tk = FOX_TK
    assert tq == 2 * tk
    qb = pl.program_id(1)
    row = lax.broadcasted_iota(jnp.int32, (AUG_K - HEAD_DIM, tq), 0)
    rhs[0:HEAD_DIM, :] = qt_ref[0]
    rhs[HEAD_DIM:AUG_K, :] = jnp.where(row < 3, 1.0, 0.0).astype(BF16)
    ones_rows = _ones_rows(tk)
    _flash_init_t(m_ref, acc_ref)

    def scores(kt, s_ref):
        k0 = pl.multiple_of(kt * tk, tk)
        k_aug = k_ref[pl.ds(k0, tk), :] + dk_ref[pl.ds(k0, tk), :]
        s_ref[...] = _dot(k_aug, rhs[...])

    def consume(kt, s_ref, diagonal):
        k0 = pl.multiple_of(kt * tk, tk)
        s = s_ref[...]
        if diagonal:
            key = k0 + lax.broadcasted_iota(jnp.int32, (tk, tq), 0)
            qry = qb * tq + lax.broadcasted_iota(jnp.int32, (tk, tq), 1)
            s = jnp.where(key <= qry, s, NEG_INF)
        vt_aug = jnp.concatenate([vt_ref[0, :, pl.ds(k0, tk)], ones_rows], axis=0)
        _flash_step_t(s, vt_aug, m_ref, acc_ref)

    scores(0, s_even)

    def pair_body(j, carry):
        scores(2 * j + 1, s_odd)
        consume(2 * j, s_even, False)
        scores(2 * j + 2, s_even)
        consume(2 * j + 1, s_odd, False)
        return carry

    lax.fori_loop(0, qb, pair_body, 0)
    scores(2 * qb + 1, s_odd)
    consume(2 * qb, s_even, True)
    consume(2 * qb + 1, s_odd, True)
    o_ref[0] = _flash_result_t(acc_ref).astype(o_ref.dtype)


def _fox_attention(qt, k_tok, k_block0, decay_k, vt):
    h, hd, s = qt.shape
    tq = FOX_TQ
    resident = pl.Buffered(1)
    return pl.pallas_call(
        _fox_kernel,
        grid=(h, s // tq),
        in_specs=[pl.BlockSpec((1, hd, tq), lambda hi, qi: (hi, 0, qi)),
                  pl.BlockSpec((s, AUG_K), lambda hi, qi: (0, k_block0 + hi), pipeline_mode=resident),
                  pl.BlockSpec((s, AUG_K), lambda hi, qi: (0, hi), pipeline_mode=resident),
                  pl.BlockSpec((1, hd, s), lambda hi, qi: (hi, 0, 0), pipeline_mode=resident)],
        out_specs=pl.BlockSpec((1, hd, tq), lambda hi, qi: (hi, 0, qi)),
        out_shape=jax.ShapeDtypeStruct((h, hd, s), BF16),
        scratch_shapes=[pltpu.VMEM((AUG_K, tq), BF16),
                        pltpu.VMEM((1, tq), F32), pltpu.VMEM((AUG_V, tq), F32),
                        pltpu.VMEM((FOX_TK, tq), F32), pltpu.VMEM((FOX_TK, tq), F32)],
        compiler_params=_cparams(("arbitrary", "arbitrary")),
        name="fox_attention",
    )(qt, k_tok, decay_k, vt)


def _merge_kernel(on_ref, of_ref, mg_ref, x_ref, wn_ref, wf_ref, wo_ref, g1_ref, lg_ref, lb_ref, o_ref, *, alpha):
    d = x_ref.shape[-1]
    a = _dot(on_ref[...], wn_ref[...])
    b = _dot(of_ref[...], wf_ref[...])
    gm = _sigmoid(mg_ref[...].astype(F32))
    merged = gm[:, 0:d] * a + gm[:, d:2 * d] * b
    y = _dot(merged.astype(BF16), wo_ref[...])
    z = alpha * x_ref[...] + (1.0 + g1_ref[...]) * y
    o_ref[...] = _layer_norm(z) * lg_ref[...] + lb_ref[...]


def _merge_project(o_nsa, o_fox, merge, x, wn, wf, wo, g1, ln_g, ln_b, alpha, tm=256):
    m, d = x.shape
    w = o_nsa.shape[1]
    resident = pl.Buffered(1)
    row = lambda i: (i, 0)
    fixed = lambda i: (0, 0)
    return pl.pallas_call(
        functools.partial(_merge_kernel, alpha=alpha),
        grid=(m // tm,),
        in_specs=[pl.BlockSpec((tm, w), row), pl.BlockSpec((tm, w), row),
                  pl.BlockSpec((tm, 2 * d), row), pl.BlockSpec((tm, d), row),
                  pl.BlockSpec((w, d), fixed, pipeline_mode=resident),
                  pl.BlockSpec((w, d), fixed, pipeline_mode=resident),
                  pl.BlockSpec((d, d), fixed, pipeline_mode=resident),
                  pl.BlockSpec((1, d), fixed), pl.BlockSpec((1, d), fixed), pl.BlockSpec((1, d), fixed)],
        out_specs=pl.BlockSpec((tm, d), row),
        out_shape=jax.ShapeDtypeStruct((m, d), F32),
        compiler_params=_cparams(("parallel",)),
        name="merge_project_ln",
    )(o_nsa, o_fox, merge, x, wn, wf, wo, g1, ln_g, ln_b)


def _router_kernel(x_ref, sc_ref, sh_ref, w_ref, b_ref, u_ref, r_ref):
    u = _layer_norm(x_ref[...]) * (1.0 + sc_ref[...]) + sh_ref[...]
    u_ref[...] = u
    logits = jnp.dot(u, w_ref[...], precision=HIGHEST, preferred_element_type=F32) + b_ref[...]
    lane = lax.broadcasted_iota(jnp.int32, (1, LANES), 1).astype(F32)
    none = float(LANES)
    is_g = lane < N_GROUPS
    lg = jnp.where(is_g, logits, NEG_INF)
    eg = jnp.exp(lg - jnp.max(lg, axis=-1, keepdims=True))
    pg = eg / jnp.sum(eg, axis=-1, keepdims=True)
    p_grp = jnp.max(pg, axis=-1, keepdims=True)
    grp = jnp.min(jnp.where(pg == p_grp, lane, none), axis=-1, keepdims=True)
    lo = N_GROUPS + grp * EXPERTS_PER_GROUP
    is_e = (lane >= lo) & (lane < lo + EXPERTS_PER_GROUP)
    le = jnp.where(is_e, logits, NEG_INF)
    ee = jnp.exp(le - jnp.max(le, axis=-1, keepdims=True))
    pe = jnp.where(is_e, ee / jnp.sum(ee, axis=-1, keepdims=True), -1.0)
    p1 = jnp.max(pe, axis=-1, keepdims=True)
    i1 = jnp.min(jnp.where(pe == p1, lane, none), axis=-1, keepdims=True)
    pe2 = jnp.where(lane == i1, -1.0, pe)
    p2 = jnp.max(pe2, axis=-1, keepdims=True)
    i2 = jnp.min(jnp.where(pe2 == p2, lane, none), axis=-1, keepdims=True)
    den = p1 + p2
    r_ref[...] = jnp.where(lane == 0, i1 - N_GROUPS,
                           jnp.where(lane == 1, i2 - N_GROUPS,
                                     jnp.where(lane == 2, p_grp * p1 / den,
                                               jnp.where(lane == 3, p_grp * p2 / den, 0.0))))


def _router(x1, sc, sh, w_r, b_r, tm=256):
    m, d = x1.shape
    row = lambda i: (i, 0)
    fixed = lambda i: (0, 0)
    return pl.pallas_call(
        _router_kernel,
        grid=(m // tm,),
        in_specs=[pl.BlockSpec((tm, d), row), pl.BlockSpec((1, d), fixed), pl.BlockSpec((1, d), fixed),
                  pl.BlockSpec((d, LANES), fixed), pl.BlockSpec((1, LANES), fixed)],
        out_specs=[pl.BlockSpec((tm, d), row), pl.BlockSpec((tm, LANES), row)],
        out_shape=[jax.ShapeDtypeStruct((m, d), F32), jax.ShapeDtypeStruct((m, LANES), F32)],
        compiler_params=_cparams(("parallel",)),
        name="moe_router",
    )(x1, sc, sh, w_r, b_r)


def _moe_kernel(be_ref, nu_ref, tok_ref, tok_next_ref, dst_ref, rw_ref, u_hbm, wg_ref, wu_ref, wd_ref, out_hbm,
                xbuf, ybuf, wgb, wub, wdb, sem_in, sem_out, *, n_dump0):
    rb = ROW_BLOCK
    i = pl.program_id(0)
    last = nu_ref[0] - 1
    slot = i % 2

    def row_in(r, tok, sl):
        return pltpu.make_async_copy(u_hbm.at[pl.ds(tok, 1), :], xbuf.at[sl, pl.ds(r, 1), :], sem_in.at[sl])

    def row_out(r, dst):
        return pltpu.make_async_copy(ybuf.at[pl.ds(r, 1), :], out_hbm.at[pl.ds(dst, 1), :], sem_out)

    @pl.when(i == 0)
    def _():
        for r in range(rb):
            row_in(r, tok_ref[0, 0, r], 0).start()
        ybuf[...] = jnp.zeros(ybuf.shape, F32)
        pltpu.make_async_copy(ybuf, out_hbm.at[pl.ds(n_dump0, rb), :], sem_out).start()

    prev = be_ref[jnp.maximum(i - 1, 0)]

    @pl.when((i <= last) & ((i == 0) | (be_ref[i] != prev)))
    def _():
        wgb[...] = wg_ref[0].astype(BF16)
        wub[...] = wu_ref[0].astype(BF16)
        wdb[...] = wd_ref[0].astype(BF16)

    @pl.when(i <= last)
    def _():
        for r in range(rb):
            row_in(r, 0, slot).wait()
        xb = xbuf[slot].astype(BF16)
        for r in range(rb):
            row_in(r, tok_next_ref[0, 0, r], 1 - slot).start()
        gate = _dot(xb, wgb[...])
        up = _dot(xb, wub[...])
        hid = (gate * _sigmoid(gate)) * up
        y = _dot(hid.astype(BF16), wdb[...]) * rw_ref[0]
        for r in range(rb):
            row_out(r, 0).wait()
        ybuf[...] = y
        for r in range(rb):
            row_out(r, dst_ref[0, 0, r]).start()

    @pl.when(i == last)
    def _():
        for r in range(rb):
            row_out(r, 0).wait()
        for r in range(rb):
            row_in(r, 0, 1 - slot).wait()


def _moe_experts(u, blk_exp, n_used, row_tok, row_dst, row_w, w_gate, w_up, w_down):
    t, d = u.shape
    n_blocks = blk_exp.shape[0]
    de = w_gate.shape[-1]
    rb = ROW_BLOCK
    tok3 = row_tok.reshape(n_blocks, 1, rb)
    grid_spec = pltpu.PrefetchScalarGridSpec(
        num_scalar_prefetch=2,
        grid=(n_blocks,),
        in_specs=[pl.BlockSpec((1, 1, rb), lambda i, be, nu: (i, 0, 0), memory_space=pltpu.SMEM),
                  pl.BlockSpec((1, 1, rb), lambda i, be, nu: (jnp.minimum(i + 1, n_blocks - 1), 0, 0),
                               memory_space=pltpu.SMEM),
                  pl.BlockSpec((1, 1, rb), lambda i, be, nu: (i, 0, 0), memory_space=pltpu.SMEM),
                  pl.BlockSpec((1, rb, 1), lambda i, be, nu: (i, 0, 0)),
                  pl.BlockSpec(memory_space=pl.ANY),
                  pl.BlockSpec((1, d, de), lambda i, be, nu: (be[i], 0, 0)),
                  pl.BlockSpec((1, d, de), lambda i, be, nu: (be[i], 0, 0)),
                  pl.BlockSpec((1, de, d), lambda i, be, nu: (be[i], 0, 0))],
        out_specs=pl.BlockSpec(memory_space=pl.ANY),
        scratch_shapes=[pltpu.VMEM((2, rb, d), F32), pltpu.VMEM((rb, d), F32),
                        pltpu.VMEM((d, de), BF16), pltpu.VMEM((d, de), BF16), pltpu.VMEM((de, d), BF16),
                        pltpu.SemaphoreType.DMA((2,)), pltpu.SemaphoreType.DMA(())],
    )
    return pl.pallas_call(
        functools.partial(_moe_kernel, n_dump0=2 * t),
        grid_spec=grid_spec,
        out_shape=jax.ShapeDtypeStruct((2 * t + rb, d), F32),
        compiler_params=_cparams(("arbitrary",)),
        name="moe_experts",
    )(blk_exp, n_used, tok3, tok3, row_dst.reshape(n_blocks, 1, rb), row_w.reshape(n_blocks, rb, 1),
      u, w_gate, w_up, w_down)


def _moe_dispatch(route, t):
    k = 2
    eid = route[:, 0:k].astype(jnp.int32).reshape(-1)
    wts = route[:, k:2 * k].reshape(-1)
    n_asg = t * k
    n_rows = n_asg + N_EXPERTS * ROW_BLOCK
    n_blocks = n_rows // ROW_BLOCK
    onehot = (eid[:, None] == jnp.arange(N_EXPERTS, dtype=jnp.int32)[None, :]).astype(jnp.int32)
    rank = jnp.sum((jnp.cumsum(onehot, axis=0) - onehot) * onehot, axis=1)
    counts = jnp.sum(onehot, axis=0)
    padded = (counts + ROW_BLOCK - 1) // ROW_BLOCK * ROW_BLOCK
    pad_end = jnp.cumsum(padded)
    pad_start = pad_end - padded
    dest = jnp.sum(onehot * pad_start[None, :], axis=1) + rank
    asg = jnp.arange(n_asg, dtype=jnp.int32)
    upd = jnp.stack([(asg % k) * t + asg // k, lax.bitcast_convert_type(wts, jnp.int32)], axis=1)
    init = jnp.stack([n_asg + jnp.arange(n_rows, dtype=jnp.int32) % ROW_BLOCK,
                      jnp.zeros((n_rows,), jnp.int32)], axis=1)
    rows = init.at[dest].set(upd)
    row_dst = rows[:, 0]
    row_w = lax.bitcast_convert_type(rows[:, 1], F32)
    row_tok = jnp.where(row_dst < n_asg, row_dst % t, 0)
    blk_start = jnp.arange(n_blocks, dtype=jnp.int32) * ROW_BLOCK
    blk_exp = jnp.minimum(jnp.sum((pad_end[None, :] <= blk_start[:, None]).astype(jnp.int32), axis=1),
                          N_EXPERTS - 1)
    n_used = (pad_end[N_EXPERTS - 1:] // ROW_BLOCK).astype(jnp.int32)
    return blk_exp, n_used, row_tok, row_dst, row_w


def _final_kernel(x_ref, y0_ref, y1_ref, g2_ref, lg_ref, lb_ref, o_ref, *, alpha):
    z = alpha * x_ref[...] + (1.0 + g2_ref[...]) * (y0_ref[...] + y1_ref[...])
    o_ref[...] = _layer_norm(z) * lg_ref[...] + lb_ref[...]


def _final_ln(x1, y2, g2, ln_g, ln_b, alpha, tm=512):
    m, d = x1.shape
    nb = m // tm
    fixed = lambda i: (0, 0)
    return pl.pallas_call(
        functools.partial(_final_kernel, alpha=alpha),
        grid=(nb,),
        in_specs=[pl.BlockSpec((tm, d), lambda i: (i, 0)),
                  pl.BlockSpec((tm, d), lambda i: (i, 0)),
                  pl.BlockSpec((tm, d), lambda i: (i + nb, 0)),
                  pl.BlockSpec((1, d), fixed), pl.BlockSpec((1, d), fixed), pl.BlockSpec((1, d), fixed)],
        out_specs=pl.BlockSpec((tm, d), lambda i: (i, 0)),
        out_shape=jax.ShapeDtypeStruct((m, d), F32),
        compiler_params=_cparams(("parallel",)),
        name="final_ln",
    )(x1, y2, y2, g2, ln_g, ln_b)


def _to_lane_blocks(a, tq):
    g, hpg, s, c = a.shape
    return a.reshape(g, hpg, s // tq, tq, c).transpose(0, 2, 4, 1, 3).reshape(g, s // tq, c, hpg * tq)


def _layer(x2d, c, w_ada, b_ada, w_in, b_fgt, t5_table, cmp_pe, cmp_w1, cmp_b1, cmp_w2,
           w_br_nsa, w_br_fox, w_o, ln1_g, ln1_b, w_rg, b_rg, w_re, b_re,
           w_gate, w_up, w_down, ln2_g, ln2_b, alpha):
    s, d = x2d.shape
    hd = HEAD_DIM
    g = NSA_GROUPS
    mod = _ada_mod(c, w_ada, b_ada)
    sh1, sc1, g1, sh2, sc2, g2 = [mod[:, i * d:(i + 1) * d] for i in range(6)]

    c_q = NSA_HEADS * hd
    c_kv = 6 * g * hd
    c_gate = 3 * NSA_HEADS
    c_fox = 3 * FOX_HEADS * hd
    off_kv = c_q
    off_gate = off_kv + c_kv
    off_fox = off_gate + c_gate
    off_fgt = off_fox + c_fox
    off_merge = off_fgt + FOX_HEADS
    qscale = hd ** -0.5 * LOG2E
    nh = FOX_HEADS
    gw = g * hd

    def kv_cols(z):
        return w_in[:, off_kv + z * gw:off_kv + (z + 1) * gw]

    def lane_padded(w, heads):
        return jnp.pad(w.reshape(d, heads, hd), ((0, 0), (0, 0), (0, AUG_K - hd))).reshape(d, heads * AUG_K)

    fox_q, fox_k, fox_v = [w_in[:, off_fox + i * nh * hd:off_fox + (i + 1) * nh * hd] for i in range(3)]
    w_ch = jnp.concatenate([w_in[:, 0:off_kv] * qscale, kv_cols(3), kv_cols(5), fox_q * qscale, fox_v],
                           axis=1).T.astype(BF16)
    w_tok = jnp.concatenate([kv_cols(0), kv_cols(1), lane_padded(kv_cols(2), g), lane_padded(kv_cols(4), g),
                             lane_padded(fox_k, nh)], axis=1).astype(BF16)
    n_small = c_gate + FOX_HEADS
    w_small = jnp.concatenate([w_in[:, off_gate:off_fox], w_in[:, off_fgt:off_merge],
                               jnp.zeros((d, LANES - n_small), F32)], axis=1).astype(BF16)
    w_merge = w_in[:, off_merge:].astype(BF16)

    u = _ln_mod(x2d, sc1, sh1, BF16)
    ch = _matmul_nt(w_ch, u, BF16, 512, "in_proj_channel_major")
    tok = _matmul(u, w_tok, BF16, 512, w_tok.shape[1] // 2, "in_proj_token_major")
    small = _matmul(u, w_small, F32, 512, LANES, "in_proj_small")
    merge = _matmul(u, w_merge, BF16, 512, 1024, "in_proj_merge")

    qt_nsa = ch[0:c_q].reshape(g, NSA_HPG, hd, s)
    vst = ch[c_q:c_q + gw].reshape(g, hd, s)
    vwt = ch[c_q + gw:c_q + 2 * gw].reshape(g, hd, s)
    fox_qt = ch[c_q + 2 * gw:c_q + 2 * gw + nh * hd].reshape(nh, hd, s)
    fox_vt = ch[c_q + 2 * gw + nh * hd:].reshape(nh, hd, s)
    tok_ks_block0 = 2 * gw // AUG_K
    tok_kw_block0 = tok_ks_block0 + g
    tok_fox_block0 = tok_kw_block0 + g

    kv_cmp_in = tok[:, 0:2 * gw].reshape(s, 2, g, hd).transpose(1, 2, 0, 3)
    kv_cmp = _compress(kv_cmp_in, cmp_pe, cmp_w1, cmp_b1, cmp_w2)
    nch = s // CMP_STRIDE
    ncp = nch + LANES
    nslc = s // SLC_LEN
    kv_cmp_pad = jnp.pad(kv_cmp.astype(BF16), ((0, 0), (0, 0), (CMP_PAD, ncp - nch - CMP_PAD), (0, 0)))
    b_cmp, b_sel, b_win, far3 = _nsa_bias_tables(t5_table)

    row_ix = np.arange(ncp)
    row_ok = (row_ix >= CMP_PAD) & (row_ix < nch - 1 + CMP_PAD)
    row_cols = np.zeros((ncp, 2 * AUG_K - hd), np.float32)
    row_cols[:, ROW_BIAS - hd:ROW_BIAS - hd + 3] = row_ok[:, None]
    row_cols[:, ROW_BIAS - hd + 3] = ~row_ok
    row_cols[row_ix, AUG_K - hd + row_ix // BF16_ROWS] = 1.0
    kc_aug = jnp.concatenate([kv_cmp_pad[0], jnp.broadcast_to(jnp.asarray(row_cols, BF16), (g,) + row_cols.shape)],
                             axis=-1)
    vct = kv_cmp_pad[1].transpose(0, 2, 1)
    nq = s // ATT_TQ
    qt = qt_nsa.reshape(g, NSA_HPG, hd, nq, ATT_TQ).transpose(0, 3, 2, 1, 4).reshape(g, nq, hd, NSA_HPG * ATT_TQ)
    oct_, selt = _nsa_compress_select(qt, kc_aug, vct, b_cmp, far3)

    period = BF16_ROWS * SLC_LEN
    k_extra = np.zeros((period, AUG_K), np.float32)
    k_extra[np.arange(period), ROW_MASK + np.arange(period) // SLC_LEN] = 1.0
    k_extra[:, ROW_BIAS:ROW_BIAS + 3] = 1.0
    gates = small[:, 0:c_gate].reshape(s, g, NSA_HPG, 3).transpose(1, 2, 0, 3)
    gates_t = jnp.pad(_to_lane_blocks(gates, ATT_TQ), ((0, 0), (0, 0), (0, 5), (0, 0)))
    o_nsa_t = _nsa_attention(qt, tok, tok_ks_block0, tok_kw_block0, jnp.asarray(k_extra, BF16), vst, vwt, selt, far3,
                             b_sel, b_win, oct_, gates_t)
    o_nsa = o_nsa_t.reshape(g, nq, hd, NSA_HPG, ATT_TQ).transpose(1, 4, 0, 3, 2).reshape(s, MIX_W)

    fgt_bias = jnp.concatenate([jnp.zeros((c_gate,), F32), b_fgt, jnp.zeros((LANES - n_small,), F32)])[None, :]
    decay_k = _decay_pieces(small, fgt_bias, c_gate, nh)
    o_fox = _fox_attention(fox_qt, tok, tok_fox_block0, decay_k, fox_vt)
    o_fox = o_fox.transpose(2, 0, 1).reshape(s, MIX_W)

    x1 = _merge_project(o_nsa, o_fox, merge, x2d, w_br_nsa.astype(BF16), w_br_fox.astype(BF16),
                        w_o.astype(BF16), g1, ln1_g[None, :], ln1_b[None, :], alpha)

    n_r = N_GROUPS + N_EXPERTS
    w_r = jnp.concatenate([w_rg, w_re.reshape(d, N_EXPERTS), jnp.zeros((d, LANES - n_r), F32)], axis=1)
    b_r = jnp.concatenate([b_rg, b_re.reshape(N_EXPERTS), jnp.zeros((LANES - n_r,), F32)])[None, :]
    u2, route = _router(x1, sc2, sh2, w_r, b_r)
    blk_exp, n_used, row_tok, row_dst, row_w = _moe_dispatch(route, s)
    y2 = _moe_experts(u2, blk_exp, n_used, row_tok, row_dst, row_w, w_gate, w_up, w_down)
    return _final_ln(x1, y2, g2, ln2_g[None, :], ln2_b[None, :], alpha)


def kernel(x, c, w_ada, b_ada, w_in, b_fgt, t5_table, cmp_pe, cmp_w1, cmp_b1, cmp_w2, w_br_nsa, w_br_fox, w_o,
           ln1_g, ln1_b, w_rg, b_rg, w_re, b_re, w_gate, w_up, w_down, ln2_g, ln2_b):
    b, s, d = x.shape
    depth = w_ada.shape[0]
    assert b == 1
    alpha = (2 * depth) ** 0.25
    h = x[0]
    for l in range(depth):
        h = _layer(h, c, w_ada[l], b_ada[l], w_in[l], b_fgt[l], t5_table, cmp_pe[l], cmp_w1[l], cmp_b1[l],
                   cmp_w2[l], w_br_nsa[l], w_br_fox[l], w_o[l], ln1_g[l], ln1_b[l], w_rg[l], b_rg[l],
                   w_re[l], b_re[l], w_gate[l], w_up[l], w_down[l], ln2_g[l], ln2_b[l], alpha)
    return h[None]
```

```python
import functools
import math

import numpy as np
import jax
import jax.numpy as jnp
from jax import lax
from jax.experimental import pallas as pl
from jax.experimental.pallas import tpu as pltpu

F32 = jnp.float32
BF16 = jnp.bfloat16
HIGHEST = lax.Precision.HIGHEST
LOG2E = math.log2(math.e)

HEAD_DIM = 64
NSA_HEADS = 8
NSA_GROUPS = 2
NSA_HPG = NSA_HEADS // NSA_GROUPS
FOX_HEADS = 8
MIX_W = NSA_HEADS * HEAD_DIM
CMP_LEN = 32
CMP_STRIDE = 16
SLC_LEN = 64
SLC_TOPK = 16
WINDOW = 512
T5_BUCKETS = 32
T5_MAX_EXACT = 16
T5_MAX_DIST = 128
N_GROUPS = 8
EXPERTS_PER_GROUP = 8
N_EXPERTS = N_GROUPS * EXPERTS_PER_GROUP
ROW_BLOCK = 128
LN_EPS = 1e-5
NEG_INF = -1e30
M_INIT = -1e29
FORCE_SCORE = 1e4

LANES = 128
BF16_ROWS = 16
CMP_PAD = 8
ATT_TQ = 256
CMP_NEAR_ROWS = 32
FOX_TQ = 1024
FOX_TK = 512
AUG_K = 128
AUG_V = HEAD_DIM + 16
ROW_MASK = HEAD_DIM
ROW_BIAS = HEAD_DIM + 16
VMEM_LIMIT = 56 * 1024 * 1024


def _cparams(sem, vmem=VMEM_LIMIT):
    return pltpu.CompilerParams(dimension_semantics=sem, vmem_limit_bytes=vmem)


def _sigmoid(x):
    return 1.0 / (1.0 + jnp.exp(-x))


def _layer_norm(x):
    mu = jnp.mean(x, axis=-1, keepdims=True)
    xc = x - mu
    var = jnp.mean(xc * xc, axis=-1, keepdims=True)
    return xc * lax.rsqrt(var + LN_EPS)


def _split3(x):
    hi = x.astype(BF16)
    r1 = x - hi.astype(F32)
    mid = r1.astype(BF16)
    lo = (r1 - mid.astype(F32)).astype(BF16)
    return hi, mid, lo


def _split3_exact(x):
    def trunc(v):
        bits = lax.bitcast_convert_type(v, jnp.uint32) & jnp.uint32(0xFFFF0000)
        return lax.bitcast_convert_type(bits, F32)
    hi = trunc(x)
    r1 = x - hi
    mid = trunc(r1)
    lo = r1 - mid
    return hi.astype(BF16), mid.astype(BF16), lo.astype(BF16)


def _dot(a, b):
    return jnp.dot(a, b, preferred_element_type=F32)


def _dot_nt(a, b):
    return lax.dot_general(a, b, (((1,), (1,)), ((), ())), preferred_element_type=F32)


def _dot3(x, w_bf16):
    hi, mid, lo = _split3(x)
    return _dot(hi, w_bf16) + _dot(mid, w_bf16) + _dot(lo, w_bf16)


def _dot3_rhs(w_bf16, x):
    hi, mid, lo = _split3(x)
    return _dot(w_bf16, hi) + _dot(w_bf16, mid) + _dot(w_bf16, lo)


def _ada_kernel(c_ref, w_ref, b_ref, o_ref):
    c = c_ref[...]
    a = c * _sigmoid(c)
    o_ref[...] = jnp.dot(a, w_ref[...], precision=HIGHEST, preferred_element_type=F32) + b_ref[...]


def _ada_mod(c, w, b):
    d, n = w.shape
    tn = 1024
    c8 = jnp.broadcast_to(c, (8, d))
    out = pl.pallas_call(
        _ada_kernel,
        grid=(n // tn,),
        in_specs=[pl.BlockSpec((8, d), lambda j: (0, 0)),
                  pl.BlockSpec((d, tn), lambda j: (0, j)),
                  pl.BlockSpec((1, tn), lambda j: (0, j))],
        out_specs=pl.BlockSpec((8, tn), lambda j: (0, j)),
        out_shape=jax.ShapeDtypeStruct((8, n), F32),
        compiler_params=_cparams(("parallel",)),
        name="ada_mod",
    )(c8, w, b.reshape(1, n))
    return out[0:1]


def _lnmod_kernel(x_ref, sc_ref, sh_ref, o_ref):
    y = _layer_norm(x_ref[...])
    o_ref[...] = (y * (1.0 + sc_ref[...]) + sh_ref[...]).astype(o_ref.dtype)


def _ln_mod(x, sc, sh, out_dtype, tm=512):
    m, d = x.shape
    return pl.pallas_call(
        _lnmod_kernel,
        grid=(m // tm,),
        in_specs=[pl.BlockSpec((tm, d), lambda i: (i, 0)),
                  pl.BlockSpec((1, d), lambda i: (0, 0)),
                  pl.BlockSpec((1, d), lambda i: (0, 0))],
        out_specs=pl.BlockSpec((tm, d), lambda i: (i, 0)),
        out_shape=jax.ShapeDtypeStruct((m, d), out_dtype),
        compiler_params=_cparams(("parallel",)),
        name="ln_mod",
    )(x, sc, sh)


def _mm_kernel(a_ref, w_ref, o_ref):
    o_ref[...] = _dot(a_ref[...], w_ref[...]).astype(o_ref.dtype)


def _matmul(a, w, out_dtype, tm, tn, name):
    m, k = a.shape
    n = w.shape[1]
    return pl.pallas_call(
        _mm_kernel,
        grid=(n // tn, m // tm),
        in_specs=[pl.BlockSpec((tm, k), lambda j, i: (i, 0)),
                  pl.BlockSpec((k, tn), lambda j, i: (0, j))],
        out_specs=pl.BlockSpec((tm, tn), lambda j, i: (i, j)),
        out_shape=jax.ShapeDtypeStruct((m, n), out_dtype),
        compiler_params=_cparams(("parallel", "parallel")),
        name=name,
    )(a, w)


def _mm_nt_kernel(w_ref, a_ref, o_ref):
    o_ref[...] = _dot_nt(w_ref[...], a_ref[...]).astype(o_ref.dtype)


def _matmul_nt(w_t, a, out_dtype, tm, name):
    n, k = w_t.shape
    m = a.shape[0]
    return pl.pallas_call(
        _mm_nt_kernel,
        grid=(m // tm,),
        in_specs=[pl.BlockSpec((n, k), lambda i: (0, 0), pipeline_mode=pl.Buffered(1)),
                  pl.BlockSpec((tm, k), lambda i: (i, 0))],
        out_specs=pl.BlockSpec((n, tm), lambda i: (0, i)),
        out_shape=jax.ShapeDtypeStruct((n, m), out_dtype),
        compiler_params=_cparams(("parallel",)),
        name=name,
    )(w_t, a)


def _gelu_tanh(x):
    return 0.5 * x * (1.0 + jnp.tanh(math.sqrt(2.0 / math.pi) * (x + 0.044715 * (x * x * x))))


def _compress_kernel(c_ref, pe_ref, w1a_ref, w1b_ref, b1_ref, w2_ref, o_ref, *, nch):
    c = c_ref[0, 0]
    w1a = w1a_ref[0]
    w1b = w1b_ref[0]
    half = CMP_STRIDE * HEAD_DIM
    a = _dot(c, w1a)
    b = _dot(c, w1b)
    b_next = pltpu.roll(b, shift=nch - 1, axis=0)
    pe = pe_ref[0]
    pb = _dot(pe[:, :half], w1a) + _dot(pe[:, half:], w1b)
    hid = _gelu_tanh(a + b_next + pb[0:1, :] + b1_ref[0])
    o_ref[0, 0] = _dot(hid.astype(BF16), w2_ref[0])


def _compress(kv_cmp, pe, w1, b1, w2):
    z, g, s, hd = kv_cmp.shape
    nch = s // CMP_STRIDE
    half = CMP_STRIDE * hd
    chunks = kv_cmp.reshape(z, g, nch, half)
    pe8 = jnp.broadcast_to(pe.reshape(z, 1, CMP_LEN * hd), (z, 8, CMP_LEN * hd)).astype(BF16)
    w1b16 = w1.astype(BF16)
    hidn = w1.shape[-1]
    return pl.pallas_call(
        functools.partial(_compress_kernel, nch=nch),
        grid=(z, g),
        in_specs=[pl.BlockSpec((1, 1, nch, half), lambda zi, gi: (zi, gi, 0, 0)),
                  pl.BlockSpec((1, 8, 2 * half), lambda zi, gi: (zi, 0, 0)),
                  pl.BlockSpec((1, half, hidn), lambda zi, gi: (zi, 0, 0)),
                  pl.BlockSpec((1, half, hidn), lambda zi, gi: (zi, 1, 0)),
                  pl.BlockSpec((1, 1, hidn), lambda zi, gi: (zi, 0, 0)),
                  pl.BlockSpec((1, hidn, hd), lambda zi, gi: (zi, 0, 0))],
        out_specs=pl.BlockSpec((1, 1, nch, hd), lambda zi, gi: (zi, gi, 0, 0)),
        out_shape=jax.ShapeDtypeStruct((z, g, nch, hd), F32),
        compiler_params=_cparams(("parallel", "parallel")),
        name="compress_kv",
    )(chunks, pe8, w1b16, w1b16, b1.reshape(z, 1, hidn), w2.astype(BF16))


def _t5_bucket_np(dist):
    n = np.maximum(dist, 0)
    ratio = np.log(np.maximum(n, T5_MAX_EXACT).astype(np.float64) / T5_MAX_EXACT)
    big = T5_MAX_EXACT + (ratio / math.log(T5_MAX_DIST / T5_MAX_EXACT)
                          * (T5_BUCKETS - T5_MAX_EXACT)).astype(np.int64)
    return np.where(n < T5_MAX_EXACT, n, np.minimum(big, T5_BUCKETS - 1)).astype(np.int32)


def _t5_lookup(tbh, dist):
    onehot = np.eye(T5_BUCKETS, dtype=np.float32)[_t5_bucket_np(dist).reshape(-1)]
    vals = jnp.einsum('ghb,nb->ghn', tbh, jnp.asarray(onehot), precision=HIGHEST)
    return vals.reshape(tbh.shape[:2] + dist.shape)


def _toeplitz_t(w, n_keys, n_q):
    length = n_keys + n_q - 1
    w_pad = jnp.concatenate([w, jnp.zeros(w.shape[:-1] + (1,), w.dtype)], axis=-1)
    reps = (1,) * (w.ndim - 1) + (n_keys,)
    flat = jnp.tile(w_pad, reps)[..., :n_keys * length]
    return flat.reshape(w.shape[:-1] + (n_keys, length))[..., n_keys - 1:n_keys - 1 + n_q]


def _att_table_t(tbh, n_keys, tq, lo, hi, minus_far):
    d = np.arange(n_keys + tq - 1) - (tq - 1)
    valid = (d >= lo) & (d < hi)
    vals = _t5_lookup(tbh, d)
    if minus_far:
        vals = vals - tbh[:, :, T5_BUCKETS - 1:]
    w = jnp.where(jnp.asarray(valid), vals * LOG2E, NEG_INF)
    t = _toeplitz_t(w, n_keys, tq)
    g, hpg = tbh.shape[:2]
    return t.transpose(0, 2, 1, 3).reshape(g, n_keys, hpg * tq)


def _nsa_bias_tables(t5_table):
    tbh = t5_table.T.reshape(NSA_GROUPS, NSA_HPG, T5_BUCKETS).astype(F32)
    j = np.arange(CMP_NEAR_ROWS)[:, None]
    i = np.arange(ATT_TQ)[None, :]
    dist = i - (CMP_LEN - 1) - CMP_STRIDE * (j - CMP_PAD)
    vals = (_t5_lookup(tbh, dist) - tbh[:, :, T5_BUCKETS - 1][:, :, None, None]) * LOG2E
    vals = jnp.where(jnp.asarray(dist >= 0)[None, None], vals, NEG_INF)
    b_cmp = vals.transpose(0, 2, 1, 3).reshape(NSA_GROUPS, CMP_NEAR_ROWS, NSA_HPG * ATT_TQ)
    far = tbh[:, :, T5_BUCKETS - 1] * LOG2E
    b_sel = _att_table_t(tbh, 2 * ATT_TQ, ATT_TQ, 0, 1 << 30, True)
    b_sel = jnp.pad(b_sel, ((0, 0), (ATT_TQ, 0), (0, 0)))
    b_win = _att_table_t(tbh, 3 * ATT_TQ, ATT_TQ, 0, WINDOW, False)
    b_win = jnp.pad(b_win, ((0, 0), (0, ATT_TQ), (0, 0)), constant_values=NEG_INF)
    hi, mid, lo = _split3_exact(jnp.repeat(far, ATT_TQ, axis=1))
    far3 = jnp.stack([hi, mid, lo, jnp.full(hi.shape, NEG_INF, BF16)], axis=1)
    far3 = jnp.pad(far3, ((0, 0), (0, BF16_ROWS - 4), (0, 0)))
    return b_cmp, b_sel, b_win, far3


def _slc_weight_np(ncp, nslc):
    w = np.zeros((ncp, nslc), np.float32)
    ratio = SLC_LEN // CMP_STRIDE
    for off, val in ((-1, 0.5), (0, 1.0), (1, 1.0), (2, 1.0), (3, 0.5)):
        j = np.arange(nslc)
        n = ratio * j + off
        ok = (n >= 0) & (n + CMP_PAD < ncp)
        w[n[ok] + CMP_PAD, j[ok]] += val
    return w


def _nsa_cmp_kernel(qt_ref, kc_ref, vct_ref, near_ref, far3_ref, oct_ref, selt_ref, rhs, s_scr, imp_scr, *, nslc):
    tq = ATT_TQ
    lanes = NSA_HPG * tq
    qb = pl.program_id(1)
    cpb = tq // CMP_STRIDE
    rhs[...] = jnp.zeros(rhs.shape, BF16)
    rhs[0:HEAD_DIM, :] = qt_ref[0, 0]
    rhs[ROW_BIAS:ROW_BIAS + BF16_ROWS, :] = far3_ref[0]
    chunk = lax.broadcasted_iota(jnp.int32, (AUG_K, lanes), 0)
    rhs[AUG_K:2 * AUG_K, :] = jnp.where(chunk >= qb + CMP_NEAR_ROWS // BF16_ROWS, NEG_INF, 0.0).astype(BF16)
    s_scr[...] = _dot(kc_ref[0], rhs[...])
    r0 = pl.multiple_of(cpb * qb, BF16_ROWS)
    s_scr[pl.ds(r0, CMP_NEAR_ROWS), :] = s_scr[pl.ds(r0, CMP_NEAR_ROWS), :] + near_ref[0]
    s = s_scr[...]
    m = jnp.max(s, axis=0, keepdims=True)
    e = jnp.exp2(s - m)
    l = jnp.sum(e, axis=0, keepdims=True)
    p = e * jnp.where(m > M_INIT, 1.0 / l, 0.0)
    oct_ref[0, 0] = _dot(vct_ref[0], p.astype(BF16))
    imp = p[:, 0:tq]
    for h in range(1, NSA_HPG):
        imp = imp + p[:, h * tq:(h + 1) * tq]
    n_lane_blocks = tq // LANES
    for c in range(n_lane_blocks):
        imp_scr[c] = imp[:, c * LANES:(c + 1) * LANES]
    ratio = SLC_LEN // CMP_STRIDE

    def taps(off):
        return jnp.concatenate([imp_scr[c, pl.ds(CMP_PAD + off, nslc, stride=ratio), :]
                                for c in range(n_lane_blocks)], axis=1)

    p_slc = 0.5 * (taps(-1) + taps(ratio - 1))
    for off in range(ratio - 1):
        p_slc = p_slc + taps(off)
    blk = lax.broadcasted_iota(jnp.int32, (nslc, tq), 0)
    cur = (qb * tq + lax.broadcasted_iota(jnp.int32, (nslc, tq), 1)) // SLC_LEN
    forced = (blk == 0) | (blk == cur) | (blk == cur - 1)
    score = jnp.where(forced, FORCE_SCORE, jnp.where(blk <= cur, p_slc, -1.0))
    blk_f = blk.astype(F32)
    sel = jnp.zeros((nslc, tq), F32)
    for _ in range(min(SLC_TOPK, nslc)):
        mx = jnp.max(score, axis=0, keepdims=True)
        first = jnp.min(jnp.where(score == mx, blk_f, float(nslc)), axis=0, keepdims=True)
        hit = blk_f == first
        sel = jnp.where(hit, 1.0, sel)
        score = jnp.where(hit, -2.0, score)
    selt_ref[0] = sel.astype(BF16)


def _nsa_compress_select(qt, kc_aug, vct, b_cmp, far3):
    g, nq, hd, lanes = qt.shape
    ncp = kc_aug.shape[1]
    tq = ATT_TQ
    nslc = nq * tq // SLC_LEN
    assert ncp // BF16_ROWS <= AUG_K
    assert tq // CMP_STRIDE == BF16_ROWS
    per_q = lambda gi, qi: (gi, qi, 0, 0)
    per_g = lambda gi, qi: (gi, 0, 0)
    return pl.pallas_call(
        functools.partial(_nsa_cmp_kernel, nslc=nslc),
        grid=(g, nq),
        in_specs=[pl.BlockSpec((1, 1, hd, lanes), per_q),
                  pl.BlockSpec((1, ncp, 2 * AUG_K), per_g),
                  pl.BlockSpec((1, hd, ncp), per_g),
                  pl.BlockSpec((1, CMP_NEAR_ROWS, lanes), per_g),
                  pl.BlockSpec((1, BF16_ROWS, lanes), per_g)],
        out_specs=[pl.BlockSpec((1, 1, hd, lanes), per_q),
                   pl.BlockSpec((1, nslc, tq), lambda gi, qi: (gi, 0, qi))],
        out_shape=[jax.ShapeDtypeStruct((g, nq, hd, lanes), F32),
                   jax.ShapeDtypeStruct((g, nslc, nq * tq), BF16)],
        scratch_shapes=[pltpu.VMEM((2 * AUG_K, lanes), BF16), pltpu.VMEM((ncp, lanes), F32),
                        pltpu.VMEM((tq // LANES, ncp, LANES), F32)],
        compiler_params=_cparams(("parallel", "parallel")),
        name="nsa_compress_select",
    )(qt, kc_aug, vct, b_cmp, far3)


def _flash_init_t(m_ref, acc_ref):
    m_ref[...] = jnp.full(m_ref.shape, M_INIT, F32)
    acc_ref[...] = jnp.zeros(acc_ref.shape, F32)


def _flash_step_t(s, vt_tile, m_ref, acc_ref):
    m_old = m_ref[...]
    m_new = jnp.maximum(m_old, jnp.max(s, axis=0, keepdims=True))
    p = jnp.exp2(s - m_new).astype(BF16)
    acc_ref[...] = jnp.exp2(m_old - m_new) * acc_ref[...] + _dot(vt_tile, p)
    m_ref[...] = m_new


def _flash_result_t(acc_ref):
    acc = acc_ref[...]
    return acc[0:HEAD_DIM, :] / acc[HEAD_DIM:HEAD_DIM + 1, :]


def _nsa_att_kernel(qt_ref, ks_ref, kx_ref, vst_ref, kw_ref, vwt_ref, selt_ref, far3_ref, bsel_ref, bwin_ref,
                    oct_ref, gate_ref, o_ref, rhs_s, rhs_w, mask_t, ms, accs, mw, accw, s_even, s_odd):
    tq = ATT_TQ
    qb = pl.program_id(1)
    ones_rows = _ones_rows(tq)
    kx_tiles = kx_ref.shape[0] // tq
    qt = qt_ref[0, 0]
    rhs_s[...] = jnp.zeros(rhs_s.shape, BF16)
    rhs_s[0:HEAD_DIM, :] = qt
    rhs_s[ROW_BIAS:ROW_BIAS + BF16_ROWS, :] = far3_ref[0]
    rhs_w[...] = jnp.zeros(rhs_w.shape, BF16)
    rhs_w[0:HEAD_DIM, :] = qt
    madd = ((selt_ref[0].astype(F32) - 1.0) * (-NEG_INF)).astype(BF16)
    mask_t[...] = jnp.concatenate([madd] * NSA_HPG, axis=1)
    _flash_init_t(ms, accs)
    _flash_init_t(mw, accw)
    blocks_per_tile = tq // SLC_LEN

    def sel_scores(kt, s_ref):
        kt = jnp.minimum(kt, qb)
        k0 = pl.multiple_of(kt * tq, tq)
        chunk = pl.multiple_of((kt * blocks_per_tile) // BF16_ROWS * BF16_ROWS, BF16_ROWS)
        rhs_s[ROW_MASK:ROW_MASK + BF16_ROWS, :] = mask_t[pl.ds(chunk, BF16_ROWS), :]
        rel = jnp.clip(kt - qb + 2, 0, 2)
        table = bsel_ref[0, pl.ds(pl.multiple_of(rel * tq, tq), tq), :]
        k_aug = ks_ref[pl.ds(k0, tq), :] + kx_ref[pl.ds(pl.multiple_of((kt % kx_tiles) * tq, tq), tq), :]
        s_ref[...] = _dot(k_aug, rhs_s[...]) + table

    def sel_consume(kt, s_ref):
        k0 = pl.multiple_of(kt * tq, tq)
        vt_aug = jnp.concatenate([vst_ref[0, :, pl.ds(k0, tq)], ones_rows], axis=0)
        _flash_step_t(s_ref[...], vt_aug, ms, accs)

    def win_scores(j, s_ref):
        kt = qb - 2 + j
        k0 = pl.multiple_of(jnp.maximum(kt, 0) * tq, tq)
        row = pl.multiple_of(jnp.where(kt >= 0, j, 3) * tq, tq)
        s_ref[...] = _dot(kw_ref[pl.ds(k0, tq), :], rhs_w[...]) + bwin_ref[0, pl.ds(row, tq), :]

    def win_consume(j, s_ref):
        k0 = pl.multiple_of(jnp.maximum(qb - 2 + j, 0) * tq, tq)
        vt_aug = jnp.concatenate([vwt_ref[0, :, pl.ds(k0, tq)], ones_rows], axis=0)
        _flash_step_t(s_ref[...], vt_aug, mw, accw)

    n_sel = qb + 1
    sel_scores(0, s_even)

    def pair_body(j, carry):
        sel_scores(2 * j + 1, s_odd)
        sel_consume(2 * j, s_even)
        sel_scores(2 * j + 2, s_even)
        sel_consume(2 * j + 1, s_odd)
        return carry

    lax.fori_loop(0, n_sel // 2, pair_body, 0)
    win_scores(0, s_odd)

    @pl.when(n_sel % 2 == 1)
    def _():
        sel_consume(qb, s_even)

    win_scores(1, s_even)
    win_consume(0, s_odd)
    win_scores(2, s_odd)
    win_consume(1, s_even)
    win_consume(2, s_odd)

    gt = _sigmoid(gate_ref[0, 0])
    out = gt[0:1, :] * oct_ref[0, 0] + gt[1:2, :] * _flash_result_t(accs) + gt[2:3, :] * _flash_result_t(accw)
    for h in range(NSA_HPG):
        o_ref[0, h] = out[:, h * tq:(h + 1) * tq].astype(o_ref.dtype)


def _nsa_attention(qt, k_tok, ks_block0, kw_block0, k_extra, vst, vwt, selt, far3, b_sel, b_win, oct_, gates_t):
    g, nq, hd, lanes = qt.shape
    s = k_tok.shape[0]
    nslc = selt.shape[1]
    tq = ATT_TQ
    resident = pl.Buffered(1)
    per_q = lambda gi, qi: (gi, qi, 0, 0)
    per_g = lambda gi, qi: (gi, 0, 0)
    return pl.pallas_call(
        _nsa_att_kernel,
        grid=(g, nq),
        in_specs=[pl.BlockSpec((1, 1, hd, lanes), per_q),
                  pl.BlockSpec((s, AUG_K), lambda gi, qi: (0, ks_block0 + gi), pipeline_mode=resident),
                  pl.BlockSpec(k_extra.shape, lambda gi, qi: (0, 0), pipeline_mode=resident),
                  pl.BlockSpec((1, hd, s), per_g, pipeline_mode=resident),
                  pl.BlockSpec((s, AUG_K), lambda gi, qi: (0, kw_block0 + gi), pipeline_mode=resident),
                  pl.BlockSpec((1, hd, s), per_g, pipeline_mode=resident),
                  pl.BlockSpec((1, nslc, tq), lambda gi, qi: (gi, 0, qi)),
                  pl.BlockSpec((1, BF16_ROWS, lanes), per_g),
                  pl.BlockSpec((1, 3 * tq, lanes), per_g, pipeline_mode=resident),
                  pl.BlockSpec((1, 4 * tq, lanes), per_g, pipeline_mode=resident),
                  pl.BlockSpec((1, 1, hd, lanes), per_q),
                  pl.BlockSpec((1, 1, 8, lanes), per_q)],
        out_specs=pl.BlockSpec((1, lanes // tq, hd, tq), lambda gi, qi: (gi, 0, 0, qi)),
        out_shape=jax.ShapeDtypeStruct((g, lanes // tq, hd, s), BF16),
        scratch_shapes=[pltpu.VMEM((AUG_K, lanes), BF16), pltpu.VMEM((AUG_K, lanes), BF16),
                        pltpu.VMEM((nslc, lanes), BF16),
                        pltpu.VMEM((1, lanes), F32), pltpu.VMEM((AUG_V, lanes), F32),
                        pltpu.VMEM((1, lanes), F32), pltpu.VMEM((AUG_V, lanes), F32),
                        pltpu.VMEM((tq, lanes), F32), pltpu.VMEM((tq, lanes), F32)],
        compiler_params=_cparams(("arbitrary", "arbitrary")),
        name="nsa_select_window",
    )(qt, k_tok, k_extra, vst, k_tok, vwt, selt, far3, b_sel, b_win, oct_, gates_t)


def _decay_kernel(z_ref, b_ref, place_ref, o_ref, carry_ref, *, tb):
    @pl.when(pl.program_id(0) == 0)
    def _():
        carry_ref[...] = jnp.zeros(carry_ref.shape, F32)

    z = z_ref[...] + b_ref[...]
    log_f = jnp.minimum(z, 0.0) - jnp.log1p(jnp.exp(-jnp.abs(z)))
    r = lax.broadcasted_iota(jnp.int32, (tb, tb), 0)
    c = lax.broadcasted_iota(jnp.int32, (tb, tb), 1)
    tri = jnp.where(r >= c, 1.0, 0.0).astype(BF16)
    run = _dot3_rhs(tri, log_f) + carry_ref[...]
    carry_ref[...] = run[tb - 1:tb, :]
    hi, mid, lo = _split3_exact(-run * LOG2E)
    o_ref[...] = (_dot(hi, place_ref[0]) + _dot(mid, place_ref[1]) + _dot(lo, place_ref[2])).astype(BF16)


def _decay_pieces(z, bias, first_lane, n_heads, tb=512):
    s, n = z.shape
    place = np.zeros((3, n, n_heads * AUG_K), np.float32)
    for h in range(n_heads):
        for piece in range(3):
            place[piece, first_lane + h, h * AUG_K + HEAD_DIM + piece] = 1.0
    return pl.pallas_call(
        functools.partial(_decay_kernel, tb=tb),
        grid=(s // tb,),
        in_specs=[pl.BlockSpec((tb, n), lambda i: (i, 0)),
                  pl.BlockSpec((1, n), lambda i: (0, 0)),
                  pl.BlockSpec((3, n, n_heads * AUG_K), lambda i: (0, 0, 0))],
        out_specs=pl.BlockSpec((tb, n_heads * AUG_K), lambda i: (i, 0)),
        out_shape=jax.ShapeDtypeStruct((s, n_heads * AUG_K), BF16),
        scratch_shapes=[pltpu.VMEM((1, n), F32)],
        compiler_params=_cparams(("arbitrary",)),
        name="decay_cumsum",
    )(z, bias, jnp.asarray(place, BF16))


def _ones_rows(width):
    return jnp.where(lax.broadcasted_iota(jnp.int32, (BF16_ROWS, width), 0) < 8, 1.0, 0.0).astype(BF16)


def _fox_kernel(qt_ref, k_ref, dk_ref, vt_ref, o_ref, rhs, m_ref, acc_ref, s_even, s_odd):
    tq = FOX_TQ
    tk = FOX_TK
    assert tq == 2 * tk
    qb = pl.program_id(1)
    row = lax.broadcasted_iota(jnp.int32, (AUG_K - HEAD_DIM, tq), 0)
    rhs[0:HEAD_DIM, :] = qt_ref[0]
    rhs[HEAD_DIM:AUG_K, :] = jnp.where(row < 3, 1.0, 0.0).astype(BF16)
    ones_rows = _ones_rows(tk)
    _flash_init_t(m_ref, acc_ref)

    def scores(kt, s_ref):
        k0 = pl.multiple_of(kt * tk, tk)
        k_aug = k_ref[pl.ds(k0, tk), :] + dk_ref[pl.ds(k0, tk), :]
        s_ref[...] = _dot(k_aug, rhs[...])

    def consume(kt, s_ref, diagonal):
        k0 = pl.multiple_of(kt * tk, tk)
        s = s_ref[...]
        if diagonal:
            key = k0 + lax.broadcasted_iota(jnp.int32, (tk, tq), 0)
            qry = qb * tq + lax.broadcasted_iota(jnp.int32, (tk, tq), 1)
            s = jnp.where(key <= qry, s, NEG_INF)
        vt_aug = jnp.concatenate([vt_ref[0, :, pl.ds(k0, tk)], ones_rows], axis=0)
        _flash_step_t(s, vt_aug, m_ref, acc_ref)

    scores(0, s_even)

    def pair_body(j, carry):
        scores(2 * j + 1, s_odd)
        consume(2 * j, s_even, False)
        scores(2 * j + 2, s_even)
        consume(2 * j + 1, s_odd, False)
        return carry

    lax.fori_loop(0, qb, pair_body, 0)
    scores(2 * qb + 1, s_odd)
    consume(2 * qb, s_even, True)
    consume(2 * qb + 1, s_odd, True)
    o_ref[0] = _flash_result_t(acc_ref).astype(o_ref.dtype)


def _fox_attention(qt, k_tok, k_block0, decay_k, vt):
    h, hd, s = qt.shape
    tq = FOX_TQ
    resident = pl.Buffered(1)
    return pl.pallas_call(
        _fox_kernel,
        grid=(h, s // tq),
        in_specs=[pl.BlockSpec((1, hd, tq), lambda hi, qi: (hi, 0, qi)),
                  pl.BlockSpec((s, AUG_K), lambda hi, qi: (0, k_block0 + hi), pipeline_mode=resident),
                  pl.BlockSpec((s, AUG_K), lambda hi, qi: (0, hi), pipeline_mode=resident),
                  pl.BlockSpec((1, hd, s), lambda hi, qi: (hi, 0, 0), pipeline_mode=resident)],
        out_specs=pl.BlockSpec((1, hd, tq), lambda hi, qi: (hi, 0, qi)),
        out_shape=jax.ShapeDtypeStruct((h, hd, s), BF16),
        scratch_shapes=[pltpu.VMEM((AUG_K, tq), BF16),
                        pltpu.VMEM((1, tq), F32), pltpu.VMEM((AUG_V, tq), F32),
                        pltpu.VMEM((FOX_TK, tq), F32), pltpu.VMEM((FOX_TK, tq), F32)],
        compiler_params=_cparams(("arbitrary", "arbitrary")),
        name="fox_attention",
    )(qt, k_tok, decay_k, vt)


def _merge_kernel(on_ref, of_ref, mg_ref, x_ref, wn_ref, wf_ref, wo_ref, g1_ref, lg_ref, lb_ref, o_ref, *, alpha):
    d = x_ref.shape[-1]
    tn = (((0,), (0,)), ((), ()))
    a = lax.dot_general(on_ref[...], wn_ref[...], tn, preferred_element_type=F32)
    b = lax.dot_general(of_ref[...], wf_ref[...], tn, preferred_element_type=F32)
    gm = _sigmoid(mg_ref[...].astype(F32))
    merged = gm[:, 0:d] * a + gm[:, d:2 * d] * b
    y = _dot(merged.astype(BF16), wo_ref[...])
    z = alpha * x_ref[...] + (1.0 + g1_ref[...]) * y
    o_ref[...] = _layer_norm(z) * lg_ref[...] + lb_ref[...]


def _merge_project(o_nsa, o_fox, merge, x, wn, wf, wo, g1, ln_g, ln_b, alpha, tm=256):
    m, d = x.shape
    w = o_nsa.shape[0]
    resident = pl.Buffered(1)
    row = lambda i: (i, 0)
    fixed = lambda i: (0, 0)
    return pl.pallas_call(
        functools.partial(_merge_kernel, alpha=alpha),
        grid=(m // tm,),
        in_specs=[pl.BlockSpec((w, tm), lambda i: (0, i)), pl.BlockSpec((w, tm), lambda i: (0, i)),
                  pl.BlockSpec((tm, 2 * d), row), pl.BlockSpec((tm, d), row),
                  pl.BlockSpec((w, d), fixed, pipeline_mode=resident),
                  pl.BlockSpec((w, d), fixed, pipeline_mode=resident),
                  pl.BlockSpec((d, d), fixed, pipeline_mode=resident),
                  pl.BlockSpec((1, d), fixed), pl.BlockSpec((1, d), fixed), pl.BlockSpec((1, d), fixed)],
        out_specs=pl.BlockSpec((tm, d), row),
        out_shape=jax.ShapeDtypeStruct((m, d), F32),
        compiler_params=_cparams(("parallel",)),
        name="merge_project_ln",
    )(o_nsa, o_fox, merge, x, wn, wf, wo, g1, ln_g, ln_b)


def _router_kernel(x_ref, sc_ref, sh_ref, w_ref, b_ref, u_ref, r_ref):
    u = _layer_norm(x_ref[...]) * (1.0 + sc_ref[...]) + sh_ref[...]
    u_ref[...] = u
    u_hi = u.astype(BF16)
    u_lo = (u - u_hi.astype(F32)).astype(BF16)
    logits = _dot(u_hi, w_ref[0]) + _dot(u_lo, w_ref[0]) + _dot(u_hi, w_ref[1]) + b_ref[...]
    lane = lax.broadcasted_iota(jnp.int32, (1, LANES), 1).astype(F32)
    none = float(LANES)
    is_g = lane < N_GROUPS
    lg = jnp.where(is_g, logits, NEG_INF)
    eg = jnp.exp(lg - jnp.max(lg, axis=-1, keepdims=True))
    pg = eg / jnp.sum(eg, axis=-1, keepdims=True)
    p_grp = jnp.max(pg, axis=-1, keepdims=True)
    grp = jnp.min(jnp.where(pg == p_grp, lane, none), axis=-1, keepdims=True)
    lo = N_GROUPS + grp * EXPERTS_PER_GROUP
    is_e = (lane >= lo) & (lane < lo + EXPERTS_PER_GROUP)
    le = jnp.where(is_e, logits, NEG_INF)
    ee = jnp.exp(le - jnp.max(le, axis=-1, keepdims=True))
    pe = jnp.where(is_e, ee / jnp.sum(ee, axis=-1, keepdims=True), -1.0)
    p1 = jnp.max(pe, axis=-1, keepdims=True)
    i1 = jnp.min(jnp.where(pe == p1, lane, none), axis=-1, keepdims=True)
    pe2 = jnp.where(lane == i1, -1.0, pe)
    p2 = jnp.max(pe2, axis=-1, keepdims=True)
    i2 = jnp.min(jnp.where(pe2 == p2, lane, none), axis=-1, keepdims=True)
    den = p1 + p2
    r_ref[...] = jnp.where(lane == 0, i1 - N_GROUPS,
                           jnp.where(lane == 1, i2 - N_GROUPS,
                                     jnp.where(lane == 2, p_grp * p1 / den,
                                               jnp.where(lane == 3, p_grp * p2 / den, 0.0))))


def _router(x1, sc, sh, w_r, b_r, tm=256):
    m, d = x1.shape
    row = lambda i: (i, 0)
    fixed = lambda i: (0, 0)
    return pl.pallas_call(
        _router_kernel,
        grid=(m // tm,),
        in_specs=[pl.BlockSpec((tm, d), row), pl.BlockSpec((1, d), fixed), pl.BlockSpec((1, d), fixed),
                  pl.BlockSpec((2, d, LANES), lambda i: (0, 0, 0)), pl.BlockSpec((1, LANES), fixed)],
        out_specs=[pl.BlockSpec((tm, d), row), pl.BlockSpec((tm, LANES), row)],
        out_shape=[jax.ShapeDtypeStruct((m, d), F32), jax.ShapeDtypeStruct((m, LANES), F32)],
        compiler_params=_cparams(("parallel",)),
        name="moe_router",
    )(x1, sc, sh, w_r, b_r)


def _moe_kernel(be_ref, nu_ref, tok_ref, tok_next_ref, dst_ref, rw_ref, u_hbm, wg_ref, wu_ref, wd_ref, out_hbm,
                xbuf, ybuf, wgb, wub, wdb, sem_in, sem_out, *, n_dump0):
    rb = ROW_BLOCK
    i = pl.program_id(0)
    last = nu_ref[0] - 1
    slot = i % 2

    def row_in(r, tok, sl):
        return pltpu.make_async_copy(u_hbm.at[pl.ds(tok, 1), :], xbuf.at[sl, pl.ds(r, 1), :], sem_in.at[sl])

    def row_out(r, dst):
        return pltpu.make_async_copy(ybuf.at[pl.ds(r, 1), :], out_hbm.at[pl.ds(dst, 1), :], sem_out)

    @pl.when(i == 0)
    def _():
        for r in range(rb):
            row_in(r, tok_ref[0, 0, r], 0).start()
        ybuf[...] = jnp.zeros(ybuf.shape, F32)
        pltpu.make_async_copy(ybuf, out_hbm.at[pl.ds(n_dump0, rb), :], sem_out).start()

    prev = be_ref[jnp.maximum(i - 1, 0)]

    @pl.when((i <= last) & ((i == 0) | (be_ref[i] != prev)))
    def _():
        wgb[...] = wg_ref[0].astype(BF16)
        wub[...] = wu_ref[0].astype(BF16)
        wdb[...] = wd_ref[0].astype(BF16)

    @pl.when(i <= last)
    def _():
        for r in range(rb):
            row_in(r, 0, slot).wait()
        xb = xbuf[slot].astype(BF16)
        for r in range(rb):
            row_in(r, tok_next_ref[0, 0, r], 1 - slot).start()
        gate = _dot(xb, wgb[...])
        up = _dot(xb, wub[...])
        hid = (gate * _sigmoid(gate)) * up
        y = _dot(hid.astype(BF16), wdb[...]) * rw_ref[0]
        for r in range(rb):
            row_out(r, 0).wait()
        ybuf[...] = y
        for r in range(rb):
            row_out(r, dst_ref[0, 0, r]).start()

    @pl.when(i == last)
    def _():
        for r in range(rb):
            row_out(r, 0).wait()
        for r in range(rb):
            row_in(r, 0, 1 - slot).wait()


def _moe_experts(u, blk_exp, n_used, row_tok, row_dst, row_w, w_gate, w_up, w_down):
    t, d = u.shape
    n_blocks = blk_exp.shape[0]
    de = w_gate.shape[-1]
    rb = ROW_BLOCK
    tok3 = row_tok.reshape(n_blocks, 1, rb)
    grid_spec = pltpu.PrefetchScalarGridSpec(
        num_scalar_prefetch=2,
        grid=(n_blocks,),
        in_specs=[pl.BlockSpec((1, 1, rb), lambda i, be, nu: (i, 0, 0), memory_space=pltpu.SMEM),
                  pl.BlockSpec((1, 1, rb), lambda i, be, nu: (jnp.minimum(i + 1, n_blocks - 1), 0, 0),
                               memory_space=pltpu.SMEM),
                  pl.BlockSpec((1, 1, rb), lambda i, be, nu: (i, 0, 0), memory_space=pltpu.SMEM),
                  pl.BlockSpec((1, rb, 1), lambda i, be, nu: (i, 0, 0)),
                  pl.BlockSpec(memory_space=pl.ANY),
                  pl.BlockSpec((1, d, de), lambda i, be, nu: (be[i], 0, 0)),
                  pl.BlockSpec((1, d, de), lambda i, be, nu: (be[i], 0, 0)),
                  pl.BlockSpec((1, de, d), lambda i, be, nu: (be[i], 0, 0))],
        out_specs=pl.BlockSpec(memory_space=pl.ANY),
        scratch_shapes=[pltpu.VMEM((2, rb, d), F32), pltpu.VMEM((rb, d), F32),
                        pltpu.VMEM((d, de), BF16), pltpu.VMEM((d, de), BF16), pltpu.VMEM((de, d), BF16),
                        pltpu.SemaphoreType.DMA((2,)), pltpu.SemaphoreType.DMA(())],
    )
    return pl.pallas_call(
        functools.partial(_moe_kernel, n_dump0=2 * t),
        grid_spec=grid_spec,
        out_shape=jax.ShapeDtypeStruct((2 * t + rb, d), F32),
        compiler_params=_cparams(("arbitrary",)),
        name="moe_experts",
    )(blk_exp, n_used, tok3, tok3, row_dst.reshape(n_blocks, 1, rb), row_w.reshape(n_blocks, rb, 1),
      u, w_gate, w_up, w_down)


def _moe_dispatch(route, t):
    k = 2
    eid = route[:, 0:k].astype(jnp.int32).reshape(-1)
    wts = route[:, k:2 * k].reshape(-1)
    n_asg = t * k
    n_rows = n_asg + N_EXPERTS * ROW_BLOCK
    n_blocks = n_rows // ROW_BLOCK
    onehot = (eid[:, None] == jnp.arange(N_EXPERTS, dtype=jnp.int32)[None, :]).astype(jnp.int32)
    rank = jnp.sum((jnp.cumsum(onehot, axis=0) - onehot) * onehot, axis=1)
    counts = jnp.sum(onehot, axis=0)
    padded = (counts + ROW_BLOCK - 1) // ROW_BLOCK * ROW_BLOCK
    pad_end = jnp.cumsum(padded)
    pad_start = pad_end - padded
    dest = jnp.sum(onehot * pad_start[None, :], axis=1) + rank
    asg = jnp.arange(n_asg, dtype=jnp.int32)
    upd = jnp.stack([(asg % k) * t + asg // k, lax.bitcast_convert_type(wts, jnp.int32)], axis=1)
    init = jnp.stack([n_asg + jnp.arange(n_rows, dtype=jnp.int32) % ROW_BLOCK,
                      jnp.zeros((n_rows,), jnp.int32)], axis=1)
    rows = init.at[dest].set(upd)
    row_dst = rows[:, 0]
    row_w = lax.bitcast_convert_type(rows[:, 1], F32)
    row_tok = jnp.where(row_dst < n_asg, row_dst % t, 0)
    blk_start = jnp.arange(n_blocks, dtype=jnp.int32) * ROW_BLOCK
    blk_exp = jnp.minimum(jnp.sum((pad_end[None, :] <= blk_start[:, None]).astype(jnp.int32), axis=1),
                          N_EXPERTS - 1)
    n_used = (pad_end[N_EXPERTS - 1:] // ROW_BLOCK).astype(jnp.int32)
    return blk_exp, n_used, row_tok, row_dst, row_w


def _final_kernel(x_ref, y0_ref, y1_ref, g2_ref, lg_ref, lb_ref, o_ref, *, alpha):
    z = alpha * x_ref[...] + (1.0 + g2_ref[...]) * (y0_ref[...] + y1_ref[...])
    o_ref[...] = _layer_norm(z) * lg_ref[...] + lb_ref[...]


def _final_ln(x1, y2, g2, ln_g, ln_b, alpha, tm=512):
    m, d = x1.shape
    nb = m // tm
    fixed = lambda i: (0, 0)
    return pl.pallas_call(
        functools.partial(_final_kernel, alpha=alpha),
        grid=(nb,),
        in_specs=[pl.BlockSpec((tm, d), lambda i: (i, 0)),
                  pl.BlockSpec((tm, d), lambda i: (i, 0)),
                  pl.BlockSpec((tm, d), lambda i: (i + nb, 0)),
                  pl.BlockSpec((1, d), fixed), pl.BlockSpec((1, d), fixed), pl.BlockSpec((1, d), fixed)],
        out_specs=pl.BlockSpec((tm, d), lambda i: (i, 0)),
        out_shape=jax.ShapeDtypeStruct((m, d), F32),
        compiler_params=_cparams(("parallel",)),
        name="final_ln",
    )(x1, y2, y2, g2, ln_g, ln_b)


def _to_lane_blocks(a, tq):
    g, hpg, s, c = a.shape
    return a.reshape(g, hpg, s // tq, tq, c).transpose(0, 2, 4, 1, 3).reshape(g, s // tq, c, hpg * tq)


def _layer(x2d, c, w_ada, b_ada, w_in, b_fgt, t5_table, cmp_pe, cmp_w1, cmp_b1, cmp_w2,
           w_br_nsa, w_br_fox, w_o, ln1_g, ln1_b, w_rg, b_rg, w_re, b_re,
           w_gate, w_up, w_down, ln2_g, ln2_b, alpha):
    s, d = x2d.shape
    hd = HEAD_DIM
    g = NSA_GROUPS
    mod = _ada_mod(c, w_ada, b_ada)
    sh1, sc1, g1, sh2, sc2, g2 = [mod[:, i * d:(i + 1) * d] for i in range(6)]

    c_q = NSA_HEADS * hd
    c_kv = 6 * g * hd
    c_gate = 3 * NSA_HEADS
    c_fox = 3 * FOX_HEADS * hd
    off_kv = c_q
    off_gate = off_kv + c_kv
    off_fox = off_gate + c_gate
    off_fgt = off_fox + c_fox
    off_merge = off_fgt + FOX_HEADS
    qscale = hd ** -0.5 * LOG2E
    nh = FOX_HEADS
    gw = g * hd

    def kv_cols(z):
        return w_in[:, off_kv + z * gw:off_kv + (z + 1) * gw]

    def lane_padded(w, heads):
        return jnp.pad(w.reshape(d, heads, hd), ((0, 0), (0, 0), (0, AUG_K - hd))).reshape(d, heads * AUG_K)

    fox_q, fox_k, fox_v = [w_in[:, off_fox + i * nh * hd:off_fox + (i + 1) * nh * hd] for i in range(3)]
    w_ch = jnp.concatenate([w_in[:, 0:off_kv] * qscale, kv_cols(3), kv_cols(5), fox_q * qscale, fox_v],
                           axis=1).T.astype(BF16)
    w_tok = jnp.concatenate([kv_cols(0), kv_cols(1), lane_padded(kv_cols(2), g), lane_padded(kv_cols(4), g),
                             lane_padded(fox_k, nh)], axis=1).astype(BF16)
    n_small = c_gate + FOX_HEADS
    w_small = jnp.concatenate([w_in[:, off_gate:off_fox], w_in[:, off_fgt:off_merge],
                               jnp.zeros((d, LANES - n_small), F32)], axis=1).astype(BF16)
    w_merge = w_in[:, off_merge:].astype(BF16)

    u = _ln_mod(x2d, sc1, sh1, BF16)
    ch = _matmul_nt(w_ch, u, BF16, 512, "in_proj_channel_major")
    tok = _matmul(u, w_tok, BF16, 512, w_tok.shape[1] // 2, "in_proj_token_major")
    small = _matmul(u, w_small, F32, 512, LANES, "in_proj_small")
    merge = _matmul(u, w_merge, BF16, 512, 1024, "in_proj_merge")

    qt_nsa = ch[0:c_q].reshape(g, NSA_HPG, hd, s)
    vst = ch[c_q:c_q + gw].reshape(g, hd, s)
    vwt = ch[c_q + gw:c_q + 2 * gw].reshape(g, hd, s)
    fox_qt = ch[c_q + 2 * gw:c_q + 2 * gw + nh * hd].reshape(nh, hd, s)
    fox_vt = ch[c_q + 2 * gw + nh * hd:].reshape(nh, hd, s)
    tok_ks_block0 = 2 * gw // AUG_K
    tok_kw_block0 = tok_ks_block0 + g
    tok_fox_block0 = tok_kw_block0 + g

    kv_cmp_in = tok[:, 0:2 * gw].reshape(s, 2, g, hd).transpose(1, 2, 0, 3)
    kv_cmp = _compress(kv_cmp_in, cmp_pe, cmp_w1, cmp_b1, cmp_w2)
    nch = s // CMP_STRIDE
    ncp = nch + LANES
    nslc = s // SLC_LEN
    kv_cmp_pad = jnp.pad(kv_cmp.astype(BF16), ((0, 0), (0, 0), (CMP_PAD, ncp - nch - CMP_PAD), (0, 0)))
    b_cmp, b_sel, b_win, far3 = _nsa_bias_tables(t5_table)

    row_ix = np.arange(ncp)
    row_ok = (row_ix >= CMP_PAD) & (row_ix < nch - 1 + CMP_PAD)
    row_cols = np.zeros((ncp, 2 * AUG_K - hd), np.float32)
    row_cols[:, ROW_BIAS - hd:ROW_BIAS - hd + 3] = row_ok[:, None]
    row_cols[:, ROW_BIAS - hd + 3] = ~row_ok
    row_cols[row_ix, AUG_K - hd + row_ix // BF16_ROWS] = 1.0
    kc_aug = jnp.concatenate([kv_cmp_pad[0], jnp.broadcast_to(jnp.asarray(row_cols, BF16), (g,) + row_cols.shape)],
                             axis=-1)
    vct = kv_cmp_pad[1].transpose(0, 2, 1)
    nq = s // ATT_TQ
    qt = qt_nsa.reshape(g, NSA_HPG, hd, nq, ATT_TQ).transpose(0, 3, 2, 1, 4).reshape(g, nq, hd, NSA_HPG * ATT_TQ)
    oct_, selt = _nsa_compress_select(qt, kc_aug, vct, b_cmp, far3)

    period = BF16_ROWS * SLC_LEN
    k_extra = np.zeros((period, AUG_K), np.float32)
    k_extra[np.arange(period), ROW_MASK + np.arange(period) // SLC_LEN] = 1.0
    k_extra[:, ROW_BIAS:ROW_BIAS + 3] = 1.0
    gates = small[:, 0:c_gate].reshape(s, g, NSA_HPG, 3).transpose(1, 2, 0, 3)
    gates_t = jnp.pad(_to_lane_blocks(gates, ATT_TQ), ((0, 0), (0, 0), (0, 5), (0, 0)))
    o_nsa_t = _nsa_attention(qt, tok, tok_ks_block0, tok_kw_block0, jnp.asarray(k_extra, BF16), vst, vwt, selt, far3,
                             b_sel, b_win, oct_, gates_t)
    o_nsa = o_nsa_t.reshape(MIX_W, s)

    fgt_bias = jnp.concatenate([jnp.zeros((c_gate,), F32), b_fgt, jnp.zeros((LANES - n_small,), F32)])[None, :]
    decay_k = _decay_pieces(small, fgt_bias, c_gate, nh)
    o_fox = _fox_attention(fox_qt, tok, tok_fox_block0, decay_k, fox_vt)
    o_fox = o_fox.reshape(MIX_W, s)

    x1 = _merge_project(o_nsa, o_fox, merge, x2d, w_br_nsa.astype(BF16), w_br_fox.astype(BF16),
                        w_o.astype(BF16), g1, ln1_g[None, :], ln1_b[None, :], alpha)

    n_r = N_GROUPS + N_EXPERTS
    w_r = jnp.concatenate([w_rg, w_re.reshape(d, N_EXPERTS), jnp.zeros((d, LANES - n_r), F32)], axis=1)
    b_r = jnp.concatenate([b_rg, b_re.reshape(N_EXPERTS), jnp.zeros((LANES - n_r,), F32)])[None, :]
    w_r_hi, w_r_lo, _ = _split3_exact(w_r)
    u2, route = _router(x1, sc2, sh2, jnp.stack([w_r_hi, w_r_lo]), b_r)
    blk_exp, n_used, row_tok, row_dst, row_w = _moe_dispatch(route, s)
    y2 = _moe_experts(u2, blk_exp, n_used, row_tok, row_dst, row_w, w_gate, w_up, w_down)
    return _final_ln(x1, y2, g2, ln2_g[None, :], ln2_b[None, :], alpha)


def kernel(x, c, w_ada, b_ada, w_in, b_fgt, t5_table, cmp_pe, cmp_w1, cmp_b1, cmp_w2, w_br_nsa, w_br_fox, w_o,
           ln1_g, ln1_b, w_rg, b_rg, w_re, b_re, w_gate, w_up, w_down, ln2_g, ln2_b):
    b, s, d = x.shape
    depth = w_ada.shape[0]
    assert b == 1
    alpha = (2 * depth) ** 0.25
    h = x[0]
    for l in range(depth):
        h = _layer(h, c, w_ada[l], b_ada[l], w_in[l], b_fgt[l], t5_table, cmp_pe[l], cmp_w1[l], cmp_b1[l],
                   cmp_w2[l], w_br_nsa[l], w_br_fox[l], w_o[l], ln1_g[l], ln1_b[l], w_rg[l], b_rg[l],
                   w_re[l], b_re[l], w_gate[l], w_up[l], w_down[l], ln2_g[l], ln2_b[l], alpha)
    return h[None]
```

```python
import functools
import math

import numpy as np
import jax
import jax.numpy as jnp
from jax import lax
from jax.experimental import pallas as pl
from jax.experimental.pallas import tpu as pltpu

F32 = jnp.float32
BF16 = jnp.bfloat16
HIGHEST = lax.Precision.HIGHEST
LOG2E = math.log2(math.e)

HEAD_DIM = 64
NSA_HEADS = 8
NSA_GROUPS = 2
NSA_HPG = NSA_HEADS // NSA_GROUPS
FOX_HEADS = 8
MIX_W = NSA_HEADS * HEAD_DIM
CMP_LEN = 32
CMP_STRIDE = 16
SLC_LEN = 64
SLC_TOPK = 16
WINDOW = 512
T5_BUCKETS = 32
T5_MAX_EXACT = 16
T5_MAX_DIST = 128
N_GROUPS = 8
EXPERTS_PER_GROUP = 8
N_EXPERTS = N_GROUPS * EXPERTS_PER_GROUP
ROW_BLOCK = 256
LN_EPS = 1e-5
NEG_INF = -1e30
M_INIT = -1e29
FORCE_SCORE = 1e4

LANES = 128
BF16_ROWS = 16
CMP_PAD = 8
ATT_TQ = 256
CMP_NEAR_ROWS = 32
FOX_TQ = 1024
FOX_TK = 512
AUG_K = 128
AUG_V = HEAD_DIM + 16
ROW_MASK = HEAD_DIM
ROW_BIAS = HEAD_DIM + 16
VMEM_LIMIT = 56 * 1024 * 1024


def _cparams(sem, vmem=VMEM_LIMIT):
    return pltpu.CompilerParams(dimension_semantics=sem, vmem_limit_bytes=vmem)


def _sigmoid(x):
    return 1.0 / (1.0 + jnp.exp(-x))


def _layer_norm(x):
    mu = jnp.mean(x, axis=-1, keepdims=True)
    xc = x - mu
    var = jnp.mean(xc * xc, axis=-1, keepdims=True)
    return xc * lax.rsqrt(var + LN_EPS)


def _split3(x):
    hi = x.astype(BF16)
    r1 = x - hi.astype(F32)
    mid = r1.astype(BF16)
    lo = (r1 - mid.astype(F32)).astype(BF16)
    return hi, mid, lo


def _split3_exact(x):
    def trunc(v):
        bits = lax.bitcast_convert_type(v, jnp.uint32) & jnp.uint32(0xFFFF0000)
        return lax.bitcast_convert_type(bits, F32)
    hi = trunc(x)
    r1 = x - hi
    mid = trunc(r1)
    lo = r1 - mid
    return hi.astype(BF16), mid.astype(BF16), lo.astype(BF16)


def _dot(a, b):
    return jnp.dot(a, b, preferred_element_type=F32)


def _dot_nt(a, b):
    return lax.dot_general(a, b, (((1,), (1,)), ((), ())), preferred_element_type=F32)


def _dot3(x, w_bf16):
    hi, mid, lo = _split3(x)
    return _dot(hi, w_bf16) + _dot(mid, w_bf16) + _dot(lo, w_bf16)


def _dot3_rhs(w_bf16, x):
    hi, mid, lo = _split3(x)
    return _dot(w_bf16, hi) + _dot(w_bf16, mid) + _dot(w_bf16, lo)


def _ada_kernel(c_ref, w_ref, b_ref, o_ref):
    c = c_ref[...]
    a = c * _sigmoid(c)
    o_ref[...] = jnp.dot(a, w_ref[...], precision=HIGHEST, preferred_element_type=F32) + b_ref[...]


def _ada_mod(c, w, b):
    d, n = w.shape
    tn = 1024
    c8 = jnp.broadcast_to(c, (8, d))
    out = pl.pallas_call(
        _ada_kernel,
        grid=(n // tn,),
        in_specs=[pl.BlockSpec((8, d), lambda j: (0, 0)),
                  pl.BlockSpec((d, tn), lambda j: (0, j)),
                  pl.BlockSpec((1, tn), lambda j: (0, j))],
        out_specs=pl.BlockSpec((8, tn), lambda j: (0, j)),
        out_shape=jax.ShapeDtypeStruct((8, n), F32),
        compiler_params=_cparams(("parallel",)),
        name="ada_mod",
    )(c8, w, b.reshape(1, n))
    return out[0:1]


def _lnmod_kernel(x_ref, sc_ref, sh_ref, o_ref):
    y = _layer_norm(x_ref[...])
    o_ref[...] = (y * (1.0 + sc_ref[...]) + sh_ref[...]).astype(o_ref.dtype)


def _ln_mod(x, sc, sh, out_dtype, tm=512):
    m, d = x.shape
    return pl.pallas_call(
        _lnmod_kernel,
        grid=(m // tm,),
        in_specs=[pl.BlockSpec((tm, d), lambda i: (i, 0)),
                  pl.BlockSpec((1, d), lambda i: (0, 0)),
                  pl.BlockSpec((1, d), lambda i: (0, 0))],
        out_specs=pl.BlockSpec((tm, d), lambda i: (i, 0)),
        out_shape=jax.ShapeDtypeStruct((m, d), out_dtype),
        compiler_params=_cparams(("parallel",)),
        name="ln_mod",
    )(x, sc, sh)


def _mm_kernel(a_ref, w_ref, o_ref):
    o_ref[...] = _dot(a_ref[...], w_ref[...]).astype(o_ref.dtype)


def _matmul(a, w, out_dtype, tm, tn, name):
    m, k = a.shape
    n = w.shape[1]
    return pl.pallas_call(
        _mm_kernel,
        grid=(n // tn, m // tm),
        in_specs=[pl.BlockSpec((tm, k), lambda j, i: (i, 0)),
                  pl.BlockSpec((k, tn), lambda j, i: (0, j))],
        out_specs=pl.BlockSpec((tm, tn), lambda j, i: (i, j)),
        out_shape=jax.ShapeDtypeStruct((m, n), out_dtype),
        compiler_params=_cparams(("parallel", "parallel")),
        name=name,
    )(a, w)


def _mm_nt_kernel(w_ref, a_ref, o_ref):
    o_ref[...] = _dot_nt(w_ref[...], a_ref[...]).astype(o_ref.dtype)


def _matmul_nt(w_t, a, out_dtype, tm, name):
    n, k = w_t.shape
    m = a.shape[0]
    return pl.pallas_call(
        _mm_nt_kernel,
        grid=(m // tm,),
        in_specs=[pl.BlockSpec((n, k), lambda i: (0, 0), pipeline_mode=pl.Buffered(1)),
                  pl.BlockSpec((tm, k), lambda i: (i, 0))],
        out_specs=pl.BlockSpec((n, tm), lambda i: (0, i)),
        out_shape=jax.ShapeDtypeStruct((n, m), out_dtype),
        compiler_params=_cparams(("parallel",)),
        name=name,
    )(w_t, a)


def _gelu_tanh(x):
    return 0.5 * x * (1.0 + jnp.tanh(math.sqrt(2.0 / math.pi) * (x + 0.044715 * (x * x * x))))


def _compress_kernel(c_ref, pe_ref, w1a_ref, w1b_ref, b1_ref, w2_ref, o_ref, *, nch):
    c = c_ref[0, 0]
    w1a = w1a_ref[0]
    w1b = w1b_ref[0]
    half = CMP_STRIDE * HEAD_DIM
    a = _dot(c, w1a)
    b = _dot(c, w1b)
    b_next = pltpu.roll(b, shift=nch - 1, axis=0)
    pe = pe_ref[0]
    pb = _dot(pe[:, :half], w1a) + _dot(pe[:, half:], w1b)
    hid = _gelu_tanh(a + b_next + pb[0:1, :] + b1_ref[0])
    o_ref[0, 0] = _dot(hid.astype(BF16), w2_ref[0])


def _compress(kv_cmp, pe, w1, b1, w2):
    z, g, s, hd = kv_cmp.shape
    nch = s // CMP_STRIDE
    half = CMP_STRIDE * hd
    chunks = kv_cmp.reshape(z, g, nch, half)
    pe8 = jnp.broadcast_to(pe.reshape(z, 1, CMP_LEN * hd), (z, 8, CMP_LEN * hd)).astype(BF16)
    w1b16 = w1.astype(BF16)
    hidn = w1.shape[-1]
    return pl.pallas_call(
        functools.partial(_compress_kernel, nch=nch),
        grid=(z, g),
        in_specs=[pl.BlockSpec((1, 1, nch, half), lambda zi, gi: (zi, gi, 0, 0)),
                  pl.BlockSpec((1, 8, 2 * half), lambda zi, gi: (zi, 0, 0)),
                  pl.BlockSpec((1, half, hidn), lambda zi, gi: (zi, 0, 0)),
                  pl.BlockSpec((1, half, hidn), lambda zi, gi: (zi, 1, 0)),
                  pl.BlockSpec((1, 1, hidn), lambda zi, gi: (zi, 0, 0)),
                  pl.BlockSpec((1, hidn, hd), lambda zi, gi: (zi, 0, 0))],
        out_specs=pl.BlockSpec((1, 1, nch, hd), lambda zi, gi: (zi, gi, 0, 0)),
        out_shape=jax.ShapeDtypeStruct((z, g, nch, hd), F32),
        compiler_params=_cparams(("parallel", "parallel")),
        name="compress_kv",
    )(chunks, pe8, w1b16, w1b16, b1.reshape(z, 1, hidn), w2.astype(BF16))


def _t5_bucket_np(dist):
    n = np.maximum(dist, 0)
    ratio = np.log(np.maximum(n, T5_MAX_EXACT).astype(np.float64) / T5_MAX_EXACT)
    big = T5_MAX_EXACT + (ratio / math.log(T5_MAX_DIST / T5_MAX_EXACT)
                          * (T5_BUCKETS - T5_MAX_EXACT)).astype(np.int64)
    return np.where(n < T5_MAX_EXACT, n, np.minimum(big, T5_BUCKETS - 1)).astype(np.int32)


def _t5_lookup(tbh, dist):
    onehot = np.eye(T5_BUCKETS, dtype=np.float32)[_t5_bucket_np(dist).reshape(-1)]
    vals = jnp.einsum('ghb,nb->ghn', tbh, jnp.asarray(onehot), precision=HIGHEST)
    return vals.reshape(tbh.shape[:2] + dist.shape)


def _toeplitz_t(w, n_keys, n_q):
    length = n_keys + n_q - 1
    w_pad = jnp.concatenate([w, jnp.zeros(w.shape[:-1] + (1,), w.dtype)], axis=-1)
    reps = (1,) * (w.ndim - 1) + (n_keys,)
    flat = jnp.tile(w_pad, reps)[..., :n_keys * length]
    return flat.reshape(w.shape[:-1] + (n_keys, length))[..., n_keys - 1:n_keys - 1 + n_q]


def _att_table_t(tbh, n_keys, tq, lo, hi, minus_far):
    d = np.arange(n_keys + tq - 1) - (tq - 1)
    valid = (d >= lo) & (d < hi)
    vals = _t5_lookup(tbh, d)
    if minus_far:
        vals = vals - tbh[:, :, T5_BUCKETS - 1:]
    w = jnp.where(jnp.asarray(valid), vals * LOG2E, NEG_INF)
    t = _toeplitz_t(w, n_keys, tq)
    g, hpg = tbh.shape[:2]
    return t.transpose(0, 2, 1, 3).reshape(g, n_keys, hpg * tq)


def _nsa_bias_tables(t5_table):
    tbh = t5_table.T.reshape(NSA_GROUPS, NSA_HPG, T5_BUCKETS).astype(F32)
    j = np.arange(CMP_NEAR_ROWS)[:, None]
    i = np.arange(ATT_TQ)[None, :]
    dist = i - (CMP_LEN - 1) - CMP_STRIDE * (j - CMP_PAD)
    vals = (_t5_lookup(tbh, dist) - tbh[:, :, T5_BUCKETS - 1][:, :, None, None]) * LOG2E
    vals = jnp.where(jnp.asarray(dist >= 0)[None, None], vals, NEG_INF)
    b_cmp = vals.transpose(0, 2, 1, 3).reshape(NSA_GROUPS, CMP_NEAR_ROWS, NSA_HPG * ATT_TQ)
    far = tbh[:, :, T5_BUCKETS - 1] * LOG2E
    b_sel = _att_table_t(tbh, 2 * ATT_TQ, ATT_TQ, 0, 1 << 30, True)
    b_sel = jnp.pad(b_sel, ((0, 0), (ATT_TQ, 0), (0, 0)))
    b_win = _att_table_t(tbh, 3 * ATT_TQ, ATT_TQ, 0, WINDOW, False)
    b_win = jnp.pad(b_win, ((0, 0), (0, ATT_TQ), (0, 0)), constant_values=NEG_INF)
    hi, mid, lo = _split3_exact(jnp.repeat(far, ATT_TQ, axis=1))
    far3 = jnp.stack([hi, mid, lo, jnp.full(hi.shape, NEG_INF, BF16)], axis=1)
    far3 = jnp.pad(far3, ((0, 0), (0, BF16_ROWS - 4), (0, 0)))
    return b_cmp, b_sel, b_win, far3


def _slc_weight_np(ncp, nslc):
    w = np.zeros((ncp, nslc), np.float32)
    ratio = SLC_LEN // CMP_STRIDE
    for off, val in ((-1, 0.5), (0, 1.0), (1, 1.0), (2, 1.0), (3, 0.5)):
        j = np.arange(nslc)
        n = ratio * j + off
        ok = (n >= 0) & (n + CMP_PAD < ncp)
        w[n[ok] + CMP_PAD, j[ok]] += val
    return w


def _nsa_cmp_kernel(qt_ref, kc_ref, vct_ref, near_ref, far3_ref, oct_ref, selt_ref, rhs, s_scr, imp_scr, *, nslc):
    tq = ATT_TQ
    lanes = NSA_HPG * tq
    qb = pl.program_id(1)
    cpb = tq // CMP_STRIDE
    rhs[...] = jnp.zeros(rhs.shape, BF16)
    rhs[0:HEAD_DIM, :] = qt_ref[0, 0]
    rhs[ROW_BIAS:ROW_BIAS + BF16_ROWS, :] = far3_ref[0]
    chunk = lax.broadcasted_iota(jnp.int32, (AUG_K, lanes), 0)
    rhs[AUG_K:2 * AUG_K, :] = jnp.where(chunk >= qb + CMP_NEAR_ROWS // BF16_ROWS, NEG_INF, 0.0).astype(BF16)
    s_scr[...] = _dot(kc_ref[0], rhs[...])
    r0 = pl.multiple_of(cpb * qb, BF16_ROWS)
    s_scr[pl.ds(r0, CMP_NEAR_ROWS), :] = s_scr[pl.ds(r0, CMP_NEAR_ROWS), :] + near_ref[0]
    s = s_scr[...]
    m = jnp.max(s, axis=0, keepdims=True)
    e = jnp.exp2(s - m)
    l = jnp.sum(e, axis=0, keepdims=True)
    p = e * jnp.where(m > M_INIT, 1.0 / l, 0.0)
    oct_ref[0, 0] = _dot(vct_ref[0], p.astype(BF16))
    imp = p[:, 0:tq]
    for h in range(1, NSA_HPG):
        imp = imp + p[:, h * tq:(h + 1) * tq]
    n_lane_blocks = tq // LANES
    for c in range(n_lane_blocks):
        imp_scr[c] = imp[:, c * LANES:(c + 1) * LANES]
    ratio = SLC_LEN // CMP_STRIDE

    def taps(off):
        return jnp.concatenate([imp_scr[c, pl.ds(CMP_PAD + off, nslc, stride=ratio), :]
                                for c in range(n_lane_blocks)], axis=1)

    p_slc = 0.5 * (taps(-1) + taps(ratio - 1))
    for off in range(ratio - 1):
        p_slc = p_slc + taps(off)
    blk = lax.broadcasted_iota(jnp.int32, (nslc, tq), 0)
    cur = (qb * tq + lax.broadcasted_iota(jnp.int32, (nslc, tq), 1)) // SLC_LEN
    forced = (blk == 0) | (blk == cur) | (blk == cur - 1)
    score = jnp.where(forced, FORCE_SCORE, jnp.where(blk <= cur, p_slc, -1.0))
    blk_f = blk.astype(F32)
    sel = jnp.zeros((nslc, tq), F32)
    for _ in range(min(SLC_TOPK, nslc)):
        mx = jnp.max(score, axis=0, keepdims=True)
        first = jnp.min(jnp.where(score == mx, blk_f, float(nslc)), axis=0, keepdims=True)
        hit = blk_f == first
        sel = jnp.where(hit, 1.0, sel)
        score = jnp.where(hit, -2.0, score)
    selt_ref[0] = sel.astype(BF16)


def _nsa_compress_select(qt, kc_aug, vct, b_cmp, far3):
    g, nq, hd, lanes = qt.shape
    ncp = kc_aug.shape[1]
    tq = ATT_TQ
    nslc = nq * tq // SLC_LEN
    assert ncp // BF16_ROWS <= AUG_K
    assert tq // CMP_STRIDE == BF16_ROWS
    per_q = lambda gi, qi: (gi, qi, 0, 0)
    per_g = lambda gi, qi: (gi, 0, 0)
    return pl.pallas_call(
        functools.partial(_nsa_cmp_kernel, nslc=nslc),
        grid=(g, nq),
        in_specs=[pl.BlockSpec((1, 1, hd, lanes), per_q),
                  pl.BlockSpec((1, ncp, 2 * AUG_K), per_g),
                  pl.BlockSpec((1, hd, ncp), per_g),
                  pl.BlockSpec((1, CMP_NEAR_ROWS, lanes), per_g),
                  pl.BlockSpec((1, BF16_ROWS, lanes), per_g)],
        out_specs=[pl.BlockSpec((1, 1, hd, lanes), per_q),
                   pl.BlockSpec((1, nslc, tq), lambda gi, qi: (gi, 0, qi))],
        out_shape=[jax.ShapeDtypeStruct((g, nq, hd, lanes), F32),
                   jax.ShapeDtypeStruct((g, nslc, nq * tq), BF16)],
        scratch_shapes=[pltpu.VMEM((2 * AUG_K, lanes), BF16), pltpu.VMEM((ncp, lanes), F32),
                        pltpu.VMEM((tq // LANES, ncp, LANES), F32)],
        compiler_params=_cparams(("parallel", "parallel")),
        name="nsa_compress_select",
    )(qt, kc_aug, vct, b_cmp, far3)


def _flash_init_t(m_ref, acc_ref):
    m_ref[...] = jnp.full(m_ref.shape, M_INIT, F32)
    acc_ref[...] = jnp.zeros(acc_ref.shape, F32)


def _flash_step_t(s, vt_tile, m_ref, acc_ref):
    m_old = m_ref[...]
    m_new = jnp.maximum(m_old, jnp.max(s, axis=0, keepdims=True))
    p = jnp.exp2(s - m_new).astype(BF16)
    acc_ref[...] = jnp.exp2(m_old - m_new) * acc_ref[...] + _dot(vt_tile, p)
    m_ref[...] = m_new


def _flash_result_t(acc_ref):
    acc = acc_ref[...]
    return acc[0:HEAD_DIM, :] / acc[HEAD_DIM:HEAD_DIM + 1, :]


def _nsa_att_kernel(qt_ref, ks_ref, kx_ref, vst_ref, kw_ref, vwt_ref, selt_ref, far3_ref, bsel_ref, bwin_ref,
                    oct_ref, gate_ref, o_ref, rhs_s, rhs_w, mask_t, ms, accs, mw, accw, s_even, s_odd):
    tq = ATT_TQ
    qb = pl.program_id(1)
    ones_rows = _ones_rows(tq)
    kx_tiles = kx_ref.shape[0] // tq
    qt = qt_ref[0, 0]
    rhs_s[...] = jnp.zeros(rhs_s.shape, BF16)
    rhs_s[0:HEAD_DIM, :] = qt
    rhs_s[ROW_BIAS:ROW_BIAS + BF16_ROWS, :] = far3_ref[0]
    rhs_w[...] = jnp.zeros(rhs_w.shape, BF16)
    rhs_w[0:HEAD_DIM, :] = qt
    madd = ((selt_ref[0].astype(F32) - 1.0) * (-NEG_INF)).astype(BF16)
    mask_t[...] = jnp.concatenate([madd] * NSA_HPG, axis=1)
    _flash_init_t(ms, accs)
    _flash_init_t(mw, accw)
    blocks_per_tile = tq // SLC_LEN

    def sel_scores(kt, s_ref):
        kt = jnp.minimum(kt, qb)
        k0 = pl.multiple_of(kt * tq, tq)
        chunk = pl.multiple_of((kt * blocks_per_tile) // BF16_ROWS * BF16_ROWS, BF16_ROWS)
        rhs_s[ROW_MASK:ROW_MASK + BF16_ROWS, :] = mask_t[pl.ds(chunk, BF16_ROWS), :]
        rel = jnp.clip(kt - qb + 2, 0, 2)
        table = bsel_ref[0, pl.ds(pl.multiple_of(rel * tq, tq), tq), :]
        k_aug = ks_ref[pl.ds(k0, tq), :] + kx_ref[pl.ds(pl.multiple_of((kt % kx_tiles) * tq, tq), tq), :]
        s_ref[...] = _dot(k_aug, rhs_s[...]) + table

    def sel_consume(kt, s_ref):
        k0 = pl.multiple_of(kt * tq, tq)
        vt_aug = jnp.concatenate([vst_ref[0, :, pl.ds(k0, tq)], ones_rows], axis=0)
        _flash_step_t(s_ref[...], vt_aug, ms, accs)

    def win_scores(j, s_ref):
        kt = qb - 2 + j
        k0 = pl.multiple_of(jnp.maximum(kt, 0) * tq, tq)
        row = pl.multiple_of(jnp.where(kt >= 0, j, 3) * tq, tq)
        s_ref[...] = _dot(kw_ref[pl.ds(k0, tq), :], rhs_w[...]) + bwin_ref[0, pl.ds(row, tq), :]

    def win_consume(j, s_ref):
        k0 = pl.multiple_of(jnp.maximum(qb - 2 + j, 0) * tq, tq)
        vt_aug = jnp.concatenate([vwt_ref[0, :, pl.ds(k0, tq)], ones_rows], axis=0)
        _flash_step_t(s_ref[...], vt_aug, mw, accw)

    n_sel = qb + 1
    sel_scores(0, s_even)

    def pair_body(j, carry):
        sel_scores(2 * j + 1, s_odd)
        sel_consume(2 * j, s_even)
        sel_scores(2 * j + 2, s_even)
        sel_consume(2 * j + 1, s_odd)
        return carry

    lax.fori_loop(0, n_sel // 2, pair_body, 0)
    win_scores(0, s_odd)

    @pl.when(n_sel % 2 == 1)
    def _():
        sel_consume(qb, s_even)

    win_scores(1, s_even)
    win_consume(0, s_odd)
    win_scores(2, s_odd)
    win_consume(1, s_even)
    win_consume(2, s_odd)

    gt = _sigmoid(gate_ref[0, 0])
    out = gt[0:1, :] * oct_ref[0, 0] + gt[1:2, :] * _flash_result_t(accs) + gt[2:3, :] * _flash_result_t(accw)
    for h in range(NSA_HPG):
        o_ref[0, h] = out[:, h * tq:(h + 1) * tq].astype(o_ref.dtype)


def _nsa_attention(qt, k_tok, ks_block0, kw_block0, k_extra, vst, vwt, selt, far3, b_sel, b_win, oct_, gates_t):
    g, nq, hd, lanes = qt.shape
    s = k_tok.shape[0]
    nslc = selt.shape[1]
    tq = ATT_TQ
    resident = pl.Buffered(1)
    per_q = lambda gi, qi: (gi, qi, 0, 0)
    per_g = lambda gi, qi: (gi, 0, 0)
    return pl.pallas_call(
        _nsa_att_kernel,
        grid=(g, nq),
        in_specs=[pl.BlockSpec((1, 1, hd, lanes), per_q),
                  pl.BlockSpec((s, AUG_K), lambda gi, qi: (0, ks_block0 + gi), pipeline_mode=resident),
                  pl.BlockSpec(k_extra.shape, lambda gi, qi: (0, 0), pipeline_mode=resident),
                  pl.BlockSpec((1, hd, s), per_g, pipeline_mode=resident),
                  pl.BlockSpec((s, AUG_K), lambda gi, qi: (0, kw_block0 + gi), pipeline_mode=resident),
                  pl.BlockSpec((1, hd, s), per_g, pipeline_mode=resident),
                  pl.BlockSpec((1, nslc, tq), lambda gi, qi: (gi, 0, qi)),
                  pl.BlockSpec((1, BF16_ROWS, lanes), per_g),
                  pl.BlockSpec((1, 3 * tq, lanes), per_g, pipeline_mode=resident),
                  pl.BlockSpec((1, 4 * tq, lanes), per_g, pipeline_mode=resident),
                  pl.BlockSpec((1, 1, hd, lanes), per_q),
                  pl.BlockSpec((1, 1, 8, lanes), per_q)],
        out_specs=pl.BlockSpec((1, lanes // tq, hd, tq), lambda gi, qi: (gi, 0, 0, qi)),
        out_shape=jax.ShapeDtypeStruct((g, lanes // tq, hd, s), BF16),
        scratch_shapes=[pltpu.VMEM((AUG_K, lanes), BF16), pltpu.VMEM((AUG_K, lanes), BF16),
                        pltpu.VMEM((nslc, lanes), BF16),
                        pltpu.VMEM((1, lanes), F32), pltpu.VMEM((AUG_V, lanes), F32),
                        pltpu.VMEM((1, lanes), F32), pltpu.VMEM((AUG_V, lanes), F32),
                        pltpu.VMEM((tq, lanes), F32), pltpu.VMEM((tq, lanes), F32)],
        compiler_params=_cparams(("arbitrary", "arbitrary")),
        name="nsa_select_window",
    )(qt, k_tok, k_extra, vst, k_tok, vwt, selt, far3, b_sel, b_win, oct_, gates_t)


def _decay_kernel(z_ref, b_ref, place_ref, o_ref, carry_ref, *, tb):
    @pl.when(pl.program_id(0) == 0)
    def _():
        carry_ref[...] = jnp.zeros(carry_ref.shape, F32)

    z = z_ref[...] + b_ref[...]
    log_f = jnp.minimum(z, 0.0) - jnp.log1p(jnp.exp(-jnp.abs(z)))
    r = lax.broadcasted_iota(jnp.int32, (tb, tb), 0)
    c = lax.broadcasted_iota(jnp.int32, (tb, tb), 1)
    tri = jnp.where(r >= c, 1.0, 0.0).astype(BF16)
    run = _dot3_rhs(tri, log_f) + carry_ref[...]
    carry_ref[...] = run[tb - 1:tb, :]
    hi, mid, lo = _split3_exact(-run * LOG2E)
    o_ref[...] = (_dot(hi, place_ref[0]) + _dot(mid, place_ref[1]) + _dot(lo, place_ref[2])).astype(BF16)


def _decay_pieces(z, bias, first_lane, n_heads, tb=512):
    s, n = z.shape
    place = np.zeros((3, n, n_heads * AUG_K), np.float32)
    for h in range(n_heads):
        for piece in range(3):
            place[piece, first_lane + h, h * AUG_K + HEAD_DIM + piece] = 1.0
    return pl.pallas_call(
        functools.partial(_decay_kernel, tb=tb),
        grid=(s // tb,),
        in_specs=[pl.BlockSpec((tb, n), lambda i: (i, 0)),
                  pl.BlockSpec((1, n), lambda i: (0, 0)),
                  pl.BlockSpec((3, n, n_heads * AUG_K), lambda i: (0, 0, 0))],
        out_specs=pl.BlockSpec((tb, n_heads * AUG_K), lambda i: (i, 0)),
        out_shape=jax.ShapeDtypeStruct((s, n_heads * AUG_K), BF16),
        scratch_shapes=[pltpu.VMEM((1, n), F32)],
        compiler_params=_cparams(("arbitrary",)),
        name="decay_cumsum",
    )(z, bias, jnp.asarray(place, BF16))


def _ones_rows(width):
    return jnp.where(lax.broadcasted_iota(jnp.int32, (BF16_ROWS, width), 0) < 8, 1.0, 0.0).astype(BF16)


def _fox_kernel(qt_ref, k_ref, dk_ref, vt_ref, o_ref, rhs, m_ref, acc_ref, s_even, s_odd):
    tq = FOX_TQ
    tk = FOX_TK
    assert tq == 2 * tk
    qb = pl.program_id(1)
    row = lax.broadcasted_iota(jnp.int32, (AUG_K - HEAD_DIM, tq), 0)
    rhs[0:HEAD_DIM, :] = qt_ref[0]
    rhs[HEAD_DIM:AUG_K, :] = jnp.where(row < 3, 1.0, 0.0).astype(BF16)
    ones_rows = _ones_rows(tk)
    _flash_init_t(m_ref, acc_ref)

    def scores(kt, s_ref):
        k0 = pl.multiple_of(kt * tk, tk)
        k_aug = k_ref[pl.ds(k0, tk), :] + dk_ref[pl.ds(k0, tk), :]
        s_ref[...] = _dot(k_aug, rhs[...])

    def consume(kt, s_ref, diagonal):
        k0 = pl.multiple_of(kt * tk, tk)
        s = s_ref[...]
        if diagonal:
            key = k0 + lax.broadcasted_iota(jnp.int32, (tk, tq), 0)
            qry = qb * tq + lax.broadcasted_iota(jnp.int32, (tk, tq), 1)
            s = jnp.where(key <= qry, s, NEG_INF)
        vt_aug = jnp.concatenate([vt_ref[0, :, pl.ds(k0, tk)], ones_rows], axis=0)
        _flash_step_t(s, vt_aug, m_ref, acc_ref)

    scores(0, s_even)

    def pair_body(j, carry):
        scores(2 * j + 1, s_odd)
        consume(2 * j, s_even, False)
        scores(2 * j + 2, s_even)
        consume(2 * j + 1, s_odd, False)
        return carry

    lax.fori_loop(0, qb, pair_body, 0)
    scores(2 * qb + 1, s_odd)
    consume(2 * qb, s_even, True)
    consume(2 * qb + 1, s_odd, True)
    o_ref[0] = _flash_result_t(acc_ref).astype(o_ref.dtype)


def _fox_attention(qt, k_tok, k_block0, decay_k, vt):
    h, hd, s = qt.shape
    tq = FOX_TQ
    resident = pl.Buffered(1)
    return pl.pallas_call(
        _fox_kernel,
        grid=(h, s // tq),
        in_specs=[pl.BlockSpec((1, hd, tq), lambda hi, qi: (hi, 0, qi)),
                  pl.BlockSpec((s, AUG_K), lambda hi, qi: (0, k_block0 + hi), pipeline_mode=resident),
                  pl.BlockSpec((s, AUG_K), lambda hi, qi: (0, hi), pipeline_mode=resident),
                  pl.BlockSpec((1, hd, s), lambda hi, qi: (hi, 0, 0), pipeline_mode=resident)],
        out_specs=pl.BlockSpec((1, hd, tq), lambda hi, qi: (hi, 0, qi)),
        out_shape=jax.ShapeDtypeStruct((h, hd, s), BF16),
        scratch_shapes=[pltpu.VMEM((AUG_K, tq), BF16),
                        pltpu.VMEM((1, tq), F32), pltpu.VMEM((AUG_V, tq), F32),
                        pltpu.VMEM((FOX_TK, tq), F32), pltpu.VMEM((FOX_TK, tq), F32)],
        compiler_params=_cparams(("arbitrary", "arbitrary")),
        name="fox_attention",
    )(qt, k_tok, decay_k, vt)


def _merge_kernel(on_ref, of_ref, mg_ref, x_ref, wn_ref, wf_ref, wo_ref, g1_ref, lg_ref, lb_ref, o_ref, *, alpha):
    d = x_ref.shape[-1]
    tn = (((0,), (0,)), ((), ()))
    a = lax.dot_general(on_ref[...], wn_ref[...], tn, preferred_element_type=F32)
    b = lax.dot_general(of_ref[...], wf_ref[...], tn, preferred_element_type=F32)
    gm = _sigmoid(mg_ref[...].astype(F32))
    merged = gm[:, 0:d] * a + gm[:, d:2 * d] * b
    y = _dot(merged.astype(BF16), wo_ref[...])
    z = alpha * x_ref[...] + (1.0 + g1_ref[...]) * y
    o_ref[...] = _layer_norm(z) * lg_ref[...] + lb_ref[...]


def _merge_project(o_nsa, o_fox, merge, x, wn, wf, wo, g1, ln_g, ln_b, alpha, tm=256):
    m, d = x.shape
    w = o_nsa.shape[0]
    resident = pl.Buffered(1)
    row = lambda i: (i, 0)
    fixed = lambda i: (0, 0)
    return pl.pallas_call(
        functools.partial(_merge_kernel, alpha=alpha),
        grid=(m // tm,),
        in_specs=[pl.BlockSpec((w, tm), lambda i: (0, i)), pl.BlockSpec((w, tm), lambda i: (0, i)),
                  pl.BlockSpec((tm, 2 * d), row), pl.BlockSpec((tm, d), row),
                  pl.BlockSpec((w, d), fixed, pipeline_mode=resident),
                  pl.BlockSpec((w, d), fixed, pipeline_mode=resident),
                  pl.BlockSpec((d, d), fixed, pipeline_mode=resident),
                  pl.BlockSpec((1, d), fixed), pl.BlockSpec((1, d), fixed), pl.BlockSpec((1, d), fixed)],
        out_specs=pl.BlockSpec((tm, d), row),
        out_shape=jax.ShapeDtypeStruct((m, d), F32),
        compiler_params=_cparams(("parallel",)),
        name="merge_project_ln",
    )(o_nsa, o_fox, merge, x, wn, wf, wo, g1, ln_g, ln_b)


def _router_kernel(x_ref, sc_ref, sh_ref, w_ref, b_ref, u_ref, r_ref):
    u = _layer_norm(x_ref[...]) * (1.0 + sc_ref[...]) + sh_ref[...]
    u_ref[...] = u
    u_hi = u.astype(BF16)
    u_lo = (u - u_hi.astype(F32)).astype(BF16)
    logits = _dot(u_hi, w_ref[0]) + _dot(u_lo, w_ref[0]) + _dot(u_hi, w_ref[1]) + b_ref[...]
    lane = lax.broadcasted_iota(jnp.int32, (1, LANES), 1).astype(F32)
    none = float(LANES)
    is_g = lane < N_GROUPS
    lg = jnp.where(is_g, logits, NEG_INF)
    eg = jnp.exp(lg - jnp.max(lg, axis=-1, keepdims=True))
    pg = eg / jnp.sum(eg, axis=-1, keepdims=True)
    p_grp = jnp.max(pg, axis=-1, keepdims=True)
    grp = jnp.min(jnp.where(pg == p_grp, lane, none), axis=-1, keepdims=True)
    lo = N_GROUPS + grp * EXPERTS_PER_GROUP
    is_e = (lane >= lo) & (lane < lo + EXPERTS_PER_GROUP)
    le = jnp.where(is_e, logits, NEG_INF)
    ee = jnp.exp(le - jnp.max(le, axis=-1, keepdims=True))
    pe = jnp.where(is_e, ee / jnp.sum(ee, axis=-1, keepdims=True), -1.0)
    p1 = jnp.max(pe, axis=-1, keepdims=True)
    i1 = jnp.min(jnp.where(pe == p1, lane, none), axis=-1, keepdims=True)
    pe2 = jnp.where(lane == i1, -1.0, pe)
    p2 = jnp.max(pe2, axis=-1, keepdims=True)
    i2 = jnp.min(jnp.where(pe2 == p2, lane, none), axis=-1, keepdims=True)
    den = p1 + p2
    r_ref[...] = jnp.where(lane == 0, i1 - N_GROUPS,
                           jnp.where(lane == 1, i2 - N_GROUPS,
                                     jnp.where(lane == 2, p_grp * p1 / den,
                                               jnp.where(lane == 3, p_grp * p2 / den, 0.0))))


def _router(x1, sc, sh, w_r, b_r, tm=256):
    m, d = x1.shape
    row = lambda i: (i, 0)
    fixed = lambda i: (0, 0)
    return pl.pallas_call(
        _router_kernel,
        grid=(m // tm,),
        in_specs=[pl.BlockSpec((tm, d), row), pl.BlockSpec((1, d), fixed), pl.BlockSpec((1, d), fixed),
                  pl.BlockSpec((2, d, LANES), lambda i: (0, 0, 0)), pl.BlockSpec((1, LANES), fixed)],
        out_specs=[pl.BlockSpec((tm, d), row), pl.BlockSpec((tm, LANES), row)],
        out_shape=[jax.ShapeDtypeStruct((m, d), F32), jax.ShapeDtypeStruct((m, LANES), F32)],
        compiler_params=_cparams(("parallel",)),
        name="moe_router",
    )(x1, sc, sh, w_r, b_r)


def _moe_kernel(be_ref, nu_ref, tok_ref, tok_next_ref, dst_ref, rw_ref, u_hbm, wg_ref, wu_ref, wd_ref, out_hbm,
                xbuf, ybuf, wgb, wub, wdb, sem_in, sem_out, *, n_dump0):
    rb = ROW_BLOCK
    i = pl.program_id(0)
    last = nu_ref[0] - 1
    slot = i % 2

    def row_in(r, tok, sl):
        return pltpu.make_async_copy(u_hbm.at[pl.ds(tok, 1), :], xbuf.at[sl, pl.ds(r, 1), :], sem_in.at[sl])

    def row_out(r, dst):
        return pltpu.make_async_copy(ybuf.at[pl.ds(r, 1), :], out_hbm.at[pl.ds(dst, 1), :], sem_out)

    @pl.when(i == 0)
    def _():
        for r in range(rb):
            row_in(r, tok_ref[0, 0, r], 0).start(priority=r % 2)
        ybuf[...] = jnp.zeros(ybuf.shape, F32)
        pltpu.make_async_copy(ybuf, out_hbm.at[pl.ds(n_dump0, rb), :], sem_out).start()

    prev = be_ref[jnp.maximum(i - 1, 0)]

    @pl.when((i <= last) & ((i == 0) | (be_ref[i] != prev)))
    def _():
        wgb[...] = wg_ref[0].astype(BF16)
        wub[...] = wu_ref[0].astype(BF16)
        wdb[...] = wd_ref[0].astype(BF16)

    @pl.when(i <= last)
    def _():
        for r in range(rb):
            row_in(r, 0, slot).wait()
        xb = xbuf[slot].astype(BF16)
        for r in range(rb):
            row_in(r, tok_next_ref[0, 0, r], 1 - slot).start(priority=r % 2)
        gate = _dot(xb, wgb[...])
        up = _dot(xb, wub[...])
        hid = (gate * _sigmoid(gate)) * up
        y = _dot(hid.astype(BF16), wdb[...]) * rw_ref[0]
        for r in range(rb):
            row_out(r, 0).wait()
        ybuf[...] = y
        for r in range(rb):
            row_out(r, dst_ref[0, 0, r]).start(priority=r % 2)

    @pl.when(i == last)
    def _():
        for r in range(rb):
            row_out(r, 0).wait()
        for r in range(rb):
            row_in(r, 0, 1 - slot).wait()


def _moe_experts(u, blk_exp, n_used, row_tok, row_dst, row_w, w_gate, w_up, w_down):
    t, d = u.shape
    n_blocks = blk_exp.shape[0]
    de = w_gate.shape[-1]
    rb = ROW_BLOCK
    tok3 = row_tok.reshape(n_blocks, 1, rb)
    grid_spec = pltpu.PrefetchScalarGridSpec(
        num_scalar_prefetch=2,
        grid=(n_blocks,),
        in_specs=[pl.BlockSpec((1, 1, rb), lambda i, be, nu: (i, 0, 0), memory_space=pltpu.SMEM),
                  pl.BlockSpec((1, 1, rb), lambda i, be, nu: (jnp.minimum(i + 1, n_blocks - 1), 0, 0),
                               memory_space=pltpu.SMEM),
                  pl.BlockSpec((1, 1, rb), lambda i, be, nu: (i, 0, 0), memory_space=pltpu.SMEM),
                  pl.BlockSpec((1, rb, 1), lambda i, be, nu: (i, 0, 0)),
                  pl.BlockSpec(memory_space=pl.ANY),
                  pl.BlockSpec((1, d, de), lambda i, be, nu: (be[i], 0, 0)),
                  pl.BlockSpec((1, d, de), lambda i, be, nu: (be[i], 0, 0)),
                  pl.BlockSpec((1, de, d), lambda i, be, nu: (be[i], 0, 0))],
        out_specs=pl.BlockSpec(memory_space=pl.ANY),
        scratch_shapes=[pltpu.VMEM((2, rb, d), F32), pltpu.VMEM((rb, d), F32),
                        pltpu.VMEM((d, de), BF16), pltpu.VMEM((d, de), BF16), pltpu.VMEM((de, d), BF16),
                        pltpu.SemaphoreType.DMA((2,)), pltpu.SemaphoreType.DMA(())],
    )
    return pl.pallas_call(
        functools.partial(_moe_kernel, n_dump0=2 * t),
        grid_spec=grid_spec,
        out_shape=jax.ShapeDtypeStruct((2 * t + rb, d), F32),
        compiler_params=_cparams(("arbitrary",)),
        name="moe_experts",
    )(blk_exp, n_used, tok3, tok3, row_dst.reshape(n_blocks, 1, rb), row_w.reshape(n_blocks, rb, 1),
      u, w_gate, w_up, w_down)


def _moe_dispatch(route, t):
    k = 2
    eid = route[:, 0:k].astype(jnp.int32).reshape(-1)
    wts = route[:, k:2 * k].reshape(-1)
    n_asg = t * k
    n_rows = n_asg + N_EXPERTS * ROW_BLOCK
    n_blocks = n_rows // ROW_BLOCK
    onehot = (eid[:, None] == jnp.arange(N_EXPERTS, dtype=jnp.int32)[None, :]).astype(jnp.int32)
    rank = jnp.sum((jnp.cumsum(onehot, axis=0) - onehot) * onehot, axis=1)
    counts = jnp.sum(onehot, axis=0)
    padded = (counts + ROW_BLOCK - 1) // ROW_BLOCK * ROW_BLOCK
    pad_end = jnp.cumsum(padded)
    pad_start = pad_end - padded
    dest = jnp.sum(onehot * pad_start[None, :], axis=1) + rank
    asg = jnp.arange(n_asg, dtype=jnp.int32)
    upd = jnp.stack([(asg % k) * t + asg // k, lax.bitcast_convert_type(wts, jnp.int32)], axis=1)
    init = jnp.stack([n_asg + jnp.arange(n_rows, dtype=jnp.int32) % ROW_BLOCK,
                      jnp.zeros((n_rows,), jnp.int32)], axis=1)
    rows = init.at[dest].set(upd)
    row_dst = rows[:, 0]
    row_w = lax.bitcast_convert_type(rows[:, 1], F32)
    row_tok = jnp.where(row_dst < n_asg, row_dst % t, 0)
    blk_start = jnp.arange(n_blocks, dtype=jnp.int32) * ROW_BLOCK
    blk_exp = jnp.minimum(jnp.sum((pad_end[None, :] <= blk_start[:, None]).astype(jnp.int32), axis=1),
                          N_EXPERTS - 1)
    n_used = (pad_end[N_EXPERTS - 1:] // ROW_BLOCK).astype(jnp.int32)
    return blk_exp, n_used, row_tok, row_dst, row_w


def _final_kernel(x_ref, y0_ref, y1_ref, g2_ref, lg_ref, lb_ref, o_ref, *, alpha):
    z = alpha * x_ref[...] + (1.0 + g2_ref[...]) * (y0_ref[...] + y1_ref[...])
    o_ref[...] = _layer_norm(z) * lg_ref[...] + lb_ref[...]


def _final_ln(x1, y2, g2, ln_g, ln_b, alpha, tm=512):
    m, d = x1.shape
    nb = m // tm
    fixed = lambda i: (0, 0)
    return pl.pallas_call(
        functools.partial(_final_kernel, alpha=alpha),
        grid=(nb,),
        in_specs=[pl.BlockSpec((tm, d), lambda i: (i, 0)),
                  pl.BlockSpec((tm, d), lambda i: (i, 0)),
                  pl.BlockSpec((tm, d), lambda i: (i + nb, 0)),
                  pl.BlockSpec((1, d), fixed), pl.BlockSpec((1, d), fixed), pl.BlockSpec((1, d), fixed)],
        out_specs=pl.BlockSpec((tm, d), lambda i: (i, 0)),
        out_shape=jax.ShapeDtypeStruct((m, d), F32),
        compiler_params=_cparams(("parallel",)),
        name="final_ln",
    )(x1, y2, y2, g2, ln_g, ln_b)


def _to_lane_blocks(a, tq):
    g, hpg, s, c = a.shape
    return a.reshape(g, hpg, s // tq, tq, c).transpose(0, 2, 4, 1, 3).reshape(g, s // tq, c, hpg * tq)


def _layer(x2d, c, w_ada, b_ada, w_in, b_fgt, t5_table, cmp_pe, cmp_w1, cmp_b1, cmp_w2,
           w_br_nsa, w_br_fox, w_o, ln1_g, ln1_b, w_rg, b_rg, w_re, b_re,
           w_gate, w_up, w_down, ln2_g, ln2_b, alpha):
    s, d = x2d.shape
    hd = HEAD_DIM
    g = NSA_GROUPS
    mod = _ada_mod(c, w_ada, b_ada)
    sh1, sc1, g1, sh2, sc2, g2 = [mod[:, i * d:(i + 1) * d] for i in range(6)]

    c_q = NSA_HEADS * hd
    c_kv = 6 * g * hd
    c_gate = 3 * NSA_HEADS
    c_fox = 3 * FOX_HEADS * hd
    off_kv = c_q
    off_gate = off_kv + c_kv
    off_fox = off_gate + c_gate
    off_fgt = off_fox + c_fox
    off_merge = off_fgt + FOX_HEADS
    qscale = hd ** -0.5 * LOG2E
    nh = FOX_HEADS
    gw = g * hd

    def kv_cols(z):
        return w_in[:, off_kv + z * gw:off_kv + (z + 1) * gw]

    def lane_padded(w, heads):
        return jnp.pad(w.reshape(d, heads, hd), ((0, 0), (0, 0), (0, AUG_K - hd))).reshape(d, heads * AUG_K)

    fox_q, fox_k, fox_v = [w_in[:, off_fox + i * nh * hd:off_fox + (i + 1) * nh * hd] for i in range(3)]
    w_ch = jnp.concatenate([w_in[:, 0:off_kv] * qscale, kv_cols(3), kv_cols(5), fox_q * qscale, fox_v],
                           axis=1).T.astype(BF16)
    w_tok = jnp.concatenate([kv_cols(0), kv_cols(1), lane_padded(kv_cols(2), g), lane_padded(kv_cols(4), g),
                             lane_padded(fox_k, nh)], axis=1).astype(BF16)
    n_small = c_gate + FOX_HEADS
    w_small = jnp.concatenate([w_in[:, off_gate:off_fox], w_in[:, off_fgt:off_merge],
                               jnp.zeros((d, LANES - n_small), F32)], axis=1).astype(BF16)
    w_merge = w_in[:, off_merge:].astype(BF16)

    u = _ln_mod(x2d, sc1, sh1, BF16)
    ch = _matmul_nt(w_ch, u, BF16, 512, "in_proj_channel_major")
    tok = _matmul(u, w_tok, BF16, 512, w_tok.shape[1] // 2, "in_proj_token_major")
    small = _matmul(u, w_small, F32, 512, LANES, "in_proj_small")
    merge = _matmul(u, w_merge, BF16, 512, 1024, "in_proj_merge")

    qt_nsa = ch[0:c_q].reshape(g, NSA_HPG, hd, s)
    vst = ch[c_q:c_q + gw].reshape(g, hd, s)
    vwt = ch[c_q + gw:c_q + 2 * gw].reshape(g, hd, s)
    fox_qt = ch[c_q + 2 * gw:c_q + 2 * gw + nh * hd].reshape(nh, hd, s)
    fox_vt = ch[c_q + 2 * gw + nh * hd:].reshape(nh, hd, s)
    tok_ks_block0 = 2 * gw // AUG_K
    tok_kw_block0 = tok_ks_block0 + g
    tok_fox_block0 = tok_kw_block0 + g

    kv_cmp_in = tok[:, 0:2 * gw].reshape(s, 2, g, hd).transpose(1, 2, 0, 3)
    kv_cmp = _compress(kv_cmp_in, cmp_pe, cmp_w1, cmp_b1, cmp_w2)
    nch = s // CMP_STRIDE
    ncp = nch + LANES
    nslc = s // SLC_LEN
    kv_cmp_pad = jnp.pad(kv_cmp.astype(BF16), ((0, 0), (0, 0), (CMP_PAD, ncp - nch - CMP_PAD), (0, 0)))
    b_cmp, b_sel, b_win, far3 = _nsa_bias_tables(t5_table)

    row_ix = np.arange(ncp)
    row_ok = (row_ix >= CMP_PAD) & (row_ix < nch - 1 + CMP_PAD)
    row_cols = np.zeros((ncp, 2 * AUG_K - hd), np.float32)
    row_cols[:, ROW_BIAS - hd:ROW_BIAS - hd + 3] = row_ok[:, None]
    row_cols[:, ROW_BIAS - hd + 3] = ~row_ok
    row_cols[row_ix, AUG_K - hd + row_ix // BF16_ROWS] = 1.0
    kc_aug = jnp.concatenate([kv_cmp_pad[0], jnp.broadcast_to(jnp.asarray(row_cols, BF16), (g,) + row_cols.shape)],
                             axis=-1)
    vct = kv_cmp_pad[1].transpose(0, 2, 1)
    nq = s // ATT_TQ
    qt = qt_nsa.reshape(g, NSA_HPG, hd, nq, ATT_TQ).transpose(0, 3, 2, 1, 4).reshape(g, nq, hd, NSA_HPG * ATT_TQ)
    oct_, selt = _nsa_compress_select(qt, kc_aug, vct, b_cmp, far3)

    period = BF16_ROWS * SLC_LEN
    k_extra = np.zeros((period, AUG_K), np.float32)
    k_extra[np.arange(period), ROW_MASK + np.arange(period) // SLC_LEN] = 1.0
    k_extra[:, ROW_BIAS:ROW_BIAS + 3] = 1.0
    gates = small[:, 0:c_gate].reshape(s, g, NSA_HPG, 3).transpose(1, 2, 0, 3)
    gates_t = jnp.pad(_to_lane_blocks(gates, ATT_TQ), ((0, 0), (0, 0), (0, 5), (0, 0)))
    o_nsa_t = _nsa_attention(qt, tok, tok_ks_block0, tok_kw_block0, jnp.asarray(k_extra, BF16), vst, vwt, selt, far3,
                             b_sel, b_win, oct_, gates_t)
    o_nsa = o_nsa_t.reshape(MIX_W, s)

    fgt_bias = jnp.concatenate([jnp.zeros((c_gate,), F32), b_fgt, jnp.zeros((LANES - n_small,), F32)])[None, :]
    decay_k = _decay_pieces(small, fgt_bias, c_gate, nh)
    o_fox = _fox_attention(fox_qt, tok, tok_fox_block0, decay_k, fox_vt)
    o_fox = o_fox.reshape(MIX_W, s)

    x1 = _merge_project(o_nsa, o_fox, merge, x2d, w_br_nsa.astype(BF16), w_br_fox.astype(BF16),
                        w_o.astype(BF16), g1, ln1_g[None, :], ln1_b[None, :], alpha)

    n_r = N_GROUPS + N_EXPERTS
    w_r = jnp.concatenate([w_rg, w_re.reshape(d, N_EXPERTS), jnp.zeros((d, LANES - n_r), F32)], axis=1)
    b_r = jnp.concatenate([b_rg, b_re.reshape(N_EXPERTS), jnp.zeros((LANES - n_r,), F32)])[None, :]
    w_r_hi, w_r_lo, _ = _split3_exact(w_r)
    u2, route = _router(x1, sc2, sh2, jnp.stack([w_r_hi, w_r_lo]), b_r)
    blk_exp, n_used, row_tok, row_dst, row_w = _moe_dispatch(route, s)
    y2 = _moe_experts(u2, blk_exp, n_used, row_tok, row_dst, row_w, w_gate, w_up, w_down)
    return _final_ln(x1, y2, g2, ln2_g[None, :], ln2_b[None, :], alpha)


def kernel(x, c, w_ada, b_ada, w_in, b_fgt, t5_table, cmp_pe, cmp_w1, cmp_b1, cmp_w2, w_br_nsa, w_br_fox, w_o,
           ln1_g, ln1_b, w_rg, b_rg, w_re, b_re, w_gate, w_up, w_down, ln2_g, ln2_b):
    b, s, d = x.shape
    depth = w_ada.shape[0]
    assert b == 1
    alpha = (2 * depth) ** 0.25
    h = x[0]
    for l in range(depth):
        h = _layer(h, c, w_ada[l], b_ada[l], w_in[l], b_fgt[l], t5_table, cmp_pe[l], cmp_w1[l], cmp_b1[l],
                   cmp_w2[l], w_br_nsa[l], w_br_fox[l], w_o[l], ln1_g[l], ln1_b[l], w_rg[l], b_rg[l],
                   w_re[l], b_re[l], w_gate[l], w_up[l], w_down[l], ln2_g[l], ln2_b[l], alpha)
    return h[None]
```

```python
import functools
import math

import numpy as np
import jax
import jax.numpy as jnp
from jax import lax
from jax.experimental import pallas as pl
from jax.experimental.pallas import tpu as pltpu

F32 = jnp.float32
BF16 = jnp.bfloat16
HIGHEST = lax.Precision.HIGHEST
LOG2E = math.log2(math.e)

HEAD_DIM = 64
NSA_HEADS = 8
NSA_GROUPS = 2
NSA_HPG = NSA_HEADS // NSA_GROUPS
FOX_HEADS = 8
MIX_W = NSA_HEADS * HEAD_DIM
CMP_LEN = 32
CMP_STRIDE = 16
SLC_LEN = 64
SLC_TOPK = 16
WINDOW = 512
T5_BUCKETS = 32
T5_MAX_EXACT = 16
T5_MAX_DIST = 128
N_GROUPS = 8
EXPERTS_PER_GROUP = 8
N_EXPERTS = N_GROUPS * EXPERTS_PER_GROUP
ROW_BLOCK = 128
LN_EPS = 1e-5
NEG_INF = -1e30
M_INIT = -1e29
FORCE_SCORE = 1e4

LANES = 128
BF16_ROWS = 16
CMP_PAD = 8
ATT_TQ = 256
CMP_NEAR_ROWS = 32
FOX_TQ = 1024
FOX_TK = 512
AUG_K = 128
AUG_V = HEAD_DIM + 16
ROW_MASK = HEAD_DIM
ROW_BIAS = HEAD_DIM + 16
VMEM_LIMIT = 56 * 1024 * 1024


def _cparams(sem, vmem=VMEM_LIMIT):
    return pltpu.CompilerParams(dimension_semantics=sem, vmem_limit_bytes=vmem)


def _sigmoid(x):
    return 1.0 / (1.0 + jnp.exp(-x))


def _layer_norm(x):
    mu = jnp.mean(x, axis=-1, keepdims=True)
    xc = x - mu
    var = jnp.mean(xc * xc, axis=-1, keepdims=True)
    return xc * lax.rsqrt(var + LN_EPS)


def _split3(x):
    hi = x.astype(BF16)
    r1 = x - hi.astype(F32)
    mid = r1.astype(BF16)
    lo = (r1 - mid.astype(F32)).astype(BF16)
    return hi, mid, lo


def _split3_exact(x):
    def trunc(v):
        bits = lax.bitcast_convert_type(v, jnp.uint32) & jnp.uint32(0xFFFF0000)
        return lax.bitcast_convert_type(bits, F32)
    hi = trunc(x)
    r1 = x - hi
    mid = trunc(r1)
    lo = r1 - mid
    return hi.astype(BF16), mid.astype(BF16), lo.astype(BF16)


def _dot(a, b):
    return jnp.dot(a, b, preferred_element_type=F32)


def _dot_nt(a, b):
    return lax.dot_general(a, b, (((1,), (1,)), ((), ())), preferred_element_type=F32)


def _dot3(x, w_bf16):
    hi, mid, lo = _split3(x)
    return _dot(hi, w_bf16) + _dot(mid, w_bf16) + _dot(lo, w_bf16)


def _dot3_rhs(w_bf16, x):
    hi, mid, lo = _split3(x)
    return _dot(w_bf16, hi) + _dot(w_bf16, mid) + _dot(w_bf16, lo)


def _ada_kernel(c_ref, w_ref, b_ref, o_ref):
    c = c_ref[...]
    a = c * _sigmoid(c)
    o_ref[...] = jnp.dot(a, w_ref[...], precision=HIGHEST, preferred_element_type=F32) + b_ref[...]


def _ada_mod(c, w, b):
    d, n = w.shape
    tn = 1024
    c8 = jnp.broadcast_to(c, (8, d))
    out = pl.pallas_call(
        _ada_kernel,
        grid=(n // tn,),
        in_specs=[pl.BlockSpec((8, d), lambda j: (0, 0)),
                  pl.BlockSpec((d, tn), lambda j: (0, j)),
                  pl.BlockSpec((1, tn), lambda j: (0, j))],
        out_specs=pl.BlockSpec((8, tn), lambda j: (0, j)),
        out_shape=jax.ShapeDtypeStruct((8, n), F32),
        compiler_params=_cparams(("parallel",)),
        name="ada_mod",
    )(c8, w, b.reshape(1, n))
    return out[0:1]


def _lnmod_kernel(x_ref, sc_ref, sh_ref, o_ref):
    y = _layer_norm(x_ref[...])
    o_ref[...] = (y * (1.0 + sc_ref[...]) + sh_ref[...]).astype(o_ref.dtype)


def _ln_mod(x, sc, sh, out_dtype, tm=512):
    m, d = x.shape
    return pl.pallas_call(
        _lnmod_kernel,
        grid=(m // tm,),
        in_specs=[pl.BlockSpec((tm, d), lambda i: (i, 0)),
                  pl.BlockSpec((1, d), lambda i: (0, 0)),
                  pl.BlockSpec((1, d), lambda i: (0, 0))],
        out_specs=pl.BlockSpec((tm, d), lambda i: (i, 0)),
        out_shape=jax.ShapeDtypeStruct((m, d), out_dtype),
        compiler_params=_cparams(("parallel",)),
        name="ln_mod",
    )(x, sc, sh)


def _mm_kernel(a_ref, w_ref, o_ref):
    o_ref[...] = _dot(a_ref[...], w_ref[...]).astype(o_ref.dtype)


def _matmul(a, w, out_dtype, tm, tn, name):
    m, k = a.shape
    n = w.shape[1]
    return pl.pallas_call(
        _mm_kernel,
        grid=(n // tn, m // tm),
        in_specs=[pl.BlockSpec((tm, k), lambda j, i: (i, 0)),
                  pl.BlockSpec((k, tn), lambda j, i: (0, j))],
        out_specs=pl.BlockSpec((tm, tn), lambda j, i: (i, j)),
        out_shape=jax.ShapeDtypeStruct((m, n), out_dtype),
        compiler_params=_cparams(("parallel", "parallel")),
        name=name,
    )(a, w)


def _mm_nt_kernel(w_ref, a_ref, o_ref):
    o_ref[...] = _dot_nt(w_ref[...], a_ref[...]).astype(o_ref.dtype)


def _matmul_nt(w_t, a, out_dtype, tm, name):
    n, k = w_t.shape
    m = a.shape[0]
    return pl.pallas_call(
        _mm_nt_kernel,
        grid=(m // tm,),
        in_specs=[pl.BlockSpec((n, k), lambda i: (0, 0), pipeline_mode=pl.Buffered(1)),
                  pl.BlockSpec((tm, k), lambda i: (i, 0))],
        out_specs=pl.BlockSpec((n, tm), lambda i: (0, i)),
        out_shape=jax.ShapeDtypeStruct((n, m), out_dtype),
        compiler_params=_cparams(("parallel",)),
        name=name,
    )(w_t, a)


def _gelu_tanh(x):
    return 0.5 * x * (1.0 + jnp.tanh(math.sqrt(2.0 / math.pi) * (x + 0.044715 * (x * x * x))))


def _compress_kernel(c_ref, pe_ref, w1a_ref, w1b_ref, b1_ref, w2_ref, o_ref, *, nch):
    c = c_ref[0, 0]
    w1a = w1a_ref[0]
    w1b = w1b_ref[0]
    half = CMP_STRIDE * HEAD_DIM
    a = _dot(c, w1a)
    b = _dot(c, w1b)
    b_next = pltpu.roll(b, shift=nch - 1, axis=0)
    pe = pe_ref[0]
    pb = _dot(pe[:, :half], w1a) + _dot(pe[:, half:], w1b)
    hid = _gelu_tanh(a + b_next + pb[0:1, :] + b1_ref[0])
    o_ref[0, 0] = _dot(hid.astype(BF16), w2_ref[0])


def _compress(kv_cmp, pe, w1, b1, w2):
    z, g, s, hd = kv_cmp.shape
    nch = s // CMP_STRIDE
    half = CMP_STRIDE * hd
    chunks = kv_cmp.reshape(z, g, nch, half)
    pe8 = jnp.broadcast_to(pe.reshape(z, 1, CMP_LEN * hd), (z, 8, CMP_LEN * hd)).astype(BF16)
    w1b16 = w1.astype(BF16)
    hidn = w1.shape[-1]
    return pl.pallas_call(
        functools.partial(_compress_kernel, nch=nch),
        grid=(z, g),
        in_specs=[pl.BlockSpec((1, 1, nch, half), lambda zi, gi: (zi, gi, 0, 0)),
                  pl.BlockSpec((1, 8, 2 * half), lambda zi, gi: (zi, 0, 0)),
                  pl.BlockSpec((1, half, hidn), lambda zi, gi: (zi, 0, 0)),
                  pl.BlockSpec((1, half, hidn), lambda zi, gi: (zi, 1, 0)),
                  pl.BlockSpec((1, 1, hidn), lambda zi, gi: (zi, 0, 0)),
                  pl.BlockSpec((1, hidn, hd), lambda zi, gi: (zi, 0, 0))],
        out_specs=pl.BlockSpec((1, 1, nch, hd), lambda zi, gi: (zi, gi, 0, 0)),
        out_shape=jax.ShapeDtypeStruct((z, g, nch, hd), F32),
        compiler_params=_cparams(("parallel", "parallel")),
        name="compress_kv",
    )(chunks, pe8, w1b16, w1b16, b1.reshape(z, 1, hidn), w2.astype(BF16))


def _t5_bucket_np(dist):
    n = np.maximum(dist, 0)
    ratio = np.log(np.maximum(n, T5_MAX_EXACT).astype(np.float64) / T5_MAX_EXACT)
    big = T5_MAX_EXACT + (ratio / math.log(T5_MAX_DIST / T5_MAX_EXACT)
                          * (T5_BUCKETS - T5_MAX_EXACT)).astype(np.int64)
    return np.where(n < T5_MAX_EXACT, n, np.minimum(big, T5_BUCKETS - 1)).astype(np.int32)


def _t5_lookup(tbh, dist):
    onehot = np.eye(T5_BUCKETS, dtype=np.float32)[_t5_bucket_np(dist).reshape(-1)]
    vals = jnp.einsum('ghb,nb->ghn', tbh, jnp.asarray(onehot), precision=HIGHEST)
    return vals.reshape(tbh.shape[:2] + dist.shape)


def _toeplitz_t(w, n_keys, n_q):
    length = n_keys + n_q - 1
    w_pad = jnp.concatenate([w, jnp.zeros(w.shape[:-1] + (1,), w.dtype)], axis=-1)
    reps = (1,) * (w.ndim - 1) + (n_keys,)
    flat = jnp.tile(w_pad, reps)[..., :n_keys * length]
    return flat.reshape(w.shape[:-1] + (n_keys, length))[..., n_keys - 1:n_keys - 1 + n_q]


def _att_table_t(tbh, n_keys, tq, lo, hi, minus_far):
    d = np.arange(n_keys + tq - 1) - (tq - 1)
    valid = (d >= lo) & (d < hi)
    vals = _t5_lookup(tbh, d)
    if minus_far:
        vals = vals - tbh[:, :, T5_BUCKETS - 1:]
    w = jnp.where(jnp.asarray(valid), vals * LOG2E, NEG_INF)
    t = _toeplitz_t(w, n_keys, tq)
    g, hpg = tbh.shape[:2]
    return t.transpose(0, 2, 1, 3).reshape(g, n_keys, hpg * tq)


def _nsa_bias_tables(t5_table):
    tbh = t5_table.T.reshape(NSA_GROUPS, NSA_HPG, T5_BUCKETS).astype(F32)
    j = np.arange(CMP_NEAR_ROWS)[:, None]
    i = np.arange(ATT_TQ)[None, :]
    dist = i - (CMP_LEN - 1) - CMP_STRIDE * (j - CMP_PAD)
    vals = (_t5_lookup(tbh, dist) - tbh[:, :, T5_BUCKETS - 1][:, :, None, None]) * LOG2E
    vals = jnp.where(jnp.asarray(dist >= 0)[None, None], vals, NEG_INF)
    b_cmp = vals.transpose(0, 2, 1, 3).reshape(NSA_GROUPS, CMP_NEAR_ROWS, NSA_HPG * ATT_TQ)
    far = tbh[:, :, T5_BUCKETS - 1] * LOG2E
    b_sel = _att_table_t(tbh, 2 * ATT_TQ, ATT_TQ, 0, 1 << 30, True)
    b_sel = jnp.pad(b_sel, ((0, 0), (ATT_TQ, 0), (0, 0)))
    b_win = _att_table_t(tbh, 3 * ATT_TQ, ATT_TQ, 0, WINDOW, False)
    b_win = jnp.pad(b_win, ((0, 0), (0, ATT_TQ), (0, 0)), constant_values=NEG_INF)
    hi, mid, lo = _split3_exact(jnp.repeat(far, ATT_TQ, axis=1))
    far3 = jnp.stack([hi, mid, lo, jnp.full(hi.shape, NEG_INF, BF16)], axis=1)
    far3 = jnp.pad(far3, ((0, 0), (0, BF16_ROWS - 4), (0, 0)))
    return b_cmp, b_sel, b_win, far3


def _nsa_cmp_kernel(qt_ref, kc_ref, vct_ref, near_ref, far3_ref, oct_ref, selt_ref, rhs, s_scr, imp_scr, *, nslc):
    tq = ATT_TQ
    lanes = NSA_HPG * tq
    qb = pl.program_id(1)
    cpb = tq // CMP_STRIDE
    rhs[...] = jnp.zeros(rhs.shape, BF16)
    rhs[0:HEAD_DIM, :] = qt_ref[0, 0]
    rhs[ROW_BIAS:ROW_BIAS + BF16_ROWS, :] = far3_ref[0]
    chunk = lax.broadcasted_iota(jnp.int32, (AUG_K, lanes), 0)
    rhs[AUG_K:2 * AUG_K, :] = jnp.where(chunk >= qb + CMP_NEAR_ROWS // BF16_ROWS, NEG_INF, 0.0).astype(BF16)
    r0 = pl.multiple_of(cpb * qb, BF16_ROWS)
    n_lane_blocks = tq // LANES
    ncp = s_scr.shape[0]
    imp_scr[...] = jnp.zeros(imp_scr.shape, F32)

    def attend(rows):
        s_scr[0:rows, :] = _dot(kc_ref[0, 0:rows, :], rhs[...])
        s_scr[pl.ds(r0, CMP_NEAR_ROWS), :] = s_scr[pl.ds(r0, CMP_NEAR_ROWS), :] + near_ref[0]
        s = s_scr[0:rows, :]
        m = jnp.max(s, axis=0, keepdims=True)
        e = jnp.exp2(s - m)
        l = jnp.sum(e, axis=0, keepdims=True)
        p = e * jnp.where(m > M_INIT, 1.0 / l, 0.0)
        oct_ref[0, 0] = _dot(vct_ref[0, :, 0:rows], p.astype(BF16))
        imp = p[:, 0:tq]
        for h in range(1, NSA_HPG):
            imp = imp + p[:, h * tq:(h + 1) * tq]
        for c in range(n_lane_blocks):
            imp_scr[c, 0:rows, :] = imp[:, c * LANES:(c + 1) * LANES]

    limits = sorted({min(ncp, -(-(ncp * k // 3) // LANES) * LANES) for k in (1, 2, 3)})
    lo_qb = 0
    for rows in limits:
        hi_qb = (rows - CMP_NEAR_ROWS) // cpb if rows < ncp else pl.num_programs(1) - 1

        @pl.when((qb >= lo_qb) & (qb <= hi_qb))
        def _(rows=rows):
            attend(rows)

        lo_qb = hi_qb + 1
    ratio = SLC_LEN // CMP_STRIDE

    def taps(off):
        return jnp.concatenate([imp_scr[c, pl.ds(CMP_PAD + off, nslc, stride=ratio), :]
                                for c in range(n_lane_blocks)], axis=1)

    p_slc = 0.5 * (taps(-1) + taps(ratio - 1))
    for off in range(ratio - 1):
        p_slc = p_slc + taps(off)
    blk = lax.broadcasted_iota(jnp.int32, (nslc, tq), 0)
    cur = (qb * tq + lax.broadcasted_iota(jnp.int32, (nslc, tq), 1)) // SLC_LEN
    forced = (blk == 0) | (blk == cur) | (blk == cur - 1)
    score = jnp.where(forced, FORCE_SCORE, jnp.where(blk <= cur, p_slc, -1.0))
    blk_f = blk.astype(F32)
    sel = jnp.zeros((nslc, tq), F32)
    for _ in range(min(SLC_TOPK, nslc)):
        mx = jnp.max(score, axis=0, keepdims=True)
        first = jnp.min(jnp.where(score == mx, blk_f, float(nslc)), axis=0, keepdims=True)
        hit = blk_f == first
        sel = jnp.where(hit, 1.0, sel)
        score = jnp.where(hit, -2.0, score)
    selt_ref[0] = sel.astype(BF16)


def _nsa_compress_select(qt, kc_aug, vct, b_cmp, far3):
    g, nq, hd, lanes = qt.shape
    ncp = kc_aug.shape[1]
    tq = ATT_TQ
    nslc = nq * tq // SLC_LEN
    assert ncp // BF16_ROWS <= AUG_K
    assert tq // CMP_STRIDE == BF16_ROWS
    per_q = lambda gi, qi: (gi, qi, 0, 0)
    per_g = lambda gi, qi: (gi, 0, 0)
    return pl.pallas_call(
        functools.partial(_nsa_cmp_kernel, nslc=nslc),
        grid=(g, nq),
        in_specs=[pl.BlockSpec((1, 1, hd, lanes), per_q),
                  pl.BlockSpec((1, ncp, 2 * AUG_K), per_g),
                  pl.BlockSpec((1, hd, ncp), per_g),
                  pl.BlockSpec((1, CMP_NEAR_ROWS, lanes), per_g),
                  pl.BlockSpec((1, BF16_ROWS, lanes), per_g)],
        out_specs=[pl.BlockSpec((1, 1, hd, lanes), per_q),
                   pl.BlockSpec((1, nslc, tq), lambda gi, qi: (gi, 0, qi))],
        out_shape=[jax.ShapeDtypeStruct((g, nq, hd, lanes), F32),
                   jax.ShapeDtypeStruct((g, nslc, nq * tq), BF16)],
        scratch_shapes=[pltpu.VMEM((2 * AUG_K, lanes), BF16), pltpu.VMEM((ncp, lanes), F32),
                        pltpu.VMEM((tq // LANES, ncp, LANES), F32)],
        compiler_params=_cparams(("parallel", "parallel")),
        name="nsa_compress_select",
    )(qt, kc_aug, vct, b_cmp, far3)


def _flash_init_t(m_ref, acc_ref):
    m_ref[...] = jnp.full(m_ref.shape, M_INIT, F32)
    acc_ref[...] = jnp.zeros(acc_ref.shape, F32)


def _flash_step_t(s, vt_tile, m_ref, acc_ref):
    m_old = m_ref[...]
    m_new = jnp.maximum(m_old, jnp.max(s, axis=0, keepdims=True))
    p = jnp.exp2(s - m_new).astype(BF16)
    acc_ref[...] = jnp.exp2(m_old - m_new) * acc_ref[...] + _dot(vt_tile, p)
    m_ref[...] = m_new


def _flash_result_t(acc_ref):
    acc = acc_ref[...]
    return acc[0:HEAD_DIM, :] / acc[HEAD_DIM:HEAD_DIM + 1, :]


def _nsa_att_kernel(qt_ref, ks_ref, kx_ref, vst_ref, kw_ref, vwt_ref, selt_ref, far3_ref, bsel_ref, bwin_ref,
                    oct_ref, gate_ref, o_ref, rhs_s, rhs_w, mask_t, ms, accs, mw, accw, s_even, s_odd):
    tq = ATT_TQ
    qb = pl.program_id(1)
    ones_rows = _ones_rows(tq)
    kx_tiles = kx_ref.shape[0] // tq
    qt = qt_ref[0, 0]
    rhs_s[...] = jnp.zeros(rhs_s.shape, BF16)
    rhs_s[0:HEAD_DIM, :] = qt
    rhs_s[ROW_BIAS:ROW_BIAS + BF16_ROWS, :] = far3_ref[0]
    rhs_w[...] = jnp.zeros(rhs_w.shape, BF16)
    rhs_w[0:HEAD_DIM, :] = qt
    madd = ((selt_ref[0].astype(F32) - 1.0) * (-NEG_INF)).astype(BF16)
    mask_t[...] = jnp.concatenate([madd] * NSA_HPG, axis=1)
    _flash_init_t(ms, accs)
    _flash_init_t(mw, accw)
    blocks_per_tile = tq // SLC_LEN

    def sel_scores(kt, s_ref):
        kt = jnp.minimum(kt, qb)
        k0 = pl.multiple_of(kt * tq, tq)
        chunk = pl.multiple_of((kt * blocks_per_tile) // BF16_ROWS * BF16_ROWS, BF16_ROWS)
        rhs_s[ROW_MASK:ROW_MASK + BF16_ROWS, :] = mask_t[pl.ds(chunk, BF16_ROWS), :]
        rel = jnp.clip(kt - qb + 2, 0, 2)
        table = bsel_ref[0, pl.ds(pl.multiple_of(rel * tq, tq), tq), :]
        k_aug = ks_ref[pl.ds(k0, tq), :] + kx_ref[pl.ds(pl.multiple_of((kt % kx_tiles) * tq, tq), tq), :]
        s_ref[...] = _dot(k_aug, rhs_s[...]) + table

    def sel_consume(kt, s_ref):
        k0 = pl.multiple_of(kt * tq, tq)
        vt_aug = jnp.concatenate([vst_ref[0, :, pl.ds(k0, tq)], ones_rows], axis=0)
        _flash_step_t(s_ref[...], vt_aug, ms, accs)

    def win_scores(j, s_ref):
        kt = qb - 2 + j
        k0 = pl.multiple_of(jnp.maximum(kt, 0) * tq, tq)
        row = pl.multiple_of(jnp.where(kt >= 0, j, 3) * tq, tq)
        s_ref[...] = _dot(kw_ref[pl.ds(k0, tq), :], rhs_w[...]) + bwin_ref[0, pl.ds(row, tq), :]

    def win_consume(j, s_ref):
        k0 = pl.multiple_of(jnp.maximum(qb - 2 + j, 0) * tq, tq)
        vt_aug = jnp.concatenate([vwt_ref[0, :, pl.ds(k0, tq)], ones_rows], axis=0)
        _flash_step_t(s_ref[...], vt_aug, mw, accw)

    n_sel = qb + 1
    sel_scores(0, s_even)

    def pair_body(j, carry):
        sel_scores(2 * j + 1, s_odd)
        sel_consume(2 * j, s_even)
        sel_scores(2 * j + 2, s_even)
        sel_consume(2 * j + 1, s_odd)
        return carry

    lax.fori_loop(0, n_sel // 2, pair_body, 0)
    win_scores(0, s_odd)

    @pl.when(n_sel % 2 == 1)
    def _():
        sel_consume(qb, s_even)

    win_scores(1, s_even)
    win_consume(0, s_odd)
    win_scores(2, s_odd)
    win_consume(1, s_even)
    win_consume(2, s_odd)

    gt = _sigmoid(gate_ref[0, 0])
    out = gt[0:1, :] * oct_ref[0, 0] + gt[1:2, :] * _flash_result_t(accs) + gt[2:3, :] * _flash_result_t(accw)
    for h in range(NSA_HPG):
        o_ref[0, h] = out[:, h * tq:(h + 1) * tq].astype(o_ref.dtype)


def _nsa_attention(qt, k_tok, ks_block0, kw_block0, k_extra, vst, vwt, selt, far3, b_sel, b_win, oct_, gates_t):
    g, nq, hd, lanes = qt.shape
    s = k_tok.shape[0]
    nslc = selt.shape[1]
    tq = ATT_TQ
    resident = pl.Buffered(1)
    per_q = lambda gi, qi: (gi, qi, 0, 0)
    per_g = lambda gi, qi: (gi, 0, 0)
    return pl.pallas_call(
        _nsa_att_kernel,
        grid=(g, nq),
        in_specs=[pl.BlockSpec((1, 1, hd, lanes), per_q),
                  pl.BlockSpec((s, AUG_K), lambda gi, qi: (0, ks_block0 + gi), pipeline_mode=resident),
                  pl.BlockSpec(k_extra.shape, lambda gi, qi: (0, 0), pipeline_mode=resident),
                  pl.BlockSpec((1, hd, s), per_g, pipeline_mode=resident),
                  pl.BlockSpec((s, AUG_K), lambda gi, qi: (0, kw_block0 + gi), pipeline_mode=resident),
                  pl.BlockSpec((1, hd, s), per_g, pipeline_mode=resident),
                  pl.BlockSpec((1, nslc, tq), lambda gi, qi: (gi, 0, qi)),
                  pl.BlockSpec((1, BF16_ROWS, lanes), per_g),
                  pl.BlockSpec((1, 3 * tq, lanes), per_g, pipeline_mode=resident),
                  pl.BlockSpec((1, 4 * tq, lanes), per_g, pipeline_mode=resident),
                  pl.BlockSpec((1, 1, hd, lanes), per_q),
                  pl.BlockSpec((1, 1, 8, lanes), per_q)],
        out_specs=pl.BlockSpec((1, lanes // tq, hd, tq), lambda gi, qi: (gi, 0, 0, qi)),
        out_shape=jax.ShapeDtypeStruct((g, lanes // tq, hd, s), BF16),
        scratch_shapes=[pltpu.VMEM((AUG_K, lanes), BF16), pltpu.VMEM((AUG_K, lanes), BF16),
                        pltpu.VMEM((nslc, lanes), BF16),
                        pltpu.VMEM((1, lanes), F32), pltpu.VMEM((AUG_V, lanes), F32),
                        pltpu.VMEM((1, lanes), F32), pltpu.VMEM((AUG_V, lanes), F32),
                        pltpu.VMEM((tq, lanes), F32), pltpu.VMEM((tq, lanes), F32)],
        compiler_params=_cparams(("arbitrary", "arbitrary")),
        name="nsa_select_window",
    )(qt, k_tok, k_extra, vst, k_tok, vwt, selt, far3, b_sel, b_win, oct_, gates_t)


def _decay_kernel(z_ref, b_ref, place_ref, o_ref, carry_ref, *, tb):
    @pl.when(pl.program_id(0) == 0)
    def _():
        carry_ref[...] = jnp.zeros(carry_ref.shape, F32)

    z = z_ref[...] + b_ref[...]
    log_f = jnp.minimum(z, 0.0) - jnp.log1p(jnp.exp(-jnp.abs(z)))
    r = lax.broadcasted_iota(jnp.int32, (tb, tb), 0)
    c = lax.broadcasted_iota(jnp.int32, (tb, tb), 1)
    tri = jnp.where(r >= c, 1.0, 0.0).astype(BF16)
    run = _dot3_rhs(tri, log_f) + carry_ref[...]
    carry_ref[...] = run[tb - 1:tb, :]
    hi, mid, lo = _split3_exact(-run * LOG2E)
    o_ref[...] = (_dot(hi, place_ref[0]) + _dot(mid, place_ref[1]) + _dot(lo, place_ref[2])).astype(BF16)


def _decay_pieces(z, bias, first_lane, n_heads, tb=512):
    s, n = z.shape
    place = np.zeros((3, n, n_heads * AUG_K), np.float32)
    for h in range(n_heads):
        for piece in range(3):
            place[piece, first_lane + h, h * AUG_K + HEAD_DIM + piece] = 1.0
    return pl.pallas_call(
        functools.partial(_decay_kernel, tb=tb),
        grid=(s // tb,),
        in_specs=[pl.BlockSpec((tb, n), lambda i: (i, 0)),
                  pl.BlockSpec((1, n), lambda i: (0, 0)),
                  pl.BlockSpec((3, n, n_heads * AUG_K), lambda i: (0, 0, 0))],
        out_specs=pl.BlockSpec((tb, n_heads * AUG_K), lambda i: (i, 0)),
        out_shape=jax.ShapeDtypeStruct((s, n_heads * AUG_K), BF16),
        scratch_shapes=[pltpu.VMEM((1, n), F32)],
        compiler_params=_cparams(("arbitrary",)),
        name="decay_cumsum",
    )(z, bias, jnp.asarray(place, BF16))


def _ones_rows(width):
    return jnp.where(lax.broadcasted_iota(jnp.int32, (BF16_ROWS, width), 0) < 8, 1.0, 0.0).astype(BF16)


def _fox_kernel(qt_ref, k_ref, dk_ref, vt_ref, o_ref, rhs, m_ref, acc_ref, s_even, s_odd):
    tq = FOX_TQ
    tk = FOX_TK
    assert tq == 2 * tk
    qb = pl.program_id(1)
    row = lax.broadcasted_iota(jnp.int32, (AUG_K - HEAD_DIM, tq), 0)
    rhs[0:HEAD_DIM, :] = qt_ref[0]
    rhs[HEAD_DIM:AUG_K, :] = jnp.where(row < 3, 1.0, 0.0).astype(BF16)
    ones_rows = _ones_rows(tk)
    _flash_init_t(m_ref, acc_ref)

    def scores(kt, s_ref):
        k0 = pl.multiple_of(kt * tk, tk)
        k_aug = k_ref[pl.ds(k0, tk), :] + dk_ref[pl.ds(k0, tk), :]
        s_ref[...] = _dot(k_aug, rhs[...])

    def consume(kt, s_ref, diagonal):
        k0 = pl.multiple_of(kt * tk, tk)
        s = s_ref[...]
        if diagonal:
            key = k0 + lax.broadcasted_iota(jnp.int32, (tk, tq), 0)
            qry = qb * tq + lax.broadcasted_iota(jnp.int32, (tk, tq), 1)
            s = jnp.where(key <= qry, s, NEG_INF)
        vt_aug = jnp.concatenate([vt_ref[0, :, pl.ds(k0, tk)], ones_rows], axis=0)
        _flash_step_t(s, vt_aug, m_ref, acc_ref)

    scores(0, s_even)

    def pair_body(j, carry):
        scores(2 * j + 1, s_odd)
        consume(2 * j, s_even, False)
        scores(2 * j + 2, s_even)
        consume(2 * j + 1, s_odd, False)
        return carry

    lax.fori_loop(0, qb, pair_body, 0)
    scores(2 * qb + 1, s_odd)
    consume(2 * qb, s_even, True)
    consume(2 * qb + 1, s_odd, True)
    o_ref[0] = _flash_result_t(acc_ref).astype(o_ref.dtype)


def _fox_attention(qt, k_tok, k_block0, decay_k, vt):
    h, hd, s = qt.shape
    tq = FOX_TQ
    resident = pl.Buffered(1)
    return pl.pallas_call(
        _fox_kernel,
        grid=(h, s // tq),
        in_specs=[pl.BlockSpec((1, hd, tq), lambda hi, qi: (hi, 0, qi)),
                  pl.BlockSpec((s, AUG_K), lambda hi, qi: (0, k_block0 + hi), pipeline_mode=resident),
                  pl.BlockSpec((s, AUG_K), lambda hi, qi: (0, hi), pipeline_mode=resident),
                  pl.BlockSpec((1, hd, s), lambda hi, qi: (hi, 0, 0), pipeline_mode=resident)],
        out_specs=pl.BlockSpec((1, hd, tq), lambda hi, qi: (hi, 0, qi)),
        out_shape=jax.ShapeDtypeStruct((h, hd, s), BF16),
        scratch_shapes=[pltpu.VMEM((AUG_K, tq), BF16),
                        pltpu.VMEM((1, tq), F32), pltpu.VMEM((AUG_V, tq), F32),
                        pltpu.VMEM((FOX_TK, tq), F32), pltpu.VMEM((FOX_TK, tq), F32)],
        compiler_params=_cparams(("arbitrary", "arbitrary")),
        name="fox_attention",
    )(qt, k_tok, decay_k, vt)


def _merge_kernel(on_ref, of_ref, mg_ref, x_ref, wn_ref, wf_ref, wo_ref, g1_ref, lg_ref, lb_ref, o_ref, *, alpha):
    d = x_ref.shape[-1]
    tn = (((0,), (0,)), ((), ()))
    a = lax.dot_general(on_ref[...], wn_ref[...], tn, preferred_element_type=F32)
    b = lax.dot_general(of_ref[...], wf_ref[...], tn, preferred_element_type=F32)
    gm = _sigmoid(mg_ref[...].astype(F32))
    merged = gm[:, 0:d] * a + gm[:, d:2 * d] * b
    y = _dot(merged.astype(BF16), wo_ref[...])
    z = alpha * x_ref[...] + (1.0 + g1_ref[...]) * y
    o_ref[...] = _layer_norm(z) * lg_ref[...] + lb_ref[...]


def _merge_project(o_nsa, o_fox, merge, x, wn, wf, wo, g1, ln_g, ln_b, alpha, tm=256):
    m, d = x.shape
    w = o_nsa.shape[0]
    resident = pl.Buffered(1)
    row = lambda i: (i, 0)
    fixed = lambda i: (0, 0)
    return pl.pallas_call(
        functools.partial(_merge_kernel, alpha=alpha),
        grid=(m // tm,),
        in_specs=[pl.BlockSpec((w, tm), lambda i: (0, i)), pl.BlockSpec((w, tm), lambda i: (0, i)),
                  pl.BlockSpec((tm, 2 * d), row), pl.BlockSpec((tm, d), row),
                  pl.BlockSpec((w, d), fixed, pipeline_mode=resident),
                  pl.BlockSpec((w, d), fixed, pipeline_mode=resident),
                  pl.BlockSpec((d, d), fixed, pipeline_mode=resident),
                  pl.BlockSpec((1, d), fixed), pl.BlockSpec((1, d), fixed), pl.BlockSpec((1, d), fixed)],
        out_specs=pl.BlockSpec((tm, d), row),
        out_shape=jax.ShapeDtypeStruct((m, d), F32),
        compiler_params=_cparams(("parallel",)),
        name="merge_project_ln",
    )(o_nsa, o_fox, merge, x, wn, wf, wo, g1, ln_g, ln_b)


def _router_kernel(x_ref, sc_ref, sh_ref, w_ref, b_ref, u_ref, r_ref):
    u = _layer_norm(x_ref[...]) * (1.0 + sc_ref[...]) + sh_ref[...]
    u_ref[...] = u
    u_hi = u.astype(BF16)
    u_lo = (u - u_hi.astype(F32)).astype(BF16)
    logits = _dot(u_hi, w_ref[0]) + _dot(u_lo, w_ref[0]) + _dot(u_hi, w_ref[1]) + b_ref[...]
    lane = lax.broadcasted_iota(jnp.int32, (1, LANES), 1).astype(F32)
    none = float(LANES)
    is_g = lane < N_GROUPS
    lg = jnp.where(is_g, logits, NEG_INF)
    eg = jnp.exp(lg - jnp.max(lg, axis=-1, keepdims=True))
    pg = eg / jnp.sum(eg, axis=-1, keepdims=True)
    p_grp = jnp.max(pg, axis=-1, keepdims=True)
    grp = jnp.min(jnp.where(pg == p_grp, lane, none), axis=-1, keepdims=True)
    lo = N_GROUPS + grp * EXPERTS_PER_GROUP
    is_e = (lane >= lo) & (lane < lo + EXPERTS_PER_GROUP)
    le = jnp.where(is_e, logits, NEG_INF)
    ee = jnp.exp(le - jnp.max(le, axis=-1, keepdims=True))
    pe = jnp.where(is_e, ee / jnp.sum(ee, axis=-1, keepdims=True), -1.0)
    p1 = jnp.max(pe, axis=-1, keepdims=True)
    i1 = jnp.min(jnp.where(pe == p1, lane, none), axis=-1, keepdims=True)
    pe2 = jnp.where(lane == i1, -1.0, pe)
    p2 = jnp.max(pe2, axis=-1, keepdims=True)
    i2 = jnp.min(jnp.where(pe2 == p2, lane, none), axis=-1, keepdims=True)
    den = p1 + p2
    r_ref[...] = jnp.where(lane == 0, i1 - N_GROUPS,
                           jnp.where(lane == 1, i2 - N_GROUPS,
                                     jnp.where(lane == 2, p_grp * p1 / den,
                                               jnp.where(lane == 3, p_grp * p2 / den, 0.0))))


def _router(x1, sc, sh, w_r, b_r, tm=256):
    m, d = x1.shape
    row = lambda i: (i, 0)
    fixed = lambda i: (0, 0)
    return pl.pallas_call(
        _router_kernel,
        grid=(m // tm,),
        in_specs=[pl.BlockSpec((tm, d), row), pl.BlockSpec((1, d), fixed), pl.BlockSpec((1, d), fixed),
                  pl.BlockSpec((2, d, LANES), lambda i: (0, 0, 0)), pl.BlockSpec((1, LANES), fixed)],
        out_specs=[pl.BlockSpec((tm, d), row), pl.BlockSpec((tm, LANES), row)],
        out_shape=[jax.ShapeDtypeStruct((m, d), F32), jax.ShapeDtypeStruct((m, LANES), F32)],
        compiler_params=_cparams(("parallel",)),
        name="moe_router",
    )(x1, sc, sh, w_r, b_r)


def _moe_kernel(be_ref, nu_ref, tok_ref, tok_next_ref, dst_ref, rw_ref, u_hbm, wg_ref, wu_ref, wd_ref, out_hbm,
                xbuf, ybuf, wgb, wub, wdb, sem_in, sem_out, *, n_dump0):
    rb = ROW_BLOCK
    i = pl.program_id(0)
    last = nu_ref[0] - 1
    slot = i % 2

    def row_in(r, tok, sl):
        return pltpu.make_async_copy(u_hbm.at[pl.ds(tok, 1), :], xbuf.at[sl, pl.ds(r, 1), :], sem_in.at[sl])

    def row_out(r, dst):
        return pltpu.make_async_copy(ybuf.at[pl.ds(r, 1), :], out_hbm.at[pl.ds(dst, 1), :], sem_out)

    @pl.when(i == 0)
    def _():
        for r in range(rb):
            row_in(r, tok_ref[0, 0, r], 0).start()
        ybuf[...] = jnp.zeros(ybuf.shape, F32)
        pltpu.make_async_copy(ybuf, out_hbm.at[pl.ds(n_dump0, rb), :], sem_out).start()

    prev = be_ref[jnp.maximum(i - 1, 0)]

    @pl.when((i <= last) & ((i == 0) | (be_ref[i] != prev)))
    def _():
        wgb[...] = wg_ref[0].astype(BF16)
        wub[...] = wu_ref[0].astype(BF16)
        wdb[...] = wd_ref[0].astype(BF16)

    @pl.when(i <= last)
    def _():
        for r in range(rb):
            row_in(r, 0, slot).wait()
        xb = xbuf[slot].astype(BF16)
        for r in range(rb):
            row_in(r, tok_next_ref[0, 0, r], 1 - slot).start()
        gate = _dot(xb, wgb[...])
        up = _dot(xb, wub[...])
        hid = (gate * _sigmoid(gate)) * up
        y = _dot(hid.astype(BF16), wdb[...]) * rw_ref[0]
        for r in range(rb):
            row_out(r, 0).wait()
        ybuf[...] = y
        for r in range(rb):
            row_out(r, dst_ref[0, 0, r]).start(priority=r % 2)

    @pl.when(i == last)
    def _():
        for r in range(rb):
            row_out(r, 0).wait()
        for r in range(rb):
            row_in(r, 0, 1 - slot).wait()


def _moe_experts(u, blk_exp, n_used, row_tok, row_dst, row_w, w_gate, w_up, w_down):
    t, d = u.shape
    n_blocks = blk_exp.shape[0]
    de = w_gate.shape[-1]
    rb = ROW_BLOCK
    tok3 = row_tok.reshape(n_blocks, 1, rb)
    grid_spec = pltpu.PrefetchScalarGridSpec(
        num_scalar_prefetch=2,
        grid=(n_blocks,),
        in_specs=[pl.BlockSpec((1, 1, rb), lambda i, be, nu: (i, 0, 0), memory_space=pltpu.SMEM),
                  pl.BlockSpec((1, 1, rb), lambda i, be, nu: (jnp.minimum(i + 1, n_blocks - 1), 0, 0),
                               memory_space=pltpu.SMEM),
                  pl.BlockSpec((1, 1, rb), lambda i, be, nu: (i, 0, 0), memory_space=pltpu.SMEM),
                  pl.BlockSpec((1, rb, 1), lambda i, be, nu: (i, 0, 0)),
                  pl.BlockSpec(memory_space=pl.ANY),
                  pl.BlockSpec((1, d, de), lambda i, be, nu: (be[i], 0, 0)),
                  pl.BlockSpec((1, d, de), lambda i, be, nu: (be[i], 0, 0)),
                  pl.BlockSpec((1, de, d), lambda i, be, nu: (be[i], 0, 0))],
        out_specs=pl.BlockSpec(memory_space=pl.ANY),
        scratch_shapes=[pltpu.VMEM((2, rb, d), F32), pltpu.VMEM((rb, d), F32),
                        pltpu.VMEM((d, de), BF16), pltpu.VMEM((d, de), BF16), pltpu.VMEM((de, d), BF16),
                        pltpu.SemaphoreType.DMA((2,)), pltpu.SemaphoreType.DMA(())],
    )
    return pl.pallas_call(
        functools.partial(_moe_kernel, n_dump0=2 * t),
        grid_spec=grid_spec,
        out_shape=jax.ShapeDtypeStruct((2 * t + rb, d), F32),
        compiler_params=_cparams(("arbitrary",)),
        name="moe_experts",
    )(blk_exp, n_used, tok3, tok3, row_dst.reshape(n_blocks, 1, rb), row_w.reshape(n_blocks, rb, 1),
      u, w_gate, w_up, w_down)


def _moe_dispatch(route, t):
    k = 2
    eid = route[:, 0:k].astype(jnp.int32).reshape(-1)
    wts = route[:, k:2 * k].reshape(-1)
    n_asg = t * k
    n_rows = n_asg + N_EXPERTS * ROW_BLOCK
    n_blocks = n_rows // ROW_BLOCK
    onehot = (eid[:, None] == jnp.arange(N_EXPERTS, dtype=jnp.int32)[None, :]).astype(jnp.int32)
    rank = jnp.sum((jnp.cumsum(onehot, axis=0) - onehot) * onehot, axis=1)
    counts = jnp.sum(onehot, axis=0)
    padded = (counts + ROW_BLOCK - 1) // ROW_BLOCK * ROW_BLOCK
    pad_end = jnp.cumsum(padded)
    pad_start = pad_end - padded
    dest = jnp.sum(onehot * pad_start[None, :], axis=1) + rank
    asg = jnp.arange(n_asg, dtype=jnp.int32)
    upd = jnp.stack([(asg % k) * t + asg // k, lax.bitcast_convert_type(wts, jnp.int32)], axis=1)
    init = jnp.stack([n_asg + jnp.arange(n_rows, dtype=jnp.int32) % ROW_BLOCK,
                      jnp.zeros((n_rows,), jnp.int32)], axis=1)
    rows = init.at[dest].set(upd)
    row_dst = rows[:, 0]
    row_w = lax.bitcast_convert_type(rows[:, 1], F32)
    row_tok = jnp.where(row_dst < n_asg, row_dst % t, 0)
    blk_start = jnp.arange(n_blocks, dtype=jnp.int32) * ROW_BLOCK
    blk_exp = jnp.minimum(jnp.sum((pad_end[None, :] <= blk_start[:, None]).astype(jnp.int32), axis=1),
                          N_EXPERTS - 1)
    n_used = (pad_end[N_EXPERTS - 1:] // ROW_BLOCK).astype(jnp.int32)
    return blk_exp, n_used, row_tok, row_dst, row_w


def _final_kernel(x_ref, y0_ref, y1_ref, g2_ref, lg_ref, lb_ref, o_ref, *, alpha):
    z = alpha * x_ref[...] + (1.0 + g2_ref[...]) * (y0_ref[...] + y1_ref[...])
    o_ref[...] = _layer_norm(z) * lg_ref[...] + lb_ref[...]


def _final_ln(x1, y2, g2, ln_g, ln_b, alpha, tm=512):
    m, d = x1.shape
    nb = m // tm
    fixed = lambda i: (0, 0)
    return pl.pallas_call(
        functools.partial(_final_kernel, alpha=alpha),
        grid=(nb,),
        in_specs=[pl.BlockSpec((tm, d), lambda i: (i, 0)),
                  pl.BlockSpec((tm, d), lambda i: (i, 0)),
                  pl.BlockSpec((tm, d), lambda i: (i + nb, 0)),
                  pl.BlockSpec((1, d), fixed), pl.BlockSpec((1, d), fixed), pl.BlockSpec((1, d), fixed)],
        out_specs=pl.BlockSpec((tm, d), lambda i: (i, 0)),
        out_shape=jax.ShapeDtypeStruct((m, d), F32),
        compiler_params=_cparams(("parallel",)),
        name="final_ln",
    )(x1, y2, y2, g2, ln_g, ln_b)


def _to_lane_blocks(a, tq):
    g, hpg, s, c = a.shape
    return a.reshape(g, hpg, s // tq, tq, c).transpose(0, 2, 4, 1, 3).reshape(g, s // tq, c, hpg * tq)


def _layer(x2d, c, w_ada, b_ada, w_in, b_fgt, t5_table, cmp_pe, cmp_w1, cmp_b1, cmp_w2,
           w_br_nsa, w_br_fox, w_o, ln1_g, ln1_b, w_rg, b_rg, w_re, b_re,
           w_gate, w_up, w_down, ln2_g, ln2_b, alpha):
    s, d = x2d.shape
    hd = HEAD_DIM
    g = NSA_GROUPS
    mod = _ada_mod(c, w_ada, b_ada)
    sh1, sc1, g1, sh2, sc2, g2 = [mod[:, i * d:(i + 1) * d] for i in range(6)]

    c_q = NSA_HEADS * hd
    c_kv = 6 * g * hd
    c_gate = 3 * NSA_HEADS
    c_fox = 3 * FOX_HEADS * hd
    off_kv = c_q
    off_gate = off_kv + c_kv
    off_fox = off_gate + c_gate
    off_fgt = off_fox + c_fox
    off_merge = off_fgt + FOX_HEADS
    qscale = hd ** -0.5 * LOG2E
    nh = FOX_HEADS
    gw = g * hd

    def kv_cols(z):
        return w_in[:, off_kv + z * gw:off_kv + (z + 1) * gw]

    def lane_padded(w, heads):
        return jnp.pad(w.reshape(d, heads, hd), ((0, 0), (0, 0), (0, AUG_K - hd))).reshape(d, heads * AUG_K)

    fox_q, fox_k, fox_v = [w_in[:, off_fox + i * nh * hd:off_fox + (i + 1) * nh * hd] for i in range(3)]
    w_ch = jnp.concatenate([w_in[:, 0:off_kv] * qscale, kv_cols(3), kv_cols(5), fox_q * qscale, fox_v],
                           axis=1).T.astype(BF16)
    w_tok = jnp.concatenate([kv_cols(0), kv_cols(1), lane_padded(kv_cols(2), g), lane_padded(kv_cols(4), g),
                             lane_padded(fox_k, nh)], axis=1).astype(BF16)
    n_small = c_gate + FOX_HEADS
    w_small = jnp.concatenate([w_in[:, off_gate:off_fox], w_in[:, off_fgt:off_merge],
                               jnp.zeros((d, LANES - n_small), F32)], axis=1).astype(BF16)
    w_merge = w_in[:, off_merge:].astype(BF16)

    u = _ln_mod(x2d, sc1, sh1, BF16)
    ch = _matmul_nt(w_ch, u, BF16, 512, "in_proj_channel_major")
    tok = _matmul(u, w_tok, BF16, 512, w_tok.shape[1] // 2, "in_proj_token_major")
    small = _matmul(u, w_small, F32, 512, LANES, "in_proj_small")
    merge = _matmul(u, w_merge, BF16, 512, 1024, "in_proj_merge")

    qt_nsa = ch[0:c_q].reshape(g, NSA_HPG, hd, s)
    vst = ch[c_q:c_q + gw].reshape(g, hd, s)
    vwt = ch[c_q + gw:c_q + 2 * gw].reshape(g, hd, s)
    fox_qt = ch[c_q + 2 * gw:c_q + 2 * gw + nh * hd].reshape(nh, hd, s)
    fox_vt = ch[c_q + 2 * gw + nh * hd:].reshape(nh, hd, s)
    tok_ks_block0 = 2 * gw // AUG_K
    tok_kw_block0 = tok_ks_block0 + g
    tok_fox_block0 = tok_kw_block0 + g

    kv_cmp_in = tok[:, 0:2 * gw].reshape(s, 2, g, hd).transpose(1, 2, 0, 3)
    kv_cmp = _compress(kv_cmp_in, cmp_pe, cmp_w1, cmp_b1, cmp_w2)
    nch = s // CMP_STRIDE
    ncp = nch + LANES
    nslc = s // SLC_LEN
    kv_cmp_pad = jnp.pad(kv_cmp.astype(BF16), ((0, 0), (0, 0), (CMP_PAD, ncp - nch - CMP_PAD), (0, 0)))
    b_cmp, b_sel, b_win, far3 = _nsa_bias_tables(t5_table)

    row_ix = np.arange(ncp)
    row_ok = (row_ix >= CMP_PAD) & (row_ix < nch - 1 + CMP_PAD)
    row_cols = np.zeros((ncp, 2 * AUG_K - hd), np.float32)
    row_cols[:, ROW_BIAS - hd:ROW_BIAS - hd + 3] = row_ok[:, None]
    row_cols[:, ROW_BIAS - hd + 3] = ~row_ok
    row_cols[row_ix, AUG_K - hd + row_ix // BF16_ROWS] = 1.0
    kc_aug = jnp.concatenate([kv_cmp_pad[0], jnp.broadcast_to(jnp.asarray(row_cols, BF16), (g,) + row_cols.shape)],
                             axis=-1)
    vct = kv_cmp_pad[1].transpose(0, 2, 1)
    nq = s // ATT_TQ
    qt = qt_nsa.reshape(g, NSA_HPG, hd, nq, ATT_TQ).transpose(0, 3, 2, 1, 4).reshape(g, nq, hd, NSA_HPG * ATT_TQ)
    oct_, selt = _nsa_compress_select(qt, kc_aug, vct, b_cmp, far3)

    period = BF16_ROWS * SLC_LEN
    k_extra = np.zeros((period, AUG_K), np.float32)
    k_extra[np.arange(period), ROW_MASK + np.arange(period) // SLC_LEN] = 1.0
    k_extra[:, ROW_BIAS:ROW_BIAS + 3] = 1.0
    gates = small[:, 0:c_gate].reshape(s, g, NSA_HPG, 3).transpose(1, 2, 0, 3)
    gates_t = jnp.pad(_to_lane_blocks(gates, ATT_TQ), ((0, 0), (0, 0), (0, 5), (0, 0)))
    o_nsa_t = _nsa_attention(qt, tok, tok_ks_block0, tok_kw_block0, jnp.asarray(k_extra, BF16), vst, vwt, selt, far3,
                             b_sel, b_win, oct_, gates_t)
    o_nsa = o_nsa_t.reshape(MIX_W, s)

    fgt_bias = jnp.concatenate([jnp.zeros((c_gate,), F32), b_fgt, jnp.zeros((LANES - n_small,), F32)])[None, :]
    decay_k = _decay_pieces(small, fgt_bias, c_gate, nh)
    o_fox = _fox_attention(fox_qt, tok, tok_fox_block0, decay_k, fox_vt)
    o_fox = o_fox.reshape(MIX_W, s)

    x1 = _merge_project(o_nsa, o_fox, merge, x2d, w_br_nsa.astype(BF16), w_br_fox.astype(BF16),
                        w_o.astype(BF16), g1, ln1_g[None, :], ln1_b[None, :], alpha)

    n_r = N_GROUPS + N_EXPERTS
    w_r = jnp.concatenate([w_rg, w_re.reshape(d, N_EXPERTS), jnp.zeros((d, LANES - n_r), F32)], axis=1)
    b_r = jnp.concatenate([b_rg, b_re.reshape(N_EXPERTS), jnp.zeros((LANES - n_r,), F32)])[None, :]
    w_r_hi, w_r_lo, _ = _split3_exact(w_r)
    u2, route = _router(x1, sc2, sh2, jnp.stack([w_r_hi, w_r_lo]), b_r)
    blk_exp, n_used, row_tok, row_dst, row_w = _moe_dispatch(route, s)
    y2 = _moe_experts(u2, blk_exp, n_used, row_tok, row_dst, row_w, w_gate, w_up, w_down)
    return _final_ln(x1, y2, g2, ln2_g[None, :], ln2_b[None, :], alpha)


def kernel(x, c, w_ada, b_ada, w_in, b_fgt, t5_table, cmp_pe, cmp_w1, cmp_b1, cmp_w2, w_br_nsa, w_br_fox, w_o,
           ln1_g, ln1_b, w_rg, b_rg, w_re, b_re, w_gate, w_up, w_down, ln2_g, ln2_b):
    b, s, d = x.shape
    depth = w_ada.shape[0]
    assert b == 1
    alpha = (2 * depth) ** 0.25
    h = x[0]
    for l in range(depth):
        h = _layer(h, c, w_ada[l], b_ada[l], w_in[l], b_fgt[l], t5_table, cmp_pe[l], cmp_w1[l], cmp_b1[l],
                   cmp_w2[l], w_br_nsa[l], w_br_fox[l], w_o[l], ln1_g[l], ln1_b[l], w_rg[l], b_rg[l],
                   w_re[l], b_re[l], w_gate[l], w_up[l], w_down[l], ln2_g[l], ln2_b[l], alpha)
    return h[None]
```

```python
import functools
import math

import numpy as np
import jax
import jax.numpy as jnp
from jax import lax
from jax.experimental import pallas as pl
from jax.experimental.pallas import tpu as pltpu

F32 = jnp.float32
BF16 = jnp.bfloat16
HIGHEST = lax.Precision.HIGHEST
LOG2E = math.log2(math.e)

HEAD_DIM = 64
NSA_HEADS = 8
NSA_GROUPS = 2
NSA_HPG = NSA_HEADS // NSA_GROUPS
FOX_HEADS = 8
MIX_W = NSA_HEADS * HEAD_DIM
CMP_LEN = 32
CMP_STRIDE = 16
SLC_LEN = 64
SLC_TOPK = 16
WINDOW = 512
T5_BUCKETS = 32
T5_MAX_EXACT = 16
T5_MAX_DIST = 128
N_GROUPS = 8
EXPERTS_PER_GROUP = 8
N_EXPERTS = N_GROUPS * EXPERTS_PER_GROUP
ROW_BLOCK = 128
LN_EPS = 1e-5
NEG_INF = -1e30
M_INIT = -1e29
FORCE_SCORE = 1e4

LANES = 128
BF16_ROWS = 16
CMP_PAD = 8
ATT_TQ = 256
CMP_NEAR_ROWS = 32
FOX_TQ = 1024
FOX_TK = 512
FOX_SKIP_MARGIN = 160.0
FOX_BOUND_SLACK = 1.02
AUG_K = 128
AUG_V = HEAD_DIM + 16
ROW_MASK = HEAD_DIM
ROW_BIAS = HEAD_DIM + 16
VMEM_LIMIT = 56 * 1024 * 1024


def _cparams(sem, vmem=VMEM_LIMIT):
    return pltpu.CompilerParams(dimension_semantics=sem, vmem_limit_bytes=vmem)


def _sigmoid(x):
    return 1.0 / (1.0 + jnp.exp(-x))


def _layer_norm(x):
    mu = jnp.mean(x, axis=-1, keepdims=True)
    xc = x - mu
    var = jnp.mean(xc * xc, axis=-1, keepdims=True)
    return xc * lax.rsqrt(var + LN_EPS)


def _split3(x):
    hi = x.astype(BF16)
    r1 = x - hi.astype(F32)
    mid = r1.astype(BF16)
    lo = (r1 - mid.astype(F32)).astype(BF16)
    return hi, mid, lo


def _split3_exact(x):
    def trunc(v):
        bits = lax.bitcast_convert_type(v, jnp.uint32) & jnp.uint32(0xFFFF0000)
        return lax.bitcast_convert_type(bits, F32)
    hi = trunc(x)
    r1 = x - hi
    mid = trunc(r1)
    lo = r1 - mid
    return hi.astype(BF16), mid.astype(BF16), lo.astype(BF16)


def _dot(a, b):
    return jnp.dot(a, b, preferred_element_type=F32)


def _dot_nt(a, b):
    return lax.dot_general(a, b, (((1,), (1,)), ((), ())), preferred_element_type=F32)


def _dot3(x, w_bf16):
    hi, mid, lo = _split3(x)
    return _dot(hi, w_bf16) + _dot(mid, w_bf16) + _dot(lo, w_bf16)


def _dot3_rhs(w_bf16, x):
    hi, mid, lo = _split3(x)
    return _dot(w_bf16, hi) + _dot(w_bf16, mid) + _dot(w_bf16, lo)


def _ada_kernel(c_ref, w_ref, b_ref, o_ref):
    c = c_ref[...]
    a = c * _sigmoid(c)
    o_ref[...] = jnp.dot(a, w_ref[...], precision=HIGHEST, preferred_element_type=F32) + b_ref[...]


def _ada_mod(c, w, b):
    d, n = w.shape
    tn = 1024
    c8 = jnp.broadcast_to(c, (8, d))
    out = pl.pallas_call(
        _ada_kernel,
        grid=(n // tn,),
        in_specs=[pl.BlockSpec((8, d), lambda j: (0, 0)),
                  pl.BlockSpec((d, tn), lambda j: (0, j)),
                  pl.BlockSpec((1, tn), lambda j: (0, j))],
        out_specs=pl.BlockSpec((8, tn), lambda j: (0, j)),
        out_shape=jax.ShapeDtypeStruct((8, n), F32),
        compiler_params=_cparams(("parallel",)),
        name="ada_mod",
    )(c8, w, b.reshape(1, n))
    return out[0:1]


def _lnmod_kernel(x_ref, sc_ref, sh_ref, o_ref):
    y = _layer_norm(x_ref[...])
    o_ref[...] = (y * (1.0 + sc_ref[...]) + sh_ref[...]).astype(o_ref.dtype)


def _ln_mod(x, sc, sh, out_dtype, tm=512):
    m, d = x.shape
    return pl.pallas_call(
        _lnmod_kernel,
        grid=(m // tm,),
        in_specs=[pl.BlockSpec((tm, d), lambda i: (i, 0)),
                  pl.BlockSpec((1, d), lambda i: (0, 0)),
                  pl.BlockSpec((1, d), lambda i: (0, 0))],
        out_specs=pl.BlockSpec((tm, d), lambda i: (i, 0)),
        out_shape=jax.ShapeDtypeStruct((m, d), out_dtype),
        compiler_params=_cparams(("parallel",)),
        name="ln_mod",
    )(x, sc, sh)


def _mm_kernel(a_ref, w_ref, o_ref):
    o_ref[...] = _dot(a_ref[...], w_ref[...]).astype(o_ref.dtype)


def _matmul(a, w, out_dtype, tm, tn, name):
    m, k = a.shape
    n = w.shape[1]
    return pl.pallas_call(
        _mm_kernel,
        grid=(n // tn, m // tm),
        in_specs=[pl.BlockSpec((tm, k), lambda j, i: (i, 0)),
                  pl.BlockSpec((k, tn), lambda j, i: (0, j))],
        out_specs=pl.BlockSpec((tm, tn), lambda j, i: (i, j)),
        out_shape=jax.ShapeDtypeStruct((m, n), out_dtype),
        compiler_params=_cparams(("parallel", "parallel")),
        name=name,
    )(a, w)


def _mm_nt_kernel(w_ref, a_ref, o_ref):
    o_ref[...] = _dot_nt(w_ref[...], a_ref[...]).astype(o_ref.dtype)


def _matmul_nt(w_t, a, out_dtype, tm, name):
    n, k = w_t.shape
    m = a.shape[0]
    return pl.pallas_call(
        _mm_nt_kernel,
        grid=(m // tm,),
        in_specs=[pl.BlockSpec((n, k), lambda i: (0, 0), pipeline_mode=pl.Buffered(1)),
                  pl.BlockSpec((tm, k), lambda i: (i, 0))],
        out_specs=pl.BlockSpec((n, tm), lambda i: (0, i)),
        out_shape=jax.ShapeDtypeStruct((n, m), out_dtype),
        compiler_params=_cparams(("parallel",)),
        name=name,
    )(w_t, a)


def _gelu_tanh(x):
    return 0.5 * x * (1.0 + jnp.tanh(math.sqrt(2.0 / math.pi) * (x + 0.044715 * (x * x * x))))


def _compress_kernel(c_ref, pe_ref, w1a_ref, w1b_ref, b1_ref, w2_ref, o_ref, *, nch):
    c = c_ref[0, 0]
    w1a = w1a_ref[0]
    w1b = w1b_ref[0]
    half = CMP_STRIDE * HEAD_DIM
    a = _dot(c, w1a)
    b = _dot(c, w1b)
    b_next = pltpu.roll(b, shift=nch - 1, axis=0)
    pe = pe_ref[0]
    pb = _dot(pe[:, :half], w1a) + _dot(pe[:, half:], w1b)
    hid = _gelu_tanh(a + b_next + pb[0:1, :] + b1_ref[0])
    o_ref[0, 0] = _dot(hid.astype(BF16), w2_ref[0])


def _compress(kv_cmp, pe, w1, b1, w2):
    z, g, s, hd = kv_cmp.shape
    nch = s // CMP_STRIDE
    half = CMP_STRIDE * hd
    chunks = kv_cmp.reshape(z, g, nch, half)
    pe8 = jnp.broadcast_to(pe.reshape(z, 1, CMP_LEN * hd), (z, 8, CMP_LEN * hd)).astype(BF16)
    w1b16 = w1.astype(BF16)
    hidn = w1.shape[-1]
    return pl.pallas_call(
        functools.partial(_compress_kernel, nch=nch),
        grid=(z, g),
        in_specs=[pl.BlockSpec((1, 1, nch, half), lambda zi, gi: (zi, gi, 0, 0)),
                  pl.BlockSpec((1, 8, 2 * half), lambda zi, gi: (zi, 0, 0)),
                  pl.BlockSpec((1, half, hidn), lambda zi, gi: (zi, 0, 0)),
                  pl.BlockSpec((1, half, hidn), lambda zi, gi: (zi, 1, 0)),
                  pl.BlockSpec((1, 1, hidn), lambda zi, gi: (zi, 0, 0)),
                  pl.BlockSpec((1, hidn, hd), lambda zi, gi: (zi, 0, 0))],
        out_specs=pl.BlockSpec((1, 1, nch, hd), lambda zi, gi: (zi, gi, 0, 0)),
        out_shape=jax.ShapeDtypeStruct((z, g, nch, hd), F32),
        compiler_params=_cparams(("parallel", "parallel")),
        name="compress_kv",
    )(chunks, pe8, w1b16, w1b16, b1.reshape(z, 1, hidn), w2.astype(BF16))


def _t5_bucket_np(dist):
    n = np.maximum(dist, 0)
    ratio = np.log(np.maximum(n, T5_MAX_EXACT).astype(np.float64) / T5_MAX_EXACT)
    big = T5_MAX_EXACT + (ratio / math.log(T5_MAX_DIST / T5_MAX_EXACT)
                          * (T5_BUCKETS - T5_MAX_EXACT)).astype(np.int64)
    return np.where(n < T5_MAX_EXACT, n, np.minimum(big, T5_BUCKETS - 1)).astype(np.int32)


def _t5_lookup(tbh, dist):
    onehot = np.eye(T5_BUCKETS, dtype=np.float32)[_t5_bucket_np(dist).reshape(-1)]
    vals = jnp.einsum('ghb,nb->ghn', tbh, jnp.asarray(onehot), precision=HIGHEST)
    return vals.reshape(tbh.shape[:2] + dist.shape)


def _toeplitz_t(w, n_keys, n_q):
    length = n_keys + n_q - 1
    w_pad = jnp.concatenate([w, jnp.zeros(w.shape[:-1] + (1,), w.dtype)], axis=-1)
    reps = (1,) * (w.ndim - 1) + (n_keys,)
    flat = jnp.tile(w_pad, reps)[..., :n_keys * length]
    return flat.reshape(w.shape[:-1] + (n_keys, length))[..., n_keys - 1:n_keys - 1 + n_q]


def _att_table_t(tbh, n_keys, tq, lo, hi, minus_far):
    d = np.arange(n_keys + tq - 1) - (tq - 1)
    valid = (d >= lo) & (d < hi)
    vals = _t5_lookup(tbh, d)
    if minus_far:
        vals = vals - tbh[:, :, T5_BUCKETS - 1:]
    w = jnp.where(jnp.asarray(valid), vals * LOG2E, NEG_INF)
    t = _toeplitz_t(w, n_keys, tq)
    g, hpg = tbh.shape[:2]
    return t.transpose(0, 2, 1, 3).reshape(g, n_keys, hpg * tq)


def _nsa_bias_tables(t5_table):
    tbh = t5_table.T.reshape(NSA_GROUPS, NSA_HPG, T5_BUCKETS).astype(F32)
    j = np.arange(CMP_NEAR_ROWS)[:, None]
    i = np.arange(ATT_TQ)[None, :]
    dist = i - (CMP_LEN - 1) - CMP_STRIDE * (j - CMP_PAD)
    vals = (_t5_lookup(tbh, dist) - tbh[:, :, T5_BUCKETS - 1][:, :, None, None]) * LOG2E
    vals = jnp.where(jnp.asarray(dist >= 0)[None, None], vals, NEG_INF)
    b_cmp = vals.transpose(0, 2, 1, 3).reshape(NSA_GROUPS, CMP_NEAR_ROWS, NSA_HPG * ATT_TQ)
    far = tbh[:, :, T5_BUCKETS - 1] * LOG2E
    b_sel = _att_table_t(tbh, 2 * ATT_TQ, ATT_TQ, 0, 1 << 30, True)
    b_sel = jnp.pad(b_sel, ((0, 0), (ATT_TQ, 0), (0, 0)))
    b_win = _att_table_t(tbh, 3 * ATT_TQ, ATT_TQ, 0, WINDOW, False)
    b_win = jnp.pad(b_win, ((0, 0), (0, ATT_TQ), (0, 0)), constant_values=NEG_INF)
    hi, mid, lo = _split3_exact(jnp.repeat(far, ATT_TQ, axis=1))
    far3 = jnp.stack([hi, mid, lo, jnp.full(hi.shape, NEG_INF, BF16)], axis=1)
    far3 = jnp.pad(far3, ((0, 0), (0, BF16_ROWS - 4), (0, 0)))
    return b_cmp, b_sel, b_win, far3


def _nsa_cmp_kernel(qt_ref, kc_ref, vct_ref, near_ref, far3_ref, oct_ref, selt_ref, rhs, s_scr, imp_scr, *, nslc):
    tq = ATT_TQ
    lanes = NSA_HPG * tq
    qb = pl.program_id(1)
    cpb = tq // CMP_STRIDE
    rhs[...] = jnp.zeros(rhs.shape, BF16)
    rhs[0:HEAD_DIM, :] = qt_ref[0, 0]
    rhs[ROW_BIAS:ROW_BIAS + BF16_ROWS, :] = far3_ref[0]
    chunk = lax.broadcasted_iota(jnp.int32, (AUG_K, lanes), 0)
    rhs[AUG_K:2 * AUG_K, :] = jnp.where(chunk >= qb + CMP_NEAR_ROWS // BF16_ROWS, NEG_INF, 0.0).astype(BF16)
    r0 = pl.multiple_of(cpb * qb, BF16_ROWS)
    n_lane_blocks = tq // LANES
    ncp = s_scr.shape[0]
    imp_scr[...] = jnp.zeros(imp_scr.shape, F32)

    def attend(rows):
        s_scr[0:rows, :] = _dot(kc_ref[0, 0:rows, :], rhs[...])
        s_scr[pl.ds(r0, CMP_NEAR_ROWS), :] = s_scr[pl.ds(r0, CMP_NEAR_ROWS), :] + near_ref[0]
        s = s_scr[0:rows, :]
        m = jnp.max(s, axis=0, keepdims=True)
        e = jnp.exp2(s - m)
        l = jnp.sum(e, axis=0, keepdims=True)
        p = e * jnp.where(m > M_INIT, 1.0 / l, 0.0)
        oct_ref[0, 0] = _dot(vct_ref[0, :, 0:rows], p.astype(BF16))
        imp = p[:, 0:tq]
        for h in range(1, NSA_HPG):
            imp = imp + p[:, h * tq:(h + 1) * tq]
        for c in range(n_lane_blocks):
            imp_scr[c, 0:rows, :] = imp[:, c * LANES:(c + 1) * LANES]

    limits = sorted({min(ncp, -(-(ncp * k // 3) // LANES) * LANES) for k in (1, 2, 3)})
    lo_qb = 0
    for rows in limits:
        hi_qb = (rows - CMP_NEAR_ROWS) // cpb if rows < ncp else pl.num_programs(1) - 1

        @pl.when((qb >= lo_qb) & (qb <= hi_qb))
        def _(rows=rows):
            attend(rows)

        lo_qb = hi_qb + 1
    ratio = SLC_LEN // CMP_STRIDE

    def taps(off):
        return jnp.concatenate([imp_scr[c, pl.ds(CMP_PAD + off, nslc, stride=ratio), :]
                                for c in range(n_lane_blocks)], axis=1)

    p_slc = 0.5 * (taps(-1) + taps(ratio - 1))
    for off in range(ratio - 1):
        p_slc = p_slc + taps(off)
    blk = lax.broadcasted_iota(jnp.int32, (nslc, tq), 0)
    cur = (qb * tq + lax.broadcasted_iota(jnp.int32, (nslc, tq), 1)) // SLC_LEN
    forced = (blk == 0) | (blk == cur) | (blk == cur - 1)
    score = jnp.where(forced, FORCE_SCORE, jnp.where(blk <= cur, p_slc, -1.0))
    blk_f = blk.astype(F32)
    sel = jnp.zeros((nslc, tq), F32)
    for _ in range(min(SLC_TOPK, nslc)):
        mx = jnp.max(score, axis=0, keepdims=True)
        first = jnp.min(jnp.where(score == mx, blk_f, float(nslc)), axis=0, keepdims=True)
        hit = blk_f == first
        sel = jnp.where(hit, 1.0, sel)
        score = jnp.where(hit, -2.0, score)
    selt_ref[0] = sel.astype(BF16)


def _nsa_compress_select(qt, kc_aug, vct, b_cmp, far3):
    g, nq, hd, lanes = qt.shape
    ncp = kc_aug.shape[1]
    tq = ATT_TQ
    nslc = nq * tq // SLC_LEN
    assert ncp // BF16_ROWS <= AUG_K
    assert tq // CMP_STRIDE == BF16_ROWS
    per_q = lambda gi, qi: (gi, qi, 0, 0)
    per_g = lambda gi, qi: (gi, 0, 0)
    return pl.pallas_call(
        functools.partial(_nsa_cmp_kernel, nslc=nslc),
        grid=(g, nq),
        in_specs=[pl.BlockSpec((1, 1, hd, lanes), per_q),
                  pl.BlockSpec((1, ncp, 2 * AUG_K), per_g),
                  pl.BlockSpec((1, hd, ncp), per_g),
                  pl.BlockSpec((1, CMP_NEAR_ROWS, lanes), per_g),
                  pl.BlockSpec((1, BF16_ROWS, lanes), per_g)],
        out_specs=[pl.BlockSpec((1, 1, hd, lanes), per_q),
                   pl.BlockSpec((1, nslc, tq), lambda gi, qi: (gi, 0, qi))],
        out_shape=[jax.ShapeDtypeStruct((g, nq, hd, lanes), F32),
                   jax.ShapeDtypeStruct((g, nslc, nq * tq), BF16)],
        scratch_shapes=[pltpu.VMEM((2 * AUG_K, lanes), BF16), pltpu.VMEM((ncp, lanes), F32),
                        pltpu.VMEM((tq // LANES, ncp, LANES), F32)],
        compiler_params=_cparams(("parallel", "parallel")),
        name="nsa_compress_select",
    )(qt, kc_aug, vct, b_cmp, far3)


def _flash_init_t(m_ref, acc_ref):
    m_ref[...] = jnp.full(m_ref.shape, M_INIT, F32)
    acc_ref[...] = jnp.zeros(acc_ref.shape, F32)


def _flash_step_t(s, vt_tile, m_ref, acc_ref):
    m_old = m_ref[...]
    m_new = jnp.maximum(m_old, jnp.max(s, axis=0, keepdims=True))
    p = jnp.exp2(s - m_new).astype(BF16)
    acc_ref[...] = jnp.exp2(m_old - m_new) * acc_ref[...] + _dot(vt_tile, p)
    m_ref[...] = m_new


def _flash_result_t(acc_ref):
    acc = acc_ref[...]
    return acc[0:HEAD_DIM, :] / acc[HEAD_DIM:HEAD_DIM + 1, :]


def _nsa_att_kernel(qt_ref, ks_ref, kx_ref, vst_ref, kw_ref, vwt_ref, selt_ref, far3_ref, bsel_ref, bwin_ref,
                    oct_ref, gate_ref, o_ref, rhs_s, rhs_w, mask_t, ms, accs, mw, accw, s_even, s_odd):
    tq = ATT_TQ
    qb = pl.program_id(1)
    ones_rows = _ones_rows(tq)
    kx_tiles = kx_ref.shape[0] // tq
    qt = qt_ref[0, 0]
    rhs_s[...] = jnp.zeros(rhs_s.shape, BF16)
    rhs_s[0:HEAD_DIM, :] = qt
    rhs_s[ROW_BIAS:ROW_BIAS + BF16_ROWS, :] = far3_ref[0]
    rhs_w[...] = jnp.zeros(rhs_w.shape, BF16)
    rhs_w[0:HEAD_DIM, :] = qt
    madd = ((selt_ref[0].astype(F32) - 1.0) * (-NEG_INF)).astype(BF16)
    mask_t[...] = jnp.concatenate([madd] * NSA_HPG, axis=1)
    _flash_init_t(ms, accs)
    _flash_init_t(mw, accw)
    blocks_per_tile = tq // SLC_LEN

    def sel_scores(kt, s_ref):
        kt = jnp.minimum(kt, qb)
        k0 = pl.multiple_of(kt * tq, tq)
        chunk = pl.multiple_of((kt * blocks_per_tile) // BF16_ROWS * BF16_ROWS, BF16_ROWS)
        rhs_s[ROW_MASK:ROW_MASK + BF16_ROWS, :] = mask_t[pl.ds(chunk, BF16_ROWS), :]
        rel = jnp.clip(kt - qb + 2, 0, 2)
        table = bsel_ref[0, pl.ds(pl.multiple_of(rel * tq, tq), tq), :]
        k_aug = ks_ref[pl.ds(k0, tq), :] + kx_ref[pl.ds(pl.multiple_of((kt % kx_tiles) * tq, tq), tq), :]
        s_ref[...] = _dot(k_aug, rhs_s[...]) + table

    def sel_consume(kt, s_ref):
        k0 = pl.multiple_of(kt * tq, tq)
        vt_aug = jnp.concatenate([vst_ref[0, :, pl.ds(k0, tq)], ones_rows], axis=0)
        _flash_step_t(s_ref[...], vt_aug, ms, accs)

    def win_scores(j, s_ref):
        kt = qb - 2 + j
        k0 = pl.multiple_of(jnp.maximum(kt, 0) * tq, tq)
        row = pl.multiple_of(jnp.where(kt >= 0, j, 3) * tq, tq)
        s_ref[...] = _dot(kw_ref[pl.ds(k0, tq), :], rhs_w[...]) + bwin_ref[0, pl.ds(row, tq), :]

    def win_consume(j, s_ref):
        k0 = pl.multiple_of(jnp.maximum(qb - 2 + j, 0) * tq, tq)
        vt_aug = jnp.concatenate([vwt_ref[0, :, pl.ds(k0, tq)], ones_rows], axis=0)
        _flash_step_t(s_ref[...], vt_aug, mw, accw)

    n_sel = qb + 1
    sel_scores(0, s_even)

    def pair_body(j, carry):
        sel_scores(2 * j + 1, s_odd)
        sel_consume(2 * j, s_even)
        sel_scores(2 * j + 2, s_even)
        sel_consume(2 * j + 1, s_odd)
        return carry

    lax.fori_loop(0, n_sel // 2, pair_body, 0)
    win_scores(0, s_odd)

    @pl.when(n_sel % 2 == 1)
    def _():
        sel_consume(qb, s_even)

    win_scores(1, s_even)
    win_consume(0, s_odd)
    win_scores(2, s_odd)
    win_consume(1, s_even)
    win_consume(2, s_odd)

    gt = _sigmoid(gate_ref[0, 0])
    out = gt[0:1, :] * oct_ref[0, 0] + gt[1:2, :] * _flash_result_t(accs) + gt[2:3, :] * _flash_result_t(accw)
    for h in range(NSA_HPG):
        o_ref[0, h] = out[:, h * tq:(h + 1) * tq].astype(o_ref.dtype)


def _nsa_attention(qt, k_tok, ks_block0, kw_block0, k_extra, vst, vwt, selt, far3, b_sel, b_win, oct_, gates_t):
    g, nq, hd, lanes = qt.shape
    s = k_tok.shape[0]
    nslc = selt.shape[1]
    tq = ATT_TQ
    resident = pl.Buffered(1)
    per_q = lambda gi, qi: (gi, qi, 0, 0)
    per_g = lambda gi, qi: (gi, 0, 0)
    return pl.pallas_call(
        _nsa_att_kernel,
        grid=(g, nq),
        in_specs=[pl.BlockSpec((1, 1, hd, lanes), per_q),
                  pl.BlockSpec((s, AUG_K), lambda gi, qi: (0, ks_block0 + gi), pipeline_mode=resident),
                  pl.BlockSpec(k_extra.shape, lambda gi, qi: (0, 0), pipeline_mode=resident),
                  pl.BlockSpec((1, hd, s), per_g, pipeline_mode=resident),
                  pl.BlockSpec((s, AUG_K), lambda gi, qi: (0, kw_block0 + gi), pipeline_mode=resident),
                  pl.BlockSpec((1, hd, s), per_g, pipeline_mode=resident),
                  pl.BlockSpec((1, nslc, tq), lambda gi, qi: (gi, 0, qi)),
                  pl.BlockSpec((1, BF16_ROWS, lanes), per_g),
                  pl.BlockSpec((1, 3 * tq, lanes), per_g, pipeline_mode=resident),
                  pl.BlockSpec((1, 4 * tq, lanes), per_g, pipeline_mode=resident),
                  pl.BlockSpec((1, 1, hd, lanes), per_q),
                  pl.BlockSpec((1, 1, 8, lanes), per_q)],
        out_specs=pl.BlockSpec((1, lanes // tq, hd, tq), lambda gi, qi: (gi, 0, 0, qi)),
        out_shape=jax.ShapeDtypeStruct((g, lanes // tq, hd, s), BF16),
        scratch_shapes=[pltpu.VMEM((AUG_K, lanes), BF16), pltpu.VMEM((AUG_K, lanes), BF16),
                        pltpu.VMEM((nslc, lanes), BF16),
                        pltpu.VMEM((1, lanes), F32), pltpu.VMEM((AUG_V, lanes), F32),
                        pltpu.VMEM((1, lanes), F32), pltpu.VMEM((AUG_V, lanes), F32),
                        pltpu.VMEM((tq, lanes), F32), pltpu.VMEM((tq, lanes), F32)],
        compiler_params=_cparams(("arbitrary", "arbitrary")),
        name="nsa_select_window",
    )(qt, k_tok, k_extra, vst, k_tok, vwt, selt, far3, b_sel, b_win, oct_, gates_t)


def _decay_kernel(z_ref, b_ref, place_ref, o_ref, end_ref, carry_ref, *, tb):
    @pl.when(pl.program_id(0) == 0)
    def _():
        carry_ref[...] = jnp.zeros(carry_ref.shape, F32)

    z = z_ref[...] + b_ref[...]
    log_f = jnp.minimum(z, 0.0) - jnp.log1p(jnp.exp(-jnp.abs(z)))
    r = lax.broadcasted_iota(jnp.int32, (tb, tb), 0)
    c = lax.broadcasted_iota(jnp.int32, (tb, tb), 1)
    tri = jnp.where(r >= c, 1.0, 0.0).astype(BF16)
    run = _dot3_rhs(tri, log_f) + carry_ref[...]
    carry_ref[...] = run[tb - 1:tb, :]
    val = -run * LOG2E
    hi, mid, lo = _split3_exact(val)
    o_ref[...] = (_dot(hi, place_ref[0]) + _dot(mid, place_ref[1]) + _dot(lo, place_ref[2])).astype(BF16)
    end_ref[...] = jnp.broadcast_to(val[tb - 1:tb, :], end_ref.shape)


def _decay_pieces(z, bias, first_lane, n_heads):
    s, n = z.shape
    tb = FOX_TK
    place = np.zeros((3, n, n_heads * AUG_K), np.float32)
    for h in range(n_heads):
        for piece in range(3):
            place[piece, first_lane + h, h * AUG_K + HEAD_DIM + piece] = 1.0
    pieces, ends = pl.pallas_call(
        functools.partial(_decay_kernel, tb=tb),
        grid=(s // tb,),
        in_specs=[pl.BlockSpec((tb, n), lambda i: (i, 0)),
                  pl.BlockSpec((1, n), lambda i: (0, 0)),
                  pl.BlockSpec((3, n, n_heads * AUG_K), lambda i: (0, 0, 0))],
        out_specs=[pl.BlockSpec((tb, n_heads * AUG_K), lambda i: (i, 0)),
                   pl.BlockSpec((8, n), lambda i: (i, 0))],
        out_shape=[jax.ShapeDtypeStruct((s, n_heads * AUG_K), BF16),
                   jax.ShapeDtypeStruct((s // tb * 8, n), F32)],
        scratch_shapes=[pltpu.VMEM((1, n), F32)],
        compiler_params=_cparams(("arbitrary",)),
        name="decay_cumsum",
    )(z, bias, jnp.asarray(place, BF16))
    return pieces, ends[::8, first_lane:first_lane + n_heads].T


def _ones_rows(width):
    return jnp.where(lax.broadcasted_iota(jnp.int32, (BF16_ROWS, width), 0) < 8, 1.0, 0.0).astype(BF16)


def _fox_kernel(ends_ref, qt_ref, k_ref, dk_ref, vt_ref, o_ref, rhs, m_ref, acc_ref, s_even, s_odd, kmax_ref):
    tq = FOX_TQ
    tk = FOX_TK
    assert tq == 2 * tk
    qb = pl.program_id(1)
    row = lax.broadcasted_iota(jnp.int32, (AUG_K - HEAD_DIM, tq), 0)
    rhs[0:HEAD_DIM, :] = qt_ref[0]
    rhs[HEAD_DIM:AUG_K, :] = jnp.where(row < 3, 1.0, 0.0).astype(BF16)
    ones_rows = _ones_rows(tk)
    _flash_init_t(m_ref, acc_ref)

    def scores(kt, s_ref):
        k0 = pl.multiple_of(kt * tk, tk)
        k_aug = k_ref[pl.ds(k0, tk), :] + dk_ref[pl.ds(k0, tk), :]
        s_ref[...] = _dot(k_aug, rhs[...])

    def consume(kt, s_ref, diagonal):
        k0 = pl.multiple_of(kt * tk, tk)
        s = s_ref[...]
        if diagonal:
            key = k0 + lax.broadcasted_iota(jnp.int32, (tk, tq), 0)
            qry = qb * tq + lax.broadcasted_iota(jnp.int32, (tk, tq), 1)
            s = jnp.where(key <= qry, s, NEG_INF)
        vt_aug = jnp.concatenate([vt_ref[0, :, pl.ds(k0, tk)], ones_rows], axis=0)
        _flash_step_t(s, vt_aug, m_ref, acc_ref)

    @pl.when(qb == 0)
    def _():
        ones = jnp.ones((AUG_K, AUG_K), BF16)

        def norm_tile(c, best):
            k = k_ref[pl.ds(pl.multiple_of(c * tk, tk), tk), :].astype(F32)
            return jnp.maximum(best, _dot((k * k).astype(BF16), ones))

        best = lax.fori_loop(0, k_ref.shape[0] // tk, norm_tile, jnp.zeros((tk, AUG_K), F32))
        kmax_ref[0] = jnp.max(jnp.sqrt(best * FOX_BOUND_SLACK))

    scores(2 * qb, s_even)
    scores(2 * qb + 1, s_odd)
    consume(2 * qb, s_even, True)
    consume(2 * qb + 1, s_odd, True)

    q = qt_ref[0].astype(F32)
    q_norm = jnp.max(jnp.sqrt(jnp.sum(q * q, axis=0, keepdims=True) * FOX_BOUND_SLACK))
    threshold = jnp.min(m_ref[...]) - FOX_SKIP_MARGIN - q_norm * kmax_ref[0]
    head = pl.program_id(0)

    def first_needed(j, first):
        return jnp.where(ends_ref[head, 2 * j + 1] >= threshold, jnp.minimum(first, j), first)

    j0 = lax.fori_loop(0, qb, first_needed, qb)

    scores(2 * j0, s_even)

    def pair_body(j, carry):
        scores(2 * j + 1, s_odd)
        consume(2 * j, s_even, False)
        scores(2 * j + 2, s_even)
        consume(2 * j + 1, s_odd, False)
        return carry

    lax.fori_loop(j0, qb, pair_body, 0)
    o_ref[0] = _flash_result_t(acc_ref).astype(o_ref.dtype)


def _fox_attention(qt, k_tok, k_block0, decay_k, decay_ends, vt):
    h, hd, s = qt.shape
    tq = FOX_TQ
    resident = pl.Buffered(1)
    grid_spec = pltpu.PrefetchScalarGridSpec(
        num_scalar_prefetch=1,
        grid=(h, s // tq),
        in_specs=[pl.BlockSpec((1, hd, tq), lambda hi, qi, ends: (hi, 0, qi)),
                  pl.BlockSpec((s, AUG_K), lambda hi, qi, ends: (0, k_block0 + hi), pipeline_mode=resident),
                  pl.BlockSpec((s, AUG_K), lambda hi, qi, ends: (0, hi), pipeline_mode=resident),
                  pl.BlockSpec((1, hd, s), lambda hi, qi, ends: (hi, 0, 0), pipeline_mode=resident)],
        out_specs=pl.BlockSpec((1, hd, tq), lambda hi, qi, ends: (hi, 0, qi)),
        scratch_shapes=[pltpu.VMEM((AUG_K, tq), BF16),
                        pltpu.VMEM((1, tq), F32), pltpu.VMEM((AUG_V, tq), F32),
                        pltpu.VMEM((FOX_TK, tq), F32), pltpu.VMEM((FOX_TK, tq), F32),
                        pltpu.SMEM((1,), F32)],
    )
    return pl.pallas_call(
        _fox_kernel,
        grid_spec=grid_spec,
        out_shape=jax.ShapeDtypeStruct((h, hd, s), BF16),
        compiler_params=_cparams(("arbitrary", "arbitrary")),
        name="fox_attention",
    )(decay_ends, qt, k_tok, decay_k, vt)


def _merge_kernel(on_ref, of_ref, mg_ref, x_ref, wn_ref, wf_ref, wo_ref, g1_ref, lg_ref, lb_ref, o_ref, *, alpha):
    d = x_ref.shape[-1]
    tn = (((0,), (0,)), ((), ()))
    a = lax.dot_general(on_ref[...], wn_ref[...], tn, preferred_element_type=F32)
    b = lax.dot_general(of_ref[...], wf_ref[...], tn, preferred_element_type=F32)
    gm = _sigmoid(mg_ref[...].astype(F32))
    merged = gm[:, 0:d] * a + gm[:, d:2 * d] * b
    y = _dot(merged.astype(BF16), wo_ref[...])
    z = alpha * x_ref[...] + (1.0 + g1_ref[...]) * y
    o_ref[...] = _layer_norm(z) * lg_ref[...] + lb_ref[...]


def _merge_project(o_nsa, o_fox, merge, x, wn, wf, wo, g1, ln_g, ln_b, alpha, tm=256):
    m, d = x.shape
    w = o_nsa.shape[0]
    resident = pl.Buffered(1)
    row = lambda i: (i, 0)
    fixed = lambda i: (0, 0)
    return pl.pallas_call(
        functools.partial(_merge_kernel, alpha=alpha),
        grid=(m // tm,),
        in_specs=[pl.BlockSpec((w, tm), lambda i: (0, i)), pl.BlockSpec((w, tm), lambda i: (0, i)),
                  pl.BlockSpec((tm, 2 * d), row), pl.BlockSpec((tm, d), row),
                  pl.BlockSpec((w, d), fixed, pipeline_mode=resident),
                  pl.BlockSpec((w, d), fixed, pipeline_mode=resident),
                  pl.BlockSpec((d, d), fixed, pipeline_mode=resident),
                  pl.BlockSpec((1, d), fixed), pl.BlockSpec((1, d), fixed), pl.BlockSpec((1, d), fixed)],
        out_specs=pl.BlockSpec((tm, d), row),
        out_shape=jax.ShapeDtypeStruct((m, d), F32),
        compiler_params=_cparams(("parallel",)),
        name="merge_project_ln",
    )(o_nsa, o_fox, merge, x, wn, wf, wo, g1, ln_g, ln_b)


def _router_kernel(x_ref, sc_ref, sh_ref, w_ref, b_ref, u_ref, r_ref):
    u = _layer_norm(x_ref[...]) * (1.0 + sc_ref[...]) + sh_ref[...]
    u_ref[...] = u
    u_hi = u.astype(BF16)
    u_lo = (u - u_hi.astype(F32)).astype(BF16)
    logits = _dot(u_hi, w_ref[0]) + _dot(u_lo, w_ref[0]) + _dot(u_hi, w_ref[1]) + b_ref[...]
    lane = lax.broadcasted_iota(jnp.int32, (1, LANES), 1).astype(F32)
    none = float(LANES)
    is_g = lane < N_GROUPS
    lg = jnp.where(is_g, logits, NEG_INF)
    eg = jnp.exp(lg - jnp.max(lg, axis=-1, keepdims=True))
    pg = eg / jnp.sum(eg, axis=-1, keepdims=True)
    p_grp = jnp.max(pg, axis=-1, keepdims=True)
    grp = jnp.min(jnp.where(pg == p_grp, lane, none), axis=-1, keepdims=True)
    lo = N_GROUPS + grp * EXPERTS_PER_GROUP
    is_e = (lane >= lo) & (lane < lo + EXPERTS_PER_GROUP)
    le = jnp.where(is_e, logits, NEG_INF)
    ee = jnp.exp(le - jnp.max(le, axis=-1, keepdims=True))
    pe = jnp.where(is_e, ee / jnp.sum(ee, axis=-1, keepdims=True), -1.0)
    p1 = jnp.max(pe, axis=-1, keepdims=True)
    i1 = jnp.min(jnp.where(pe == p1, lane, none), axis=-1, keepdims=True)
    pe2 = jnp.where(lane == i1, -1.0, pe)
    p2 = jnp.max(pe2, axis=-1, keepdims=True)
    i2 = jnp.min(jnp.where(pe2 == p2, lane, none), axis=-1, keepdims=True)
    den = p1 + p2
    r_ref[...] = jnp.where(lane == 0, i1 - N_GROUPS,
                           jnp.where(lane == 1, i2 - N_GROUPS,
                                     jnp.where(lane == 2, p_grp * p1 / den,
                                               jnp.where(lane == 3, p_grp * p2 / den, 0.0))))


def _router(x1, sc, sh, w_r, b_r, tm=256):
    m, d = x1.shape
    row = lambda i: (i, 0)
    fixed = lambda i: (0, 0)
    return pl.pallas_call(
        _router_kernel,
        grid=(m // tm,),
        in_specs=[pl.BlockSpec((tm, d), row), pl.BlockSpec((1, d), fixed), pl.BlockSpec((1, d), fixed),
                  pl.BlockSpec((2, d, LANES), lambda i: (0, 0, 0)), pl.BlockSpec((1, LANES), fixed)],
        out_specs=[pl.BlockSpec((tm, d), row), pl.BlockSpec((tm, LANES), row)],
        out_shape=[jax.ShapeDtypeStruct((m, d), F32), jax.ShapeDtypeStruct((m, LANES), F32)],
        compiler_params=_cparams(("parallel",)),
        name="moe_router",
    )(x1, sc, sh, w_r, b_r)


def _moe_kernel(be_ref, nu_ref, tok_ref, tok_next_ref, dst_ref, rw_ref, u_hbm, wg_ref, wu_ref, wd_ref, out_hbm,
                xbuf, ybuf, wgb, wub, wdb, sem_in, sem_out, *, n_dump0):
    rb = ROW_BLOCK
    i = pl.program_id(0)
    last = nu_ref[0] - 1
    slot = i % 2

    def row_in(r, tok, sl):
        return pltpu.make_async_copy(u_hbm.at[pl.ds(tok, 1), :], xbuf.at[sl, pl.ds(r, 1), :], sem_in.at[sl])

    def row_out(r, dst):
        return pltpu.make_async_copy(ybuf.at[pl.ds(r, 1), :], out_hbm.at[pl.ds(dst, 1), :], sem_out)

    @pl.when(i == 0)
    def _():
        for r in range(rb):
            row_in(r, tok_ref[0, 0, r], 0).start()
        ybuf[...] = jnp.zeros(ybuf.shape, F32)
        pltpu.make_async_copy(ybuf, out_hbm.at[pl.ds(n_dump0, rb), :], sem_out).start()

    prev = be_ref[jnp.maximum(i - 1, 0)]

    @pl.when((i <= last) & ((i == 0) | (be_ref[i] != prev)))
    def _():
        wgb[...] = wg_ref[0].astype(BF16)
        wub[...] = wu_ref[0].astype(BF16)
        wdb[...] = wd_ref[0].astype(BF16)

    @pl.when(i <= last)
    def _():
        for r in range(rb):
            row_in(r, 0, slot).wait()
        xb = xbuf[slot].astype(BF16)
        for r in range(rb):
            row_in(r, tok_next_ref[0, 0, r], 1 - slot).start()
        gate = _dot(xb, wgb[...])
        up = _dot(xb, wub[...])
        hid = (gate * _sigmoid(gate)) * up
        y = _dot(hid.astype(BF16), wdb[...]) * rw_ref[0]
        for r in range(rb):
            row_out(r, 0).wait()
        ybuf[...] = y
        for r in range(rb):
            row_out(r, dst_ref[0, 0, r]).start(priority=r % 2)

    @pl.when(i == last)
    def _():
        for r in range(rb):
            row_out(r, 0).wait()
        for r in range(rb):
            row_in(r, 0, 1 - slot).wait()


def _moe_experts(u, blk_exp, n_used, row_tok, row_dst, row_w, w_gate, w_up, w_down):
    t, d = u.shape
    n_blocks = blk_exp.shape[0]
    de = w_gate.shape[-1]
    rb = ROW_BLOCK
    tok3 = row_tok.reshape(n_blocks, 1, rb)
    grid_spec = pltpu.PrefetchScalarGridSpec(
        num_scalar_prefetch=2,
        grid=(n_blocks,),
        in_specs=[pl.BlockSpec((1, 1, rb), lambda i, be, nu: (i, 0, 0), memory_space=pltpu.SMEM),
                  pl.BlockSpec((1, 1, rb), lambda i, be, nu: (jnp.minimum(i + 1, n_blocks - 1), 0, 0),
                               memory_space=pltpu.SMEM),
                  pl.BlockSpec((1, 1, rb), lambda i, be, nu: (i, 0, 0), memory_space=pltpu.SMEM),
                  pl.BlockSpec((1, rb, 1), lambda i, be, nu: (i, 0, 0)),
                  pl.BlockSpec(memory_space=pl.ANY),
                  pl.BlockSpec((1, d, de), lambda i, be, nu: (be[i], 0, 0)),
                  pl.BlockSpec((1, d, de), lambda i, be, nu: (be[i], 0, 0)),
                  pl.BlockSpec((1, de, d), lambda i, be, nu: (be[i], 0, 0))],
        out_specs=pl.BlockSpec(memory_space=pl.ANY),
        scratch_shapes=[pltpu.VMEM((2, rb, d), F32), pltpu.VMEM((rb, d), F32),
                        pltpu.VMEM((d, de), BF16), pltpu.VMEM((d, de), BF16), pltpu.VMEM((de, d), BF16),
                        pltpu.SemaphoreType.DMA((2,)), pltpu.SemaphoreType.DMA(())],
    )
    return pl.pallas_call(
        functools.partial(_moe_kernel, n_dump0=2 * t),
        grid_spec=grid_spec,
        out_shape=jax.ShapeDtypeStruct((2 * t + rb, d), F32),
        compiler_params=_cparams(("arbitrary",)),
        name="moe_experts",
    )(blk_exp, n_used, tok3, tok3, row_dst.reshape(n_blocks, 1, rb), row_w.reshape(n_blocks, rb, 1),
      u, w_gate, w_up, w_down)


def _moe_dispatch(route, t):
    k = 2
    eid = route[:, 0:k].astype(jnp.int32).reshape(-1)
    wts = route[:, k:2 * k].reshape(-1)
    n_asg = t * k
    n_rows = n_asg + N_EXPERTS * ROW_BLOCK
    n_blocks = n_rows // ROW_BLOCK
    onehot = (eid[:, None] == jnp.arange(N_EXPERTS, dtype=jnp.int32)[None, :]).astype(jnp.int32)
    rank = jnp.sum((jnp.cumsum(onehot, axis=0) - onehot) * onehot, axis=1)
    counts = jnp.sum(onehot, axis=0)
    padded = (counts + ROW_BLOCK - 1) // ROW_BLOCK * ROW_BLOCK
    pad_end = jnp.cumsum(padded)
    pad_start = pad_end - padded
    dest = jnp.sum(onehot * pad_start[None, :], axis=1) + rank
    asg = jnp.arange(n_asg, dtype=jnp.int32)
    upd = jnp.stack([(asg % k) * t + asg // k, lax.bitcast_convert_type(wts, jnp.int32)], axis=1)
    init = jnp.stack([n_asg + jnp.arange(n_rows, dtype=jnp.int32) % ROW_BLOCK,
                      jnp.zeros((n_rows,), jnp.int32)], axis=1)
    rows = init.at[dest].set(upd)
    row_dst = rows[:, 0]
    row_w = lax.bitcast_convert_type(rows[:, 1], F32)
    row_tok = jnp.where(row_dst < n_asg, row_dst % t, 0)
    blk_start = jnp.arange(n_blocks, dtype=jnp.int32) * ROW_BLOCK
    blk_exp = jnp.minimum(jnp.sum((pad_end[None, :] <= blk_start[:, None]).astype(jnp.int32), axis=1),
                          N_EXPERTS - 1)
    n_used = (pad_end[N_EXPERTS - 1:] // ROW_BLOCK).astype(jnp.int32)
    return blk_exp, n_used, row_tok, row_dst, row_w


def _final_kernel(x_ref, y0_ref, y1_ref, g2_ref, lg_ref, lb_ref, o_ref, *, alpha):
    z = alpha * x_ref[...] + (1.0 + g2_ref[...]) * (y0_ref[...] + y1_ref[...])
    o_ref[...] = _layer_norm(z) * lg_ref[...] + lb_ref[...]


def _final_ln(x1, y2, g2, ln_g, ln_b, alpha, tm=512):
    m, d = x1.shape
    nb = m // tm
    fixed = lambda i: (0, 0)
    return pl.pallas_call(
        functools.partial(_final_kernel, alpha=alpha),
        grid=(nb,),
        in_specs=[pl.BlockSpec((tm, d), lambda i: (i, 0)),
                  pl.BlockSpec((tm, d), lambda i: (i, 0)),
                  pl.BlockSpec((tm, d), lambda i: (i + nb, 0)),
                  pl.BlockSpec((1, d), fixed), pl.BlockSpec((1, d), fixed), pl.BlockSpec((1, d), fixed)],
        out_specs=pl.BlockSpec((tm, d), lambda i: (i, 0)),
        out_shape=jax.ShapeDtypeStruct((m, d), F32),
        compiler_params=_cparams(("parallel",)),
        name="final_ln",
    )(x1, y2, y2, g2, ln_g, ln_b)


def _to_lane_blocks(a, tq):
    g, hpg, s, c = a.shape
    return a.reshape(g, hpg, s // tq, tq, c).transpose(0, 2, 4, 1, 3).reshape(g, s // tq, c, hpg * tq)


def _layer(x2d, c, w_ada, b_ada, w_in, b_fgt, t5_table, cmp_pe, cmp_w1, cmp_b1, cmp_w2,
           w_br_nsa, w_br_fox, w_o, ln1_g, ln1_b, w_rg, b_rg, w_re, b_re,
           w_gate, w_up, w_down, ln2_g, ln2_b, alpha):
    s, d = x2d.shape
    hd = HEAD_DIM
    g = NSA_GROUPS
    mod = _ada_mod(c, w_ada, b_ada)
    sh1, sc1, g1, sh2, sc2, g2 = [mod[:, i * d:(i + 1) * d] for i in range(6)]

    c_q = NSA_HEADS * hd
    c_kv = 6 * g * hd
    c_gate = 3 * NSA_HEADS
    c_fox = 3 * FOX_HEADS * hd
    off_kv = c_q
    off_gate = off_kv + c_kv
    off_fox = off_gate + c_gate
    off_fgt = off_fox + c_fox
    off_merge = off_fgt + FOX_HEADS
    qscale = hd ** -0.5 * LOG2E
    nh = FOX_HEADS
    gw = g * hd

    def kv_cols(z):
        return w_in[:, off_kv + z * gw:off_kv + (z + 1) * gw]

    def lane_padded(w, heads):
        return jnp.pad(w.reshape(d, heads, hd), ((0, 0), (0, 0), (0, AUG_K - hd))).reshape(d, heads * AUG_K)

    fox_q, fox_k, fox_v = [w_in[:, off_fox + i * nh * hd:off_fox + (i + 1) * nh * hd] for i in range(3)]
    w_ch = jnp.concatenate([w_in[:, 0:off_kv] * qscale, kv_cols(3), kv_cols(5), fox_q * qscale, fox_v],
                           axis=1).T.astype(BF16)
    w_tok = jnp.concatenate([kv_cols(0), kv_cols(1), lane_padded(kv_cols(2), g), lane_padded(kv_cols(4), g),
                             lane_padded(fox_k, nh)], axis=1).astype(BF16)
    n_small = c_gate + FOX_HEADS
    w_small = jnp.concatenate([w_in[:, off_gate:off_fox], w_in[:, off_fgt:off_merge],
                               jnp.zeros((d, LANES - n_small), F32)], axis=1).astype(BF16)
    w_merge = w_in[:, off_merge:].astype(BF16)

    u = _ln_mod(x2d, sc1, sh1, BF16)
    ch = _matmul_nt(w_ch, u, BF16, 512, "in_proj_channel_major")
    tok = _matmul(u, w_tok, BF16, 512, w_tok.shape[1] // 2, "in_proj_token_major")
    small = _matmul(u, w_small, F32, 512, LANES, "in_proj_small")
    merge = _matmul(u, w_merge, BF16, 512, 1024, "in_proj_merge")

    qt_nsa = ch[0:c_q].reshape(g, NSA_HPG, hd, s)
    vst = ch[c_q:c_q + gw].reshape(g, hd, s)
    vwt = ch[c_q + gw:c_q + 2 * gw].reshape(g, hd, s)
    fox_qt = ch[c_q + 2 * gw:c_q + 2 * gw + nh * hd].reshape(nh, hd, s)
    fox_vt = ch[c_q + 2 * gw + nh * hd:].reshape(nh, hd, s)
    tok_ks_block0 = 2 * gw // AUG_K
    tok_kw_block0 = tok_ks_block0 + g
    tok_fox_block0 = tok_kw_block0 + g

    kv_cmp_in = tok[:, 0:2 * gw].reshape(s, 2, g, hd).transpose(1, 2, 0, 3)
    kv_cmp = _compress(kv_cmp_in, cmp_pe, cmp_w1, cmp_b1, cmp_w2)
    nch = s // CMP_STRIDE
    ncp = nch + LANES
    nslc = s // SLC_LEN
    kv_cmp_pad = jnp.pad(kv_cmp.astype(BF16), ((0, 0), (0, 0), (CMP_PAD, ncp - nch - CMP_PAD), (0, 0)))
    b_cmp, b_sel, b_win, far3 = _nsa_bias_tables(t5_table)

    row_ix = np.arange(ncp)
    row_ok = (row_ix >= CMP_PAD) & (row_ix < nch - 1 + CMP_PAD)
    row_cols = np.zeros((ncp, 2 * AUG_K - hd), np.float32)
    row_cols[:, ROW_BIAS - hd:ROW_BIAS - hd + 3] = row_ok[:, None]
    row_cols[:, ROW_BIAS - hd + 3] = ~row_ok
    row_cols[row_ix, AUG_K - hd + row_ix // BF16_ROWS] = 1.0
    kc_aug = jnp.concatenate([kv_cmp_pad[0], jnp.broadcast_to(jnp.asarray(row_cols, BF16), (g,) + row_cols.shape)],
                             axis=-1)
    vct = kv_cmp_pad[1].transpose(0, 2, 1)
    nq = s // ATT_TQ
    qt = qt_nsa.reshape(g, NSA_HPG, hd, nq, ATT_TQ).transpose(0, 3, 2, 1, 4).reshape(g, nq, hd, NSA_HPG * ATT_TQ)
    oct_, selt = _nsa_compress_select(qt, kc_aug, vct, b_cmp, far3)

    period = BF16_ROWS * SLC_LEN
    k_extra = np.zeros((period, AUG_K), np.float32)
    k_extra[np.arange(period), ROW_MASK + np.arange(period) // SLC_LEN] = 1.0
    k_extra[:, ROW_BIAS:ROW_BIAS + 3] = 1.0
    gates = small[:, 0:c_gate].reshape(s, g, NSA_HPG, 3).transpose(1, 2, 0, 3)
    gates_t = jnp.pad(_to_lane_blocks(gates, ATT_TQ), ((0, 0), (0, 0), (0, 5), (0, 0)))
    o_nsa_t = _nsa_attention(qt, tok, tok_ks_block0, tok_kw_block0, jnp.asarray(k_extra, BF16), vst, vwt, selt, far3,
                             b_sel, b_win, oct_, gates_t)
    o_nsa = o_nsa_t.reshape(MIX_W, s)

    fgt_bias = jnp.concatenate([jnp.zeros((c_gate,), F32), b_fgt, jnp.zeros((LANES - n_small,), F32)])[None, :]
    decay_k, decay_ends = _decay_pieces(small, fgt_bias, c_gate, nh)
    o_fox = _fox_attention(fox_qt, tok, tok_fox_block0, decay_k, decay_ends, fox_vt)
    o_fox = o_fox.reshape(MIX_W, s)

    x1 = _merge_project(o_nsa, o_fox, merge, x2d, w_br_nsa.astype(BF16), w_br_fox.astype(BF16),
                        w_o.astype(BF16), g1, ln1_g[None, :], ln1_b[None, :], alpha)

    n_r = N_GROUPS + N_EXPERTS
    w_r = jnp.concatenate([w_rg, w_re.reshape(d, N_EXPERTS), jnp.zeros((d, LANES - n_r), F32)], axis=1)
    b_r = jnp.concatenate([b_rg, b_re.reshape(N_EXPERTS), jnp.zeros((LANES - n_r,), F32)])[None, :]
    w_r_hi, w_r_lo, _ = _split3_exact(w_r)
    u2, route = _router(x1, sc2, sh2, jnp.stack([w_r_hi, w_r_lo]), b_r)
    blk_exp, n_used, row_tok, row_dst, row_w = _moe_dispatch(route, s)
    y2 = _moe_experts(u2, blk_exp, n_used, row_tok, row_dst, row_w, w_gate, w_up, w_down)
    return _final_ln(x1, y2, g2, ln2_g[None, :], ln2_b[None, :], alpha)


def kernel(x, c, w_ada, b_ada, w_in, b_fgt, t5_table, cmp_pe, cmp_w1, cmp_b1, cmp_w2, w_br_nsa, w_br_fox, w_o,
           ln1_g, ln1_b, w_rg, b_rg, w_re, b_re, w_gate, w_up, w_down, ln2_g, ln2_b):
    b, s, d = x.shape
    depth = w_ada.shape[0]
    assert b == 1
    alpha = (2 * depth) ** 0.25
    h = x[0]
    for l in range(depth):
        h = _layer(h, c, w_ada[l], b_ada[l], w_in[l], b_fgt[l], t5_table, cmp_pe[l], cmp_w1[l], cmp_b1[l],
                   cmp_w2[l], w_br_nsa[l], w_br_fox[l], w_o[l], ln1_g[l], ln1_b[l], w_rg[l], b_rg[l],
                   w_re[l], b_re[l], w_gate[l], w_up[l], w_down[l], ln2_g[l], ln2_b[l], alpha)
    return h[None]
```

```python
import functools
import math

import numpy as np
import jax
import jax.numpy as jnp
from jax import lax
from jax.experimental import pallas as pl
from jax.experimental.pallas import tpu as pltpu

F32 = jnp.float32
BF16 = jnp.bfloat16
HIGHEST = lax.Precision.HIGHEST
LOG2E = math.log2(math.e)

HEAD_DIM = 64
NSA_HEADS = 8
NSA_GROUPS = 2
NSA_HPG = NSA_HEADS // NSA_GROUPS
FOX_HEADS = 8
MIX_W = NSA_HEADS * HEAD_DIM
CMP_LEN = 32
CMP_STRIDE = 16
SLC_LEN = 64
SLC_TOPK = 16
WINDOW = 512
T5_BUCKETS = 32
T5_MAX_EXACT = 16
T5_MAX_DIST = 128
N_GROUPS = 8
EXPERTS_PER_GROUP = 8
N_EXPERTS = N_GROUPS * EXPERTS_PER_GROUP
ROW_BLOCK = 128
LN_EPS = 1e-5
NEG_INF = -1e30
M_INIT = -1e29
FORCE_SCORE = 1e4

LANES = 128
BF16_ROWS = 16
CMP_PAD = 8
ATT_TQ = 256
CMP_NEAR_ROWS = 32
FOX_TQ = 1024
FOX_TK = 512
FOX_SKIP_MARGIN = 160.0
FOX_BOUND_SLACK = 1.02
AUG_K = 128
AUG_V = HEAD_DIM + 16
ROW_MASK = HEAD_DIM
ROW_BIAS = HEAD_DIM + 16
VMEM_LIMIT = 56 * 1024 * 1024


def _cparams(sem, vmem=VMEM_LIMIT):
    return pltpu.CompilerParams(dimension_semantics=sem, vmem_limit_bytes=vmem)


def _sigmoid(x):
    return 1.0 / (1.0 + jnp.exp(-x))


def _layer_norm(x):
    mu = jnp.mean(x, axis=-1, keepdims=True)
    xc = x - mu
    var = jnp.mean(xc * xc, axis=-1, keepdims=True)
    return xc * lax.rsqrt(var + LN_EPS)


def _split3(x):
    hi = x.astype(BF16)
    r1 = x - hi.astype(F32)
    mid = r1.astype(BF16)
    lo = (r1 - mid.astype(F32)).astype(BF16)
    return hi, mid, lo


def _split3_exact(x):
    def trunc(v):
        bits = lax.bitcast_convert_type(v, jnp.uint32) & jnp.uint32(0xFFFF0000)
        return lax.bitcast_convert_type(bits, F32)
    hi = trunc(x)
    r1 = x - hi
    mid = trunc(r1)
    lo = r1 - mid
    return hi.astype(BF16), mid.astype(BF16), lo.astype(BF16)


def _dot(a, b):
    return jnp.dot(a, b, preferred_element_type=F32)


def _dot_nt(a, b):
    return lax.dot_general(a, b, (((1,), (1,)), ((), ())), preferred_element_type=F32)


def _dot3(x, w_bf16):
    hi, mid, lo = _split3(x)
    return _dot(hi, w_bf16) + _dot(mid, w_bf16) + _dot(lo, w_bf16)


def _dot3_rhs(w_bf16, x):
    hi, mid, lo = _split3(x)
    return _dot(w_bf16, hi) + _dot(w_bf16, mid) + _dot(w_bf16, lo)


def _ada_kernel(c_ref, w_ref, b_ref, o_ref):
    c = c_ref[...]
    a = c * _sigmoid(c)
    o_ref[...] = jnp.dot(a, w_ref[...], precision=HIGHEST, preferred_element_type=F32) + b_ref[...]


def _ada_mod(c, w, b):
    d, n = w.shape
    tn = 1024
    c8 = jnp.broadcast_to(c, (8, d))
    out = pl.pallas_call(
        _ada_kernel,
        grid=(n // tn,),
        in_specs=[pl.BlockSpec((8, d), lambda j: (0, 0)),
                  pl.BlockSpec((d, tn), lambda j: (0, j)),
                  pl.BlockSpec((1, tn), lambda j: (0, j))],
        out_specs=pl.BlockSpec((8, tn), lambda j: (0, j)),
        out_shape=jax.ShapeDtypeStruct((8, n), F32),
        compiler_params=_cparams(("parallel",)),
        name="ada_mod",
    )(c8, w, b.reshape(1, n))
    return out[0:1]


def _lnmod_kernel(x_ref, sc_ref, sh_ref, o_ref):
    y = _layer_norm(x_ref[...])
    o_ref[...] = (y * (1.0 + sc_ref[...]) + sh_ref[...]).astype(o_ref.dtype)


def _ln_mod(x, sc, sh, out_dtype, tm=512):
    m, d = x.shape
    return pl.pallas_call(
        _lnmod_kernel,
        grid=(m // tm,),
        in_specs=[pl.BlockSpec((tm, d), lambda i: (i, 0)),
                  pl.BlockSpec((1, d), lambda i: (0, 0)),
                  pl.BlockSpec((1, d), lambda i: (0, 0))],
        out_specs=pl.BlockSpec((tm, d), lambda i: (i, 0)),
        out_shape=jax.ShapeDtypeStruct((m, d), out_dtype),
        compiler_params=_cparams(("parallel",)),
        name="ln_mod",
    )(x, sc, sh)


def _mm_kernel(a_ref, w_ref, o_ref):
    o_ref[...] = _dot(a_ref[...], w_ref[...]).astype(o_ref.dtype)


def _matmul(a, w, out_dtype, tm, tn, name):
    m, k = a.shape
    n = w.shape[1]
    return pl.pallas_call(
        _mm_kernel,
        grid=(n // tn, m // tm),
        in_specs=[pl.BlockSpec((tm, k), lambda j, i: (i, 0)),
                  pl.BlockSpec((k, tn), lambda j, i: (0, j))],
        out_specs=pl.BlockSpec((tm, tn), lambda j, i: (i, j)),
        out_shape=jax.ShapeDtypeStruct((m, n), out_dtype),
        compiler_params=_cparams(("parallel", "parallel")),
        name=name,
    )(a, w)


def _mm_nt_kernel(w_ref, a_ref, o_ref):
    o_ref[...] = _dot_nt(w_ref[...], a_ref[...]).astype(o_ref.dtype)


def _matmul_nt(w_t, a, out_dtype, tm, name):
    n, k = w_t.shape
    m = a.shape[0]
    return pl.pallas_call(
        _mm_nt_kernel,
        grid=(m // tm,),
        in_specs=[pl.BlockSpec((n, k), lambda i: (0, 0), pipeline_mode=pl.Buffered(1)),
                  pl.BlockSpec((tm, k), lambda i: (i, 0))],
        out_specs=pl.BlockSpec((n, tm), lambda i: (0, i)),
        out_shape=jax.ShapeDtypeStruct((n, m), out_dtype),
        compiler_params=_cparams(("parallel",)),
        name=name,
    )(w_t, a)


def _gelu_tanh(x):
    return 0.5 * x * (1.0 + jnp.tanh(math.sqrt(2.0 / math.pi) * (x + 0.044715 * (x * x * x))))


def _compress_kernel(c_ref, pe_ref, w1a_ref, w1b_ref, b1_ref, w2_ref, o_ref, *, nch):
    c = c_ref[0, 0]
    w1a = w1a_ref[0]
    w1b = w1b_ref[0]
    half = CMP_STRIDE * HEAD_DIM
    a = _dot(c, w1a)
    b = _dot(c, w1b)
    b_next = pltpu.roll(b, shift=nch - 1, axis=0)
    pe = pe_ref[0]
    pb = _dot(pe[:, :half], w1a) + _dot(pe[:, half:], w1b)
    hid = _gelu_tanh(a + b_next + pb[0:1, :] + b1_ref[0])
    o_ref[0, 0] = _dot(hid.astype(BF16), w2_ref[0])


def _compress(kv_cmp, pe, w1, b1, w2):
    z, g, s, hd = kv_cmp.shape
    nch = s // CMP_STRIDE
    half = CMP_STRIDE * hd
    chunks = kv_cmp.reshape(z, g, nch, half)
    pe8 = jnp.broadcast_to(pe.reshape(z, 1, CMP_LEN * hd), (z, 8, CMP_LEN * hd)).astype(BF16)
    w1b16 = w1.astype(BF16)
    hidn = w1.shape[-1]
    return pl.pallas_call(
        functools.partial(_compress_kernel, nch=nch),
        grid=(z, g),
        in_specs=[pl.BlockSpec((1, 1, nch, half), lambda zi, gi: (zi, gi, 0, 0)),
                  pl.BlockSpec((1, 8, 2 * half), lambda zi, gi: (zi, 0, 0)),
                  pl.BlockSpec((1, half, hidn), lambda zi, gi: (zi, 0, 0)),
                  pl.BlockSpec((1, half, hidn), lambda zi, gi: (zi, 1, 0)),
                  pl.BlockSpec((1, 1, hidn), lambda zi, gi: (zi, 0, 0)),
                  pl.BlockSpec((1, hidn, hd), lambda zi, gi: (zi, 0, 0))],
        out_specs=pl.BlockSpec((1, 1, nch, hd), lambda zi, gi: (zi, gi, 0, 0)),
        out_shape=jax.ShapeDtypeStruct((z, g, nch, hd), F32),
        compiler_params=_cparams(("parallel", "parallel")),
        name="compress_kv",
    )(chunks, pe8, w1b16, w1b16, b1.reshape(z, 1, hidn), w2.astype(BF16))


def _t5_bucket_np(dist):
    n = np.maximum(dist, 0)
    ratio = np.log(np.maximum(n, T5_MAX_EXACT).astype(np.float64) / T5_MAX_EXACT)
    big = T5_MAX_EXACT + (ratio / math.log(T5_MAX_DIST / T5_MAX_EXACT)
                          * (T5_BUCKETS - T5_MAX_EXACT)).astype(np.int64)
    return np.where(n < T5_MAX_EXACT, n, np.minimum(big, T5_BUCKETS - 1)).astype(np.int32)


def _t5_lookup(tbh, dist):
    onehot = np.eye(T5_BUCKETS, dtype=np.float32)[_t5_bucket_np(dist).reshape(-1)]
    vals = jnp.einsum('ghb,nb->ghn', tbh, jnp.asarray(onehot), precision=HIGHEST)
    return vals.reshape(tbh.shape[:2] + dist.shape)


def _toeplitz_kernel(w_ref, o_ref, *, n_keys, tq, lead_zero, trail_masked):
    width = w_ref.shape[-1]
    rows = jnp.broadcast_to(w_ref[0, 0], (n_keys, width))
    rolled = pltpu.roll(rows, shift=width - (n_keys - 1), axis=1, stride=1, stride_axis=0)
    parts = [rolled[:, 0:tq]]
    if lead_zero:
        parts.insert(0, jnp.zeros((lead_zero, tq), F32))
    if trail_masked:
        parts.append(jnp.full((trail_masked, tq), NEG_INF, F32))
    o_ref[0] = jnp.concatenate(parts, axis=0)


def _att_table_t(tbh, n_keys, tq, lo, hi, minus_far, lead_zero=0, trail_masked=0):
    length = n_keys + tq - 1
    d = np.arange(length) - (tq - 1)
    valid = (d >= lo) & (d < hi)
    vals = _t5_lookup(tbh, d)
    if minus_far:
        vals = vals - tbh[:, :, T5_BUCKETS - 1:]
    w = jnp.where(jnp.asarray(valid), vals * LOG2E, NEG_INF)
    width = pl.next_power_of_2(length)
    w = jnp.pad(w, ((0, 0), (0, 0), (0, width - length)))[:, :, None, :]
    g, hpg = tbh.shape[:2]
    n_rows = lead_zero + n_keys + trail_masked
    return pl.pallas_call(
        functools.partial(_toeplitz_kernel, n_keys=n_keys, tq=tq, lead_zero=lead_zero, trail_masked=trail_masked),
        grid=(g, hpg),
        in_specs=[pl.BlockSpec((1, 1, 1, width), lambda gi, hi: (gi, hi, 0, 0))],
        out_specs=pl.BlockSpec((1, n_rows, tq), lambda gi, hi: (gi, 0, hi)),
        out_shape=jax.ShapeDtypeStruct((g, n_rows, hpg * tq), F32),
        compiler_params=_cparams(("parallel", "parallel")),
        name="t5_toeplitz",
    )(w)


def _nsa_bias_tables(t5_table):
    tbh = t5_table.T.reshape(NSA_GROUPS, NSA_HPG, T5_BUCKETS).astype(F32)
    j = np.arange(CMP_NEAR_ROWS)[:, None]
    i = np.arange(ATT_TQ)[None, :]
    dist = i - (CMP_LEN - 1) - CMP_STRIDE * (j - CMP_PAD)
    vals = (_t5_lookup(tbh, dist) - tbh[:, :, T5_BUCKETS - 1][:, :, None, None]) * LOG2E
    vals = jnp.where(jnp.asarray(dist >= 0)[None, None], vals, NEG_INF)
    b_cmp = vals.transpose(0, 2, 1, 3).reshape(NSA_GROUPS, CMP_NEAR_ROWS, NSA_HPG * ATT_TQ)
    far = tbh[:, :, T5_BUCKETS - 1] * LOG2E
    b_sel = _att_table_t(tbh, 2 * ATT_TQ, ATT_TQ, 0, 1 << 30, True, lead_zero=ATT_TQ)
    b_win = _att_table_t(tbh, 3 * ATT_TQ, ATT_TQ, 0, WINDOW, False, trail_masked=ATT_TQ)
    hi, mid, lo = _split3_exact(jnp.repeat(far, ATT_TQ, axis=1))
    far3 = jnp.stack([hi, mid, lo, jnp.full(hi.shape, NEG_INF, BF16)], axis=1)
    far3 = jnp.pad(far3, ((0, 0), (0, BF16_ROWS - 4), (0, 0)))
    return b_cmp, b_sel, b_win, far3


def _nsa_cmp_kernel(qt_ref, kc_ref, vct_ref, near_ref, far3_ref, oct_ref, selt_ref, rhs, s_scr, imp_scr, *, nslc):
    tq = ATT_TQ
    lanes = NSA_HPG * tq
    qb = pl.program_id(1)
    cpb = tq // CMP_STRIDE
    rhs[...] = jnp.zeros(rhs.shape, BF16)
    rhs[0:HEAD_DIM, :] = qt_ref[0, 0]
    rhs[ROW_BIAS:ROW_BIAS + BF16_ROWS, :] = far3_ref[0]
    chunk = lax.broadcasted_iota(jnp.int32, (AUG_K, lanes), 0)
    rhs[AUG_K:2 * AUG_K, :] = jnp.where(chunk >= qb + CMP_NEAR_ROWS // BF16_ROWS, NEG_INF, 0.0).astype(BF16)
    r0 = pl.multiple_of(cpb * qb, BF16_ROWS)
    n_lane_blocks = tq // LANES
    ncp = s_scr.shape[0]
    imp_scr[...] = jnp.zeros(imp_scr.shape, F32)

    def attend(rows):
        s_scr[0:rows, :] = _dot(kc_ref[0, 0:rows, :], rhs[...])
        s_scr[pl.ds(r0, CMP_NEAR_ROWS), :] = s_scr[pl.ds(r0, CMP_NEAR_ROWS), :] + near_ref[0]
        s = s_scr[0:rows, :]
        m = jnp.max(s, axis=0, keepdims=True)
        e = jnp.exp2(s - m)
        l = jnp.sum(e, axis=0, keepdims=True)
        p = e * jnp.where(m > M_INIT, 1.0 / l, 0.0)
        oct_ref[0, 0] = _dot(vct_ref[0, :, 0:rows], p.astype(BF16))
        imp = p[:, 0:tq]
        for h in range(1, NSA_HPG):
            imp = imp + p[:, h * tq:(h + 1) * tq]
        for c in range(n_lane_blocks):
            imp_scr[c, 0:rows, :] = imp[:, c * LANES:(c + 1) * LANES]

    limits = sorted({min(ncp, -(-(ncp * k // 3) // LANES) * LANES) for k in (1, 2, 3)})
    lo_qb = 0
    for rows in limits:
        hi_qb = (rows - CMP_NEAR_ROWS) // cpb if rows < ncp else pl.num_programs(1) - 1

        @pl.when((qb >= lo_qb) & (qb <= hi_qb))
        def _(rows=rows):
            attend(rows)

        lo_qb = hi_qb + 1
    ratio = SLC_LEN // CMP_STRIDE

    def taps(off):
        return jnp.concatenate([imp_scr[c, pl.ds(CMP_PAD + off, nslc, stride=ratio), :]
                                for c in range(n_lane_blocks)], axis=1)

    p_slc = 0.5 * (taps(-1) + taps(ratio - 1))
    for off in range(ratio - 1):
        p_slc = p_slc + taps(off)
    blk = lax.broadcasted_iota(jnp.int32, (nslc, tq), 0)
    cur = (qb * tq + lax.broadcasted_iota(jnp.int32, (nslc, tq), 1)) // SLC_LEN
    forced = (blk == 0) | (blk == cur) | (blk == cur - 1)
    score = jnp.where(forced, FORCE_SCORE, jnp.where(blk <= cur, p_slc, -1.0))
    blk_f = blk.astype(F32)
    sel = jnp.zeros((nslc, tq), F32)
    for _ in range(min(SLC_TOPK, nslc)):
        mx = jnp.max(score, axis=0, keepdims=True)
        first = jnp.min(jnp.where(score == mx, blk_f, float(nslc)), axis=0, keepdims=True)
        hit = blk_f == first
        sel = jnp.where(hit, 1.0, sel)
        score = jnp.where(hit, -2.0, score)
    selt_ref[0] = sel.astype(BF16)


def _nsa_compress_select(qt, kc_aug, vct, b_cmp, far3):
    g, nq, hd, lanes = qt.shape
    ncp = kc_aug.shape[1]
    tq = ATT_TQ
    nslc = nq * tq // SLC_LEN
    assert ncp // BF16_ROWS <= AUG_K
    assert tq // CMP_STRIDE == BF16_ROWS
    per_q = lambda gi, qi: (gi, qi, 0, 0)
    per_g = lambda gi, qi: (gi, 0, 0)
    return pl.pallas_call(
        functools.partial(_nsa_cmp_kernel, nslc=nslc),
        grid=(g, nq),
        in_specs=[pl.BlockSpec((1, 1, hd, lanes), per_q),
                  pl.BlockSpec((1, ncp, 2 * AUG_K), per_g),
                  pl.BlockSpec((1, hd, ncp), per_g),
                  pl.BlockSpec((1, CMP_NEAR_ROWS, lanes), per_g),
                  pl.BlockSpec((1, BF16_ROWS, lanes), per_g)],
        out_specs=[pl.BlockSpec((1, 1, hd, lanes), per_q),
                   pl.BlockSpec((1, nslc, tq), lambda gi, qi: (gi, 0, qi))],
        out_shape=[jax.ShapeDtypeStruct((g, nq, hd, lanes), F32),
                   jax.ShapeDtypeStruct((g, nslc, nq * tq), BF16)],
        scratch_shapes=[pltpu.VMEM((2 * AUG_K, lanes), BF16), pltpu.VMEM((ncp, lanes), F32),
                        pltpu.VMEM((tq // LANES, ncp, LANES), F32)],
        compiler_params=_cparams(("parallel", "parallel")),
        name="nsa_compress_select",
    )(qt, kc_aug, vct, b_cmp, far3)


def _flash_init_t(m_ref, acc_ref):
    m_ref[...] = jnp.full(m_ref.shape, M_INIT, F32)
    acc_ref[...] = jnp.zeros(acc_ref.shape, F32)


def _flash_step_t(s, vt_tile, m_ref, acc_ref):
    m_old = m_ref[...]
    m_new = jnp.maximum(m_old, jnp.max(s, axis=0, keepdims=True))
    p = jnp.exp2(s - m_new).astype(BF16)
    acc_ref[...] = jnp.exp2(m_old - m_new) * acc_ref[...] + _dot(vt_tile, p)
    m_ref[...] = m_new


def _flash_result_t(acc_ref):
    acc = acc_ref[...]
    return acc[0:HEAD_DIM, :] / acc[HEAD_DIM:HEAD_DIM + 1, :]


def _nsa_att_kernel(qt_ref, ks_ref, kx_ref, vst_ref, kw_ref, vwt_ref, selt_ref, far3_ref, bsel_ref, bwin_ref,
                    oct_ref, gate_ref, o_ref, rhs_s, rhs_w, mask_t, ms, accs, mw, accw, s_even, s_odd):
    tq = ATT_TQ
    qb = pl.program_id(1)
    ones_rows = _ones_rows(tq)
    kx_tiles = kx_ref.shape[0] // tq
    qt = qt_ref[0, 0]
    rhs_s[...] = jnp.zeros(rhs_s.shape, BF16)
    rhs_s[0:HEAD_DIM, :] = qt
    rhs_s[ROW_BIAS:ROW_BIAS + BF16_ROWS, :] = far3_ref[0]
    rhs_w[...] = jnp.zeros(rhs_w.shape, BF16)
    rhs_w[0:HEAD_DIM, :] = qt
    madd = ((selt_ref[0].astype(F32) - 1.0) * (-NEG_INF)).astype(BF16)
    mask_t[...] = jnp.concatenate([madd] * NSA_HPG, axis=1)
    _flash_init_t(ms, accs)
    _flash_init_t(mw, accw)
    blocks_per_tile = tq // SLC_LEN

    def sel_scores(kt, s_ref):
        kt = jnp.minimum(kt, qb)
        k0 = pl.multiple_of(kt * tq, tq)
        chunk = pl.multiple_of((kt * blocks_per_tile) // BF16_ROWS * BF16_ROWS, BF16_ROWS)
        rhs_s[ROW_MASK:ROW_MASK + BF16_ROWS, :] = mask_t[pl.ds(chunk, BF16_ROWS), :]
        rel = jnp.clip(kt - qb + 2, 0, 2)
        table = bsel_ref[0, pl.ds(pl.multiple_of(rel * tq, tq), tq), :]
        k_aug = ks_ref[pl.ds(k0, tq), :] + kx_ref[pl.ds(pl.multiple_of((kt % kx_tiles) * tq, tq), tq), :]
        s_ref[...] = _dot(k_aug, rhs_s[...]) + table

    def sel_consume(kt, s_ref):
        k0 = pl.multiple_of(kt * tq, tq)
        vt_aug = jnp.concatenate([vst_ref[0, :, pl.ds(k0, tq)], ones_rows], axis=0)
        _flash_step_t(s_ref[...], vt_aug, ms, accs)

    def win_scores(j, s_ref):
        kt = qb - 2 + j
        k0 = pl.multiple_of(jnp.maximum(kt, 0) * tq, tq)
        row = pl.multiple_of(jnp.where(kt >= 0, j, 3) * tq, tq)
        s_ref[...] = _dot(kw_ref[pl.ds(k0, tq), :], rhs_w[...]) + bwin_ref[0, pl.ds(row, tq), :]

    def win_consume(j, s_ref):
        k0 = pl.multiple_of(jnp.maximum(qb - 2 + j, 0) * tq, tq)
        vt_aug = jnp.concatenate([vwt_ref[0, :, pl.ds(k0, tq)], ones_rows], axis=0)
        _flash_step_t(s_ref[...], vt_aug, mw, accw)

    n_sel = qb + 1
    sel_scores(0, s_even)

    def pair_body(j, carry):
        sel_scores(2 * j + 1, s_odd)
        sel_consume(2 * j, s_even)
        sel_scores(2 * j + 2, s_even)
        sel_consume(2 * j + 1, s_odd)
        return carry

    lax.fori_loop(0, n_sel // 2, pair_body, 0)
    win_scores(0, s_odd)

    @pl.when(n_sel % 2 == 1)
    def _():
        sel_consume(qb, s_even)

    win_scores(1, s_even)
    win_consume(0, s_odd)
    win_scores(2, s_odd)
    win_consume(1, s_even)
    win_consume(2, s_odd)

    gt = _sigmoid(gate_ref[0, 0])
    out = gt[0:1, :] * oct_ref[0, 0] + gt[1:2, :] * _flash_result_t(accs) + gt[2:3, :] * _flash_result_t(accw)
    for h in range(NSA_HPG):
        o_ref[0, h] = out[:, h * tq:(h + 1) * tq].astype(o_ref.dtype)


def _nsa_attention(qt, k_tok, ks_block0, kw_block0, k_extra, ch, vs_block0, vw_block0, selt, far3, b_sel, b_win,
                   oct_, gates_t):
    g, nq, hd, lanes = qt.shape
    s = k_tok.shape[0]
    nslc = selt.shape[1]
    tq = ATT_TQ
    resident = pl.Buffered(1)
    per_q = lambda gi, qi: (gi, qi, 0, 0)
    per_g = lambda gi, qi: (gi, 0, 0)
    return pl.pallas_call(
        _nsa_att_kernel,
        grid=(g, nq),
        in_specs=[pl.BlockSpec((1, 1, hd, lanes), per_q),
                  pl.BlockSpec((s, AUG_K), lambda gi, qi: (0, ks_block0 + gi), pipeline_mode=resident),
                  pl.BlockSpec(k_extra.shape, lambda gi, qi: (0, 0), pipeline_mode=resident),
                  pl.BlockSpec((1, hd, s), lambda gi, qi: (vs_block0 + gi, 0, 0), pipeline_mode=resident),
                  pl.BlockSpec((s, AUG_K), lambda gi, qi: (0, kw_block0 + gi), pipeline_mode=resident),
                  pl.BlockSpec((1, hd, s), lambda gi, qi: (vw_block0 + gi, 0, 0), pipeline_mode=resident),
                  pl.BlockSpec((1, nslc, tq), lambda gi, qi: (gi, 0, qi)),
                  pl.BlockSpec((1, BF16_ROWS, lanes), per_g),
                  pl.BlockSpec((1, 3 * tq, lanes), per_g, pipeline_mode=resident),
                  pl.BlockSpec((1, 4 * tq, lanes), per_g, pipeline_mode=resident),
                  pl.BlockSpec((1, 1, hd, lanes), per_q),
                  pl.BlockSpec((1, 1, 8, lanes), per_q)],
        out_specs=pl.BlockSpec((1, lanes // tq, hd, tq), lambda gi, qi: (gi, 0, 0, qi)),
        out_shape=jax.ShapeDtypeStruct((g, lanes // tq, hd, s), BF16),
        scratch_shapes=[pltpu.VMEM((AUG_K, lanes), BF16), pltpu.VMEM((AUG_K, lanes), BF16),
                        pltpu.VMEM((nslc, lanes), BF16),
                        pltpu.VMEM((1, lanes), F32), pltpu.VMEM((AUG_V, lanes), F32),
                        pltpu.VMEM((1, lanes), F32), pltpu.VMEM((AUG_V, lanes), F32),
                        pltpu.VMEM((tq, lanes), F32), pltpu.VMEM((tq, lanes), F32)],
        compiler_params=_cparams(("arbitrary", "arbitrary")),
        name="nsa_select_window",
    )(qt, k_tok, k_extra, ch, k_tok, ch, selt, far3, b_sel, b_win, oct_, gates_t)


def _decay_kernel(z_ref, b_ref, place_ref, o_ref, end_ref, carry_ref, *, tb):
    @pl.when(pl.program_id(0) == 0)
    def _():
        carry_ref[...] = jnp.zeros(carry_ref.shape, F32)

    z = z_ref[...] + b_ref[...]
    log_f = jnp.minimum(z, 0.0) - jnp.log1p(jnp.exp(-jnp.abs(z)))
    r = lax.broadcasted_iota(jnp.int32, (tb, tb), 0)
    c = lax.broadcasted_iota(jnp.int32, (tb, tb), 1)
    tri = jnp.where(r >= c, 1.0, 0.0).astype(BF16)
    run = _dot3_rhs(tri, log_f) + carry_ref[...]
    carry_ref[...] = run[tb - 1:tb, :]
    val = -run * LOG2E
    hi, mid, lo = _split3_exact(val)
    o_ref[...] = (_dot(hi, place_ref[0]) + _dot(mid, place_ref[1]) + _dot(lo, place_ref[2])).astype(BF16)
    end_ref[...] = jnp.broadcast_to(val[tb - 1:tb, :], end_ref.shape)


def _decay_pieces(z, bias, first_lane, n_heads):
    s, n = z.shape
    tb = FOX_TK
    place = np.zeros((3, n, n_heads * AUG_K), np.float32)
    for h in range(n_heads):
        for piece in range(3):
            place[piece, first_lane + h, h * AUG_K + HEAD_DIM + piece] = 1.0
    pieces, ends = pl.pallas_call(
        functools.partial(_decay_kernel, tb=tb),
        grid=(s // tb,),
        in_specs=[pl.BlockSpec((tb, n), lambda i: (i, 0)),
                  pl.BlockSpec((1, n), lambda i: (0, 0)),
                  pl.BlockSpec((3, n, n_heads * AUG_K), lambda i: (0, 0, 0))],
        out_specs=[pl.BlockSpec((tb, n_heads * AUG_K), lambda i: (i, 0)),
                   pl.BlockSpec((8, n), lambda i: (i, 0))],
        out_shape=[jax.ShapeDtypeStruct((s, n_heads * AUG_K), BF16),
                   jax.ShapeDtypeStruct((s // tb * 8, n), F32)],
        scratch_shapes=[pltpu.VMEM((1, n), F32)],
        compiler_params=_cparams(("arbitrary",)),
        name="decay_cumsum",
    )(z, bias, jnp.asarray(place, BF16))
    return pieces, ends[::8, first_lane:first_lane + n_heads].T


def _ones_rows(width):
    return jnp.where(lax.broadcasted_iota(jnp.int32, (BF16_ROWS, width), 0) < 8, 1.0, 0.0).astype(BF16)


def _fox_kernel(ends_ref, qt_ref, k_ref, dk_ref, vt_ref, o_ref, rhs, m_ref, acc_ref, s_even, s_odd, kmax_ref):
    tq = FOX_TQ
    tk = FOX_TK
    assert tq == 2 * tk
    qb = pl.program_id(1)
    row = lax.broadcasted_iota(jnp.int32, (AUG_K - HEAD_DIM, tq), 0)
    rhs[0:HEAD_DIM, :] = qt_ref[0]
    rhs[HEAD_DIM:AUG_K, :] = jnp.where(row < 3, 1.0, 0.0).astype(BF16)
    ones_rows = _ones_rows(tk)
    _flash_init_t(m_ref, acc_ref)

    def scores(kt, s_ref):
        k0 = pl.multiple_of(kt * tk, tk)
        k_aug = k_ref[pl.ds(k0, tk), :] + dk_ref[pl.ds(k0, tk), :]
        s_ref[...] = _dot(k_aug, rhs[...])

    def consume(kt, s_ref, diagonal):
        k0 = pl.multiple_of(kt * tk, tk)
        s = s_ref[...]
        if diagonal:
            key = k0 + lax.broadcasted_iota(jnp.int32, (tk, tq), 0)
            qry = qb * tq + lax.broadcasted_iota(jnp.int32, (tk, tq), 1)
            s = jnp.where(key <= qry, s, NEG_INF)
        vt_aug = jnp.concatenate([vt_ref[0, :, pl.ds(k0, tk)], ones_rows], axis=0)
        _flash_step_t(s, vt_aug, m_ref, acc_ref)

    @pl.when(qb == 0)
    def _():
        ones = jnp.ones((AUG_K, AUG_K), BF16)

        def norm_tile(c, best):
            k = k_ref[pl.ds(pl.multiple_of(c * tk, tk), tk), :].astype(F32)
            return jnp.maximum(best, _dot((k * k).astype(BF16), ones))

        best = lax.fori_loop(0, k_ref.shape[0] // tk, norm_tile, jnp.zeros((tk, AUG_K), F32))
        kmax_ref[0] = jnp.max(jnp.sqrt(best * FOX_BOUND_SLACK))

    scores(2 * qb, s_even)
    scores(2 * qb + 1, s_odd)
    consume(2 * qb, s_even, True)
    consume(2 * qb + 1, s_odd, True)

    q = qt_ref[0].astype(F32)
    q_norm = jnp.max(jnp.sqrt(jnp.sum(q * q, axis=0, keepdims=True) * FOX_BOUND_SLACK))
    threshold = jnp.min(m_ref[...]) - FOX_SKIP_MARGIN - q_norm * kmax_ref[0]
    head = pl.program_id(0)

    def first_needed(j, first):
        return jnp.where(ends_ref[head, 2 * j + 1] >= threshold, jnp.minimum(first, j), first)

    j0 = lax.fori_loop(0, qb, first_needed, qb)

    scores(2 * j0, s_even)

    def pair_body(j, carry):
        scores(2 * j + 1, s_odd)
        consume(2 * j, s_even, False)
        scores(2 * j + 2, s_even)
        consume(2 * j + 1, s_odd, False)
        return carry

    lax.fori_loop(j0, qb, pair_body, 0)
    o_ref[0] = _flash_result_t(acc_ref).astype(o_ref.dtype)


def _fox_attention(ch, q_block0, v_block0, k_tok, k_block0, decay_k, decay_ends):
    _, hd, s = ch.shape
    h = decay_ends.shape[0]
    tq = FOX_TQ
    resident = pl.Buffered(1)
    grid_spec = pltpu.PrefetchScalarGridSpec(
        num_scalar_prefetch=1,
        grid=(h, s // tq),
        in_specs=[pl.BlockSpec((1, hd, tq), lambda hi, qi, ends: (q_block0 + hi, 0, qi)),
                  pl.BlockSpec((s, AUG_K), lambda hi, qi, ends: (0, k_block0 + hi), pipeline_mode=resident),
                  pl.BlockSpec((s, AUG_K), lambda hi, qi, ends: (0, hi), pipeline_mode=resident),
                  pl.BlockSpec((1, hd, s), lambda hi, qi, ends: (v_block0 + hi, 0, 0), pipeline_mode=resident)],
        out_specs=pl.BlockSpec((1, hd, tq), lambda hi, qi, ends: (hi, 0, qi)),
        scratch_shapes=[pltpu.VMEM((AUG_K, tq), BF16),
                        pltpu.VMEM((1, tq), F32), pltpu.VMEM((AUG_V, tq), F32),
                        pltpu.VMEM((FOX_TK, tq), F32), pltpu.VMEM((FOX_TK, tq), F32),
                        pltpu.SMEM((1,), F32)],
    )
    return pl.pallas_call(
        _fox_kernel,
        grid_spec=grid_spec,
        out_shape=jax.ShapeDtypeStruct((h, hd, s), BF16),
        compiler_params=_cparams(("arbitrary", "arbitrary")),
        name="fox_attention",
    )(decay_ends, ch, k_tok, decay_k, ch)


def _merge_kernel(on_ref, of_ref, mg_ref, x_ref, wn_ref, wf_ref, wo_ref, g1_ref, lg_ref, lb_ref, o_ref, *, alpha):
    d = x_ref.shape[-1]
    tn = (((0,), (0,)), ((), ()))
    a = lax.dot_general(on_ref[...], wn_ref[...], tn, preferred_element_type=F32)
    b = lax.dot_general(of_ref[...], wf_ref[...], tn, preferred_element_type=F32)
    gm = _sigmoid(mg_ref[...].astype(F32))
    merged = gm[:, 0:d] * a + gm[:, d:2 * d] * b
    y = _dot(merged.astype(BF16), wo_ref[...])
    z = alpha * x_ref[...] + (1.0 + g1_ref[...]) * y
    o_ref[...] = _layer_norm(z) * lg_ref[...] + lb_ref[...]


def _merge_project(o_nsa, o_fox, merge, x, wn, wf, wo, g1, ln_g, ln_b, alpha, tm=256):
    m, d = x.shape
    w = o_nsa.shape[0]
    resident = pl.Buffered(1)
    row = lambda i: (i, 0)
    fixed = lambda i: (0, 0)
    return pl.pallas_call(
        functools.partial(_merge_kernel, alpha=alpha),
        grid=(m // tm,),
        in_specs=[pl.BlockSpec((w, tm), lambda i: (0, i)), pl.BlockSpec((w, tm), lambda i: (0, i)),
                  pl.BlockSpec((tm, 2 * d), row), pl.BlockSpec((tm, d), row),
                  pl.BlockSpec((w, d), fixed, pipeline_mode=resident),
                  pl.BlockSpec((w, d), fixed, pipeline_mode=resident),
                  pl.BlockSpec((d, d), fixed, pipeline_mode=resident),
                  pl.BlockSpec((1, d), fixed), pl.BlockSpec((1, d), fixed), pl.BlockSpec((1, d), fixed)],
        out_specs=pl.BlockSpec((tm, d), row),
        out_shape=jax.ShapeDtypeStruct((m, d), F32),
        compiler_params=_cparams(("parallel",)),
        name="merge_project_ln",
    )(o_nsa, o_fox, merge, x, wn, wf, wo, g1, ln_g, ln_b)


def _router_kernel(x_ref, sc_ref, sh_ref, w_ref, b_ref, u_ref, r_ref):
    u = _layer_norm(x_ref[...]) * (1.0 + sc_ref[...]) + sh_ref[...]
    u_ref[...] = u
    u_hi = u.astype(BF16)
    u_lo = (u - u_hi.astype(F32)).astype(BF16)
    logits = _dot(u_hi, w_ref[0]) + _dot(u_lo, w_ref[0]) + _dot(u_hi, w_ref[1]) + b_ref[...]
    lane = lax.broadcasted_iota(jnp.int32, (1, LANES), 1).astype(F32)
    none = float(LANES)
    is_g = lane < N_GROUPS
    lg = jnp.where(is_g, logits, NEG_INF)
    eg = jnp.exp(lg - jnp.max(lg, axis=-1, keepdims=True))
    pg = eg / jnp.sum(eg, axis=-1, keepdims=True)
    p_grp = jnp.max(pg, axis=-1, keepdims=True)
    grp = jnp.min(jnp.where(pg == p_grp, lane, none), axis=-1, keepdims=True)
    lo = N_GROUPS + grp * EXPERTS_PER_GROUP
    is_e = (lane >= lo) & (lane < lo + EXPERTS_PER_GROUP)
    le = jnp.where(is_e, logits, NEG_INF)
    ee = jnp.exp(le - jnp.max(le, axis=-1, keepdims=True))
    pe = jnp.where(is_e, ee / jnp.sum(ee, axis=-1, keepdims=True), -1.0)
    p1 = jnp.max(pe, axis=-1, keepdims=True)
    i1 = jnp.min(jnp.where(pe == p1, lane, none), axis=-1, keepdims=True)
    pe2 = jnp.where(lane == i1, -1.0, pe)
    p2 = jnp.max(pe2, axis=-1, keepdims=True)
    i2 = jnp.min(jnp.where(pe2 == p2, lane, none), axis=-1, keepdims=True)
    den = p1 + p2
    r_ref[...] = jnp.where(lane == 0, i1 - N_GROUPS,
                           jnp.where(lane == 1, i2 - N_GROUPS,
                                     jnp.where(lane == 2, p_grp * p1 / den,
                                               jnp.where(lane == 3, p_grp * p2 / den, 0.0))))


def _router(x1, sc, sh, w_r, b_r, tm=256):
    m, d = x1.shape
    row = lambda i: (i, 0)
    fixed = lambda i: (0, 0)
    return pl.pallas_call(
        _router_kernel,
        grid=(m // tm,),
        in_specs=[pl.BlockSpec((tm, d), row), pl.BlockSpec((1, d), fixed), pl.BlockSpec((1, d), fixed),
                  pl.BlockSpec((2, d, LANES), lambda i: (0, 0, 0)), pl.BlockSpec((1, LANES), fixed)],
        out_specs=[pl.BlockSpec((tm, d), row), pl.BlockSpec((tm, LANES), row)],
        out_shape=[jax.ShapeDtypeStruct((m, d), F32), jax.ShapeDtypeStruct((m, LANES), F32)],
        compiler_params=_cparams(("parallel",)),
        name="moe_router",
    )(x1, sc, sh, w_r, b_r)


def _moe_kernel(be_ref, nu_ref, tok_ref, tok_next_ref, dst_ref, rw_ref, u_hbm, wg_ref, wu_ref, wd_ref, out_hbm,
                xbuf, ybuf, wgb, wub, wdb, sem_in, sem_out, *, n_dump0):
    rb = ROW_BLOCK
    i = pl.program_id(0)
    last = nu_ref[0] - 1
    slot = i % 2

    def row_in(r, tok, sl):
        return pltpu.make_async_copy(u_hbm.at[pl.ds(tok, 1), :], xbuf.at[sl, pl.ds(r, 1), :], sem_in.at[sl])

    def row_out(r, dst):
        return pltpu.make_async_copy(ybuf.at[pl.ds(r, 1), :], out_hbm.at[pl.ds(dst, 1), :], sem_out)

    @pl.when(i == 0)
    def _():
        for r in range(rb):
            row_in(r, tok_ref[0, 0, r], 0).start()
        ybuf[...] = jnp.zeros(ybuf.shape, F32)
        pltpu.make_async_copy(ybuf, out_hbm.at[pl.ds(n_dump0, rb), :], sem_out).start()

    prev = be_ref[jnp.maximum(i - 1, 0)]

    @pl.when((i <= last) & ((i == 0) | (be_ref[i] != prev)))
    def _():
        wgb[...] = wg_ref[0].astype(BF16)
        wub[...] = wu_ref[0].astype(BF16)
        wdb[...] = wd_ref[0].astype(BF16)

    @pl.when(i <= last)
    def _():
        for r in range(rb):
            row_in(r, 0, slot).wait()
        xb = xbuf[slot].astype(BF16)
        for r in range(rb):
            row_in(r, tok_next_ref[0, 0, r], 1 - slot).start()
        gate = _dot(xb, wgb[...])
        up = _dot(xb, wub[...])
        hid = (gate * _sigmoid(gate)) * up
        y = _dot(hid.astype(BF16), wdb[...]) * rw_ref[0]
        for r in range(rb):
            row_out(r, 0).wait()
        ybuf[...] = y
        for r in range(rb):
            row_out(r, dst_ref[0, 0, r]).start(priority=r % 2)

    @pl.when(i == last)
    def _():
        for r in range(rb):
            row_out(r, 0).wait()
        for r in range(rb):
            row_in(r, 0, 1 - slot).wait()


def _moe_experts(u, blk_exp, n_used, row_tok, row_dst, row_w, w_gate, w_up, w_down):
    t, d = u.shape
    n_blocks = blk_exp.shape[0]
    de = w_gate.shape[-1]
    rb = ROW_BLOCK
    tok3 = row_tok.reshape(n_blocks, 1, rb)
    grid_spec = pltpu.PrefetchScalarGridSpec(
        num_scalar_prefetch=2,
        grid=(n_blocks,),
        in_specs=[pl.BlockSpec((1, 1, rb), lambda i, be, nu: (i, 0, 0), memory_space=pltpu.SMEM),
                  pl.BlockSpec((1, 1, rb), lambda i, be, nu: (jnp.minimum(i + 1, n_blocks - 1), 0, 0),
                               memory_space=pltpu.SMEM),
                  pl.BlockSpec((1, 1, rb), lambda i, be, nu: (i, 0, 0), memory_space=pltpu.SMEM),
                  pl.BlockSpec((1, rb, 1), lambda i, be, nu: (i, 0, 0)),
                  pl.BlockSpec(memory_space=pl.ANY),
                  pl.BlockSpec((1, d, de), lambda i, be, nu: (be[i], 0, 0)),
                  pl.BlockSpec((1, d, de), lambda i, be, nu: (be[i], 0, 0)),
                  pl.BlockSpec((1, de, d), lambda i, be, nu: (be[i], 0, 0))],
        out_specs=pl.BlockSpec(memory_space=pl.ANY),
        scratch_shapes=[pltpu.VMEM((2, rb, d), F32), pltpu.VMEM((rb, d), F32),
                        pltpu.VMEM((d, de), BF16), pltpu.VMEM((d, de), BF16), pltpu.VMEM((de, d), BF16),
                        pltpu.SemaphoreType.DMA((2,)), pltpu.SemaphoreType.DMA(())],
    )
    return pl.pallas_call(
        functools.partial(_moe_kernel, n_dump0=2 * t),
        grid_spec=grid_spec,
        out_shape=jax.ShapeDtypeStruct((2 * t + rb, d), F32),
        compiler_params=_cparams(("arbitrary",)),
        name="moe_experts",
    )(blk_exp, n_used, tok3, tok3, row_dst.reshape(n_blocks, 1, rb), row_w.reshape(n_blocks, rb, 1),
      u, w_gate, w_up, w_down)


def _moe_dispatch(route, t):
    k = 2
    eid = route[:, 0:k].astype(jnp.int32).reshape(-1)
    wts = route[:, k:2 * k].reshape(-1)
    n_asg = t * k
    n_rows = n_asg + N_EXPERTS * ROW_BLOCK
    n_blocks = n_rows // ROW_BLOCK
    onehot = (eid[:, None] == jnp.arange(N_EXPERTS, dtype=jnp.int32)[None, :]).astype(jnp.int32)
    rank = jnp.sum((jnp.cumsum(onehot, axis=0) - onehot) * onehot, axis=1)
    counts = jnp.sum(onehot, axis=0)
    padded = (counts + ROW_BLOCK - 1) // ROW_BLOCK * ROW_BLOCK
    pad_end = jnp.cumsum(padded)
    pad_start = pad_end - padded
    dest = jnp.sum(onehot * pad_start[None, :], axis=1) + rank
    asg = jnp.arange(n_asg, dtype=jnp.int32)
    upd = jnp.stack([(asg % k) * t + asg // k, lax.bitcast_convert_type(wts, jnp.int32)], axis=1)
    init = jnp.stack([n_asg + jnp.arange(n_rows, dtype=jnp.int32) % ROW_BLOCK,
                      jnp.zeros((n_rows,), jnp.int32)], axis=1)
    rows = init.at[dest].set(upd)
    row_dst = rows[:, 0]
    row_w = lax.bitcast_convert_type(rows[:, 1], F32)
    row_tok = jnp.where(row_dst < n_asg, row_dst % t, 0)
    blk_start = jnp.arange(n_blocks, dtype=jnp.int32) * ROW_BLOCK
    blk_exp = jnp.minimum(jnp.sum((pad_end[None, :] <= blk_start[:, None]).astype(jnp.int32), axis=1),
                          N_EXPERTS - 1)
    n_used = (pad_end[N_EXPERTS - 1:] // ROW_BLOCK).astype(jnp.int32)
    return blk_exp, n_used, row_tok, row_dst, row_w


def _final_kernel(x_ref, y0_ref, y1_ref, g2_ref, lg_ref, lb_ref, o_ref, *, alpha):
    z = alpha * x_ref[...] + (1.0 + g2_ref[...]) * (y0_ref[...] + y1_ref[...])
    o_ref[...] = _layer_norm(z) * lg_ref[...] + lb_ref[...]


def _final_ln(x1, y2, g2, ln_g, ln_b, alpha, tm=512):
    m, d = x1.shape
    nb = m // tm
    fixed = lambda i: (0, 0)
    return pl.pallas_call(
        functools.partial(_final_kernel, alpha=alpha),
        grid=(nb,),
        in_specs=[pl.BlockSpec((tm, d), lambda i: (i, 0)),
                  pl.BlockSpec((tm, d), lambda i: (i, 0)),
                  pl.BlockSpec((tm, d), lambda i: (i + nb, 0)),
                  pl.BlockSpec((1, d), fixed), pl.BlockSpec((1, d), fixed), pl.BlockSpec((1, d), fixed)],
        out_specs=pl.BlockSpec((tm, d), lambda i: (i, 0)),
        out_shape=jax.ShapeDtypeStruct((m, d), F32),
        compiler_params=_cparams(("parallel",)),
        name="final_ln",
    )(x1, y2, y2, g2, ln_g, ln_b)


def _to_lane_blocks(a, tq):
    g, hpg, s, c = a.shape
    return a.reshape(g, hpg, s // tq, tq, c).transpose(0, 2, 4, 1, 3).reshape(g, s // tq, c, hpg * tq)


def _layer(x2d, c, w_ada, b_ada, w_in, b_fgt, t5_table, cmp_pe, cmp_w1, cmp_b1, cmp_w2,
           w_br_nsa, w_br_fox, w_o, ln1_g, ln1_b, w_rg, b_rg, w_re, b_re,
           w_gate, w_up, w_down, ln2_g, ln2_b, alpha):
    s, d = x2d.shape
    hd = HEAD_DIM
    g = NSA_GROUPS
    mod = _ada_mod(c, w_ada, b_ada)
    sh1, sc1, g1, sh2, sc2, g2 = [mod[:, i * d:(i + 1) * d] for i in range(6)]

    c_q = NSA_HEADS * hd
    c_kv = 6 * g * hd
    c_gate = 3 * NSA_HEADS
    c_fox = 3 * FOX_HEADS * hd
    off_kv = c_q
    off_gate = off_kv + c_kv
    off_fox = off_gate + c_gate
    off_fgt = off_fox + c_fox
    off_merge = off_fgt + FOX_HEADS
    qscale = hd ** -0.5 * LOG2E
    nh = FOX_HEADS
    gw = g * hd

    def kv_cols(z):
        return w_in[:, off_kv + z * gw:off_kv + (z + 1) * gw]

    def lane_padded(w, heads):
        return jnp.pad(w.reshape(d, heads, hd), ((0, 0), (0, 0), (0, AUG_K - hd))).reshape(d, heads * AUG_K)

    fox_q, fox_k, fox_v = [w_in[:, off_fox + i * nh * hd:off_fox + (i + 1) * nh * hd] for i in range(3)]
    w_ch = jnp.concatenate([w_in[:, 0:off_kv] * qscale, kv_cols(3), kv_cols(5), fox_q * qscale, fox_v],
                           axis=1).T.astype(BF16)
    w_tok = jnp.concatenate([kv_cols(0), kv_cols(1), lane_padded(kv_cols(2), g), lane_padded(kv_cols(4), g),
                             lane_padded(fox_k, nh)], axis=1).astype(BF16)
    n_small = c_gate + FOX_HEADS
    w_small = jnp.concatenate([w_in[:, off_gate:off_fox], w_in[:, off_fgt:off_merge],
                               jnp.zeros((d, LANES - n_small), F32)], axis=1).astype(BF16)
    w_merge = w_in[:, off_merge:].astype(BF16)

    u = _ln_mod(x2d, sc1, sh1, BF16)
    ch = _matmul_nt(w_ch, u, BF16, 512, "in_proj_channel_major")
    tok = _matmul(u, w_tok, BF16, 512, w_tok.shape[1] // 2, "in_proj_token_major")
    small = _matmul(u, w_small, F32, 512, LANES, "in_proj_small")
    merge = _matmul(u, w_merge, BF16, 512, 1024, "in_proj_merge")

    qt_nsa = ch[0:c_q].reshape(g, NSA_HPG, hd, s)
    ch = ch.reshape(-1, hd, s)
    ch_vs_block0 = NSA_HEADS
    ch_vw_block0 = ch_vs_block0 + g
    ch_fq_block0 = ch_vw_block0 + g
    ch_fv_block0 = ch_fq_block0 + nh
    tok_ks_block0 = 2 * gw // AUG_K
    tok_kw_block0 = tok_ks_block0 + g
    tok_fox_block0 = tok_kw_block0 + g

    kv_cmp_in = tok[:, 0:2 * gw].reshape(s, 2, g, hd).transpose(1, 2, 0, 3)
    kv_cmp = _compress(kv_cmp_in, cmp_pe, cmp_w1, cmp_b1, cmp_w2)
    nch = s // CMP_STRIDE
    ncp = nch + LANES
    nslc = s // SLC_LEN
    kv_cmp_pad = jnp.pad(kv_cmp.astype(BF16), ((0, 0), (0, 0), (CMP_PAD, ncp - nch - CMP_PAD), (0, 0)))
    b_cmp, b_sel, b_win, far3 = _nsa_bias_tables(t5_table)

    row_ix = np.arange(ncp)
    row_ok = (row_ix >= CMP_PAD) & (row_ix < nch - 1 + CMP_PAD)
    row_cols = np.zeros((ncp, 2 * AUG_K - hd), np.float32)
    row_cols[:, ROW_BIAS - hd:ROW_BIAS - hd + 3] = row_ok[:, None]
    row_cols[:, ROW_BIAS - hd + 3] = ~row_ok
    row_cols[row_ix, AUG_K - hd + row_ix // BF16_ROWS] = 1.0
    kc_aug = jnp.concatenate([kv_cmp_pad[0], jnp.broadcast_to(jnp.asarray(row_cols, BF16), (g,) + row_cols.shape)],
                             axis=-1)
    vct = kv_cmp_pad[1].transpose(0, 2, 1)
    nq = s // ATT_TQ
    qt = qt_nsa.reshape(g, NSA_HPG, hd, nq, ATT_TQ).transpose(0, 3, 2, 1, 4).reshape(g, nq, hd, NSA_HPG * ATT_TQ)
    oct_, selt = _nsa_compress_select(qt, kc_aug, vct, b_cmp, far3)

    period = BF16_ROWS * SLC_LEN
    k_extra = np.zeros((period, AUG_K), np.float32)
    k_extra[np.arange(period), ROW_MASK + np.arange(period) // SLC_LEN] = 1.0
    k_extra[:, ROW_BIAS:ROW_BIAS + 3] = 1.0
    gates = small[:, 0:c_gate].reshape(s, g, NSA_HPG, 3).transpose(1, 2, 0, 3)
    gates_t = jnp.pad(_to_lane_blocks(gates, ATT_TQ), ((0, 0), (0, 0), (0, 5), (0, 0)))
    o_nsa_t = _nsa_attention(qt, tok, tok_ks_block0, tok_kw_block0, jnp.asarray(k_extra, BF16), ch, ch_vs_block0,
                             ch_vw_block0, selt, far3, b_sel, b_win, oct_, gates_t)
    o_nsa = o_nsa_t.reshape(MIX_W, s)

    fgt_bias = jnp.concatenate([jnp.zeros((c_gate,), F32), b_fgt, jnp.zeros((LANES - n_small,), F32)])[None, :]
    decay_k, decay_ends = _decay_pieces(small, fgt_bias, c_gate, nh)
    o_fox = _fox_attention(ch, ch_fq_block0, ch_fv_block0, tok, tok_fox_block0, decay_k, decay_ends)
    o_fox = o_fox.reshape(MIX_W, s)

    x1 = _merge_project(o_nsa, o_fox, merge, x2d, w_br_nsa.astype(BF16), w_br_fox.astype(BF16),
                        w_o.astype(BF16), g1, ln1_g[None, :], ln1_b[None, :], alpha)

    n_r = N_GROUPS + N_EXPERTS
    w_r = jnp.concatenate([w_rg, w_re.reshape(d, N_EXPERTS), jnp.zeros((d, LANES - n_r), F32)], axis=1)
    b_r = jnp.concatenate([b_rg, b_re.reshape(N_EXPERTS), jnp.zeros((LANES - n_r,), F32)])[None, :]
    w_r_hi, w_r_lo, _ = _split3_exact(w_r)
    u2, route = _router(x1, sc2, sh2, jnp.stack([w_r_hi, w_r_lo]), b_r)
    blk_exp, n_used, row_tok, row_dst, row_w = _moe_dispatch(route, s)
    y2 = _moe_experts(u2, blk_exp, n_used, row_tok, row_dst, row_w, w_gate, w_up, w_down)
    return _final_ln(x1, y2, g2, ln2_g[None, :], ln2_b[None, :], alpha)


def kernel(x, c, w_ada, b_ada, w_in, b_fgt, t5_table, cmp_pe, cmp_w1, cmp_b1, cmp_w2, w_br_nsa, w_br_fox, w_o,
           ln1_g, ln1_b, w_rg, b_rg, w_re, b_re, w_gate, w_up, w_down, ln2_g, ln2_b):
    b, s, d = x.shape
    depth = w_ada.shape[0]
    assert b == 1
    alpha = (2 * depth) ** 0.25
    h = x[0]
    for l in range(depth):
        h = _layer(h, c, w_ada[l], b_ada[l], w_in[l], b_fgt[l], t5_table, cmp_pe[l], cmp_w1[l], cmp_b1[l],
                   cmp_w2[l], w_br_nsa[l], w_br_fox[l], w_o[l], ln1_g[l], ln1_b[l], w_rg[l], b_rg[l],
                   w_re[l], b_re[l], w_gate[l], w_up[l], w_down[l], ln2_g[l], ln2_b[l], alpha)
    return h[None]
```

```python
import functools
import math

import numpy as np
import jax
import jax.numpy as jnp
from jax import lax
from jax.experimental import pallas as pl
from jax.experimental.pallas import tpu as pltpu

F32 = jnp.float32
BF16 = jnp.bfloat16
HIGHEST = lax.Precision.HIGHEST
LOG2E = math.log2(math.e)

HEAD_DIM = 64
NSA_HEADS = 8
NSA_GROUPS = 2
NSA_HPG = NSA_HEADS // NSA_GROUPS
FOX_HEADS = 8
MIX_W = NSA_HEADS * HEAD_DIM
CMP_LEN = 32
CMP_STRIDE = 16
SLC_LEN = 64
SLC_TOPK = 16
WINDOW = 512
T5_BUCKETS = 32
T5_MAX_EXACT = 16
T5_MAX_DIST = 128
N_GROUPS = 8
EXPERTS_PER_GROUP = 8
N_EXPERTS = N_GROUPS * EXPERTS_PER_GROUP
ROW_BLOCK = 128
LN_EPS = 1e-5
NEG_INF = -1e30
M_INIT = -1e29
FORCE_SCORE = 1e4

LANES = 128
BF16_ROWS = 16
CMP_PAD = 8
ATT_TQ = 256
CMP_NEAR_ROWS = 32
FOX_TQ = 1024
FOX_TK = 512
FOX_SKIP_MARGIN = 160.0
FOX_BOUND_SLACK = 1.02
AUG_K = 128
AUG_V = HEAD_DIM + 16
ROW_MASK = HEAD_DIM
ROW_BIAS = HEAD_DIM + 16
VMEM_LIMIT = 56 * 1024 * 1024


def _cparams(sem, vmem=VMEM_LIMIT):
    return pltpu.CompilerParams(dimension_semantics=sem, vmem_limit_bytes=vmem)


def _sigmoid(x):
    return 1.0 / (1.0 + jnp.exp(-x))


def _layer_norm(x):
    mu = jnp.mean(x, axis=-1, keepdims=True)
    xc = x - mu
    var = jnp.mean(xc * xc, axis=-1, keepdims=True)
    return xc * lax.rsqrt(var + LN_EPS)


def _split3(x):
    hi = x.astype(BF16)
    r1 = x - hi.astype(F32)
    mid = r1.astype(BF16)
    lo = (r1 - mid.astype(F32)).astype(BF16)
    return hi, mid, lo


def _split3_exact(x):
    def trunc(v):
        bits = lax.bitcast_convert_type(v, jnp.uint32) & jnp.uint32(0xFFFF0000)
        return lax.bitcast_convert_type(bits, F32)
    hi = trunc(x)
    r1 = x - hi
    mid = trunc(r1)
    lo = r1 - mid
    return hi.astype(BF16), mid.astype(BF16), lo.astype(BF16)


def _dot(a, b):
    return jnp.dot(a, b, preferred_element_type=F32)


def _dot_nt(a, b):
    return lax.dot_general(a, b, (((1,), (1,)), ((), ())), preferred_element_type=F32)


def _dot3(x, w_bf16):
    hi, mid, lo = _split3(x)
    return _dot(hi, w_bf16) + _dot(mid, w_bf16) + _dot(lo, w_bf16)


def _dot3_rhs(w_bf16, x):
    hi, mid, lo = _split3(x)
    return _dot(w_bf16, hi) + _dot(w_bf16, mid) + _dot(w_bf16, lo)


def _ada_kernel(c_ref, w_ref, b_ref, o_ref):
    c = c_ref[...]
    a = c * _sigmoid(c)
    o_ref[...] = jnp.dot(a, w_ref[...], precision=HIGHEST, preferred_element_type=F32) + b_ref[...]


def _ada_mod(c, w, b):
    d, n = w.shape
    tn = 1024
    c8 = jnp.broadcast_to(c, (8, d))
    out = pl.pallas_call(
        _ada_kernel,
        grid=(n // tn,),
        in_specs=[pl.BlockSpec((8, d), lambda j: (0, 0)),
                  pl.BlockSpec((d, tn), lambda j: (0, j)),
                  pl.BlockSpec((1, tn), lambda j: (0, j))],
        out_specs=pl.BlockSpec((8, tn), lambda j: (0, j)),
        out_shape=jax.ShapeDtypeStruct((8, n), F32),
        compiler_params=_cparams(("parallel",)),
        name="ada_mod",
    )(c8, w, b.reshape(1, n))
    return out[0:1]


def _lnmod_kernel(x_ref, sc_ref, sh_ref, o_ref):
    y = _layer_norm(x_ref[...])
    o_ref[...] = (y * (1.0 + sc_ref[...]) + sh_ref[...]).astype(o_ref.dtype)


def _ln_mod(x, sc, sh, out_dtype, tm=512):
    m, d = x.shape
    return pl.pallas_call(
        _lnmod_kernel,
        grid=(m // tm,),
        in_specs=[pl.BlockSpec((tm, d), lambda i: (i, 0)),
                  pl.BlockSpec((1, d), lambda i: (0, 0)),
                  pl.BlockSpec((1, d), lambda i: (0, 0))],
        out_specs=pl.BlockSpec((tm, d), lambda i: (i, 0)),
        out_shape=jax.ShapeDtypeStruct((m, d), out_dtype),
        compiler_params=_cparams(("parallel",)),
        name="ln_mod",
    )(x, sc, sh)


def _mm_kernel(a_ref, w_ref, o_ref):
    o_ref[...] = _dot(a_ref[...], w_ref[...]).astype(o_ref.dtype)


def _matmul(a, w, out_dtype, tm, tn, name):
    m, k = a.shape
    n = w.shape[1]
    return pl.pallas_call(
        _mm_kernel,
        grid=(n // tn, m // tm),
        in_specs=[pl.BlockSpec((tm, k), lambda j, i: (i, 0)),
                  pl.BlockSpec((k, tn), lambda j, i: (0, j))],
        out_specs=pl.BlockSpec((tm, tn), lambda j, i: (i, j)),
        out_shape=jax.ShapeDtypeStruct((m, n), out_dtype),
        compiler_params=_cparams(("parallel", "parallel")),
        name=name,
    )(a, w)


def _mm_nt_kernel(w_ref, a_ref, o_ref):
    o_ref[...] = _dot_nt(w_ref[...], a_ref[...]).astype(o_ref.dtype)


def _matmul_nt(w_t, a, out_dtype, tm, name):
    n, k = w_t.shape
    m = a.shape[0]
    return pl.pallas_call(
        _mm_nt_kernel,
        grid=(m // tm,),
        in_specs=[pl.BlockSpec((n, k), lambda i: (0, 0), pipeline_mode=pl.Buffered(1)),
                  pl.BlockSpec((tm, k), lambda i: (i, 0))],
        out_specs=pl.BlockSpec((n, tm), lambda i: (0, i)),
        out_shape=jax.ShapeDtypeStruct((n, m), out_dtype),
        compiler_params=_cparams(("parallel",)),
        name=name,
    )(w_t, a)


def _gelu_tanh(x):
    return 0.5 * x * (1.0 + jnp.tanh(math.sqrt(2.0 / math.pi) * (x + 0.044715 * (x * x * x))))


def _compress_kernel(c_ref, pe_ref, w1a_ref, w1b_ref, b1_ref, w2_ref, o_ref, *, nch):
    c = c_ref[0, 0]
    w1a = w1a_ref[0]
    w1b = w1b_ref[0]
    half = CMP_STRIDE * HEAD_DIM
    a = _dot(c, w1a)
    b = _dot(c, w1b)
    b_next = pltpu.roll(b, shift=nch - 1, axis=0)
    pe = pe_ref[0]
    pb = _dot(pe[:, :half], w1a) + _dot(pe[:, half:], w1b)
    hid = _gelu_tanh(a + b_next + pb[0:1, :] + b1_ref[0])
    o_ref[0, 0] = _dot(hid.astype(BF16), w2_ref[0])


def _compress(kv_cmp, pe, w1, b1, w2):
    z, g, s, hd = kv_cmp.shape
    nch = s // CMP_STRIDE
    half = CMP_STRIDE * hd
    chunks = kv_cmp.reshape(z, g, nch, half)
    pe8 = jnp.broadcast_to(pe.reshape(z, 1, CMP_LEN * hd), (z, 8, CMP_LEN * hd)).astype(BF16)
    w1b16 = w1.astype(BF16)
    hidn = w1.shape[-1]
    return pl.pallas_call(
        functools.partial(_compress_kernel, nch=nch),
        grid=(z, g),
        in_specs=[pl.BlockSpec((1, 1, nch, half), lambda zi, gi: (zi, gi, 0, 0)),
                  pl.BlockSpec((1, 8, 2 * half), lambda zi, gi: (zi, 0, 0)),
                  pl.BlockSpec((1, half, hidn), lambda zi, gi: (zi, 0, 0)),
                  pl.BlockSpec((1, half, hidn), lambda zi, gi: (zi, 1, 0)),
                  pl.BlockSpec((1, 1, hidn), lambda zi, gi: (zi, 0, 0)),
                  pl.BlockSpec((1, hidn, hd), lambda zi, gi: (zi, 0, 0))],
        out_specs=pl.BlockSpec((1, 1, nch, hd), lambda zi, gi: (zi, gi, 0, 0)),
        out_shape=jax.ShapeDtypeStruct((z, g, nch, hd), F32),
        compiler_params=_cparams(("parallel", "parallel")),
        name="compress_kv",
    )(chunks, pe8, w1b16, w1b16, b1.reshape(z, 1, hidn), w2.astype(BF16))


def _t5_bucket_np(dist):
    n = np.maximum(dist, 0)
    ratio = np.log(np.maximum(n, T5_MAX_EXACT).astype(np.float64) / T5_MAX_EXACT)
    big = T5_MAX_EXACT + (ratio / math.log(T5_MAX_DIST / T5_MAX_EXACT)
                          * (T5_BUCKETS - T5_MAX_EXACT)).astype(np.int64)
    return np.where(n < T5_MAX_EXACT, n, np.minimum(big, T5_BUCKETS - 1)).astype(np.int32)


def _t5_lookup(tbh, dist):
    onehot = np.eye(T5_BUCKETS, dtype=np.float32)[_t5_bucket_np(dist).reshape(-1)]
    vals = jnp.einsum('ghb,nb->ghn', tbh, jnp.asarray(onehot), precision=HIGHEST)
    return vals.reshape(tbh.shape[:2] + dist.shape)


def _toeplitz_kernel(w_ref, o_ref, *, n_keys, tq, lead_zero, trail_masked):
    width = w_ref.shape[-1]
    rows = jnp.broadcast_to(w_ref[0, 0], (n_keys, width))
    rolled = pltpu.roll(rows, shift=width - (n_keys - 1), axis=1, stride=1, stride_axis=0)
    parts = [rolled[:, 0:tq]]
    if lead_zero:
        parts.insert(0, jnp.zeros((lead_zero, tq), F32))
    if trail_masked:
        parts.append(jnp.full((trail_masked, tq), NEG_INF, F32))
    o_ref[0] = jnp.concatenate(parts, axis=0)


def _att_table_t(tbh, n_keys, tq, lo, hi, minus_far, lead_zero=0, trail_masked=0):
    length = n_keys + tq - 1
    d = np.arange(length) - (tq - 1)
    valid = (d >= lo) & (d < hi)
    vals = _t5_lookup(tbh, d)
    if minus_far:
        vals = vals - tbh[:, :, T5_BUCKETS - 1:]
    w = jnp.where(jnp.asarray(valid), vals * LOG2E, NEG_INF)
    width = pl.next_power_of_2(length)
    w = jnp.pad(w, ((0, 0), (0, 0), (0, width - length)))[:, :, None, :]
    g, hpg = tbh.shape[:2]
    n_rows = lead_zero + n_keys + trail_masked
    return pl.pallas_call(
        functools.partial(_toeplitz_kernel, n_keys=n_keys, tq=tq, lead_zero=lead_zero, trail_masked=trail_masked),
        grid=(g, hpg),
        in_specs=[pl.BlockSpec((1, 1, 1, width), lambda gi, hi: (gi, hi, 0, 0))],
        out_specs=pl.BlockSpec((1, n_rows, tq), lambda gi, hi: (gi, 0, hi)),
        out_shape=jax.ShapeDtypeStruct((g, n_rows, hpg * tq), F32),
        compiler_params=_cparams(("parallel", "parallel")),
        name="t5_toeplitz",
    )(w)


def _nsa_bias_tables(t5_table):
    tbh = t5_table.T.reshape(NSA_GROUPS, NSA_HPG, T5_BUCKETS).astype(F32)
    j = np.arange(CMP_NEAR_ROWS)[:, None]
    i = np.arange(ATT_TQ)[None, :]
    dist = i - (CMP_LEN - 1) - CMP_STRIDE * (j - CMP_PAD)
    vals = (_t5_lookup(tbh, dist) - tbh[:, :, T5_BUCKETS - 1][:, :, None, None]) * LOG2E
    vals = jnp.where(jnp.asarray(dist >= 0)[None, None], vals, NEG_INF)
    b_cmp = vals.transpose(0, 2, 1, 3).reshape(NSA_GROUPS, CMP_NEAR_ROWS, NSA_HPG * ATT_TQ)
    far = tbh[:, :, T5_BUCKETS - 1] * LOG2E
    b_sel = _att_table_t(tbh, 2 * ATT_TQ, ATT_TQ, 0, 1 << 30, True, lead_zero=ATT_TQ)
    b_win = _att_table_t(tbh, 3 * ATT_TQ, ATT_TQ, 0, WINDOW, False, trail_masked=ATT_TQ)
    hi, mid, lo = _split3_exact(jnp.repeat(far, ATT_TQ, axis=1))
    far3 = jnp.stack([hi, mid, lo, jnp.full(hi.shape, NEG_INF, BF16)], axis=1)
    far3 = jnp.pad(far3, ((0, 0), (0, BF16_ROWS - 4), (0, 0)))
    return b_cmp, b_sel, b_win, far3


def _nsa_cmp_kernel(qt_ref, kc_ref, vct_ref, near_ref, far3_ref, oct_ref, selt_ref, rhs, s_scr, imp_scr, *, nslc):
    tq = ATT_TQ
    lanes = NSA_HPG * tq
    qb = pl.program_id(1)
    cpb = tq // CMP_STRIDE
    rhs[...] = jnp.zeros(rhs.shape, BF16)
    rhs[0:HEAD_DIM, :] = qt_ref[0, 0]
    rhs[ROW_BIAS:ROW_BIAS + BF16_ROWS, :] = far3_ref[0]
    chunk = lax.broadcasted_iota(jnp.int32, (AUG_K, lanes), 0)
    rhs[AUG_K:2 * AUG_K, :] = jnp.where(chunk >= qb + CMP_NEAR_ROWS // BF16_ROWS, NEG_INF, 0.0).astype(BF16)
    r0 = pl.multiple_of(cpb * qb, BF16_ROWS)
    n_lane_blocks = tq // LANES
    ncp = s_scr.shape[0]
    imp_scr[...] = jnp.zeros(imp_scr.shape, F32)

    def attend(rows):
        s_scr[0:rows, :] = _dot(kc_ref[0, 0:rows, :], rhs[...])
        s_scr[pl.ds(r0, CMP_NEAR_ROWS), :] = s_scr[pl.ds(r0, CMP_NEAR_ROWS), :] + near_ref[0]
        s = s_scr[0:rows, :]
        m = jnp.max(s, axis=0, keepdims=True)
        e = jnp.exp2(s - m)
        l = jnp.sum(e, axis=0, keepdims=True)
        p = e * jnp.where(m > M_INIT, 1.0 / l, 0.0)
        oct_ref[0, 0] = _dot(vct_ref[0, :, 0:rows], p.astype(BF16))
        imp = p[:, 0:tq]
        for h in range(1, NSA_HPG):
            imp = imp + p[:, h * tq:(h + 1) * tq]
        for c in range(n_lane_blocks):
            imp_scr[c, 0:rows, :] = imp[:, c * LANES:(c + 1) * LANES]

    limits = sorted({min(ncp, -(-(ncp * k // 3) // LANES) * LANES) for k in (1, 2, 3)})
    lo_qb = 0
    for rows in limits:
        hi_qb = (rows - CMP_NEAR_ROWS) // cpb if rows < ncp else pl.num_programs(1) - 1

        @pl.when((qb >= lo_qb) & (qb <= hi_qb))
        def _(rows=rows):
            attend(rows)

        lo_qb = hi_qb + 1
    ratio = SLC_LEN // CMP_STRIDE

    def taps(off):
        return jnp.concatenate([imp_scr[c, pl.ds(CMP_PAD + off, nslc, stride=ratio), :]
                                for c in range(n_lane_blocks)], axis=1)

    p_slc = 0.5 * (taps(-1) + taps(ratio - 1))
    for off in range(ratio - 1):
        p_slc = p_slc + taps(off)
    blk = lax.broadcasted_iota(jnp.int32, (nslc, tq), 0)
    cur = (qb * tq + lax.broadcasted_iota(jnp.int32, (nslc, tq), 1)) // SLC_LEN
    forced = (blk == 0) | (blk == cur) | (blk == cur - 1)
    score = jnp.where(forced, FORCE_SCORE, jnp.where(blk <= cur, p_slc, -1.0))
    blk_f = blk.astype(F32)
    sel = jnp.zeros((nslc, tq), F32)
    for _ in range(min(SLC_TOPK, nslc)):
        mx = jnp.max(score, axis=0, keepdims=True)
        first = jnp.min(jnp.where(score == mx, blk_f, float(nslc)), axis=0, keepdims=True)
        hit = blk_f == first
        sel = jnp.where(hit, 1.0, sel)
        score = jnp.where(hit, -2.0, score)
    selt_ref[0] = sel.astype(BF16)


def _nsa_compress_select(qt, kc_aug, vct, b_cmp, far3):
    g, nq, hd, lanes = qt.shape
    ncp = kc_aug.shape[1]
    tq = ATT_TQ
    nslc = nq * tq // SLC_LEN
    assert ncp // BF16_ROWS <= AUG_K
    assert tq // CMP_STRIDE == BF16_ROWS
    per_q = lambda gi, qi: (gi, qi, 0, 0)
    per_g = lambda gi, qi: (gi, 0, 0)
    return pl.pallas_call(
        functools.partial(_nsa_cmp_kernel, nslc=nslc),
        grid=(g, nq),
        in_specs=[pl.BlockSpec((1, 1, hd, lanes), per_q),
                  pl.BlockSpec((1, ncp, 2 * AUG_K), per_g),
                  pl.BlockSpec((1, hd, ncp), per_g),
                  pl.BlockSpec((1, CMP_NEAR_ROWS, lanes), per_g),
                  pl.BlockSpec((1, BF16_ROWS, lanes), per_g)],
        out_specs=[pl.BlockSpec((1, 1, hd, lanes), per_q),
                   pl.BlockSpec((1, nslc, tq), lambda gi, qi: (gi, 0, qi))],
        out_shape=[jax.ShapeDtypeStruct((g, nq, hd, lanes), F32),
                   jax.ShapeDtypeStruct((g, nslc, nq * tq), BF16)],
        scratch_shapes=[pltpu.VMEM((2 * AUG_K, lanes), BF16), pltpu.VMEM((ncp, lanes), F32),
                        pltpu.VMEM((tq // LANES, ncp, LANES), F32)],
        compiler_params=_cparams(("parallel", "parallel")),
        name="nsa_compress_select",
    )(qt, kc_aug, vct, b_cmp, far3)


def _flash_init_t(m_ref, acc_ref):
    m_ref[...] = jnp.full(m_ref.shape, M_INIT, F32)
    acc_ref[...] = jnp.zeros(acc_ref.shape, F32)


def _flash_step_t(s, vt_tile, m_ref, acc_ref):
    m_old = m_ref[...]
    m_new = jnp.maximum(m_old, jnp.max(s, axis=0, keepdims=True))
    p = jnp.exp2(s - m_new).astype(BF16)
    acc_ref[...] = jnp.exp2(m_old - m_new) * acc_ref[...] + _dot(vt_tile, p)
    m_ref[...] = m_new


def _flash_result_t(acc_ref):
    acc = acc_ref[...]
    return acc[0:HEAD_DIM, :] / acc[HEAD_DIM:HEAD_DIM + 1, :]


def _nsa_att_kernel(qt_ref, ks_ref, kx_ref, vst_ref, kw_ref, vwt_ref, selt_ref, far3_ref, bsel_ref, bwin_ref,
                    oct_ref, gate_ref, o_ref, rhs_s, rhs_w, mask_t, ms, accs, mw, accw, s_even, s_odd):
    tq = ATT_TQ
    qb = pl.program_id(1)
    ones_rows = _ones_rows(tq)
    kx_tiles = kx_ref.shape[0] // tq
    qt = qt_ref[0, 0]
    rhs_s[...] = jnp.zeros(rhs_s.shape, BF16)
    rhs_s[0:HEAD_DIM, :] = qt
    rhs_s[ROW_BIAS:ROW_BIAS + BF16_ROWS, :] = far3_ref[0]
    rhs_w[...] = jnp.zeros(rhs_w.shape, BF16)
    rhs_w[0:HEAD_DIM, :] = qt
    madd = ((selt_ref[0].astype(F32) - 1.0) * (-NEG_INF)).astype(BF16)
    mask_t[...] = jnp.concatenate([madd] * NSA_HPG, axis=1)
    _flash_init_t(ms, accs)
    _flash_init_t(mw, accw)
    blocks_per_tile = tq // SLC_LEN

    def sel_scores(kt, s_ref):
        kt = jnp.minimum(kt, qb)
        k0 = pl.multiple_of(kt * tq, tq)
        chunk = pl.multiple_of((kt * blocks_per_tile) // BF16_ROWS * BF16_ROWS, BF16_ROWS)
        rhs_s[ROW_MASK:ROW_MASK + BF16_ROWS, :] = mask_t[pl.ds(chunk, BF16_ROWS), :]
        rel = jnp.clip(kt - qb + 2, 0, 2)
        table = bsel_ref[0, pl.ds(pl.multiple_of(rel * tq, tq), tq), :]
        k_aug = ks_ref[pl.ds(k0, tq), :] + kx_ref[pl.ds(pl.multiple_of((kt % kx_tiles) * tq, tq), tq), :]
        s_ref[...] = _dot(k_aug, rhs_s[...]) + table

    def sel_consume(kt, s_ref):
        k0 = pl.multiple_of(kt * tq, tq)
        vt_aug = jnp.concatenate([vst_ref[0, :, pl.ds(k0, tq)], ones_rows], axis=0)
        _flash_step_t(s_ref[...], vt_aug, ms, accs)

    def win_scores(j, s_ref):
        kt = qb - 2 + j
        k0 = pl.multiple_of(jnp.maximum(kt, 0) * tq, tq)
        row = pl.multiple_of(jnp.where(kt >= 0, j, 3) * tq, tq)
        s_ref[...] = _dot(kw_ref[pl.ds(k0, tq), :], rhs_w[...]) + bwin_ref[0, pl.ds(row, tq), :]

    def win_consume(j, s_ref):
        k0 = pl.multiple_of(jnp.maximum(qb - 2 + j, 0) * tq, tq)
        vt_aug = jnp.concatenate([vwt_ref[0, :, pl.ds(k0, tq)], ones_rows], axis=0)
        _flash_step_t(s_ref[...], vt_aug, mw, accw)

    n_sel = qb + 1
    sel_scores(0, s_even)

    def pair_body(j, carry):
        sel_scores(2 * j + 1, s_odd)
        sel_consume(2 * j, s_even)
        sel_scores(2 * j + 2, s_even)
        sel_consume(2 * j + 1, s_odd)
        return carry

    lax.fori_loop(0, n_sel // 2, pair_body, 0)
    win_scores(0, s_odd)

    @pl.when(n_sel % 2 == 1)
    def _():
        sel_consume(qb, s_even)

    win_scores(1, s_even)
    win_consume(0, s_odd)
    win_scores(2, s_odd)
    win_consume(1, s_even)
    win_consume(2, s_odd)

    gt = _sigmoid(gate_ref[0, 0])
    out = gt[0:1, :] * oct_ref[0, 0] + gt[1:2, :] * _flash_result_t(accs) + gt[2:3, :] * _flash_result_t(accw)
    for h in range(NSA_HPG):
        o_ref[0, h] = out[:, h * tq:(h + 1) * tq].astype(o_ref.dtype)


def _nsa_attention(qt, k_tok, ks_block0, kw_block0, k_extra, ch, vs_block0, vw_block0, selt, far3, b_sel, b_win,
                   oct_, gates_t):
    g, nq, hd, lanes = qt.shape
    s = k_tok.shape[0]
    nslc = selt.shape[1]
    tq = ATT_TQ
    resident = pl.Buffered(1)
    per_q = lambda gi, qi: (gi, qi, 0, 0)
    per_g = lambda gi, qi: (gi, 0, 0)
    return pl.pallas_call(
        _nsa_att_kernel,
        grid=(g, nq),
        in_specs=[pl.BlockSpec((1, 1, hd, lanes), per_q),
                  pl.BlockSpec((s, AUG_K), lambda gi, qi: (0, ks_block0 + gi), pipeline_mode=resident),
                  pl.BlockSpec(k_extra.shape, lambda gi, qi: (0, 0), pipeline_mode=resident),
                  pl.BlockSpec((1, hd, s), lambda gi, qi: (vs_block0 + gi, 0, 0), pipeline_mode=resident),
                  pl.BlockSpec((s, AUG_K), lambda gi, qi: (0, kw_block0 + gi), pipeline_mode=resident),
                  pl.BlockSpec((1, hd, s), lambda gi, qi: (vw_block0 + gi, 0, 0), pipeline_mode=resident),
                  pl.BlockSpec((1, nslc, tq), lambda gi, qi: (gi, 0, qi)),
                  pl.BlockSpec((1, BF16_ROWS, lanes), per_g),
                  pl.BlockSpec((1, 3 * tq, lanes), per_g, pipeline_mode=resident),
                  pl.BlockSpec((1, 4 * tq, lanes), per_g, pipeline_mode=resident),
                  pl.BlockSpec((1, 1, hd, lanes), per_q),
                  pl.BlockSpec((1, 1, 8, lanes), per_q)],
        out_specs=pl.BlockSpec((1, lanes // tq, hd, tq), lambda gi, qi: (gi, 0, 0, qi)),
        out_shape=jax.ShapeDtypeStruct((g, lanes // tq, hd, s), BF16),
        scratch_shapes=[pltpu.VMEM((AUG_K, lanes), BF16), pltpu.VMEM((AUG_K, lanes), BF16),
                        pltpu.VMEM((nslc, lanes), BF16),
                        pltpu.VMEM((1, lanes), F32), pltpu.VMEM((AUG_V, lanes), F32),
                        pltpu.VMEM((1, lanes), F32), pltpu.VMEM((AUG_V, lanes), F32),
                        pltpu.VMEM((tq, lanes), F32), pltpu.VMEM((tq, lanes), F32)],
        compiler_params=_cparams(("arbitrary", "arbitrary")),
        name="nsa_select_window",
    )(qt, k_tok, k_extra, ch, k_tok, ch, selt, far3, b_sel, b_win, oct_, gates_t)


def _decay_kernel(z_ref, b_ref, place_ref, o_ref, end_ref, carry_ref, *, tb):
    @pl.when(pl.program_id(0) == 0)
    def _():
        carry_ref[...] = jnp.zeros(carry_ref.shape, F32)

    z = z_ref[...] + b_ref[...]
    log_f = jnp.minimum(z, 0.0) - jnp.log1p(jnp.exp(-jnp.abs(z)))
    r = lax.broadcasted_iota(jnp.int32, (tb, tb), 0)
    c = lax.broadcasted_iota(jnp.int32, (tb, tb), 1)
    tri = jnp.where(r >= c, 1.0, 0.0).astype(BF16)
    run = _dot3_rhs(tri, log_f) + carry_ref[...]
    carry_ref[...] = run[tb - 1:tb, :]
    val = -run * LOG2E
    hi, mid, lo = _split3_exact(val)
    o_ref[...] = (_dot(hi, place_ref[0]) + _dot(mid, place_ref[1]) + _dot(lo, place_ref[2])).astype(BF16)
    end_ref[...] = jnp.broadcast_to(val[tb - 1:tb, :], end_ref.shape)


def _decay_pieces(z, bias, first_lane, n_heads):
    s, n = z.shape
    tb = FOX_TK
    place = np.zeros((3, n, n_heads * AUG_K), np.float32)
    for h in range(n_heads):
        for piece in range(3):
            place[piece, first_lane + h, h * AUG_K + HEAD_DIM + piece] = 1.0
    pieces, ends = pl.pallas_call(
        functools.partial(_decay_kernel, tb=tb),
        grid=(s // tb,),
        in_specs=[pl.BlockSpec((tb, n), lambda i: (i, 0)),
                  pl.BlockSpec((1, n), lambda i: (0, 0)),
                  pl.BlockSpec((3, n, n_heads * AUG_K), lambda i: (0, 0, 0))],
        out_specs=[pl.BlockSpec((tb, n_heads * AUG_K), lambda i: (i, 0)),
                   pl.BlockSpec((8, n), lambda i: (i, 0))],
        out_shape=[jax.ShapeDtypeStruct((s, n_heads * AUG_K), BF16),
                   jax.ShapeDtypeStruct((s // tb * 8, n), F32)],
        scratch_shapes=[pltpu.VMEM((1, n), F32)],
        compiler_params=_cparams(("arbitrary",)),
        name="decay_cumsum",
    )(z, bias, jnp.asarray(place, BF16))
    return pieces, ends[::8, first_lane:first_lane + n_heads].T


def _ones_rows(width):
    return jnp.where(lax.broadcasted_iota(jnp.int32, (BF16_ROWS, width), 0) < 8, 1.0, 0.0).astype(BF16)


def _fox_kernel(ends_ref, qt_ref, k_ref, dk_ref, vt_ref, o_ref, rhs, m_ref, acc_ref, s_even, s_odd, kmax_ref):
    tq = FOX_TQ
    tk = FOX_TK
    assert tq == 2 * tk
    qb = pl.program_id(1)
    row = lax.broadcasted_iota(jnp.int32, (AUG_K - HEAD_DIM, tq), 0)
    rhs[0:HEAD_DIM, :] = qt_ref[0]
    rhs[HEAD_DIM:AUG_K, :] = jnp.where(row < 3, 1.0, 0.0).astype(BF16)
    ones_rows = _ones_rows(tk)
    _flash_init_t(m_ref, acc_ref)

    def scores(kt, s_ref):
        k0 = pl.multiple_of(kt * tk, tk)
        k_aug = k_ref[pl.ds(k0, tk), :] + dk_ref[pl.ds(k0, tk), :]
        s_ref[...] = _dot(k_aug, rhs[...])

    def consume(kt, s_ref, diagonal):
        k0 = pl.multiple_of(kt * tk, tk)
        s = s_ref[...]
        if diagonal:
            key = k0 + lax.broadcasted_iota(jnp.int32, (tk, tq), 0)
            qry = qb * tq + lax.broadcasted_iota(jnp.int32, (tk, tq), 1)
            s = jnp.where(key <= qry, s, NEG_INF)
        vt_aug = jnp.concatenate([vt_ref[0, :, pl.ds(k0, tk)], ones_rows], axis=0)
        _flash_step_t(s, vt_aug, m_ref, acc_ref)

    @pl.when(qb == 0)
    def _():
        ones = jnp.ones((AUG_K, AUG_K), BF16)

        def norm_tile(c, best):
            k = k_ref[pl.ds(pl.multiple_of(c * tk, tk), tk), :].astype(F32)
            return jnp.maximum(best, _dot((k * k).astype(BF16), ones))

        best = lax.fori_loop(0, k_ref.shape[0] // tk, norm_tile, jnp.zeros((tk, AUG_K), F32))
        kmax_ref[0] = jnp.max(jnp.sqrt(best * FOX_BOUND_SLACK))

    scores(2 * qb, s_even)
    scores(2 * qb + 1, s_odd)
    consume(2 * qb, s_even, True)
    consume(2 * qb + 1, s_odd, True)

    q = qt_ref[0].astype(F32)
    q_norm = jnp.max(jnp.sqrt(jnp.sum(q * q, axis=0, keepdims=True) * FOX_BOUND_SLACK))
    threshold = jnp.min(m_ref[...]) - FOX_SKIP_MARGIN - q_norm * kmax_ref[0]
    head = pl.program_id(0)

    def first_needed(j, first):
        return jnp.where(ends_ref[head, 2 * j + 1] >= threshold, jnp.minimum(first, j), first)

    j0 = lax.fori_loop(0, qb, first_needed, qb)

    scores(2 * j0, s_even)

    def pair_body(j, carry):
        scores(2 * j + 1, s_odd)
        consume(2 * j, s_even, False)
        scores(2 * j + 2, s_even)
        consume(2 * j + 1, s_odd, False)
        return carry

    lax.fori_loop(j0, qb, pair_body, 0)
    o_ref[0] = _flash_result_t(acc_ref).astype(o_ref.dtype)


def _fox_attention(ch, q_block0, v_block0, k_tok, k_block0, decay_k, decay_ends):
    _, hd, s = ch.shape
    h = decay_ends.shape[0]
    tq = FOX_TQ
    resident = pl.Buffered(1)
    grid_spec = pltpu.PrefetchScalarGridSpec(
        num_scalar_prefetch=1,
        grid=(h, s // tq),
        in_specs=[pl.BlockSpec((1, hd, tq), lambda hi, qi, ends: (q_block0 + hi, 0, qi)),
                  pl.BlockSpec((s, AUG_K), lambda hi, qi, ends: (0, k_block0 + hi), pipeline_mode=resident),
                  pl.BlockSpec((s, AUG_K), lambda hi, qi, ends: (0, hi), pipeline_mode=resident),
                  pl.BlockSpec((1, hd, s), lambda hi, qi, ends: (v_block0 + hi, 0, 0), pipeline_mode=resident)],
        out_specs=pl.BlockSpec((1, hd, tq), lambda hi, qi, ends: (hi, 0, qi)),
        scratch_shapes=[pltpu.VMEM((AUG_K, tq), BF16),
                        pltpu.VMEM((1, tq), F32), pltpu.VMEM((AUG_V, tq), F32),
                        pltpu.VMEM((FOX_TK, tq), F32), pltpu.VMEM((FOX_TK, tq), F32),
                        pltpu.SMEM((1,), F32)],
    )
    return pl.pallas_call(
        _fox_kernel,
        grid_spec=grid_spec,
        out_shape=jax.ShapeDtypeStruct((h, hd, s), BF16),
        compiler_params=_cparams(("arbitrary", "arbitrary")),
        name="fox_attention",
    )(decay_ends, ch, k_tok, decay_k, ch)


def _merge_kernel(on_ref, of_ref, mg_ref, x_ref, wn_ref, wf_ref, wo_ref, g1_ref, lg_ref, lb_ref, o_ref, *, alpha):
    d = x_ref.shape[-1]
    tn = (((0,), (0,)), ((), ()))
    a = lax.dot_general(on_ref[...], wn_ref[...], tn, preferred_element_type=F32)
    b = lax.dot_general(of_ref[...], wf_ref[...], tn, preferred_element_type=F32)
    gm = _sigmoid(mg_ref[...].astype(F32))
    merged = gm[:, 0:d] * a + gm[:, d:2 * d] * b
    y = _dot(merged.astype(BF16), wo_ref[...])
    z = alpha * x_ref[...] + (1.0 + g1_ref[...]) * y
    o_ref[...] = _layer_norm(z) * lg_ref[...] + lb_ref[...]


def _merge_project(o_nsa, o_fox, merge, x, wn, wf, wo, g1, ln_g, ln_b, alpha, tm=256):
    m, d = x.shape
    w = o_nsa.shape[0]
    resident = pl.Buffered(1)
    row = lambda i: (i, 0)
    fixed = lambda i: (0, 0)
    return pl.pallas_call(
        functools.partial(_merge_kernel, alpha=alpha),
        grid=(m // tm,),
        in_specs=[pl.BlockSpec((w, tm), lambda i: (0, i)), pl.BlockSpec((w, tm), lambda i: (0, i)),
                  pl.BlockSpec((tm, 2 * d), row), pl.BlockSpec((tm, d), row),
                  pl.BlockSpec((w, d), fixed, pipeline_mode=resident),
                  pl.BlockSpec((w, d), fixed, pipeline_mode=resident),
                  pl.BlockSpec((d, d), fixed, pipeline_mode=resident),
                  pl.BlockSpec((1, d), fixed), pl.BlockSpec((1, d), fixed), pl.BlockSpec((1, d), fixed)],
        out_specs=pl.BlockSpec((tm, d), row),
        out_shape=jax.ShapeDtypeStruct((m, d), F32),
        compiler_params=_cparams(("parallel",)),
        name="merge_project_ln",
    )(o_nsa, o_fox, merge, x, wn, wf, wo, g1, ln_g, ln_b)


def _router_kernel(x_ref, sc_ref, sh_ref, w_ref, b_ref, u_ref, r_ref):
    u = _layer_norm(x_ref[...]) * (1.0 + sc_ref[...]) + sh_ref[...]
    u_ref[...] = u
    u_hi = u.astype(BF16)
    u_lo = (u - u_hi.astype(F32)).astype(BF16)
    logits = _dot(u_hi, w_ref[0]) + _dot(u_lo, w_ref[0]) + _dot(u_hi, w_ref[1]) + b_ref[...]
    lane = lax.broadcasted_iota(jnp.int32, (1, LANES), 1).astype(F32)
    none = float(LANES)
    is_g = lane < N_GROUPS
    lg = jnp.where(is_g, logits, NEG_INF)
    eg = jnp.exp(lg - jnp.max(lg, axis=-1, keepdims=True))
    pg = eg / jnp.sum(eg, axis=-1, keepdims=True)
    p_grp = jnp.max(pg, axis=-1, keepdims=True)
    grp = jnp.min(jnp.where(pg == p_grp, lane, none), axis=-1, keepdims=True)
    lo = N_GROUPS + grp * EXPERTS_PER_GROUP
    is_e = (lane >= lo) & (lane < lo + EXPERTS_PER_GROUP)
    le = jnp.where(is_e, logits, NEG_INF)
    ee = jnp.exp(le - jnp.max(le, axis=-1, keepdims=True))
    pe = jnp.where(is_e, ee / jnp.sum(ee, axis=-1, keepdims=True), -1.0)
    p1 = jnp.max(pe, axis=-1, keepdims=True)
    i1 = jnp.min(jnp.where(pe == p1, lane, none), axis=-1, keepdims=True)
    pe2 = jnp.where(lane == i1, -1.0, pe)
    p2 = jnp.max(pe2, axis=-1, keepdims=True)
    i2 = jnp.min(jnp.where(pe2 == p2, lane, none), axis=-1, keepdims=True)
    den = p1 + p2
    r_ref[...] = jnp.where(lane == 0, i1 - N_GROUPS,
                           jnp.where(lane == 1, i2 - N_GROUPS,
                                     jnp.where(lane == 2, p_grp * p1 / den,
                                               jnp.where(lane == 3, p_grp * p2 / den, 0.0))))


def _router(x1, sc, sh, w_r, b_r, tm=256):
    m, d = x1.shape
    row = lambda i: (i, 0)
    fixed = lambda i: (0, 0)
    return pl.pallas_call(
        _router_kernel,
        grid=(m // tm,),
        in_specs=[pl.BlockSpec((tm, d), row), pl.BlockSpec((1, d), fixed), pl.BlockSpec((1, d), fixed),
                  pl.BlockSpec((2, d, LANES), lambda i: (0, 0, 0)), pl.BlockSpec((1, LANES), fixed)],
        out_specs=[pl.BlockSpec((tm, d), row), pl.BlockSpec((tm, LANES), row)],
        out_shape=[jax.ShapeDtypeStruct((m, d), F32), jax.ShapeDtypeStruct((m, LANES), F32)],
        compiler_params=_cparams(("parallel",)),
        name="moe_router",
    )(x1, sc, sh, w_r, b_r)


def _moe_kernel(be_ref, nu_ref, tok_ref, tok_next_ref, dst_ref, rw_ref, u_hbm, wg_ref, wu_ref, wd_ref, out_hbm,
                xbuf, ybuf, wgb, wub, wdb, sem_in, sem_out, *, n_dump0):
    rb = ROW_BLOCK
    i = pl.program_id(0)
    last = nu_ref[0] - 1
    slot = i % 2

    def row_in(r, tok, sl):
        return pltpu.make_async_copy(u_hbm.at[pl.ds(tok, 1)], xbuf.at[sl, pl.ds(r, 1)], sem_in.at[sl])

    def row_out(r, dst):
        return pltpu.make_async_copy(ybuf.at[pl.ds(r, 1), :], out_hbm.at[pl.ds(dst, 1), :], sem_out)

    @pl.when(i == 0)
    def _():
        for r in range(rb):
            row_in(r, tok_ref[0, 0, r], 0).start()
        ybuf[...] = jnp.zeros(ybuf.shape, F32)
        pltpu.make_async_copy(ybuf, out_hbm.at[pl.ds(n_dump0, rb), :], sem_out).start()

    prev = be_ref[jnp.maximum(i - 1, 0)]

    @pl.when((i <= last) & ((i == 0) | (be_ref[i] != prev)))
    def _():
        wgb[...] = wg_ref[0].astype(BF16)
        wub[...] = wu_ref[0].astype(BF16)
        wdb[...] = wd_ref[0].astype(BF16)

    @pl.when(i <= last)
    def _():
        for r in range(rb):
            row_in(r, 0, slot).wait()
        xb = jnp.concatenate([xbuf[slot, :, j, :] for j in range(xbuf.shape[2])], axis=1).astype(BF16)
        for r in range(rb):
            row_in(r, tok_next_ref[0, 0, r], 1 - slot).start()
        gate = _dot(xb, wgb[...])
        up = _dot(xb, wub[...])
        hid = (gate * _sigmoid(gate)) * up
        y = _dot(hid.astype(BF16), wdb[...]) * rw_ref[0]
        for r in range(rb):
            row_out(r, 0).wait()
        ybuf[...] = y
        for r in range(rb):
            row_out(r, dst_ref[0, 0, r]).start(priority=r % 2)

    @pl.when(i == last)
    def _():
        for r in range(rb):
            row_out(r, 0).wait()
        for r in range(rb):
            row_in(r, 0, 1 - slot).wait()


def _moe_experts(u, blk_exp, n_used, row_tok, row_dst, row_w, w_gate, w_up, w_down):
    t, d = u.shape
    n_blocks = blk_exp.shape[0]
    de = w_gate.shape[-1]
    rb = ROW_BLOCK
    tok3 = row_tok.reshape(n_blocks, 1, rb)
    grid_spec = pltpu.PrefetchScalarGridSpec(
        num_scalar_prefetch=2,
        grid=(n_blocks,),
        in_specs=[pl.BlockSpec((1, 1, rb), lambda i, be, nu: (i, 0, 0), memory_space=pltpu.SMEM),
                  pl.BlockSpec((1, 1, rb), lambda i, be, nu: (jnp.minimum(i + 1, n_blocks - 1), 0, 0),
                               memory_space=pltpu.SMEM),
                  pl.BlockSpec((1, 1, rb), lambda i, be, nu: (i, 0, 0), memory_space=pltpu.SMEM),
                  pl.BlockSpec((1, rb, 1), lambda i, be, nu: (i, 0, 0)),
                  pl.BlockSpec(memory_space=pl.ANY),
                  pl.BlockSpec((1, d, de), lambda i, be, nu: (be[i], 0, 0)),
                  pl.BlockSpec((1, d, de), lambda i, be, nu: (be[i], 0, 0)),
                  pl.BlockSpec((1, de, d), lambda i, be, nu: (be[i], 0, 0))],
        out_specs=pl.BlockSpec(memory_space=pl.ANY),
        scratch_shapes=[pltpu.VMEM((2, rb, d // LANES, LANES), F32), pltpu.VMEM((rb, d), F32),
                        pltpu.VMEM((d, de), BF16), pltpu.VMEM((d, de), BF16), pltpu.VMEM((de, d), BF16),
                        pltpu.SemaphoreType.DMA((2,)), pltpu.SemaphoreType.DMA(())],
    )
    return pl.pallas_call(
        functools.partial(_moe_kernel, n_dump0=2 * t),
        grid_spec=grid_spec,
        out_shape=jax.ShapeDtypeStruct((2 * t + rb, d), F32),
        compiler_params=_cparams(("arbitrary",)),
        name="moe_experts",
    )(blk_exp, n_used, tok3, tok3, row_dst.reshape(n_blocks, 1, rb), row_w.reshape(n_blocks, rb, 1),
      u.reshape(t, d // LANES, LANES), w_gate, w_up, w_down)


def _moe_dispatch(route, t):
    k = 2
    eid = route[:, 0:k].astype(jnp.int32).reshape(-1)
    wts = route[:, k:2 * k].reshape(-1)
    n_asg = t * k
    n_rows = n_asg + N_EXPERTS * ROW_BLOCK
    n_blocks = n_rows // ROW_BLOCK
    onehot = (eid[:, None] == jnp.arange(N_EXPERTS, dtype=jnp.int32)[None, :]).astype(jnp.int32)
    rank = jnp.sum((jnp.cumsum(onehot, axis=0) - onehot) * onehot, axis=1)
    counts = jnp.sum(onehot, axis=0)
    padded = (counts + ROW_BLOCK - 1) // ROW_BLOCK * ROW_BLOCK
    pad_end = jnp.cumsum(padded)
    pad_start = pad_end - padded
    dest = jnp.sum(onehot * pad_start[None, :], axis=1) + rank
    asg = jnp.arange(n_asg, dtype=jnp.int32)
    upd = jnp.stack([(asg % k) * t + asg // k, lax.bitcast_convert_type(wts, jnp.int32)], axis=1)
    init = jnp.stack([n_asg + jnp.arange(n_rows, dtype=jnp.int32) % ROW_BLOCK,
                      jnp.zeros((n_rows,), jnp.int32)], axis=1)
    rows = init.at[dest].set(upd)
    row_dst = rows[:, 0]
    row_w = lax.bitcast_convert_type(rows[:, 1], F32)
    row_tok = jnp.where(row_dst < n_asg, row_dst % t, 0)
    blk_start = jnp.arange(n_blocks, dtype=jnp.int32) * ROW_BLOCK
    blk_exp = jnp.minimum(jnp.sum((pad_end[None, :] <= blk_start[:, None]).astype(jnp.int32), axis=1),
                          N_EXPERTS - 1)
    n_used = (pad_end[N_EXPERTS - 1:] // ROW_BLOCK).astype(jnp.int32)
    return blk_exp, n_used, row_tok, row_dst, row_w


def _final_kernel(x_ref, y0_ref, y1_ref, g2_ref, lg_ref, lb_ref, o_ref, *, alpha):
    z = alpha * x_ref[...] + (1.0 + g2_ref[...]) * (y0_ref[...] + y1_ref[...])
    o_ref[...] = _layer_norm(z) * lg_ref[...] + lb_ref[...]


def _final_ln(x1, y2, g2, ln_g, ln_b, alpha, tm=512):
    m, d = x1.shape
    nb = m // tm
    fixed = lambda i: (0, 0)
    return pl.pallas_call(
        functools.partial(_final_kernel, alpha=alpha),
        grid=(nb,),
        in_specs=[pl.BlockSpec((tm, d), lambda i: (i, 0)),
                  pl.BlockSpec((tm, d), lambda i: (i, 0)),
                  pl.BlockSpec((tm, d), lambda i: (i + nb, 0)),
                  pl.BlockSpec((1, d), fixed), pl.BlockSpec((1, d), fixed), pl.BlockSpec((1, d), fixed)],
        out_specs=pl.BlockSpec((tm, d), lambda i: (i, 0)),
        out_shape=jax.ShapeDtypeStruct((m, d), F32),
        compiler_params=_cparams(("parallel",)),
        name="final_ln",
    )(x1, y2, y2, g2, ln_g, ln_b)


def _to_lane_blocks(a, tq):
    g, hpg, s, c = a.shape
    return a.reshape(g, hpg, s // tq, tq, c).transpose(0, 2, 4, 1, 3).reshape(g, s // tq, c, hpg * tq)


def _layer(x2d, c, w_ada, b_ada, w_in, b_fgt, t5_table, cmp_pe, cmp_w1, cmp_b1, cmp_w2,
           w_br_nsa, w_br_fox, w_o, ln1_g, ln1_b, w_rg, b_rg, w_re, b_re,
           w_gate, w_up, w_down, ln2_g, ln2_b, alpha):
    s, d = x2d.shape
    hd = HEAD_DIM
    g = NSA_GROUPS
    mod = _ada_mod(c, w_ada, b_ada)
    sh1, sc1, g1, sh2, sc2, g2 = [mod[:, i * d:(i + 1) * d] for i in range(6)]

    c_q = NSA_HEADS * hd
    c_kv = 6 * g * hd
    c_gate = 3 * NSA_HEADS
    c_fox = 3 * FOX_HEADS * hd
    off_kv = c_q
    off_gate = off_kv + c_kv
    off_fox = off_gate + c_gate
    off_fgt = off_fox + c_fox
    off_merge = off_fgt + FOX_HEADS
    qscale = hd ** -0.5 * LOG2E
    nh = FOX_HEADS
    gw = g * hd

    def kv_cols(z):
        return w_in[:, off_kv + z * gw:off_kv + (z + 1) * gw]

    def lane_padded(w, heads):
        return jnp.pad(w.reshape(d, heads, hd), ((0, 0), (0, 0), (0, AUG_K - hd))).reshape(d, heads * AUG_K)

    fox_q, fox_k, fox_v = [w_in[:, off_fox + i * nh * hd:off_fox + (i + 1) * nh * hd] for i in range(3)]
    w_ch = jnp.concatenate([w_in[:, 0:off_kv] * qscale, kv_cols(3), kv_cols(5), fox_q * qscale, fox_v],
                           axis=1).T.astype(BF16)
    w_tok = jnp.concatenate([kv_cols(0), kv_cols(1), lane_padded(kv_cols(2), g), lane_padded(kv_cols(4), g),
                             lane_padded(fox_k, nh)], axis=1).astype(BF16)
    n_small = c_gate + FOX_HEADS
    w_small = jnp.concatenate([w_in[:, off_gate:off_fox], w_in[:, off_fgt:off_merge],
                               jnp.zeros((d, LANES - n_small), F32)], axis=1).astype(BF16)
    w_merge = w_in[:, off_merge:].astype(BF16)

    u = _ln_mod(x2d, sc1, sh1, BF16)
    ch = _matmul_nt(w_ch, u, BF16, 512, "in_proj_channel_major")
    tok = _matmul(u, w_tok, BF16, 512, w_tok.shape[1] // 2, "in_proj_token_major")
    small = _matmul(u, w_small, F32, 512, LANES, "in_proj_small")
    merge = _matmul(u, w_merge, BF16, 512, 1024, "in_proj_merge")

    qt_nsa = ch[0:c_q].reshape(g, NSA_HPG, hd, s)
    ch = ch.reshape(-1, hd, s)
    ch_vs_block0 = NSA_HEADS
    ch_vw_block0 = ch_vs_block0 + g
    ch_fq_block0 = ch_vw_block0 + g
    ch_fv_block0 = ch_fq_block0 + nh
    tok_ks_block0 = 2 * gw // AUG_K
    tok_kw_block0 = tok_ks_block0 + g
    tok_fox_block0 = tok_kw_block0 + g

    kv_cmp_in = tok[:, 0:2 * gw].reshape(s, 2, g, hd).transpose(1, 2, 0, 3)
    kv_cmp = _compress(kv_cmp_in, cmp_pe, cmp_w1, cmp_b1, cmp_w2)
    nch = s // CMP_STRIDE
    ncp = nch + LANES
    nslc = s // SLC_LEN
    kv_cmp_pad = jnp.pad(kv_cmp.astype(BF16), ((0, 0), (0, 0), (CMP_PAD, ncp - nch - CMP_PAD), (0, 0)))
    b_cmp, b_sel, b_win, far3 = _nsa_bias_tables(t5_table)

    row_ix = np.arange(ncp)
    row_ok = (row_ix >= CMP_PAD) & (row_ix < nch - 1 + CMP_PAD)
    row_cols = np.zeros((ncp, 2 * AUG_K - hd), np.float32)
    row_cols[:, ROW_BIAS - hd:ROW_BIAS - hd + 3] = row_ok[:, None]
    row_cols[:, ROW_BIAS - hd + 3] = ~row_ok
    row_cols[row_ix, AUG_K - hd + row_ix // BF16_ROWS] = 1.0
    kc_aug = jnp.concatenate([kv_cmp_pad[0], jnp.broadcast_to(jnp.asarray(row_cols, BF16), (g,) + row_cols.shape)],
                             axis=-1)
    vct = kv_cmp_pad[1].transpose(0, 2, 1)
    nq = s // ATT_TQ
    qt = qt_nsa.reshape(g, NSA_HPG, hd, nq, ATT_TQ).transpose(0, 3, 2, 1, 4).reshape(g, nq, hd, NSA_HPG * ATT_TQ)
    oct_, selt = _nsa_compress_select(qt, kc_aug, vct, b_cmp, far3)

    period = BF16_ROWS * SLC_LEN
    k_extra = np.zeros((period, AUG_K), np.float32)
    k_extra[np.arange(period), ROW_MASK + np.arange(period) // SLC_LEN] = 1.0
    k_extra[:, ROW_BIAS:ROW_BIAS + 3] = 1.0
    gates = small[:, 0:c_gate].reshape(s, g, NSA_HPG, 3).transpose(1, 2, 0, 3)
    gates_t = jnp.pad(_to_lane_blocks(gates, ATT_TQ), ((0, 0), (0, 0), (0, 5), (0, 0)))
    o_nsa_t = _nsa_attention(qt, tok, tok_ks_block0, tok_kw_block0, jnp.asarray(k_extra, BF16), ch, ch_vs_block0,
                             ch_vw_block0, selt, far3, b_sel, b_win, oct_, gates_t)
    o_nsa = o_nsa_t.reshape(MIX_W, s)

    fgt_bias = jnp.concatenate([jnp.zeros((c_gate,), F32), b_fgt, jnp.zeros((LANES - n_small,), F32)])[None, :]
    decay_k, decay_ends = _decay_pieces(small, fgt_bias, c_gate, nh)
    o_fox = _fox_attention(ch, ch_fq_block0, ch_fv_block0, tok, tok_fox_block0, decay_k, decay_ends)
    o_fox = o_fox.reshape(MIX_W, s)

    x1 = _merge_project(o_nsa, o_fox, merge, x2d, w_br_nsa.astype(BF16), w_br_fox.astype(BF16),
                        w_o.astype(BF16), g1, ln1_g[None, :], ln1_b[None, :], alpha)

    n_r = N_GROUPS + N_EXPERTS
    w_r = jnp.concatenate([w_rg, w_re.reshape(d, N_EXPERTS), jnp.zeros((d, LANES - n_r), F32)], axis=1)
    b_r = jnp.concatenate([b_rg, b_re.reshape(N_EXPERTS), jnp.zeros((LANES - n_r,), F32)])[None, :]
    w_r_hi, w_r_lo, _ = _split3_exact(w_r)
    u2, route = _router(x1, sc2, sh2, jnp.stack([w_r_hi, w_r_lo]), b_r)
    blk_exp, n_used, row_tok, row_dst, row_w = _moe_dispatch(route, s)
    y2 = _moe_experts(u2, blk_exp, n_used, row_tok, row_dst, row_w, w_gate, w_up, w_down)
    return _final_ln(x1, y2, g2, ln2_g[None, :], ln2_b[None, :], alpha)


def kernel(x, c, w_ada, b_ada, w_in, b_fgt, t5_table, cmp_pe, cmp_w1, cmp_b1, cmp_w2, w_br_nsa, w_br_fox, w_o,
           ln1_g, ln1_b, w_rg, b_rg, w_re, b_re, w_gate, w_up, w_down, ln2_g, ln2_b):
    b, s, d = x.shape
    depth = w_ada.shape[0]
    assert b == 1
    alpha = (2 * depth) ** 0.25
    h = x[0]
    for l in range(depth):
        h = _layer(h, c, w_ada[l], b_ada[l], w_in[l], b_fgt[l], t5_table, cmp_pe[l], cmp_w1[l], cmp_b1[l],
                   cmp_w2[l], w_br_nsa[l], w_br_fox[l], w_o[l], ln1_g[l], ln1_b[l], w_rg[l], b_rg[l],
                   w_re[l], b_re[l], w_gate[l], w_up[l], w_down[l], ln2_g[l], ln2_b[l], alpha)
    return h[None]
```

```python
import functools
import math

import numpy as np
import jax
import jax.numpy as jnp
from jax import lax
from jax.experimental import pallas as pl
from jax.experimental.pallas import tpu as pltpu

F32 = jnp.float32
BF16 = jnp.bfloat16
HIGHEST = lax.Precision.HIGHEST
LOG2E = math.log2(math.e)

HEAD_DIM = 64
NSA_HEADS = 8
NSA_GROUPS = 2
NSA_HPG = NSA_HEADS // NSA_GROUPS
FOX_HEADS = 8
MIX_W = NSA_HEADS * HEAD_DIM
CMP_LEN = 32
CMP_STRIDE = 16
SLC_LEN = 64
SLC_TOPK = 16
WINDOW = 512
T5_BUCKETS = 32
T5_MAX_EXACT = 16
T5_MAX_DIST = 128
N_GROUPS = 8
EXPERTS_PER_GROUP = 8
N_EXPERTS = N_GROUPS * EXPERTS_PER_GROUP
ROW_BLOCK = 128
LN_EPS = 1e-5
NEG_INF = -1e30
M_INIT = -1e29
FORCE_SCORE = 1e4

LANES = 128
BF16_ROWS = 16
CMP_PAD = 8
ATT_TQ = 256
CMP_NEAR_ROWS = 32
FOX_TQ = 1024
FOX_TK = 512
FOX_SKIP_MARGIN = 160.0
FOX_BOUND_SLACK = 1.02
AUG_K = 128
AUG_V = HEAD_DIM + 16
ROW_MASK = HEAD_DIM
ROW_BIAS = HEAD_DIM + 16
VMEM_LIMIT = 56 * 1024 * 1024


def _cparams(sem, vmem=VMEM_LIMIT):
    return pltpu.CompilerParams(dimension_semantics=sem, vmem_limit_bytes=vmem)


def _sigmoid(x):
    return 1.0 / (1.0 + jnp.exp(-x))


def _layer_norm(x):
    mu = jnp.mean(x, axis=-1, keepdims=True)
    xc = x - mu
    var = jnp.mean(xc * xc, axis=-1, keepdims=True)
    return xc * lax.rsqrt(var + LN_EPS)


def _split3(x):
    hi = x.astype(BF16)
    r1 = x - hi.astype(F32)
    mid = r1.astype(BF16)
    lo = (r1 - mid.astype(F32)).astype(BF16)
    return hi, mid, lo


def _split3_exact(x):
    def trunc(v):
        bits = lax.bitcast_convert_type(v, jnp.uint32) & jnp.uint32(0xFFFF0000)
        return lax.bitcast_convert_type(bits, F32)
    hi = trunc(x)
    r1 = x - hi
    mid = trunc(r1)
    lo = r1 - mid
    return hi.astype(BF16), mid.astype(BF16), lo.astype(BF16)


def _dot(a, b):
    return jnp.dot(a, b, preferred_element_type=F32)


def _dot_nt(a, b):
    return lax.dot_general(a, b, (((1,), (1,)), ((), ())), preferred_element_type=F32)


def _dot3(x, w_bf16):
    hi, mid, lo = _split3(x)
    return _dot(hi, w_bf16) + _dot(mid, w_bf16) + _dot(lo, w_bf16)


def _dot3_rhs(w_bf16, x):
    hi, mid, lo = _split3(x)
    return _dot(w_bf16, hi) + _dot(w_bf16, mid) + _dot(w_bf16, lo)


def _ada_kernel(c_ref, w_ref, b_ref, o_ref):
    c = c_ref[...]
    a = c * _sigmoid(c)
    o_ref[...] = jnp.dot(a, w_ref[...], precision=HIGHEST, preferred_element_type=F32) + b_ref[...]


def _ada_mod(c, w, b):
    d, n = w.shape
    tn = 1024
    c8 = jnp.broadcast_to(c, (8, d))
    out = pl.pallas_call(
        _ada_kernel,
        grid=(n // tn,),
        in_specs=[pl.BlockSpec((8, d), lambda j: (0, 0)),
                  pl.BlockSpec((d, tn), lambda j: (0, j)),
                  pl.BlockSpec((1, tn), lambda j: (0, j))],
        out_specs=pl.BlockSpec((8, tn), lambda j: (0, j)),
        out_shape=jax.ShapeDtypeStruct((8, n), F32),
        compiler_params=_cparams(("parallel",)),
        name="ada_mod",
    )(c8, w, b.reshape(1, n))
    return out[0:1]


def _lnmod_kernel(x_ref, sc_ref, sh_ref, o_ref):
    y = _layer_norm(x_ref[...])
    o_ref[...] = (y * (1.0 + sc_ref[...]) + sh_ref[...]).astype(o_ref.dtype)


def _ln_mod(x, sc, sh, out_dtype, tm=512):
    m, d = x.shape
    return pl.pallas_call(
        _lnmod_kernel,
        grid=(m // tm,),
        in_specs=[pl.BlockSpec((tm, d), lambda i: (i, 0)),
                  pl.BlockSpec((1, d), lambda i: (0, 0)),
                  pl.BlockSpec((1, d), lambda i: (0, 0))],
        out_specs=pl.BlockSpec((tm, d), lambda i: (i, 0)),
        out_shape=jax.ShapeDtypeStruct((m, d), out_dtype),
        compiler_params=_cparams(("parallel",)),
        name="ln_mod",
    )(x, sc, sh)


def _mm_kernel(a_ref, w_ref, o_ref):
    o_ref[...] = _dot(a_ref[...], w_ref[...]).astype(o_ref.dtype)


def _matmul(a, w, out_dtype, tm, tn, name):
    m, k = a.shape
    n = w.shape[1]
    return pl.pallas_call(
        _mm_kernel,
        grid=(n // tn, m // tm),
        in_specs=[pl.BlockSpec((tm, k), lambda j, i: (i, 0)),
                  pl.BlockSpec((k, tn), lambda j, i: (0, j))],
        out_specs=pl.BlockSpec((tm, tn), lambda j, i: (i, j)),
        out_shape=jax.ShapeDtypeStruct((m, n), out_dtype),
        compiler_params=_cparams(("parallel", "parallel")),
        name=name,
    )(a, w)


def _mm_nt_kernel(w_ref, a_ref, o_ref):
    o_ref[...] = _dot_nt(w_ref[...], a_ref[...]).astype(o_ref.dtype)


def _matmul_nt(w_t, a, out_dtype, tm, name):
    n, k = w_t.shape
    m = a.shape[0]
    return pl.pallas_call(
        _mm_nt_kernel,
        grid=(m // tm,),
        in_specs=[pl.BlockSpec((n, k), lambda i: (0, 0), pipeline_mode=pl.Buffered(1)),
                  pl.BlockSpec((tm, k), lambda i: (i, 0))],
        out_specs=pl.BlockSpec((n, tm), lambda i: (0, i)),
        out_shape=jax.ShapeDtypeStruct((n, m), out_dtype),
        compiler_params=_cparams(("parallel",)),
        name=name,
    )(w_t, a)


def _gelu_tanh(x):
    return 0.5 * x * (1.0 + jnp.tanh(math.sqrt(2.0 / math.pi) * (x + 0.044715 * (x * x * x))))


def _compress_kernel(c_ref, pe_ref, w1a_ref, w1b_ref, b1_ref, w2_ref, o_ref, *, nch):
    c = c_ref[0, 0]
    w1a = w1a_ref[0]
    w1b = w1b_ref[0]
    half = CMP_STRIDE * HEAD_DIM
    a = _dot(c, w1a)
    b = _dot(c, w1b)
    b_next = pltpu.roll(b, shift=nch - 1, axis=0)
    pe = pe_ref[0]
    pb = _dot(pe[:, :half], w1a) + _dot(pe[:, half:], w1b)
    hid = _gelu_tanh(a + b_next + pb[0:1, :] + b1_ref[0])
    o_ref[0, 0] = _dot(hid.astype(BF16), w2_ref[0])


def _compress(kv_cmp, pe, w1, b1, w2):
    z, g, s, hd = kv_cmp.shape
    nch = s // CMP_STRIDE
    half = CMP_STRIDE * hd
    chunks = kv_cmp.reshape(z, g, nch, half)
    pe8 = jnp.broadcast_to(pe.reshape(z, 1, CMP_LEN * hd), (z, 8, CMP_LEN * hd)).astype(BF16)
    w1b16 = w1.astype(BF16)
    hidn = w1.shape[-1]
    return pl.pallas_call(
        functools.partial(_compress_kernel, nch=nch),
        grid=(z, g),
        in_specs=[pl.BlockSpec((1, 1, nch, half), lambda zi, gi: (zi, gi, 0, 0)),
                  pl.BlockSpec((1, 8, 2 * half), lambda zi, gi: (zi, 0, 0)),
                  pl.BlockSpec((1, half, hidn), lambda zi, gi: (zi, 0, 0)),
                  pl.BlockSpec((1, half, hidn), lambda zi, gi: (zi, 1, 0)),
                  pl.BlockSpec((1, 1, hidn), lambda zi, gi: (zi, 0, 0)),
                  pl.BlockSpec((1, hidn, hd), lambda zi, gi: (zi, 0, 0))],
        out_specs=pl.BlockSpec((1, 1, nch, hd), lambda zi, gi: (zi, gi, 0, 0)),
        out_shape=jax.ShapeDtypeStruct((z, g, nch, hd), F32),
        compiler_params=_cparams(("parallel", "parallel")),
        name="compress_kv",
    )(chunks, pe8, w1b16, w1b16, b1.reshape(z, 1, hidn), w2.astype(BF16))


def _t5_bucket_np(dist):
    n = np.maximum(dist, 0)
    ratio = np.log(np.maximum(n, T5_MAX_EXACT).astype(np.float64) / T5_MAX_EXACT)
    big = T5_MAX_EXACT + (ratio / math.log(T5_MAX_DIST / T5_MAX_EXACT)
                          * (T5_BUCKETS - T5_MAX_EXACT)).astype(np.int64)
    return np.where(n < T5_MAX_EXACT, n, np.minimum(big, T5_BUCKETS - 1)).astype(np.int32)


def _t5_lookup(tbh, dist):
    onehot = np.eye(T5_BUCKETS, dtype=np.float32)[_t5_bucket_np(dist).reshape(-1)]
    vals = jnp.einsum('ghb,nb->ghn', tbh, jnp.asarray(onehot), precision=HIGHEST)
    return vals.reshape(tbh.shape[:2] + dist.shape)


def _toeplitz_kernel(w_ref, o_ref, *, n_keys, tq, lead_zero, trail_masked):
    width = w_ref.shape[-1]
    rows = jnp.broadcast_to(w_ref[0, 0], (n_keys, width))
    rolled = pltpu.roll(rows, shift=width - (n_keys - 1), axis=1, stride=1, stride_axis=0)
    parts = [rolled[:, 0:tq]]
    if lead_zero:
        parts.insert(0, jnp.zeros((lead_zero, tq), F32))
    if trail_masked:
        parts.append(jnp.full((trail_masked, tq), NEG_INF, F32))
    o_ref[0] = jnp.concatenate(parts, axis=0)


def _att_table_t(tbh, n_keys, tq, lo, hi, minus_far, lead_zero=0, trail_masked=0):
    length = n_keys + tq - 1
    d = np.arange(length) - (tq - 1)
    valid = (d >= lo) & (d < hi)
    vals = _t5_lookup(tbh, d)
    if minus_far:
        vals = vals - tbh[:, :, T5_BUCKETS - 1:]
    w = jnp.where(jnp.asarray(valid), vals * LOG2E, NEG_INF)
    width = pl.next_power_of_2(length)
    w = jnp.pad(w, ((0, 0), (0, 0), (0, width - length)))[:, :, None, :]
    g, hpg = tbh.shape[:2]
    n_rows = lead_zero + n_keys + trail_masked
    return pl.pallas_call(
        functools.partial(_toeplitz_kernel, n_keys=n_keys, tq=tq, lead_zero=lead_zero, trail_masked=trail_masked),
        grid=(g, hpg),
        in_specs=[pl.BlockSpec((1, 1, 1, width), lambda gi, hi: (gi, hi, 0, 0))],
        out_specs=pl.BlockSpec((1, n_rows, tq), lambda gi, hi: (gi, 0, hi)),
        out_shape=jax.ShapeDtypeStruct((g, n_rows, hpg * tq), F32),
        compiler_params=_cparams(("parallel", "parallel")),
        name="t5_toeplitz",
    )(w)


def _nsa_bias_tables(t5_table):
    tbh = t5_table.T.reshape(NSA_GROUPS, NSA_HPG, T5_BUCKETS).astype(F32)
    j = np.arange(CMP_NEAR_ROWS)[:, None]
    i = np.arange(ATT_TQ)[None, :]
    dist = i - (CMP_LEN - 1) - CMP_STRIDE * (j - CMP_PAD)
    vals = (_t5_lookup(tbh, dist) - tbh[:, :, T5_BUCKETS - 1][:, :, None, None]) * LOG2E
    vals = jnp.where(jnp.asarray(dist >= 0)[None, None], vals, NEG_INF)
    b_cmp = vals.transpose(0, 2, 1, 3).reshape(NSA_GROUPS, CMP_NEAR_ROWS, NSA_HPG * ATT_TQ)
    far = tbh[:, :, T5_BUCKETS - 1] * LOG2E
    b_sel = _att_table_t(tbh, 2 * ATT_TQ, ATT_TQ, 0, 1 << 30, True, lead_zero=ATT_TQ)
    b_win = _att_table_t(tbh, 3 * ATT_TQ, ATT_TQ, 0, WINDOW, False, trail_masked=ATT_TQ)
    hi, mid, lo = _split3_exact(jnp.repeat(far, ATT_TQ, axis=1))
    far3 = jnp.stack([hi, mid, lo, jnp.full(hi.shape, NEG_INF, BF16)], axis=1)
    far3 = jnp.pad(far3, ((0, 0), (0, BF16_ROWS - 4), (0, 0)))
    return b_cmp, b_sel, b_win, far3


def _nsa_cmp_kernel(qt_ref, kc_ref, vct_ref, near_ref, far3_ref, oct_ref, selt_ref, rhs, s_scr, imp_scr, *, nslc):
    tq = ATT_TQ
    lanes = NSA_HPG * tq
    qb = pl.program_id(1)
    cpb = tq // CMP_STRIDE
    rhs[...] = jnp.zeros(rhs.shape, BF16)
    rhs[0:HEAD_DIM, :] = qt_ref[0, 0]
    rhs[ROW_BIAS:ROW_BIAS + BF16_ROWS, :] = far3_ref[0]
    chunk = lax.broadcasted_iota(jnp.int32, (AUG_K, lanes), 0)
    rhs[AUG_K:2 * AUG_K, :] = jnp.where(chunk >= qb + CMP_NEAR_ROWS // BF16_ROWS, NEG_INF, 0.0).astype(BF16)
    r0 = pl.multiple_of(cpb * qb, BF16_ROWS)
    n_lane_blocks = tq // LANES
    ncp = s_scr.shape[0]
    imp_scr[...] = jnp.zeros(imp_scr.shape, F32)

    def attend(rows):
        s_scr[0:rows, :] = _dot(kc_ref[0, 0:rows, :], rhs[...])
        s_scr[pl.ds(r0, CMP_NEAR_ROWS), :] = s_scr[pl.ds(r0, CMP_NEAR_ROWS), :] + near_ref[0]
        s = s_scr[0:rows, :]
        m = jnp.max(s, axis=0, keepdims=True)
        e = jnp.exp2(s - m)
        l = jnp.sum(e, axis=0, keepdims=True)
        p = e * jnp.where(m > M_INIT, 1.0 / l, 0.0)
        oct_ref[0, 0] = _dot(vct_ref[0, :, 0:rows], p.astype(BF16))
        imp = p[:, 0:tq]
        for h in range(1, NSA_HPG):
            imp = imp + p[:, h * tq:(h + 1) * tq]
        for c in range(n_lane_blocks):
            imp_scr[c, 0:rows, :] = imp[:, c * LANES:(c + 1) * LANES]

    limits = sorted({min(ncp, -(-(ncp * k // 3) // LANES) * LANES) for k in (1, 2, 3)})
    lo_qb = 0
    for rows in limits:
        hi_qb = (rows - CMP_NEAR_ROWS) // cpb if rows < ncp else pl.num_programs(1) - 1

        @pl.when((qb >= lo_qb) & (qb <= hi_qb))
        def _(rows=rows):
            attend(rows)

        lo_qb = hi_qb + 1
    ratio = SLC_LEN // CMP_STRIDE

    def taps(off):
        return jnp.concatenate([imp_scr[c, pl.ds(CMP_PAD + off, nslc, stride=ratio), :]
                                for c in range(n_lane_blocks)], axis=1)

    p_slc = 0.5 * (taps(-1) + taps(ratio - 1))
    for off in range(ratio - 1):
        p_slc = p_slc + taps(off)
    blk = lax.broadcasted_iota(jnp.int32, (nslc, tq), 0)
    cur = (qb * tq + lax.broadcasted_iota(jnp.int32, (nslc, tq), 1)) // SLC_LEN
    forced = (blk == 0) | (blk == cur) | (blk == cur - 1)
    score = jnp.where(forced, FORCE_SCORE, jnp.where(blk <= cur, p_slc, -1.0))
    blk_f = blk.astype(F32)
    sel = jnp.zeros((nslc, tq), F32)
    for _ in range(min(SLC_TOPK, nslc)):
        mx = jnp.max(score, axis=0, keepdims=True)
        first = jnp.min(jnp.where(score == mx, blk_f, float(nslc)), axis=0, keepdims=True)
        hit = blk_f == first
        sel = jnp.where(hit, 1.0, sel)
        score = jnp.where(hit, -2.0, score)
    selt_ref[0] = sel.astype(BF16)


def _nsa_compress_select(qt, kc_aug, vct, b_cmp, far3):
    g, nq, hd, lanes = qt.shape
    ncp = kc_aug.shape[1]
    tq = ATT_TQ
    nslc = nq * tq // SLC_LEN
    assert ncp // BF16_ROWS <= AUG_K
    assert tq // CMP_STRIDE == BF16_ROWS
    per_q = lambda gi, qi: (gi, qi, 0, 0)
    per_g = lambda gi, qi: (gi, 0, 0)
    return pl.pallas_call(
        functools.partial(_nsa_cmp_kernel, nslc=nslc),
        grid=(g, nq),
        in_specs=[pl.BlockSpec((1, 1, hd, lanes), per_q),
                  pl.BlockSpec((1, ncp, 2 * AUG_K), per_g),
                  pl.BlockSpec((1, hd, ncp), per_g),
                  pl.BlockSpec((1, CMP_NEAR_ROWS, lanes), per_g),
                  pl.BlockSpec((1, BF16_ROWS, lanes), per_g)],
        out_specs=[pl.BlockSpec((1, 1, hd, lanes), per_q),
                   pl.BlockSpec((1, nslc, tq), lambda gi, qi: (gi, 0, qi))],
        out_shape=[jax.ShapeDtypeStruct((g, nq, hd, lanes), F32),
                   jax.ShapeDtypeStruct((g, nslc, nq * tq), BF16)],
        scratch_shapes=[pltpu.VMEM((2 * AUG_K, lanes), BF16), pltpu.VMEM((ncp, lanes), F32),
                        pltpu.VMEM((tq // LANES, ncp, LANES), F32)],
        compiler_params=_cparams(("parallel", "parallel")),
        name="nsa_compress_select",
    )(qt, kc_aug, vct, b_cmp, far3)


def _flash_init_t(m_ref, acc_ref):
    m_ref[...] = jnp.full(m_ref.shape, M_INIT, F32)
    acc_ref[...] = jnp.zeros(acc_ref.shape, F32)


def _flash_step_t(s, vt_tile, m_ref, acc_ref):
    m_old = m_ref[...]
    m_new = jnp.maximum(m_old, jnp.max(s, axis=0, keepdims=True))
    p = jnp.exp2(s - m_new).astype(BF16)
    acc_ref[...] = jnp.exp2(m_old - m_new) * acc_ref[...] + _dot(vt_tile, p)
    m_ref[...] = m_new


def _flash_result_t(acc_ref):
    acc = acc_ref[...]
    return acc[0:HEAD_DIM, :] / acc[HEAD_DIM:HEAD_DIM + 1, :]


def _nsa_att_kernel(qt_ref, ks_ref, kx_ref, vst_ref, kw_ref, vwt_ref, selt_ref, far3_ref, bsel_ref, bwin_ref,
                    oct_ref, gate_ref, o_ref, rhs_s, rhs_w, mask_t, ms, accs, mw, accw, s_even, s_odd):
    tq = ATT_TQ
    qb = pl.program_id(1)
    ones_rows = _ones_rows(tq)
    kx_tiles = kx_ref.shape[0] // tq
    qt = qt_ref[0, 0]
    rhs_s[...] = jnp.zeros(rhs_s.shape, BF16)
    rhs_s[0:HEAD_DIM, :] = qt
    rhs_s[ROW_BIAS:ROW_BIAS + BF16_ROWS, :] = far3_ref[0]
    rhs_w[...] = jnp.zeros(rhs_w.shape, BF16)
    rhs_w[0:HEAD_DIM, :] = qt
    madd = ((selt_ref[0].astype(F32) - 1.0) * (-NEG_INF)).astype(BF16)
    mask_t[...] = jnp.concatenate([madd] * NSA_HPG, axis=1)
    _flash_init_t(ms, accs)
    _flash_init_t(mw, accw)
    blocks_per_tile = tq // SLC_LEN

    def sel_scores(kt, s_ref):
        kt = jnp.minimum(kt, qb)
        k0 = pl.multiple_of(kt * tq, tq)
        chunk = pl.multiple_of((kt * blocks_per_tile) // BF16_ROWS * BF16_ROWS, BF16_ROWS)
        rhs_s[ROW_MASK:ROW_MASK + BF16_ROWS, :] = mask_t[pl.ds(chunk, BF16_ROWS), :]
        rel = jnp.clip(kt - qb + 2, 0, 2)
        table = bsel_ref[0, pl.ds(pl.multiple_of(rel * tq, tq), tq), :]
        k_aug = ks_ref[pl.ds(k0, tq), :] + kx_ref[pl.ds(pl.multiple_of((kt % kx_tiles) * tq, tq), tq), :]
        s_ref[...] = _dot(k_aug, rhs_s[...]) + table

    def sel_consume(kt, s_ref):
        k0 = pl.multiple_of(kt * tq, tq)
        vt_aug = jnp.concatenate([vst_ref[0, :, pl.ds(k0, tq)], ones_rows], axis=0)
        _flash_step_t(s_ref[...], vt_aug, ms, accs)

    def win_scores(j, s_ref):
        kt = qb - 2 + j
        k0 = pl.multiple_of(jnp.maximum(kt, 0) * tq, tq)
        row = pl.multiple_of(jnp.where(kt >= 0, j, 3) * tq, tq)
        s_ref[...] = _dot(kw_ref[pl.ds(k0, tq), :], rhs_w[...]) + bwin_ref[0, pl.ds(row, tq), :]

    def win_consume(j, s_ref):
        k0 = pl.multiple_of(jnp.maximum(qb - 2 + j, 0) * tq, tq)
        vt_aug = jnp.concatenate([vwt_ref[0, :, pl.ds(k0, tq)], ones_rows], axis=0)
        _flash_step_t(s_ref[...], vt_aug, mw, accw)

    n_sel = qb + 1
    sel_scores(0, s_even)

    def pair_body(j, carry):
        sel_scores(2 * j + 1, s_odd)
        sel_consume(2 * j, s_even)
        sel_scores(2 * j + 2, s_even)
        sel_consume(2 * j + 1, s_odd)
        return carry

    lax.fori_loop(0, n_sel // 2, pair_body, 0)
    win_scores(0, s_odd)

    @pl.when(n_sel % 2 == 1)
    def _():
        sel_consume(qb, s_even)

    win_scores(1, s_even)
    win_consume(0, s_odd)
    win_scores(2, s_odd)
    win_consume(1, s_even)
    win_consume(2, s_odd)

    gt = _sigmoid(gate_ref[0, 0])
    out = gt[0:1, :] * oct_ref[0, 0] + gt[1:2, :] * _flash_result_t(accs) + gt[2:3, :] * _flash_result_t(accw)
    for h in range(NSA_HPG):
        o_ref[0, h] = out[:, h * tq:(h + 1) * tq].astype(o_ref.dtype)


def _nsa_attention(qt, k_tok, ks_block0, kw_block0, k_extra, ch, vs_block0, vw_block0, selt, far3, b_sel, b_win,
                   oct_, gates_t):
    g, nq, hd, lanes = qt.shape
    s = k_tok.shape[0]
    nslc = selt.shape[1]
    tq = ATT_TQ
    resident = pl.Buffered(1)
    per_q = lambda gi, qi: (gi, qi, 0, 0)
    per_g = lambda gi, qi: (gi, 0, 0)
    return pl.pallas_call(
        _nsa_att_kernel,
        grid=(g, nq),
        in_specs=[pl.BlockSpec((1, 1, hd, lanes), per_q),
                  pl.BlockSpec((s, AUG_K), lambda gi, qi: (0, ks_block0 + gi), pipeline_mode=resident),
                  pl.BlockSpec(k_extra.shape, lambda gi, qi: (0, 0), pipeline_mode=resident),
                  pl.BlockSpec((1, hd, s), lambda gi, qi: (vs_block0 + gi, 0, 0), pipeline_mode=resident),
                  pl.BlockSpec((s, AUG_K), lambda gi, qi: (0, kw_block0 + gi), pipeline_mode=resident),
                  pl.BlockSpec((1, hd, s), lambda gi, qi: (vw_block0 + gi, 0, 0), pipeline_mode=resident),
                  pl.BlockSpec((1, nslc, tq), lambda gi, qi: (gi, 0, qi)),
                  pl.BlockSpec((1, BF16_ROWS, lanes), per_g),
                  pl.BlockSpec((1, 3 * tq, lanes), per_g, pipeline_mode=resident),
                  pl.BlockSpec((1, 4 * tq, lanes), per_g, pipeline_mode=resident),
                  pl.BlockSpec((1, 1, hd, lanes), per_q),
                  pl.BlockSpec((1, 1, 8, lanes), per_q)],
        out_specs=pl.BlockSpec((1, lanes // tq, hd, tq), lambda gi, qi: (gi, 0, 0, qi)),
        out_shape=jax.ShapeDtypeStruct((g, lanes // tq, hd, s), BF16),
        scratch_shapes=[pltpu.VMEM((AUG_K, lanes), BF16), pltpu.VMEM((AUG_K, lanes), BF16),
                        pltpu.VMEM((nslc, lanes), BF16),
                        pltpu.VMEM((1, lanes), F32), pltpu.VMEM((AUG_V, lanes), F32),
                        pltpu.VMEM((1, lanes), F32), pltpu.VMEM((AUG_V, lanes), F32),
                        pltpu.VMEM((tq, lanes), F32), pltpu.VMEM((tq, lanes), F32)],
        compiler_params=_cparams(("arbitrary", "arbitrary")),
        name="nsa_select_window",
    )(qt, k_tok, k_extra, ch, k_tok, ch, selt, far3, b_sel, b_win, oct_, gates_t)


def _decay_kernel(z_ref, b_ref, place_ref, o_ref, end_ref, carry_ref, *, tb):
    @pl.when(pl.program_id(0) == 0)
    def _():
        carry_ref[...] = jnp.zeros(carry_ref.shape, F32)

    z = z_ref[...] + b_ref[...]
    log_f = jnp.minimum(z, 0.0) - jnp.log1p(jnp.exp(-jnp.abs(z)))
    r = lax.broadcasted_iota(jnp.int32, (tb, tb), 0)
    c = lax.broadcasted_iota(jnp.int32, (tb, tb), 1)
    tri = jnp.where(r >= c, 1.0, 0.0).astype(BF16)
    run = _dot3_rhs(tri, log_f) + carry_ref[...]
    carry_ref[...] = run[tb - 1:tb, :]
    val = -run * LOG2E
    hi, mid, lo = _split3_exact(val)
    o_ref[...] = (_dot(hi, place_ref[0]) + _dot(mid, place_ref[1]) + _dot(lo, place_ref[2])).astype(BF16)
    end_ref[...] = jnp.broadcast_to(val[tb - 1:tb, :], end_ref.shape)


def _decay_pieces(z, bias, first_lane, n_heads):
    s, n = z.shape
    tb = FOX_TK
    place = np.zeros((3, n, n_heads * AUG_K), np.float32)
    for h in range(n_heads):
        for piece in range(3):
            place[piece, first_lane + h, h * AUG_K + HEAD_DIM + piece] = 1.0
    pieces, ends = pl.pallas_call(
        functools.partial(_decay_kernel, tb=tb),
        grid=(s // tb,),
        in_specs=[pl.BlockSpec((tb, n), lambda i: (i, 0)),
                  pl.BlockSpec((1, n), lambda i: (0, 0)),
                  pl.BlockSpec((3, n, n_heads * AUG_K), lambda i: (0, 0, 0))],
        out_specs=[pl.BlockSpec((tb, n_heads * AUG_K), lambda i: (i, 0)),
                   pl.BlockSpec((8, n), lambda i: (i, 0))],
        out_shape=[jax.ShapeDtypeStruct((s, n_heads * AUG_K), BF16),
                   jax.ShapeDtypeStruct((s // tb * 8, n), F32)],
        scratch_shapes=[pltpu.VMEM((1, n), F32)],
        compiler_params=_cparams(("arbitrary",)),
        name="decay_cumsum",
    )(z, bias, jnp.asarray(place, BF16))
    return pieces, ends[::8, first_lane:first_lane + n_heads].T


def _ones_rows(width):
    return jnp.where(lax.broadcasted_iota(jnp.int32, (BF16_ROWS, width), 0) < 8, 1.0, 0.0).astype(BF16)


def _fox_kernel(ends_ref, qt_ref, k_ref, dk_ref, vt_ref, o_ref, rhs, m_ref, acc_ref, s_even, s_odd, kmax_ref):
    tq = FOX_TQ
    tk = FOX_TK
    assert tq == 2 * tk
    qb = pl.program_id(1)
    row = lax.broadcasted_iota(jnp.int32, (AUG_K - HEAD_DIM, tq), 0)
    rhs[0:HEAD_DIM, :] = qt_ref[0]
    rhs[HEAD_DIM:AUG_K, :] = jnp.where(row < 3, 1.0, 0.0).astype(BF16)
    ones_rows = _ones_rows(tk)
    _flash_init_t(m_ref, acc_ref)

    def scores(kt, s_ref):
        k0 = pl.multiple_of(kt * tk, tk)
        k_aug = k_ref[pl.ds(k0, tk), :] + dk_ref[pl.ds(k0, tk), :]
        s_ref[...] = _dot(k_aug, rhs[...])

    def consume(kt, s_ref, diagonal):
        k0 = pl.multiple_of(kt * tk, tk)
        s = s_ref[...]
        if diagonal:
            key = k0 + lax.broadcasted_iota(jnp.int32, (tk, tq), 0)
            qry = qb * tq + lax.broadcasted_iota(jnp.int32, (tk, tq), 1)
            s = jnp.where(key <= qry, s, NEG_INF)
        vt_aug = jnp.concatenate([vt_ref[0, :, pl.ds(k0, tk)], ones_rows], axis=0)
        _flash_step_t(s, vt_aug, m_ref, acc_ref)

    @pl.when(qb == 0)
    def _():
        ones = jnp.ones((AUG_K, AUG_K), BF16)

        def norm_tile(c, best):
            k = k_ref[pl.ds(pl.multiple_of(c * tk, tk), tk), :].astype(F32)
            return jnp.maximum(best, _dot((k * k).astype(BF16), ones))

        best = lax.fori_loop(0, k_ref.shape[0] // tk, norm_tile, jnp.zeros((tk, AUG_K), F32))
        kmax_ref[0] = jnp.max(jnp.sqrt(best * FOX_BOUND_SLACK))

    scores(2 * qb, s_even)
    scores(2 * qb + 1, s_odd)
    consume(2 * qb, s_even, True)
    consume(2 * qb + 1, s_odd, True)

    q = qt_ref[0].astype(F32)
    q_norm = jnp.max(jnp.sqrt(jnp.sum(q * q, axis=0, keepdims=True) * FOX_BOUND_SLACK))
    threshold = jnp.min(m_ref[...]) - FOX_SKIP_MARGIN - q_norm * kmax_ref[0]
    head = pl.program_id(0)

    def first_needed(j, first):
        return jnp.where(ends_ref[head, 2 * j + 1] >= threshold, jnp.minimum(first, j), first)

    j0 = lax.fori_loop(0, qb, first_needed, qb)

    scores(2 * j0, s_even)

    def pair_body(j, carry):
        scores(2 * j + 1, s_odd)
        consume(2 * j, s_even, False)
        scores(2 * j + 2, s_even)
        consume(2 * j + 1, s_odd, False)
        return carry

    lax.fori_loop(j0, qb, pair_body, 0)
    o_ref[0] = _flash_result_t(acc_ref).astype(o_ref.dtype)


def _fox_attention(ch, q_block0, v_block0, k_tok, k_block0, decay_k, decay_ends):
    _, hd, s = ch.shape
    h = decay_ends.shape[0]
    tq = FOX_TQ
    grid_spec = pltpu.PrefetchScalarGridSpec(
        num_scalar_prefetch=1,
        grid=(h, s // tq),
        in_specs=[pl.BlockSpec((1, hd, tq), lambda hi, qi, ends: (q_block0 + hi, 0, qi)),
                  pl.BlockSpec((s, AUG_K), lambda hi, qi, ends: (0, k_block0 + hi)),
                  pl.BlockSpec((s, AUG_K), lambda hi, qi, ends: (0, hi)),
                  pl.BlockSpec((1, hd, s), lambda hi, qi, ends: (v_block0 + hi, 0, 0))],
        out_specs=pl.BlockSpec((1, hd, tq), lambda hi, qi, ends: (hi, 0, qi)),
        scratch_shapes=[pltpu.VMEM((AUG_K, tq), BF16),
                        pltpu.VMEM((1, tq), F32), pltpu.VMEM((AUG_V, tq), F32),
                        pltpu.VMEM((FOX_TK, tq), F32), pltpu.VMEM((FOX_TK, tq), F32),
                        pltpu.SMEM((1,), F32)],
    )
    return pl.pallas_call(
        _fox_kernel,
        grid_spec=grid_spec,
        out_shape=jax.ShapeDtypeStruct((h, hd, s), BF16),
        compiler_params=_cparams(("arbitrary", "arbitrary")),
        name="fox_attention",
    )(decay_ends, ch, k_tok, decay_k, ch)


def _merge_kernel(on_ref, of_ref, mg_ref, x_ref, wn_ref, wf_ref, wo_ref, g1_ref, lg_ref, lb_ref,
                  sc2_ref, sh2_ref, wr_ref, br_ref, o_ref, u_ref, r_ref, *, alpha):
    d = x_ref.shape[-1]
    tn = (((0,), (0,)), ((), ()))
    a = lax.dot_general(on_ref[...], wn_ref[...], tn, preferred_element_type=F32)
    b = lax.dot_general(of_ref[...], wf_ref[...], tn, preferred_element_type=F32)
    gm = _sigmoid(mg_ref[...].astype(F32))
    merged = gm[:, 0:d] * a + gm[:, d:2 * d] * b
    y = _dot(merged.astype(BF16), wo_ref[...])
    z = alpha * x_ref[...] + (1.0 + g1_ref[...]) * y
    x1 = _layer_norm(z) * lg_ref[...] + lb_ref[...]
    o_ref[...] = x1
    _route_tile(x1, sc2_ref, sh2_ref, wr_ref, br_ref, u_ref, r_ref)


def _merge_project(o_nsa, o_fox, merge, x, wn, wf, wo, g1, ln_g, ln_b, sc2, sh2, w_r, b_r, alpha, tm=256):
    m, d = x.shape
    w = o_nsa.shape[0]
    resident = pl.Buffered(1)
    row = lambda i: (i, 0)
    fixed = lambda i: (0, 0)
    return pl.pallas_call(
        functools.partial(_merge_kernel, alpha=alpha),
        grid=(m // tm,),
        in_specs=[pl.BlockSpec((w, tm), lambda i: (0, i)), pl.BlockSpec((w, tm), lambda i: (0, i)),
                  pl.BlockSpec((tm, 2 * d), row), pl.BlockSpec((tm, d), row),
                  pl.BlockSpec((w, d), fixed, pipeline_mode=resident),
                  pl.BlockSpec((w, d), fixed, pipeline_mode=resident),
                  pl.BlockSpec((d, d), fixed, pipeline_mode=resident),
                  pl.BlockSpec((1, d), fixed), pl.BlockSpec((1, d), fixed), pl.BlockSpec((1, d), fixed),
                  pl.BlockSpec((1, d), fixed), pl.BlockSpec((1, d), fixed),
                  pl.BlockSpec((2, d, LANES), lambda i: (0, 0, 0)), pl.BlockSpec((1, LANES), fixed)],
        out_specs=[pl.BlockSpec((tm, d), row), pl.BlockSpec((tm, d), row), pl.BlockSpec((tm, LANES), row)],
        out_shape=[jax.ShapeDtypeStruct((m, d), F32), jax.ShapeDtypeStruct((m, d), F32),
                   jax.ShapeDtypeStruct((m, LANES), F32)],
        compiler_params=_cparams(("parallel",)),
        name="merge_project_ln_route",
    )(o_nsa, o_fox, merge, x, wn, wf, wo, g1, ln_g, ln_b, sc2, sh2, w_r, b_r)


def _route_tile(x1, sc_ref, sh_ref, w_ref, b_ref, u_ref, r_ref):
    u = _layer_norm(x1) * (1.0 + sc_ref[...]) + sh_ref[...]
    u_ref[...] = u
    u_hi = u.astype(BF16)
    u_lo = (u - u_hi.astype(F32)).astype(BF16)
    logits = _dot(u_hi, w_ref[0]) + _dot(u_lo, w_ref[0]) + _dot(u_hi, w_ref[1]) + b_ref[...]
    lane = lax.broadcasted_iota(jnp.int32, (1, LANES), 1).astype(F32)
    none = float(LANES)
    is_g = lane < N_GROUPS
    lg = jnp.where(is_g, logits, NEG_INF)
    eg = jnp.exp(lg - jnp.max(lg, axis=-1, keepdims=True))
    pg = eg / jnp.sum(eg, axis=-1, keepdims=True)
    p_grp = jnp.max(pg, axis=-1, keepdims=True)
    grp = jnp.min(jnp.where(pg == p_grp, lane, none), axis=-1, keepdims=True)
    lo = N_GROUPS + grp * EXPERTS_PER_GROUP
    is_e = (lane >= lo) & (lane < lo + EXPERTS_PER_GROUP)
    le = jnp.where(is_e, logits, NEG_INF)
    ee = jnp.exp(le - jnp.max(le, axis=-1, keepdims=True))
    pe = jnp.where(is_e, ee / jnp.sum(ee, axis=-1, keepdims=True), -1.0)
    p1 = jnp.max(pe, axis=-1, keepdims=True)
    i1 = jnp.min(jnp.where(pe == p1, lane, none), axis=-1, keepdims=True)
    pe2 = jnp.where(lane == i1, -1.0, pe)
    p2 = jnp.max(pe2, axis=-1, keepdims=True)
    i2 = jnp.min(jnp.where(pe2 == p2, lane, none), axis=-1, keepdims=True)
    den = p1 + p2
    r_ref[...] = jnp.where(lane == 0, i1 - N_GROUPS,
                           jnp.where(lane == 1, i2 - N_GROUPS,
                                     jnp.where(lane == 2, p_grp * p1 / den,
                                               jnp.where(lane == 3, p_grp * p2 / den, 0.0))))


def _moe_kernel(be_ref, nu_ref, tok_ref, tok_next_ref, dst_ref, rw_ref, u_hbm, wg_ref, wu_ref, wd_ref, out_hbm,
                xbuf, ybuf, wgb, wub, wdb, sem_in, sem_out, *, n_dump0):
    rb = ROW_BLOCK
    i = pl.program_id(0)
    last = nu_ref[0] - 1
    slot = i % 2

    def row_in(r, tok, sl):
        return pltpu.make_async_copy(u_hbm.at[pl.ds(tok, 1), :], xbuf.at[sl, pl.ds(r, 1), :], sem_in.at[sl])

    def row_out(r, dst):
        return pltpu.make_async_copy(ybuf.at[pl.ds(r, 1), :], out_hbm.at[pl.ds(dst, 1), :], sem_out)

    @pl.when(i == 0)
    def _():
        for r in range(rb):
            row_in(r, tok_ref[0, 0, r], 0).start()
        ybuf[...] = jnp.zeros(ybuf.shape, F32)
        pltpu.make_async_copy(ybuf, out_hbm.at[pl.ds(n_dump0, rb), :], sem_out).start()

    prev = be_ref[jnp.maximum(i - 1, 0)]

    @pl.when((i <= last) & ((i == 0) | (be_ref[i] != prev)))
    def _():
        wgb[...] = wg_ref[0].astype(BF16)
        wub[...] = wu_ref[0].astype(BF16)
        wdb[...] = wd_ref[0].astype(BF16)

    @pl.when(i <= last)
    def _():
        for r in range(rb):
            row_in(r, 0, slot).wait()
        xb = xbuf[slot].astype(BF16)
        for r in range(rb):
            row_in(r, tok_next_ref[0, 0, r], 1 - slot).start()
        gate = _dot(xb, wgb[...])
        up = _dot(xb, wub[...])
        hid = (gate * _sigmoid(gate)) * up
        y = _dot(hid.astype(BF16), wdb[...]) * rw_ref[0]
        for r in range(rb):
            row_out(r, 0).wait()
        ybuf[...] = y
        for r in range(rb):
            row_out(r, dst_ref[0, 0, r]).start(priority=r % 2)

    @pl.when(i == last)
    def _():
        for r in range(rb):
            row_out(r, 0).wait()
        for r in range(rb):
            row_in(r, 0, 1 - slot).wait()


def _moe_experts(u, blk_exp, n_used, row_tok, row_dst, row_w, w_gate, w_up, w_down):
    t, d = u.shape
    n_blocks = blk_exp.shape[0]
    de = w_gate.shape[-1]
    rb = ROW_BLOCK
    tok3 = row_tok.reshape(n_blocks, 1, rb)
    grid_spec = pltpu.PrefetchScalarGridSpec(
        num_scalar_prefetch=2,
        grid=(n_blocks,),
        in_specs=[pl.BlockSpec((1, 1, rb), lambda i, be, nu: (i, 0, 0), memory_space=pltpu.SMEM),
                  pl.BlockSpec((1, 1, rb), lambda i, be, nu: (jnp.minimum(i + 1, n_blocks - 1), 0, 0),
                               memory_space=pltpu.SMEM),
                  pl.BlockSpec((1, 1, rb), lambda i, be, nu: (i, 0, 0), memory_space=pltpu.SMEM),
                  pl.BlockSpec((1, rb, 1), lambda i, be, nu: (i, 0, 0)),
                  pl.BlockSpec(memory_space=pl.ANY),
                  pl.BlockSpec((1, d, de), lambda i, be, nu: (be[i], 0, 0)),
                  pl.BlockSpec((1, d, de), lambda i, be, nu: (be[i], 0, 0)),
                  pl.BlockSpec((1, de, d), lambda i, be, nu: (be[i], 0, 0))],
        out_specs=pl.BlockSpec(memory_space=pl.ANY),
        scratch_shapes=[pltpu.VMEM((2, rb, d), F32), pltpu.VMEM((rb, d), F32),
                        pltpu.VMEM((d, de), BF16), pltpu.VMEM((d, de), BF16), pltpu.VMEM((de, d), BF16),
                        pltpu.SemaphoreType.DMA((2,)), pltpu.SemaphoreType.DMA(())],
    )
    return pl.pallas_call(
        functools.partial(_moe_kernel, n_dump0=2 * t),
        grid_spec=grid_spec,
        out_shape=jax.ShapeDtypeStruct((2 * t + rb, d), F32),
        compiler_params=_cparams(("arbitrary",)),
        name="moe_experts",
    )(blk_exp, n_used, tok3, tok3, row_dst.reshape(n_blocks, 1, rb), row_w.reshape(n_blocks, rb, 1),
      u, w_gate, w_up, w_down)


def _moe_dispatch(route, t):
    k = 2
    eid = route[:, 0:k].astype(jnp.int32).reshape(-1)
    wts = route[:, k:2 * k].reshape(-1)
    n_asg = t * k
    n_rows = n_asg + N_EXPERTS * ROW_BLOCK
    n_blocks = n_rows // ROW_BLOCK
    onehot = (eid[:, None] == jnp.arange(N_EXPERTS, dtype=jnp.int32)[None, :]).astype(jnp.int32)
    rank = jnp.sum((jnp.cumsum(onehot, axis=0) - onehot) * onehot, axis=1)
    counts = jnp.sum(onehot, axis=0)
    padded = (counts + ROW_BLOCK - 1) // ROW_BLOCK * ROW_BLOCK
    pad_end = jnp.cumsum(padded)
    pad_start = pad_end - padded
    dest = jnp.sum(onehot * pad_start[None, :], axis=1) + rank
    asg = jnp.arange(n_asg, dtype=jnp.int32)
    upd = jnp.stack([(asg % k) * t + asg // k, lax.bitcast_convert_type(wts, jnp.int32)], axis=1)
    init = jnp.stack([n_asg + jnp.arange(n_rows, dtype=jnp.int32) % ROW_BLOCK,
                      jnp.zeros((n_rows,), jnp.int32)], axis=1)
    rows = init.at[dest].set(upd)
    row_dst = rows[:, 0]
    row_w = lax.bitcast_convert_type(rows[:, 1], F32)
    row_tok = jnp.where(row_dst < n_asg, row_dst % t, 0)
    blk_start = jnp.arange(n_blocks, dtype=jnp.int32) * ROW_BLOCK
    blk_exp = jnp.minimum(jnp.sum((pad_end[None, :] <= blk_start[:, None]).astype(jnp.int32), axis=1),
                          N_EXPERTS - 1)
    n_used = (pad_end[N_EXPERTS - 1:] // ROW_BLOCK).astype(jnp.int32)
    return blk_exp, n_used, row_tok, row_dst, row_w


def _final_kernel(x_ref, y0_ref, y1_ref, g2_ref, lg_ref, lb_ref, o_ref, *, alpha):
    z = alpha * x_ref[...] + (1.0 + g2_ref[...]) * (y0_ref[...] + y1_ref[...])
    o_ref[...] = _layer_norm(z) * lg_ref[...] + lb_ref[...]


def _final_ln(x1, y2, g2, ln_g, ln_b, alpha, tm=512):
    m, d = x1.shape
    nb = m // tm
    fixed = lambda i: (0, 0)
    return pl.pallas_call(
        functools.partial(_final_kernel, alpha=alpha),
        grid=(nb,),
        in_specs=[pl.BlockSpec((tm, d), lambda i: (i, 0)),
                  pl.BlockSpec((tm, d), lambda i: (i, 0)),
                  pl.BlockSpec((tm, d), lambda i: (i + nb, 0)),
                  pl.BlockSpec((1, d), fixed), pl.BlockSpec((1, d), fixed), pl.BlockSpec((1, d), fixed)],
        out_specs=pl.BlockSpec((tm, d), lambda i: (i, 0)),
        out_shape=jax.ShapeDtypeStruct((m, d), F32),
        compiler_params=_cparams(("parallel",)),
        name="final_ln",
    )(x1, y2, y2, g2, ln_g, ln_b)


def _to_lane_blocks(a, tq):
    g, hpg, s, c = a.shape
    return a.reshape(g, hpg, s // tq, tq, c).transpose(0, 2, 4, 1, 3).reshape(g, s // tq, c, hpg * tq)


def _layer(x2d, c, w_ada, b_ada, w_in, b_fgt, t5_table, cmp_pe, cmp_w1, cmp_b1, cmp_w2,
           w_br_nsa, w_br_fox, w_o, ln1_g, ln1_b, w_rg, b_rg, w_re, b_re,
           w_gate, w_up, w_down, ln2_g, ln2_b, alpha):
    s, d = x2d.shape
    hd = HEAD_DIM
    g = NSA_GROUPS
    mod = _ada_mod(c, w_ada, b_ada)
    sh1, sc1, g1, sh2, sc2, g2 = [mod[:, i * d:(i + 1) * d] for i in range(6)]

    c_q = NSA_HEADS * hd
    c_kv = 6 * g * hd
    c_gate = 3 * NSA_HEADS
    c_fox = 3 * FOX_HEADS * hd
    off_kv = c_q
    off_gate = off_kv + c_kv
    off_fox = off_gate + c_gate
    off_fgt = off_fox + c_fox
    off_merge = off_fgt + FOX_HEADS
    qscale = hd ** -0.5 * LOG2E
    nh = FOX_HEADS
    gw = g * hd

    def kv_cols(z):
        return w_in[:, off_kv + z * gw:off_kv + (z + 1) * gw]

    def lane_padded(w, heads):
        return jnp.pad(w.reshape(d, heads, hd), ((0, 0), (0, 0), (0, AUG_K - hd))).reshape(d, heads * AUG_K)

    fox_q, fox_k, fox_v = [w_in[:, off_fox + i * nh * hd:off_fox + (i + 1) * nh * hd] for i in range(3)]
    w_ch = jnp.concatenate([w_in[:, 0:off_kv] * qscale, kv_cols(3), kv_cols(5), fox_q * qscale, fox_v],
                           axis=1).T.astype(BF16)
    w_tok = jnp.concatenate([kv_cols(0), kv_cols(1), lane_padded(kv_cols(2), g), lane_padded(kv_cols(4), g),
                             lane_padded(fox_k, nh)], axis=1).astype(BF16)
    n_small = c_gate + FOX_HEADS
    w_small = jnp.concatenate([w_in[:, off_gate:off_fox], w_in[:, off_fgt:off_merge],
                               jnp.zeros((d, LANES - n_small), F32)], axis=1).astype(BF16)
    w_merge = w_in[:, off_merge:].astype(BF16)

    u = _ln_mod(x2d, sc1, sh1, BF16)
    ch = _matmul_nt(w_ch, u, BF16, 512, "in_proj_channel_major")
    tok = _matmul(u, w_tok, BF16, 512, w_tok.shape[1] // 2, "in_proj_token_major")
    small = _matmul(u, w_small, F32, 512, LANES, "in_proj_small")
    merge = _matmul(u, w_merge, BF16, 512, 1024, "in_proj_merge")

    qt_nsa = ch[0:c_q].reshape(g, NSA_HPG, hd, s)
    ch = ch.reshape(-1, hd, s)
    ch_vs_block0 = NSA_HEADS
    ch_vw_block0 = ch_vs_block0 + g
    ch_fq_block0 = ch_vw_block0 + g
    ch_fv_block0 = ch_fq_block0 + nh
    tok_ks_block0 = 2 * gw // AUG_K
    tok_kw_block0 = tok_ks_block0 + g
    tok_fox_block0 = tok_kw_block0 + g

    kv_cmp_in = tok[:, 0:2 * gw].reshape(s, 2, g, hd).transpose(1, 2, 0, 3)
    kv_cmp = _compress(kv_cmp_in, cmp_pe, cmp_w1, cmp_b1, cmp_w2)
    nch = s // CMP_STRIDE
    ncp = nch + LANES
    nslc = s // SLC_LEN
    kv_cmp_pad = jnp.pad(kv_cmp.astype(BF16), ((0, 0), (0, 0), (CMP_PAD, ncp - nch - CMP_PAD), (0, 0)))
    b_cmp, b_sel, b_win, far3 = _nsa_bias_tables(t5_table)

    row_ix = np.arange(ncp)
    row_ok = (row_ix >= CMP_PAD) & (row_ix < nch - 1 + CMP_PAD)
    row_cols = np.zeros((ncp, 2 * AUG_K - hd), np.float32)
    row_cols[:, ROW_BIAS - hd:ROW_BIAS - hd + 3] = row_ok[:, None]
    row_cols[:, ROW_BIAS - hd + 3] = ~row_ok
    row_cols[row_ix, AUG_K - hd + row_ix // BF16_ROWS] = 1.0
    kc_aug = jnp.concatenate([kv_cmp_pad[0], jnp.broadcast_to(jnp.asarray(row_cols, BF16), (g,) + row_cols.shape)],
                             axis=-1)
    vct = kv_cmp_pad[1].transpose(0, 2, 1)
    nq = s // ATT_TQ
    qt = qt_nsa.reshape(g, NSA_HPG, hd, nq, ATT_TQ).transpose(0, 3, 2, 1, 4).reshape(g, nq, hd, NSA_HPG * ATT_TQ)
    oct_, selt = _nsa_compress_select(qt, kc_aug, vct, b_cmp, far3)

    period = BF16_ROWS * SLC_LEN
    k_extra = np.zeros((period, AUG_K), np.float32)
    k_extra[np.arange(period), ROW_MASK + np.arange(period) // SLC_LEN] = 1.0
    k_extra[:, ROW_BIAS:ROW_BIAS + 3] = 1.0
    gates = small[:, 0:c_gate].reshape(s, g, NSA_HPG, 3).transpose(1, 2, 0, 3)
    gates_t = jnp.pad(_to_lane_blocks(gates, ATT_TQ), ((0, 0), (0, 0), (0, 5), (0, 0)))
    o_nsa_t = _nsa_attention(qt, tok, tok_ks_block0, tok_kw_block0, jnp.asarray(k_extra, BF16), ch, ch_vs_block0,
                             ch_vw_block0, selt, far3, b_sel, b_win, oct_, gates_t)
    o_nsa = o_nsa_t.reshape(MIX_W, s)

    fgt_bias = jnp.concatenate([jnp.zeros((c_gate,), F32), b_fgt, jnp.zeros((LANES - n_small,), F32)])[None, :]
    decay_k, decay_ends = _decay_pieces(small, fgt_bias, c_gate, nh)
    o_fox = _fox_attention(ch, ch_fq_block0, ch_fv_block0, tok, tok_fox_block0, decay_k, decay_ends)
    o_fox = o_fox.reshape(MIX_W, s)

    n_r = N_GROUPS + N_EXPERTS
    w_r = jnp.concatenate([w_rg, w_re.reshape(d, N_EXPERTS), jnp.zeros((d, LANES - n_r), F32)], axis=1)
    b_r = jnp.concatenate([b_rg, b_re.reshape(N_EXPERTS), jnp.zeros((LANES - n_r,), F32)])[None, :]
    w_r_hi, w_r_lo, _ = _split3_exact(w_r)
    x1, u2, route = _merge_project(o_nsa, o_fox, merge, x2d, w_br_nsa.astype(BF16), w_br_fox.astype(BF16),
                                   w_o.astype(BF16), g1, ln1_g[None, :], ln1_b[None, :], sc2, sh2,
                                   jnp.stack([w_r_hi, w_r_lo]), b_r, alpha)

    blk_exp, n_used, row_tok, row_dst, row_w = _moe_dispatch(route, s)
    y2 = _moe_experts(u2, blk_exp, n_used, row_tok, row_dst, row_w, w_gate, w_up, w_down)
    return _final_ln(x1, y2, g2, ln2_g[None, :], ln2_b[None, :], alpha)


def kernel(x, c, w_ada, b_ada, w_in, b_fgt, t5_table, cmp_pe, cmp_w1, cmp_b1, cmp_w2, w_br_nsa, w_br_fox, w_o,
           ln1_g, ln1_b, w_rg, b_rg, w_re, b_re, w_gate, w_up, w_down, ln2_g, ln2_b):
    b, s, d = x.shape
    depth = w_ada.shape[0]
    assert b == 1
    alpha = (2 * depth) ** 0.25
    h = x[0]
    for l in range(depth):
        h = _layer(h, c, w_ada[l], b_ada[l], w_in[l], b_fgt[l], t5_table, cmp_pe[l], cmp_w1[l], cmp_b1[l],
                   cmp_w2[l], w_br_nsa[l], w_br_fox[l], w_o[l], ln1_g[l], ln1_b[l], w_rg[l], b_rg[l],
                   w_re[l], b_re[l], w_gate[l], w_up[l], w_down[l], ln2_g[l], ln2_b[l], alpha)
    return h[None]
```

```python
import functools
import math

import numpy as np
import jax
import jax.numpy as jnp
from jax import lax
from jax.experimental import pallas as pl
from jax.experimental.pallas import tpu as pltpu

F32 = jnp.float32
BF16 = jnp.bfloat16
HIGHEST = lax.Precision.HIGHEST
LOG2E = math.log2(math.e)

HEAD_DIM = 64
NSA_HEADS = 8
NSA_GROUPS = 2
NSA_HPG = NSA_HEADS // NSA_GROUPS
FOX_HEADS = 8
MIX_W = NSA_HEADS * HEAD_DIM
CMP_LEN = 32
CMP_STRIDE = 16
SLC_LEN = 64
SLC_TOPK = 16
WINDOW = 512
T5_BUCKETS = 32
T5_MAX_EXACT = 16
T5_MAX_DIST = 128
N_GROUPS = 8
EXPERTS_PER_GROUP = 8
N_EXPERTS = N_GROUPS * EXPERTS_PER_GROUP
ROW_BLOCK = 128
LN_EPS = 1e-5
NEG_INF = -1e30
M_INIT = -1e29
FORCE_SCORE = 1e4

LANES = 128
BF16_ROWS = 16
CMP_PAD = 8
ATT_TQ = 256
CMP_NEAR_ROWS = 32
FOX_TQ = 1024
FOX_TK = 512
FOX_SKIP_MARGIN = 160.0
FOX_BOUND_SLACK = 1.02
PROJ_TM = 1024
AUG_K = 128
AUG_V = HEAD_DIM + 16
ROW_MASK = HEAD_DIM
ROW_BIAS = HEAD_DIM + 16
VMEM_LIMIT = 56 * 1024 * 1024


def _cparams(sem, vmem=VMEM_LIMIT):
    return pltpu.CompilerParams(dimension_semantics=sem, vmem_limit_bytes=vmem)


def _sigmoid(x):
    return 1.0 / (1.0 + jnp.exp(-x))


def _layer_norm(x):
    mu = jnp.mean(x, axis=-1, keepdims=True)
    xc = x - mu
    var = jnp.mean(xc * xc, axis=-1, keepdims=True)
    return xc * lax.rsqrt(var + LN_EPS)


def _split3(x):
    hi = x.astype(BF16)
    r1 = x - hi.astype(F32)
    mid = r1.astype(BF16)
    lo = (r1 - mid.astype(F32)).astype(BF16)
    return hi, mid, lo


def _split3_exact(x):
    def trunc(v):
        bits = lax.bitcast_convert_type(v, jnp.uint32) & jnp.uint32(0xFFFF0000)
        return lax.bitcast_convert_type(bits, F32)
    hi = trunc(x)
    r1 = x - hi
    mid = trunc(r1)
    lo = r1 - mid
    return hi.astype(BF16), mid.astype(BF16), lo.astype(BF16)


def _dot(a, b):
    return jnp.dot(a, b, preferred_element_type=F32)


def _dot_nt(a, b):
    return lax.dot_general(a, b, (((1,), (1,)), ((), ())), preferred_element_type=F32)


def _dot3(x, w_bf16):
    hi, mid, lo = _split3(x)
    return _dot(hi, w_bf16) + _dot(mid, w_bf16) + _dot(lo, w_bf16)


def _dot3_rhs(w_bf16, x):
    hi, mid, lo = _split3(x)
    return _dot(w_bf16, hi) + _dot(w_bf16, mid) + _dot(w_bf16, lo)


def _ada_kernel(c_ref, w_ref, b_ref, o_ref):
    c = c_ref[...]
    a = c * _sigmoid(c)
    o_ref[...] = jnp.dot(a, w_ref[...], precision=HIGHEST, preferred_element_type=F32) + b_ref[...]


def _ada_mod(c, w, b):
    d, n = w.shape
    tn = 2048
    c8 = jnp.broadcast_to(c, (8, d))
    out = pl.pallas_call(
        _ada_kernel,
        grid=(n // tn,),
        in_specs=[pl.BlockSpec((8, d), lambda j: (0, 0)),
                  pl.BlockSpec((d, tn), lambda j: (0, j)),
                  pl.BlockSpec((1, tn), lambda j: (0, j))],
        out_specs=pl.BlockSpec((8, tn), lambda j: (0, j)),
        out_shape=jax.ShapeDtypeStruct((8, n), F32),
        compiler_params=_cparams(("parallel",)),
        name="ada_mod",
    )(c8, w, b.reshape(1, n))
    return out[0:1]


def _lnmod_kernel(x_ref, sc_ref, sh_ref, o_ref):
    y = _layer_norm(x_ref[...])
    o_ref[...] = (y * (1.0 + sc_ref[...]) + sh_ref[...]).astype(o_ref.dtype)


def _ln_mod(x, sc, sh, out_dtype, tm=512):
    m, d = x.shape
    return pl.pallas_call(
        _lnmod_kernel,
        grid=(m // tm,),
        in_specs=[pl.BlockSpec((tm, d), lambda i: (i, 0)),
                  pl.BlockSpec((1, d), lambda i: (0, 0)),
                  pl.BlockSpec((1, d), lambda i: (0, 0))],
        out_specs=pl.BlockSpec((tm, d), lambda i: (i, 0)),
        out_shape=jax.ShapeDtypeStruct((m, d), out_dtype),
        compiler_params=_cparams(("parallel",)),
        name="ln_mod",
    )(x, sc, sh)


def _mm_kernel(a_ref, w_ref, o_ref):
    o_ref[...] = _dot(a_ref[...], w_ref[...]).astype(o_ref.dtype)


def _matmul(a, w, out_dtype, tm, tn, name):
    m, k = a.shape
    n = w.shape[1]
    return pl.pallas_call(
        _mm_kernel,
        grid=(n // tn, m // tm),
        in_specs=[pl.BlockSpec((tm, k), lambda j, i: (i, 0)),
                  pl.BlockSpec((k, tn), lambda j, i: (0, j))],
        out_specs=pl.BlockSpec((tm, tn), lambda j, i: (i, j)),
        out_shape=jax.ShapeDtypeStruct((m, n), out_dtype),
        compiler_params=_cparams(("parallel", "parallel")),
        name=name,
    )(a, w)


def _mm_nt_kernel(w_ref, a_ref, o_ref):
    o_ref[...] = _dot_nt(w_ref[...], a_ref[...]).astype(o_ref.dtype)


def _matmul_nt(w_t, a, out_dtype, tm, name):
    n, k = w_t.shape
    m = a.shape[0]
    return pl.pallas_call(
        _mm_nt_kernel,
        grid=(m // tm,),
        in_specs=[pl.BlockSpec((n, k), lambda i: (0, 0), pipeline_mode=pl.Buffered(1)),
                  pl.BlockSpec((tm, k), lambda i: (i, 0))],
        out_specs=pl.BlockSpec((n, tm), lambda i: (0, i)),
        out_shape=jax.ShapeDtypeStruct((n, m), out_dtype),
        compiler_params=_cparams(("parallel",)),
        name=name,
    )(w_t, a)


def _gelu_tanh(x):
    return 0.5 * x * (1.0 + jnp.tanh(math.sqrt(2.0 / math.pi) * (x + 0.044715 * (x * x * x))))


def _compress_kernel(c_ref, pe_ref, w1a_ref, w1b_ref, b1_ref, w2_ref, o_ref, *, nch):
    c = c_ref[0, 0]
    w1a = w1a_ref[0]
    w1b = w1b_ref[0]
    half = CMP_STRIDE * HEAD_DIM
    a = _dot(c, w1a)
    b = _dot(c, w1b)
    b_next = pltpu.roll(b, shift=nch - 1, axis=0)
    pe = pe_ref[0]
    pb = _dot(pe[:, :half], w1a) + _dot(pe[:, half:], w1b)
    hid = _gelu_tanh(a + b_next + pb[0:1, :] + b1_ref[0])
    o_ref[0, 0] = _dot(hid.astype(BF16), w2_ref[0])


def _compress(kv_cmp, pe, w1, b1, w2):
    z, g, s, hd = kv_cmp.shape
    nch = s // CMP_STRIDE
    half = CMP_STRIDE * hd
    chunks = kv_cmp.reshape(z, g, nch, half)
    pe8 = jnp.broadcast_to(pe.reshape(z, 1, CMP_LEN * hd), (z, 8, CMP_LEN * hd)).astype(BF16)
    w1b16 = w1.astype(BF16)
    hidn = w1.shape[-1]
    return pl.pallas_call(
        functools.partial(_compress_kernel, nch=nch),
        grid=(z, g),
        in_specs=[pl.BlockSpec((1, 1, nch, half), lambda zi, gi: (zi, gi, 0, 0)),
                  pl.BlockSpec((1, 8, 2 * half), lambda zi, gi: (zi, 0, 0)),
                  pl.BlockSpec((1, half, hidn), lambda zi, gi: (zi, 0, 0)),
                  pl.BlockSpec((1, half, hidn), lambda zi, gi: (zi, 1, 0)),
                  pl.BlockSpec((1, 1, hidn), lambda zi, gi: (zi, 0, 0)),
                  pl.BlockSpec((1, hidn, hd), lambda zi, gi: (zi, 0, 0))],
        out_specs=pl.BlockSpec((1, 1, nch, hd), lambda zi, gi: (zi, gi, 0, 0)),
        out_shape=jax.ShapeDtypeStruct((z, g, nch, hd), F32),
        compiler_params=_cparams(("parallel", "parallel")),
        name="compress_kv",
    )(chunks, pe8, w1b16, w1b16, b1.reshape(z, 1, hidn), w2.astype(BF16))


def _t5_bucket_np(dist):
    n = np.maximum(dist, 0)
    ratio = np.log(np.maximum(n, T5_MAX_EXACT).astype(np.float64) / T5_MAX_EXACT)
    big = T5_MAX_EXACT + (ratio / math.log(T5_MAX_DIST / T5_MAX_EXACT)
                          * (T5_BUCKETS - T5_MAX_EXACT)).astype(np.int64)
    return np.where(n < T5_MAX_EXACT, n, np.minimum(big, T5_BUCKETS - 1)).astype(np.int32)


def _t5_lookup(tbh, dist):
    onehot = np.eye(T5_BUCKETS, dtype=np.float32)[_t5_bucket_np(dist).reshape(-1)]
    vals = jnp.einsum('ghb,nb->ghn', tbh, jnp.asarray(onehot), precision=HIGHEST)
    return vals.reshape(tbh.shape[:2] + dist.shape)


def _toeplitz_kernel(w_ref, o_ref, *, n_keys, tq, lead_zero, trail_masked):
    width = w_ref.shape[-1]
    rows = jnp.broadcast_to(w_ref[0, 0], (n_keys, width))
    rolled = pltpu.roll(rows, shift=width - (n_keys - 1), axis=1, stride=1, stride_axis=0)
    parts = [rolled[:, 0:tq]]
    if lead_zero:
        parts.insert(0, jnp.zeros((lead_zero, tq), F32))
    if trail_masked:
        parts.append(jnp.full((trail_masked, tq), NEG_INF, F32))
    o_ref[0] = jnp.concatenate(parts, axis=0)


def _att_table_t(tbh, n_keys, tq, lo, hi, minus_far, lead_zero=0, trail_masked=0):
    length = n_keys + tq - 1
    d = np.arange(length) - (tq - 1)
    valid = (d >= lo) & (d < hi)
    vals = _t5_lookup(tbh, d)
    if minus_far:
        vals = vals - tbh[:, :, T5_BUCKETS - 1:]
    w = jnp.where(jnp.asarray(valid), vals * LOG2E, NEG_INF)
    width = pl.next_power_of_2(length)
    w = jnp.pad(w, ((0, 0), (0, 0), (0, width - length)))[:, :, None, :]
    g, hpg = tbh.shape[:2]
    n_rows = lead_zero + n_keys + trail_masked
    return pl.pallas_call(
        functools.partial(_toeplitz_kernel, n_keys=n_keys, tq=tq, lead_zero=lead_zero, trail_masked=trail_masked),
        grid=(g, hpg),
        in_specs=[pl.BlockSpec((1, 1, 1, width), lambda gi, hi: (gi, hi, 0, 0))],
        out_specs=pl.BlockSpec((1, n_rows, tq), lambda gi, hi: (gi, 0, hi)),
        out_shape=jax.ShapeDtypeStruct((g, n_rows, hpg * tq), F32),
        compiler_params=_cparams(("parallel", "parallel")),
        name="t5_toeplitz",
    )(w)


def _nsa_bias_tables(t5_table):
    tbh = t5_table.T.reshape(NSA_GROUPS, NSA_HPG, T5_BUCKETS).astype(F32)
    j = np.arange(CMP_NEAR_ROWS)[:, None]
    i = np.arange(ATT_TQ)[None, :]
    dist = i - (CMP_LEN - 1) - CMP_STRIDE * (j - CMP_PAD)
    vals = (_t5_lookup(tbh, dist) - tbh[:, :, T5_BUCKETS - 1][:, :, None, None]) * LOG2E
    vals = jnp.where(jnp.asarray(dist >= 0)[None, None], vals, NEG_INF)
    b_cmp = vals.transpose(0, 2, 1, 3).reshape(NSA_GROUPS, CMP_NEAR_ROWS, NSA_HPG * ATT_TQ)
    far = tbh[:, :, T5_BUCKETS - 1] * LOG2E
    b_sel = _att_table_t(tbh, 2 * ATT_TQ, ATT_TQ, 0, 1 << 30, True, lead_zero=ATT_TQ)
    b_win = _att_table_t(tbh, 3 * ATT_TQ, ATT_TQ, 0, WINDOW, False, trail_masked=ATT_TQ)
    hi, mid, lo = _split3_exact(jnp.repeat(far, ATT_TQ, axis=1))
    far3 = jnp.stack([hi, mid, lo, jnp.full(hi.shape, NEG_INF, BF16)], axis=1)
    far3 = jnp.pad(far3, ((0, 0), (0, BF16_ROWS - 4), (0, 0)))
    return b_cmp, b_sel, b_win, far3


def _nsa_cmp_kernel(qt_ref, kc_ref, vct_ref, near_ref, far3_ref, oct_ref, selt_ref, rhs, s_scr, imp_scr, *, nslc):
    tq = ATT_TQ
    lanes = NSA_HPG * tq
    qb = pl.program_id(1)
    cpb = tq // CMP_STRIDE
    rhs[...] = jnp.zeros(rhs.shape, BF16)
    rhs[0:HEAD_DIM, :] = qt_ref[0, 0]
    rhs[ROW_BIAS:ROW_BIAS + BF16_ROWS, :] = far3_ref[0]
    chunk = lax.broadcasted_iota(jnp.int32, (AUG_K, lanes), 0)
    rhs[AUG_K:2 * AUG_K, :] = jnp.where(chunk >= qb + CMP_NEAR_ROWS // BF16_ROWS, NEG_INF, 0.0).astype(BF16)
    r0 = pl.multiple_of(cpb * qb, BF16_ROWS)
    n_lane_blocks = tq // LANES
    ncp = s_scr.shape[0]
    imp_scr[...] = jnp.zeros(imp_scr.shape, F32)

    def attend(rows):
        s_scr[0:rows, :] = _dot(kc_ref[0, 0:rows, :], rhs[...])
        s_scr[pl.ds(r0, CMP_NEAR_ROWS), :] = s_scr[pl.ds(r0, CMP_NEAR_ROWS), :] + near_ref[0]
        s = s_scr[0:rows, :]
        m = jnp.max(s, axis=0, keepdims=True)
        e = jnp.exp2(s - m)
        l = jnp.sum(e, axis=0, keepdims=True)
        p = e * jnp.where(m > M_INIT, 1.0 / l, 0.0)
        oct_ref[0, 0] = _dot(vct_ref[0, :, 0:rows], p.astype(BF16))
        imp = p[:, 0:tq]
        for h in range(1, NSA_HPG):
            imp = imp + p[:, h * tq:(h + 1) * tq]
        for c in range(n_lane_blocks):
            imp_scr[c, 0:rows, :] = imp[:, c * LANES:(c + 1) * LANES]

    limits = sorted({min(ncp, -(-(ncp * k // 3) // LANES) * LANES) for k in (1, 2, 3)})
    lo_qb = 0
    for rows in limits:
        hi_qb = (rows - CMP_NEAR_ROWS) // cpb if rows < ncp else pl.num_programs(1) - 1

        @pl.when((qb >= lo_qb) & (qb <= hi_qb))
        def _(rows=rows):
            attend(rows)

        lo_qb = hi_qb + 1
    ratio = SLC_LEN // CMP_STRIDE

    def taps(off):
        return jnp.concatenate([imp_scr[c, pl.ds(CMP_PAD + off, nslc, stride=ratio), :]
                                for c in range(n_lane_blocks)], axis=1)

    p_slc = 0.5 * (taps(-1) + taps(ratio - 1))
    for off in range(ratio - 1):
        p_slc = p_slc + taps(off)
    blk = lax.broadcasted_iota(jnp.int32, (nslc, tq), 0)
    cur = (qb * tq + lax.broadcasted_iota(jnp.int32, (nslc, tq), 1)) // SLC_LEN
    forced = (blk == 0) | (blk == cur) | (blk == cur - 1)
    score = jnp.where(forced, FORCE_SCORE, jnp.where(blk <= cur, p_slc, -1.0))
    blk_f = blk.astype(F32)
    sel = jnp.zeros((nslc, tq), F32)
    for _ in range(min(SLC_TOPK, nslc)):
        mx = jnp.max(score, axis=0, keepdims=True)
        first = jnp.min(jnp.where(score == mx, blk_f, float(nslc)), axis=0, keepdims=True)
        hit = blk_f == first
        sel = jnp.where(hit, 1.0, sel)
        score = jnp.where(hit, -2.0, score)
    selt_ref[0] = sel.astype(BF16)


def _nsa_compress_select(qt, kc_aug, vct, b_cmp, far3):
    g, nq, hd, lanes = qt.shape
    ncp = kc_aug.shape[1]
    tq = ATT_TQ
    nslc = nq * tq // SLC_LEN
    assert ncp // BF16_ROWS <= AUG_K
    assert tq // CMP_STRIDE == BF16_ROWS
    per_q = lambda gi, qi: (gi, qi, 0, 0)
    per_g = lambda gi, qi: (gi, 0, 0)
    return pl.pallas_call(
        functools.partial(_nsa_cmp_kernel, nslc=nslc),
        grid=(g, nq),
        in_specs=[pl.BlockSpec((1, 1, hd, lanes), per_q),
                  pl.BlockSpec((1, ncp, 2 * AUG_K), per_g),
                  pl.BlockSpec((1, hd, ncp), per_g),
                  pl.BlockSpec((1, CMP_NEAR_ROWS, lanes), per_g),
                  pl.BlockSpec((1, BF16_ROWS, lanes), per_g)],
        out_specs=[pl.BlockSpec((1, 1, hd, lanes), per_q),
                   pl.BlockSpec((1, nslc, tq), lambda gi, qi: (gi, 0, qi))],
        out_shape=[jax.ShapeDtypeStruct((g, nq, hd, lanes), F32),
                   jax.ShapeDtypeStruct((g, nslc, nq * tq), BF16)],
        scratch_shapes=[pltpu.VMEM((2 * AUG_K, lanes), BF16), pltpu.VMEM((ncp, lanes), F32),
                        pltpu.VMEM((tq // LANES, ncp, LANES), F32)],
        compiler_params=_cparams(("parallel", "parallel")),
        name="nsa_compress_select",
    )(qt, kc_aug, vct, b_cmp, far3)


def _flash_init_t(m_ref, acc_ref):
    m_ref[...] = jnp.full(m_ref.shape, M_INIT, F32)
    acc_ref[...] = jnp.zeros(acc_ref.shape, F32)


def _flash_step_t(s, vt_tile, m_ref, acc_ref):
    m_old = m_ref[...]
    m_new = jnp.maximum(m_old, jnp.max(s, axis=0, keepdims=True))
    p = jnp.exp2(s - m_new).astype(BF16)
    acc_ref[...] = jnp.exp2(m_old - m_new) * acc_ref[...] + _dot(vt_tile, p)
    m_ref[...] = m_new


def _flash_result_t(acc_ref):
    acc = acc_ref[...]
    return acc[0:HEAD_DIM, :] / acc[HEAD_DIM:HEAD_DIM + 1, :]


def _nsa_att_kernel(qt_ref, ks_ref, kx_ref, vst_ref, kw_ref, vwt_ref, selt_ref, far3_ref, bsel_ref, bwin_ref,
                    oct_ref, gate_ref, o_ref, rhs_s, rhs_w, mask_t, ms, accs, mw, accw, s_even, s_odd):
    tq = ATT_TQ
    qb = pl.program_id(1)
    ones_rows = _ones_rows(tq)
    kx_tiles = kx_ref.shape[0] // tq
    qt = qt_ref[0, 0]
    rhs_s[...] = jnp.zeros(rhs_s.shape, BF16)
    rhs_s[0:HEAD_DIM, :] = qt
    rhs_s[ROW_BIAS:ROW_BIAS + BF16_ROWS, :] = far3_ref[0]
    rhs_w[...] = jnp.zeros(rhs_w.shape, BF16)
    rhs_w[0:HEAD_DIM, :] = qt
    madd = ((selt_ref[0].astype(F32) - 1.0) * (-NEG_INF)).astype(BF16)
    mask_t[...] = jnp.concatenate([madd] * NSA_HPG, axis=1)
    _flash_init_t(ms, accs)
    _flash_init_t(mw, accw)
    blocks_per_tile = tq // SLC_LEN

    def sel_scores(kt, s_ref):
        kt = jnp.minimum(kt, qb)
        k0 = pl.multiple_of(kt * tq, tq)
        chunk = pl.multiple_of((kt * blocks_per_tile) // BF16_ROWS * BF16_ROWS, BF16_ROWS)
        rhs_s[ROW_MASK:ROW_MASK + BF16_ROWS, :] = mask_t[pl.ds(chunk, BF16_ROWS), :]
        rel = jnp.clip(kt - qb + 2, 0, 2)
        table = bsel_ref[0, pl.ds(pl.multiple_of(rel * tq, tq), tq), :]
        k_aug = ks_ref[pl.ds(k0, tq), :] + kx_ref[pl.ds(pl.multiple_of((kt % kx_tiles) * tq, tq), tq), :]
        s_ref[...] = _dot(k_aug, rhs_s[...]) + table

    def sel_consume(kt, s_ref):
        k0 = pl.multiple_of(kt * tq, tq)
        vt_aug = jnp.concatenate([vst_ref[0, :, pl.ds(k0, tq)], ones_rows], axis=0)
        _flash_step_t(s_ref[...], vt_aug, ms, accs)

    def win_scores(j, s_ref):
        kt = qb - 2 + j
        k0 = pl.multiple_of(jnp.maximum(kt, 0) * tq, tq)
        row = pl.multiple_of(jnp.where(kt >= 0, j, 3) * tq, tq)
        s_ref[...] = _dot(kw_ref[pl.ds(k0, tq), :], rhs_w[...]) + bwin_ref[0, pl.ds(row, tq), :]

    def win_consume(j, s_ref):
        k0 = pl.multiple_of(jnp.maximum(qb - 2 + j, 0) * tq, tq)
        vt_aug = jnp.concatenate([vwt_ref[0, :, pl.ds(k0, tq)], ones_rows], axis=0)
        _flash_step_t(s_ref[...], vt_aug, mw, accw)

    n_sel = qb + 1
    sel_scores(0, s_even)

    def pair_body(j, carry):
        sel_scores(2 * j + 1, s_odd)
        sel_consume(2 * j, s_even)
        sel_scores(2 * j + 2, s_even)
        sel_consume(2 * j + 1, s_odd)
        return carry

    lax.fori_loop(0, n_sel // 2, pair_body, 0)
    win_scores(0, s_odd)

    @pl.when(n_sel % 2 == 1)
    def _():
        sel_consume(qb, s_even)

    win_scores(1, s_even)
    win_consume(0, s_odd)
    win_scores(2, s_odd)
    win_consume(1, s_even)
    win_consume(2, s_odd)

    gt = _sigmoid(gate_ref[0, 0])
    out = gt[0:1, :] * oct_ref[0, 0] + gt[1:2, :] * _flash_result_t(accs) + gt[2:3, :] * _flash_result_t(accw)
    for h in range(NSA_HPG):
        o_ref[0, h] = out[:, h * tq:(h + 1) * tq].astype(o_ref.dtype)


def _nsa_attention(qt, k_tok, ks_block0, kw_block0, k_extra, ch, vs_block0, vw_block0, selt, far3, b_sel, b_win,
                   oct_, gates_t):
    g, nq, hd, lanes = qt.shape
    s = k_tok.shape[0]
    nslc = selt.shape[1]
    tq = ATT_TQ
    resident = pl.Buffered(1)
    per_q = lambda gi, qi: (gi, qi, 0, 0)
    per_g = lambda gi, qi: (gi, 0, 0)
    return pl.pallas_call(
        _nsa_att_kernel,
        grid=(g, nq),
        in_specs=[pl.BlockSpec((1, 1, hd, lanes), per_q),
                  pl.BlockSpec((s, AUG_K), lambda gi, qi: (0, ks_block0 + gi), pipeline_mode=resident),
                  pl.BlockSpec(k_extra.shape, lambda gi, qi: (0, 0), pipeline_mode=resident),
                  pl.BlockSpec((1, hd, s), lambda gi, qi: (vs_block0 + gi, 0, 0), pipeline_mode=resident),
                  pl.BlockSpec((s, AUG_K), lambda gi, qi: (0, kw_block0 + gi), pipeline_mode=resident),
                  pl.BlockSpec((1, hd, s), lambda gi, qi: (vw_block0 + gi, 0, 0), pipeline_mode=resident),
                  pl.BlockSpec((1, nslc, tq), lambda gi, qi: (gi, 0, qi)),
                  pl.BlockSpec((1, BF16_ROWS, lanes), per_g),
                  pl.BlockSpec((1, 3 * tq, lanes), per_g, pipeline_mode=resident),
                  pl.BlockSpec((1, 4 * tq, lanes), per_g, pipeline_mode=resident),
                  pl.BlockSpec((1, 1, hd, lanes), per_q),
                  pl.BlockSpec((1, 1, 8, lanes), per_q)],
        out_specs=pl.BlockSpec((1, lanes // tq, hd, tq), lambda gi, qi: (gi, 0, 0, qi)),
        out_shape=jax.ShapeDtypeStruct((g, lanes // tq, hd, s), BF16),
        scratch_shapes=[pltpu.VMEM((AUG_K, lanes), BF16), pltpu.VMEM((AUG_K, lanes), BF16),
                        pltpu.VMEM((nslc, lanes), BF16),
                        pltpu.VMEM((1, lanes), F32), pltpu.VMEM((AUG_V, lanes), F32),
                        pltpu.VMEM((1, lanes), F32), pltpu.VMEM((AUG_V, lanes), F32),
                        pltpu.VMEM((tq, lanes), F32), pltpu.VMEM((tq, lanes), F32)],
        compiler_params=_cparams(("arbitrary", "arbitrary")),
        name="nsa_select_window",
    )(qt, k_tok, k_extra, ch, k_tok, ch, selt, far3, b_sel, b_win, oct_, gates_t)


def _decay_kernel(z_ref, b_ref, place_ref, o_ref, end_ref, carry_ref, *, tb):
    @pl.when(pl.program_id(0) == 0)
    def _():
        carry_ref[...] = jnp.zeros(carry_ref.shape, F32)

    z = z_ref[...] + b_ref[...]
    log_f = jnp.minimum(z, 0.0) - jnp.log1p(jnp.exp(-jnp.abs(z)))
    r = lax.broadcasted_iota(jnp.int32, (tb, tb), 0)
    c = lax.broadcasted_iota(jnp.int32, (tb, tb), 1)
    tri = jnp.where(r >= c, 1.0, 0.0).astype(BF16)
    run = _dot3_rhs(tri, log_f) + carry_ref[...]
    carry_ref[...] = run[tb - 1:tb, :]
    val = -run * LOG2E
    hi, mid, lo = _split3_exact(val)
    o_ref[...] = (_dot(hi, place_ref[0]) + _dot(mid, place_ref[1]) + _dot(lo, place_ref[2])).astype(BF16)
    end_ref[...] = jnp.broadcast_to(val[tb - 1:tb, :], end_ref.shape)


def _decay_pieces(z, bias, first_lane, n_heads):
    s, n = z.shape
    tb = FOX_TK
    place = np.zeros((3, n, n_heads * AUG_K), np.float32)
    for h in range(n_heads):
        for piece in range(3):
            place[piece, first_lane + h, h * AUG_K + HEAD_DIM + piece] = 1.0
    pieces, ends = pl.pallas_call(
        functools.partial(_decay_kernel, tb=tb),
        grid=(s // tb,),
        in_specs=[pl.BlockSpec((tb, n), lambda i: (i, 0)),
                  pl.BlockSpec((1, n), lambda i: (0, 0)),
                  pl.BlockSpec((3, n, n_heads * AUG_K), lambda i: (0, 0, 0))],
        out_specs=[pl.BlockSpec((tb, n_heads * AUG_K), lambda i: (i, 0)),
                   pl.BlockSpec((8, n), lambda i: (i, 0))],
        out_shape=[jax.ShapeDtypeStruct((s, n_heads * AUG_K), BF16),
                   jax.ShapeDtypeStruct((s // tb * 8, n), F32)],
        scratch_shapes=[pltpu.VMEM((1, n), F32)],
        compiler_params=_cparams(("arbitrary",)),
        name="decay_cumsum",
    )(z, bias, jnp.asarray(place, BF16))
    return pieces, ends[::8, first_lane:first_lane + n_heads].T


def _ones_rows(width):
    return jnp.where(lax.broadcasted_iota(jnp.int32, (BF16_ROWS, width), 0) < 8, 1.0, 0.0).astype(BF16)


def _fox_kernel(ends_ref, qt_ref, k_ref, dk_ref, vt_ref, o_ref, rhs, m_ref, acc_ref, s_even, s_odd, kmax_ref):
    tq = FOX_TQ
    tk = FOX_TK
    assert tq == 2 * tk
    qb = pl.program_id(1)
    row = lax.broadcasted_iota(jnp.int32, (AUG_K - HEAD_DIM, tq), 0)
    rhs[0:HEAD_DIM, :] = qt_ref[0]
    rhs[HEAD_DIM:AUG_K, :] = jnp.where(row < 3, 1.0, 0.0).astype(BF16)
    ones_rows = _ones_rows(tk)
    _flash_init_t(m_ref, acc_ref)

    def scores(kt, s_ref):
        k0 = pl.multiple_of(kt * tk, tk)
        k_aug = k_ref[pl.ds(k0, tk), :] + dk_ref[pl.ds(k0, tk), :]
        s_ref[...] = _dot(k_aug, rhs[...])

    def consume(kt, s_ref, diagonal):
        k0 = pl.multiple_of(kt * tk, tk)
        s = s_ref[...]
        if diagonal:
            key = k0 + lax.broadcasted_iota(jnp.int32, (tk, tq), 0)
            qry = qb * tq + lax.broadcasted_iota(jnp.int32, (tk, tq), 1)
            s = jnp.where(key <= qry, s, NEG_INF)
        vt_aug = jnp.concatenate([vt_ref[0, :, pl.ds(k0, tk)], ones_rows], axis=0)
        _flash_step_t(s, vt_aug, m_ref, acc_ref)

    @pl.when(qb == 0)
    def _():
        ones = jnp.ones((AUG_K, AUG_K), BF16)

        def norm_tile(c, best):
            k = k_ref[pl.ds(pl.multiple_of(c * tk, tk), tk), :].astype(F32)
            return jnp.maximum(best, _dot((k * k).astype(BF16), ones))

        best = lax.fori_loop(0, k_ref.shape[0] // tk, norm_tile, jnp.zeros((tk, AUG_K), F32))
        kmax_ref[0] = jnp.max(jnp.sqrt(best * FOX_BOUND_SLACK))

    scores(2 * qb, s_even)
    scores(2 * qb + 1, s_odd)
    consume(2 * qb, s_even, True)
    consume(2 * qb + 1, s_odd, True)

    q = qt_ref[0].astype(F32)
    q_norm = jnp.max(jnp.sqrt(jnp.sum(q * q, axis=0, keepdims=True) * FOX_BOUND_SLACK))
    threshold = jnp.min(m_ref[...]) - FOX_SKIP_MARGIN - q_norm * kmax_ref[0]
    head = pl.program_id(0)

    def first_needed(j, first):
        return jnp.where(ends_ref[head, 2 * j + 1] >= threshold, jnp.minimum(first, j), first)

    j0 = lax.fori_loop(0, qb, first_needed, qb)

    scores(2 * j0, s_even)

    def pair_body(j, carry):
        scores(2 * j + 1, s_odd)
        consume(2 * j, s_even, False)
        scores(2 * j + 2, s_even)
        consume(2 * j + 1, s_odd, False)
        return carry

    lax.fori_loop(j0, qb, pair_body, 0)
    o_ref[0] = _flash_result_t(acc_ref).astype(o_ref.dtype)


def _fox_attention(ch, q_block0, v_block0, k_tok, k_block0, decay_k, decay_ends):
    _, hd, s = ch.shape
    h = decay_ends.shape[0]
    tq = FOX_TQ
    grid_spec = pltpu.PrefetchScalarGridSpec(
        num_scalar_prefetch=1,
        grid=(h, s // tq),
        in_specs=[pl.BlockSpec((1, hd, tq), lambda hi, qi, ends: (q_block0 + hi, 0, qi)),
                  pl.BlockSpec((s, AUG_K), lambda hi, qi, ends: (0, k_block0 + hi)),
                  pl.BlockSpec((s, AUG_K), lambda hi, qi, ends: (0, hi)),
                  pl.BlockSpec((1, hd, s), lambda hi, qi, ends: (v_block0 + hi, 0, 0))],
        out_specs=pl.BlockSpec((1, hd, tq), lambda hi, qi, ends: (hi, 0, qi)),
        scratch_shapes=[pltpu.VMEM((AUG_K, tq), BF16),
                        pltpu.VMEM((1, tq), F32), pltpu.VMEM((AUG_V, tq), F32),
                        pltpu.VMEM((FOX_TK, tq), F32), pltpu.VMEM((FOX_TK, tq), F32),
                        pltpu.SMEM((1,), F32)],
    )
    return pl.pallas_call(
        _fox_kernel,
        grid_spec=grid_spec,
        out_shape=jax.ShapeDtypeStruct((h, hd, s), BF16),
        compiler_params=_cparams(("arbitrary", "arbitrary")),
        name="fox_attention",
    )(decay_ends, ch, k_tok, decay_k, ch)


def _merge_kernel(on_ref, of_ref, mg_ref, x_ref, wn_ref, wf_ref, wo_ref, g1_ref, lg_ref, lb_ref,
                  sc2_ref, sh2_ref, wr_ref, br_ref, o_ref, u_ref, r_ref, *, alpha):
    d = x_ref.shape[-1]
    tn = (((0,), (0,)), ((), ()))
    a = lax.dot_general(on_ref[...], wn_ref[...], tn, preferred_element_type=F32)
    b = lax.dot_general(of_ref[...], wf_ref[...], tn, preferred_element_type=F32)
    gm = _sigmoid(mg_ref[...].astype(F32))
    merged = gm[:, 0:d] * a + gm[:, d:2 * d] * b
    y = _dot(merged.astype(BF16), wo_ref[...])
    z = alpha * x_ref[...] + (1.0 + g1_ref[...]) * y
    x1 = _layer_norm(z) * lg_ref[...] + lb_ref[...]
    o_ref[...] = x1
    _route_tile(x1, sc2_ref, sh2_ref, wr_ref, br_ref, u_ref, r_ref)


def _merge_project(o_nsa, o_fox, merge, x, wn, wf, wo, g1, ln_g, ln_b, sc2, sh2, w_r, b_r, alpha, tm=256):
    m, d = x.shape
    w = o_nsa.shape[0]
    resident = pl.Buffered(1)
    row = lambda i: (i, 0)
    fixed = lambda i: (0, 0)
    return pl.pallas_call(
        functools.partial(_merge_kernel, alpha=alpha),
        grid=(m // tm,),
        in_specs=[pl.BlockSpec((w, tm), lambda i: (0, i)), pl.BlockSpec((w, tm), lambda i: (0, i)),
                  pl.BlockSpec((tm, 2 * d), row), pl.BlockSpec((tm, d), row),
                  pl.BlockSpec((w, d), fixed, pipeline_mode=resident),
                  pl.BlockSpec((w, d), fixed, pipeline_mode=resident),
                  pl.BlockSpec((d, d), fixed, pipeline_mode=resident),
                  pl.BlockSpec((1, d), fixed), pl.BlockSpec((1, d), fixed), pl.BlockSpec((1, d), fixed),
                  pl.BlockSpec((1, d), fixed), pl.BlockSpec((1, d), fixed),
                  pl.BlockSpec((2, d, LANES), lambda i: (0, 0, 0)), pl.BlockSpec((1, LANES), fixed)],
        out_specs=[pl.BlockSpec((tm, d), row), pl.BlockSpec((tm, d), row), pl.BlockSpec((tm, LANES), row)],
        out_shape=[jax.ShapeDtypeStruct((m, d), F32), jax.ShapeDtypeStruct((m, d), F32),
                   jax.ShapeDtypeStruct((m, LANES), F32)],
        compiler_params=_cparams(("parallel",)),
        name="merge_project_ln_route",
    )(o_nsa, o_fox, merge, x, wn, wf, wo, g1, ln_g, ln_b, sc2, sh2, w_r, b_r)


def _route_tile(x1, sc_ref, sh_ref, w_ref, b_ref, u_ref, r_ref):
    u = _layer_norm(x1) * (1.0 + sc_ref[...]) + sh_ref[...]
    u_ref[...] = u
    u_hi = u.astype(BF16)
    u_lo = (u - u_hi.astype(F32)).astype(BF16)
    logits = _dot(u_hi, w_ref[0]) + _dot(u_lo, w_ref[0]) + _dot(u_hi, w_ref[1]) + b_ref[...]
    lane = lax.broadcasted_iota(jnp.int32, (1, LANES), 1).astype(F32)
    none = float(LANES)
    is_g = lane < N_GROUPS
    lg = jnp.where(is_g, logits, NEG_INF)
    eg = jnp.exp(lg - jnp.max(lg, axis=-1, keepdims=True))
    pg = eg / jnp.sum(eg, axis=-1, keepdims=True)
    p_grp = jnp.max(pg, axis=-1, keepdims=True)
    grp = jnp.min(jnp.where(pg == p_grp, lane, none), axis=-1, keepdims=True)
    lo = N_GROUPS + grp * EXPERTS_PER_GROUP
    is_e = (lane >= lo) & (lane < lo + EXPERTS_PER_GROUP)
    le = jnp.where(is_e, logits, NEG_INF)
    ee = jnp.exp(le - jnp.max(le, axis=-1, keepdims=True))
    pe = jnp.where(is_e, ee / jnp.sum(ee, axis=-1, keepdims=True), -1.0)
    p1 = jnp.max(pe, axis=-1, keepdims=True)
    i1 = jnp.min(jnp.where(pe == p1, lane, none), axis=-1, keepdims=True)
    pe2 = jnp.where(lane == i1, -1.0, pe)
    p2 = jnp.max(pe2, axis=-1, keepdims=True)
    i2 = jnp.min(jnp.where(pe2 == p2, lane, none), axis=-1, keepdims=True)
    den = p1 + p2
    r_ref[...] = jnp.where(lane == 0, i1 - N_GROUPS,
                           jnp.where(lane == 1, i2 - N_GROUPS,
                                     jnp.where(lane == 2, p_grp * p1 / den,
                                               jnp.where(lane == 3, p_grp * p2 / den, 0.0))))


def _moe_kernel(be_ref, nu_ref, tok_ref, tok_next_ref, dst_ref, rw_ref, u_hbm, wg_ref, wu_ref, wd_ref, out_hbm,
                xbuf, ybuf, wgb, wub, wdb, sem_in, sem_out, *, n_dump0):
    rb = ROW_BLOCK
    i = pl.program_id(0)
    last = nu_ref[0] - 1
    slot = i % 2

    def row_in(r, tok, sl):
        return pltpu.make_async_copy(u_hbm.at[pl.ds(tok, 1), :], xbuf.at[sl, pl.ds(r, 1), :], sem_in.at[sl])

    def row_out(r, dst):
        return pltpu.make_async_copy(ybuf.at[pl.ds(r, 1), :], out_hbm.at[pl.ds(dst, 1), :], sem_out)

    @pl.when(i == 0)
    def _():
        for r in range(rb):
            row_in(r, tok_ref[0, 0, r], 0).start()
        ybuf[...] = jnp.zeros(ybuf.shape, F32)
        pltpu.make_async_copy(ybuf, out_hbm.at[pl.ds(n_dump0, rb), :], sem_out).start()

    prev = be_ref[jnp.maximum(i - 1, 0)]

    @pl.when((i <= last) & ((i == 0) | (be_ref[i] != prev)))
    def _():
        wgb[...] = wg_ref[0].astype(BF16)
        wub[...] = wu_ref[0].astype(BF16)
        wdb[...] = wd_ref[0].astype(BF16)

    @pl.when(i <= last)
    def _():
        for r in range(rb):
            row_in(r, 0, slot).wait()
        xb = xbuf[slot].astype(BF16)
        for r in range(rb):
            row_in(r, tok_next_ref[0, 0, r], 1 - slot).start()
        gate = _dot(xb, wgb[...])
        up = _dot(xb, wub[...])
        hid = (gate * _sigmoid(gate)) * up
        y = _dot(hid.astype(BF16), wdb[...]) * rw_ref[0]
        for r in range(rb):
            row_out(r, 0).wait()
        ybuf[...] = y
        for r in range(rb):
            row_out(r, dst_ref[0, 0, r]).start(priority=r % 2)

    @pl.when(i == last)
    def _():
        for r in range(rb):
            row_out(r, 0).wait()
        for r in range(rb):
            row_in(r, 0, 1 - slot).wait()


def _moe_experts(u, blk_exp, n_used, row_tok, row_dst, row_w, w_gate, w_up, w_down):
    t, d = u.shape
    n_blocks = blk_exp.shape[0]
    de = w_gate.shape[-1]
    rb = ROW_BLOCK
    tok3 = row_tok.reshape(n_blocks, 1, rb)
    grid_spec = pltpu.PrefetchScalarGridSpec(
        num_scalar_prefetch=2,
        grid=(n_blocks,),
        in_specs=[pl.BlockSpec((1, 1, rb), lambda i, be, nu: (i, 0, 0), memory_space=pltpu.SMEM),
                  pl.BlockSpec((1, 1, rb), lambda i, be, nu: (jnp.minimum(i + 1, n_blocks - 1), 0, 0),
                               memory_space=pltpu.SMEM),
                  pl.BlockSpec((1, 1, rb), lambda i, be, nu: (i, 0, 0), memory_space=pltpu.SMEM),
                  pl.BlockSpec((1, rb, 1), lambda i, be, nu: (i, 0, 0)),
                  pl.BlockSpec(memory_space=pl.ANY),
                  pl.BlockSpec((1, d, de), lambda i, be, nu: (be[i], 0, 0)),
                  pl.BlockSpec((1, d, de), lambda i, be, nu: (be[i], 0, 0)),
                  pl.BlockSpec((1, de, d), lambda i, be, nu: (be[i], 0, 0))],
        out_specs=pl.BlockSpec(memory_space=pl.ANY),
        scratch_shapes=[pltpu.VMEM((2, rb, d), F32), pltpu.VMEM((rb, d), F32),
                        pltpu.VMEM((d, de), BF16), pltpu.VMEM((d, de), BF16), pltpu.VMEM((de, d), BF16),
                        pltpu.SemaphoreType.DMA((2,)), pltpu.SemaphoreType.DMA(())],
    )
    return pl.pallas_call(
        functools.partial(_moe_kernel, n_dump0=2 * t),
        grid_spec=grid_spec,
        out_shape=jax.ShapeDtypeStruct((2 * t + rb, d), F32),
        compiler_params=_cparams(("arbitrary",)),
        name="moe_experts",
    )(blk_exp, n_used, tok3, tok3, row_dst.reshape(n_blocks, 1, rb), row_w.reshape(n_blocks, rb, 1),
      u, w_gate, w_up, w_down)


def _moe_dispatch(route, t):
    k = 2
    eid = route[:, 0:k].astype(jnp.int32).reshape(-1)
    wts = route[:, k:2 * k].reshape(-1)
    n_asg = t * k
    n_rows = n_asg + N_EXPERTS * ROW_BLOCK
    n_blocks = n_rows // ROW_BLOCK
    onehot = (eid[:, None] == jnp.arange(N_EXPERTS, dtype=jnp.int32)[None, :]).astype(jnp.int32)
    rank = jnp.sum((jnp.cumsum(onehot, axis=0) - onehot) * onehot, axis=1)
    counts = jnp.sum(onehot, axis=0)
    padded = (counts + ROW_BLOCK - 1) // ROW_BLOCK * ROW_BLOCK
    pad_end = jnp.cumsum(padded)
    pad_start = pad_end - padded
    dest = jnp.sum(onehot * pad_start[None, :], axis=1) + rank
    asg = jnp.arange(n_asg, dtype=jnp.int32)
    upd = jnp.stack([(asg % k) * t + asg // k, lax.bitcast_convert_type(wts, jnp.int32)], axis=1)
    init = jnp.stack([n_asg + jnp.arange(n_rows, dtype=jnp.int32) % ROW_BLOCK,
                      jnp.zeros((n_rows,), jnp.int32)], axis=1)
    rows = init.at[dest].set(upd)
    row_dst = rows[:, 0]
    row_w = lax.bitcast_convert_type(rows[:, 1], F32)
    row_tok = jnp.where(row_dst < n_asg, row_dst % t, 0)
    blk_start = jnp.arange(n_blocks, dtype=jnp.int32) * ROW_BLOCK
    blk_exp = jnp.minimum(jnp.sum((pad_end[None, :] <= blk_start[:, None]).astype(jnp.int32), axis=1),
                          N_EXPERTS - 1)
    n_used = (pad_end[N_EXPERTS - 1:] // ROW_BLOCK).astype(jnp.int32)
    return blk_exp, n_used, row_tok, row_dst, row_w


def _final_kernel(x_ref, y0_ref, y1_ref, g2_ref, lg_ref, lb_ref, o_ref, *, alpha):
    z = alpha * x_ref[...] + (1.0 + g2_ref[...]) * (y0_ref[...] + y1_ref[...])
    o_ref[...] = _layer_norm(z) * lg_ref[...] + lb_ref[...]


def _final_ln(x1, y2, g2, ln_g, ln_b, alpha, tm=512):
    m, d = x1.shape
    nb = m // tm
    fixed = lambda i: (0, 0)
    return pl.pallas_call(
        functools.partial(_final_kernel, alpha=alpha),
        grid=(nb,),
        in_specs=[pl.BlockSpec((tm, d), lambda i: (i, 0)),
                  pl.BlockSpec((tm, d), lambda i: (i, 0)),
                  pl.BlockSpec((tm, d), lambda i: (i + nb, 0)),
                  pl.BlockSpec((1, d), fixed), pl.BlockSpec((1, d), fixed), pl.BlockSpec((1, d), fixed)],
        out_specs=pl.BlockSpec((tm, d), lambda i: (i, 0)),
        out_shape=jax.ShapeDtypeStruct((m, d), F32),
        compiler_params=_cparams(("parallel",)),
        name="final_ln",
    )(x1, y2, y2, g2, ln_g, ln_b)


def _to_lane_blocks(a, tq):
    g, hpg, s, c = a.shape
    return a.reshape(g, hpg, s // tq, tq, c).transpose(0, 2, 4, 1, 3).reshape(g, s // tq, c, hpg * tq)


def _layer(x2d, c, w_ada, b_ada, w_in, b_fgt, t5_table, cmp_pe, cmp_w1, cmp_b1, cmp_w2,
           w_br_nsa, w_br_fox, w_o, ln1_g, ln1_b, w_rg, b_rg, w_re, b_re,
           w_gate, w_up, w_down, ln2_g, ln2_b, alpha):
    s, d = x2d.shape
    hd = HEAD_DIM
    g = NSA_GROUPS
    mod = _ada_mod(c, w_ada, b_ada)
    sh1, sc1, g1, sh2, sc2, g2 = [mod[:, i * d:(i + 1) * d] for i in range(6)]

    c_q = NSA_HEADS * hd
    c_kv = 6 * g * hd
    c_gate = 3 * NSA_HEADS
    c_fox = 3 * FOX_HEADS * hd
    off_kv = c_q
    off_gate = off_kv + c_kv
    off_fox = off_gate + c_gate
    off_fgt = off_fox + c_fox
    off_merge = off_fgt + FOX_HEADS
    qscale = hd ** -0.5 * LOG2E
    nh = FOX_HEADS
    gw = g * hd

    def kv_cols(z):
        return w_in[:, off_kv + z * gw:off_kv + (z + 1) * gw]

    def lane_padded(w, heads):
        return jnp.pad(w.reshape(d, heads, hd), ((0, 0), (0, 0), (0, AUG_K - hd))).reshape(d, heads * AUG_K)

    fox_q, fox_k, fox_v = [w_in[:, off_fox + i * nh * hd:off_fox + (i + 1) * nh * hd] for i in range(3)]
    w_ch = jnp.concatenate([w_in[:, 0:off_kv] * qscale, kv_cols(3), kv_cols(5), fox_q * qscale, fox_v],
                           axis=1).T.astype(BF16)
    w_tok = jnp.concatenate([kv_cols(0), kv_cols(1), lane_padded(kv_cols(2), g), lane_padded(kv_cols(4), g),
                             lane_padded(fox_k, nh)], axis=1).astype(BF16)
    n_small = c_gate + FOX_HEADS
    w_small = jnp.concatenate([w_in[:, off_gate:off_fox], w_in[:, off_fgt:off_merge],
                               jnp.zeros((d, LANES - n_small), F32)], axis=1).astype(BF16)
    w_merge = w_in[:, off_merge:].astype(BF16)

    u = _ln_mod(x2d, sc1, sh1, BF16, tm=PROJ_TM)
    ch = _matmul_nt(w_ch, u, BF16, PROJ_TM, "in_proj_channel_major")
    tok = _matmul(u, w_tok, BF16, PROJ_TM, w_tok.shape[1] // 2, "in_proj_token_major")
    small = _matmul(u, w_small, F32, PROJ_TM, LANES, "in_proj_small")
    merge = _matmul(u, w_merge, BF16, PROJ_TM, 1024, "in_proj_merge")

    qt_nsa = ch[0:c_q].reshape(g, NSA_HPG, hd, s)
    ch = ch.reshape(-1, hd, s)
    ch_vs_block0 = NSA_HEADS
    ch_vw_block0 = ch_vs_block0 + g
    ch_fq_block0 = ch_vw_block0 + g
    ch_fv_block0 = ch_fq_block0 + nh
    tok_ks_block0 = 2 * gw // AUG_K
    tok_kw_block0 = tok_ks_block0 + g
    tok_fox_block0 = tok_kw_block0 + g

    kv_cmp_in = tok[:, 0:2 * gw].reshape(s, 2, g, hd).transpose(1, 2, 0, 3)
    kv_cmp = _compress(kv_cmp_in, cmp_pe, cmp_w1, cmp_b1, cmp_w2)
    nch = s // CMP_STRIDE
    ncp = nch + LANES
    nslc = s // SLC_LEN
    kv_cmp_pad = jnp.pad(kv_cmp.astype(BF16), ((0, 0), (0, 0), (CMP_PAD, ncp - nch - CMP_PAD), (0, 0)))
    b_cmp, b_sel, b_win, far3 = _nsa_bias_tables(t5_table)

    row_ix = np.arange(ncp)
    row_ok = (row_ix >= CMP_PAD) & (row_ix < nch - 1 + CMP_PAD)
    row_cols = np.zeros((ncp, 2 * AUG_K - hd), np.float32)
    row_cols[:, ROW_BIAS - hd:ROW_BIAS - hd + 3] = row_ok[:, None]
    row_cols[:, ROW_BIAS - hd + 3] = ~row_ok
    row_cols[row_ix, AUG_K - hd + row_ix // BF16_ROWS] = 1.0
    kc_aug = jnp.concatenate([kv_cmp_pad[0], jnp.broadcast_to(jnp.asarray(row_cols, BF16), (g,) + row_cols.shape)],
                             axis=-1)
    vct = kv_cmp_pad[1].transpose(0, 2, 1)
    nq = s // ATT_TQ
    qt = qt_nsa.reshape(g, NSA_HPG, hd, nq, ATT_TQ).transpose(0, 3, 2, 1, 4).reshape(g, nq, hd, NSA_HPG * ATT_TQ)
    oct_, selt = _nsa_compress_select(qt, kc_aug, vct, b_cmp, far3)

    period = BF16_ROWS * SLC_LEN
    k_extra = np.zeros((period, AUG_K), np.float32)
    k_extra[np.arange(period), ROW_MASK + np.arange(period) // SLC_LEN] = 1.0
    k_extra[:, ROW_BIAS:ROW_BIAS + 3] = 1.0
    gates = small[:, 0:c_gate].reshape(s, g, NSA_HPG, 3).transpose(1, 2, 0, 3)
    gates_t = jnp.pad(_to_lane_blocks(gates, ATT_TQ), ((0, 0), (0, 0), (0, 5), (0, 0)))
    o_nsa_t = _nsa_attention(qt, tok, tok_ks_block0, tok_kw_block0, jnp.asarray(k_extra, BF16), ch, ch_vs_block0,
                             ch_vw_block0, selt, far3, b_sel, b_win, oct_, gates_t)
    o_nsa = o_nsa_t.reshape(MIX_W, s)

    fgt_bias = jnp.concatenate([jnp.zeros((c_gate,), F32), b_fgt, jnp.zeros((LANES - n_small,), F32)])[None, :]
    decay_k, decay_ends = _decay_pieces(small, fgt_bias, c_gate, nh)
    o_fox = _fox_attention(ch, ch_fq_block0, ch_fv_block0, tok, tok_fox_block0, decay_k, decay_ends)
    o_fox = o_fox.reshape(MIX_W, s)

    n_r = N_GROUPS + N_EXPERTS
    w_r = jnp.concatenate([w_rg, w_re.reshape(d, N_EXPERTS), jnp.zeros((d, LANES - n_r), F32)], axis=1)
    b_r = jnp.concatenate([b_rg, b_re.reshape(N_EXPERTS), jnp.zeros((LANES - n_r,), F32)])[None, :]
    w_r_hi, w_r_lo, _ = _split3_exact(w_r)
    x1, u2, route = _merge_project(o_nsa, o_fox, merge, x2d, w_br_nsa.astype(BF16), w_br_fox.astype(BF16),
                                   w_o.astype(BF16), g1, ln1_g[None, :], ln1_b[None, :], sc2, sh2,
                                   jnp.stack([w_r_hi, w_r_lo]), b_r, alpha)

    blk_exp, n_used, row_tok, row_dst, row_w = _moe_dispatch(route, s)
    y2 = _moe_experts(u2, blk_exp, n_used, row_tok, row_dst, row_w, w_gate, w_up, w_down)
    return _final_ln(x1, y2, g2, ln2_g[None, :], ln2_b[None, :], alpha)


def kernel(x, c, w_ada, b_ada, w_in, b_fgt, t5_table, cmp_pe, cmp_w1, cmp_b1, cmp_w2, w_br_nsa, w_br_fox, w_o,
           ln1_g, ln1_b, w_rg, b_rg, w_re, b_re, w_gate, w_up, w_down, ln2_g, ln2_b):
    b, s, d = x.shape
    depth = w_ada.shape[0]
    assert b == 1
    alpha = (2 * depth) ** 0.25
    h = x[0]
    for l in range(depth):
        h = _layer(h, c, w_ada[l], b_ada[l], w_in[l], b_fgt[l], t5_table, cmp_pe[l], cmp_w1[l], cmp_b1[l],
                   cmp_w2[l], w_br_nsa[l], w_br_fox[l], w_o[l], ln1_g[l], ln1_b[l], w_rg[l], b_rg[l],
                   w_re[l], b_re[l], w_gate[l], w_up[l], w_down[l], ln2_g[l], ln2_b[l], alpha)
    return h[None]
```

```python
import functools
import math

import numpy as np
import jax
import jax.numpy as jnp
from jax import lax
from jax.experimental import pallas as pl
from jax.experimental.pallas import tpu as pltpu

F32 = jnp.float32
BF16 = jnp.bfloat16
HIGHEST = lax.Precision.HIGHEST
LOG2E = math.log2(math.e)

HEAD_DIM = 64
NSA_HEADS = 8
NSA_GROUPS = 2
NSA_HPG = NSA_HEADS // NSA_GROUPS
FOX_HEADS = 8
MIX_W = NSA_HEADS * HEAD_DIM
CMP_LEN = 32
CMP_STRIDE = 16
SLC_LEN = 64
SLC_TOPK = 16
WINDOW = 512
T5_BUCKETS = 32
T5_MAX_EXACT = 16
T5_MAX_DIST = 128
N_GROUPS = 8
EXPERTS_PER_GROUP = 8
N_EXPERTS = N_GROUPS * EXPERTS_PER_GROUP
ROW_BLOCK = 128
LN_EPS = 1e-5
NEG_INF = -1e30
M_INIT = -1e29
FORCE_SCORE = 1e4

LANES = 128
BF16_ROWS = 16
CMP_PAD = 8
ATT_TQ = 256
CMP_NEAR_ROWS = 32
FOX_TQ = 1024
FOX_TK = 512
FOX_SKIP_MARGIN = 160.0
FOX_BOUND_SLACK = 1.02
PROJ_TM = 1024
AUG_K = 128
AUG_V = HEAD_DIM + 16
ROW_MASK = HEAD_DIM
ROW_BIAS = HEAD_DIM + 16
VMEM_LIMIT = 56 * 1024 * 1024


def _cparams(sem, vmem=VMEM_LIMIT):
    return pltpu.CompilerParams(dimension_semantics=sem, vmem_limit_bytes=vmem)


def _sigmoid(x):
    return 1.0 / (1.0 + jnp.exp(-x))


def _layer_norm(x):
    mu = jnp.mean(x, axis=-1, keepdims=True)
    xc = x - mu
    var = jnp.mean(xc * xc, axis=-1, keepdims=True)
    return xc * lax.rsqrt(var + LN_EPS)


def _split3(x):
    hi = x.astype(BF16)
    r1 = x - hi.astype(F32)
    mid = r1.astype(BF16)
    lo = (r1 - mid.astype(F32)).astype(BF16)
    return hi, mid, lo


def _split3_exact(x):
    def trunc(v):
        bits = lax.bitcast_convert_type(v, jnp.uint32) & jnp.uint32(0xFFFF0000)
        return lax.bitcast_convert_type(bits, F32)
    hi = trunc(x)
    r1 = x - hi
    mid = trunc(r1)
    lo = r1 - mid
    return hi.astype(BF16), mid.astype(BF16), lo.astype(BF16)


def _dot(a, b):
    return jnp.dot(a, b, preferred_element_type=F32)


def _dot_nt(a, b):
    return lax.dot_general(a, b, (((1,), (1,)), ((), ())), preferred_element_type=F32)


def _dot3(x, w_bf16):
    hi, mid, lo = _split3(x)
    return _dot(hi, w_bf16) + _dot(mid, w_bf16) + _dot(lo, w_bf16)


def _dot3_rhs(w_bf16, x):
    hi, mid, lo = _split3(x)
    return _dot(w_bf16, hi) + _dot(w_bf16, mid) + _dot(w_bf16, lo)


def _ada_kernel(c_ref, w_ref, b_ref, o_ref):
    c = c_ref[...]
    a = c * _sigmoid(c)
    o_ref[...] = jnp.dot(a, w_ref[...], precision=HIGHEST, preferred_element_type=F32) + b_ref[...]


def _ada_mod(c, w, b):
    d, n = w.shape
    tn = 1024
    c8 = jnp.broadcast_to(c, (8, d))
    out = pl.pallas_call(
        _ada_kernel,
        grid=(n // tn,),
        in_specs=[pl.BlockSpec((8, d), lambda j: (0, 0)),
                  pl.BlockSpec((d, tn), lambda j: (0, j)),
                  pl.BlockSpec((1, tn), lambda j: (0, j))],
        out_specs=pl.BlockSpec((8, tn), lambda j: (0, j)),
        out_shape=jax.ShapeDtypeStruct((8, n), F32),
        compiler_params=_cparams(("parallel",)),
        name="ada_mod",
    )(c8, w, b.reshape(1, n))
    return out[0:1]


def _lnmod_kernel(x_ref, sc_ref, sh_ref, o_ref):
    y = _layer_norm(x_ref[...])
    o_ref[...] = (y * (1.0 + sc_ref[...]) + sh_ref[...]).astype(o_ref.dtype)


def _ln_mod(x, sc, sh, out_dtype, tm=512):
    m, d = x.shape
    return pl.pallas_call(
        _lnmod_kernel,
        grid=(m // tm,),
        in_specs=[pl.BlockSpec((tm, d), lambda i: (i, 0)),
                  pl.BlockSpec((1, d), lambda i: (0, 0)),
                  pl.BlockSpec((1, d), lambda i: (0, 0))],
        out_specs=pl.BlockSpec((tm, d), lambda i: (i, 0)),
        out_shape=jax.ShapeDtypeStruct((m, d), out_dtype),
        compiler_params=_cparams(("parallel",)),
        name="ln_mod",
    )(x, sc, sh)


def _mm_kernel(a_ref, w_ref, o_ref):
    o_ref[...] = _dot(a_ref[...], w_ref[...]).astype(o_ref.dtype)


def _matmul(a, w, out_dtype, tm, tn, name):
    m, k = a.shape
    n = w.shape[1]
    return pl.pallas_call(
        _mm_kernel,
        grid=(n // tn, m // tm),
        in_specs=[pl.BlockSpec((tm, k), lambda j, i: (i, 0)),
                  pl.BlockSpec((k, tn), lambda j, i: (0, j))],
        out_specs=pl.BlockSpec((tm, tn), lambda j, i: (i, j)),
        out_shape=jax.ShapeDtypeStruct((m, n), out_dtype),
        compiler_params=_cparams(("parallel", "parallel")),
        name=name,
    )(a, w)


def _mm_nt_kernel(w_ref, a_ref, o_ref):
    o_ref[...] = _dot_nt(w_ref[...], a_ref[...]).astype(o_ref.dtype)


def _matmul_nt(w_t, a, out_dtype, tm, name):
    n, k = w_t.shape
    m = a.shape[0]
    return pl.pallas_call(
        _mm_nt_kernel,
        grid=(m // tm,),
        in_specs=[pl.BlockSpec((n, k), lambda i: (0, 0), pipeline_mode=pl.Buffered(1)),
                  pl.BlockSpec((tm, k), lambda i: (i, 0))],
        out_specs=pl.BlockSpec((n, tm), lambda i: (0, i)),
        out_shape=jax.ShapeDtypeStruct((n, m), out_dtype),
        compiler_params=_cparams(("parallel",)),
        name=name,
    )(w_t, a)


def _gelu_tanh(x):
    return 0.5 * x * (1.0 + jnp.tanh(math.sqrt(2.0 / math.pi) * (x + 0.044715 * (x * x * x))))


def _compress_kernel(c_ref, pe_ref, w1a_ref, w1b_ref, b1_ref, w2_ref, o_ref, *, nch):
    c = c_ref[0, 0]
    w1a = w1a_ref[0]
    w1b = w1b_ref[0]
    half = CMP_STRIDE * HEAD_DIM
    a = _dot(c, w1a)
    b = _dot(c, w1b)
    b_next = pltpu.roll(b, shift=nch - 1, axis=0)
    pe = pe_ref[0]
    pb = _dot(pe[:, :half], w1a) + _dot(pe[:, half:], w1b)
    hid = _gelu_tanh(a + b_next + pb[0:1, :] + b1_ref[0])
    o_ref[0, 0] = _dot(hid.astype(BF16), w2_ref[0])


def _compress(kv_cmp, pe, w1, b1, w2):
    z, g, s, hd = kv_cmp.shape
    nch = s // CMP_STRIDE
    half = CMP_STRIDE * hd
    chunks = kv_cmp.reshape(z, g, nch, half)
    pe8 = jnp.broadcast_to(pe.reshape(z, 1, CMP_LEN * hd), (z, 8, CMP_LEN * hd)).astype(BF16)
    w1b16 = w1.astype(BF16)
    hidn = w1.shape[-1]
    return pl.pallas_call(
        functools.partial(_compress_kernel, nch=nch),
        grid=(z, g),
        in_specs=[pl.BlockSpec((1, 1, nch, half), lambda zi, gi: (zi, gi, 0, 0)),
                  pl.BlockSpec((1, 8, 2 * half), lambda zi, gi: (zi, 0, 0)),
                  pl.BlockSpec((1, half, hidn), lambda zi, gi: (zi, 0, 0)),
                  pl.BlockSpec((1, half, hidn), lambda zi, gi: (zi, 1, 0)),
                  pl.BlockSpec((1, 1, hidn), lambda zi, gi: (zi, 0, 0)),
                  pl.BlockSpec((1, hidn, hd), lambda zi, gi: (zi, 0, 0))],
        out_specs=pl.BlockSpec((1, 1, nch, hd), lambda zi, gi: (zi, gi, 0, 0)),
        out_shape=jax.ShapeDtypeStruct((z, g, nch, hd), F32),
        compiler_params=_cparams(("parallel", "parallel")),
        name="compress_kv",
    )(chunks, pe8, w1b16, w1b16, b1.reshape(z, 1, hidn), w2.astype(BF16))


def _t5_bucket_np(dist):
    n = np.maximum(dist, 0)
    ratio = np.log(np.maximum(n, T5_MAX_EXACT).astype(np.float64) / T5_MAX_EXACT)
    big = T5_MAX_EXACT + (ratio / math.log(T5_MAX_DIST / T5_MAX_EXACT)
                          * (T5_BUCKETS - T5_MAX_EXACT)).astype(np.int64)
    return np.where(n < T5_MAX_EXACT, n, np.minimum(big, T5_BUCKETS - 1)).astype(np.int32)


def _t5_lookup(tbh, dist):
    onehot = np.eye(T5_BUCKETS, dtype=np.float32)[_t5_bucket_np(dist).reshape(-1)]
    vals = jnp.einsum('ghb,nb->ghn', tbh, jnp.asarray(onehot), precision=HIGHEST)
    return vals.reshape(tbh.shape[:2] + dist.shape)


def _toeplitz_kernel(w_ref, o_ref, *, n_keys, tq, lead_zero, trail_masked):
    width = w_ref.shape[-1]
    rows = jnp.broadcast_to(w_ref[0, 0], (n_keys, width))
    rolled = pltpu.roll(rows, shift=width - (n_keys - 1), axis=1, stride=1, stride_axis=0)
    parts = [rolled[:, 0:tq]]
    if lead_zero:
        parts.insert(0, jnp.zeros((lead_zero, tq), F32))
    if trail_masked:
        parts.append(jnp.full((trail_masked, tq), NEG_INF, F32))
    o_ref[0] = jnp.concatenate(parts, axis=0)


def _att_table_t(tbh, n_keys, tq, lo, hi, minus_far, lead_zero=0, trail_masked=0):
    length = n_keys + tq - 1
    d = np.arange(length) - (tq - 1)
    valid = (d >= lo) & (d < hi)
    vals = _t5_lookup(tbh, d)
    if minus_far:
        vals = vals - tbh[:, :, T5_BUCKETS - 1:]
    w = jnp.where(jnp.asarray(valid), vals * LOG2E, NEG_INF)
    width = pl.next_power_of_2(length)
    w = jnp.pad(w, ((0, 0), (0, 0), (0, width - length)))[:, :, None, :]
    g, hpg = tbh.shape[:2]
    n_rows = lead_zero + n_keys + trail_masked
    return pl.pallas_call(
        functools.partial(_toeplitz_kernel, n_keys=n_keys, tq=tq, lead_zero=lead_zero, trail_masked=trail_masked),
        grid=(g, hpg),
        in_specs=[pl.BlockSpec((1, 1, 1, width), lambda gi, hi: (gi, hi, 0, 0))],
        out_specs=pl.BlockSpec((1, n_rows, tq), lambda gi, hi: (gi, 0, hi)),
        out_shape=jax.ShapeDtypeStruct((g, n_rows, hpg * tq), F32),
        compiler_params=_cparams(("parallel", "parallel")),
        name="t5_toeplitz",
    )(w)


def _nsa_bias_tables(t5_table):
    tbh = t5_table.T.reshape(NSA_GROUPS, NSA_HPG, T5_BUCKETS).astype(F32)
    j = np.arange(CMP_NEAR_ROWS)[:, None]
    i = np.arange(ATT_TQ)[None, :]
    dist = i - (CMP_LEN - 1) - CMP_STRIDE * (j - CMP_PAD)
    vals = (_t5_lookup(tbh, dist) - tbh[:, :, T5_BUCKETS - 1][:, :, None, None]) * LOG2E
    vals = jnp.where(jnp.asarray(dist >= 0)[None, None], vals, NEG_INF)
    b_cmp = vals.transpose(0, 2, 1, 3).reshape(NSA_GROUPS, CMP_NEAR_ROWS, NSA_HPG * ATT_TQ)
    far = tbh[:, :, T5_BUCKETS - 1] * LOG2E
    b_sel = _att_table_t(tbh, 2 * ATT_TQ, ATT_TQ, 0, 1 << 30, True, lead_zero=ATT_TQ)
    b_win = _att_table_t(tbh, 3 * ATT_TQ, ATT_TQ, 0, WINDOW, False, trail_masked=ATT_TQ)
    hi, mid, lo = _split3_exact(jnp.repeat(far, ATT_TQ, axis=1))
    far3 = jnp.stack([hi, mid, lo, jnp.full(hi.shape, NEG_INF, BF16)], axis=1)
    far3 = jnp.pad(far3, ((0, 0), (0, BF16_ROWS - 4), (0, 0)))
    return b_cmp, b_sel, b_win, far3


def _nsa_cmp_kernel(qt_ref, kc_ref, vct_ref, near_ref, far3_ref, oct_ref, selt_ref, rhs, s_scr, imp_scr, *, nslc):
    tq = ATT_TQ
    lanes = NSA_HPG * tq
    qb = pl.program_id(1)
    cpb = tq // CMP_STRIDE
    rhs[...] = jnp.zeros(rhs.shape, BF16)
    rhs[0:HEAD_DIM, :] = qt_ref[0, 0]
    rhs[ROW_BIAS:ROW_BIAS + BF16_ROWS, :] = far3_ref[0]
    chunk = lax.broadcasted_iota(jnp.int32, (AUG_K, lanes), 0)
    rhs[AUG_K:2 * AUG_K, :] = jnp.where(chunk >= qb + CMP_NEAR_ROWS // BF16_ROWS, NEG_INF, 0.0).astype(BF16)
    r0 = pl.multiple_of(cpb * qb, BF16_ROWS)
    n_lane_blocks = tq // LANES
    ncp = s_scr.shape[0]
    imp_scr[...] = jnp.zeros(imp_scr.shape, F32)

    def attend(rows):
        s_scr[0:rows, :] = _dot(kc_ref[0, 0:rows, :], rhs[...])
        s_scr[pl.ds(r0, CMP_NEAR_ROWS), :] = s_scr[pl.ds(r0, CMP_NEAR_ROWS), :] + near_ref[0]
        s = s_scr[0:rows, :]
        m = jnp.max(s, axis=0, keepdims=True)
        e = jnp.exp2(s - m)
        l = jnp.sum(e, axis=0, keepdims=True)
        p = e * jnp.where(m > M_INIT, 1.0 / l, 0.0)
        oct_ref[0, 0] = _dot(vct_ref[0, :, 0:rows], p.astype(BF16))
        imp = p[:, 0:tq]
        for h in range(1, NSA_HPG):
            imp = imp + p[:, h * tq:(h + 1) * tq]
        for c in range(n_lane_blocks):
            imp_scr[c, 0:rows, :] = imp[:, c * LANES:(c + 1) * LANES]

    limits = sorted({min(ncp, -(-(ncp * k // 3) // LANES) * LANES) for k in (1, 2, 3)})
    lo_qb = 0
    for rows in limits:
        hi_qb = (rows - CMP_NEAR_ROWS) // cpb if rows < ncp else pl.num_programs(1) - 1

        @pl.when((qb >= lo_qb) & (qb <= hi_qb))
        def _(rows=rows):
            attend(rows)

        lo_qb = hi_qb + 1
    ratio = SLC_LEN // CMP_STRIDE

    def taps(off):
        return jnp.concatenate([imp_scr[c, pl.ds(CMP_PAD + off, nslc, stride=ratio), :]
                                for c in range(n_lane_blocks)], axis=1)

    p_slc = 0.5 * (taps(-1) + taps(ratio - 1))
    for off in range(ratio - 1):
        p_slc = p_slc + taps(off)
    blk = lax.broadcasted_iota(jnp.int32, (nslc, tq), 0)
    cur = (qb * tq + lax.broadcasted_iota(jnp.int32, (nslc, tq), 1)) // SLC_LEN
    forced = (blk == 0) | (blk == cur) | (blk == cur - 1)
    score = jnp.where(forced, FORCE_SCORE, jnp.where(blk <= cur, p_slc, -1.0))
    blk_f = blk.astype(F32)
    sel = jnp.zeros((nslc, tq), F32)
    for _ in range(min(SLC_TOPK, nslc)):
        mx = jnp.max(score, axis=0, keepdims=True)
        first = jnp.min(jnp.where(score == mx, blk_f, float(nslc)), axis=0, keepdims=True)
        hit = blk_f == first
        sel = jnp.where(hit, 1.0, sel)
        score = jnp.where(hit, -2.0, score)
    selt_ref[0] = sel.astype(BF16)


def _nsa_compress_select(qt, kc_aug, vct, b_cmp, far3):
    g, nq, hd, lanes = qt.shape
    ncp = kc_aug.shape[1]
    tq = ATT_TQ
    nslc = nq * tq // SLC_LEN
    assert ncp // BF16_ROWS <= AUG_K
    assert tq // CMP_STRIDE == BF16_ROWS
    per_q = lambda gi, qi: (gi, qi, 0, 0)
    per_g = lambda gi, qi: (gi, 0, 0)
    return pl.pallas_call(
        functools.partial(_nsa_cmp_kernel, nslc=nslc),
        grid=(g, nq),
        in_specs=[pl.BlockSpec((1, 1, hd, lanes), per_q),
                  pl.BlockSpec((1, ncp, 2 * AUG_K), per_g),
                  pl.BlockSpec((1, hd, ncp), per_g),
                  pl.BlockSpec((1, CMP_NEAR_ROWS, lanes), per_g),
                  pl.BlockSpec((1, BF16_ROWS, lanes), per_g)],
        out_specs=[pl.BlockSpec((1, 1, hd, lanes), per_q),
                   pl.BlockSpec((1, nslc, tq), lambda gi, qi: (gi, 0, qi))],
        out_shape=[jax.ShapeDtypeStruct((g, nq, hd, lanes), F32),
                   jax.ShapeDtypeStruct((g, nslc, nq * tq), BF16)],
        scratch_shapes=[pltpu.VMEM((2 * AUG_K, lanes), BF16), pltpu.VMEM((ncp, lanes), F32),
                        pltpu.VMEM((tq // LANES, ncp, LANES), F32)],
        compiler_params=_cparams(("parallel", "parallel")),
        name="nsa_compress_select",
    )(qt, kc_aug, vct, b_cmp, far3)


def _flash_init_t(m_ref, acc_ref):
    m_ref[...] = jnp.full(m_ref.shape, M_INIT, F32)
    acc_ref[...] = jnp.zeros(acc_ref.shape, F32)


def _flash_step_t(s, vt_tile, m_ref, acc_ref):
    m_old = m_ref[...]
    m_new = jnp.maximum(m_old, jnp.max(s, axis=0, keepdims=True))
    p = jnp.exp2(s - m_new).astype(BF16)
    acc_ref[...] = jnp.exp2(m_old - m_new) * acc_ref[...] + _dot(vt_tile, p)
    m_ref[...] = m_new


def _flash_result_t(acc_ref):
    acc = acc_ref[...]
    return acc[0:HEAD_DIM, :] / acc[HEAD_DIM:HEAD_DIM + 1, :]


def _nsa_att_kernel(qt_ref, ks_ref, kx_ref, vst_ref, kw_ref, vwt_ref, selt_ref, far3_ref, bsel_ref, bwin_ref,
                    oct_ref, gate_ref, o_ref, rhs_s, rhs_w, mask_t, ms, accs, mw, accw, s_even, s_odd):
    tq = ATT_TQ
    qb = pl.program_id(1)
    ones_rows = _ones_rows(tq)
    kx_tiles = kx_ref.shape[0] // tq
    qt = qt_ref[0, 0]
    rhs_s[...] = jnp.zeros(rhs_s.shape, BF16)
    rhs_s[0:HEAD_DIM, :] = qt
    rhs_s[ROW_BIAS:ROW_BIAS + BF16_ROWS, :] = far3_ref[0]
    rhs_w[...] = jnp.zeros(rhs_w.shape, BF16)
    rhs_w[0:HEAD_DIM, :] = qt
    madd = ((selt_ref[0].astype(F32) - 1.0) * (-NEG_INF)).astype(BF16)
    mask_t[...] = jnp.concatenate([madd] * NSA_HPG, axis=1)
    _flash_init_t(ms, accs)
    _flash_init_t(mw, accw)
    blocks_per_tile = tq // SLC_LEN

    def sel_scores(kt, s_ref):
        kt = jnp.minimum(kt, qb)
        k0 = pl.multiple_of(kt * tq, tq)
        chunk = pl.multiple_of((kt * blocks_per_tile) // BF16_ROWS * BF16_ROWS, BF16_ROWS)
        rhs_s[ROW_MASK:ROW_MASK + BF16_ROWS, :] = mask_t[pl.ds(chunk, BF16_ROWS), :]
        rel = jnp.clip(kt - qb + 2, 0, 2)
        table = bsel_ref[0, pl.ds(pl.multiple_of(rel * tq, tq), tq), :]
        k_aug = ks_ref[pl.ds(k0, tq), :] + kx_ref[pl.ds(pl.multiple_of((kt % kx_tiles) * tq, tq), tq), :]
        s_ref[...] = _dot(k_aug, rhs_s[...]) + table

    def sel_consume(kt, s_ref):
        k0 = pl.multiple_of(kt * tq, tq)
        vt_aug = jnp.concatenate([vst_ref[0, :, pl.ds(k0, tq)], ones_rows], axis=0)
        _flash_step_t(s_ref[...], vt_aug, ms, accs)

    def win_scores(j, s_ref):
        kt = qb - 2 + j
        k0 = pl.multiple_of(jnp.maximum(kt, 0) * tq, tq)
        row = pl.multiple_of(jnp.where(kt >= 0, j, 3) * tq, tq)
        s_ref[...] = _dot(kw_ref[pl.ds(k0, tq), :], rhs_w[...]) + bwin_ref[0, pl.ds(row, tq), :]

    def win_consume(j, s_ref):
        k0 = pl.multiple_of(jnp.maximum(qb - 2 + j, 0) * tq, tq)
        vt_aug = jnp.concatenate([vwt_ref[0, :, pl.ds(k0, tq)], ones_rows], axis=0)
        _flash_step_t(s_ref[...], vt_aug, mw, accw)

    n_sel = qb + 1
    sel_scores(0, s_even)

    def pair_body(j, carry):
        sel_scores(2 * j + 1, s_odd)
        sel_consume(2 * j, s_even)
        sel_scores(2 * j + 2, s_even)
        sel_consume(2 * j + 1, s_odd)
        return carry

    lax.fori_loop(0, n_sel // 2, pair_body, 0)
    win_scores(0, s_odd)

    @pl.when(n_sel % 2 == 1)
    def _():
        sel_consume(qb, s_even)

    win_scores(1, s_even)
    win_consume(0, s_odd)
    win_scores(2, s_odd)
    win_consume(1, s_even)
    win_consume(2, s_odd)

    gt = _sigmoid(gate_ref[0, 0])
    out = gt[0:1, :] * oct_ref[0, 0] + gt[1:2, :] * _flash_result_t(accs) + gt[2:3, :] * _flash_result_t(accw)
    for h in range(NSA_HPG):
        o_ref[0, h] = out[:, h * tq:(h + 1) * tq].astype(o_ref.dtype)


def _nsa_attention(qt, k_tok, ks_block0, kw_block0, k_extra, ch, vs_block0, vw_block0, selt, far3, b_sel, b_win,
                   oct_, gates_t):
    g, nq, hd, lanes = qt.shape
    s = k_tok.shape[0]
    nslc = selt.shape[1]
    tq = ATT_TQ
    resident = pl.Buffered(1)
    per_q = lambda gi, qi: (gi, qi, 0, 0)
    per_g = lambda gi, qi: (gi, 0, 0)
    return pl.pallas_call(
        _nsa_att_kernel,
        grid=(g, nq),
        in_specs=[pl.BlockSpec((1, 1, hd, lanes), per_q),
                  pl.BlockSpec((s, AUG_K), lambda gi, qi: (0, ks_block0 + gi), pipeline_mode=resident),
                  pl.BlockSpec(k_extra.shape, lambda gi, qi: (0, 0), pipeline_mode=resident),
                  pl.BlockSpec((1, hd, s), lambda gi, qi: (vs_block0 + gi, 0, 0), pipeline_mode=resident),
                  pl.BlockSpec((s, AUG_K), lambda gi, qi: (0, kw_block0 + gi), pipeline_mode=resident),
                  pl.BlockSpec((1, hd, s), lambda gi, qi: (vw_block0 + gi, 0, 0), pipeline_mode=resident),
                  pl.BlockSpec((1, nslc, tq), lambda gi, qi: (gi, 0, qi)),
                  pl.BlockSpec((1, BF16_ROWS, lanes), per_g),
                  pl.BlockSpec((1, 3 * tq, lanes), per_g, pipeline_mode=resident),
                  pl.BlockSpec((1, 4 * tq, lanes), per_g, pipeline_mode=resident),
                  pl.BlockSpec((1, 1, hd, lanes), per_q),
                  pl.BlockSpec((1, 1, 8, lanes), per_q)],
        out_specs=pl.BlockSpec((1, lanes // tq, hd, tq), lambda gi, qi: (gi, 0, 0, qi)),
        out_shape=jax.ShapeDtypeStruct((g, lanes // tq, hd, s), BF16),
        scratch_shapes=[pltpu.VMEM((AUG_K, lanes), BF16), pltpu.VMEM((AUG_K, lanes), BF16),
                        pltpu.VMEM((nslc, lanes), BF16),
                        pltpu.VMEM((1, lanes), F32), pltpu.VMEM((AUG_V, lanes), F32),
                        pltpu.VMEM((1, lanes), F32), pltpu.VMEM((AUG_V, lanes), F32),
                        pltpu.VMEM((tq, lanes), F32), pltpu.VMEM((tq, lanes), F32)],
        compiler_params=_cparams(("arbitrary", "arbitrary")),
        name="nsa_select_window",
    )(qt, k_tok, k_extra, ch, k_tok, ch, selt, far3, b_sel, b_win, oct_, gates_t)


def _decay_kernel(z_ref, b_ref, place_ref, o_ref, end_ref, carry_ref, *, tb):
    @pl.when(pl.program_id(0) == 0)
    def _():
        carry_ref[...] = jnp.zeros(carry_ref.shape, F32)

    z = z_ref[...] + b_ref[...]
    log_f = jnp.minimum(z, 0.0) - jnp.log1p(jnp.exp(-jnp.abs(z)))
    r = lax.broadcasted_iota(jnp.int32, (tb, tb), 0)
    c = lax.broadcasted_iota(jnp.int32, (tb, tb), 1)
    tri = jnp.where(r >= c, 1.0, 0.0).astype(BF16)
    run = _dot3_rhs(tri, log_f) + carry_ref[...]
    carry_ref[...] = run[tb - 1:tb, :]
    val = -run * LOG2E
    hi, mid, lo = _split3_exact(val)
    o_ref[...] = (_dot(hi, place_ref[0]) + _dot(mid, place_ref[1]) + _dot(lo, place_ref[2])).astype(BF16)
    end_ref[...] = jnp.broadcast_to(val[tb - 1:tb, :], end_ref.shape)


def _decay_pieces(z, bias, first_lane, n_heads):
    s, n = z.shape
    tb = FOX_TK
    place = np.zeros((3, n, n_heads * AUG_K), np.float32)
    for h in range(n_heads):
        for piece in range(3):
            place[piece, first_lane + h, h * AUG_K + HEAD_DIM + piece] = 1.0
    pieces, ends = pl.pallas_call(
        functools.partial(_decay_kernel, tb=tb),
        grid=(s // tb,),
        in_specs=[pl.BlockSpec((tb, n), lambda i: (i, 0)),
                  pl.BlockSpec((1, n), lambda i: (0, 0)),
                  pl.BlockSpec((3, n, n_heads * AUG_K), lambda i: (0, 0, 0))],
        out_specs=[pl.BlockSpec((tb, n_heads * AUG_K), lambda i: (i, 0)),
                   pl.BlockSpec((8, n), lambda i: (i, 0))],
        out_shape=[jax.ShapeDtypeStruct((s, n_heads * AUG_K), BF16),
                   jax.ShapeDtypeStruct((s // tb * 8, n), F32)],
        scratch_shapes=[pltpu.VMEM((1, n), F32)],
        compiler_params=_cparams(("arbitrary",)),
        name="decay_cumsum",
    )(z, bias, jnp.asarray(place, BF16))
    return pieces, ends[::8, first_lane:first_lane + n_heads].T


def _ones_rows(width):
    return jnp.where(lax.broadcasted_iota(jnp.int32, (BF16_ROWS, width), 0) < 8, 1.0, 0.0).astype(BF16)


def _fox_kernel(ends_ref, qt_ref, k_ref, dk_ref, vt_ref, o_ref, rhs, m_ref, acc_ref, s_even, s_odd, kmax_ref):
    tq = FOX_TQ
    tk = FOX_TK
    assert tq == 2 * tk
    qb = pl.program_id(1)
    row = lax.broadcasted_iota(jnp.int32, (AUG_K - HEAD_DIM, tq), 0)
    rhs[0:HEAD_DIM, :] = qt_ref[0]
    rhs[HEAD_DIM:AUG_K, :] = jnp.where(row < 3, 1.0, 0.0).astype(BF16)
    ones_rows = _ones_rows(tk)
    _flash_init_t(m_ref, acc_ref)

    def scores(kt, s_ref):
        k0 = pl.multiple_of(kt * tk, tk)
        k_aug = k_ref[pl.ds(k0, tk), :] + dk_ref[pl.ds(k0, tk), :]
        s_ref[...] = _dot(k_aug, rhs[...])

    def consume(kt, s_ref, diagonal):
        k0 = pl.multiple_of(kt * tk, tk)
        s = s_ref[...]
        if diagonal:
            key = k0 + lax.broadcasted_iota(jnp.int32, (tk, tq), 0)
            qry = qb * tq + lax.broadcasted_iota(jnp.int32, (tk, tq), 1)
            s = jnp.where(key <= qry, s, NEG_INF)
        vt_aug = jnp.concatenate([vt_ref[0, :, pl.ds(k0, tk)], ones_rows], axis=0)
        _flash_step_t(s, vt_aug, m_ref, acc_ref)

    @pl.when(qb == 0)
    def _():
        ones = jnp.ones((AUG_K, AUG_K), BF16)

        def norm_tile(c, best):
            k = k_ref[pl.ds(pl.multiple_of(c * tk, tk), tk), :].astype(F32)
            return jnp.maximum(best, _dot((k * k).astype(BF16), ones))

        best = lax.fori_loop(0, k_ref.shape[0] // tk, norm_tile, jnp.zeros((tk, AUG_K), F32))
        kmax_ref[0] = jnp.max(jnp.sqrt(best * FOX_BOUND_SLACK))

    scores(2 * qb, s_even)
    scores(2 * qb + 1, s_odd)
    consume(2 * qb, s_even, True)
    consume(2 * qb + 1, s_odd, True)

    q = qt_ref[0].astype(F32)
    q_norm = jnp.max(jnp.sqrt(jnp.sum(q * q, axis=0, keepdims=True) * FOX_BOUND_SLACK))
    threshold = jnp.min(m_ref[...]) - FOX_SKIP_MARGIN - q_norm * kmax_ref[0]
    head = pl.program_id(0)

    def first_needed(j, first):
        return jnp.where(ends_ref[head, 2 * j + 1] >= threshold, jnp.minimum(first, j), first)

    j0 = lax.fori_loop(0, qb, first_needed, qb)

    scores(2 * j0, s_even)

    def pair_body(j, carry):
        scores(2 * j + 1, s_odd)
        consume(2 * j, s_even, False)
        scores(2 * j + 2, s_even)
        consume(2 * j + 1, s_odd, False)
        return carry

    lax.fori_loop(j0, qb, pair_body, 0)
    o_ref[0] = _flash_result_t(acc_ref).astype(o_ref.dtype)


def _fox_attention(ch, q_block0, v_block0, k_tok, k_block0, decay_k, decay_ends):
    _, hd, s = ch.shape
    h = decay_ends.shape[0]
    tq = FOX_TQ
    grid_spec = pltpu.PrefetchScalarGridSpec(
        num_scalar_prefetch=1,
        grid=(h, s // tq),
        in_specs=[pl.BlockSpec((1, hd, tq), lambda hi, qi, ends: (q_block0 + hi, 0, qi)),
                  pl.BlockSpec((s, AUG_K), lambda hi, qi, ends: (0, k_block0 + hi)),
                  pl.BlockSpec((s, AUG_K), lambda hi, qi, ends: (0, hi)),
                  pl.BlockSpec((1, hd, s), lambda hi, qi, ends: (v_block0 + hi, 0, 0))],
        out_specs=pl.BlockSpec((1, hd, tq), lambda hi, qi, ends: (hi, 0, qi)),
        scratch_shapes=[pltpu.VMEM((AUG_K, tq), BF16),
                        pltpu.VMEM((1, tq), F32), pltpu.VMEM((AUG_V, tq), F32),
                        pltpu.VMEM((FOX_TK, tq), F32), pltpu.VMEM((FOX_TK, tq), F32),
                        pltpu.SMEM((1,), F32)],
    )
    return pl.pallas_call(
        _fox_kernel,
        grid_spec=grid_spec,
        out_shape=jax.ShapeDtypeStruct((h, hd, s), BF16),
        compiler_params=_cparams(("arbitrary", "arbitrary")),
        name="fox_attention",
    )(decay_ends, ch, k_tok, decay_k, ch)


def _merge_kernel(on_ref, of_ref, mg_ref, x_ref, wn_ref, wf_ref, wo_ref, g1_ref, lg_ref, lb_ref,
                  sc2_ref, sh2_ref, wr_ref, br_ref, o_ref, u_ref, r_ref, *, alpha):
    d = x_ref.shape[-1]
    tn = (((0,), (0,)), ((), ()))
    a = lax.dot_general(on_ref[...], wn_ref[...], tn, preferred_element_type=F32)
    b = lax.dot_general(of_ref[...], wf_ref[...], tn, preferred_element_type=F32)
    gm = _sigmoid(mg_ref[...].astype(F32))
    merged = gm[:, 0:d] * a + gm[:, d:2 * d] * b
    y = _dot(merged.astype(BF16), wo_ref[...])
    z = alpha * x_ref[...] + (1.0 + g1_ref[...]) * y
    x1 = _layer_norm(z) * lg_ref[...] + lb_ref[...]
    o_ref[...] = x1
    _route_tile(x1, sc2_ref, sh2_ref, wr_ref, br_ref, u_ref, r_ref)


def _merge_project(o_nsa, o_fox, merge, x, wn, wf, wo, g1, ln_g, ln_b, sc2, sh2, w_r, b_r, alpha, tm=256):
    m, d = x.shape
    w = o_nsa.shape[0]
    resident = pl.Buffered(1)
    row = lambda i: (i, 0)
    fixed = lambda i: (0, 0)
    return pl.pallas_call(
        functools.partial(_merge_kernel, alpha=alpha),
        grid=(m // tm,),
        in_specs=[pl.BlockSpec((w, tm), lambda i: (0, i)), pl.BlockSpec((w, tm), lambda i: (0, i)),
                  pl.BlockSpec((tm, 2 * d), row), pl.BlockSpec((tm, d), row),
                  pl.BlockSpec((w, d), fixed, pipeline_mode=resident),
                  pl.BlockSpec((w, d), fixed, pipeline_mode=resident),
                  pl.BlockSpec((d, d), fixed, pipeline_mode=resident),
                  pl.BlockSpec((1, d), fixed), pl.BlockSpec((1, d), fixed), pl.BlockSpec((1, d), fixed),
                  pl.BlockSpec((1, d), fixed), pl.BlockSpec((1, d), fixed),
                  pl.BlockSpec((2, d, LANES), lambda i: (0, 0, 0)), pl.BlockSpec((1, LANES), fixed)],
        out_specs=[pl.BlockSpec((tm, d), row), pl.BlockSpec((tm, d), row), pl.BlockSpec((tm, LANES), row)],
        out_shape=[jax.ShapeDtypeStruct((m, d), F32), jax.ShapeDtypeStruct((m, d), F32),
                   jax.ShapeDtypeStruct((m, LANES), F32)],
        compiler_params=_cparams(("parallel",)),
        name="merge_project_ln_route",
    )(o_nsa, o_fox, merge, x, wn, wf, wo, g1, ln_g, ln_b, sc2, sh2, w_r, b_r)


def _route_tile(x1, sc_ref, sh_ref, w_ref, b_ref, u_ref, r_ref):
    u = _layer_norm(x1) * (1.0 + sc_ref[...]) + sh_ref[...]
    u_ref[...] = u
    u_hi = u.astype(BF16)
    u_lo = (u - u_hi.astype(F32)).astype(BF16)
    logits = _dot(u_hi, w_ref[0]) + _dot(u_lo, w_ref[0]) + _dot(u_hi, w_ref[1]) + b_ref[...]
    lane = lax.broadcasted_iota(jnp.int32, (1, LANES), 1).astype(F32)
    none = float(LANES)
    is_g = lane < N_GROUPS
    lg = jnp.where(is_g, logits, NEG_INF)
    eg = jnp.exp(lg - jnp.max(lg, axis=-1, keepdims=True))
    pg = eg / jnp.sum(eg, axis=-1, keepdims=True)
    p_grp = jnp.max(pg, axis=-1, keepdims=True)
    grp = jnp.min(jnp.where(pg == p_grp, lane, none), axis=-1, keepdims=True)
    lo = N_GROUPS + grp * EXPERTS_PER_GROUP
    is_e = (lane >= lo) & (lane < lo + EXPERTS_PER_GROUP)
    le = jnp.where(is_e, logits, NEG_INF)
    ee = jnp.exp(le - jnp.max(le, axis=-1, keepdims=True))
    pe = jnp.where(is_e, ee / jnp.sum(ee, axis=-1, keepdims=True), -1.0)
    p1 = jnp.max(pe, axis=-1, keepdims=True)
    i1 = jnp.min(jnp.where(pe == p1, lane, none), axis=-1, keepdims=True)
    pe2 = jnp.where(lane == i1, -1.0, pe)
    p2 = jnp.max(pe2, axis=-1, keepdims=True)
    i2 = jnp.min(jnp.where(pe2 == p2, lane, none), axis=-1, keepdims=True)
    den = p1 + p2
    r_ref[...] = jnp.where(lane == 0, i1 - N_GROUPS,
                           jnp.where(lane == 1, i2 - N_GROUPS,
                                     jnp.where(lane == 2, p_grp * p1 / den,
                                               jnp.where(lane == 3, p_grp * p2 / den, 0.0))))


def _moe_kernel(be_ref, nu_ref, tok_ref, tok_next_ref, dst_ref, rw_ref, u_hbm, wg_ref, wu_ref, wd_ref, out_hbm,
                xbuf, ybuf, wgb, wub, wdb, sem_in, sem_out, *, n_dump0):
    rb = ROW_BLOCK
    i = pl.program_id(0)
    last = nu_ref[0] - 1
    slot = i % 2

    def row_in(r, tok, sl):
        return pltpu.make_async_copy(u_hbm.at[pl.ds(tok, 1), :], xbuf.at[sl, pl.ds(r, 1), :], sem_in.at[sl])

    def row_out(r, dst):
        return pltpu.make_async_copy(ybuf.at[pl.ds(r, 1), :], out_hbm.at[pl.ds(dst, 1), :], sem_out)

    @pl.when(i == 0)
    def _():
        for r in range(rb):
            row_in(r, tok_ref[0, 0, r], 0).start()
        ybuf[...] = jnp.zeros(ybuf.shape, F32)
        pltpu.make_async_copy(ybuf, out_hbm.at[pl.ds(n_dump0, rb), :], sem_out).start()

    prev = be_ref[jnp.maximum(i - 1, 0)]

    @pl.when((i <= last) & ((i == 0) | (be_ref[i] != prev)))
    def _():
        wgb[...] = wg_ref[0].astype(BF16)
        wub[...] = wu_ref[0].astype(BF16)
        wdb[...] = wd_ref[0].astype(BF16)

    @pl.when(i <= last)
    def _():
        for r in range(rb):
            row_in(r, 0, slot).wait()
        xb = xbuf[slot].astype(BF16)
        for r in range(rb):
            row_in(r, tok_next_ref[0, 0, r], 1 - slot).start()
        gate = _dot(xb, wgb[...])
        up = _dot(xb, wub[...])
        hid = (gate * _sigmoid(gate)) * up
        y = _dot(hid.astype(BF16), wdb[...]) * rw_ref[0]
        for r in range(rb):
            row_out(r, 0).wait()
        ybuf[...] = y
        for r in range(rb):
            row_out(r, dst_ref[0, 0, r]).start(priority=r % 2)

    @pl.when(i == last)
    def _():
        for r in range(rb):
            row_out(r, 0).wait()
        for r in range(rb):
            row_in(r, 0, 1 - slot).wait()


def _moe_experts(u, blk_exp, n_used, row_tok, row_dst, row_w, w_gate, w_up, w_down):
    t, d = u.shape
    n_blocks = blk_exp.shape[0]
    de = w_gate.shape[-1]
    rb = ROW_BLOCK
    tok3 = row_tok.reshape(n_blocks, 1, rb)
    grid_spec = pltpu.PrefetchScalarGridSpec(
        num_scalar_prefetch=2,
        grid=(n_blocks,),
        in_specs=[pl.BlockSpec((1, 1, rb), lambda i, be, nu: (i, 0, 0), memory_space=pltpu.SMEM),
                  pl.BlockSpec((1, 1, rb), lambda i, be, nu: (jnp.minimum(i + 1, n_blocks - 1), 0, 0),
                               memory_space=pltpu.SMEM),
                  pl.BlockSpec((1, 1, rb), lambda i, be, nu: (i, 0, 0), memory_space=pltpu.SMEM),
                  pl.BlockSpec((1, rb, 1), lambda i, be, nu: (i, 0, 0)),
                  pl.BlockSpec(memory_space=pl.ANY),
                  pl.BlockSpec((1, d, de), lambda i, be, nu: (be[i], 0, 0)),
                  pl.BlockSpec((1, d, de), lambda i, be, nu: (be[i], 0, 0)),
                  pl.BlockSpec((1, de, d), lambda i, be, nu: (be[i], 0, 0))],
        out_specs=pl.BlockSpec(memory_space=pl.ANY),
        scratch_shapes=[pltpu.VMEM((2, rb, d), F32), pltpu.VMEM((rb, d), F32),
                        pltpu.VMEM((d, de), BF16), pltpu.VMEM((d, de), BF16), pltpu.VMEM((de, d), BF16),
                        pltpu.SemaphoreType.DMA((2,)), pltpu.SemaphoreType.DMA(())],
    )
    return pl.pallas_call(
        functools.partial(_moe_kernel, n_dump0=2 * t),
        grid_spec=grid_spec,
        out_shape=jax.ShapeDtypeStruct((2 * t + rb, d), F32),
        compiler_params=_cparams(("arbitrary",)),
        name="moe_experts",
    )(blk_exp, n_used, tok3, tok3, row_dst.reshape(n_blocks, 1, rb), row_w.reshape(n_blocks, rb, 1),
      u, w_gate, w_up, w_down)


def _moe_dispatch(route, t):
    k = 2
    eid = route[:, 0:k].astype(jnp.int32).reshape(-1)
    wts = route[:, k:2 * k].reshape(-1)
    n_asg = t * k
    n_rows = n_asg + N_EXPERTS * ROW_BLOCK
    n_blocks = n_rows // ROW_BLOCK
    onehot = (eid[:, None] == jnp.arange(N_EXPERTS, dtype=jnp.int32)[None, :]).astype(jnp.int32)
    rank = jnp.sum((jnp.cumsum(onehot, axis=0) - onehot) * onehot, axis=1)
    counts = jnp.sum(onehot, axis=0)
    padded = (counts + ROW_BLOCK - 1) // ROW_BLOCK * ROW_BLOCK
    pad_end = jnp.cumsum(padded)
    pad_start = pad_end - padded
    dest = jnp.sum(onehot * pad_start[None, :], axis=1) + rank
    asg = jnp.arange(n_asg, dtype=jnp.int32)
    upd = jnp.stack([(asg % k) * t + asg // k, lax.bitcast_convert_type(wts, jnp.int32)], axis=1)
    init = jnp.stack([n_asg + jnp.arange(n_rows, dtype=jnp.int32) % ROW_BLOCK,
                      jnp.zeros((n_rows,), jnp.int32)], axis=1)
    rows = init.at[dest].set(upd)
    row_dst = rows[:, 0]
    row_w = lax.bitcast_convert_type(rows[:, 1], F32)
    row_tok = jnp.where(row_dst < n_asg, row_dst % t, 0)
    blk_start = jnp.arange(n_blocks, dtype=jnp.int32) * ROW_BLOCK
    blk_exp = jnp.minimum(jnp.sum((pad_end[None, :] <= blk_start[:, None]).astype(jnp.int32), axis=1),
                          N_EXPERTS - 1)
    n_used = (pad_end[N_EXPERTS - 1:] // ROW_BLOCK).astype(jnp.int32)
    return blk_exp, n_used, row_tok, row_dst, row_w


def _final_kernel(x_ref, y0_ref, y1_ref, g2_ref, lg_ref, lb_ref, o_ref, *, alpha):
    z = alpha * x_ref[...] + (1.0 + g2_ref[...]) * (y0_ref[...] + y1_ref[...])
    o_ref[...] = _layer_norm(z) * lg_ref[...] + lb_ref[...]


def _final_ln(x1, y2, g2, ln_g, ln_b, alpha, tm=512):
    m, d = x1.shape
    nb = m // tm
    fixed = lambda i: (0, 0)
    return pl.pallas_call(
        functools.partial(_final_kernel, alpha=alpha),
        grid=(nb,),
        in_specs=[pl.BlockSpec((tm, d), lambda i: (i, 0)),
                  pl.BlockSpec((tm, d), lambda i: (i, 0)),
                  pl.BlockSpec((tm, d), lambda i: (i + nb, 0)),
                  pl.BlockSpec((1, d), fixed), pl.BlockSpec((1, d), fixed), pl.BlockSpec((1, d), fixed)],
        out_specs=pl.BlockSpec((tm, d), lambda i: (i, 0)),
        out_shape=jax.ShapeDtypeStruct((m, d), F32),
        compiler_params=_cparams(("parallel",)),
        name="final_ln",
    )(x1, y2, y2, g2, ln_g, ln_b)


def _to_lane_blocks(a, tq):
    g, hpg, s, c = a.shape
    return a.reshape(g, hpg, s // tq, tq, c).transpose(0, 2, 4, 1, 3).reshape(g, s // tq, c, hpg * tq)


def _layer(x2d, c, w_ada, b_ada, w_in, b_fgt, t5_table, cmp_pe, cmp_w1, cmp_b1, cmp_w2,
           w_br_nsa, w_br_fox, w_o, ln1_g, ln1_b, w_rg, b_rg, w_re, b_re,
           w_gate, w_up, w_down, ln2_g, ln2_b, alpha):
    s, d = x2d.shape
    hd = HEAD_DIM
    g = NSA_GROUPS
    mod = _ada_mod(c, w_ada, b_ada)
    sh1, sc1, g1, sh2, sc2, g2 = [mod[:, i * d:(i + 1) * d] for i in range(6)]

    c_q = NSA_HEADS * hd
    c_kv = 6 * g * hd
    c_gate = 3 * NSA_HEADS
    c_fox = 3 * FOX_HEADS * hd
    off_kv = c_q
    off_gate = off_kv + c_kv
    off_fox = off_gate + c_gate
    off_fgt = off_fox + c_fox
    off_merge = off_fgt + FOX_HEADS
    qscale = hd ** -0.5 * LOG2E
    nh = FOX_HEADS
    gw = g * hd

    def kv_cols(z):
        return w_in[:, off_kv + z * gw:off_kv + (z + 1) * gw]

    def lane_padded(w, heads):
        return jnp.pad(w.reshape(d, heads, hd), ((0, 0), (0, 0), (0, AUG_K - hd))).reshape(d, heads * AUG_K)

    fox_q, fox_k, fox_v = [w_in[:, off_fox + i * nh * hd:off_fox + (i + 1) * nh * hd] for i in range(3)]
    w_ch = jnp.concatenate([w_in[:, 0:off_kv] * qscale, kv_cols(3), kv_cols(5), fox_q * qscale, fox_v],
                           axis=1).T.astype(BF16)
    w_tok = jnp.concatenate([kv_cols(0), kv_cols(1), lane_padded(kv_cols(2), g), lane_padded(kv_cols(4), g),
                             lane_padded(fox_k, nh)], axis=1).astype(BF16)
    n_small = c_gate + FOX_HEADS
    w_small = jnp.concatenate([w_in[:, off_gate:off_fox], w_in[:, off_fgt:off_merge],
                               jnp.zeros((d, LANES - n_small), F32)], axis=1).astype(BF16)
    w_merge = w_in[:, off_merge:].astype(BF16)

    u = _ln_mod(x2d, sc1, sh1, BF16, tm=PROJ_TM)
    ch = _matmul_nt(w_ch, u, BF16, PROJ_TM, "in_proj_channel_major")
    tok = _matmul(u, w_tok, BF16, PROJ_TM, w_tok.shape[1], "in_proj_token_major")
    small = _matmul(u, w_small, F32, PROJ_TM, LANES, "in_proj_small")
    merge = _matmul(u, w_merge, BF16, PROJ_TM, 2048, "in_proj_merge")

    qt_nsa = ch[0:c_q].reshape(g, NSA_HPG, hd, s)
    ch = ch.reshape(-1, hd, s)
    ch_vs_block0 = NSA_HEADS
    ch_vw_block0 = ch_vs_block0 + g
    ch_fq_block0 = ch_vw_block0 + g
    ch_fv_block0 = ch_fq_block0 + nh
    tok_ks_block0 = 2 * gw // AUG_K
    tok_kw_block0 = tok_ks_block0 + g
    tok_fox_block0 = tok_kw_block0 + g

    kv_cmp_in = tok[:, 0:2 * gw].reshape(s, 2, g, hd).transpose(1, 2, 0, 3)
    kv_cmp = _compress(kv_cmp_in, cmp_pe, cmp_w1, cmp_b1, cmp_w2)
    nch = s // CMP_STRIDE
    ncp = nch + LANES
    nslc = s // SLC_LEN
    kv_cmp_pad = jnp.pad(kv_cmp.astype(BF16), ((0, 0), (0, 0), (CMP_PAD, ncp - nch - CMP_PAD), (0, 0)))
    b_cmp, b_sel, b_win, far3 = _nsa_bias_tables(t5_table)

    row_ix = np.arange(ncp)
    row_ok = (row_ix >= CMP_PAD) & (row_ix < nch - 1 + CMP_PAD)
    row_cols = np.zeros((ncp, 2 * AUG_K - hd), np.float32)
    row_cols[:, ROW_BIAS - hd:ROW_BIAS - hd + 3] = row_ok[:, None]
    row_cols[:, ROW_BIAS - hd + 3] = ~row_ok
    row_cols[row_ix, AUG_K - hd + row_ix // BF16_ROWS] = 1.0
    kc_aug = jnp.concatenate([kv_cmp_pad[0], jnp.broadcast_to(jnp.asarray(row_cols, BF16), (g,) + row_cols.shape)],
                             axis=-1)
    vct = kv_cmp_pad[1].transpose(0, 2, 1)
    nq = s // ATT_TQ
    qt = qt_nsa.reshape(g, NSA_HPG, hd, nq, ATT_TQ).transpose(0, 3, 2, 1, 4).reshape(g, nq, hd, NSA_HPG * ATT_TQ)
    oct_, selt = _nsa_compress_select(qt, kc_aug, vct, b_cmp, far3)

    period = BF16_ROWS * SLC_LEN
    k_extra = np.zeros((period, AUG_K), np.float32)
    k_extra[np.arange(period), ROW_MASK + np.arange(period) // SLC_LEN] = 1.0
    k_extra[:, ROW_BIAS:ROW_BIAS + 3] = 1.0
    gates = small[:, 0:c_gate].reshape(s, g, NSA_HPG, 3).transpose(1, 2, 0, 3)
    gates_t = jnp.pad(_to_lane_blocks(gates, ATT_TQ), ((0, 0), (0, 0), (0, 5), (0, 0)))
    o_nsa_t = _nsa_attention(qt, tok, tok_ks_block0, tok_kw_block0, jnp.asarray(k_extra, BF16), ch, ch_vs_block0,
                             ch_vw_block0, selt, far3, b_sel, b_win, oct_, gates_t)
    o_nsa = o_nsa_t.reshape(MIX_W, s)

    fgt_bias = jnp.concatenate([jnp.zeros((c_gate,), F32), b_fgt, jnp.zeros((LANES - n_small,), F32)])[None, :]
    decay_k, decay_ends = _decay_pieces(small, fgt_bias, c_gate, nh)
    o_fox = _fox_attention(ch, ch_fq_block0, ch_fv_block0, tok, tok_fox_block0, decay_k, decay_ends)
    o_fox = o_fox.reshape(MIX_W, s)

    n_r = N_GROUPS + N_EXPERTS
    w_r = jnp.concatenate([w_rg, w_re.reshape(d, N_EXPERTS), jnp.zeros((d, LANES - n_r), F32)], axis=1)
    b_r = jnp.concatenate([b_rg, b_re.reshape(N_EXPERTS), jnp.zeros((LANES - n_r,), F32)])[None, :]
    w_r_hi, w_r_lo, _ = _split3_exact(w_r)
    x1, u2, route = _merge_project(o_nsa, o_fox, merge, x2d, w_br_nsa.astype(BF16), w_br_fox.astype(BF16),
                                   w_o.astype(BF16), g1, ln1_g[None, :], ln1_b[None, :], sc2, sh2,
                                   jnp.stack([w_r_hi, w_r_lo]), b_r, alpha)

    blk_exp, n_used, row_tok, row_dst, row_w = _moe_dispatch(route, s)
    y2 = _moe_experts(u2, blk_exp, n_used, row_tok, row_dst, row_w, w_gate, w_up, w_down)
    return _final_ln(x1, y2, g2, ln2_g[None, :], ln2_b[None, :], alpha)


def kernel(x, c, w_ada, b_ada, w_in, b_fgt, t5_table, cmp_pe, cmp_w1, cmp_b1, cmp_w2, w_br_nsa, w_br_fox, w_o,
           ln1_g, ln1_b, w_rg, b_rg, w_re, b_re, w_gate, w_up, w_down, ln2_g, ln2_b):
    b, s, d = x.shape
    depth = w_ada.shape[0]
    assert b == 1
    alpha = (2 * depth) ** 0.25
    h = x[0]
    for l in range(depth):
        h = _layer(h, c, w_ada[l], b_ada[l], w_in[l], b_fgt[l], t5_table, cmp_pe[l], cmp_w1[l], cmp_b1[l],
                   cmp_w2[l], w_br_nsa[l], w_br_fox[l], w_o[l], ln1_g[l], ln1_b[l], w_rg[l], b_rg[l],
                   w_re[l], b_re[l], w_gate[l], w_up[l], w_down[l], ln2_g[l], ln2_b[l], alpha)
    return h[None]
```

```python
import functools
import math

import numpy as np
import jax
import jax.numpy as jnp
from jax import lax
from jax.experimental import pallas as pl
from jax.experimental.pallas import tpu as pltpu

F32 = jnp.float32
BF16 = jnp.bfloat16
HIGHEST = lax.Precision.HIGHEST
LOG2E = math.log2(math.e)

HEAD_DIM = 64
NSA_HEADS = 8
NSA_GROUPS = 2
NSA_HPG = NSA_HEADS // NSA_GROUPS
FOX_HEADS = 8
MIX_W = NSA_HEADS * HEAD_DIM
CMP_LEN = 32
CMP_STRIDE = 16
SLC_LEN = 64
SLC_TOPK = 16
WINDOW = 512
T5_BUCKETS = 32
T5_MAX_EXACT = 16
T5_MAX_DIST = 128
N_GROUPS = 8
EXPERTS_PER_GROUP = 8
N_EXPERTS = N_GROUPS * EXPERTS_PER_GROUP
ROW_BLOCK = 128
LN_EPS = 1e-5
NEG_INF = -1e30
M_INIT = -1e29
FORCE_SCORE = 1e4

LANES = 128
BF16_ROWS = 16
CMP_PAD = 8
ATT_TQ = 256
CMP_NEAR_ROWS = 32
FOX_TQ = 1024
FOX_TK = 512
FOX_SKIP_MARGIN = 160.0
FOX_BOUND_SLACK = 1.02
PROJ_TM = 1024
AUG_K = 128
AUG_V = HEAD_DIM + 16
ROW_MASK = HEAD_DIM
ROW_BIAS = HEAD_DIM + 16
VMEM_LIMIT = 56 * 1024 * 1024


def _cparams(sem, vmem=VMEM_LIMIT):
    return pltpu.CompilerParams(dimension_semantics=sem, vmem_limit_bytes=vmem)


def _sigmoid(x):
    return 1.0 / (1.0 + jnp.exp(-x))


def _layer_norm(x):
    mu = jnp.mean(x, axis=-1, keepdims=True)
    xc = x - mu
    var = jnp.mean(xc * xc, axis=-1, keepdims=True)
    return xc * lax.rsqrt(var + LN_EPS)


def _split3(x):
    hi = x.astype(BF16)
    r1 = x - hi.astype(F32)
    mid = r1.astype(BF16)
    lo = (r1 - mid.astype(F32)).astype(BF16)
    return hi, mid, lo


def _split3_exact(x):
    def trunc(v):
        bits = lax.bitcast_convert_type(v, jnp.uint32) & jnp.uint32(0xFFFF0000)
        return lax.bitcast_convert_type(bits, F32)
    hi = trunc(x)
    r1 = x - hi
    mid = trunc(r1)
    lo = r1 - mid
    return hi.astype(BF16), mid.astype(BF16), lo.astype(BF16)


def _dot(a, b):
    return jnp.dot(a, b, preferred_element_type=F32)


def _dot_nt(a, b):
    return lax.dot_general(a, b, (((1,), (1,)), ((), ())), preferred_element_type=F32)


def _dot3(x, w_bf16):
    hi, mid, lo = _split3(x)
    return _dot(hi, w_bf16) + _dot(mid, w_bf16) + _dot(lo, w_bf16)


def _dot3_rhs(w_bf16, x):
    hi, mid, lo = _split3(x)
    return _dot(w_bf16, hi) + _dot(w_bf16, mid) + _dot(w_bf16, lo)


def _ada_kernel(c_ref, w_ref, b_ref, o_ref):
    c = c_ref[...]
    a = c * _sigmoid(c)
    o_ref[...] = jnp.dot(a, w_ref[...], precision=HIGHEST, preferred_element_type=F32) + b_ref[...]


def _ada_mod(c, w, b):
    d, n = w.shape
    tn = 1024
    c8 = jnp.broadcast_to(c, (8, d))
    out = pl.pallas_call(
        _ada_kernel,
        grid=(n // tn,),
        in_specs=[pl.BlockSpec((8, d), lambda j: (0, 0)),
                  pl.BlockSpec((d, tn), lambda j: (0, j)),
                  pl.BlockSpec((1, tn), lambda j: (0, j))],
        out_specs=pl.BlockSpec((8, tn), lambda j: (0, j)),
        out_shape=jax.ShapeDtypeStruct((8, n), F32),
        compiler_params=_cparams(("parallel",)),
        name="ada_mod",
    )(c8, w, b.reshape(1, n))
    return out[0:1]


def _lnmod_kernel(x_ref, sc_ref, sh_ref, o_ref):
    y = _layer_norm(x_ref[...])
    o_ref[...] = (y * (1.0 + sc_ref[...]) + sh_ref[...]).astype(o_ref.dtype)


def _ln_mod(x, sc, sh, out_dtype, tm=512):
    m, d = x.shape
    return pl.pallas_call(
        _lnmod_kernel,
        grid=(m // tm,),
        in_specs=[pl.BlockSpec((tm, d), lambda i: (i, 0)),
                  pl.BlockSpec((1, d), lambda i: (0, 0)),
                  pl.BlockSpec((1, d), lambda i: (0, 0))],
        out_specs=pl.BlockSpec((tm, d), lambda i: (i, 0)),
        out_shape=jax.ShapeDtypeStruct((m, d), out_dtype),
        compiler_params=_cparams(("parallel",)),
        name="ln_mod",
    )(x, sc, sh)


def _mm_kernel(a_ref, w_ref, o_ref):
    o_ref[...] = _dot(a_ref[...], w_ref[...]).astype(o_ref.dtype)


def _matmul(a, w, out_dtype, tm, tn, name):
    m, k = a.shape
    n = w.shape[1]
    return pl.pallas_call(
        _mm_kernel,
        grid=(n // tn, m // tm),
        in_specs=[pl.BlockSpec((tm, k), lambda j, i: (i, 0)),
                  pl.BlockSpec((k, tn), lambda j, i: (0, j))],
        out_specs=pl.BlockSpec((tm, tn), lambda j, i: (i, j)),
        out_shape=jax.ShapeDtypeStruct((m, n), out_dtype),
        compiler_params=_cparams(("parallel", "parallel")),
        name=name,
    )(a, w)


def _mm_nt_kernel(w_ref, a_ref, o_ref):
    o_ref[...] = _dot_nt(w_ref[...], a_ref[...]).astype(o_ref.dtype)


def _matmul_nt(w_t, a, out_dtype, tm, name):
    n, k = w_t.shape
    m = a.shape[0]
    return pl.pallas_call(
        _mm_nt_kernel,
        grid=(m // tm,),
        in_specs=[pl.BlockSpec((n, k), lambda i: (0, 0), pipeline_mode=pl.Buffered(1)),
                  pl.BlockSpec((tm, k), lambda i: (i, 0))],
        out_specs=pl.BlockSpec((n, tm), lambda i: (0, i)),
        out_shape=jax.ShapeDtypeStruct((n, m), out_dtype),
        compiler_params=_cparams(("parallel",)),
        name=name,
    )(w_t, a)


def _gelu_tanh(x):
    return 0.5 * x * (1.0 + jnp.tanh(math.sqrt(2.0 / math.pi) * (x + 0.044715 * (x * x * x))))


def _compress_kernel(c_ref, pe_ref, w1a_ref, w1b_ref, b1_ref, w2_ref, o_ref, *, nch):
    c = c_ref[0, 0]
    w1a = w1a_ref[0]
    w1b = w1b_ref[0]
    half = CMP_STRIDE * HEAD_DIM
    a = _dot(c, w1a)
    b = _dot(c, w1b)
    b_next = pltpu.roll(b, shift=nch - 1, axis=0)
    pe = pe_ref[0]
    pb = _dot(pe[:, :half], w1a) + _dot(pe[:, half:], w1b)
    hid = _gelu_tanh(a + b_next + pb[0:1, :] + b1_ref[0])
    o_ref[0, 0] = _dot(hid.astype(BF16), w2_ref[0])


def _compress(kv_cmp, pe, w1, b1, w2):
    z, g, s, hd = kv_cmp.shape
    nch = s // CMP_STRIDE
    half = CMP_STRIDE * hd
    chunks = kv_cmp.reshape(z, g, nch, half)
    pe8 = jnp.broadcast_to(pe.reshape(z, 1, CMP_LEN * hd), (z, 8, CMP_LEN * hd)).astype(BF16)
    w1b16 = w1.astype(BF16)
    hidn = w1.shape[-1]
    return pl.pallas_call(
        functools.partial(_compress_kernel, nch=nch),
        grid=(z, g),
        in_specs=[pl.BlockSpec((1, 1, nch, half), lambda zi, gi: (zi, gi, 0, 0)),
                  pl.BlockSpec((1, 8, 2 * half), lambda zi, gi: (zi, 0, 0)),
                  pl.BlockSpec((1, half, hidn), lambda zi, gi: (zi, 0, 0)),
                  pl.BlockSpec((1, half, hidn), lambda zi, gi: (zi, 1, 0)),
                  pl.BlockSpec((1, 1, hidn), lambda zi, gi: (zi, 0, 0)),
                  pl.BlockSpec((1, hidn, hd), lambda zi, gi: (zi, 0, 0))],
        out_specs=pl.BlockSpec((1, 1, nch, hd), lambda zi, gi: (zi, gi, 0, 0)),
        out_shape=jax.ShapeDtypeStruct((z, g, nch, hd), F32),
        compiler_params=_cparams(("parallel", "parallel")),
        name="compress_kv",
    )(chunks, pe8, w1b16, w1b16, b1.reshape(z, 1, hidn), w2.astype(BF16))


def _t5_bucket_np(dist):
    n = np.maximum(dist, 0)
    ratio = np.log(np.maximum(n, T5_MAX_EXACT).astype(np.float64) / T5_MAX_EXACT)
    big = T5_MAX_EXACT + (ratio / math.log(T5_MAX_DIST / T5_MAX_EXACT)
                          * (T5_BUCKETS - T5_MAX_EXACT)).astype(np.int64)
    return np.where(n < T5_MAX_EXACT, n, np.minimum(big, T5_BUCKETS - 1)).astype(np.int32)


def _t5_lookup(tbh, dist):
    onehot = np.eye(T5_BUCKETS, dtype=np.float32)[_t5_bucket_np(dist).reshape(-1)]
    vals = jnp.einsum('ghb,nb->ghn', tbh, jnp.asarray(onehot), precision=HIGHEST)
    return vals.reshape(tbh.shape[:2] + dist.shape)


def _toeplitz_kernel(w_ref, o_ref, *, n_keys, tq, lead_zero, trail_masked):
    width = w_ref.shape[-1]
    rows = jnp.broadcast_to(w_ref[0, 0], (n_keys, width))
    rolled = pltpu.roll(rows, shift=width - (n_keys - 1), axis=1, stride=1, stride_axis=0)
    parts = [rolled[:, 0:tq]]
    if lead_zero:
        parts.insert(0, jnp.zeros((lead_zero, tq), F32))
    if trail_masked:
        parts.append(jnp.full((trail_masked, tq), NEG_INF, F32))
    o_ref[0] = jnp.concatenate(parts, axis=0)


def _att_table_t(tbh, n_keys, tq, lo, hi, minus_far, lead_zero=0, trail_masked=0):
    length = n_keys + tq - 1
    d = np.arange(length) - (tq - 1)
    valid = (d >= lo) & (d < hi)
    vals = _t5_lookup(tbh, d)
    if minus_far:
        vals = vals - tbh[:, :, T5_BUCKETS - 1:]
    w = jnp.where(jnp.asarray(valid), vals * LOG2E, NEG_INF)
    width = pl.next_power_of_2(length)
    w = jnp.pad(w, ((0, 0), (0, 0), (0, width - length)))[:, :, None, :]
    g, hpg = tbh.shape[:2]
    n_rows = lead_zero + n_keys + trail_masked
    return pl.pallas_call(
        functools.partial(_toeplitz_kernel, n_keys=n_keys, tq=tq, lead_zero=lead_zero, trail_masked=trail_masked),
        grid=(g, hpg),
        in_specs=[pl.BlockSpec((1, 1, 1, width), lambda gi, hi: (gi, hi, 0, 0))],
        out_specs=pl.BlockSpec((1, n_rows, tq), lambda gi, hi: (gi, 0, hi)),
        out_shape=jax.ShapeDtypeStruct((g, n_rows, hpg * tq), F32),
        compiler_params=_cparams(("parallel", "parallel")),
        name="t5_toeplitz",
    )(w)


def _nsa_bias_tables(t5_table):
    tbh = t5_table.T.reshape(NSA_GROUPS, NSA_HPG, T5_BUCKETS).astype(F32)
    j = np.arange(CMP_NEAR_ROWS)[:, None]
    i = np.arange(ATT_TQ)[None, :]
    dist = i - (CMP_LEN - 1) - CMP_STRIDE * (j - CMP_PAD)
    vals = (_t5_lookup(tbh, dist) - tbh[:, :, T5_BUCKETS - 1][:, :, None, None]) * LOG2E
    vals = jnp.where(jnp.asarray(dist >= 0)[None, None], vals, NEG_INF)
    b_cmp = vals.transpose(0, 2, 1, 3).reshape(NSA_GROUPS, CMP_NEAR_ROWS, NSA_HPG * ATT_TQ)
    far = tbh[:, :, T5_BUCKETS - 1] * LOG2E
    b_sel = _att_table_t(tbh, 2 * ATT_TQ, ATT_TQ, 0, 1 << 30, True, lead_zero=ATT_TQ)
    b_win = _att_table_t(tbh, 3 * ATT_TQ, ATT_TQ, 0, WINDOW, False, trail_masked=ATT_TQ)
    hi, mid, lo = _split3_exact(jnp.repeat(far, ATT_TQ, axis=1))
    far3 = jnp.stack([hi, mid, lo, jnp.full(hi.shape, NEG_INF, BF16)], axis=1)
    far3 = jnp.pad(far3, ((0, 0), (0, BF16_ROWS - 4), (0, 0)))
    return b_cmp, b_sel, b_win, far3


def _nsa_cmp_kernel(qt_ref, kc_ref, vct_ref, near_ref, far3_ref, oct_ref, selt_ref, rhs, s_scr, imp_scr, *, nslc):
    tq = ATT_TQ
    lanes = NSA_HPG * tq
    qb = pl.program_id(1)
    cpb = tq // CMP_STRIDE
    rhs[...] = jnp.zeros(rhs.shape, BF16)
    for h in range(NSA_HPG):
        rhs[0:HEAD_DIM, h * tq:(h + 1) * tq] = qt_ref[h]
    rhs[ROW_BIAS:ROW_BIAS + BF16_ROWS, :] = far3_ref[0]
    chunk = lax.broadcasted_iota(jnp.int32, (AUG_K, lanes), 0)
    rhs[AUG_K:2 * AUG_K, :] = jnp.where(chunk >= qb + CMP_NEAR_ROWS // BF16_ROWS, NEG_INF, 0.0).astype(BF16)
    r0 = pl.multiple_of(cpb * qb, BF16_ROWS)
    n_lane_blocks = tq // LANES
    ncp = s_scr.shape[0]
    imp_scr[...] = jnp.zeros(imp_scr.shape, F32)

    def attend(rows):
        s_scr[0:rows, :] = _dot(kc_ref[0, 0:rows, :], rhs[...])
        s_scr[pl.ds(r0, CMP_NEAR_ROWS), :] = s_scr[pl.ds(r0, CMP_NEAR_ROWS), :] + near_ref[0]
        s = s_scr[0:rows, :]
        m = jnp.max(s, axis=0, keepdims=True)
        e = jnp.exp2(s - m)
        l = jnp.sum(e, axis=0, keepdims=True)
        p = e * jnp.where(m > M_INIT, 1.0 / l, 0.0)
        oct_ref[0, 0] = _dot(vct_ref[0, :, 0:rows], p.astype(BF16))
        imp = p[:, 0:tq]
        for h in range(1, NSA_HPG):
            imp = imp + p[:, h * tq:(h + 1) * tq]
        for c in range(n_lane_blocks):
            imp_scr[c, 0:rows, :] = imp[:, c * LANES:(c + 1) * LANES]

    limits = sorted({min(ncp, -(-(ncp * k // 3) // LANES) * LANES) for k in (1, 2, 3)})
    lo_qb = 0
    for rows in limits:
        hi_qb = (rows - CMP_NEAR_ROWS) // cpb if rows < ncp else pl.num_programs(1) - 1

        @pl.when((qb >= lo_qb) & (qb <= hi_qb))
        def _(rows=rows):
            attend(rows)

        lo_qb = hi_qb + 1
    ratio = SLC_LEN // CMP_STRIDE

    def taps(off):
        return jnp.concatenate([imp_scr[c, pl.ds(CMP_PAD + off, nslc, stride=ratio), :]
                                for c in range(n_lane_blocks)], axis=1)

    p_slc = 0.5 * (taps(-1) + taps(ratio - 1))
    for off in range(ratio - 1):
        p_slc = p_slc + taps(off)
    blk = lax.broadcasted_iota(jnp.int32, (nslc, tq), 0)
    cur = (qb * tq + lax.broadcasted_iota(jnp.int32, (nslc, tq), 1)) // SLC_LEN
    forced = (blk == 0) | (blk == cur) | (blk == cur - 1)
    score = jnp.where(forced, FORCE_SCORE, jnp.where(blk <= cur, p_slc, -1.0))
    blk_f = blk.astype(F32)
    sel = jnp.zeros((nslc, tq), F32)
    for _ in range(min(SLC_TOPK, nslc)):
        mx = jnp.max(score, axis=0, keepdims=True)
        first = jnp.min(jnp.where(score == mx, blk_f, float(nslc)), axis=0, keepdims=True)
        hit = blk_f == first
        sel = jnp.where(hit, 1.0, sel)
        score = jnp.where(hit, -2.0, score)
    selt_ref[0] = sel.astype(BF16)


def _nsa_compress_select(ch, kc_aug, vct, b_cmp, far3):
    g = kc_aug.shape[0]
    _, hd, s = ch.shape
    ncp = kc_aug.shape[1]
    tq = ATT_TQ
    nq = s // tq
    lanes = NSA_HPG * tq
    nslc = nq * tq // SLC_LEN
    assert ncp // BF16_ROWS <= AUG_K
    assert tq // CMP_STRIDE == BF16_ROWS
    per_q = lambda gi, qi: (gi, qi, 0, 0)
    per_g = lambda gi, qi: (gi, 0, 0)
    return pl.pallas_call(
        functools.partial(_nsa_cmp_kernel, nslc=nslc),
        grid=(g, nq),
        in_specs=[pl.BlockSpec((NSA_HPG, hd, tq), lambda gi, qi: (gi, 0, qi)),
                  pl.BlockSpec((1, ncp, 2 * AUG_K), per_g),
                  pl.BlockSpec((1, hd, ncp), per_g),
                  pl.BlockSpec((1, CMP_NEAR_ROWS, lanes), per_g),
                  pl.BlockSpec((1, BF16_ROWS, lanes), per_g)],
        out_specs=[pl.BlockSpec((1, 1, hd, lanes), per_q),
                   pl.BlockSpec((1, nslc, tq), lambda gi, qi: (gi, 0, qi))],
        out_shape=[jax.ShapeDtypeStruct((g, nq, hd, lanes), F32),
                   jax.ShapeDtypeStruct((g, nslc, nq * tq), BF16)],
        scratch_shapes=[pltpu.VMEM((2 * AUG_K, lanes), BF16), pltpu.VMEM((ncp, lanes), F32),
                        pltpu.VMEM((tq // LANES, ncp, LANES), F32)],
        compiler_params=_cparams(("parallel", "parallel")),
        name="nsa_compress_select",
    )(ch, kc_aug, vct, b_cmp, far3)


def _flash_init_t(m_ref, acc_ref):
    m_ref[...] = jnp.full(m_ref.shape, M_INIT, F32)
    acc_ref[...] = jnp.zeros(acc_ref.shape, F32)


def _flash_step_t(s, vt_tile, m_ref, acc_ref):
    m_old = m_ref[...]
    m_new = jnp.maximum(m_old, jnp.max(s, axis=0, keepdims=True))
    p = jnp.exp2(s - m_new).astype(BF16)
    acc_ref[...] = jnp.exp2(m_old - m_new) * acc_ref[...] + _dot(vt_tile, p)
    m_ref[...] = m_new


def _flash_result_t(acc_ref):
    acc = acc_ref[...]
    return acc[0:HEAD_DIM, :] / acc[HEAD_DIM:HEAD_DIM + 1, :]


def _nsa_att_kernel(qt_ref, ks_ref, kx_ref, vst_ref, kw_ref, vwt_ref, selt_ref, far3_ref, bsel_ref, bwin_ref,
                    oct_ref, gate_ref, o_ref, rhs_s, rhs_w, mask_t, ms, accs, mw, accw, s_even, s_odd):
    tq = ATT_TQ
    qb = pl.program_id(1)
    ones_rows = _ones_rows(tq)
    kx_tiles = kx_ref.shape[0] // tq
    qt = jnp.concatenate([qt_ref[h] for h in range(NSA_HPG)], axis=1)
    rhs_s[...] = jnp.zeros(rhs_s.shape, BF16)
    rhs_s[0:HEAD_DIM, :] = qt
    rhs_s[ROW_BIAS:ROW_BIAS + BF16_ROWS, :] = far3_ref[0]
    rhs_w[...] = jnp.zeros(rhs_w.shape, BF16)
    rhs_w[0:HEAD_DIM, :] = qt
    madd = ((selt_ref[0].astype(F32) - 1.0) * (-NEG_INF)).astype(BF16)
    mask_t[...] = jnp.concatenate([madd] * NSA_HPG, axis=1)
    _flash_init_t(ms, accs)
    _flash_init_t(mw, accw)
    blocks_per_tile = tq // SLC_LEN

    def sel_scores(kt, s_ref):
        kt = jnp.minimum(kt, qb)
        k0 = pl.multiple_of(kt * tq, tq)
        chunk = pl.multiple_of((kt * blocks_per_tile) // BF16_ROWS * BF16_ROWS, BF16_ROWS)
        rhs_s[ROW_MASK:ROW_MASK + BF16_ROWS, :] = mask_t[pl.ds(chunk, BF16_ROWS), :]
        rel = jnp.clip(kt - qb + 2, 0, 2)
        table = bsel_ref[0, pl.ds(pl.multiple_of(rel * tq, tq), tq), :]
        k_aug = ks_ref[pl.ds(k0, tq), :] + kx_ref[pl.ds(pl.multiple_of((kt % kx_tiles) * tq, tq), tq), :]
        s_ref[...] = _dot(k_aug, rhs_s[...]) + table

    def sel_consume(kt, s_ref):
        k0 = pl.multiple_of(kt * tq, tq)
        vt_aug = jnp.concatenate([vst_ref[0, :, pl.ds(k0, tq)], ones_rows], axis=0)
        _flash_step_t(s_ref[...], vt_aug, ms, accs)

    def win_scores(j, s_ref):
        kt = qb - 2 + j
        k0 = pl.multiple_of(jnp.maximum(kt, 0) * tq, tq)
        row = pl.multiple_of(jnp.where(kt >= 0, j, 3) * tq, tq)
        s_ref[...] = _dot(kw_ref[pl.ds(k0, tq), :], rhs_w[...]) + bwin_ref[0, pl.ds(row, tq), :]

    def win_consume(j, s_ref):
        k0 = pl.multiple_of(jnp.maximum(qb - 2 + j, 0) * tq, tq)
        vt_aug = jnp.concatenate([vwt_ref[0, :, pl.ds(k0, tq)], ones_rows], axis=0)
        _flash_step_t(s_ref[...], vt_aug, mw, accw)

    n_sel = qb + 1
    sel_scores(0, s_even)

    def pair_body(j, carry):
        sel_scores(2 * j + 1, s_odd)
        sel_consume(2 * j, s_even)
        sel_scores(2 * j + 2, s_even)
        sel_consume(2 * j + 1, s_odd)
        return carry

    lax.fori_loop(0, n_sel // 2, pair_body, 0)
    win_scores(0, s_odd)

    @pl.when(n_sel % 2 == 1)
    def _():
        sel_consume(qb, s_even)

    win_scores(1, s_even)
    win_consume(0, s_odd)
    win_scores(2, s_odd)
    win_consume(1, s_even)
    win_consume(2, s_odd)

    gt = _sigmoid(gate_ref[0, 0])
    out = gt[0:1, :] * oct_ref[0, 0] + gt[1:2, :] * _flash_result_t(accs) + gt[2:3, :] * _flash_result_t(accw)
    for h in range(NSA_HPG):
        o_ref[0, h] = out[:, h * tq:(h + 1) * tq].astype(o_ref.dtype)


def _nsa_attention(k_tok, ks_block0, kw_block0, k_extra, ch, vs_block0, vw_block0, selt, far3, b_sel, b_win,
                   oct_, gates_t):
    g, nq, hd, lanes = oct_.shape
    s = k_tok.shape[0]
    nslc = selt.shape[1]
    tq = ATT_TQ
    resident = pl.Buffered(1)
    per_q = lambda gi, qi: (gi, qi, 0, 0)
    per_g = lambda gi, qi: (gi, 0, 0)
    return pl.pallas_call(
        _nsa_att_kernel,
        grid=(g, nq),
        in_specs=[pl.BlockSpec((NSA_HPG, hd, tq), lambda gi, qi: (gi, 0, qi)),
                  pl.BlockSpec((s, AUG_K), lambda gi, qi: (0, ks_block0 + gi), pipeline_mode=resident),
                  pl.BlockSpec(k_extra.shape, lambda gi, qi: (0, 0), pipeline_mode=resident),
                  pl.BlockSpec((1, hd, s), lambda gi, qi: (vs_block0 + gi, 0, 0), pipeline_mode=resident),
                  pl.BlockSpec((s, AUG_K), lambda gi, qi: (0, kw_block0 + gi), pipeline_mode=resident),
                  pl.BlockSpec((1, hd, s), lambda gi, qi: (vw_block0 + gi, 0, 0), pipeline_mode=resident),
                  pl.BlockSpec((1, nslc, tq), lambda gi, qi: (gi, 0, qi)),
                  pl.BlockSpec((1, BF16_ROWS, lanes), per_g),
                  pl.BlockSpec((1, 3 * tq, lanes), per_g, pipeline_mode=resident),
                  pl.BlockSpec((1, 4 * tq, lanes), per_g, pipeline_mode=resident),
                  pl.BlockSpec((1, 1, hd, lanes), per_q),
                  pl.BlockSpec((1, 1, 8, lanes), per_q)],
        out_specs=pl.BlockSpec((1, lanes // tq, hd, tq), lambda gi, qi: (gi, 0, 0, qi)),
        out_shape=jax.ShapeDtypeStruct((g, lanes // tq, hd, s), BF16),
        scratch_shapes=[pltpu.VMEM((AUG_K, lanes), BF16), pltpu.VMEM((AUG_K, lanes), BF16),
                        pltpu.VMEM((nslc, lanes), BF16),
                        pltpu.VMEM((1, lanes), F32), pltpu.VMEM((AUG_V, lanes), F32),
                        pltpu.VMEM((1, lanes), F32), pltpu.VMEM((AUG_V, lanes), F32),
                        pltpu.VMEM((tq, lanes), F32), pltpu.VMEM((tq, lanes), F32)],
        compiler_params=_cparams(("arbitrary", "arbitrary")),
        name="nsa_select_window",
    )(ch, k_tok, k_extra, ch, k_tok, ch, selt, far3, b_sel, b_win, oct_, gates_t)


def _decay_kernel(z_ref, b_ref, place_ref, o_ref, end_ref, carry_ref, *, tb):
    @pl.when(pl.program_id(0) == 0)
    def _():
        carry_ref[...] = jnp.zeros(carry_ref.shape, F32)

    z = z_ref[...] + b_ref[...]
    log_f = jnp.minimum(z, 0.0) - jnp.log1p(jnp.exp(-jnp.abs(z)))
    r = lax.broadcasted_iota(jnp.int32, (tb, tb), 0)
    c = lax.broadcasted_iota(jnp.int32, (tb, tb), 1)
    tri = jnp.where(r >= c, 1.0, 0.0).astype(BF16)
    run = _dot3_rhs(tri, log_f) + carry_ref[...]
    carry_ref[...] = run[tb - 1:tb, :]
    val = -run * LOG2E
    hi, mid, lo = _split3_exact(val)
    o_ref[...] = (_dot(hi, place_ref[0]) + _dot(mid, place_ref[1]) + _dot(lo, place_ref[2])).astype(BF16)
    end_ref[...] = jnp.broadcast_to(val[tb - 1:tb, :], end_ref.shape)


def _decay_pieces(z, bias, first_lane, n_heads):
    s, n = z.shape
    tb = FOX_TK
    place = np.zeros((3, n, n_heads * AUG_K), np.float32)
    for h in range(n_heads):
        for piece in range(3):
            place[piece, first_lane + h, h * AUG_K + HEAD_DIM + piece] = 1.0
    pieces, ends = pl.pallas_call(
        functools.partial(_decay_kernel, tb=tb),
        grid=(s // tb,),
        in_specs=[pl.BlockSpec((tb, n), lambda i: (i, 0)),
                  pl.BlockSpec((1, n), lambda i: (0, 0)),
                  pl.BlockSpec((3, n, n_heads * AUG_K), lambda i: (0, 0, 0))],
        out_specs=[pl.BlockSpec((tb, n_heads * AUG_K), lambda i: (i, 0)),
                   pl.BlockSpec((8, n), lambda i: (i, 0))],
        out_shape=[jax.ShapeDtypeStruct((s, n_heads * AUG_K), BF16),
                   jax.ShapeDtypeStruct((s // tb * 8, n), F32)],
        scratch_shapes=[pltpu.VMEM((1, n), F32)],
        compiler_params=_cparams(("arbitrary",)),
        name="decay_cumsum",
    )(z, bias, jnp.asarray(place, BF16))
    return pieces, ends[::8, first_lane:first_lane + n_heads].T


def _ones_rows(width):
    return jnp.where(lax.broadcasted_iota(jnp.int32, (BF16_ROWS, width), 0) < 8, 1.0, 0.0).astype(BF16)


def _fox_kernel(ends_ref, qt_ref, k_ref, dk_ref, vt_ref, o_ref, rhs, m_ref, acc_ref, s_even, s_odd, kmax_ref):
    tq = FOX_TQ
    tk = FOX_TK
    assert tq == 2 * tk
    qb = pl.program_id(1)
    row = lax.broadcasted_iota(jnp.int32, (AUG_K - HEAD_DIM, tq), 0)
    rhs[0:HEAD_DIM, :] = qt_ref[0]
    rhs[HEAD_DIM:AUG_K, :] = jnp.where(row < 3, 1.0, 0.0).astype(BF16)
    ones_rows = _ones_rows(tk)
    _flash_init_t(m_ref, acc_ref)

    def scores(kt, s_ref):
        k0 = pl.multiple_of(kt * tk, tk)
        k_aug = k_ref[pl.ds(k0, tk), :] + dk_ref[pl.ds(k0, tk), :]
        s_ref[...] = _dot(k_aug, rhs[...])

    def consume(kt, s_ref, diagonal):
        k0 = pl.multiple_of(kt * tk, tk)
        s = s_ref[...]
        if diagonal:
            key = k0 + lax.broadcasted_iota(jnp.int32, (tk, tq), 0)
            qry = qb * tq + lax.broadcasted_iota(jnp.int32, (tk, tq), 1)
            s = jnp.where(key <= qry, s, NEG_INF)
        vt_aug = jnp.concatenate([vt_ref[0, :, pl.ds(k0, tk)], ones_rows], axis=0)
        _flash_step_t(s, vt_aug, m_ref, acc_ref)

    @pl.when(qb == 0)
    def _():
        ones = jnp.ones((AUG_K, AUG_K), BF16)

        def norm_tile(c, best):
            k = k_ref[pl.ds(pl.multiple_of(c * tk, tk), tk), :].astype(F32)
            return jnp.maximum(best, _dot((k * k).astype(BF16), ones))

        best = lax.fori_loop(0, k_ref.shape[0] // tk, norm_tile, jnp.zeros((tk, AUG_K), F32))
        kmax_ref[0] = jnp.max(jnp.sqrt(best * FOX_BOUND_SLACK))

    scores(2 * qb, s_even)
    scores(2 * qb + 1, s_odd)
    consume(2 * qb, s_even, True)
    consume(2 * qb + 1, s_odd, True)

    q = qt_ref[0].astype(F32)
    q_norm = jnp.max(jnp.sqrt(jnp.sum(q * q, axis=0, keepdims=True) * FOX_BOUND_SLACK))
    threshold = jnp.min(m_ref[...]) - FOX_SKIP_MARGIN - q_norm * kmax_ref[0]
    head = pl.program_id(0)

    def first_needed(j, first):
        return jnp.where(ends_ref[head, 2 * j + 1] >= threshold, jnp.minimum(first, j), first)

    j0 = lax.fori_loop(0, qb, first_needed, qb)

    scores(2 * j0, s_even)

    def pair_body(j, carry):
        scores(2 * j + 1, s_odd)
        consume(2 * j, s_even, False)
        scores(2 * j + 2, s_even)
        consume(2 * j + 1, s_odd, False)
        return carry

    lax.fori_loop(j0, qb, pair_body, 0)
    o_ref[0] = _flash_result_t(acc_ref).astype(o_ref.dtype)


def _fox_attention(ch, q_block0, v_block0, k_tok, k_block0, decay_k, decay_ends):
    _, hd, s = ch.shape
    h = decay_ends.shape[0]
    tq = FOX_TQ
    grid_spec = pltpu.PrefetchScalarGridSpec(
        num_scalar_prefetch=1,
        grid=(h, s // tq),
        in_specs=[pl.BlockSpec((1, hd, tq), lambda hi, qi, ends: (q_block0 + hi, 0, qi)),
                  pl.BlockSpec((s, AUG_K), lambda hi, qi, ends: (0, k_block0 + hi)),
                  pl.BlockSpec((s, AUG_K), lambda hi, qi, ends: (0, hi)),
                  pl.BlockSpec((1, hd, s), lambda hi, qi, ends: (v_block0 + hi, 0, 0))],
        out_specs=pl.BlockSpec((1, hd, tq), lambda hi, qi, ends: (hi, 0, qi)),
        scratch_shapes=[pltpu.VMEM((AUG_K, tq), BF16),
                        pltpu.VMEM((1, tq), F32), pltpu.VMEM((AUG_V, tq), F32),
                        pltpu.VMEM((FOX_TK, tq), F32), pltpu.VMEM((FOX_TK, tq), F32),
                        pltpu.SMEM((1,), F32)],
    )
    return pl.pallas_call(
        _fox_kernel,
        grid_spec=grid_spec,
        out_shape=jax.ShapeDtypeStruct((h, hd, s), BF16),
        compiler_params=_cparams(("arbitrary", "arbitrary")),
        name="fox_attention",
    )(decay_ends, ch, k_tok, decay_k, ch)


def _merge_kernel(on_ref, of_ref, mg_ref, x_ref, wn_ref, wf_ref, wo_ref, g1_ref, lg_ref, lb_ref,
                  sc2_ref, sh2_ref, wr_ref, br_ref, o_ref, u_ref, r_ref, *, alpha):
    d = x_ref.shape[-1]
    tn = (((0,), (0,)), ((), ()))
    a = lax.dot_general(on_ref[...], wn_ref[...], tn, preferred_element_type=F32)
    b = lax.dot_general(of_ref[...], wf_ref[...], tn, preferred_element_type=F32)
    gm = _sigmoid(mg_ref[...].astype(F32))
    merged = gm[:, 0:d] * a + gm[:, d:2 * d] * b
    y = _dot(merged.astype(BF16), wo_ref[...])
    z = alpha * x_ref[...] + (1.0 + g1_ref[...]) * y
    x1 = _layer_norm(z) * lg_ref[...] + lb_ref[...]
    o_ref[...] = x1
    _route_tile(x1, sc2_ref, sh2_ref, wr_ref, br_ref, u_ref, r_ref)


def _merge_project(o_nsa, o_fox, merge, x, wn, wf, wo, g1, ln_g, ln_b, sc2, sh2, w_r, b_r, alpha, tm=256):
    m, d = x.shape
    w = o_nsa.shape[0]
    resident = pl.Buffered(1)
    row = lambda i: (i, 0)
    fixed = lambda i: (0, 0)
    return pl.pallas_call(
        functools.partial(_merge_kernel, alpha=alpha),
        grid=(m // tm,),
        in_specs=[pl.BlockSpec((w, tm), lambda i: (0, i)), pl.BlockSpec((w, tm), lambda i: (0, i)),
                  pl.BlockSpec((tm, 2 * d), row), pl.BlockSpec((tm, d), row),
                  pl.BlockSpec((w, d), fixed, pipeline_mode=resident),
                  pl.BlockSpec((w, d), fixed, pipeline_mode=resident),
                  pl.BlockSpec((d, d), fixed, pipeline_mode=resident),
                  pl.BlockSpec((1, d), fixed), pl.BlockSpec((1, d), fixed), pl.BlockSpec((1, d), fixed),
                  pl.BlockSpec((1, d), fixed), pl.BlockSpec((1, d), fixed),
                  pl.BlockSpec((2, d, LANES), lambda i: (0, 0, 0)), pl.BlockSpec((1, LANES), fixed)],
        out_specs=[pl.BlockSpec((tm, d), row), pl.BlockSpec((tm, d), row), pl.BlockSpec((tm, LANES), row)],
        out_shape=[jax.ShapeDtypeStruct((m, d), F32), jax.ShapeDtypeStruct((m, d), F32),
                   jax.ShapeDtypeStruct((m, LANES), F32)],
        compiler_params=_cparams(("parallel",)),
        name="merge_project_ln_route",
    )(o_nsa, o_fox, merge, x, wn, wf, wo, g1, ln_g, ln_b, sc2, sh2, w_r, b_r)


def _route_tile(x1, sc_ref, sh_ref, w_ref, b_ref, u_ref, r_ref):
    u = _layer_norm(x1) * (1.0 + sc_ref[...]) + sh_ref[...]
    u_ref[...] = u
    u_hi = u.astype(BF16)
    u_lo = (u - u_hi.astype(F32)).astype(BF16)
    logits = _dot(u_hi, w_ref[0]) + _dot(u_lo, w_ref[0]) + _dot(u_hi, w_ref[1]) + b_ref[...]
    lane = lax.broadcasted_iota(jnp.int32, (1, LANES), 1).astype(F32)
    none = float(LANES)
    is_g = lane < N_GROUPS
    lg = jnp.where(is_g, logits, NEG_INF)
    eg = jnp.exp(lg - jnp.max(lg, axis=-1, keepdims=True))
    pg = eg / jnp.sum(eg, axis=-1, keepdims=True)
    p_grp = jnp.max(pg, axis=-1, keepdims=True)
    grp = jnp.min(jnp.where(pg == p_grp, lane, none), axis=-1, keepdims=True)
    lo = N_GROUPS + grp * EXPERTS_PER_GROUP
    is_e = (lane >= lo) & (lane < lo + EXPERTS_PER_GROUP)
    le = jnp.where(is_e, logits, NEG_INF)
    ee = jnp.exp(le - jnp.max(le, axis=-1, keepdims=True))
    pe = jnp.where(is_e, ee / jnp.sum(ee, axis=-1, keepdims=True), -1.0)
    p1 = jnp.max(pe, axis=-1, keepdims=True)
    i1 = jnp.min(jnp.where(pe == p1, lane, none), axis=-1, keepdims=True)
    pe2 = jnp.where(lane == i1, -1.0, pe)
    p2 = jnp.max(pe2, axis=-1, keepdims=True)
    i2 = jnp.min(jnp.where(pe2 == p2, lane, none), axis=-1, keepdims=True)
    den = p1 + p2
    r_ref[...] = jnp.where(lane == 0, i1 - N_GROUPS,
                           jnp.where(lane == 1, i2 - N_GROUPS,
                                     jnp.where(lane == 2, p_grp * p1 / den,
                                               jnp.where(lane == 3, p_grp * p2 / den, 0.0))))


def _moe_kernel(be_ref, nu_ref, tok_ref, tok_next_ref, dst_ref, rw_ref, u_hbm, wg_ref, wu_ref, wd_ref, out_hbm,
                xbuf, ybuf, wgb, wub, wdb, sem_in, sem_out, *, n_dump0):
    rb = ROW_BLOCK
    i = pl.program_id(0)
    last = nu_ref[0] - 1
    slot = i % 2

    def row_in(r, tok, sl):
        return pltpu.make_async_copy(u_hbm.at[pl.ds(tok, 1), :], xbuf.at[sl, pl.ds(r, 1), :], sem_in.at[sl])

    def row_out(r, dst):
        return pltpu.make_async_copy(ybuf.at[pl.ds(r, 1), :], out_hbm.at[pl.ds(dst, 1), :], sem_out)

    @pl.when(i == 0)
    def _():
        for r in range(rb):
            row_in(r, tok_ref[0, 0, r], 0).start()
        ybuf[...] = jnp.zeros(ybuf.shape, F32)
        pltpu.make_async_copy(ybuf, out_hbm.at[pl.ds(n_dump0, rb), :], sem_out).start()

    prev = be_ref[jnp.maximum(i - 1, 0)]

    @pl.when((i <= last) & ((i == 0) | (be_ref[i] != prev)))
    def _():
        wgb[...] = wg_ref[0].astype(BF16)
        wub[...] = wu_ref[0].astype(BF16)
        wdb[...] = wd_ref[0].astype(BF16)

    @pl.when(i <= last)
    def _():
        for r in range(rb):
            row_in(r, 0, slot).wait()
        xb = xbuf[slot].astype(BF16)
        for r in range(rb):
            row_in(r, tok_next_ref[0, 0, r], 1 - slot).start()
        gate = _dot(xb, wgb[...])
        up = _dot(xb, wub[...])
        hid = (gate * _sigmoid(gate)) * up
        y = _dot(hid.astype(BF16), wdb[...]) * rw_ref[0]
        for r in range(rb):
            row_out(r, 0).wait()
        ybuf[...] = y
        for r in range(rb):
            row_out(r, dst_ref[0, 0, r]).start(priority=r % 2)

    @pl.when(i == last)
    def _():
        for r in range(rb):
            row_out(r, 0).wait()
        for r in range(rb):
            row_in(r, 0, 1 - slot).wait()


def _moe_experts(u, blk_exp, n_used, row_tok, row_dst, row_w, w_gate, w_up, w_down):
    t, d = u.shape
    n_blocks = blk_exp.shape[0]
    de = w_gate.shape[-1]
    rb = ROW_BLOCK
    tok3 = row_tok.reshape(n_blocks, 1, rb)
    grid_spec = pltpu.PrefetchScalarGridSpec(
        num_scalar_prefetch=2,
        grid=(n_blocks,),
        in_specs=[pl.BlockSpec((1, 1, rb), lambda i, be, nu: (i, 0, 0), memory_space=pltpu.SMEM),
                  pl.BlockSpec((1, 1, rb), lambda i, be, nu: (jnp.minimum(i + 1, n_blocks - 1), 0, 0),
                               memory_space=pltpu.SMEM),
                  pl.BlockSpec((1, 1, rb), lambda i, be, nu: (i, 0, 0), memory_space=pltpu.SMEM),
                  pl.BlockSpec((1, rb, 1), lambda i, be, nu: (i, 0, 0)),
                  pl.BlockSpec(memory_space=pl.ANY),
                  pl.BlockSpec((1, d, de), lambda i, be, nu: (be[i], 0, 0)),
                  pl.BlockSpec((1, d, de), lambda i, be, nu: (be[i], 0, 0)),
                  pl.BlockSpec((1, de, d), lambda i, be, nu: (be[i], 0, 0))],
        out_specs=pl.BlockSpec(memory_space=pl.ANY),
        scratch_shapes=[pltpu.VMEM((2, rb, d), F32), pltpu.VMEM((rb, d), F32),
                        pltpu.VMEM((d, de), BF16), pltpu.VMEM((d, de), BF16), pltpu.VMEM((de, d), BF16),
                        pltpu.SemaphoreType.DMA((2,)), pltpu.SemaphoreType.DMA(())],
    )
    return pl.pallas_call(
        functools.partial(_moe_kernel, n_dump0=2 * t),
        grid_spec=grid_spec,
        out_shape=jax.ShapeDtypeStruct((2 * t + rb, d), F32),
        compiler_params=_cparams(("arbitrary",)),
        name="moe_experts",
    )(blk_exp, n_used, tok3, tok3, row_dst.reshape(n_blocks, 1, rb), row_w.reshape(n_blocks, rb, 1),
      u, w_gate, w_up, w_down)


def _moe_dispatch(route, t):
    k = 2
    eid = route[:, 0:k].astype(jnp.int32).reshape(-1)
    wts = route[:, k:2 * k].reshape(-1)
    n_asg = t * k
    n_rows = n_asg + N_EXPERTS * ROW_BLOCK
    n_blocks = n_rows // ROW_BLOCK
    onehot = (eid[:, None] == jnp.arange(N_EXPERTS, dtype=jnp.int32)[None, :]).astype(jnp.int32)
    rank = jnp.sum((jnp.cumsum(onehot, axis=0) - onehot) * onehot, axis=1)
    counts = jnp.sum(onehot, axis=0)
    padded = (counts + ROW_BLOCK - 1) // ROW_BLOCK * ROW_BLOCK
    pad_end = jnp.cumsum(padded)
    pad_start = pad_end - padded
    dest = jnp.sum(onehot * pad_start[None, :], axis=1) + rank
    asg = jnp.arange(n_asg, dtype=jnp.int32)
    upd = jnp.stack([(asg % k) * t + asg // k, lax.bitcast_convert_type(wts, jnp.int32)], axis=1)
    init = jnp.stack([n_asg + jnp.arange(n_rows, dtype=jnp.int32) % ROW_BLOCK,
                      jnp.zeros((n_rows,), jnp.int32)], axis=1)
    rows = init.at[dest].set(upd)
    row_dst = rows[:, 0]
    row_w = lax.bitcast_convert_type(rows[:, 1], F32)
    row_tok = jnp.where(row_dst < n_asg, row_dst % t, 0)
    blk_start = jnp.arange(n_blocks, dtype=jnp.int32) * ROW_BLOCK
    blk_exp = jnp.minimum(jnp.sum((pad_end[None, :] <= blk_start[:, None]).astype(jnp.int32), axis=1),
                          N_EXPERTS - 1)
    n_used = (pad_end[N_EXPERTS - 1:] // ROW_BLOCK).astype(jnp.int32)
    return blk_exp, n_used, row_tok, row_dst, row_w


def _final_kernel(x_ref, y0_ref, y1_ref, g2_ref, lg_ref, lb_ref, o_ref, *, alpha):
    z = alpha * x_ref[...] + (1.0 + g2_ref[...]) * (y0_ref[...] + y1_ref[...])
    o_ref[...] = _layer_norm(z) * lg_ref[...] + lb_ref[...]


def _final_ln(x1, y2, g2, ln_g, ln_b, alpha, tm=512):
    m, d = x1.shape
    nb = m // tm
    fixed = lambda i: (0, 0)
    return pl.pallas_call(
        functools.partial(_final_kernel, alpha=alpha),
        grid=(nb,),
        in_specs=[pl.BlockSpec((tm, d), lambda i: (i, 0)),
                  pl.BlockSpec((tm, d), lambda i: (i, 0)),
                  pl.BlockSpec((tm, d), lambda i: (i + nb, 0)),
                  pl.BlockSpec((1, d), fixed), pl.BlockSpec((1, d), fixed), pl.BlockSpec((1, d), fixed)],
        out_specs=pl.BlockSpec((tm, d), lambda i: (i, 0)),
        out_shape=jax.ShapeDtypeStruct((m, d), F32),
        compiler_params=_cparams(("parallel",)),
        name="final_ln",
    )(x1, y2, y2, g2, ln_g, ln_b)


def _to_lane_blocks(a, tq):
    g, hpg, s, c = a.shape
    return a.reshape(g, hpg, s // tq, tq, c).transpose(0, 2, 4, 1, 3).reshape(g, s // tq, c, hpg * tq)


def _layer(x2d, c, w_ada, b_ada, w_in, b_fgt, t5_table, cmp_pe, cmp_w1, cmp_b1, cmp_w2,
           w_br_nsa, w_br_fox, w_o, ln1_g, ln1_b, w_rg, b_rg, w_re, b_re,
           w_gate, w_up, w_down, ln2_g, ln2_b, alpha):
    s, d = x2d.shape
    hd = HEAD_DIM
    g = NSA_GROUPS
    mod = _ada_mod(c, w_ada, b_ada)
    sh1, sc1, g1, sh2, sc2, g2 = [mod[:, i * d:(i + 1) * d] for i in range(6)]

    c_q = NSA_HEADS * hd
    c_kv = 6 * g * hd
    c_gate = 3 * NSA_HEADS
    c_fox = 3 * FOX_HEADS * hd
    off_kv = c_q
    off_gate = off_kv + c_kv
    off_fox = off_gate + c_gate
    off_fgt = off_fox + c_fox
    off_merge = off_fgt + FOX_HEADS
    qscale = hd ** -0.5 * LOG2E
    nh = FOX_HEADS
    gw = g * hd

    def kv_cols(z):
        return w_in[:, off_kv + z * gw:off_kv + (z + 1) * gw]

    def lane_padded(w, heads):
        return jnp.pad(w.reshape(d, heads, hd), ((0, 0), (0, 0), (0, AUG_K - hd))).reshape(d, heads * AUG_K)

    fox_q, fox_k, fox_v = [w_in[:, off_fox + i * nh * hd:off_fox + (i + 1) * nh * hd] for i in range(3)]
    w_ch = jnp.concatenate([w_in[:, 0:off_kv] * qscale, kv_cols(3), kv_cols(5), fox_q * qscale, fox_v],
                           axis=1).T.astype(BF16)
    w_tok = jnp.concatenate([kv_cols(0), kv_cols(1), lane_padded(kv_cols(2), g), lane_padded(kv_cols(4), g),
                             lane_padded(fox_k, nh)], axis=1).astype(BF16)
    n_small = c_gate + FOX_HEADS
    w_small = jnp.concatenate([w_in[:, off_gate:off_fox], w_in[:, off_fgt:off_merge],
                               jnp.zeros((d, LANES - n_small), F32)], axis=1).astype(BF16)
    w_merge = w_in[:, off_merge:].astype(BF16)

    u = _ln_mod(x2d, sc1, sh1, BF16, tm=PROJ_TM)
    ch = _matmul_nt(w_ch, u, BF16, PROJ_TM, "in_proj_channel_major")
    tok = _matmul(u, w_tok, BF16, PROJ_TM, w_tok.shape[1], "in_proj_token_major")
    small = _matmul(u, w_small, F32, PROJ_TM, LANES, "in_proj_small")
    merge = _matmul(u, w_merge, BF16, PROJ_TM, 2048, "in_proj_merge")

    ch = ch.reshape(-1, hd, s)
    ch_vs_block0 = NSA_HEADS
    ch_vw_block0 = ch_vs_block0 + g
    ch_fq_block0 = ch_vw_block0 + g
    ch_fv_block0 = ch_fq_block0 + nh
    tok_ks_block0 = 2 * gw // AUG_K
    tok_kw_block0 = tok_ks_block0 + g
    tok_fox_block0 = tok_kw_block0 + g

    kv_cmp_in = tok[:, 0:2 * gw].reshape(s, 2, g, hd).transpose(1, 2, 0, 3)
    kv_cmp = _compress(kv_cmp_in, cmp_pe, cmp_w1, cmp_b1, cmp_w2)
    nch = s // CMP_STRIDE
    ncp = nch + LANES
    nslc = s // SLC_LEN
    kv_cmp_pad = jnp.pad(kv_cmp.astype(BF16), ((0, 0), (0, 0), (CMP_PAD, ncp - nch - CMP_PAD), (0, 0)))
    b_cmp, b_sel, b_win, far3 = _nsa_bias_tables(t5_table)

    row_ix = np.arange(ncp)
    row_ok = (row_ix >= CMP_PAD) & (row_ix < nch - 1 + CMP_PAD)
    row_cols = np.zeros((ncp, 2 * AUG_K - hd), np.float32)
    row_cols[:, ROW_BIAS - hd:ROW_BIAS - hd + 3] = row_ok[:, None]
    row_cols[:, ROW_BIAS - hd + 3] = ~row_ok
    row_cols[row_ix, AUG_K - hd + row_ix // BF16_ROWS] = 1.0
    kc_aug = jnp.concatenate([kv_cmp_pad[0], jnp.broadcast_to(jnp.asarray(row_cols, BF16), (g,) + row_cols.shape)],
                             axis=-1)
    vct = kv_cmp_pad[1].transpose(0, 2, 1)
    oct_, selt = _nsa_compress_select(ch, kc_aug, vct, b_cmp, far3)

    period = BF16_ROWS * SLC_LEN
    k_extra = np.zeros((period, AUG_K), np.float32)
    k_extra[np.arange(period), ROW_MASK + np.arange(period) // SLC_LEN] = 1.0
    k_extra[:, ROW_BIAS:ROW_BIAS + 3] = 1.0
    gates = small[:, 0:c_gate].reshape(s, g, NSA_HPG, 3).transpose(1, 2, 0, 3)
    gates_t = jnp.pad(_to_lane_blocks(gates, ATT_TQ), ((0, 0), (0, 0), (0, 5), (0, 0)))
    o_nsa_t = _nsa_attention(tok, tok_ks_block0, tok_kw_block0, jnp.asarray(k_extra, BF16), ch, ch_vs_block0,
                             ch_vw_block0, selt, far3, b_sel, b_win, oct_, gates_t)
    o_nsa = o_nsa_t.reshape(MIX_W, s)

    fgt_bias = jnp.concatenate([jnp.zeros((c_gate,), F32), b_fgt, jnp.zeros((LANES - n_small,), F32)])[None, :]
    decay_k, decay_ends = _decay_pieces(small, fgt_bias, c_gate, nh)
    o_fox = _fox_attention(ch, ch_fq_block0, ch_fv_block0, tok, tok_fox_block0, decay_k, decay_ends)
    o_fox = o_fox.reshape(MIX_W, s)

    n_r = N_GROUPS + N_EXPERTS
    w_r = jnp.concatenate([w_rg, w_re.reshape(d, N_EXPERTS), jnp.zeros((d, LANES - n_r), F32)], axis=1)
    b_r = jnp.concatenate([b_rg, b_re.reshape(N_EXPERTS), jnp.zeros((LANES - n_r,), F32)])[None, :]
    w_r_hi, w_r_lo, _ = _split3_exact(w_r)
    x1, u2, route = _merge_project(o_nsa, o_fox, merge, x2d, w_br_nsa.astype(BF16), w_br_fox.astype(BF16),
                                   w_o.astype(BF16), g1, ln1_g[None, :], ln1_b[None, :], sc2, sh2,
                                   jnp.stack([w_r_hi, w_r_lo]), b_r, alpha)

    blk_exp, n_used, row_tok, row_dst, row_w = _moe_dispatch(route, s)
    y2 = _moe_experts(u2, blk_exp, n_used, row_tok, row_dst, row_w, w_gate, w_up, w_down)
    return _final_ln(x1, y2, g2, ln2_g[None, :], ln2_b[None, :], alpha)


def kernel(x, c, w_ada, b_ada, w_in, b_fgt, t5_table, cmp_pe, cmp_w1, cmp_b1, cmp_w2, w_br_nsa, w_br_fox, w_o,
           ln1_g, ln1_b, w_rg, b_rg, w_re, b_re, w_gate, w_up, w_down, ln2_g, ln2_b):
    b, s, d = x.shape
    depth = w_ada.shape[0]
    assert b == 1
    alpha = (2 * depth) ** 0.25
    h = x[0]
    for l in range(depth):
        h = _layer(h, c, w_ada[l], b_ada[l], w_in[l], b_fgt[l], t5_table, cmp_pe[l], cmp_w1[l], cmp_b1[l],
                   cmp_w2[l], w_br_nsa[l], w_br_fox[l], w_o[l], ln1_g[l], ln1_b[l], w_rg[l], b_rg[l],
                   w_re[l], b_re[l], w_gate[l], w_up[l], w_down[l], ln2_g[l], ln2_b[l], alpha)
    return h[None]
```

```python
import functools
import math

import numpy as np
import jax
import jax.numpy as jnp
from jax import lax
from jax.experimental import pallas as pl
from jax.experimental.pallas import tpu as pltpu

F32 = jnp.float32
BF16 = jnp.bfloat16
HIGHEST = lax.Precision.HIGHEST
LOG2E = math.log2(math.e)

HEAD_DIM = 64
NSA_HEADS = 8
NSA_GROUPS = 2
NSA_HPG = NSA_HEADS // NSA_GROUPS
FOX_HEADS = 8
MIX_W = NSA_HEADS * HEAD_DIM
CMP_LEN = 32
CMP_STRIDE = 16
SLC_LEN = 64
SLC_TOPK = 16
WINDOW = 512
T5_BUCKETS = 32
T5_MAX_EXACT = 16
T5_MAX_DIST = 128
N_GROUPS = 8
EXPERTS_PER_GROUP = 8
N_EXPERTS = N_GROUPS * EXPERTS_PER_GROUP
ROW_BLOCK = 128
LN_EPS = 1e-5
NEG_INF = -1e30
M_INIT = -1e29
FORCE_SCORE = 1e4

LANES = 128
BF16_ROWS = 16
CMP_PAD = 8
ATT_TQ = 256
CMP_NEAR_ROWS = 32
FOX_TQ = 1024
FOX_TK = 512
FOX_SKIP_MARGIN = 160.0
FOX_BOUND_SLACK = 1.02
PROJ_TM = 1024
AUG_K = 128
AUG_V = HEAD_DIM + 16
ROW_MASK = HEAD_DIM
ROW_BIAS = HEAD_DIM + 16
VMEM_LIMIT = 56 * 1024 * 1024


def _cparams(sem, vmem=VMEM_LIMIT):
    return pltpu.CompilerParams(dimension_semantics=sem, vmem_limit_bytes=vmem)


def _sigmoid(x):
    return 1.0 / (1.0 + jnp.exp(-x))


def _layer_norm(x):
    mu = jnp.mean(x, axis=-1, keepdims=True)
    xc = x - mu
    var = jnp.mean(xc * xc, axis=-1, keepdims=True)
    return xc * lax.rsqrt(var + LN_EPS)


def _split3(x):
    hi = x.astype(BF16)
    r1 = x - hi.astype(F32)
    mid = r1.astype(BF16)
    lo = (r1 - mid.astype(F32)).astype(BF16)
    return hi, mid, lo


def _split3_exact(x):
    def trunc(v):
        bits = lax.bitcast_convert_type(v, jnp.uint32) & jnp.uint32(0xFFFF0000)
        return lax.bitcast_convert_type(bits, F32)
    hi = trunc(x)
    r1 = x - hi
    mid = trunc(r1)
    lo = r1 - mid
    return hi.astype(BF16), mid.astype(BF16), lo.astype(BF16)


def _dot(a, b):
    return jnp.dot(a, b, preferred_element_type=F32)


def _dot_nt(a, b):
    return lax.dot_general(a, b, (((1,), (1,)), ((), ())), preferred_element_type=F32)


def _dot3(x, w_bf16):
    hi, mid, lo = _split3(x)
    return _dot(hi, w_bf16) + _dot(mid, w_bf16) + _dot(lo, w_bf16)


def _dot3_rhs(w_bf16, x):
    hi, mid, lo = _split3(x)
    return _dot(w_bf16, hi) + _dot(w_bf16, mid) + _dot(w_bf16, lo)


def _ada_kernel(c_ref, w_ref, b_ref, o_ref):
    c = c_ref[...]
    a = c * _sigmoid(c)
    o_ref[...] = jnp.dot(a, w_ref[...], precision=HIGHEST, preferred_element_type=F32) + b_ref[...]


def _ada_mod(c, w, b):
    d, n = w.shape
    tn = 1024
    c8 = jnp.broadcast_to(c, (8, d))
    out = pl.pallas_call(
        _ada_kernel,
        grid=(n // tn,),
        in_specs=[pl.BlockSpec((8, d), lambda j: (0, 0)),
                  pl.BlockSpec((d, tn), lambda j: (0, j)),
                  pl.BlockSpec((1, tn), lambda j: (0, j))],
        out_specs=pl.BlockSpec((8, tn), lambda j: (0, j)),
        out_shape=jax.ShapeDtypeStruct((8, n), F32),
        compiler_params=_cparams(("parallel",)),
        name="ada_mod",
    )(c8, w, b.reshape(1, n))
    return out[0:1]


def _lnmod_kernel(x_ref, sc_ref, sh_ref, o_ref):
    y = _layer_norm(x_ref[...])
    o_ref[...] = (y * (1.0 + sc_ref[...]) + sh_ref[...]).astype(o_ref.dtype)


def _ln_mod(x, sc, sh, out_dtype, tm=512):
    m, d = x.shape
    return pl.pallas_call(
        _lnmod_kernel,
        grid=(m // tm,),
        in_specs=[pl.BlockSpec((tm, d), lambda i: (i, 0)),
                  pl.BlockSpec((1, d), lambda i: (0, 0)),
                  pl.BlockSpec((1, d), lambda i: (0, 0))],
        out_specs=pl.BlockSpec((tm, d), lambda i: (i, 0)),
        out_shape=jax.ShapeDtypeStruct((m, d), out_dtype),
        compiler_params=_cparams(("parallel",)),
        name="ln_mod",
    )(x, sc, sh)


def _mm_kernel(a_ref, w_ref, o_ref):
    o_ref[...] = _dot(a_ref[...], w_ref[...]).astype(o_ref.dtype)


def _matmul(a, w, out_dtype, tm, tn, name):
    m, k = a.shape
    n = w.shape[1]
    return pl.pallas_call(
        _mm_kernel,
        grid=(n // tn, m // tm),
        in_specs=[pl.BlockSpec((tm, k), lambda j, i: (i, 0)),
                  pl.BlockSpec((k, tn), lambda j, i: (0, j))],
        out_specs=pl.BlockSpec((tm, tn), lambda j, i: (i, j)),
        out_shape=jax.ShapeDtypeStruct((m, n), out_dtype),
        compiler_params=_cparams(("parallel", "parallel")),
        name=name,
    )(a, w)


def _mm_nt_kernel(w_ref, a_ref, o_ref):
    o_ref[...] = _dot_nt(w_ref[...], a_ref[...]).astype(o_ref.dtype)


def _matmul_nt(w_t, a, out_dtype, tm, name):
    n, k = w_t.shape
    m = a.shape[0]
    return pl.pallas_call(
        _mm_nt_kernel,
        grid=(m // tm,),
        in_specs=[pl.BlockSpec((n, k), lambda i: (0, 0), pipeline_mode=pl.Buffered(1)),
                  pl.BlockSpec((tm, k), lambda i: (i, 0))],
        out_specs=pl.BlockSpec((n, tm), lambda i: (0, i)),
        out_shape=jax.ShapeDtypeStruct((n, m), out_dtype),
        compiler_params=_cparams(("parallel",)),
        name=name,
    )(w_t, a)


def _gelu_tanh(x):
    return 0.5 * x * (1.0 + jnp.tanh(math.sqrt(2.0 / math.pi) * (x + 0.044715 * (x * x * x))))


def _compress_kernel(c_ref, pe_ref, w1a_ref, w1b_ref, b1_ref, w2_ref, o_ref, *, nch):
    c = c_ref[0, 0]
    w1a = w1a_ref[0]
    w1b = w1b_ref[0]
    half = CMP_STRIDE * HEAD_DIM
    a = _dot(c, w1a)
    b = _dot(c, w1b)
    b_next = pltpu.roll(b, shift=nch - 1, axis=0)
    pe = pe_ref[0]
    pb = _dot(pe[:, :half], w1a) + _dot(pe[:, half:], w1b)
    hid = _gelu_tanh(a + b_next + pb[0:1, :] + b1_ref[0])
    o_ref[0, 0] = _dot(hid.astype(BF16), w2_ref[0])


def _compress(kv_cmp, pe, w1, b1, w2):
    z, g, s, hd = kv_cmp.shape
    nch = s // CMP_STRIDE
    half = CMP_STRIDE * hd
    chunks = kv_cmp.reshape(z, g, nch, half)
    pe8 = jnp.broadcast_to(pe.reshape(z, 1, CMP_LEN * hd), (z, 8, CMP_LEN * hd)).astype(BF16)
    w1b16 = w1.astype(BF16)
    hidn = w1.shape[-1]
    return pl.pallas_call(
        functools.partial(_compress_kernel, nch=nch),
        grid=(z, g),
        in_specs=[pl.BlockSpec((1, 1, nch, half), lambda zi, gi: (zi, gi, 0, 0)),
                  pl.BlockSpec((1, 8, 2 * half), lambda zi, gi: (zi, 0, 0)),
                  pl.BlockSpec((1, half, hidn), lambda zi, gi: (zi, 0, 0)),
                  pl.BlockSpec((1, half, hidn), lambda zi, gi: (zi, 1, 0)),
                  pl.BlockSpec((1, 1, hidn), lambda zi, gi: (zi, 0, 0)),
                  pl.BlockSpec((1, hidn, hd), lambda zi, gi: (zi, 0, 0))],
        out_specs=pl.BlockSpec((1, 1, nch, hd), lambda zi, gi: (zi, gi, 0, 0)),
        out_shape=jax.ShapeDtypeStruct((z, g, nch, hd), F32),
        compiler_params=_cparams(("parallel", "parallel")),
        name="compress_kv",
    )(chunks, pe8, w1b16, w1b16, b1.reshape(z, 1, hidn), w2.astype(BF16))


def _t5_bucket_np(dist):
    n = np.maximum(dist, 0)
    ratio = np.log(np.maximum(n, T5_MAX_EXACT).astype(np.float64) / T5_MAX_EXACT)
    big = T5_MAX_EXACT + (ratio / math.log(T5_MAX_DIST / T5_MAX_EXACT)
                          * (T5_BUCKETS - T5_MAX_EXACT)).astype(np.int64)
    return np.where(n < T5_MAX_EXACT, n, np.minimum(big, T5_BUCKETS - 1)).astype(np.int32)


def _t5_lookup(tbh, dist):
    onehot = np.eye(T5_BUCKETS, dtype=np.float32)[_t5_bucket_np(dist).reshape(-1)]
    vals = jnp.einsum('ghb,nb->ghn', tbh, jnp.asarray(onehot), precision=HIGHEST)
    return vals.reshape(tbh.shape[:2] + dist.shape)


def _toeplitz_kernel(w_ref, o_ref, *, n_keys, tq, lead_zero, trail_masked):
    width = w_ref.shape[-1]
    rows = jnp.broadcast_to(w_ref[0, 0], (n_keys, width))
    rolled = pltpu.roll(rows, shift=width - (n_keys - 1), axis=1, stride=1, stride_axis=0)
    parts = [rolled[:, 0:tq]]
    if lead_zero:
        parts.insert(0, jnp.zeros((lead_zero, tq), F32))
    if trail_masked:
        parts.append(jnp.full((trail_masked, tq), NEG_INF, F32))
    o_ref[0] = jnp.concatenate(parts, axis=0)


def _att_table_t(tbh, n_keys, tq, lo, hi, minus_far, lead_zero=0, trail_masked=0):
    length = n_keys + tq - 1
    d = np.arange(length) - (tq - 1)
    valid = (d >= lo) & (d < hi)
    vals = _t5_lookup(tbh, d)
    if minus_far:
        vals = vals - tbh[:, :, T5_BUCKETS - 1:]
    w = jnp.where(jnp.asarray(valid), vals * LOG2E, NEG_INF)
    width = pl.next_power_of_2(length)
    w = jnp.pad(w, ((0, 0), (0, 0), (0, width - length)))[:, :, None, :]
    g, hpg = tbh.shape[:2]
    n_rows = lead_zero + n_keys + trail_masked
    return pl.pallas_call(
        functools.partial(_toeplitz_kernel, n_keys=n_keys, tq=tq, lead_zero=lead_zero, trail_masked=trail_masked),
        grid=(g, hpg),
        in_specs=[pl.BlockSpec((1, 1, 1, width), lambda gi, hi: (gi, hi, 0, 0))],
        out_specs=pl.BlockSpec((1, n_rows, tq), lambda gi, hi: (gi, 0, hi)),
        out_shape=jax.ShapeDtypeStruct((g, n_rows, hpg * tq), F32),
        compiler_params=_cparams(("parallel", "parallel")),
        name="t5_toeplitz",
    )(w)


def _nsa_bias_tables(t5_table):
    tbh = t5_table.T.reshape(NSA_GROUPS, NSA_HPG, T5_BUCKETS).astype(F32)
    j = np.arange(CMP_NEAR_ROWS)[:, None]
    i = np.arange(ATT_TQ)[None, :]
    dist = i - (CMP_LEN - 1) - CMP_STRIDE * (j - CMP_PAD)
    vals = (_t5_lookup(tbh, dist) - tbh[:, :, T5_BUCKETS - 1][:, :, None, None]) * LOG2E
    vals = jnp.where(jnp.asarray(dist >= 0)[None, None], vals, NEG_INF)
    b_cmp = vals.transpose(0, 2, 1, 3).reshape(NSA_GROUPS, CMP_NEAR_ROWS, NSA_HPG * ATT_TQ)
    far = tbh[:, :, T5_BUCKETS - 1] * LOG2E
    b_sel = _att_table_t(tbh, 2 * ATT_TQ, ATT_TQ, 0, 1 << 30, True, lead_zero=ATT_TQ)
    b_win = _att_table_t(tbh, 3 * ATT_TQ, ATT_TQ, 0, WINDOW, False, trail_masked=ATT_TQ)
    hi, mid, lo = _split3_exact(jnp.repeat(far, ATT_TQ, axis=1))
    far3 = jnp.stack([hi, mid, lo, jnp.full(hi.shape, NEG_INF, BF16)], axis=1)
    far3 = jnp.pad(far3, ((0, 0), (0, BF16_ROWS - 4), (0, 0)))
    return b_cmp, b_sel, b_win, far3


def _nsa_cmp_kernel(qt_ref, kc_ref, vct_ref, near_ref, far3_ref, oct_ref, selt_ref, rhs, s_scr, imp_scr, *, nslc):
    tq = ATT_TQ
    lanes = NSA_HPG * tq
    qb = pl.program_id(1)
    cpb = tq // CMP_STRIDE
    rhs[...] = jnp.zeros(rhs.shape, BF16)
    for h in range(NSA_HPG):
        rhs[0:HEAD_DIM, h * tq:(h + 1) * tq] = qt_ref[h]
    rhs[ROW_BIAS:ROW_BIAS + BF16_ROWS, :] = far3_ref[0]
    chunk = lax.broadcasted_iota(jnp.int32, (AUG_K, lanes), 0)
    rhs[AUG_K:2 * AUG_K, :] = jnp.where(chunk >= qb + CMP_NEAR_ROWS // BF16_ROWS, NEG_INF, 0.0).astype(BF16)
    r0 = pl.multiple_of(cpb * qb, BF16_ROWS)
    n_lane_blocks = tq // LANES
    ncp = s_scr.shape[0]
    imp_scr[...] = jnp.zeros(imp_scr.shape, F32)

    def attend(rows):
        s_scr[0:rows, :] = _dot(kc_ref[0, 0:rows, :], rhs[...])
        s_scr[pl.ds(r0, CMP_NEAR_ROWS), :] = s_scr[pl.ds(r0, CMP_NEAR_ROWS), :] + near_ref[0]
        s = s_scr[0:rows, :]
        m = jnp.max(s, axis=0, keepdims=True)
        e = jnp.exp2(s - m)
        l = jnp.sum(e, axis=0, keepdims=True)
        p = e * jnp.where(m > M_INIT, 1.0 / l, 0.0)
        oct_ref[0, 0] = _dot(vct_ref[0, :, 0:rows], p.astype(BF16))
        imp = p[:, 0:tq]
        for h in range(1, NSA_HPG):
            imp = imp + p[:, h * tq:(h + 1) * tq]
        for c in range(n_lane_blocks):
            imp_scr[c, 0:rows, :] = imp[:, c * LANES:(c + 1) * LANES]

    limits = sorted({min(ncp, -(-(ncp * k // 3) // LANES) * LANES) for k in (1, 2, 3)})
    lo_qb = 0
    for rows in limits:
        hi_qb = (rows - CMP_NEAR_ROWS) // cpb if rows < ncp else pl.num_programs(1) - 1

        @pl.when((qb >= lo_qb) & (qb <= hi_qb))
        def _(rows=rows):
            attend(rows)

        lo_qb = hi_qb + 1
    ratio = SLC_LEN // CMP_STRIDE

    def taps(off):
        return jnp.concatenate([imp_scr[c, pl.ds(CMP_PAD + off, nslc, stride=ratio), :]
                                for c in range(n_lane_blocks)], axis=1)

    p_slc = 0.5 * (taps(-1) + taps(ratio - 1))
    for off in range(ratio - 1):
        p_slc = p_slc + taps(off)
    blk = lax.broadcasted_iota(jnp.int32, (nslc, tq), 0)
    cur = (qb * tq + lax.broadcasted_iota(jnp.int32, (nslc, tq), 1)) // SLC_LEN
    forced = (blk == 0) | (blk == cur) | (blk == cur - 1)
    score = jnp.where(forced, FORCE_SCORE, jnp.where(blk <= cur, p_slc, -1.0))
    blk_f = blk.astype(F32)
    sel = jnp.zeros((nslc, tq), F32)
    for _ in range(min(SLC_TOPK, nslc)):
        mx = jnp.max(score, axis=0, keepdims=True)
        first = jnp.min(jnp.where(score == mx, blk_f, float(nslc)), axis=0, keepdims=True)
        hit = blk_f == first
        sel = jnp.where(hit, 1.0, sel)
        score = jnp.where(hit, -2.0, score)
    selt_ref[0] = sel.astype(BF16)


def _nsa_compress_select(ch, kc_aug, vct, b_cmp, far3):
    g = kc_aug.shape[0]
    _, hd, s = ch.shape
    ncp = kc_aug.shape[1]
    tq = ATT_TQ
    nq = s // tq
    lanes = NSA_HPG * tq
    nslc = nq * tq // SLC_LEN
    assert ncp // BF16_ROWS <= AUG_K
    assert tq // CMP_STRIDE == BF16_ROWS
    per_q = lambda gi, qi: (gi, qi, 0, 0)
    per_g = lambda gi, qi: (gi, 0, 0)
    return pl.pallas_call(
        functools.partial(_nsa_cmp_kernel, nslc=nslc),
        grid=(g, nq),
        in_specs=[pl.BlockSpec((NSA_HPG, hd, tq), lambda gi, qi: (gi, 0, qi)),
                  pl.BlockSpec((1, ncp, 2 * AUG_K), per_g),
                  pl.BlockSpec((1, hd, ncp), per_g),
                  pl.BlockSpec((1, CMP_NEAR_ROWS, lanes), per_g),
                  pl.BlockSpec((1, BF16_ROWS, lanes), per_g)],
        out_specs=[pl.BlockSpec((1, 1, hd, lanes), per_q),
                   pl.BlockSpec((1, nslc, tq), lambda gi, qi: (gi, 0, qi))],
        out_shape=[jax.ShapeDtypeStruct((g, nq, hd, lanes), F32),
                   jax.ShapeDtypeStruct((g, nslc, nq * tq), BF16)],
        scratch_shapes=[pltpu.VMEM((2 * AUG_K, lanes), BF16), pltpu.VMEM((ncp, lanes), F32),
                        pltpu.VMEM((tq // LANES, ncp, LANES), F32)],
        compiler_params=_cparams(("parallel", "parallel")),
        name="nsa_compress_select",
    )(ch, kc_aug, vct, b_cmp, far3)


def _flash_init_t(m_ref, acc_ref):
    m_ref[...] = jnp.full(m_ref.shape, M_INIT, F32)
    acc_ref[...] = jnp.zeros(acc_ref.shape, F32)


def _flash_step_t(s, vt_tile, m_ref, acc_ref):
    m_old = m_ref[...]
    m_new = jnp.maximum(m_old, jnp.max(s, axis=0, keepdims=True))
    p = jnp.exp2(s - m_new).astype(BF16)
    acc_ref[...] = jnp.exp2(m_old - m_new) * acc_ref[...] + _dot(vt_tile, p)
    m_ref[...] = m_new


def _flash_result_t(acc_ref):
    acc = acc_ref[...]
    return acc[0:HEAD_DIM, :] / acc[HEAD_DIM:HEAD_DIM + 1, :]


def _nsa_att_kernel(qt_ref, ks_ref, kx_ref, vst_ref, kw_ref, vwt_ref, selt_ref, far3_ref, bsel_ref, bwin_ref,
                    oct_ref, gate_ref, o_ref, rhs_s, rhs_w, mask_t, ms, accs, mw, accw, s_even, s_odd):
    tq = ATT_TQ
    qb = pl.program_id(1)
    ones_rows = _ones_rows(tq)
    kx_tiles = kx_ref.shape[0] // tq
    qt = jnp.concatenate([qt_ref[h] for h in range(NSA_HPG)], axis=1)
    rhs_s[...] = jnp.zeros(rhs_s.shape, BF16)
    rhs_s[0:HEAD_DIM, :] = qt
    rhs_s[ROW_BIAS:ROW_BIAS + BF16_ROWS, :] = far3_ref[0]
    rhs_w[...] = jnp.zeros(rhs_w.shape, BF16)
    rhs_w[0:HEAD_DIM, :] = qt
    madd = ((selt_ref[0].astype(F32) - 1.0) * (-NEG_INF)).astype(BF16)
    mask_t[...] = jnp.concatenate([madd] * NSA_HPG, axis=1)
    _flash_init_t(ms, accs)
    _flash_init_t(mw, accw)
    blocks_per_tile = tq // SLC_LEN

    def sel_scores(kt, s_ref):
        kt = jnp.minimum(kt, qb)
        k0 = pl.multiple_of(kt * tq, tq)
        chunk = pl.multiple_of((kt * blocks_per_tile) // BF16_ROWS * BF16_ROWS, BF16_ROWS)
        rhs_s[ROW_MASK:ROW_MASK + BF16_ROWS, :] = mask_t[pl.ds(chunk, BF16_ROWS), :]
        rel = jnp.clip(kt - qb + 2, 0, 2)
        table = bsel_ref[0, pl.ds(pl.multiple_of(rel * tq, tq), tq), :]
        k_aug = ks_ref[pl.ds(k0, tq), :] + kx_ref[pl.ds(pl.multiple_of((kt % kx_tiles) * tq, tq), tq), :]
        s_ref[...] = _dot(k_aug, rhs_s[...]) + table

    def sel_consume(kt, s_ref):
        k0 = pl.multiple_of(kt * tq, tq)
        vt_aug = jnp.concatenate([vst_ref[0, :, pl.ds(k0, tq)], ones_rows], axis=0)
        _flash_step_t(s_ref[...], vt_aug, ms, accs)

    def win_scores(j, s_ref):
        kt = qb - 2 + j
        k0 = pl.multiple_of(jnp.maximum(kt, 0) * tq, tq)
        row = pl.multiple_of(jnp.where(kt >= 0, j, 3) * tq, tq)
        s_ref[...] = _dot(kw_ref[pl.ds(k0, tq), :], rhs_w[...]) + bwin_ref[0, pl.ds(row, tq), :]

    def win_consume(j, s_ref):
        k0 = pl.multiple_of(jnp.maximum(qb - 2 + j, 0) * tq, tq)
        vt_aug = jnp.concatenate([vwt_ref[0, :, pl.ds(k0, tq)], ones_rows], axis=0)
        _flash_step_t(s_ref[...], vt_aug, mw, accw)

    n_sel = qb + 1
    sel_scores(0, s_even)

    def pair_body(j, carry):
        sel_scores(2 * j + 1, s_odd)
        sel_consume(2 * j, s_even)
        sel_scores(2 * j + 2, s_even)
        sel_consume(2 * j + 1, s_odd)
        return carry

    lax.fori_loop(0, n_sel // 2, pair_body, 0)
    win_scores(0, s_odd)

    @pl.when(n_sel % 2 == 1)
    def _():
        sel_consume(qb, s_even)

    win_scores(1, s_even)
    win_consume(0, s_odd)
    win_scores(2, s_odd)
    win_consume(1, s_even)
    win_consume(2, s_odd)

    gt = _sigmoid(gate_ref[0, 0])
    out = gt[0:1, :] * oct_ref[0, 0] + gt[1:2, :] * _flash_result_t(accs) + gt[2:3, :] * _flash_result_t(accw)
    for h in range(NSA_HPG):
        o_ref[0, h] = out[:, h * tq:(h + 1) * tq].astype(o_ref.dtype)


def _nsa_attention(k_tok, ks_block0, kw_block0, k_extra, ch, vs_block0, vw_block0, selt, far3, b_sel, b_win,
                   oct_, gates_t):
    g, nq, hd, lanes = oct_.shape
    s = k_tok.shape[0]
    nslc = selt.shape[1]
    tq = ATT_TQ
    resident = pl.Buffered(1)
    per_q = lambda gi, qi: (gi, qi, 0, 0)
    per_g = lambda gi, qi: (gi, 0, 0)
    return pl.pallas_call(
        _nsa_att_kernel,
        grid=(g, nq),
        in_specs=[pl.BlockSpec((NSA_HPG, hd, tq), lambda gi, qi: (gi, 0, qi)),
                  pl.BlockSpec((s, AUG_K), lambda gi, qi: (0, ks_block0 + gi), pipeline_mode=resident),
                  pl.BlockSpec(k_extra.shape, lambda gi, qi: (0, 0), pipeline_mode=resident),
                  pl.BlockSpec((1, hd, s), lambda gi, qi: (vs_block0 + gi, 0, 0), pipeline_mode=resident),
                  pl.BlockSpec((s, AUG_K), lambda gi, qi: (0, kw_block0 + gi), pipeline_mode=resident),
                  pl.BlockSpec((1, hd, s), lambda gi, qi: (vw_block0 + gi, 0, 0), pipeline_mode=resident),
                  pl.BlockSpec((1, nslc, tq), lambda gi, qi: (gi, 0, qi)),
                  pl.BlockSpec((1, BF16_ROWS, lanes), per_g),
                  pl.BlockSpec((1, 3 * tq, lanes), per_g, pipeline_mode=resident),
                  pl.BlockSpec((1, 4 * tq, lanes), per_g, pipeline_mode=resident),
                  pl.BlockSpec((1, 1, hd, lanes), per_q),
                  pl.BlockSpec((1, 1, 8, lanes), per_q)],
        out_specs=pl.BlockSpec((1, lanes // tq, hd, tq), lambda gi, qi: (gi, 0, 0, qi)),
        out_shape=jax.ShapeDtypeStruct((g, lanes // tq, hd, s), BF16),
        scratch_shapes=[pltpu.VMEM((AUG_K, lanes), BF16), pltpu.VMEM((AUG_K, lanes), BF16),
                        pltpu.VMEM((nslc, lanes), BF16),
                        pltpu.VMEM((1, lanes), F32), pltpu.VMEM((AUG_V, lanes), F32),
                        pltpu.VMEM((1, lanes), F32), pltpu.VMEM((AUG_V, lanes), F32),
                        pltpu.VMEM((tq, lanes), F32), pltpu.VMEM((tq, lanes), F32)],
        compiler_params=_cparams(("arbitrary", "arbitrary")),
        name="nsa_select_window",
    )(ch, k_tok, k_extra, ch, k_tok, ch, selt, far3, b_sel, b_win, oct_, gates_t)


def _decay_kernel(z_ref, b_ref, place_ref, o_ref, end_ref, carry_ref, *, tb):
    @pl.when(pl.program_id(0) == 0)
    def _():
        carry_ref[...] = jnp.zeros(carry_ref.shape, F32)

    z = z_ref[...] + b_ref[...]
    log_f = jnp.minimum(z, 0.0) - jnp.log1p(jnp.exp(-jnp.abs(z)))
    r = lax.broadcasted_iota(jnp.int32, (tb, tb), 0)
    c = lax.broadcasted_iota(jnp.int32, (tb, tb), 1)
    tri = jnp.where(r >= c, 1.0, 0.0).astype(BF16)
    run = _dot3_rhs(tri, log_f) + carry_ref[...]
    carry_ref[...] = run[tb - 1:tb, :]
    val = -run * LOG2E
    hi, mid, lo = _split3_exact(val)
    o_ref[...] = (_dot(hi, place_ref[0]) + _dot(mid, place_ref[1]) + _dot(lo, place_ref[2])).astype(BF16)
    end_ref[...] = jnp.broadcast_to(val[tb - 1:tb, :], end_ref.shape)


def _decay_pieces(z, bias, first_lane, n_heads):
    s, n = z.shape
    tb = FOX_TK
    place = np.zeros((3, n, n_heads * AUG_K), np.float32)
    for h in range(n_heads):
        for piece in range(3):
            place[piece, first_lane + h, h * AUG_K + HEAD_DIM + piece] = 1.0
    pieces, ends = pl.pallas_call(
        functools.partial(_decay_kernel, tb=tb),
        grid=(s // tb,),
        in_specs=[pl.BlockSpec((tb, n), lambda i: (i, 0)),
                  pl.BlockSpec((1, n), lambda i: (0, 0)),
                  pl.BlockSpec((3, n, n_heads * AUG_K), lambda i: (0, 0, 0))],
        out_specs=[pl.BlockSpec((tb, n_heads * AUG_K), lambda i: (i, 0)),
                   pl.BlockSpec((8, n), lambda i: (i, 0))],
        out_shape=[jax.ShapeDtypeStruct((s, n_heads * AUG_K), BF16),
                   jax.ShapeDtypeStruct((s // tb * 8, n), F32)],
        scratch_shapes=[pltpu.VMEM((1, n), F32)],
        compiler_params=_cparams(("arbitrary",)),
        name="decay_cumsum",
    )(z, bias, jnp.asarray(place, BF16))
    return pieces, ends[::8, first_lane:first_lane + n_heads].T


def _ones_rows(width):
    return jnp.where(lax.broadcasted_iota(jnp.int32, (BF16_ROWS, width), 0) < 8, 1.0, 0.0).astype(BF16)


def _fox_kernel(ends_ref, qt_ref, k_ref, dk_ref, vt_ref, o_ref, rhs, m_ref, acc_ref, s_even, s_odd, kmax_ref):
    tq = FOX_TQ
    tk = FOX_TK
    assert tq == 2 * tk
    qb = pl.program_id(1)
    row = lax.broadcasted_iota(jnp.int32, (AUG_K - HEAD_DIM, tq), 0)
    rhs[0:HEAD_DIM, :] = qt_ref[0]
    rhs[HEAD_DIM:AUG_K, :] = jnp.where(row < 3, 1.0, 0.0).astype(BF16)
    ones_rows = _ones_rows(tk)
    _flash_init_t(m_ref, acc_ref)

    def scores(kt, s_ref):
        k0 = pl.multiple_of(kt * tk, tk)
        k_aug = k_ref[pl.ds(k0, tk), :] + dk_ref[pl.ds(k0, tk), :]
        s_ref[...] = _dot(k_aug, rhs[...])

    def consume(kt, s_ref, diagonal):
        k0 = pl.multiple_of(kt * tk, tk)
        s = s_ref[...]
        if diagonal:
            key = k0 + lax.broadcasted_iota(jnp.int32, (tk, tq), 0)
            qry = qb * tq + lax.broadcasted_iota(jnp.int32, (tk, tq), 1)
            s = jnp.where(key <= qry, s, NEG_INF)
        vt_aug = jnp.concatenate([vt_ref[0, :, pl.ds(k0, tk)], ones_rows], axis=0)
        _flash_step_t(s, vt_aug, m_ref, acc_ref)

    @pl.when(qb == 0)
    def _():
        ones = jnp.ones((AUG_K, AUG_K), BF16)

        def norm_tile(c, best):
            k = k_ref[pl.ds(pl.multiple_of(c * tk, tk), tk), :].astype(F32)
            return jnp.maximum(best, _dot((k * k).astype(BF16), ones))

        best = lax.fori_loop(0, k_ref.shape[0] // tk, norm_tile, jnp.zeros((tk, AUG_K), F32))
        kmax_ref[0] = jnp.max(jnp.sqrt(best * FOX_BOUND_SLACK))

    scores(2 * qb, s_even)
    scores(2 * qb + 1, s_odd)
    consume(2 * qb, s_even, True)
    consume(2 * qb + 1, s_odd, True)

    q = qt_ref[0].astype(F32)
    q_norm = jnp.max(jnp.sqrt(jnp.sum(q * q, axis=0, keepdims=True) * FOX_BOUND_SLACK))
    threshold = jnp.min(m_ref[...]) - FOX_SKIP_MARGIN - q_norm * kmax_ref[0]
    head = pl.program_id(0)

    def first_needed(j, first):
        return jnp.where(ends_ref[head, 2 * j + 1] >= threshold, jnp.minimum(first, j), first)

    j0 = lax.fori_loop(0, qb, first_needed, qb)

    scores(2 * j0, s_even)

    def pair_body(j, carry):
        scores(2 * j + 1, s_odd)
        consume(2 * j, s_even, False)
        scores(2 * j + 2, s_even)
        consume(2 * j + 1, s_odd, False)
        return carry

    lax.fori_loop(j0, qb, pair_body, 0)
    o_ref[0] = _flash_result_t(acc_ref).astype(o_ref.dtype)


def _fox_attention(ch, q_block0, v_block0, k_tok, k_block0, decay_k, decay_ends):
    _, hd, s = ch.shape
    h = decay_ends.shape[0]
    tq = FOX_TQ
    grid_spec = pltpu.PrefetchScalarGridSpec(
        num_scalar_prefetch=1,
        grid=(h, s // tq),
        in_specs=[pl.BlockSpec((1, hd, tq), lambda hi, qi, ends: (q_block0 + hi, 0, qi)),
                  pl.BlockSpec((s, AUG_K), lambda hi, qi, ends: (0, k_block0 + hi)),
                  pl.BlockSpec((s, AUG_K), lambda hi, qi, ends: (0, hi)),
                  pl.BlockSpec((1, hd, s), lambda hi, qi, ends: (v_block0 + hi, 0, 0))],
        out_specs=pl.BlockSpec((1, hd, tq), lambda hi, qi, ends: (hi, 0, qi)),
        scratch_shapes=[pltpu.VMEM((AUG_K, tq), BF16),
                        pltpu.VMEM((1, tq), F32), pltpu.VMEM((AUG_V, tq), F32),
                        pltpu.VMEM((FOX_TK, tq), F32), pltpu.VMEM((FOX_TK, tq), F32),
                        pltpu.SMEM((1,), F32)],
    )
    return pl.pallas_call(
        _fox_kernel,
        grid_spec=grid_spec,
        out_shape=jax.ShapeDtypeStruct((h, hd, s), BF16),
        compiler_params=_cparams(("arbitrary", "arbitrary")),
        name="fox_attention",
    )(decay_ends, ch, k_tok, decay_k, ch)


def _merge_kernel(on_ref, of_ref, mg_ref, x_ref, wn_ref, wf_ref, wo_ref, g1_ref, lg_ref, lb_ref,
                  sc2_ref, sh2_ref, wr_ref, br_ref, o_ref, u_ref, r_ref, *, alpha):
    d = x_ref.shape[-1]
    tn = (((0,), (0,)), ((), ()))
    a = lax.dot_general(on_ref[...], wn_ref[...], tn, preferred_element_type=F32)
    b = lax.dot_general(of_ref[...], wf_ref[...], tn, preferred_element_type=F32)
    gm = _sigmoid(mg_ref[...].astype(F32))
    merged = gm[:, 0:d] * a + gm[:, d:2 * d] * b
    y = _dot(merged.astype(BF16), wo_ref[...])
    z = alpha * x_ref[...] + (1.0 + g1_ref[...]) * y
    x1 = _layer_norm(z) * lg_ref[...] + lb_ref[...]
    o_ref[...] = x1
    _route_tile(x1, sc2_ref, sh2_ref, wr_ref, br_ref, u_ref, r_ref)


def _merge_project(o_nsa, o_fox, merge, x, wn, wf, wo, g1, ln_g, ln_b, sc2, sh2, w_r, b_r, alpha, tm=256):
    m, d = x.shape
    w = o_nsa.shape[0]
    resident = pl.Buffered(1)
    row = lambda i: (i, 0)
    fixed = lambda i: (0, 0)
    return pl.pallas_call(
        functools.partial(_merge_kernel, alpha=alpha),
        grid=(m // tm,),
        in_specs=[pl.BlockSpec((w, tm), lambda i: (0, i)), pl.BlockSpec((w, tm), lambda i: (0, i)),
                  pl.BlockSpec((tm, 2 * d), row), pl.BlockSpec((tm, d), row),
                  pl.BlockSpec((w, d), fixed, pipeline_mode=resident),
                  pl.BlockSpec((w, d), fixed, pipeline_mode=resident),
                  pl.BlockSpec((d, d), fixed, pipeline_mode=resident),
                  pl.BlockSpec((1, d), fixed), pl.BlockSpec((1, d), fixed), pl.BlockSpec((1, d), fixed),
                  pl.BlockSpec((1, d), fixed), pl.BlockSpec((1, d), fixed),
                  pl.BlockSpec((2, d, LANES), lambda i: (0, 0, 0)), pl.BlockSpec((1, LANES), fixed)],
        out_specs=[pl.BlockSpec((tm, d), row), pl.BlockSpec((tm, d), row), pl.BlockSpec((tm, LANES), row)],
        out_shape=[jax.ShapeDtypeStruct((m, d), F32), jax.ShapeDtypeStruct((m, d), F32),
                   jax.ShapeDtypeStruct((m, LANES), F32)],
        compiler_params=_cparams(("parallel",)),
        name="merge_project_ln_route",
    )(o_nsa, o_fox, merge, x, wn, wf, wo, g1, ln_g, ln_b, sc2, sh2, w_r, b_r)


def _route_tile(x1, sc_ref, sh_ref, w_ref, b_ref, u_ref, r_ref):
    u = _layer_norm(x1) * (1.0 + sc_ref[...]) + sh_ref[...]
    u_ref[...] = u
    u_hi = u.astype(BF16)
    u_lo = (u - u_hi.astype(F32)).astype(BF16)
    logits = _dot(u_hi, w_ref[0]) + _dot(u_lo, w_ref[0]) + _dot(u_hi, w_ref[1]) + b_ref[...]
    lane = lax.broadcasted_iota(jnp.int32, (1, LANES), 1).astype(F32)
    none = float(LANES)
    is_g = lane < N_GROUPS
    lg = jnp.where(is_g, logits, NEG_INF)
    eg = jnp.exp(lg - jnp.max(lg, axis=-1, keepdims=True))
    pg = eg / jnp.sum(eg, axis=-1, keepdims=True)
    p_grp = jnp.max(pg, axis=-1, keepdims=True)
    grp = jnp.min(jnp.where(pg == p_grp, lane, none), axis=-1, keepdims=True)
    lo = N_GROUPS + grp * EXPERTS_PER_GROUP
    is_e = (lane >= lo) & (lane < lo + EXPERTS_PER_GROUP)
    le = jnp.where(is_e, logits, NEG_INF)
    ee = jnp.exp(le - jnp.max(le, axis=-1, keepdims=True))
    pe = jnp.where(is_e, ee / jnp.sum(ee, axis=-1, keepdims=True), -1.0)
    p1 = jnp.max(pe, axis=-1, keepdims=True)
    i1 = jnp.min(jnp.where(pe == p1, lane, none), axis=-1, keepdims=True)
    pe2 = jnp.where(lane == i1, -1.0, pe)
    p2 = jnp.max(pe2, axis=-1, keepdims=True)
    i2 = jnp.min(jnp.where(pe2 == p2, lane, none), axis=-1, keepdims=True)
    den = p1 + p2
    r_ref[...] = jnp.where(lane == 0, i1 - N_GROUPS,
                           jnp.where(lane == 1, i2 - N_GROUPS,
                                     jnp.where(lane == 2, p_grp * p1 / den,
                                               jnp.where(lane == 3, p_grp * p2 / den, 0.0))))


def _moe_kernel(be_ref, nu_ref, tok_ref, tok_next_ref, dst_ref, rw_ref, u_hbm, wg_ref, wu_ref, wd_ref, out_hbm,
                xbuf, ybuf, wgb, wub, wdb, sem_in, sem_out, *, n_dump0):
    rb = ROW_BLOCK
    i = pl.program_id(0)
    last = nu_ref[0] - 1
    slot = i % 2

    def row_in(r, tok, sl):
        return pltpu.make_async_copy(u_hbm.at[pl.ds(tok, 1), :], xbuf.at[sl, pl.ds(r, 1), :], sem_in.at[sl])

    def row_out(r, dst):
        return pltpu.make_async_copy(ybuf.at[pl.ds(r, 1), :], out_hbm.at[pl.ds(dst, 1), :], sem_out)

    @pl.when(i == 0)
    def _():
        for r in range(rb):
            row_in(r, tok_ref[0, 0, r], 0).start()
        ybuf[...] = jnp.zeros(ybuf.shape, F32)
        pltpu.make_async_copy(ybuf, out_hbm.at[pl.ds(n_dump0, rb), :], sem_out).start()

    prev = be_ref[jnp.maximum(i - 1, 0)]

    @pl.when((i <= last) & ((i == 0) | (be_ref[i] != prev)))
    def _():
        wgb[...] = wg_ref[0].astype(BF16)
        wub[...] = wu_ref[0].astype(BF16)
        wdb[...] = wd_ref[0].astype(BF16)

    @pl.when(i <= last)
    def _():
        for r in range(rb):
            row_in(r, 0, slot).wait()
        xb = xbuf[slot].astype(BF16)
        for r in range(rb):
            row_in(r, tok_next_ref[0, 0, r], 1 - slot).start()
        gate = _dot(xb, wgb[...])
        up = _dot(xb, wub[...])
        hid = (gate * _sigmoid(gate)) * up
        y = _dot(hid.astype(BF16), wdb[...]) * rw_ref[0]
        for r in range(rb):
            row_out(r, 0).wait()
        ybuf[...] = y
        for r in range(rb):
            row_out(r, dst_ref[0, 0, r]).start(priority=r % 2)

    @pl.when(i == last)
    def _():
        for r in range(rb):
            row_out(r, 0).wait()
        for r in range(rb):
            row_in(r, 0, 1 - slot).wait()


def _moe_experts(u, blk_exp, n_used, row_tok, row_dst, row_w, w_gate, w_up, w_down):
    t, d = u.shape
    n_blocks = blk_exp.shape[0]
    de = w_gate.shape[-1]
    rb = ROW_BLOCK
    tok3 = row_tok.reshape(n_blocks, 1, rb)
    grid_spec = pltpu.PrefetchScalarGridSpec(
        num_scalar_prefetch=2,
        grid=(n_blocks,),
        in_specs=[pl.BlockSpec((1, 1, rb), lambda i, be, nu: (i, 0, 0), memory_space=pltpu.SMEM),
                  pl.BlockSpec((1, 1, rb), lambda i, be, nu: (jnp.minimum(i + 1, n_blocks - 1), 0, 0),
                               memory_space=pltpu.SMEM),
                  pl.BlockSpec((1, 1, rb), lambda i, be, nu: (i, 0, 0), memory_space=pltpu.SMEM),
                  pl.BlockSpec((1, rb, 1), lambda i, be, nu: (i, 0, 0)),
                  pl.BlockSpec(memory_space=pl.ANY),
                  pl.BlockSpec((1, d, de), lambda i, be, nu: (be[i], 0, 0)),
                  pl.BlockSpec((1, d, de), lambda i, be, nu: (be[i], 0, 0)),
                  pl.BlockSpec((1, de, d), lambda i, be, nu: (be[i], 0, 0))],
        out_specs=pl.BlockSpec(memory_space=pl.ANY),
        scratch_shapes=[pltpu.VMEM((2, rb, d), F32), pltpu.VMEM((rb, d), F32),
                        pltpu.VMEM((d, de), BF16), pltpu.VMEM((d, de), BF16), pltpu.VMEM((de, d), BF16),
                        pltpu.SemaphoreType.DMA((2,)), pltpu.SemaphoreType.DMA(())],
    )
    return pl.pallas_call(
        functools.partial(_moe_kernel, n_dump0=2 * t),
        grid_spec=grid_spec,
        out_shape=jax.ShapeDtypeStruct((2 * t + rb, d), F32),
        compiler_params=_cparams(("arbitrary",)),
        name="moe_experts",
    )(blk_exp, n_used, tok3, tok3, row_dst.reshape(n_blocks, 1, rb), row_w.reshape(n_blocks, rb, 1),
      u, w_gate, w_up, w_down)


def _moe_dispatch(route, t):
    k = 2
    eid = route[:, 0:k].astype(jnp.int32).reshape(-1)
    wts = route[:, k:2 * k].reshape(-1)
    n_asg = t * k
    n_rows = n_asg + N_EXPERTS * ROW_BLOCK
    n_blocks = n_rows // ROW_BLOCK
    onehot = (eid[:, None] == jnp.arange(N_EXPERTS, dtype=jnp.int32)[None, :]).astype(jnp.int32)
    rank = jnp.sum((jnp.cumsum(onehot, axis=0) - onehot) * onehot, axis=1)
    counts = jnp.sum(onehot, axis=0)
    padded = (counts + ROW_BLOCK - 1) // ROW_BLOCK * ROW_BLOCK
    pad_end = jnp.cumsum(padded)
    pad_start = pad_end - padded
    dest = jnp.sum(onehot * pad_start[None, :], axis=1) + rank
    asg = jnp.arange(n_asg, dtype=jnp.int32)
    upd = jnp.stack([(asg % k) * t + asg // k, lax.bitcast_convert_type(wts, jnp.int32)], axis=1)
    init = jnp.stack([n_asg + jnp.arange(n_rows, dtype=jnp.int32) % ROW_BLOCK,
                      jnp.zeros((n_rows,), jnp.int32)], axis=1)
    rows = init.at[dest].set(upd)
    row_dst = rows[:, 0]
    row_w = lax.bitcast_convert_type(rows[:, 1], F32)
    row_tok = jnp.where(row_dst < n_asg, row_dst % t, 0)
    blk_start = jnp.arange(n_blocks, dtype=jnp.int32) * ROW_BLOCK
    blk_exp = jnp.minimum(jnp.sum((pad_end[None, :] <= blk_start[:, None]).astype(jnp.int32), axis=1),
                          N_EXPERTS - 1)
    n_used = (pad_end[N_EXPERTS - 1:] // ROW_BLOCK).astype(jnp.int32)
    return blk_exp, n_used, row_tok, row_dst, row_w


def _final_kernel(x_ref, y0_ref, y1_ref, g2_ref, lg_ref, lb_ref, o_ref, *, alpha):
    z = alpha * x_ref[...] + (1.0 + g2_ref[...]) * (y0_ref[...] + y1_ref[...])
    o_ref[...] = _layer_norm(z) * lg_ref[...] + lb_ref[...]


def _final_ln(x1, y2, g2, ln_g, ln_b, alpha, tm=512):
    m, d = x1.shape
    nb = m // tm
    fixed = lambda i: (0, 0)
    return pl.pallas_call(
        functools.partial(_final_kernel, alpha=alpha),
        grid=(nb,),
        in_specs=[pl.BlockSpec((tm, d), lambda i: (i, 0)),
                  pl.BlockSpec((tm, d), lambda i: (i, 0)),
                  pl.BlockSpec((tm, d), lambda i: (i + nb, 0)),
                  pl.BlockSpec((1, d), fixed), pl.BlockSpec((1, d), fixed), pl.BlockSpec((1, d), fixed)],
        out_specs=pl.BlockSpec((tm, d), lambda i: (i, 0)),
        out_shape=jax.ShapeDtypeStruct((m, d), F32),
        compiler_params=_cparams(("parallel",)),
        name="final_ln",
    )(x1, y2, y2, g2, ln_g, ln_b)


def _to_lane_blocks(a, tq):
    g, hpg, s, c = a.shape
    return a.reshape(g, hpg, s // tq, tq, c).transpose(0, 2, 4, 1, 3).reshape(g, s // tq, c, hpg * tq)


def _layer(x2d, c, w_ada, b_ada, w_in, b_fgt, t5_table, cmp_pe, cmp_w1, cmp_b1, cmp_w2,
           w_br_nsa, w_br_fox, w_o, ln1_g, ln1_b, w_rg, b_rg, w_re, b_re,
           w_gate, w_up, w_down, ln2_g, ln2_b, alpha):
    s, d = x2d.shape
    hd = HEAD_DIM
    g = NSA_GROUPS
    mod = _ada_mod(c, w_ada, b_ada)
    sh1, sc1, g1, sh2, sc2, g2 = [mod[:, i * d:(i + 1) * d] for i in range(6)]

    c_q = NSA_HEADS * hd
    c_kv = 6 * g * hd
    c_gate = 3 * NSA_HEADS
    c_fox = 3 * FOX_HEADS * hd
    off_kv = c_q
    off_gate = off_kv + c_kv
    off_fox = off_gate + c_gate
    off_fgt = off_fox + c_fox
    off_merge = off_fgt + FOX_HEADS
    qscale = hd ** -0.5 * LOG2E
    nh = FOX_HEADS
    gw = g * hd

    def kv_cols(z):
        return w_in[:, off_kv + z * gw:off_kv + (z + 1) * gw]

    def lane_padded(w, heads):
        return jnp.pad(w.reshape(d, heads, hd), ((0, 0), (0, 0), (0, AUG_K - hd))).reshape(d, heads * AUG_K)

    fox_q, fox_k, fox_v = [w_in[:, off_fox + i * nh * hd:off_fox + (i + 1) * nh * hd] for i in range(3)]
    w_ch = jnp.concatenate([w_in[:, 0:off_kv] * qscale, kv_cols(3), kv_cols(5), fox_q * qscale, fox_v],
                           axis=1).astype(BF16).T
    w_tok = jnp.concatenate([kv_cols(0), kv_cols(1), lane_padded(kv_cols(2), g), lane_padded(kv_cols(4), g),
                             lane_padded(fox_k, nh)], axis=1).astype(BF16)
    n_small = c_gate + FOX_HEADS
    w_small = jnp.concatenate([w_in[:, off_gate:off_fox], w_in[:, off_fgt:off_merge],
                               jnp.zeros((d, LANES - n_small), F32)], axis=1).astype(BF16)
    w_merge = w_in[:, off_merge:].astype(BF16)

    u = _ln_mod(x2d, sc1, sh1, BF16, tm=PROJ_TM)
    ch = _matmul_nt(w_ch, u, BF16, PROJ_TM, "in_proj_channel_major")
    tok = _matmul(u, w_tok, BF16, PROJ_TM, w_tok.shape[1], "in_proj_token_major")
    small = _matmul(u, w_small, F32, PROJ_TM, LANES, "in_proj_small")
    merge = _matmul(u, w_merge, BF16, PROJ_TM, 2048, "in_proj_merge")

    ch = ch.reshape(-1, hd, s)
    ch_vs_block0 = NSA_HEADS
    ch_vw_block0 = ch_vs_block0 + g
    ch_fq_block0 = ch_vw_block0 + g
    ch_fv_block0 = ch_fq_block0 + nh
    tok_ks_block0 = 2 * gw // AUG_K
    tok_kw_block0 = tok_ks_block0 + g
    tok_fox_block0 = tok_kw_block0 + g

    kv_cmp_in = tok[:, 0:2 * gw].reshape(s, 2, g, hd).transpose(1, 2, 0, 3)
    kv_cmp = _compress(kv_cmp_in, cmp_pe, cmp_w1, cmp_b1, cmp_w2)
    nch = s // CMP_STRIDE
    ncp = nch + LANES
    nslc = s // SLC_LEN
    kv_cmp_pad = jnp.pad(kv_cmp.astype(BF16), ((0, 0), (0, 0), (CMP_PAD, ncp - nch - CMP_PAD), (0, 0)))
    b_cmp, b_sel, b_win, far3 = _nsa_bias_tables(t5_table)

    row_ix = np.arange(ncp)
    row_ok = (row_ix >= CMP_PAD) & (row_ix < nch - 1 + CMP_PAD)
    row_cols = np.zeros((ncp, 2 * AUG_K - hd), np.float32)
    row_cols[:, ROW_BIAS - hd:ROW_BIAS - hd + 3] = row_ok[:, None]
    row_cols[:, ROW_BIAS - hd + 3] = ~row_ok
    row_cols[row_ix, AUG_K - hd + row_ix // BF16_ROWS] = 1.0
    kc_aug = jnp.concatenate([kv_cmp_pad[0], jnp.broadcast_to(jnp.asarray(row_cols, BF16), (g,) + row_cols.shape)],
                             axis=-1)
    vct = kv_cmp_pad[1].transpose(0, 2, 1)
    oct_, selt = _nsa_compress_select(ch, kc_aug, vct, b_cmp, far3)

    period = BF16_ROWS * SLC_LEN
    k_extra = np.zeros((period, AUG_K), np.float32)
    k_extra[np.arange(period), ROW_MASK + np.arange(period) // SLC_LEN] = 1.0
    k_extra[:, ROW_BIAS:ROW_BIAS + 3] = 1.0
    gates = small[:, 0:c_gate].reshape(s, g, NSA_HPG, 3).transpose(1, 2, 0, 3)
    gates_t = jnp.pad(_to_lane_blocks(gates, ATT_TQ), ((0, 0), (0, 0), (0, 5), (0, 0)))
    o_nsa_t = _nsa_attention(tok, tok_ks_block0, tok_kw_block0, jnp.asarray(k_extra, BF16), ch, ch_vs_block0,
                             ch_vw_block0, selt, far3, b_sel, b_win, oct_, gates_t)
    o_nsa = o_nsa_t.reshape(MIX_W, s)

    fgt_bias = jnp.concatenate([jnp.zeros((c_gate,), F32), b_fgt, jnp.zeros((LANES - n_small,), F32)])[None, :]
    decay_k, decay_ends = _decay_pieces(small, fgt_bias, c_gate, nh)
    o_fox = _fox_attention(ch, ch_fq_block0, ch_fv_block0, tok, tok_fox_block0, decay_k, decay_ends)
    o_fox = o_fox.reshape(MIX_W, s)

    n_r = N_GROUPS + N_EXPERTS
    w_r = jnp.concatenate([w_rg, w_re.reshape(d, N_EXPERTS), jnp.zeros((d, LANES - n_r), F32)], axis=1)
    b_r = jnp.concatenate([b_rg, b_re.reshape(N_EXPERTS), jnp.zeros((LANES - n_r,), F32)])[None, :]
    w_r_hi, w_r_lo, _ = _split3_exact(w_r)
    x1, u2, route = _merge_project(o_nsa, o_fox, merge, x2d, w_br_nsa.astype(BF16), w_br_fox.astype(BF16),
                                   w_o.astype(BF16), g1, ln1_g[None, :], ln1_b[None, :], sc2, sh2,
                                   jnp.stack([w_r_hi, w_r_lo]), b_r, alpha)

    blk_exp, n_used, row_tok, row_dst, row_w = _moe_dispatch(route, s)
    y2 = _moe_experts(u2, blk_exp, n_used, row_tok, row_dst, row_w, w_gate, w_up, w_down)
    return _final_ln(x1, y2, g2, ln2_g[None, :], ln2_b[None, :], alpha)


def kernel(x, c, w_ada, b_ada, w_in, b_fgt, t5_table, cmp_pe, cmp_w1, cmp_b1, cmp_w2, w_br_nsa, w_br_fox, w_o,
           ln1_g, ln1_b, w_rg, b_rg, w_re, b_re, w_gate, w_up, w_down, ln2_g, ln2_b):
    b, s, d = x.shape
    depth = w_ada.shape[0]
    assert b == 1
    alpha = (2 * depth) ** 0.25
    h = x[0]
    for l in range(depth):
        h = _layer(h, c, w_ada[l], b_ada[l], w_in[l], b_fgt[l], t5_table, cmp_pe[l], cmp_w1[l], cmp_b1[l],
                   cmp_w2[l], w_br_nsa[l], w_br_fox[l], w_o[l], ln1_g[l], ln1_b[l], w_rg[l], b_rg[l],
                   w_re[l], b_re[l], w_gate[l], w_up[l], w_down[l], ln2_g[l], ln2_b[l], alpha)
    return h[None]
```

```python
import functools
import math

import numpy as np
import jax
import jax.numpy as jnp
from jax import lax
from jax.experimental import pallas as pl
from jax.experimental.pallas import tpu as pltpu

F32 = jnp.float32
BF16 = jnp.bfloat16
HIGHEST = lax.Precision.HIGHEST
LOG2E = math.log2(math.e)

HEAD_DIM = 64
NSA_HEADS = 8
NSA_GROUPS = 2
NSA_HPG = NSA_HEADS // NSA_GROUPS
FOX_HEADS = 8
MIX_W = NSA_HEADS * HEAD_DIM
CMP_LEN = 32
CMP_STRIDE = 16
SLC_LEN = 64
SLC_TOPK = 16
WINDOW = 512
T5_BUCKETS = 32
T5_MAX_EXACT = 16
T5_MAX_DIST = 128
N_GROUPS = 8
EXPERTS_PER_GROUP = 8
N_EXPERTS = N_GROUPS * EXPERTS_PER_GROUP
ROW_BLOCK = 128
LN_EPS = 1e-5
NEG_INF = -1e30
M_INIT = -1e29
FORCE_SCORE = 1e4

LANES = 128
BF16_ROWS = 16
CMP_PAD = 8
ATT_TQ = 256
CMP_NEAR_ROWS = 32
FOX_TQ = 1024
FOX_TK = 512
FOX_SKIP_MARGIN = 160.0
FOX_BOUND_SLACK = 1.02
PROJ_TM = 1024
AUG_K = 128
AUG_V = HEAD_DIM + 16
ROW_MASK = HEAD_DIM
ROW_BIAS = HEAD_DIM + 16
VMEM_LIMIT = 56 * 1024 * 1024


def _cparams(sem, vmem=VMEM_LIMIT):
    return pltpu.CompilerParams(dimension_semantics=sem, vmem_limit_bytes=vmem)


def _sigmoid(x):
    return 1.0 / (1.0 + jnp.exp(-x))


def _layer_norm(x):
    mu = jnp.mean(x, axis=-1, keepdims=True)
    xc = x - mu
    var = jnp.mean(xc * xc, axis=-1, keepdims=True)
    return xc * lax.rsqrt(var + LN_EPS)


def _split3(x):
    hi = x.astype(BF16)
    r1 = x - hi.astype(F32)
    mid = r1.astype(BF16)
    lo = (r1 - mid.astype(F32)).astype(BF16)
    return hi, mid, lo


def _split3_exact(x):
    def trunc(v):
        bits = lax.bitcast_convert_type(v, jnp.uint32) & jnp.uint32(0xFFFF0000)
        return lax.bitcast_convert_type(bits, F32)
    hi = trunc(x)
    r1 = x - hi
    mid = trunc(r1)
    lo = r1 - mid
    return hi.astype(BF16), mid.astype(BF16), lo.astype(BF16)


def _dot(a, b):
    return jnp.dot(a, b, preferred_element_type=F32)


def _dot_nt(a, b):
    return lax.dot_general(a, b, (((1,), (1,)), ((), ())), preferred_element_type=F32)


def _dot3(x, w_bf16):
    hi, mid, lo = _split3(x)
    return _dot(hi, w_bf16) + _dot(mid, w_bf16) + _dot(lo, w_bf16)


def _dot3_rhs(w_bf16, x):
    hi, mid, lo = _split3(x)
    return _dot(w_bf16, hi) + _dot(w_bf16, mid) + _dot(w_bf16, lo)


def _ada_kernel(c_ref, w_ref, b_ref, o_ref):
    c = c_ref[...]
    a = c * _sigmoid(c)
    o_ref[...] = jnp.dot(a, w_ref[...], precision=HIGHEST, preferred_element_type=F32) + b_ref[...]


def _ada_mod(c, w, b):
    d, n = w.shape
    tn = 1024
    c8 = jnp.broadcast_to(c, (8, d))
    out = pl.pallas_call(
        _ada_kernel,
        grid=(n // tn,),
        in_specs=[pl.BlockSpec((8, d), lambda j: (0, 0)),
                  pl.BlockSpec((d, tn), lambda j: (0, j)),
                  pl.BlockSpec((1, tn), lambda j: (0, j))],
        out_specs=pl.BlockSpec((8, tn), lambda j: (0, j)),
        out_shape=jax.ShapeDtypeStruct((8, n), F32),
        compiler_params=_cparams(("parallel",)),
        name="ada_mod",
    )(c8, w, b.reshape(1, n))
    return out[0:1]


def _lnmod_kernel(x_ref, sc_ref, sh_ref, o_ref):
    y = _layer_norm(x_ref[...])
    o_ref[...] = (y * (1.0 + sc_ref[...]) + sh_ref[...]).astype(o_ref.dtype)


def _ln_mod(x, sc, sh, out_dtype, tm=512):
    m, d = x.shape
    return pl.pallas_call(
        _lnmod_kernel,
        grid=(m // tm,),
        in_specs=[pl.BlockSpec((tm, d), lambda i: (i, 0)),
                  pl.BlockSpec((1, d), lambda i: (0, 0)),
                  pl.BlockSpec((1, d), lambda i: (0, 0))],
        out_specs=pl.BlockSpec((tm, d), lambda i: (i, 0)),
        out_shape=jax.ShapeDtypeStruct((m, d), out_dtype),
        compiler_params=_cparams(("parallel",)),
        name="ln_mod",
    )(x, sc, sh)


def _mm_kernel(a_ref, w_ref, o_ref):
    o_ref[...] = _dot(a_ref[...], w_ref[...]).astype(o_ref.dtype)


def _matmul(a, w, out_dtype, tm, tn, name):
    m, k = a.shape
    n = w.shape[1]
    return pl.pallas_call(
        _mm_kernel,
        grid=(n // tn, m // tm),
        in_specs=[pl.BlockSpec((tm, k), lambda j, i: (i, 0)),
                  pl.BlockSpec((k, tn), lambda j, i: (0, j))],
        out_specs=pl.BlockSpec((tm, tn), lambda j, i: (i, j)),
        out_shape=jax.ShapeDtypeStruct((m, n), out_dtype),
        compiler_params=_cparams(("parallel", "parallel")),
        name=name,
    )(a, w)


def _mm_nt_kernel(w_ref, a_ref, o_ref):
    o_ref[...] = _dot_nt(w_ref[...], a_ref[...]).astype(o_ref.dtype)


def _matmul_nt(w_t, a, out_dtype, tm, name):
    n, k = w_t.shape
    m = a.shape[0]
    return pl.pallas_call(
        _mm_nt_kernel,
        grid=(m // tm,),
        in_specs=[pl.BlockSpec((n, k), lambda i: (0, 0), pipeline_mode=pl.Buffered(1)),
                  pl.BlockSpec((tm, k), lambda i: (i, 0))],
        out_specs=pl.BlockSpec((n, tm), lambda i: (0, i)),
        out_shape=jax.ShapeDtypeStruct((n, m), out_dtype),
        compiler_params=_cparams(("parallel",)),
        name=name,
    )(w_t, a)


def _gelu_tanh(x):
    return 0.5 * x * (1.0 + jnp.tanh(math.sqrt(2.0 / math.pi) * (x + 0.044715 * (x * x * x))))


def _compress_kernel(c_ref, pe_ref, w1a_ref, w1b_ref, b1_ref, w2_ref, o_ref, *, nch):
    c = c_ref[0, 0]
    w1a = w1a_ref[0]
    w1b = w1b_ref[0]
    half = CMP_STRIDE * HEAD_DIM
    a = _dot(c, w1a)
    b = _dot(c, w1b)
    b_next = pltpu.roll(b, shift=nch - 1, axis=0)
    pe = pe_ref[0]
    pb = _dot(pe[:, :half], w1a) + _dot(pe[:, half:], w1b)
    hid = _gelu_tanh(a + b_next + pb[0:1, :] + b1_ref[0])
    o_ref[0, 0] = _dot(hid.astype(BF16), w2_ref[0])


def _compress(kv_cmp, pe, w1, b1, w2):
    z, g, s, hd = kv_cmp.shape
    nch = s // CMP_STRIDE
    half = CMP_STRIDE * hd
    chunks = kv_cmp.reshape(z, g, nch, half)
    pe8 = jnp.broadcast_to(pe.reshape(z, 1, CMP_LEN * hd), (z, 8, CMP_LEN * hd)).astype(BF16)
    w1b16 = w1.astype(BF16)
    hidn = w1.shape[-1]
    return pl.pallas_call(
        functools.partial(_compress_kernel, nch=nch),
        grid=(z, g),
        in_specs=[pl.BlockSpec((1, 1, nch, half), lambda zi, gi: (zi, gi, 0, 0)),
                  pl.BlockSpec((1, 8, 2 * half), lambda zi, gi: (zi, 0, 0)),
                  pl.BlockSpec((1, half, hidn), lambda zi, gi: (zi, 0, 0)),
                  pl.BlockSpec((1, half, hidn), lambda zi, gi: (zi, 1, 0)),
                  pl.BlockSpec((1, 1, hidn), lambda zi, gi: (zi, 0, 0)),
                  pl.BlockSpec((1, hidn, hd), lambda zi, gi: (zi, 0, 0))],
        out_specs=pl.BlockSpec((1, 1, nch, hd), lambda zi, gi: (zi, gi, 0, 0)),
        out_shape=jax.ShapeDtypeStruct((z, g, nch, hd), F32),
        compiler_params=_cparams(("parallel", "parallel")),
        name="compress_kv",
    )(chunks, pe8, w1b16, w1b16, b1.reshape(z, 1, hidn), w2.astype(BF16))


def _t5_bucket_np(dist):
    n = np.maximum(dist, 0)
    ratio = np.log(np.maximum(n, T5_MAX_EXACT).astype(np.float64) / T5_MAX_EXACT)
    big = T5_MAX_EXACT + (ratio / math.log(T5_MAX_DIST / T5_MAX_EXACT)
                          * (T5_BUCKETS - T5_MAX_EXACT)).astype(np.int64)
    return np.where(n < T5_MAX_EXACT, n, np.minimum(big, T5_BUCKETS - 1)).astype(np.int32)


def _t5_lookup(tbh, dist):
    onehot = np.eye(T5_BUCKETS, dtype=np.float32)[_t5_bucket_np(dist).reshape(-1)]
    vals = jnp.einsum('ghb,nb->ghn', tbh, jnp.asarray(onehot), precision=HIGHEST)
    return vals.reshape(tbh.shape[:2] + dist.shape)


def _toeplitz_kernel(w_ref, o_ref, *, n_keys, tq, lead_zero, trail_masked):
    width = w_ref.shape[-1]
    rows = jnp.broadcast_to(w_ref[0, 0], (n_keys, width))
    rolled = pltpu.roll(rows, shift=width - (n_keys - 1), axis=1, stride=1, stride_axis=0)
    parts = [rolled[:, 0:tq]]
    if lead_zero:
        parts.insert(0, jnp.zeros((lead_zero, tq), F32))
    if trail_masked:
        parts.append(jnp.full((trail_masked, tq), NEG_INF, F32))
    o_ref[0] = jnp.concatenate(parts, axis=0)


def _att_table_t(tbh, n_keys, tq, lo, hi, minus_far, lead_zero=0, trail_masked=0):
    length = n_keys + tq - 1
    d = np.arange(length) - (tq - 1)
    valid = (d >= lo) & (d < hi)
    vals = _t5_lookup(tbh, d)
    if minus_far:
        vals = vals - tbh[:, :, T5_BUCKETS - 1:]
    w = jnp.where(jnp.asarray(valid), vals * LOG2E, NEG_INF)
    width = pl.next_power_of_2(length)
    w = jnp.pad(w, ((0, 0), (0, 0), (0, width - length)))[:, :, None, :]
    g, hpg = tbh.shape[:2]
    n_rows = lead_zero + n_keys + trail_masked
    return pl.pallas_call(
        functools.partial(_toeplitz_kernel, n_keys=n_keys, tq=tq, lead_zero=lead_zero, trail_masked=trail_masked),
        grid=(g, hpg),
        in_specs=[pl.BlockSpec((1, 1, 1, width), lambda gi, hi: (gi, hi, 0, 0))],
        out_specs=pl.BlockSpec((1, n_rows, tq), lambda gi, hi: (gi, 0, hi)),
        out_shape=jax.ShapeDtypeStruct((g, n_rows, hpg * tq), F32),
        compiler_params=_cparams(("parallel", "parallel")),
        name="t5_toeplitz",
    )(w)


def _nsa_bias_tables(t5_table):
    tbh = t5_table.T.reshape(NSA_GROUPS, NSA_HPG, T5_BUCKETS).astype(F32)
    j = np.arange(CMP_NEAR_ROWS)[:, None]
    i = np.arange(ATT_TQ)[None, :]
    dist = i - (CMP_LEN - 1) - CMP_STRIDE * (j - CMP_PAD)
    vals = (_t5_lookup(tbh, dist) - tbh[:, :, T5_BUCKETS - 1][:, :, None, None]) * LOG2E
    vals = jnp.where(jnp.asarray(dist >= 0)[None, None], vals, NEG_INF)
    b_cmp = vals.transpose(0, 2, 1, 3).reshape(NSA_GROUPS, CMP_NEAR_ROWS, NSA_HPG * ATT_TQ)
    far = tbh[:, :, T5_BUCKETS - 1] * LOG2E
    b_sel = _att_table_t(tbh, 2 * ATT_TQ, ATT_TQ, 0, 1 << 30, True, lead_zero=ATT_TQ)
    b_win = _att_table_t(tbh, 3 * ATT_TQ, ATT_TQ, 0, WINDOW, False, trail_masked=ATT_TQ)
    hi, mid, lo = _split3_exact(jnp.repeat(far, ATT_TQ, axis=1))
    far3 = jnp.stack([hi, mid, lo, jnp.full(hi.shape, NEG_INF, BF16)], axis=1)
    far3 = jnp.pad(far3, ((0, 0), (0, BF16_ROWS - 4), (0, 0)))
    return b_cmp, b_sel, b_win, far3


def _nsa_cmp_kernel(qt_ref, kc_ref, vct_ref, near_ref, far3_ref, oct_ref, selt_ref, rhs, s_scr, imp_scr, *, nslc):
    tq = ATT_TQ
    lanes = NSA_HPG * tq
    qb = pl.program_id(1)
    cpb = tq // CMP_STRIDE
    rhs[...] = jnp.zeros(rhs.shape, BF16)
    for h in range(NSA_HPG):
        rhs[0:HEAD_DIM, h * tq:(h + 1) * tq] = qt_ref[h]
    rhs[ROW_BIAS:ROW_BIAS + BF16_ROWS, :] = far3_ref[0]
    chunk = lax.broadcasted_iota(jnp.int32, (AUG_K, lanes), 0)
    rhs[AUG_K:2 * AUG_K, :] = jnp.where(chunk >= qb + CMP_NEAR_ROWS // BF16_ROWS, NEG_INF, 0.0).astype(BF16)
    r0 = pl.multiple_of(cpb * qb, BF16_ROWS)
    n_lane_blocks = tq // LANES
    ncp = s_scr.shape[0]
    imp_scr[...] = jnp.zeros(imp_scr.shape, F32)

    def attend(rows):
        s_scr[0:rows, :] = _dot(kc_ref[0, 0:rows, :], rhs[...])
        s_scr[pl.ds(r0, CMP_NEAR_ROWS), :] = s_scr[pl.ds(r0, CMP_NEAR_ROWS), :] + near_ref[0]
        s = s_scr[0:rows, :]
        m = jnp.max(s, axis=0, keepdims=True)
        e = jnp.exp2(s - m)
        l = jnp.sum(e, axis=0, keepdims=True)
        p = e * jnp.where(m > M_INIT, 1.0 / l, 0.0)
        oct_ref[0, 0] = _dot(vct_ref[0, :, 0:rows], p.astype(BF16))
        imp = p[:, 0:tq]
        for h in range(1, NSA_HPG):
            imp = imp + p[:, h * tq:(h + 1) * tq]
        for c in range(n_lane_blocks):
            imp_scr[c, 0:rows, :] = imp[:, c * LANES:(c + 1) * LANES]

    limits = sorted({min(ncp, -(-(ncp * k // 3) // LANES) * LANES) for k in (1, 2, 3)})
    lo_qb = 0
    for rows in limits:
        hi_qb = (rows - CMP_NEAR_ROWS) // cpb if rows < ncp else pl.num_programs(1) - 1

        @pl.when((qb >= lo_qb) & (qb <= hi_qb))
        def _(rows=rows):
            attend(rows)

        lo_qb = hi_qb + 1
    ratio = SLC_LEN // CMP_STRIDE

    def taps(off):
        return jnp.concatenate([imp_scr[c, pl.ds(CMP_PAD + off, nslc, stride=ratio), :]
                                for c in range(n_lane_blocks)], axis=1)

    p_slc = 0.5 * (taps(-1) + taps(ratio - 1))
    for off in range(ratio - 1):
        p_slc = p_slc + taps(off)
    blk = lax.broadcasted_iota(jnp.int32, (nslc, tq), 0)
    cur = (qb * tq + lax.broadcasted_iota(jnp.int32, (nslc, tq), 1)) // SLC_LEN
    forced = (blk == 0) | (blk == cur) | (blk == cur - 1)
    score = jnp.where(forced, FORCE_SCORE, jnp.where(blk <= cur, p_slc, -1.0))
    blk_f = blk.astype(F32)
    sel = jnp.zeros((nslc, tq), F32)
    for _ in range(min(SLC_TOPK, nslc)):
        mx = jnp.max(score, axis=0, keepdims=True)
        first = jnp.min(jnp.where(score == mx, blk_f, float(nslc)), axis=0, keepdims=True)
        hit = blk_f == first
        sel = jnp.where(hit, 1.0, sel)
        score = jnp.where(hit, -2.0, score)
    selt_ref[0] = sel.astype(BF16)


def _nsa_compress_select(ch, kc_aug, vct, b_cmp, far3):
    g = kc_aug.shape[0]
    _, hd, s = ch.shape
    ncp = kc_aug.shape[1]
    tq = ATT_TQ
    nq = s // tq
    lanes = NSA_HPG * tq
    nslc = nq * tq // SLC_LEN
    assert ncp // BF16_ROWS <= AUG_K
    assert tq // CMP_STRIDE == BF16_ROWS
    per_q = lambda gi, qi: (gi, qi, 0, 0)
    per_g = lambda gi, qi: (gi, 0, 0)
    return pl.pallas_call(
        functools.partial(_nsa_cmp_kernel, nslc=nslc),
        grid=(g, nq),
        in_specs=[pl.BlockSpec((NSA_HPG, hd, tq), lambda gi, qi: (gi, 0, qi)),
                  pl.BlockSpec((1, ncp, 2 * AUG_K), per_g),
                  pl.BlockSpec((1, hd, ncp), per_g),
                  pl.BlockSpec((1, CMP_NEAR_ROWS, lanes), per_g),
                  pl.BlockSpec((1, BF16_ROWS, lanes), per_g)],
        out_specs=[pl.BlockSpec((1, 1, hd, lanes), per_q),
                   pl.BlockSpec((1, nslc, tq), lambda gi, qi: (gi, 0, qi))],
        out_shape=[jax.ShapeDtypeStruct((g, nq, hd, lanes), F32),
                   jax.ShapeDtypeStruct((g, nslc, nq * tq), BF16)],
        scratch_shapes=[pltpu.VMEM((2 * AUG_K, lanes), BF16), pltpu.VMEM((ncp, lanes), F32),
                        pltpu.VMEM((tq // LANES, ncp, LANES), F32)],
        compiler_params=_cparams(("parallel", "parallel")),
        name="nsa_compress_select",
    )(ch, kc_aug, vct, b_cmp, far3)


def _flash_init_t(m_ref, acc_ref):
    m_ref[...] = jnp.full(m_ref.shape, M_INIT, F32)
    acc_ref[...] = jnp.zeros(acc_ref.shape, F32)


def _flash_step_t(s, vt_tile, m_ref, acc_ref):
    m_old = m_ref[...]
    m_new = jnp.maximum(m_old, jnp.max(s, axis=0, keepdims=True))
    p = jnp.exp2(s - m_new).astype(BF16)
    acc_ref[...] = jnp.exp2(m_old - m_new) * acc_ref[...] + _dot(vt_tile, p)
    m_ref[...] = m_new


def _flash_result_t(acc_ref):
    acc = acc_ref[...]
    return acc[0:HEAD_DIM, :] / acc[HEAD_DIM:HEAD_DIM + 1, :]


def _nsa_att_kernel(qt_ref, ks_ref, kx_ref, vst_ref, kw_ref, vwt_ref, selt_ref, far3_ref, bsel_ref, bwin_ref,
                    oct_ref, gate_ref, o_ref, rhs_s, rhs_w, mask_t, ms, accs, mw, accw, s_even, s_odd):
    tq = ATT_TQ
    qb = pl.program_id(1)
    ones_rows = _ones_rows(tq)
    kx_tiles = kx_ref.shape[0] // tq
    qt = jnp.concatenate([qt_ref[h] for h in range(NSA_HPG)], axis=1)
    rhs_s[...] = jnp.zeros(rhs_s.shape, BF16)
    rhs_s[0:HEAD_DIM, :] = qt
    rhs_s[ROW_BIAS:ROW_BIAS + BF16_ROWS, :] = far3_ref[0]
    rhs_w[...] = jnp.zeros(rhs_w.shape, BF16)
    rhs_w[0:HEAD_DIM, :] = qt
    madd = ((selt_ref[0].astype(F32) - 1.0) * (-NEG_INF)).astype(BF16)
    mask_t[...] = jnp.concatenate([madd] * NSA_HPG, axis=1)
    _flash_init_t(ms, accs)
    _flash_init_t(mw, accw)
    blocks_per_tile = tq // SLC_LEN

    def sel_scores(kt, s_ref):
        kt = jnp.minimum(kt, qb)
        k0 = pl.multiple_of(kt * tq, tq)
        chunk = pl.multiple_of((kt * blocks_per_tile) // BF16_ROWS * BF16_ROWS, BF16_ROWS)
        rhs_s[ROW_MASK:ROW_MASK + BF16_ROWS, :] = mask_t[pl.ds(chunk, BF16_ROWS), :]
        rel = jnp.clip(kt - qb + 2, 0, 2)
        table = bsel_ref[0, pl.ds(pl.multiple_of(rel * tq, tq), tq), :]
        k_aug = ks_ref[pl.ds(k0, tq), :] + kx_ref[pl.ds(pl.multiple_of((kt % kx_tiles) * tq, tq), tq), :]
        s_ref[...] = _dot(k_aug, rhs_s[...]) + table

    def sel_consume(kt, s_ref):
        k0 = pl.multiple_of(kt * tq, tq)
        vt_aug = jnp.concatenate([vst_ref[0, :, pl.ds(k0, tq)], ones_rows], axis=0)
        _flash_step_t(s_ref[...], vt_aug, ms, accs)

    def win_scores(j, s_ref):
        kt = qb - 2 + j
        k0 = pl.multiple_of(jnp.maximum(kt, 0) * tq, tq)
        row = pl.multiple_of(jnp.where(kt >= 0, j, 3) * tq, tq)
        s_ref[...] = _dot(kw_ref[pl.ds(k0, tq), :], rhs_w[...]) + bwin_ref[0, pl.ds(row, tq), :]

    def win_consume(j, s_ref):
        k0 = pl.multiple_of(jnp.maximum(qb - 2 + j, 0) * tq, tq)
        vt_aug = jnp.concatenate([vwt_ref[0, :, pl.ds(k0, tq)], ones_rows], axis=0)
        _flash_step_t(s_ref[...], vt_aug, mw, accw)

    n_sel = qb + 1
    sel_scores(0, s_even)

    def pair_body(j, carry):
        sel_scores(2 * j + 1, s_odd)
        sel_consume(2 * j, s_even)
        sel_scores(2 * j + 2, s_even)
        sel_consume(2 * j + 1, s_odd)
        return carry

    lax.fori_loop(0, n_sel // 2, pair_body, 0)
    win_scores(0, s_odd)

    @pl.when(n_sel % 2 == 1)
    def _():
        sel_consume(qb, s_even)

    win_scores(1, s_even)
    win_consume(0, s_odd)
    win_scores(2, s_odd)
    win_consume(1, s_even)
    win_consume(2, s_odd)

    gt = _sigmoid(gate_ref[0, 0])
    out = gt[0:1, :] * oct_ref[0, 0] + gt[1:2, :] * _flash_result_t(accs) + gt[2:3, :] * _flash_result_t(accw)
    for h in range(NSA_HPG):
        o_ref[0, h] = out[:, h * tq:(h + 1) * tq].astype(o_ref.dtype)


def _nsa_attention(k_tok, ks_block0, kw_block0, k_extra, ch, vs_block0, vw_block0, selt, far3, b_sel, b_win,
                   oct_, gates_t):
    g, nq, hd, lanes = oct_.shape
    s = k_tok.shape[0]
    nslc = selt.shape[1]
    tq = ATT_TQ
    resident = pl.Buffered(1)
    per_q = lambda gi, qi: (gi, qi, 0, 0)
    per_g = lambda gi, qi: (gi, 0, 0)
    return pl.pallas_call(
        _nsa_att_kernel,
        grid=(g, nq),
        in_specs=[pl.BlockSpec((NSA_HPG, hd, tq), lambda gi, qi: (gi, 0, qi)),
                  pl.BlockSpec((s, AUG_K), lambda gi, qi: (0, ks_block0 + gi), pipeline_mode=resident),
                  pl.BlockSpec(k_extra.shape, lambda gi, qi: (0, 0), pipeline_mode=resident),
                  pl.BlockSpec((1, hd, s), lambda gi, qi: (vs_block0 + gi, 0, 0), pipeline_mode=resident),
                  pl.BlockSpec((s, AUG_K), lambda gi, qi: (0, kw_block0 + gi), pipeline_mode=resident),
                  pl.BlockSpec((1, hd, s), lambda gi, qi: (vw_block0 + gi, 0, 0), pipeline_mode=resident),
                  pl.BlockSpec((1, nslc, tq), lambda gi, qi: (gi, 0, qi)),
                  pl.BlockSpec((1, BF16_ROWS, lanes), per_g),
                  pl.BlockSpec((1, 3 * tq, lanes), per_g, pipeline_mode=resident),
                  pl.BlockSpec((1, 4 * tq, lanes), per_g, pipeline_mode=resident),
                  pl.BlockSpec((1, 1, hd, lanes), per_q),
                  pl.BlockSpec((1, 1, 8, lanes), per_q)],
        out_specs=pl.BlockSpec((1, lanes // tq, hd, tq), lambda gi, qi: (gi, 0, 0, qi)),
        out_shape=jax.ShapeDtypeStruct((g, lanes // tq, hd, s), BF16),
        scratch_shapes=[pltpu.VMEM((AUG_K, lanes), BF16), pltpu.VMEM((AUG_K, lanes), BF16),
                        pltpu.VMEM((nslc, lanes), BF16),
                        pltpu.VMEM((1, lanes), F32), pltpu.VMEM((AUG_V, lanes), F32),
                        pltpu.VMEM((1, lanes), F32), pltpu.VMEM((AUG_V, lanes), F32),
                        pltpu.VMEM((tq, lanes), F32), pltpu.VMEM((tq, lanes), F32)],
        compiler_params=_cparams(("arbitrary", "arbitrary")),
        name="nsa_select_window",
    )(ch, k_tok, k_extra, ch, k_tok, ch, selt, far3, b_sel, b_win, oct_, gates_t)


def _decay_kernel(z_ref, b_ref, place_ref, o_ref, end_ref, carry_ref, *, tb):
    @pl.when(pl.program_id(0) == 0)
    def _():
        carry_ref[...] = jnp.zeros(carry_ref.shape, F32)

    z = z_ref[...] + b_ref[...]
    log_f = jnp.minimum(z, 0.0) - jnp.log1p(jnp.exp(-jnp.abs(z)))
    r = lax.broadcasted_iota(jnp.int32, (tb, tb), 0)
    c = lax.broadcasted_iota(jnp.int32, (tb, tb), 1)
    tri = jnp.where(r >= c, 1.0, 0.0).astype(BF16)
    run = _dot3_rhs(tri, log_f) + carry_ref[...]
    carry_ref[...] = run[tb - 1:tb, :]
    val = -run * LOG2E
    hi, mid, lo = _split3_exact(val)
    o_ref[...] = (_dot(hi, place_ref[0]) + _dot(mid, place_ref[1]) + _dot(lo, place_ref[2])).astype(BF16)
    end_ref[...] = jnp.broadcast_to(val[tb - 1:tb, :], end_ref.shape)


def _decay_pieces(z, bias, first_lane, n_heads):
    s, n = z.shape
    tb = FOX_TK
    place = np.zeros((3, n, n_heads * AUG_K), np.float32)
    for h in range(n_heads):
        for piece in range(3):
            place[piece, first_lane + h, h * AUG_K + HEAD_DIM + piece] = 1.0
    pieces, ends = pl.pallas_call(
        functools.partial(_decay_kernel, tb=tb),
        grid=(s // tb,),
        in_specs=[pl.BlockSpec((tb, n), lambda i: (i, 0)),
                  pl.BlockSpec((1, n), lambda i: (0, 0)),
                  pl.BlockSpec((3, n, n_heads * AUG_K), lambda i: (0, 0, 0))],
        out_specs=[pl.BlockSpec((tb, n_heads * AUG_K), lambda i: (i, 0)),
                   pl.BlockSpec((8, n), lambda i: (i, 0))],
        out_shape=[jax.ShapeDtypeStruct((s, n_heads * AUG_K), BF16),
                   jax.ShapeDtypeStruct((s // tb * 8, n), F32)],
        scratch_shapes=[pltpu.VMEM((1, n), F32)],
        compiler_params=_cparams(("arbitrary",)),
        name="decay_cumsum",
    )(z, bias, jnp.asarray(place, BF16))
    return pieces, ends[::8, first_lane:first_lane + n_heads].T


def _ones_rows(width):
    return jnp.where(lax.broadcasted_iota(jnp.int32, (BF16_ROWS, width), 0) < 8, 1.0, 0.0).astype(BF16)


def _fox_kernel(ends_ref, qt_ref, k_ref, dk_ref, vt_ref, o_ref, rhs, m_ref, acc_ref, s_even, s_odd, kmax_ref):
    tq = FOX_TQ
    tk = FOX_TK
    assert tq == 2 * tk
    qb = pl.program_id(1)
    row = lax.broadcasted_iota(jnp.int32, (AUG_K - HEAD_DIM, tq), 0)
    rhs[0:HEAD_DIM, :] = qt_ref[0]
    rhs[HEAD_DIM:AUG_K, :] = jnp.where(row < 3, 1.0, 0.0).astype(BF16)
    ones_rows = _ones_rows(tk)
    _flash_init_t(m_ref, acc_ref)

    def scores(kt, s_ref):
        k0 = pl.multiple_of(kt * tk, tk)
        k_aug = k_ref[pl.ds(k0, tk), :] + dk_ref[pl.ds(k0, tk), :]
        s_ref[...] = _dot(k_aug, rhs[...])

    def consume(kt, s_ref, diagonal):
        k0 = pl.multiple_of(kt * tk, tk)
        s = s_ref[...]
        if diagonal:
            key = k0 + lax.broadcasted_iota(jnp.int32, (tk, tq), 0)
            qry = qb * tq + lax.broadcasted_iota(jnp.int32, (tk, tq), 1)
            s = jnp.where(key <= qry, s, NEG_INF)
        vt_aug = jnp.concatenate([vt_ref[0, :, pl.ds(k0, tk)], ones_rows], axis=0)
        _flash_step_t(s, vt_aug, m_ref, acc_ref)

    @pl.when(qb == 0)
    def _():
        ones = jnp.ones((AUG_K, AUG_K), BF16)

        def norm_tile(c, best):
            k = k_ref[pl.ds(pl.multiple_of(c * tk, tk), tk), :].astype(F32)
            return jnp.maximum(best, _dot((k * k).astype(BF16), ones))

        best = lax.fori_loop(0, k_ref.shape[0] // tk, norm_tile, jnp.zeros((tk, AUG_K), F32))
        kmax_ref[0] = jnp.max(jnp.sqrt(best * FOX_BOUND_SLACK))

    scores(2 * qb, s_even)
    scores(2 * qb + 1, s_odd)
    consume(2 * qb, s_even, True)
    consume(2 * qb + 1, s_odd, True)

    q = qt_ref[0].astype(F32)
    q_norm = jnp.max(jnp.sqrt(jnp.sum(q * q, axis=0, keepdims=True) * FOX_BOUND_SLACK))
    threshold = jnp.min(m_ref[...]) - FOX_SKIP_MARGIN - q_norm * kmax_ref[0]
    head = pl.program_id(0)

    def first_needed(j, first):
        return jnp.where(ends_ref[head, 2 * j + 1] >= threshold, jnp.minimum(first, j), first)

    j0 = lax.fori_loop(0, qb, first_needed, qb)

    scores(2 * j0, s_even)

    def pair_body(j, carry):
        scores(2 * j + 1, s_odd)
        consume(2 * j, s_even, False)
        scores(2 * j + 2, s_even)
        consume(2 * j + 1, s_odd, False)
        return carry

    lax.fori_loop(j0, qb, pair_body, 0)
    o_ref[0] = _flash_result_t(acc_ref).astype(o_ref.dtype)


def _fox_attention(ch, q_block0, v_block0, k_tok, k_block0, decay_k, decay_ends):
    _, hd, s = ch.shape
    h = decay_ends.shape[0]
    tq = FOX_TQ
    grid_spec = pltpu.PrefetchScalarGridSpec(
        num_scalar_prefetch=1,
        grid=(h, s // tq),
        in_specs=[pl.BlockSpec((1, hd, tq), lambda hi, qi, ends: (q_block0 + hi, 0, qi)),
                  pl.BlockSpec((s, AUG_K), lambda hi, qi, ends: (0, k_block0 + hi)),
                  pl.BlockSpec((s, AUG_K), lambda hi, qi, ends: (0, hi)),
                  pl.BlockSpec((1, hd, s), lambda hi, qi, ends: (v_block0 + hi, 0, 0))],
        out_specs=pl.BlockSpec((1, hd, tq), lambda hi, qi, ends: (hi, 0, qi)),
        scratch_shapes=[pltpu.VMEM((AUG_K, tq), BF16),
                        pltpu.VMEM((1, tq), F32), pltpu.VMEM((AUG_V, tq), F32),
                        pltpu.VMEM((FOX_TK, tq), F32), pltpu.VMEM((FOX_TK, tq), F32),
                        pltpu.SMEM((1,), F32)],
    )
    return pl.pallas_call(
        _fox_kernel,
        grid_spec=grid_spec,
        out_shape=jax.ShapeDtypeStruct((h, hd, s), BF16),
        compiler_params=_cparams(("arbitrary", "arbitrary")),
        name="fox_attention",
    )(decay_ends, ch, k_tok, decay_k, ch)


def _merge_kernel(on_ref, of_ref, mg_ref, x_ref, wn_ref, wf_ref, wo_ref, g1_ref, lg_ref, lb_ref,
                  sc2_ref, sh2_ref, wr_ref, br_ref, o_ref, u_ref, r_ref, *, alpha):
    d = x_ref.shape[-1]
    tn = (((0,), (0,)), ((), ()))
    a = lax.dot_general(on_ref[...], wn_ref[...], tn, preferred_element_type=F32)
    b = lax.dot_general(of_ref[...], wf_ref[...], tn, preferred_element_type=F32)
    gm = _sigmoid(mg_ref[...].astype(F32))
    merged = gm[:, 0:d] * a + gm[:, d:2 * d] * b
    y = _dot(merged.astype(BF16), wo_ref[...])
    z = alpha * x_ref[...] + (1.0 + g1_ref[...]) * y
    x1 = _layer_norm(z) * lg_ref[...] + lb_ref[...]
    o_ref[...] = x1
    _route_tile(x1, sc2_ref, sh2_ref, wr_ref, br_ref, u_ref, r_ref)


def _merge_project(o_nsa, o_fox, merge, x, wn, wf, wo, g1, ln_g, ln_b, sc2, sh2, w_r, b_r, alpha, tm=256):
    m, d = x.shape
    w = o_nsa.shape[0]
    resident = pl.Buffered(1)
    row = lambda i: (i, 0)
    fixed = lambda i: (0, 0)
    return pl.pallas_call(
        functools.partial(_merge_kernel, alpha=alpha),
        grid=(m // tm,),
        in_specs=[pl.BlockSpec((w, tm), lambda i: (0, i)), pl.BlockSpec((w, tm), lambda i: (0, i)),
                  pl.BlockSpec((tm, 2 * d), row), pl.BlockSpec((tm, d), row),
                  pl.BlockSpec((w, d), fixed, pipeline_mode=resident),
                  pl.BlockSpec((w, d), fixed, pipeline_mode=resident),
                  pl.BlockSpec((d, d), fixed, pipeline_mode=resident),
                  pl.BlockSpec((1, d), fixed), pl.BlockSpec((1, d), fixed), pl.BlockSpec((1, d), fixed),
                  pl.BlockSpec((1, d), fixed), pl.BlockSpec((1, d), fixed),
                  pl.BlockSpec((2, d, LANES), lambda i: (0, 0, 0)), pl.BlockSpec((1, LANES), fixed)],
        out_specs=[pl.BlockSpec((tm, d), row), pl.BlockSpec((tm, d), row), pl.BlockSpec((tm, LANES), row)],
        out_shape=[jax.ShapeDtypeStruct((m, d), F32), jax.ShapeDtypeStruct((m, d), F32),
                   jax.ShapeDtypeStruct((m, LANES), F32)],
        compiler_params=_cparams(("parallel",)),
        name="merge_project_ln_route",
    )(o_nsa, o_fox, merge, x, wn, wf, wo, g1, ln_g, ln_b, sc2, sh2, w_r, b_r)


def _route_tile(x1, sc_ref, sh_ref, w_ref, b_ref, u_ref, r_ref):
    u = _layer_norm(x1) * (1.0 + sc_ref[...]) + sh_ref[...]
    u_ref[...] = u
    u_hi = u.astype(BF16)
    u_lo = (u - u_hi.astype(F32)).astype(BF16)
    logits = _dot(u_hi, w_ref[0]) + _dot(u_lo, w_ref[0]) + _dot(u_hi, w_ref[1]) + b_ref[...]
    lane = lax.broadcasted_iota(jnp.int32, (1, LANES), 1).astype(F32)
    none = float(LANES)
    is_g = lane < N_GROUPS
    lg = jnp.where(is_g, logits, NEG_INF)
    eg = jnp.exp(lg - jnp.max(lg, axis=-1, keepdims=True))
    pg = eg / jnp.sum(eg, axis=-1, keepdims=True)
    p_grp = jnp.max(pg, axis=-1, keepdims=True)
    grp = jnp.min(jnp.where(pg == p_grp, lane, none), axis=-1, keepdims=True)
    lo = N_GROUPS + grp * EXPERTS_PER_GROUP
    is_e = (lane >= lo) & (lane < lo + EXPERTS_PER_GROUP)
    le = jnp.where(is_e, logits, NEG_INF)
    ee = jnp.exp(le - jnp.max(le, axis=-1, keepdims=True))
    pe = jnp.where(is_e, ee / jnp.sum(ee, axis=-1, keepdims=True), -1.0)
    p1 = jnp.max(pe, axis=-1, keepdims=True)
    i1 = jnp.min(jnp.where(pe == p1, lane, none), axis=-1, keepdims=True)
    pe2 = jnp.where(lane == i1, -1.0, pe)
    p2 = jnp.max(pe2, axis=-1, keepdims=True)
    i2 = jnp.min(jnp.where(pe2 == p2, lane, none), axis=-1, keepdims=True)
    den = p1 + p2
    r_ref[...] = jnp.where(lane == 0, i1 - N_GROUPS,
                           jnp.where(lane == 1, i2 - N_GROUPS,
                                     jnp.where(lane == 2, p_grp * p1 / den,
                                               jnp.where(lane == 3, p_grp * p2 / den, 0.0))))


def _moe_kernel(be_ref, nu_ref, nx_ref, ws_ref, tok_ref, tok_next_ref, dst_ref, rw_ref, u_hbm, wg_hbm, wu_hbm, wd_hbm,
                out_hbm, xbuf, ybuf, wgf, wuf, wdf, wgb, wub, wdb, sem_in, sem_out, sem_w, *, n_dump0):
    rb = ROW_BLOCK
    i = pl.program_id(0)
    last = nu_ref[0] - 1
    slot = i % 2

    def weight_copies(e, ws):
        return [pltpu.make_async_copy(src.at[e], dst.at[ws], sem_w.at[ws])
                for src, dst in ((wg_hbm, wgf), (wu_hbm, wuf), (wd_hbm, wdf))]

    def row_in(r, tok, sl):
        return pltpu.make_async_copy(u_hbm.at[pl.ds(tok, 1), :], xbuf.at[sl, pl.ds(r, 1), :], sem_in.at[sl])

    def row_out(r, dst):
        return pltpu.make_async_copy(ybuf.at[pl.ds(r, 1), :], out_hbm.at[pl.ds(dst, 1), :], sem_out)

    @pl.when(i == 0)
    def _():
        for r in range(rb):
            row_in(r, tok_ref[0, 0, r], 0).start()
        ybuf[...] = jnp.zeros(ybuf.shape, F32)
        pltpu.make_async_copy(ybuf, out_hbm.at[pl.ds(n_dump0, rb), :], sem_out).start()
        for cp in weight_copies(be_ref[0], ws_ref[0]):
            cp.start()

    prev = be_ref[jnp.maximum(i - 1, 0)]

    @pl.when((i <= last) & ((i == 0) | (be_ref[i] != prev)))
    def _():
        ws = ws_ref[i]
        for cp in weight_copies(be_ref[i], ws):
            cp.wait()
        wgb[...] = wgf[ws].astype(BF16)
        wub[...] = wuf[ws].astype(BF16)
        wdb[...] = wdf[ws].astype(BF16)

        @pl.when(nx_ref[i] >= 0)
        def _():
            for cp in weight_copies(nx_ref[i], 1 - ws):
                cp.start()

    @pl.when(i <= last)
    def _():
        for r in range(rb):
            row_in(r, 0, slot).wait()
        xb = xbuf[slot].astype(BF16)
        for r in range(rb):
            row_in(r, tok_next_ref[0, 0, r], 1 - slot).start()
        gate = _dot(xb, wgb[...])
        up = _dot(xb, wub[...])
        hid = (gate * _sigmoid(gate)) * up
        y = _dot(hid.astype(BF16), wdb[...]) * rw_ref[0]
        for r in range(rb):
            row_out(r, 0).wait()
        ybuf[...] = y
        for r in range(rb):
            row_out(r, dst_ref[0, 0, r]).start(priority=r % 2)

    @pl.when(i == last)
    def _():
        for r in range(rb):
            row_out(r, 0).wait()
        for r in range(rb):
            row_in(r, 0, 1 - slot).wait()


def _moe_experts(u, blk_exp, n_used, next_exp, w_slot, row_tok, row_dst, row_w, w_gate, w_up, w_down):
    t, d = u.shape
    n_blocks = blk_exp.shape[0]
    de = w_gate.shape[-1]
    rb = ROW_BLOCK
    tok3 = row_tok.reshape(n_blocks, 1, rb)
    grid_spec = pltpu.PrefetchScalarGridSpec(
        num_scalar_prefetch=4,
        grid=(n_blocks,),
        in_specs=[pl.BlockSpec((1, 1, rb), lambda i, *_: (i, 0, 0), memory_space=pltpu.SMEM),
                  pl.BlockSpec((1, 1, rb), lambda i, *_: (jnp.minimum(i + 1, n_blocks - 1), 0, 0),
                               memory_space=pltpu.SMEM),
                  pl.BlockSpec((1, 1, rb), lambda i, *_: (i, 0, 0), memory_space=pltpu.SMEM),
                  pl.BlockSpec((1, rb, 1), lambda i, *_: (i, 0, 0)),
                  pl.BlockSpec(memory_space=pl.ANY),
                  pl.BlockSpec(memory_space=pl.ANY), pl.BlockSpec(memory_space=pl.ANY),
                  pl.BlockSpec(memory_space=pl.ANY)],
        out_specs=pl.BlockSpec(memory_space=pl.ANY),
        scratch_shapes=[pltpu.VMEM((2, rb, d), F32), pltpu.VMEM((rb, d), F32),
                        pltpu.VMEM((2, d, de), F32), pltpu.VMEM((2, d, de), F32), pltpu.VMEM((2, de, d), F32),
                        pltpu.VMEM((d, de), BF16), pltpu.VMEM((d, de), BF16), pltpu.VMEM((de, d), BF16),
                        pltpu.SemaphoreType.DMA((2,)), pltpu.SemaphoreType.DMA(()), pltpu.SemaphoreType.DMA((2,))],
    )
    return pl.pallas_call(
        functools.partial(_moe_kernel, n_dump0=2 * t),
        grid_spec=grid_spec,
        out_shape=jax.ShapeDtypeStruct((2 * t + rb, d), F32),
        compiler_params=_cparams(("arbitrary",)),
        name="moe_experts",
    )(blk_exp, n_used, next_exp, w_slot, tok3, tok3, row_dst.reshape(n_blocks, 1, rb),
      row_w.reshape(n_blocks, rb, 1), u, w_gate, w_up, w_down)


def _moe_dispatch(route, t):
    k = 2
    eid = route[:, 0:k].astype(jnp.int32).reshape(-1)
    wts = route[:, k:2 * k].reshape(-1)
    n_asg = t * k
    n_rows = n_asg + N_EXPERTS * ROW_BLOCK
    n_blocks = n_rows // ROW_BLOCK
    onehot = (eid[:, None] == jnp.arange(N_EXPERTS, dtype=jnp.int32)[None, :]).astype(jnp.int32)
    rank = jnp.sum((jnp.cumsum(onehot, axis=0) - onehot) * onehot, axis=1)
    counts = jnp.sum(onehot, axis=0)
    padded = (counts + ROW_BLOCK - 1) // ROW_BLOCK * ROW_BLOCK
    pad_end = jnp.cumsum(padded)
    pad_start = pad_end - padded
    dest = jnp.sum(onehot * pad_start[None, :], axis=1) + rank
    asg = jnp.arange(n_asg, dtype=jnp.int32)
    upd = jnp.stack([(asg % k) * t + asg // k, lax.bitcast_convert_type(wts, jnp.int32)], axis=1)
    init = jnp.stack([n_asg + jnp.arange(n_rows, dtype=jnp.int32) % ROW_BLOCK,
                      jnp.zeros((n_rows,), jnp.int32)], axis=1)
    rows = init.at[dest].set(upd)
    row_dst = rows[:, 0]
    row_w = lax.bitcast_convert_type(rows[:, 1], F32)
    row_tok = jnp.where(row_dst < n_asg, row_dst % t, 0)
    blk_start = jnp.arange(n_blocks, dtype=jnp.int32) * ROW_BLOCK
    blk_exp = jnp.minimum(jnp.sum((pad_end[None, :] <= blk_start[:, None]).astype(jnp.int32), axis=1),
                          N_EXPERTS - 1)
    n_used = (pad_end[N_EXPERTS - 1:] // ROW_BLOCK).astype(jnp.int32)
    ids = jnp.arange(N_EXPERTS, dtype=jnp.int32)
    owner = jnp.where(counts > 0, ids, N_EXPERTS)
    later = jnp.concatenate([lax.cummin(owner, axis=0, reverse=True)[1:], jnp.full((1,), N_EXPERTS, jnp.int32)])
    next_of = jnp.where(later < N_EXPERTS, later, -1)
    slot_of = (jnp.cumsum((counts > 0).astype(jnp.int32)) - 1) % 2
    pick = (blk_exp[:, None] == ids[None, :]).astype(jnp.int32)
    next_exp = jnp.sum(pick * next_of[None, :], axis=1)
    w_slot = jnp.sum(pick * slot_of[None, :], axis=1)
    return blk_exp, n_used, next_exp, w_slot, row_tok, row_dst, row_w


def _final_kernel(x_ref, y0_ref, y1_ref, g2_ref, lg_ref, lb_ref, o_ref, *, alpha):
    z = alpha * x_ref[...] + (1.0 + g2_ref[...]) * (y0_ref[...] + y1_ref[...])
    o_ref[...] = _layer_norm(z) * lg_ref[...] + lb_ref[...]


def _final_ln(x1, y2, g2, ln_g, ln_b, alpha, tm=512):
    m, d = x1.shape
    nb = m // tm
    fixed = lambda i: (0, 0)
    return pl.pallas_call(
        functools.partial(_final_kernel, alpha=alpha),
        grid=(nb,),
        in_specs=[pl.BlockSpec((tm, d), lambda i: (i, 0)),
                  pl.BlockSpec((tm, d), lambda i: (i, 0)),
                  pl.BlockSpec((tm, d), lambda i: (i + nb, 0)),
                  pl.BlockSpec((1, d), fixed), pl.BlockSpec((1, d), fixed), pl.BlockSpec((1, d), fixed)],
        out_specs=pl.BlockSpec((tm, d), lambda i: (i, 0)),
        out_shape=jax.ShapeDtypeStruct((m, d), F32),
        compiler_params=_cparams(("parallel",)),
        name="final_ln",
    )(x1, y2, y2, g2, ln_g, ln_b)


def _to_lane_blocks(a, tq):
    g, hpg, s, c = a.shape
    return a.reshape(g, hpg, s // tq, tq, c).transpose(0, 2, 4, 1, 3).reshape(g, s // tq, c, hpg * tq)


def _layer(x2d, c, w_ada, b_ada, w_in, b_fgt, t5_table, cmp_pe, cmp_w1, cmp_b1, cmp_w2,
           w_br_nsa, w_br_fox, w_o, ln1_g, ln1_b, w_rg, b_rg, w_re, b_re,
           w_gate, w_up, w_down, ln2_g, ln2_b, alpha):
    s, d = x2d.shape
    hd = HEAD_DIM
    g = NSA_GROUPS
    mod = _ada_mod(c, w_ada, b_ada)
    sh1, sc1, g1, sh2, sc2, g2 = [mod[:, i * d:(i + 1) * d] for i in range(6)]

    c_q = NSA_HEADS * hd
    c_kv = 6 * g * hd
    c_gate = 3 * NSA_HEADS
    c_fox = 3 * FOX_HEADS * hd
    off_kv = c_q
    off_gate = off_kv + c_kv
    off_fox = off_gate + c_gate
    off_fgt = off_fox + c_fox
    off_merge = off_fgt + FOX_HEADS
    qscale = hd ** -0.5 * LOG2E
    nh = FOX_HEADS
    gw = g * hd

    def kv_cols(z):
        return w_in[:, off_kv + z * gw:off_kv + (z + 1) * gw]

    def lane_padded(w, heads):
        return jnp.pad(w.reshape(d, heads, hd), ((0, 0), (0, 0), (0, AUG_K - hd))).reshape(d, heads * AUG_K)

    fox_q, fox_k, fox_v = [w_in[:, off_fox + i * nh * hd:off_fox + (i + 1) * nh * hd] for i in range(3)]
    w_ch = jnp.concatenate([w_in[:, 0:off_kv] * qscale, kv_cols(3), kv_cols(5), fox_q * qscale, fox_v],
                           axis=1).astype(BF16).T
    w_tok = jnp.concatenate([kv_cols(0), kv_cols(1), lane_padded(kv_cols(2), g), lane_padded(kv_cols(4), g),
                             lane_padded(fox_k, nh)], axis=1).astype(BF16)
    n_small = c_gate + FOX_HEADS
    w_small = jnp.concatenate([w_in[:, off_gate:off_fox], w_in[:, off_fgt:off_merge],
                               jnp.zeros((d, LANES - n_small), F32)], axis=1).astype(BF16)
    w_merge = w_in[:, off_merge:].astype(BF16)

    u = _ln_mod(x2d, sc1, sh1, BF16, tm=PROJ_TM)
    ch = _matmul_nt(w_ch, u, BF16, PROJ_TM, "in_proj_channel_major")
    tok = _matmul(u, w_tok, BF16, PROJ_TM, w_tok.shape[1], "in_proj_token_major")
    small = _matmul(u, w_small, F32, PROJ_TM, LANES, "in_proj_small")
    merge = _matmul(u, w_merge, BF16, PROJ_TM, 2048, "in_proj_merge")

    ch = ch.reshape(-1, hd, s)
    ch_vs_block0 = NSA_HEADS
    ch_vw_block0 = ch_vs_block0 + g
    ch_fq_block0 = ch_vw_block0 + g
    ch_fv_block0 = ch_fq_block0 + nh
    tok_ks_block0 = 2 * gw // AUG_K
    tok_kw_block0 = tok_ks_block0 + g
    tok_fox_block0 = tok_kw_block0 + g

    kv_cmp_in = tok[:, 0:2 * gw].reshape(s, 2, g, hd).transpose(1, 2, 0, 3)
    kv_cmp = _compress(kv_cmp_in, cmp_pe, cmp_w1, cmp_b1, cmp_w2)
    nch = s // CMP_STRIDE
    ncp = nch + LANES
    nslc = s // SLC_LEN
    kv_cmp_pad = jnp.pad(kv_cmp.astype(BF16), ((0, 0), (0, 0), (CMP_PAD, ncp - nch - CMP_PAD), (0, 0)))
    b_cmp, b_sel, b_win, far3 = _nsa_bias_tables(t5_table)

    row_ix = np.arange(ncp)
    row_ok = (row_ix >= CMP_PAD) & (row_ix < nch - 1 + CMP_PAD)
    row_cols = np.zeros((ncp, 2 * AUG_K - hd), np.float32)
    row_cols[:, ROW_BIAS - hd:ROW_BIAS - hd + 3] = row_ok[:, None]
    row_cols[:, ROW_BIAS - hd + 3] = ~row_ok
    row_cols[row_ix, AUG_K - hd + row_ix // BF16_ROWS] = 1.0
    kc_aug = jnp.concatenate([kv_cmp_pad[0], jnp.broadcast_to(jnp.asarray(row_cols, BF16), (g,) + row_cols.shape)],
                             axis=-1)
    vct = kv_cmp_pad[1].transpose(0, 2, 1)
    oct_, selt = _nsa_compress_select(ch, kc_aug, vct, b_cmp, far3)

    period = BF16_ROWS * SLC_LEN
    k_extra = np.zeros((period, AUG_K), np.float32)
    k_extra[np.arange(period), ROW_MASK + np.arange(period) // SLC_LEN] = 1.0
    k_extra[:, ROW_BIAS:ROW_BIAS + 3] = 1.0
    gates = small[:, 0:c_gate].reshape(s, g, NSA_HPG, 3).transpose(1, 2, 0, 3)
    gates_t = jnp.pad(_to_lane_blocks(gates, ATT_TQ), ((0, 0), (0, 0), (0, 5), (0, 0)))
    o_nsa_t = _nsa_attention(tok, tok_ks_block0, tok_kw_block0, jnp.asarray(k_extra, BF16), ch, ch_vs_block0,
                             ch_vw_block0, selt, far3, b_sel, b_win, oct_, gates_t)
    o_nsa = o_nsa_t.reshape(MIX_W, s)

    fgt_bias = jnp.concatenate([jnp.zeros((c_gate,), F32), b_fgt, jnp.zeros((LANES - n_small,), F32)])[None, :]
    decay_k, decay_ends = _decay_pieces(small, fgt_bias, c_gate, nh)
    o_fox = _fox_attention(ch, ch_fq_block0, ch_fv_block0, tok, tok_fox_block0, decay_k, decay_ends)
    o_fox = o_fox.reshape(MIX_W, s)

    n_r = N_GROUPS + N_EXPERTS
    w_r = jnp.concatenate([w_rg, w_re.reshape(d, N_EXPERTS), jnp.zeros((d, LANES - n_r), F32)], axis=1)
    b_r = jnp.concatenate([b_rg, b_re.reshape(N_EXPERTS), jnp.zeros((LANES - n_r,), F32)])[None, :]
    w_r_hi, w_r_lo, _ = _split3_exact(w_r)
    x1, u2, route = _merge_project(o_nsa, o_fox, merge, x2d, w_br_nsa.astype(BF16), w_br_fox.astype(BF16),
                                   w_o.astype(BF16), g1, ln1_g[None, :], ln1_b[None, :], sc2, sh2,
                                   jnp.stack([w_r_hi, w_r_lo]), b_r, alpha)

    blk_exp, n_used, next_exp, w_slot, row_tok, row_dst, row_w = _moe_dispatch(route, s)
    y2 = _moe_experts(u2, blk_exp, n_used, next_exp, w_slot, row_tok, row_dst, row_w, w_gate, w_up, w_down)
    return _final_ln(x1, y2, g2, ln2_g[None, :], ln2_b[None, :], alpha)


def kernel(x, c, w_ada, b_ada, w_in, b_fgt, t5_table, cmp_pe, cmp_w1, cmp_b1, cmp_w2, w_br_nsa, w_br_fox, w_o,
           ln1_g, ln1_b, w_rg, b_rg, w_re, b_re, w_gate, w_up, w_down, ln2_g, ln2_b):
    b, s, d = x.shape
    depth = w_ada.shape[0]
    assert b == 1
    alpha = (2 * depth) ** 0.25
    h = x[0]
    for l in range(depth):
        h = _layer(h, c, w_ada[l], b_ada[l], w_in[l], b_fgt[l], t5_table, cmp_pe[l], cmp_w1[l], cmp_b1[l],
                   cmp_w2[l], w_br_nsa[l], w_br_fox[l], w_o[l], ln1_g[l], ln1_b[l], w_rg[l], b_rg[l],
                   w_re[l], b_re[l], w_gate[l], w_up[l], w_down[l], ln2_g[l], ln2_b[l], alpha)
    return h[None]
```
